```python
import math
import jax, jax.numpy as jnp
from jax import lax
import numpy as np

D_MODEL = 1024
BATCH = 8
SEQ = 4096
DEPTH = 2

ATT_HEADS = 8
ATT_KV_HEADS = 2
ATT_HEAD_DIM = 64
ATT_WIDTH = ATT_HEADS * ATT_HEAD_DIM
ATT_KV_WIDTH = ATT_KV_HEADS * ATT_HEAD_DIM
WINDOW = 128
CONV_WIDTH = 512
CONV_K = 31
DN_HEADS = 4
DN_HEAD_DIM = 128
DN_WIDTH = DN_HEADS * DN_HEAD_DIM
DN_CONV_K = 4
DN_CHUNK = 64
N_BRANCH = 3
EPS = 1e-6
NEG_INF = -1e30
IN_SIZES = (ATT_WIDTH, ATT_KV_WIDTH, ATT_KV_WIDTH, ATT_WIDTH,
            2 * CONV_WIDTH, CONV_WIDTH,
            DN_WIDTH, DN_WIDTH, DN_WIDTH, DN_HEADS, DN_HEADS, DN_WIDTH,
            N_BRANCH * D_MODEL)
D_IN = sum(IN_SIZES)

kernel_name = "hybrid_swa_conformer_gdn_gated_merge"


def rms_norm(x, g):
    xf = x.astype(jnp.float32)
    y = xf * lax.rsqrt(jnp.mean(xf * xf, axis=-1, keepdims=True) + EPS)
    return (y * g.astype(jnp.float32)).astype(x.dtype)


def layer_norm(x, g, b):
    xf = x.astype(jnp.float32)
    mu = jnp.mean(xf, axis=-1, keepdims=True)
    xc = xf - mu
    var = jnp.mean(xc * xc, axis=-1, keepdims=True)
    y = xc * lax.rsqrt(var + EPS) * g.astype(jnp.float32) + b.astype(jnp.float32)
    return y.astype(x.dtype)


def l2_norm(x):
    return x * lax.rsqrt(jnp.sum(x * x, axis=-1, keepdims=True) + EPS)


def causal_dwconv(x, w):
    k_width, ch = w.shape
    return lax.conv_general_dilated(
        x, w[:, None, :].astype(x.dtype), window_strides=(1,), padding=[(k_width - 1, 0)],
        dimension_numbers=("NWC", "WIO", "NWC"), feature_group_count=ch)


def alibi_slopes(n_heads):
    return jnp.exp2(-8.0 * jnp.arange(1, n_heads + 1, dtype=jnp.float32) / n_heads)


def sliding_window_attention(q, k, v, sinks):
    bsz, seq, n_h, d = q.shape
    n_kv = k.shape[2]
    grp = n_h // n_kv
    nb = seq // WINDOW
    qb = q.reshape(bsz, nb, WINDOW, n_kv, grp, d)

    def with_prev(t):
        tb = t.reshape(bsz, nb, WINDOW, n_kv, d)
        prev = jnp.concatenate([jnp.zeros_like(tb[:, :1]), tb[:, :-1]], axis=1)
        return jnp.concatenate([prev, tb], axis=2)

    kc, vc = with_prev(k), with_prev(v)
    s = jnp.einsum("bnqhgd,bnkhd->bhgnqk", qb, kc,
                   preferred_element_type=jnp.float32)
    qi = jnp.arange(WINDOW)[:, None]
    kj = jnp.arange(2 * WINDOW)[None, :]
    dist = qi + WINDOW - kj
    blk = jnp.arange(nb)[:, None, None]
    valid = (dist >= 0) & (dist < WINDOW) & ((blk > 0) | (kj >= WINDOW))
    slopes = alibi_slopes(n_h).reshape(n_kv, grp)[:, :, None, None, None]
    s = s - slopes * dist.astype(jnp.float32)
    s = jnp.where(valid, s, NEG_INF)
    sink = sinks.astype(jnp.float32).reshape(n_kv, grp)[:, :, None, None, None]
    m = jnp.maximum(jnp.max(s, axis=-1, keepdims=True), sink)
    p = jnp.exp(s - m)
    denom = jnp.sum(p, axis=-1, keepdims=True) + jnp.exp(sink - m)
    p = (p / denom).astype(v.dtype)
    o = jnp.einsum("bhgnqk,bnkhd->bnqhgd", p, vc)
    return o.reshape(bsz, seq, n_h * d)


def gated_delta_rule(q, k, v, g, beta):
    bsz, seq, n_h, dk = q.shape
    dv = v.shape[-1]
    cs = DN_CHUNK
    nc = seq // cs

    def chunks(t):
        return jnp.moveaxis(t.reshape(bsz, nc, cs, n_h, *t.shape[3:]), 3, 1)

    q, k, v, g, beta = chunks(q), chunks(k), chunks(v), chunks(g), chunks(beta)
    gc = jnp.cumsum(g, axis=-1)
    kb = k * beta[..., None]
    vb = v * beta[..., None]
    idx = jnp.arange(cs)
    lower = idx[:, None] >= idx[None, :]
    strict = idx[:, None] > idx[None, :]
    diff = gc[..., :, None] - gc[..., None, :]
    decay = jnp.where(lower, jnp.exp(jnp.where(lower, diff, 0.0)), 0.0)
    a = jnp.where(strict, jnp.einsum("bhncd,bhnsd->bhncs", kb, k) * decay, 0.0)
    eye = jnp.eye(cs, dtype=a.dtype)
    tmat = lax.linalg.triangular_solve(eye + a, jnp.broadcast_to(eye, a.shape),
                                       left_side=True, lower=True)
    u = tmat @ vb
    w = tmat @ (kb * jnp.exp(gc)[..., None])
    intra = jnp.where(lower, jnp.einsum("bhncd,bhnsd->bhncs", q, k) * decay, 0.0)
    qe = q * jnp.exp(gc)[..., None]
    g_last = gc[..., -1]
    ke = k * jnp.exp(g_last[..., None] - gc)[..., None]

    def step(state, inp):
        u_c, w_c, qe_c, ke_c, intra_c, gl = inp
        v_new = u_c - w_c @ state
        o_c = qe_c @ state + intra_c @ v_new
        state = state * jnp.exp(gl)[..., None, None] + jnp.swapaxes(ke_c, -1, -2) @ v_new
        return state, o_c

    xs = (jnp.moveaxis(u, 2, 0), jnp.moveaxis(w, 2, 0), jnp.moveaxis(qe, 2, 0),
          jnp.moveaxis(ke, 2, 0), jnp.moveaxis(intra, 2, 0), jnp.moveaxis(g_last, 2, 0))
    s0 = jnp.zeros((bsz, n_h, dk, dv), jnp.float32)
    _, o = lax.scan(step, s0, xs)
    return jnp.transpose(o, (1, 0, 3, 2, 4)).reshape(bsz, seq, n_h, dv)


def hybrid_layer(x, c, w_ada, b_ada, norm_g, w_in, q_norm_g, k_norm_g, sinks,
                 dw_w, dw_b, ln_g, ln_b, pw2_w, pw2_b, sconv_w, a_log, dt_bias, dn_norm_g,
                 w_proj_a, w_proj_b, w_proj_c, w_out):
    bsz, seq, _ = x.shape
    mod = jax.nn.silu(c) @ w_ada + b_ada
    shift, scale, gate = jnp.split(mod, 3, axis=-1)
    h = rms_norm(x, norm_g) * (1.0 + scale[:, None, :]) + shift[:, None, :]

    proj = h @ w_in
    split_points = np.cumsum(IN_SIZES)[:-1].tolist()
    (qa, ka, va, za, glu_in, zb, qc, kc, vc, ac, bc, zc, mg) = jnp.split(proj, split_points, axis=-1)

    qa = rms_norm(qa.reshape(bsz, seq, ATT_HEADS, ATT_HEAD_DIM), q_norm_g) * (ATT_HEAD_DIM ** -0.5)
    ka = rms_norm(ka.reshape(bsz, seq, ATT_KV_HEADS, ATT_HEAD_DIM), k_norm_g)
    va = va.reshape(bsz, seq, ATT_KV_HEADS, ATT_HEAD_DIM)
    ya = sliding_window_attention(qa, ka, va, sinks) * jax.nn.silu(za)

    val, gt = jnp.split(glu_in, 2, axis=-1)
    ub = val * jax.nn.sigmoid(gt)
    ub = causal_dwconv(ub, dw_w) + dw_b
    ub = jax.nn.silu(layer_norm(ub, ln_g, ln_b))
    yb = (ub @ pw2_w + pw2_b) * jax.nn.silu(zb)

    qkv = jax.nn.silu(causal_dwconv(jnp.concatenate([qc, kc, vc], axis=-1), sconv_w))
    qc, kc, vc = jnp.split(qkv, 3, axis=-1)
    qd = l2_norm(qc.reshape(bsz, seq, DN_HEADS, DN_HEAD_DIM).astype(jnp.float32)) * (DN_HEAD_DIM ** -0.5)
    kd = l2_norm(kc.reshape(bsz, seq, DN_HEADS, DN_HEAD_DIM).astype(jnp.float32))
    vd = vc.reshape(bsz, seq, DN_HEADS, DN_HEAD_DIM).astype(jnp.float32)
    beta = jax.nn.sigmoid(bc.astype(jnp.float32))
    g = -jnp.exp(a_log.astype(jnp.float32)) * jax.nn.softplus(
        ac.astype(jnp.float32) + dt_bias.astype(jnp.float32))
    od = gated_delta_rule(qd, kd, vd, g, beta)
    od = rms_norm(od, dn_norm_g).astype(x.dtype).reshape(bsz, seq, DN_WIDTH)
    yc = od * jax.nn.silu(zc)

    ga, gb, gcb = jnp.split(jax.nn.sigmoid(mg), N_BRANCH, axis=-1)
    merged = ga * (ya @ w_proj_a) + gb * (yb @ w_proj_b) + gcb * (yc @ w_proj_c)
    return x + gate[:, None, :] * (merged @ w_out)


def _fwd_setup_inputs(seed: int = 0) -> dict:
    key = jax.random.key(seed)
    ks = jax.random.split(key, 24)
    L, D = DEPTH, D_MODEL
    f32 = jnp.float32

    def nrm(k, shape, scale):
        return jax.random.normal(k, shape, f32) * scale

    dt = jnp.exp(jax.random.uniform(ks[17], (L, DN_HEADS), f32,
                                    minval=math.log(1e-3), maxval=math.log(1e-1)))
    return {
        "x": nrm(ks[0], (BATCH, SEQ, D), 1.0),
        "c": nrm(ks[1], (BATCH, D), 1.0),
        "w_ada": nrm(ks[2], (L, D, 3 * D), 0.5 * D ** -0.5),
        "b_ada": nrm(ks[3], (L, 3 * D), 0.02),
        "norm_g": 1.0 + nrm(ks[4], (L, D), 0.02),
        "w_in": nrm(ks[5], (L, D, D_IN), D ** -0.5),
        "q_norm_g": 1.0 + nrm(ks[6], (L, ATT_HEAD_DIM), 0.02),
        "k_norm_g": 1.0 + nrm(ks[7], (L, ATT_HEAD_DIM), 0.02),
        "sinks": nrm(ks[8], (L, ATT_HEADS), 0.5),
        "dw_w": nrm(ks[9], (L, CONV_K, CONV_WIDTH), CONV_K ** -0.5),
        "dw_b": nrm(ks[10], (L, CONV_WIDTH), 0.02),
        "ln_g": 1.0 + nrm(ks[11], (L, CONV_WIDTH), 0.02),
        "ln_b": nrm(ks[12], (L, CONV_WIDTH), 0.02),
        "pw2_w": nrm(ks[13], (L, CONV_WIDTH, CONV_WIDTH), CONV_WIDTH ** -0.5),
        "pw2_b": nrm(ks[14], (L, CONV_WIDTH), 0.02),
        "sconv_w": nrm(ks[15], (L, DN_CONV_K, 3 * DN_WIDTH), DN_CONV_K ** -0.5),
        "a_log": jnp.log(jax.random.uniform(ks[16], (L, DN_HEADS), f32, minval=1.0, maxval=16.0)),
        "dt_bias": dt + jnp.log(-jnp.expm1(-dt)),
        "dn_norm_g": 1.0 + nrm(ks[18], (L, DN_HEAD_DIM), 0.02),
        "w_proj_a": nrm(ks[19], (L, ATT_WIDTH, D), ATT_WIDTH ** -0.5),
        "w_proj_b": nrm(ks[20], (L, CONV_WIDTH, D), CONV_WIDTH ** -0.5),
        "w_proj_c": nrm(ks[21], (L, DN_WIDTH, D), DN_WIDTH ** -0.5),
        "w_out": nrm(ks[22], (L, D, D), D ** -0.5),
    }


def _fwd_reference(x, c, w_ada, b_ada, norm_g, w_in, q_norm_g, k_norm_g, sinks,
              dw_w, dw_b, ln_g, ln_b, pw2_w, pw2_b, sconv_w, a_log, dt_bias, dn_norm_g,
              w_proj_a, w_proj_b, w_proj_c, w_out):
    for l in range(DEPTH):
        x = hybrid_layer(x, c, w_ada[l], b_ada[l], norm_g[l], w_in[l], q_norm_g[l], k_norm_g[l],
                         sinks[l], dw_w[l], dw_b[l], ln_g[l], ln_b[l], pw2_w[l], pw2_b[l],
                         sconv_w[l], a_log[l], dt_bias[l], dn_norm_g[l],
                         w_proj_a[l], w_proj_b[l], w_proj_c[l], w_out[l])
    return x


import jax as _jax
import jax.numpy as _jnp

TWIN_FORMAT = 'train_step'
FWD_PARAMS = ['x', 'c', 'w_ada', 'b_ada', 'norm_g', 'w_in', 'q_norm_g', 'k_norm_g', 'sinks', 'dw_w', 'dw_b', 'ln_g', 'ln_b', 'pw2_w', 'pw2_b', 'sconv_w', 'a_log', 'dt_bias', 'dn_norm_g', 'w_proj_a', 'w_proj_b', 'w_proj_c', 'w_out']
TWIN_WEIGHTS = ['w_ada', 'b_ada', 'norm_g', 'w_in', 'q_norm_g', 'k_norm_g', 'sinks', 'dw_w', 'dw_b', 'ln_g', 'ln_b', 'pw2_w', 'pw2_b', 'sconv_w', 'a_log', 'dt_bias', 'dn_norm_g', 'w_proj_a', 'w_proj_b', 'w_proj_c', 'w_out']
TWIN_DIFF_INPUT = 'x'
TWIN_INPUTS = ['x', 'c', 'w_ada', 'b_ada', 'norm_g', 'w_in', 'q_norm_g', 'k_norm_g', 'sinks', 'dw_w', 'dw_b', 'ln_g', 'ln_b', 'pw2_w', 'pw2_b', 'sconv_w', 'a_log', 'dt_bias', 'dn_norm_g', 'w_proj_a', 'w_proj_b', 'w_proj_c', 'w_out', 'loss_target', 'm_w_ada', 'm_b_ada', 'm_norm_g', 'm_w_in', 'm_q_norm_g', 'm_k_norm_g', 'm_sinks', 'm_dw_w', 'm_dw_b', 'm_ln_g', 'm_ln_b', 'm_pw2_w', 'm_pw2_b', 'm_sconv_w', 'm_a_log', 'm_dt_bias', 'm_dn_norm_g', 'm_w_proj_a', 'm_w_proj_b', 'm_w_proj_c', 'm_w_out', 'v_w_ada', 'v_b_ada', 'v_norm_g', 'v_w_in', 'v_q_norm_g', 'v_k_norm_g', 'v_sinks', 'v_dw_w', 'v_dw_b', 'v_ln_g', 'v_ln_b', 'v_pw2_w', 'v_pw2_b', 'v_sconv_w', 'v_a_log', 'v_dt_bias', 'v_dn_norm_g', 'v_w_proj_a', 'v_w_proj_b', 'v_w_proj_c', 'v_w_out']
TWIN_OUTPUTS = ['loss', 'grad_x', 'grad_w_ada', 'grad_b_ada', 'grad_norm_g', 'grad_w_in', 'grad_q_norm_g', 'grad_k_norm_g', 'grad_sinks', 'grad_dw_w', 'grad_dw_b', 'grad_ln_g', 'grad_ln_b', 'grad_pw2_w', 'grad_pw2_b', 'grad_sconv_w', 'grad_a_log', 'grad_dt_bias', 'grad_dn_norm_g', 'grad_w_proj_a', 'grad_w_proj_b', 'grad_w_proj_c', 'grad_w_out', 'delta_w_ada', 'delta_b_ada', 'delta_norm_g', 'delta_w_in', 'delta_q_norm_g', 'delta_k_norm_g', 'delta_sinks', 'delta_dw_w', 'delta_dw_b', 'delta_ln_g', 'delta_ln_b', 'delta_pw2_w', 'delta_pw2_b', 'delta_sconv_w', 'delta_a_log', 'delta_dt_bias', 'delta_dn_norm_g', 'delta_w_proj_a', 'delta_w_proj_b', 'delta_w_proj_c', 'delta_w_out', 'new_m_w_ada', 'new_m_b_ada', 'new_m_norm_g', 'new_m_w_in', 'new_m_q_norm_g', 'new_m_k_norm_g', 'new_m_sinks', 'new_m_dw_w', 'new_m_dw_b', 'new_m_ln_g', 'new_m_ln_b', 'new_m_pw2_w', 'new_m_pw2_b', 'new_m_sconv_w', 'new_m_a_log', 'new_m_dt_bias', 'new_m_dn_norm_g', 'new_m_w_proj_a', 'new_m_w_proj_b', 'new_m_w_proj_c', 'new_m_w_out', 'new_v_w_ada', 'new_v_b_ada', 'new_v_norm_g', 'new_v_w_in', 'new_v_q_norm_g', 'new_v_k_norm_g', 'new_v_sinks', 'new_v_dw_w', 'new_v_dw_b', 'new_v_ln_g', 'new_v_ln_b', 'new_v_pw2_w', 'new_v_pw2_b', 'new_v_sconv_w', 'new_v_a_log', 'new_v_dt_bias', 'new_v_dn_norm_g', 'new_v_w_proj_a', 'new_v_w_proj_b', 'new_v_w_proj_c', 'new_v_w_out']
TWIN_LEAF_KINDS = {'loss': 'loss', 'grad_x': 'grad_x', 'grad_w_ada': 'grad_w', 'grad_b_ada': 'grad_w', 'grad_norm_g': 'grad_w', 'grad_w_in': 'grad_w', 'grad_q_norm_g': 'grad_w', 'grad_k_norm_g': 'grad_w', 'grad_sinks': 'grad_w', 'grad_dw_w': 'grad_w', 'grad_dw_b': 'grad_w', 'grad_ln_g': 'grad_w', 'grad_ln_b': 'grad_w', 'grad_pw2_w': 'grad_w', 'grad_pw2_b': 'grad_w', 'grad_sconv_w': 'grad_w', 'grad_a_log': 'grad_w', 'grad_dt_bias': 'grad_w', 'grad_dn_norm_g': 'grad_w', 'grad_w_proj_a': 'grad_w', 'grad_w_proj_b': 'grad_w', 'grad_w_proj_c': 'grad_w', 'grad_w_out': 'grad_w', 'delta_w_ada': 'delta_w', 'delta_b_ada': 'delta_w', 'delta_norm_g': 'delta_w', 'delta_w_in': 'delta_w', 'delta_q_norm_g': 'delta_w', 'delta_k_norm_g': 'delta_w', 'delta_sinks': 'delta_w', 'delta_dw_w': 'delta_w', 'delta_dw_b': 'delta_w', 'delta_ln_g': 'delta_w', 'delta_ln_b': 'delta_w', 'delta_pw2_w': 'delta_w', 'delta_pw2_b': 'delta_w', 'delta_sconv_w': 'delta_w', 'delta_a_log': 'delta_w', 'delta_dt_bias': 'delta_w', 'delta_dn_norm_g': 'delta_w', 'delta_w_proj_a': 'delta_w', 'delta_w_proj_b': 'delta_w', 'delta_w_proj_c': 'delta_w', 'delta_w_out': 'delta_w', 'new_m_w_ada': 'new_m', 'new_m_b_ada': 'new_m', 'new_m_norm_g': 'new_m', 'new_m_w_in': 'new_m', 'new_m_q_norm_g': 'new_m', 'new_m_k_norm_g': 'new_m', 'new_m_sinks': 'new_m', 'new_m_dw_w': 'new_m', 'new_m_dw_b': 'new_m', 'new_m_ln_g': 'new_m', 'new_m_ln_b': 'new_m', 'new_m_pw2_w': 'new_m', 'new_m_pw2_b': 'new_m', 'new_m_sconv_w': 'new_m', 'new_m_a_log': 'new_m', 'new_m_dt_bias': 'new_m', 'new_m_dn_norm_g': 'new_m', 'new_m_w_proj_a': 'new_m', 'new_m_w_proj_b': 'new_m', 'new_m_w_proj_c': 'new_m', 'new_m_w_out': 'new_m', 'new_v_w_ada': 'new_v', 'new_v_b_ada': 'new_v', 'new_v_norm_g': 'new_v', 'new_v_w_in': 'new_v', 'new_v_q_norm_g': 'new_v', 'new_v_k_norm_g': 'new_v', 'new_v_sinks': 'new_v', 'new_v_dw_w': 'new_v', 'new_v_dw_b': 'new_v', 'new_v_ln_g': 'new_v', 'new_v_ln_b': 'new_v', 'new_v_pw2_w': 'new_v', 'new_v_pw2_b': 'new_v', 'new_v_sconv_w': 'new_v', 'new_v_a_log': 'new_v', 'new_v_dt_bias': 'new_v', 'new_v_dn_norm_g': 'new_v', 'new_v_w_proj_a': 'new_v', 'new_v_w_proj_b': 'new_v', 'new_v_w_proj_c': 'new_v', 'new_v_w_out': 'new_v'}


def _forward(args):
    return _fwd_reference(*[args[k] for k in FWD_PARAMS])


def _output_shape():
    def fwd():
        inp = _fwd_setup_inputs(0)
        return _fwd_reference(*[inp[k] for k in FWD_PARAMS])
    out = _jax.eval_shape(fwd)
    return out.shape, out.dtype

N_MICROBATCH = 1
ADAM_LR = 0.001
ADAM_B1 = 0.9
ADAM_B2 = 0.999
ADAM_EPS = 1e-08
ADAM_WD = 0.01
ADAM_STEP = 10
PER_EXAMPLE_BATCH_AXIS = {'x': 0, 'c': 0, 'loss_target': 0}
SHARED_INPUTS = []
_WEIGHT_DTYPES = {'w_ada': _jnp.float32, 'b_ada': _jnp.float32, 'norm_g': _jnp.float32, 'w_in': _jnp.float32, 'q_norm_g': _jnp.float32, 'k_norm_g': _jnp.float32, 'sinks': _jnp.float32, 'dw_w': _jnp.float32, 'dw_b': _jnp.float32, 'ln_g': _jnp.float32, 'ln_b': _jnp.float32, 'pw2_w': _jnp.float32, 'pw2_b': _jnp.float32, 'sconv_w': _jnp.float32, 'a_log': _jnp.float32, 'dt_bias': _jnp.float32, 'dn_norm_g': _jnp.float32, 'w_proj_a': _jnp.float32, 'w_proj_b': _jnp.float32, 'w_proj_c': _jnp.float32, 'w_out': _jnp.float32}
MOMENT_SCALE = {'w_ada': 2.813050e-01, 'b_ada': 7.213295e-01, 'norm_g': 7.259176e-01, 'w_in': 4.012176e-02, 'q_norm_g': 2.435956e-01, 'k_norm_g': 2.438772e-01, 'sinks': 9.043210e-01, 'dw_w': 3.078827e-02, 'dw_b': 1.593267e-01, 'ln_g': 3.476302e-01, 'ln_b': 2.234044e-01, 'pw2_w': 4.083885e-02, 'pw2_b': 1.678278e-01, 'sconv_w': 6.795045e-02, 'a_log': 2.058319e+00, 'dt_bias': 1.963933e+00, 'dn_norm_g': 3.147867e+00, 'w_proj_a': 1.598808e-02, 'w_proj_b': 2.094029e-02, 'w_proj_c': 4.782177e-02, 'w_out': 5.033040e-02}


def _to_microbatches(a, axis):
    t = _jnp.moveaxis(a, axis, 0)
    t = t.reshape((N_MICROBATCH, t.shape[0] // N_MICROBATCH) + t.shape[1:])
    return _jnp.moveaxis(t, 1, axis + 1)


def setup_inputs(seed: int = 0) -> dict:
    inp = _fwd_setup_inputs(seed)
    key = _jax.random.fold_in(_jax.random.key(seed), 7919)
    shape, _ = _output_shape()
    out = dict(inp)
    out["loss_target"] = _jax.random.normal(_jax.random.fold_in(key, 0), shape, _jnp.float32)
    for i, name in enumerate(TWIN_WEIGHTS):
        w = inp[name].astype(_jnp.float32)
        if MOMENT_SCALE is None:
            s = _jnp.sqrt(_jnp.mean(_jnp.square(w)) + 1e-30)
        else:
            s = MOMENT_SCALE[name]
        km, kv = _jax.random.split(_jax.random.fold_in(key, i + 1))
        out[name] = w
        out["m_" + name] = s * _jax.random.normal(km, w.shape, _jnp.float32)
        out["v_" + name] = (s * s) * _jax.random.uniform(kv, w.shape, _jnp.float32, 0.5, 1.5)
    if N_MICROBATCH > 1:
        for name, axis in PER_EXAMPLE_BATCH_AXIS.items():
            out[name] = _to_microbatches(out[name], axis)
    return {'x': out['x'], 'c': out['c'], 'w_ada': out['w_ada'], 'b_ada': out['b_ada'], 'norm_g': out['norm_g'], 'w_in': out['w_in'], 'q_norm_g': out['q_norm_g'], 'k_norm_g': out['k_norm_g'], 'sinks': out['sinks'], 'dw_w': out['dw_w'], 'dw_b': out['dw_b'], 'ln_g': out['ln_g'], 'ln_b': out['ln_b'], 'pw2_w': out['pw2_w'], 'pw2_b': out['pw2_b'], 'sconv_w': out['sconv_w'], 'a_log': out['a_log'], 'dt_bias': out['dt_bias'], 'dn_norm_g': out['dn_norm_g'], 'w_proj_a': out['w_proj_a'], 'w_proj_b': out['w_proj_b'], 'w_proj_c': out['w_proj_c'], 'w_out': out['w_out'], 'loss_target': out['loss_target'], 'm_w_ada': out['m_w_ada'], 'm_b_ada': out['m_b_ada'], 'm_norm_g': out['m_norm_g'], 'm_w_in': out['m_w_in'], 'm_q_norm_g': out['m_q_norm_g'], 'm_k_norm_g': out['m_k_norm_g'], 'm_sinks': out['m_sinks'], 'm_dw_w': out['m_dw_w'], 'm_dw_b': out['m_dw_b'], 'm_ln_g': out['m_ln_g'], 'm_ln_b': out['m_ln_b'], 'm_pw2_w': out['m_pw2_w'], 'm_pw2_b': out['m_pw2_b'], 'm_sconv_w': out['m_sconv_w'], 'm_a_log': out['m_a_log'], 'm_dt_bias': out['m_dt_bias'], 'm_dn_norm_g': out['m_dn_norm_g'], 'm_w_proj_a': out['m_w_proj_a'], 'm_w_proj_b': out['m_w_proj_b'], 'm_w_proj_c': out['m_w_proj_c'], 'm_w_out': out['m_w_out'], 'v_w_ada': out['v_w_ada'], 'v_b_ada': out['v_b_ada'], 'v_norm_g': out['v_norm_g'], 'v_w_in': out['v_w_in'], 'v_q_norm_g': out['v_q_norm_g'], 'v_k_norm_g': out['v_k_norm_g'], 'v_sinks': out['v_sinks'], 'v_dw_w': out['v_dw_w'], 'v_dw_b': out['v_dw_b'], 'v_ln_g': out['v_ln_g'], 'v_ln_b': out['v_ln_b'], 'v_pw2_w': out['v_pw2_w'], 'v_pw2_b': out['v_pw2_b'], 'v_sconv_w': out['v_sconv_w'], 'v_a_log': out['v_a_log'], 'v_dt_bias': out['v_dt_bias'], 'v_dn_norm_g': out['v_dn_norm_g'], 'v_w_proj_a': out['v_w_proj_a'], 'v_w_proj_b': out['v_w_proj_b'], 'v_w_proj_c': out['v_w_proj_c'], 'v_w_out': out['v_w_out']}


def _loss(weights, diff, rest, loss_target):
    with _jax.named_scope("forward"):
        args = {**rest, TWIN_DIFF_INPUT: diff, **{k: w.astype(_WEIGHT_DTYPES[k]) for k, w in weights.items()}}
        y = _forward(args)
    with _jax.named_scope("loss_head"):
        err = _jnp.square(y.astype(_jnp.float32) - loss_target)
        return 0.5 * _jnp.sum(_jnp.mean(err, axis=-1)) if err.ndim else 0.5 * err


def _adamw(w, g, m, v):
    m = ADAM_B1 * m + (1.0 - ADAM_B1) * g
    v = ADAM_B2 * v + (1.0 - ADAM_B2) * _jnp.square(g)
    m_hat = m / (1.0 - ADAM_B1 ** ADAM_STEP)
    v_hat = v / (1.0 - ADAM_B2 ** ADAM_STEP)
    delta = -ADAM_LR * (m_hat / (_jnp.sqrt(v_hat) + ADAM_EPS) + ADAM_WD * w)
    return delta, m, v


def reference(x, c, w_ada, b_ada, norm_g, w_in, q_norm_g, k_norm_g, sinks, dw_w, dw_b, ln_g, ln_b, pw2_w, pw2_b, sconv_w, a_log, dt_bias, dn_norm_g, w_proj_a, w_proj_b, w_proj_c, w_out, loss_target, m_w_ada, m_b_ada, m_norm_g, m_w_in, m_q_norm_g, m_k_norm_g, m_sinks, m_dw_w, m_dw_b, m_ln_g, m_ln_b, m_pw2_w, m_pw2_b, m_sconv_w, m_a_log, m_dt_bias, m_dn_norm_g, m_w_proj_a, m_w_proj_b, m_w_proj_c, m_w_out, v_w_ada, v_b_ada, v_norm_g, v_w_in, v_q_norm_g, v_k_norm_g, v_sinks, v_dw_w, v_dw_b, v_ln_g, v_ln_b, v_pw2_w, v_pw2_b, v_sconv_w, v_a_log, v_dt_bias, v_dn_norm_g, v_w_proj_a, v_w_proj_b, v_w_proj_c, v_w_out):
    given = dict(x=x, c=c, w_ada=w_ada, b_ada=b_ada, norm_g=norm_g, w_in=w_in, q_norm_g=q_norm_g, k_norm_g=k_norm_g, sinks=sinks, dw_w=dw_w, dw_b=dw_b, ln_g=ln_g, ln_b=ln_b, pw2_w=pw2_w, pw2_b=pw2_b, sconv_w=sconv_w, a_log=a_log, dt_bias=dt_bias, dn_norm_g=dn_norm_g, w_proj_a=w_proj_a, w_proj_b=w_proj_b, w_proj_c=w_proj_c, w_out=w_out, loss_target=loss_target, m_w_ada=m_w_ada, m_b_ada=m_b_ada, m_norm_g=m_norm_g, m_w_in=m_w_in, m_q_norm_g=m_q_norm_g, m_k_norm_g=m_k_norm_g, m_sinks=m_sinks, m_dw_w=m_dw_w, m_dw_b=m_dw_b, m_ln_g=m_ln_g, m_ln_b=m_ln_b, m_pw2_w=m_pw2_w, m_pw2_b=m_pw2_b, m_sconv_w=m_sconv_w, m_a_log=m_a_log, m_dt_bias=m_dt_bias, m_dn_norm_g=m_dn_norm_g, m_w_proj_a=m_w_proj_a, m_w_proj_b=m_w_proj_b, m_w_proj_c=m_w_proj_c, m_w_out=m_w_out, v_w_ada=v_w_ada, v_b_ada=v_b_ada, v_norm_g=v_norm_g, v_w_in=v_w_in, v_q_norm_g=v_q_norm_g, v_k_norm_g=v_k_norm_g, v_sinks=v_sinks, v_dw_w=v_dw_w, v_dw_b=v_dw_b, v_ln_g=v_ln_g, v_ln_b=v_ln_b, v_pw2_w=v_pw2_w, v_pw2_b=v_pw2_b, v_sconv_w=v_sconv_w, v_a_log=v_a_log, v_dt_bias=v_dt_bias, v_dn_norm_g=v_dn_norm_g, v_w_proj_a=v_w_proj_a, v_w_proj_b=v_w_proj_b, v_w_proj_c=v_w_proj_c, v_w_out=v_w_out)
    weights = {n: given[n] for n in TWIN_WEIGHTS}
    shared = {n: given[n] for n in SHARED_INPUTS}
    per_example = {n: given[n] for n in ['x', 'c']}
    grad_fn = _jax.value_and_grad(_loss, argnums=(0, 1))

    def one_microbatch(ex, loss_target):
        ex = dict(ex)
        diff = ex.pop(TWIN_DIFF_INPUT)
        return grad_fn(weights, diff, {**shared, **ex}, loss_target)

    if N_MICROBATCH == 1:
        loss, (grad_w, grad_x) = one_microbatch(per_example, given["loss_target"])
    else:
        def body(carry, xs):
            loss_sum, grad_sum = carry
            l_k, (gw_k, gx_k) = one_microbatch(xs[0], xs[1])
            with _jax.named_scope("update"):
                return (loss_sum + l_k, _jax.tree.map(_jnp.add, grad_sum, gw_k)), gx_k

        init = (_jnp.zeros((), _jnp.float32), _jax.tree.map(_jnp.zeros_like, weights))
        (loss, grad_w), grad_x = _jax.lax.scan(body, init, (per_example, given["loss_target"]))
    with _jax.named_scope("update"):
        delta_w, new_m, new_v = {}, {}, {}
        for n in TWIN_WEIGHTS:
            delta_w[n], new_m[n], new_v[n] = _adamw(weights[n], grad_w[n], given["m_" + n], given["v_" + n])
    return (loss, grad_x, *[grad_w[n] for n in TWIN_WEIGHTS], *[delta_w[n] for n in TWIN_WEIGHTS],
            *[new_m[n] for n in TWIN_WEIGHTS], *[new_v[n] for n in TWIN_WEIGHTS])
```

```python
import functools

import numpy as np
import jax
import jax.numpy as jnp
from jax import lax
from jax.experimental import pallas as pl
from jax.experimental.pallas import tpu as pltpu

F32 = jnp.float32
BF16 = jnp.bfloat16
MESH = pl.DeviceIdType.MESH

D_MODEL = 1024
DEPTH = 2
ATT_HEADS = 8
ATT_HEAD_DIM = 64
WINDOW = 128
CONV_K = 31
DN_HEADS = 4
DN_CONV_K = 4
DN_CHUNK = 64
EPS = 1e-6
NEG_INF = -1e30
N_CHIPS = 4
D_IN = 7944

ADAM_LR = 0.001
ADAM_B1 = 0.9
ADAM_B2 = 0.999
ADAM_EPS = 1e-08
ADAM_WD = 0.01
ADAM_STEP = 10

VMEM_LIMIT = 56 * 1024 * 1024

P_QA, P_ZA, P_GLU, P_ZB, P_ZC, P_MG, P_QKV, P_KA, P_VA, P_AB, P_TOTAL = (
    0, 512, 1024, 2048, 2560, 3072, 6144, 7680, 7808, 7936, 8064)
HEAD_ORDER = (0, 4, 1, 5, 2, 6, 3, 7)


def _in_pieces():
    p = [(0 + 64 * h, 64) for h in HEAD_ORDER]
    p += [(768 + 64 * h, 64) for h in HEAD_ORDER]
    for g in range(4):
        p += [(1280 + 128 * g, 128), (1792 + 128 * g, 128)]
    p += [(2304, 512), (4360, 512), (4872, 3072), (2816, 1536), (512, 128), (640, 128), (4352, 8)]
    return p


def _pad_w_in(w):
    parts = [w[:, s:s + n] for s, n in _in_pieces()]
    parts.append(jnp.zeros((w.shape[0], P_TOTAL - D_IN), w.dtype))
    return jnp.concatenate(parts, axis=1)


def _unpad_w_in(wp):
    pieces = _in_pieces()
    starts = np.cumsum([0] + [n for _, n in pieces])[:-1]
    order = sorted(range(len(pieces)), key=lambda i: pieces[i][0])
    return jnp.concatenate([wp[:, int(starts[i]):int(starts[i]) + pieces[i][1]] for i in order], axis=1)


def _perm_heads_rows(w):
    return jnp.concatenate([w[64 * h:64 * h + 64] for h in HEAD_ORDER], axis=0)


def _unperm_heads_rows(w):
    inv = [HEAD_ORDER.index(h) for h in range(8)]
    return jnp.concatenate([w[64 * s:64 * s + 64] for s in inv], axis=0)


def _dot(a, b, dims, exact):
    if exact:
        return lax.dot_general(a.astype(F32), b.astype(F32), (dims, ((), ())), precision=lax.Precision.HIGHEST,
                               preferred_element_type=F32)
    return lax.dot_general(a.astype(BF16), b.astype(BF16), (dims, ((), ())), preferred_element_type=F32)


def _make_mm(exact):
    @jax.custom_vjp
    def nn(a, b):
        return _dot(a, b, ((1,), (0,)), exact)

    @jax.custom_vjp
    def nt(a, b):
        return _dot(a, b, ((1,), (1,)), exact)

    @jax.custom_vjp
    def tn(a, b):
        return _dot(a, b, ((0,), (0,)), exact)

    nn.defvjp(lambda a, b: (nn(a, b), (a, b)),
              lambda r, g: (nt(g, r[1]).astype(r[0].dtype), tn(r[0], g).astype(r[1].dtype)))
    nt.defvjp(lambda a, b: (nt(a, b), (a, b)),
              lambda r, g: (nn(g, r[1]).astype(r[0].dtype), tn(g, r[0]).astype(r[1].dtype)))
    tn.defvjp(lambda a, b: (tn(a, b), (a, b)),
              lambda r, g: (nt(r[1], g).astype(r[0].dtype), nn(r[0], g).astype(r[1].dtype)))
    return nn, nt, tn


mm, mm_nt, mm_tn = _make_mm(False)
xmm, xmm_nt, xmm_tn = _make_mm(True)


def _sigmoid(x):
    return 1.0 / (1.0 + jnp.exp(-x))


def _silu(x):
    return x * _sigmoid(x)


def _softplus(x):
    return jnp.maximum(x, 0.0) + jnp.log(1.0 + jnp.exp(-jnp.abs(x)))


def _cparams(n_grid):
    return pltpu.CompilerParams(dimension_semantics=("arbitrary",) * n_grid, vmem_limit_bytes=VMEM_LIMIT)


def _row_spec(tm, width, colblk):
    return pl.BlockSpec((tm, width), lambda i, cb=colblk: (i, cb))


def _const_spec(shape):
    nd = len(shape)
    return pl.BlockSpec(tuple(shape), lambda i, nd=nd: (0,) * nd)


def rowwise_fwd(name, f, rows, consts, outs, tm):
    n_r, n_c = len(rows), len(consts)
    t = rows[0][0].shape[0]

    def body(*refs):
        vals = [r[...] for r in refs[:n_r + n_c]]
        res = f(*vals)
        if not isinstance(res, (tuple, list)):
            res = (res,)
        for o_ref, v in zip(refs[n_r + n_c:], res):
            o_ref[...] = v.astype(o_ref.dtype)

    return pl.pallas_call(
        body, name=name, grid=(t // tm,),
        in_specs=[_row_spec(tm, w, cb) for _, w, cb in rows] + [_const_spec(c.shape) for c in consts],
        out_specs=[_row_spec(tm, w, 0) for w, _ in outs],
        out_shape=[jax.ShapeDtypeStruct((t, w), dt) for w, dt in outs],
        compiler_params=_cparams(1),
    )(*[a for a, _, _ in rows], *consts)


def rowwise_bwd(name, f, rows, consts, cts, row_grad_dtypes, tm):
    n_r, n_c, n_ct = len(rows), len(consts), len(cts)
    t = rows[0][0].shape[0]
    keep = [k for k, dt in enumerate(row_grad_dtypes) if dt is not None]

    def body(*refs):
        ins = [r[...].astype(F32) for r in refs[:n_r + n_c]]
        g_out = [r[...].astype(F32) for r in refs[n_r + n_c:n_r + n_c + n_ct]]
        out_refs = refs[n_r + n_c + n_ct:]

        def fw(*a):
            res = f(*a)
            return tuple(res) if isinstance(res, (tuple, list)) else (res,)

        _, vjp = jax.vjp(fw, *ins)
        grads = vjp(tuple(g_out))
        for o_ref, k in zip(out_refs[:len(keep)], keep):
            o_ref[...] = grads[k].astype(o_ref.dtype)
        first = pl.program_id(0) == 0
        for o_ref, g in zip(out_refs[len(keep):], grads[n_r:]):
            @pl.when(first)
            def _(o_ref=o_ref, g=g):
                o_ref[...] = g

            @pl.when(jnp.logical_not(first))
            def _(o_ref=o_ref, g=g):
                o_ref[...] += g

    return pl.pallas_call(
        body, name=name, grid=(t // tm,),
        in_specs=[_row_spec(tm, w, cb) for _, w, cb in rows] + [_const_spec(c.shape) for c in consts]
        + [_row_spec(tm, w, cb) for _, w, cb in cts],
        out_specs=[_row_spec(tm, rows[k][1], 0) for k in keep] + [_const_spec(c.shape) for c in consts],
        out_shape=[jax.ShapeDtypeStruct((t, rows[k][1]), row_grad_dtypes[k]) for k in keep]
        + [jax.ShapeDtypeStruct(c.shape, F32) for c in consts],
        compiler_params=_cparams(1),
    )(*[a for a, _, _ in rows], *consts, *[a for a, _, _ in cts])


def f_norm_mod(x, g, scale, shift):
    y = x * lax.rsqrt(jnp.mean(x * x, axis=-1, keepdims=True) + EPS) * g
    return y * (1.0 + scale) + shift


def f_conf_tail(u, zb, ln_g, ln_b, pw2_w, pw2_b):
    mu = jnp.mean(u, axis=-1, keepdims=True)
    xc = u - mu
    var = jnp.mean(xc * xc, axis=-1, keepdims=True)
    y = _silu(xc * lax.rsqrt(var + EPS) * ln_g + ln_b)
    return (mm(y, pw2_w) + pw2_b) * _silu(zb)


def f_merge(ya, yb, yc, mg, x, gate, wpa, wpb, wpc, wout):
    d = D_MODEL
    merged = (_sigmoid(mg[:, :d]) * mm(ya, wpa) + _sigmoid(mg[:, d:2 * d]) * mm(yb, wpb)
              + _sigmoid(mg[:, 2 * d:]) * mm(yc, wpc))
    return x + gate * mm(merged, wout)


def matmul_nn(name, a, b, out_dtype, tm, tn, tk):
    m, k = a.shape
    n = b.shape[1]
    nk = k // tk

    def body(a_ref, b_ref, o_ref, acc_ref):
        kk = pl.program_id(2)
        part = jnp.dot(a_ref[...].astype(BF16), b_ref[...].astype(BF16), preferred_element_type=F32)

        @pl.when(kk == 0)
        def _():
            acc_ref[...] = part

        @pl.when(kk > 0)
        def _():
            acc_ref[...] += part

        @pl.when(kk == nk - 1)
        def _():
            o_ref[...] = acc_ref[...].astype(o_ref.dtype)

    return pl.pallas_call(
        body, name=name, grid=(m // tm, n // tn, nk),
        in_specs=[pl.BlockSpec((tm, tk), lambda i, j, kk: (i, kk)), pl.BlockSpec((tk, tn), lambda i, j, kk: (kk, j))],
        out_specs=pl.BlockSpec((tm, tn), lambda i, j, kk: (i, j)),
        out_shape=jax.ShapeDtypeStruct((m, n), out_dtype),
        scratch_shapes=[pltpu.VMEM((tm, tn), F32)],
        compiler_params=_cparams(3),
    )(a, b)


def matmul_tn(name, a, b, ta, tn, tm):
    m, k = a.shape
    n = b.shape[1]
    nm = m // tm

    def body(a_ref, b_ref, o_ref):
        mm_ = pl.program_id(2)
        part = lax.dot_general(a_ref[...].astype(BF16), b_ref[...].astype(BF16), (((0,), (0,)), ((), ())),
                               preferred_element_type=F32)

        @pl.when(mm_ == 0)
        def _():
            o_ref[...] = part

        @pl.when(mm_ > 0)
        def _():
            o_ref[...] += part

    return pl.pallas_call(
        body, name=name, grid=(k // ta, n // tn, nm),
        in_specs=[pl.BlockSpec((tm, ta), lambda i, j, r: (r, i)), pl.BlockSpec((tm, tn), lambda i, j, r: (r, j))],
        out_specs=pl.BlockSpec((ta, tn), lambda i, j, r: (i, j)),
        out_shape=jax.ShapeDtypeStruct((k, n), F32),
        compiler_params=_cparams(3),
    )(a, b)


def ada_fwd(name, c8, w_ada, b_ada):
    def body(c_ref, w_ref, b_ref, o_ref):
        o_ref[...] = mm(_silu(c_ref[...]), w_ref[...]) + b_ref[...]

    return pl.pallas_call(
        body, name=name, out_shape=jax.ShapeDtypeStruct((8, 3 * D_MODEL), F32),
        compiler_params=pltpu.CompilerParams(vmem_limit_bytes=VMEM_LIMIT),
    )(c8, w_ada, b_ada)


def ada_bwd(name, c8, dmod8):
    tn = 768

    def body(c_ref, d_ref, o_ref):
        row0 = lax.broadcasted_iota(jnp.int32, (8, 1), 0) == 0
        sc = jnp.where(row0, _silu(c_ref[...]), 0.0)
        o_ref[...] = mm_tn(sc, d_ref[...])

    return pl.pallas_call(
        body, name=name, grid=(3 * D_MODEL // tn,),
        in_specs=[pl.BlockSpec((8, D_MODEL), lambda j: (0, 0)), pl.BlockSpec((8, tn), lambda j: (0, j))],
        out_specs=pl.BlockSpec((D_MODEL, tn), lambda j: (0, j)),
        out_shape=jax.ShapeDtypeStruct((D_MODEL, 3 * D_MODEL), F32),
        compiler_params=_cparams(1),
    )(c8, dmod8)


def _f_attn(first_block, q, za, kc, vc, kp, vp, qg, kg, sinks):
    w = WINDOW
    lane = lax.broadcasted_iota(jnp.int32, (1, 128), 1)
    halves = [lane < 64, lane >= 64]

    def rms_halves(x, g):
        x2 = x * x
        s0 = jnp.sum(jnp.where(halves[0], x2, 0.0), axis=-1, keepdims=True)
        s1 = jnp.sum(jnp.where(halves[1], x2, 0.0), axis=-1, keepdims=True)
        r = jnp.where(halves[0], lax.rsqrt(s0 / 64.0 + EPS), lax.rsqrt(s1 / 64.0 + EPS))
        return x * r * g

    kcat = rms_halves(jnp.concatenate([kp, kc], axis=0), kg)
    vcat = jnp.concatenate([vp, vc], axis=0)
    qi = lax.broadcasted_iota(jnp.int32, (w, 2 * w), 0)
    kj = lax.broadcasted_iota(jnp.int32, (w, 2 * w), 1)
    dist = qi + w - kj
    valid = (dist >= 0) & (dist < w) & (jnp.logical_not(first_block) | (kj >= w))
    distf = dist.astype(F32)
    outs = []
    for grp in range(4):
        qn = rms_halves(q[:, 128 * grp:128 * grp + 128], qg) * (ATT_HEAD_DIM ** -0.5)
        o_grp = jnp.zeros((w, 128), F32)
        for half in range(2):
            head = HEAD_ORDER[2 * grp + half]
            slope = 2.0 ** (-8.0 * (head + 1) / ATT_HEADS)
            sink = jnp.sum(jnp.where(lane == head, sinks, 0.0), axis=-1, keepdims=True)
            s = mm_nt(jnp.where(halves[half], qn, 0.0), kcat) - slope * distf
            s = jnp.where(valid, s, NEG_INF)
            m = lax.stop_gradient(jnp.maximum(jnp.max(s, axis=-1, keepdims=True), sink))
            p = jnp.exp(s - m)
            denom = jnp.sum(p, axis=-1, keepdims=True) + jnp.exp(sink - m)
            o_grp = o_grp + mm(p / denom, jnp.where(halves[half], vcat, 0.0))
        outs.append(o_grp)
    return jnp.concatenate(outs, axis=1) * _silu(za)


def attn_fwd(name, proj, qg, kg, sinks):
    t = proj.shape[0]
    nb = t // WINDOW

    def body(q_ref, za_ref, kc_ref, vc_ref, kp_ref, vp_ref, qg_ref, kg_ref, s_ref, o_ref):
        first = pl.program_id(0) == 0
        o_ref[...] = _f_attn(first, q_ref[...], za_ref[...], kc_ref[...], vc_ref[...], kp_ref[...], vp_ref[...],
                             qg_ref[...], kg_ref[...], s_ref[...])

    cur = lambda cb: (lambda i: (i, cb))
    prev = lambda cb: (lambda i: (jnp.maximum(i - 1, 0), cb))
    return pl.pallas_call(
        body, name=name, grid=(nb,),
        in_specs=[pl.BlockSpec((WINDOW, 512), cur(P_QA // 512)), pl.BlockSpec((WINDOW, 512), cur(P_ZA // 512)),
                  pl.BlockSpec((WINDOW, 128), cur(P_KA // 128)), pl.BlockSpec((WINDOW, 128), cur(P_VA // 128)),
                  pl.BlockSpec((WINDOW, 128), prev(P_KA // 128)), pl.BlockSpec((WINDOW, 128), prev(P_VA // 128)),
                  _const_spec((1, 128)), _const_spec((1, 128)), _const_spec((1, 128))],
        out_specs=pl.BlockSpec((WINDOW, 512), lambda i: (i, 0)),
        out_shape=jax.ShapeDtypeStruct((t, 512), F32),
        compiler_params=_cparams(1),
    )(proj, proj, proj, proj, proj, proj, qg, kg, sinks)


def attn_bwd(name, proj, qg, kg, sinks, dya):
    t = proj.shape[0]
    nb = t // WINDOW

    def body(q_ref, za_ref, kc_ref, vc_ref, kp_ref, vp_ref, qg_ref, kg_ref, s_ref, dy_ref,
             dqz_ref, dkv_ref, dqg_ref, dkg_ref, ds_ref, carry_ref):
        j = pl.program_id(0)
        first = j == nb - 1

        @pl.when(j == 0)
        def _():
            carry_ref[...] = jnp.zeros_like(carry_ref)
            dqg_ref[...] = jnp.zeros_like(dqg_ref)
            dkg_ref[...] = jnp.zeros_like(dkg_ref)
            ds_ref[...] = jnp.zeros_like(ds_ref)

        ins = [r[...] for r in (q_ref, za_ref, kc_ref, vc_ref, kp_ref, vp_ref, qg_ref, kg_ref, s_ref)]
        _, vjp = jax.vjp(functools.partial(_f_attn, first), *ins)
        dq, dza, dkc, dvc, dkp, dvp, dqg, dkg, dsk = vjp(dy_ref[...])
        dqz_ref[:, 0:512] = dq.astype(dqz_ref.dtype)
        dqz_ref[:, 512:1024] = dza.astype(dqz_ref.dtype)
        dkv_ref[:, 0:128] = (dkc + carry_ref[0]).astype(dkv_ref.dtype)
        dkv_ref[:, 128:256] = (dvc + carry_ref[1]).astype(dkv_ref.dtype)
        carry_ref[0] = dkp
        carry_ref[1] = dvp
        dqg_ref[...] += dqg
        dkg_ref[...] += dkg
        ds_ref[...] += dsk

    cur = lambda cb: (lambda j: (nb - 1 - j, cb))
    prev = lambda cb: (lambda j: (jnp.maximum(nb - 2 - j, 0), cb))
    return pl.pallas_call(
        body, name=name, grid=(nb,),
        in_specs=[pl.BlockSpec((WINDOW, 512), cur(P_QA // 512)), pl.BlockSpec((WINDOW, 512), cur(P_ZA // 512)),
                  pl.BlockSpec((WINDOW, 128), cur(P_KA // 128)), pl.BlockSpec((WINDOW, 128), cur(P_VA // 128)),
                  pl.BlockSpec((WINDOW, 128), prev(P_KA // 128)), pl.BlockSpec((WINDOW, 128), prev(P_VA // 128)),
                  _const_spec((1, 128)), _const_spec((1, 128)), _const_spec((1, 128)),
                  pl.BlockSpec((WINDOW, 512), cur(0))],
        out_specs=[pl.BlockSpec((WINDOW, 1024), cur(0)), pl.BlockSpec((WINDOW, 256), cur(0)),
                   _const_spec((1, 128)), _const_spec((1, 128)), _const_spec((1, 128))],
        out_shape=[jax.ShapeDtypeStruct((t, 1024), BF16), jax.ShapeDtypeStruct((t, 256), BF16),
                   jax.ShapeDtypeStruct((1, 128), F32), jax.ShapeDtypeStruct((1, 128), F32),
                   jax.ShapeDtypeStruct((1, 128), F32)],
        scratch_shapes=[pltpu.VMEM((2, WINDOW, 128), F32)],
        compiler_params=_cparams(1),
    )(proj, proj, proj, proj, proj, proj, qg, kg, sinks, dya)


CONV_ROWS = 256


def _conv_taps(src_ref, w_ref, n_taps, base, t):
    for r0 in range(0, t, CONV_ROWS):
        acc = w_ref[0:1, :] * src_ref[pl.ds(r0 + base, CONV_ROWS), :]
        for k in range(1, n_taps):
            acc = acc + w_ref[k:k + 1, :] * src_ref[pl.ds(r0 + base + k, CONV_ROWS), :]
        yield r0, acc


def _conv_wgrad(dy_ref, src_ref, n_taps, base, t, dy_base=0):
    out = []
    for k in range(n_taps):
        acc = jnp.zeros((8, 128), F32)
        for r0 in range(0, t, CONV_ROWS):
            prod = dy_ref[pl.ds(r0 + dy_base, CONV_ROWS), :] * src_ref[pl.ds(r0 + base + k, CONV_ROWS), :]
            acc = acc + jnp.sum(prod.reshape(CONV_ROWS // 8, 8, 128), axis=0)
        out.append(jnp.sum(acc, axis=0, keepdims=True))
    return out


def glu_conv_fwd(name, proj, w32, bias):
    t = proj.shape[0]
    pad = 32

    def body(x_ref, w_ref, b_ref, o_ref, u_ref):
        u_ref[0:pad, :] = jnp.zeros((pad, 128), F32)
        u_ref[pad:pad + t, :] = x_ref[:, 0:128] * _sigmoid(x_ref[:, 128:256])
        for r0, acc in _conv_taps(u_ref, w_ref, CONV_K, pad - (CONV_K - 1), t):
            o_ref[pl.ds(r0, CONV_ROWS), :] = acc + b_ref[...]

    return pl.pallas_call(
        body, name=name, grid=(4,),
        in_specs=[pl.BlockSpec((t, 256), lambda cb: (0, P_GLU // 256 + cb)), pl.BlockSpec((32, 128), lambda cb: (0, cb)),
                  pl.BlockSpec((1, 128), lambda cb: (0, cb))],
        out_specs=pl.BlockSpec((t, 128), lambda cb: (0, cb)),
        out_shape=jax.ShapeDtypeStruct((t, 512), F32),
        scratch_shapes=[pltpu.VMEM((t + pad, 128), F32)],
        compiler_params=_cparams(1),
    )(proj, w32, bias)


def glu_conv_bwd(name, proj, w32, dub):
    t = proj.shape[0]
    pad = 32
    k1 = CONV_K - 1

    def body(x_ref, w_ref, dy_ref, dx_ref, dw_ref, db_ref, u_ref, dyp_ref, wrev_ref):
        val = x_ref[:, 0:128]
        sg = _sigmoid(x_ref[:, 128:256])
        u_ref[0:pad, :] = jnp.zeros((pad, 128), F32)
        u_ref[pad:pad + t, :] = val * sg
        dyp_ref[0:t, :] = dy_ref[...]
        dyp_ref[t:t + pad, :] = jnp.zeros((pad, 128), F32)
        for k in range(CONV_K):
            wrev_ref[k:k + 1, :] = w_ref[k1 - k:k1 - k + 1, :]
        wrev_ref[CONV_K:32, :] = jnp.zeros((32 - CONV_K, 128), F32)
        for r0, du in _conv_taps(dyp_ref, wrev_ref, CONV_K, 0, t):
            v = x_ref[pl.ds(r0, CONV_ROWS), 0:128]
            s = _sigmoid(x_ref[pl.ds(r0, CONV_ROWS), 128:256])
            dx_ref[pl.ds(r0, CONV_ROWS), 0:128] = (du * s).astype(dx_ref.dtype)
            dx_ref[pl.ds(r0, CONV_ROWS), 128:256] = (du * v * s * (1.0 - s)).astype(dx_ref.dtype)
        dws = _conv_wgrad(dyp_ref, u_ref, CONV_K, pad - k1, t)
        for k in range(CONV_K):
            dw_ref[k:k + 1, :] = dws[k]
        dw_ref[CONV_K:32, :] = jnp.zeros((32 - CONV_K, 128), F32)
        db_ref[...] = jnp.sum(dy_ref[...], axis=0, keepdims=True)

    return pl.pallas_call(
        body, name=name, grid=(4,),
        in_specs=[pl.BlockSpec((t, 256), lambda cb: (0, P_GLU // 256 + cb)), pl.BlockSpec((32, 128), lambda cb: (0, cb)),
                  pl.BlockSpec((t, 128), lambda cb: (0, cb))],
        out_specs=[pl.BlockSpec((t, 256), lambda cb: (0, cb)), pl.BlockSpec((32, 128), lambda cb: (0, cb)),
                   pl.BlockSpec((1, 128), lambda cb: (0, cb))],
        out_shape=[jax.ShapeDtypeStruct((t, 1024), BF16), jax.ShapeDtypeStruct((32, 512), F32),
                   jax.ShapeDtypeStruct((1, 512), F32)],
        scratch_shapes=[pltpu.VMEM((t + pad, 128), F32), pltpu.VMEM((t + pad, 128), F32), pltpu.VMEM((32, 128), F32)],
        compiler_params=_cparams(1),
    )(proj, w32, dub)


def sconv_fwd(name, proj, w8):
    t = proj.shape[0]
    pad = 8
    k1 = DN_CONV_K - 1

    def body(x_ref, w_ref, o_ref, xp_ref):
        xp_ref[0:pad, :] = jnp.zeros((pad, 128), F32)
        xp_ref[pad:pad + t, :] = x_ref[...]
        for r0, acc in _conv_taps(xp_ref, w_ref, DN_CONV_K, pad - k1, t):
            o_ref[pl.ds(r0, CONV_ROWS), :] = _silu(acc)

    return pl.pallas_call(
        body, name=name, grid=(12,),
        in_specs=[pl.BlockSpec((t, 128), lambda cb: (0, P_QKV // 128 + cb)), pl.BlockSpec((8, 128), lambda cb: (0, cb))],
        out_specs=pl.BlockSpec((t, 128), lambda cb: (0, cb)),
        out_shape=jax.ShapeDtypeStruct((t, 1536), F32),
        scratch_shapes=[pltpu.VMEM((t + pad, 128), F32)],
        compiler_params=_cparams(1),
    )(proj, w8)


def sconv_bwd(name, proj, w8, dqkv):
    t = proj.shape[0]
    pad = 8
    k1 = DN_CONV_K - 1

    def body(x_ref, w_ref, dy_ref, dx_ref, dw_ref, xp_ref, dpp_ref, wrev_ref):
        xp_ref[0:pad, :] = jnp.zeros((pad, 128), F32)
        xp_ref[pad:pad + t, :] = x_ref[...]
        for r0, pre in _conv_taps(xp_ref, w_ref, DN_CONV_K, pad - k1, t):
            s = _sigmoid(pre)
            dpp_ref[pl.ds(r0, CONV_ROWS), :] = dy_ref[pl.ds(r0, CONV_ROWS), :] * (s * (1.0 + pre * (1.0 - s)))
        dpp_ref[t:t + pad, :] = jnp.zeros((pad, 128), F32)
        for k in range(DN_CONV_K):
            wrev_ref[k:k + 1, :] = w_ref[k1 - k:k1 - k + 1, :]
        wrev_ref[DN_CONV_K:8, :] = jnp.zeros((8 - DN_CONV_K, 128), F32)
        for r0, dx in _conv_taps(dpp_ref, wrev_ref, DN_CONV_K, 0, t):
            dx_ref[pl.ds(r0, CONV_ROWS), :] = dx.astype(dx_ref.dtype)
        dws = _conv_wgrad(dpp_ref, xp_ref, DN_CONV_K, pad - k1, t)
        for k in range(DN_CONV_K):
            dw_ref[k:k + 1, :] = dws[k]
        dw_ref[DN_CONV_K:8, :] = jnp.zeros((8 - DN_CONV_K, 128), F32)

    return pl.pallas_call(
        body, name=name, grid=(12,),
        in_specs=[pl.BlockSpec((t, 128), lambda cb: (0, P_QKV // 128 + cb)), pl.BlockSpec((8, 128), lambda cb: (0, cb)),
                  pl.BlockSpec((t, 128), lambda cb: (0, cb))],
        out_specs=[pl.BlockSpec((t, 128), lambda cb: (0, cb)), pl.BlockSpec((8, 128), lambda cb: (0, cb))],
        out_shape=[jax.ShapeDtypeStruct((t, 1536), BF16), jax.ShapeDtypeStruct((8, 1536), F32)],
        scratch_shapes=[pltpu.VMEM((t + pad, 128), F32), pltpu.VMEM((t + pad, 128), F32), pltpu.VMEM((8, 128), F32)],
        compiler_params=_cparams(1),
    )(proj, w8, dqkv)


def _f_delta(qkv, ab, zc, s0, s1, s2, s3, a_log, dt_bias, dn_g):
    cs = DN_CHUNK
    n = 2 * cs
    states = (s0, s1, s2, s3)
    lane = lax.broadcasted_iota(jnp.int32, (1, 128), 1)
    ri = lax.broadcasted_iota(jnp.int32, (n, n), 0)
    ci = lax.broadcasted_iota(jnp.int32, (n, n), 1)
    same = (ri // cs) == (ci // cs)
    lower = same & (ri >= ci)
    strict = same & (ri > ci)
    lower_f = jnp.where(lower, 1.0, 0.0)
    upper_f = jnp.where(same & (ri <= ci), 1.0, 0.0)
    same_f = jnp.where(same, 1.0, 0.0)
    eye = jnp.where(ri == ci, 1.0, 0.0)
    sum_top = jnp.where(ci < cs, 1.0, 0.0)
    sum_bot = jnp.where(ci >= cs, 1.0, 0.0)
    top = lax.broadcasted_iota(jnp.int32, (n, 1), 0) < cs

    def pick(row, idx):
        return jnp.sum(jnp.where(lane == idx, row, 0.0), axis=-1, keepdims=True)

    def l2n(x):
        return x * lax.rsqrt(jnp.sum(x * x, axis=-1, keepdims=True) + EPS)

    ys, new_states = [], []
    for pair in range(2):
        hs = (2 * pair, 2 * pair + 1)
        stack = lambda f: jnp.concatenate([f(hs[0]), f(hs[1])], axis=0)
        qd = l2n(stack(lambda h: qkv[:, 128 * h:128 * h + 128])) * (128 ** -0.5)
        kd = l2n(stack(lambda h: qkv[:, 512 + 128 * h:512 + 128 * h + 128]))
        vd = stack(lambda h: qkv[:, 1024 + 128 * h:1024 + 128 * h + 128])
        beta = _sigmoid(stack(lambda h: pick(ab, 4 + h)))
        g = stack(lambda h: -jnp.exp(pick(a_log, h)) * _softplus(pick(ab, h) + pick(dt_bias, h)))
        g_b = g * jnp.ones((1, n), F32)
        gc_col = xmm(lower_f, g_b)
        gc_row = xmm_tn(g_b, upper_f)
        gl_b = xmm(same_f, g_b)
        decay = jnp.where(lower, jnp.exp(jnp.where(lower, gc_col - gc_row, 0.0)), 0.0)
        kb = kd * beta
        vb = vd * beta
        a = jnp.where(strict, mm_nt(kb, kd) * decay, 0.0)
        tmat = eye - a
        pw = a
        for _ in range(5):
            pw = xmm(pw, pw)
            tmat = tmat + xmm(tmat, pw)
        egc = jnp.exp(gc_col)
        u = mm(tmat, vb)
        wm = mm(tmat, kb * egc)
        intra = jnp.where(lower, mm_nt(qd, kd) * decay, 0.0)
        qe = qd * egc
        ke = kd * jnp.exp(gl_b - gc_col)
        st = (states[hs[0]], states[hs[1]])
        v_new = u - jnp.concatenate([mm(wm[:cs], st[0]), mm(wm[cs:], st[1])], axis=0)
        o = jnp.concatenate([mm(qe[:cs], st[0]), mm(qe[cs:], st[1])], axis=0) + mm(intra, v_new)
        g_end = [xmm(sum_top, g_b), xmm(sum_bot, g_b)]
        new_states.append(st[0] * jnp.exp(g_end[0]) + mm_tn(jnp.where(top, ke, 0.0), v_new))
        new_states.append(st[1] * jnp.exp(g_end[1]) + mm_tn(jnp.where(top, 0.0, ke), v_new))
        od = o * lax.rsqrt(jnp.mean(o * o, axis=-1, keepdims=True) + EPS) * dn_g
        ys += [od[:cs] * _silu(zc[:, 128 * hs[0]:128 * hs[0] + 128]), od[cs:] * _silu(zc[:, 128 * hs[1]:128 * hs[1] + 128])]
    y = jnp.concatenate([ys[0], ys[1], ys[2], ys[3]], axis=1)
    return (y, *new_states)


def delta_fwd(name, qkv, proj, a_log, dt_bias, dn_g):
    t = qkv.shape[0]
    nc = t // DN_CHUNK

    def body(qkv_ref, ab_ref, zc_ref, al_ref, dt_ref, g_ref, y_ref, ssave_ref, s_ref):
        @pl.when(pl.program_id(0) == 0)
        def _():
            s_ref[...] = jnp.zeros_like(s_ref)

        ssave_ref[0] = s_ref[...]
        st = [s_ref[128 * h:128 * h + 128, :] for h in range(4)]
        y, *ns = _f_delta(qkv_ref[...], ab_ref[...], zc_ref[...], *st, al_ref[...], dt_ref[...], g_ref[...])
        y_ref[...] = y
        for h in range(4):
            s_ref[128 * h:128 * h + 128, :] = ns[h]

    return pl.pallas_call(
        body, name=name, grid=(nc,),
        in_specs=[pl.BlockSpec((DN_CHUNK, 1536), lambda i: (i, 0)), pl.BlockSpec((DN_CHUNK, 128), lambda i: (i, P_AB // 128)),
                  pl.BlockSpec((DN_CHUNK, 512), lambda i: (i, P_ZC // 512)),
                  _const_spec((1, 128)), _const_spec((1, 128)), _const_spec((1, 128))],
        out_specs=[pl.BlockSpec((DN_CHUNK, 512), lambda i: (i, 0)), pl.BlockSpec((1, 512, 128), lambda i: (i, 0, 0))],
        out_shape=[jax.ShapeDtypeStruct((t, 512), F32), jax.ShapeDtypeStruct((nc, 512, 128), F32)],
        scratch_shapes=[pltpu.VMEM((512, 128), F32)],
        compiler_params=_cparams(1),
    )(qkv, proj, proj, a_log, dt_bias, dn_g)


def delta_bwd(name, qkv, proj, ssave, a_log, dt_bias, dn_g, dyc):
    t = qkv.shape[0]
    nc = t // DN_CHUNK

    def body(qkv_ref, ab_ref, zc_ref, ss_ref, al_ref, dt_ref, g_ref, dy_ref,
             dqkv_ref, dab_ref, dzc_ref, dal_ref, ddt_ref, dg_ref, ds_ref):
        @pl.when(pl.program_id(0) == 0)
        def _():
            ds_ref[...] = jnp.zeros_like(ds_ref)
            dal_ref[...] = jnp.zeros_like(dal_ref)
            ddt_ref[...] = jnp.zeros_like(ddt_ref)
            dg_ref[...] = jnp.zeros_like(dg_ref)

        st = [ss_ref[0, 128 * h:128 * h + 128, :] for h in range(4)]
        _, vjp = jax.vjp(_f_delta, qkv_ref[...], ab_ref[...], zc_ref[...], *st, al_ref[...], dt_ref[...], g_ref[...])
        dst = tuple(ds_ref[128 * h:128 * h + 128, :] for h in range(4))
        dqkv, dab, dzc, d0, d1, d2, d3, dal, ddt, dg = vjp((dy_ref[...], *dst))
        dqkv_ref[...] = dqkv
        dab_ref[...] = dab.astype(dab_ref.dtype)
        dzc_ref[...] = dzc.astype(dzc_ref.dtype)
        for h, d in enumerate((d0, d1, d2, d3)):
            ds_ref[128 * h:128 * h + 128, :] = d
        dal_ref[...] += dal
        ddt_ref[...] += ddt
        dg_ref[...] += dg

    rev = lambda cb: (lambda j: (nc - 1 - j, cb))
    return pl.pallas_call(
        body, name=name, grid=(nc,),
        in_specs=[pl.BlockSpec((DN_CHUNK, 1536), rev(0)), pl.BlockSpec((DN_CHUNK, 128), rev(P_AB // 128)),
                  pl.BlockSpec((DN_CHUNK, 512), rev(P_ZC // 512)), pl.BlockSpec((1, 512, 128), lambda j: (nc - 1 - j, 0, 0)),
                  _const_spec((1, 128)), _const_spec((1, 128)), _const_spec((1, 128)),
                  pl.BlockSpec((DN_CHUNK, 512), rev(0))],
        out_specs=[pl.BlockSpec((DN_CHUNK, 1536), rev(0)), pl.BlockSpec((DN_CHUNK, 128), rev(0)),
                   pl.BlockSpec((DN_CHUNK, 512), rev(0)),
                   _const_spec((1, 128)), _const_spec((1, 128)), _const_spec((1, 128))],
        out_shape=[jax.ShapeDtypeStruct((t, 1536), F32), jax.ShapeDtypeStruct((t, 128), BF16),
                   jax.ShapeDtypeStruct((t, 512), BF16),
                   jax.ShapeDtypeStruct((1, 128), F32), jax.ShapeDtypeStruct((1, 128), F32), jax.ShapeDtypeStruct((1, 128), F32)],
        scratch_shapes=[pltpu.VMEM((512, 128), F32)],
        compiler_params=_cparams(1),
    )(qkv, proj, proj, ssave, a_log, dt_bias, dn_g, dyc)


def loss_head(name, y, target, tm):
    t, d = y.shape

    def body(y_ref, t_ref, dy_ref, l_ref):
        err = y_ref[...] - t_ref[...]
        dy_ref[...] = err * (1.0 / d)
        part = 0.5 * jnp.sum(jnp.sum(err * err, axis=-1, keepdims=True) * (1.0 / d), axis=0, keepdims=True)

        @pl.when(pl.program_id(0) == 0)
        def _():
            l_ref[...] = part

        @pl.when(pl.program_id(0) > 0)
        def _():
            l_ref[...] += part

    return pl.pallas_call(
        body, name=name, grid=(t // tm,),
        in_specs=[_row_spec(tm, d, 0), _row_spec(tm, d, 0)],
        out_specs=[_row_spec(tm, d, 0), _const_spec((1, 1))],
        out_shape=[jax.ShapeDtypeStruct((t, d), F32), jax.ShapeDtypeStruct((1, 1), F32)],
        compiler_params=_cparams(1),
    )(y, target)


TM = 512
TM_MERGE = 256
TN_IN = 1152


def _lane_pad(v, n=128):
    return jnp.pad(v.astype(F32), (0, n - v.shape[0]))[None, :]


def f_norm_mod_res(x, g, scale, shift):
    return f_norm_mod(x, g, scale, shift), x


def prep_layer(w):
    p = dict(w)
    p["wp"] = _pad_w_in(w["w_in"])
    p["wpt"] = p["wp"].T
    p["wpa"] = _perm_heads_rows(w["w_proj_a"])
    p["dw32"] = jnp.pad(w["dw_w"], ((0, 32 - CONV_K), (0, 0)))
    p["sconv8"] = jnp.pad(w["sconv_w"], ((0, 8 - DN_CONV_K), (0, 0)))
    p["qg"] = jnp.tile(w["q_norm_g"], 2)[None, :]
    p["kg"] = jnp.tile(w["k_norm_g"], 2)[None, :]
    p["sinks128"] = _lane_pad(w["sinks"])
    p["al"] = _lane_pad(w["a_log"])
    p["dtb"] = _lane_pad(w["dt_bias"])
    p["dng"] = w["dn_norm_g"][None, :]
    return p


def layer_fwd(tag, x, c8, p):
    mod = ada_fwd(f"ada_fwd{tag}", c8, p["w_ada"], p["b_ada"][None, :])[0:1]
    d = D_MODEL
    shift, scale, gate = mod[:, :d], mod[:, d:2 * d], mod[:, 2 * d:]
    g = p["norm_g"][None, :]
    (h,) = rowwise_fwd(f"norm_fwd{tag}", f_norm_mod, [(x, d, 0)], [g, scale, shift], [(d, BF16)], TM)
    proj = matmul_nn(f"inproj_fwd{tag}", h, p["wp"], F32, TM, TN_IN, d)
    ya = attn_fwd(f"attn_fwd{tag}", proj, p["qg"], p["kg"], p["sinks128"])
    ub = glu_conv_fwd(f"glu_conv_fwd{tag}", proj, p["dw32"], p["dw_b"][None, :])
    conf_consts = [p["ln_g"][None, :], p["ln_b"][None, :], p["pw2_w"], p["pw2_b"][None, :]]
    (yb,) = rowwise_fwd(f"conf_fwd{tag}", f_conf_tail, [(ub, 512, 0), (proj, 512, P_ZB // 512)], conf_consts, [(512, F32)], TM)
    qkv = sconv_fwd(f"sconv_fwd{tag}", proj, p["sconv8"])
    yc, ssave = delta_fwd(f"delta_fwd{tag}", qkv, proj, p["al"], p["dtb"], p["dng"])
    merge_consts = [gate, p["wpa"], p["w_proj_b"], p["w_proj_c"], p["w_out"]]
    merge_rows = [(ya, 512, 0), (yb, 512, 0), (yc, 512, 0), (proj, 3 * d, P_MG // (3 * d)), (x, d, 0)]
    (xn,) = rowwise_fwd(f"merge_fwd{tag}", f_merge, merge_rows, merge_consts, [(d, F32)], TM_MERGE)
    saved = dict(x=x, h=h, proj=proj, ub=ub, qkv=qkv, ssave=ssave, norm_consts=[g, scale, shift],
                 conf_consts=conf_consts, merge_consts=merge_consts, merge_rows=merge_rows)
    return xn, saved


def layer_bwd(tag, dxn, c8, p, s):
    d = D_MODEL
    proj = s["proj"]
    dya, dyb, dyc, dmg, dgate, dwpa, dwpb, dwpc, dwout = rowwise_bwd(
        f"merge_bwd{tag}", f_merge, s["merge_rows"], s["merge_consts"], [(dxn, d, 0)], [F32, F32, F32, BF16, None], TM_MERGE)
    dqz, dkv, dqg, dkg, dsinks = attn_bwd(f"attn_bwd{tag}", proj, p["qg"], p["kg"], p["sinks128"], dya)
    dub, dzb, dln_g, dln_b, dpw2_w, dpw2_b = rowwise_bwd(
        f"conf_bwd{tag}", f_conf_tail, [(s["ub"], 512, 0), (proj, 512, P_ZB // 512)], s["conf_consts"], [(dyb, 512, 0)],
        [F32, BF16], TM)
    dglu, ddw32, ddw_b = glu_conv_bwd(f"glu_conv_bwd{tag}", proj, p["dw32"], dub)
    dqkv, dab, dzc, dal, ddtb, ddng = delta_bwd(f"delta_bwd{tag}", s["qkv"], proj, s["ssave"], p["al"], p["dtb"], p["dng"], dyc)
    dqkv_pre, dsconv8 = sconv_bwd(f"sconv_bwd{tag}", proj, p["sconv8"], dqkv)
    dproj = jnp.concatenate([dqz, dglu, dzb, dzc, dmg, dqkv_pre, dkv, dab], axis=1)
    dh = matmul_nn(f"inproj_bwd_dh{tag}", dproj, p["wpt"], F32, TM, d, TN_IN)
    dwp = matmul_tn(f"inproj_bwd_dw{tag}", s["h"], dproj, 512, TN_IN, TM)
    dx, dnorm_g, dscale, dshift = rowwise_bwd(
        f"norm_bwd{tag}", f_norm_mod_res, [(s["x"], d, 0)], s["norm_consts"], [(dh, d, 0), (dxn, d, 0)], [F32], TM)
    dmod = jnp.concatenate([dshift, dscale, dgate], axis=1)
    dw_ada = ada_bwd(f"ada_bwd{tag}", c8, jnp.pad(dmod, ((0, 7), (0, 0))))
    grads = dict(
        w_ada=dw_ada, b_ada=dmod[0], norm_g=dnorm_g[0], w_in=_unpad_w_in(dwp),
        q_norm_g=dqg[0, :64] + dqg[0, 64:], k_norm_g=dkg[0, :64] + dkg[0, 64:], sinks=dsinks[0, :ATT_HEADS],
        dw_w=ddw32[:CONV_K], dw_b=ddw_b[0], ln_g=dln_g[0], ln_b=dln_b[0], pw2_w=dpw2_w, pw2_b=dpw2_b[0],
        sconv_w=dsconv8[:DN_CONV_K], a_log=dal[0, :DN_HEADS], dt_bias=ddtb[0, :DN_HEADS], dn_norm_g=ddng[0],
        w_proj_a=_unperm_heads_rows(dwpa), w_proj_b=dwpb, w_proj_c=dwpc, w_out=dwout)
    return dx, grads


SHARDED = {"w_ada": (2, False), "w_in": (2, False), "dw_w": (2, True), "pw2_w": (1, False), "sconv_w": (2, True),
           "w_proj_a": (2, False), "w_proj_b": (2, False), "w_proj_c": (2, False), "w_out": (1, False)}
SMALL = ("b_ada", "norm_g", "q_norm_g", "k_norm_g", "sinks", "dw_b", "ln_g", "ln_b", "pw2_b", "a_log", "dt_bias",
         "dn_norm_g")
PACK_COLS = 1024
PACK_ROWS = 6944
HALF_ROWS = PACK_ROWS // 2
SUM_ROWS = 496
SMALL_ROWS = 104


def _pack_rows(flat_parts, lead):
    flat = jnp.concatenate(flat_parts, axis=-1)
    n = PACK_ROWS * PACK_COLS
    assert flat.shape[-1] <= n, flat.shape
    flat = jnp.pad(flat, [(0, 0)] * (flat.ndim - 1) + [(0, n - flat.shape[-1])])
    return flat.reshape(lead + (PACK_ROWS, PACK_COLS))


def pack_weight_shards(shards):
    parts = []
    for name, (_, exact) in SHARDED.items():
        v = shards[name]
        hi = v.astype(BF16)
        parts.append(hi.reshape(-1))
        if exact:
            parts.append((v - hi.astype(F32)).astype(BF16).reshape(-1))
    return _pack_rows(parts, ())


def _join_shards(v, axis):
    if axis == 2:
        return jnp.transpose(v, (1, 2, 0, 3)).reshape(v.shape[1], v.shape[2], N_CHIPS * v.shape[3])
    return jnp.transpose(v, (1, 0, 2, 3)).reshape(v.shape[1], N_CHIPS * v.shape[2], v.shape[3])


def _split_shards(v, axis):
    l, a, b = v.shape
    if axis == 2:
        return jnp.transpose(v.reshape(l, a, N_CHIPS, b // N_CHIPS), (2, 0, 1, 3))
    return jnp.transpose(v.reshape(l, N_CHIPS, a // N_CHIPS, b), (1, 0, 2, 3))


def unpack_gathered_weights(gathered, shard_shapes):
    flat = gathered.reshape(N_CHIPS, -1)
    out, off = {}, 0
    for name, (axis, exact) in SHARDED.items():
        shp = shard_shapes[name]
        n = int(np.prod(shp))
        v = flat[:, off:off + n].reshape((N_CHIPS,) + tuple(shp))
        off += n
        if exact:
            v = v.astype(F32) + flat[:, off:off + n].reshape((N_CHIPS,) + tuple(shp)).astype(F32)
            off += n
        out[name] = _join_shards(v, axis)
    return out


def pack_full_grads(grads):
    parts = [_split_shards(grads[name], axis).reshape(N_CHIPS, -1) for name, (axis, _) in SHARDED.items()]
    return _pack_rows(parts, (N_CHIPS,))


def unpack_shard_grads(packed, shard_shapes):
    flat = packed.reshape(-1)
    out, off = {}, 0
    for name in SHARDED:
        shp = shard_shapes[name]
        n = int(np.prod(shp))
        out[name] = flat[off:off + n].reshape(shp)
        off += n
    return out


def pack_small(vals):
    flat = jnp.concatenate([vals[n].astype(F32).reshape(-1) for n in SMALL])
    return jnp.pad(flat, (0, SMALL_ROWS * 128 - flat.shape[0])).reshape(SMALL_ROWS, 128)


def unpack_small(packed, shapes):
    flat = packed.reshape(-1)
    out, off = {}, 0
    for n in SMALL:
        k = int(np.prod(shapes[n]))
        out[n] = flat[off:off + k].reshape(shapes[n])
        off += k
    return out


ANY = pl.BlockSpec(memory_space=pl.ANY)


def _place():
    x, y, c = lax.axis_index("x"), lax.axis_index("y"), lax.axis_index("c")
    chips = [(1 - x, y), (x, 1 - y), (1 - x, 1 - y)]
    return x, y, c, chips


def weights_allgather(packed):
    r, cols = packed.shape
    h = r // 2

    def body(src_ref, out_ref, send_sems, recv_sems, local_sem):
        x, y, c, chips = _place()
        sibling = (x, y, 1 - c)
        my_half = pl.ds(c * h, h)
        other_half = pl.ds((1 - c) * h, h)

        def ici(j, slot_chip, to):
            src = src_ref.at[my_half] if to is not None else out_ref.at[2 * slot_chip[0] + slot_chip[1], my_half]
            return pltpu.make_async_remote_copy(
                src_ref=src, dst_ref=out_ref.at[2 * slot_chip[0] + slot_chip[1], my_half],
                send_sem=send_sems.at[j], recv_sem=recv_sems.at[j],
                device_id=(x, y, c) if to is None else to, device_id_type=MESH)

        def d2d(j, slot_chip, half):
            ref = out_ref.at[2 * slot_chip[0] + slot_chip[1], half]
            return pltpu.make_async_remote_copy(src_ref=ref, dst_ref=ref, send_sem=send_sems.at[3 + j],
                                                recv_sem=recv_sems.at[3 + j], device_id=sibling, device_id_type=MESH)

        mine = pltpu.make_async_copy(src_ref, out_ref.at[2 * x + y], local_sem)
        mine.start()
        first = [ici(j, (x, y), (*chip, c)) for j, chip in enumerate(chips)]
        for cp in first:
            cp.start()
        passed = [d2d(j, chip, my_half) for j, chip in enumerate(chips)]
        for j, chip in enumerate(chips):
            ici(j, chip, None).wait_recv()
            passed[j].start()
        for j, chip in enumerate(chips):
            d2d(j, chip, other_half).wait_recv()
        for cp in first + passed:
            cp.wait_send()
        mine.wait()

    return pl.pallas_call(
        body, name="weights_allgather", out_shape=jax.ShapeDtypeStruct((N_CHIPS, r, cols), packed.dtype),
        in_specs=[ANY], out_specs=ANY,
        scratch_shapes=[pltpu.SemaphoreType.DMA((6,)), pltpu.SemaphoreType.DMA((6,)), pltpu.SemaphoreType.DMA],
    )(packed)


def grads_pair_exchange(g):
    _, r, cols = g.shape
    h = r // 2

    def body(g_ref, recv_ref, send_sem, recv_sem):
        x, y, c, _ = _place()
        cp = pltpu.make_async_remote_copy(src_ref=g_ref.at[:, pl.ds((1 - c) * h, h), :], dst_ref=recv_ref,
                                          send_sem=send_sem, recv_sem=recv_sem, device_id=(x, y, 1 - c), device_id_type=MESH)
        cp.start()
        cp.wait()

    return pl.pallas_call(
        body, name="grads_pair_exchange", out_shape=jax.ShapeDtypeStruct((N_CHIPS, h, cols), g.dtype),
        in_specs=[ANY], out_specs=ANY, scratch_shapes=[pltpu.SemaphoreType.DMA, pltpu.SemaphoreType.DMA],
    )(g)


def grads_pair_sum(g, recv):
    _, h, cols = recv.shape
    nb = h // SUM_ROWS

    def body(a_ref, b_ref, o_ref):
        o_ref[...] = a_ref[...] + b_ref[...]

    return pl.pallas_call(
        body, name="grads_pair_sum", grid=(N_CHIPS, nb),
        in_specs=[pl.BlockSpec((1, SUM_ROWS, cols), lambda s, i: (s, lax.axis_index("c") * nb + i, 0)),
                  pl.BlockSpec((1, SUM_ROWS, cols), lambda s, i: (s, i, 0))],
        out_specs=pl.BlockSpec((1, SUM_ROWS, cols), lambda s, i: (s, i, 0)),
        out_shape=jax.ShapeDtypeStruct(recv.shape, F32),
        compiler_params=_cparams(2),
    )(g, recv)


def grads_chip_exchange(p):
    _, h, cols = p.shape

    def body(p_ref, recv_ref, send_sems, recv_sems):
        x, y, c, chips = _place()
        cps = [pltpu.make_async_remote_copy(src_ref=p_ref.at[2 * chip[0] + chip[1]], dst_ref=recv_ref.at[j],
                                            send_sem=send_sems.at[j], recv_sem=recv_sems.at[j],
                                            device_id=(*chip, c), device_id_type=MESH) for j, chip in enumerate(chips)]
        for cp in cps:
            cp.start()
        for cp in cps:
            cp.wait()

    return pl.pallas_call(
        body, name="grads_chip_exchange", out_shape=jax.ShapeDtypeStruct((3, h, cols), p.dtype),
        in_specs=[ANY], out_specs=ANY, scratch_shapes=[pltpu.SemaphoreType.DMA((3,)), pltpu.SemaphoreType.DMA((3,))],
    )(p)


def grads_chip_sum(p, recv):
    _, h, cols = p.shape
    nb = h // SUM_ROWS

    def body(a_ref, b_ref, o_ref):
        o_ref[...] = ((a_ref[0] + b_ref[0]) + b_ref[1]) + b_ref[2]

    return pl.pallas_call(
        body, name="grads_chip_sum", grid=(nb,),
        in_specs=[pl.BlockSpec((1, SUM_ROWS, cols), lambda i: (2 * lax.axis_index("x") + lax.axis_index("y"), i, 0)),
                  pl.BlockSpec((3, SUM_ROWS, cols), lambda i: (0, i, 0))],
        out_specs=pl.BlockSpec((SUM_ROWS, cols), lambda i: (i, 0)),
        out_shape=jax.ShapeDtypeStruct((h, cols), F32),
        compiler_params=_cparams(1),
    )(p, recv)


def grads_pair_gather(red):
    h, cols = red.shape

    def body(red_ref, out_ref, send_sem, recv_sem, local_sem):
        x, y, c, _ = _place()
        mine = pltpu.make_async_copy(red_ref, out_ref.at[pl.ds(c * h, h)], local_sem)
        mine.start()
        cp = pltpu.make_async_remote_copy(src_ref=red_ref, dst_ref=out_ref.at[pl.ds(c * h, h)], send_sem=send_sem,
                                          recv_sem=recv_sem, device_id=(x, y, 1 - c), device_id_type=MESH)
        cp.start()
        pltpu.make_async_remote_copy(src_ref=red_ref, dst_ref=out_ref.at[pl.ds((1 - c) * h, h)], send_sem=send_sem,
                                     recv_sem=recv_sem, device_id=(x, y, 1 - c), device_id_type=MESH).wait_recv()
        cp.wait_send()
        mine.wait()

    return pl.pallas_call(
        body, name="grads_pair_gather", out_shape=jax.ShapeDtypeStruct((2 * h, cols), red.dtype),
        in_specs=[ANY], out_specs=ANY,
        scratch_shapes=[pltpu.SemaphoreType.DMA, pltpu.SemaphoreType.DMA, pltpu.SemaphoreType.DMA],
    )(red)


def small_allreduce(v):
    m, n = v.shape

    def body(x_ref, sum_ref, all_ref, send_sems, recv_sems, local_sem):
        x, y, c, chips = _place()
        me, sibling = (x, y, c), (x, y, 1 - c)

        def rows(px, py, pc):
            return all_ref.at[pl.ds((4 * px + 2 * py + pc) * m, m), :]

        def copy(k, block, to, src=None):
            return pltpu.make_async_remote_copy(src_ref=rows(*block) if src is None else src, dst_ref=rows(*block),
                                                send_sem=send_sems.at[k], recv_sem=recv_sems.at[k],
                                                device_id=to, device_id_type=MESH)

        mine = pltpu.make_async_copy(x_ref, rows(*me), local_sem)
        mine.start()
        first = [copy(0, me, sibling, src=x_ref)]
        first += [copy(1 + j, me, (*chip, c), src=x_ref) for j, chip in enumerate(chips)]
        for cp in first:
            cp.start()
        passed = [copy(4 + j, (*chip, c), sibling) for j, chip in enumerate(chips)]
        for j, chip in enumerate(chips):
            copy(1 + j, (*chip, c), me).wait_recv()
            passed[j].start()
        copy(0, sibling, me).wait_recv()
        for j, chip in enumerate(chips):
            copy(4 + j, (*chip, 1 - c), me).wait_recv()
        for cp in first + passed:
            cp.wait_send()
        mine.wait()
        acc = all_ref[0:m, :]
        for dev in range(1, 8):
            acc = acc + all_ref[dev * m:(dev + 1) * m, :]
        sum_ref[...] = acc

    vm = pl.BlockSpec(memory_space=pltpu.VMEM)
    return pl.pallas_call(
        body, name="small_allreduce",
        out_shape=[jax.ShapeDtypeStruct((m, n), F32), jax.ShapeDtypeStruct((8 * m, n), F32)],
        in_specs=[vm], out_specs=[vm, vm],
        scratch_shapes=[pltpu.SemaphoreType.DMA((7,)), pltpu.SemaphoreType.DMA((7,)), pltpu.SemaphoreType.DMA],
    )(v)[0]


def reduce_scatter_grads(g):
    recv = grads_pair_exchange(g)
    part = grads_pair_sum(g, recv)
    recv2 = grads_chip_exchange(part)
    red = grads_chip_sum(part, recv2)
    return grads_pair_gather(red)


def adamw(name, w, g, m, v, tr):
    r, cols = w.shape

    def body(w_ref, g_ref, m_ref, v_ref, d_ref, nm_ref, nv_ref):
        gv = g_ref[...]
        nm = ADAM_B1 * m_ref[...] + (1.0 - ADAM_B1) * gv
        nv = ADAM_B2 * v_ref[...] + (1.0 - ADAM_B2) * (gv * gv)
        m_hat = nm / (1.0 - ADAM_B1 ** ADAM_STEP)
        v_hat = nv / (1.0 - ADAM_B2 ** ADAM_STEP)
        d_ref[...] = -ADAM_LR * (m_hat / (jnp.sqrt(v_hat) + ADAM_EPS) + ADAM_WD * w_ref[...])
        nm_ref[...] = nm
        nv_ref[...] = nv

    spec = pl.BlockSpec((tr, cols), lambda i: (i, 0))
    return pl.pallas_call(
        body, name=name, grid=(r // tr,), in_specs=[spec] * 4, out_specs=[spec] * 3,
        out_shape=[jax.ShapeDtypeStruct((r, cols), F32)] * 3, compiler_params=_cparams(1),
    )(w, g, m, v)


ADAM_ROWS = {"w_ada": 512, "w_in": 256, "dw_w": 62, "pw2_w": 256, "sconv_w": 8, "w_proj_a": 512, "w_proj_b": 512,
             "w_proj_c": 512, "w_out": 256}

WEIGHT_NAMES = ("w_ada", "b_ada", "norm_g", "w_in", "q_norm_g", "k_norm_g", "sinks", "dw_w", "dw_b", "ln_g", "ln_b",
                "pw2_w", "pw2_b", "sconv_w", "a_log", "dt_bias", "dn_norm_g", "w_proj_a", "w_proj_b", "w_proj_c", "w_out")


def kernel(x, c, w_ada, b_ada, norm_g, w_in, q_norm_g, k_norm_g, sinks, dw_w, dw_b, ln_g, ln_b, pw2_w, pw2_b, sconv_w, a_log, dt_bias, dn_norm_g, w_proj_a, w_proj_b, w_proj_c, w_out, loss_target, m_w_ada, m_b_ada, m_norm_g, m_w_in, m_q_norm_g, m_k_norm_g, m_sinks, m_dw_w, m_dw_b, m_ln_g, m_ln_b, m_pw2_w, m_pw2_b, m_sconv_w, m_a_log, m_dt_bias, m_dn_norm_g, m_w_proj_a, m_w_proj_b, m_w_proj_c, m_w_out, v_w_ada, v_b_ada, v_norm_g, v_w_in, v_q_norm_g, v_k_norm_g, v_sinks, v_dw_w, v_dw_b, v_ln_g, v_ln_b, v_pw2_w, v_pw2_b, v_sconv_w, v_a_log, v_dt_bias, v_dn_norm_g, v_w_proj_a, v_w_proj_b, v_w_proj_c, v_w_out):
    args = dict(locals())
    w = {n: args[n] for n in WEIGHT_NAMES}
    mom = {n: args["m_" + n] for n in WEIGHT_NAMES}
    var = {n: args["v_" + n] for n in WEIGHT_NAMES}
    shard_shapes = {n: w[n].shape for n in SHARDED}

    full = unpack_gathered_weights(weights_allgather(pack_weight_shards(w)), shard_shapes)
    layers = []
    for l in range(DEPTH):
        lw = {n: (full[n][l] if n in SHARDED else w[n][l]) for n in WEIGHT_NAMES}
        layers.append(prep_layer(lw))

    c8 = jnp.tile(c, (8, 1))
    act, saved = x[0], []
    for l in range(DEPTH):
        act, s = layer_fwd(str(l), act, c8, layers[l])
        saved.append(s)
    dact, loss_part = loss_head("loss_head", act, loss_target[0], TM)
    loss = lax.psum(loss_part[0, 0], ("x", "y", "c"))
    layer_grads = [None] * DEPTH
    for l in reversed(range(DEPTH)):
        dact, layer_grads[l] = layer_bwd(str(l), dact, c8, layers[l], saved[l])
    grads = {n: jnp.stack([layer_grads[l][n] for l in range(DEPTH)]) for n in WEIGHT_NAMES}

    shard_grads = unpack_shard_grads(reduce_scatter_grads(pack_full_grads(grads)), shard_shapes)
    small_shapes = {n: w[n].shape for n in SMALL}
    small_grads = small_allreduce(pack_small(grads))
    final_grads = {**shard_grads, **unpack_small(small_grads, small_shapes)}

    delta, new_m, new_v = {}, {}, {}
    for n in SHARDED:
        shp = w[n].shape
        two_d = lambda a, shp=shp: a.reshape(shp[0] * shp[1], shp[2])
        d, nm, nv = adamw("adamw_" + n, two_d(w[n]), two_d(final_grads[n]), two_d(mom[n]), two_d(var[n]), ADAM_ROWS[n])
        delta[n], new_m[n], new_v[n] = d.reshape(shp), nm.reshape(shp), nv.reshape(shp)
    d, nm, nv = adamw("adamw_small", pack_small(w), small_grads, pack_small(mom), pack_small(var), SMALL_ROWS)
    delta.update(unpack_small(d, small_shapes))
    new_m.update(unpack_small(nm, small_shapes))
    new_v.update(unpack_small(nv, small_shapes))

    return (loss, dact[None], *[final_grads[n] for n in WEIGHT_NAMES], *[delta[n] for n in WEIGHT_NAMES],
            *[new_m[n] for n in WEIGHT_NAMES], *[new_v[n] for n in WEIGHT_NAMES])
```

```python
import functools

import numpy as np
import jax
import jax.numpy as jnp
from jax import lax
from jax.experimental import pallas as pl
from jax.experimental.pallas import tpu as pltpu

F32 = jnp.float32
BF16 = jnp.bfloat16
MESH = pl.DeviceIdType.MESH

D_MODEL = 1024
DEPTH = 2
ATT_HEADS = 8
ATT_HEAD_DIM = 64
WINDOW = 128
CONV_K = 31
DN_HEADS = 4
DN_CONV_K = 4
DN_CHUNK = 64
EPS = 1e-6
NEG_INF = -1e30
N_CHIPS = 4
D_IN = 7944

ADAM_LR = 0.001
ADAM_B1 = 0.9
ADAM_B2 = 0.999
ADAM_EPS = 1e-08
ADAM_WD = 0.01
ADAM_STEP = 10

VMEM_LIMIT = 56 * 1024 * 1024

P_QA, P_ZA, P_GLU, P_ZB, P_ZC, P_MG, P_QKV, P_KA, P_VA, P_AB, P_TOTAL = (
    0, 512, 1024, 2048, 2560, 3072, 6144, 7680, 7808, 7936, 8064)
HEAD_ORDER = (0, 4, 1, 5, 2, 6, 3, 7)


def _in_pieces():
    p = [(0 + 64 * h, 64) for h in HEAD_ORDER]
    p += [(768 + 64 * h, 64) for h in HEAD_ORDER]
    for g in range(4):
        p += [(1280 + 128 * g, 128), (1792 + 128 * g, 128)]
    p += [(2304, 512), (4360, 512), (4872, 3072), (2816, 1536), (512, 128), (640, 128), (4352, 8)]
    return p


def _perm_heads_rows(w):
    return jnp.concatenate([w[64 * h:64 * h + 64] for h in HEAD_ORDER], axis=0)


def _unperm_heads_rows(w):
    inv = [HEAD_ORDER.index(h) for h in range(8)]
    return jnp.concatenate([w[64 * s:64 * s + 64] for s in inv], axis=0)


def _dot(a, b, dims, exact):
    if exact:
        return lax.dot_general(a.astype(F32), b.astype(F32), (dims, ((), ())), precision=lax.Precision.HIGHEST,
                               preferred_element_type=F32)
    return lax.dot_general(a.astype(BF16), b.astype(BF16), (dims, ((), ())), preferred_element_type=F32)


def _make_mm(exact):
    @jax.custom_vjp
    def nn(a, b):
        return _dot(a, b, ((1,), (0,)), exact)

    @jax.custom_vjp
    def nt(a, b):
        return _dot(a, b, ((1,), (1,)), exact)

    @jax.custom_vjp
    def tn(a, b):
        return _dot(a, b, ((0,), (0,)), exact)

    nn.defvjp(lambda a, b: (nn(a, b), (a, b)),
              lambda r, g: (nt(g, r[1]).astype(r[0].dtype), tn(r[0], g).astype(r[1].dtype)))
    nt.defvjp(lambda a, b: (nt(a, b), (a, b)),
              lambda r, g: (nn(g, r[1]).astype(r[0].dtype), tn(g, r[0]).astype(r[1].dtype)))
    tn.defvjp(lambda a, b: (tn(a, b), (a, b)),
              lambda r, g: (nt(r[1], g).astype(r[0].dtype), nn(r[0], g).astype(r[1].dtype)))
    return nn, nt, tn


mm, mm_nt, mm_tn = _make_mm(False)
xmm, xmm_nt, xmm_tn = _make_mm(True)


def _sigmoid(x):
    return 1.0 / (1.0 + jnp.exp(-x))


def _silu(x):
    return x * _sigmoid(x)


def _softplus(x):
    return jnp.maximum(x, 0.0) + jnp.log(1.0 + jnp.exp(-jnp.abs(x)))


def _cparams(n_grid):
    return pltpu.CompilerParams(dimension_semantics=("arbitrary",) * n_grid, vmem_limit_bytes=VMEM_LIMIT)


def _row_spec(tm, width, colblk):
    return pl.BlockSpec((tm, width), lambda i, cb=colblk: (i, cb))


def _const_spec(shape):
    nd = len(shape)
    return pl.BlockSpec(tuple(shape), lambda i, nd=nd: (0,) * nd)


def rowwise_fwd(name, f, rows, consts, outs, tm):
    n_r, n_c = len(rows), len(consts)
    t = rows[0][0].shape[0]

    def body(*refs):
        vals = [r[...] for r in refs[:n_r + n_c]]
        res = f(*vals)
        if not isinstance(res, (tuple, list)):
            res = (res,)
        for o_ref, v in zip(refs[n_r + n_c:], res):
            o_ref[...] = v.astype(o_ref.dtype)

    return pl.pallas_call(
        body, name=name, grid=(t // tm,),
        in_specs=[_row_spec(tm, w, cb) for _, w, cb in rows] + [_const_spec(c.shape) for c in consts],
        out_specs=[_row_spec(tm, w, 0) for w, _ in outs],
        out_shape=[jax.ShapeDtypeStruct((t, w), dt) for w, dt in outs],
        compiler_params=_cparams(1),
    )(*[a for a, _, _ in rows], *consts)


def rowwise_bwd(name, f, rows, consts, cts, row_grad_dtypes, tm):
    n_r, n_c, n_ct = len(rows), len(consts), len(cts)
    t = rows[0][0].shape[0]
    keep = [k for k, dt in enumerate(row_grad_dtypes) if dt is not None]

    def body(*refs):
        ins = [r[...].astype(F32) for r in refs[:n_r + n_c]]
        g_out = [r[...].astype(F32) for r in refs[n_r + n_c:n_r + n_c + n_ct]]
        out_refs = refs[n_r + n_c + n_ct:]

        def fw(*a):
            res = f(*a)
            return tuple(res) if isinstance(res, (tuple, list)) else (res,)

        _, vjp = jax.vjp(fw, *ins)
        grads = vjp(tuple(g_out))
        for o_ref, k in zip(out_refs[:len(keep)], keep):
            o_ref[...] = grads[k].astype(o_ref.dtype)
        first = pl.program_id(0) == 0
        for o_ref, g in zip(out_refs[len(keep):], grads[n_r:]):
            @pl.when(first)
            def _(o_ref=o_ref, g=g):
                o_ref[...] = g

            @pl.when(jnp.logical_not(first))
            def _(o_ref=o_ref, g=g):
                o_ref[...] += g

    return pl.pallas_call(
        body, name=name, grid=(t // tm,),
        in_specs=[_row_spec(tm, w, cb) for _, w, cb in rows] + [_const_spec(c.shape) for c in consts]
        + [_row_spec(tm, w, cb) for _, w, cb in cts],
        out_specs=[_row_spec(tm, rows[k][1], 0) for k in keep] + [_const_spec(c.shape) for c in consts],
        out_shape=[jax.ShapeDtypeStruct((t, rows[k][1]), row_grad_dtypes[k]) for k in keep]
        + [jax.ShapeDtypeStruct(c.shape, F32) for c in consts],
        compiler_params=_cparams(1),
    )(*[a for a, _, _ in rows], *consts, *[a for a, _, _ in cts])


def f_norm_mod(x, g, scale, shift):
    y = x * lax.rsqrt(jnp.mean(x * x, axis=-1, keepdims=True) + EPS) * g
    return y * (1.0 + scale) + shift


def f_conf_tail(u, zb, ln_g, ln_b, pw2_w, pw2_b):
    mu = jnp.mean(u, axis=-1, keepdims=True)
    xc = u - mu
    var = jnp.mean(xc * xc, axis=-1, keepdims=True)
    y = _silu(xc * lax.rsqrt(var + EPS) * ln_g + ln_b)
    return (mm(y, pw2_w) + pw2_b) * _silu(zb)


def f_merge(ya, yb, yc, mg, x, gate, wpa, wpb, wpc, wout):
    d = D_MODEL
    merged = (_sigmoid(mg[:, :d]) * mm(ya, wpa) + _sigmoid(mg[:, d:2 * d]) * mm(yb, wpb)
              + _sigmoid(mg[:, 2 * d:]) * mm(yc, wpc))
    return x + gate * mm(merged, wout)


def matmul_nn(name, a, b, out_dtype, tm, tn, tk):
    m, k = a.shape
    n = b.shape[1]
    nk = k // tk

    def body(a_ref, b_ref, o_ref, acc_ref):
        kk = pl.program_id(2)
        part = jnp.dot(a_ref[...].astype(BF16), b_ref[...].astype(BF16), preferred_element_type=F32)

        @pl.when(kk == 0)
        def _():
            acc_ref[...] = part

        @pl.when(kk > 0)
        def _():
            acc_ref[...] += part

        @pl.when(kk == nk - 1)
        def _():
            o_ref[...] = acc_ref[...].astype(o_ref.dtype)

    return pl.pallas_call(
        body, name=name, grid=(m // tm, n // tn, nk),
        in_specs=[pl.BlockSpec((tm, tk), lambda i, j, kk: (i, kk)), pl.BlockSpec((tk, tn), lambda i, j, kk: (kk, j))],
        out_specs=pl.BlockSpec((tm, tn), lambda i, j, kk: (i, j)),
        out_shape=jax.ShapeDtypeStruct((m, n), out_dtype),
        scratch_shapes=[pltpu.VMEM((tm, tn), F32)],
        compiler_params=_cparams(3),
    )(a, b)


def matmul_tn(name, a, b, ta, tn, tm):
    m, k = a.shape
    n = b.shape[1]
    nm = m // tm

    def body(a_ref, b_ref, o_ref):
        mm_ = pl.program_id(2)
        part = lax.dot_general(a_ref[...].astype(BF16), b_ref[...].astype(BF16), (((0,), (0,)), ((), ())),
                               preferred_element_type=F32)

        @pl.when(mm_ == 0)
        def _():
            o_ref[...] = part

        @pl.when(mm_ > 0)
        def _():
            o_ref[...] += part

    return pl.pallas_call(
        body, name=name, grid=(k // ta, n // tn, nm),
        in_specs=[pl.BlockSpec((tm, ta), lambda i, j, r: (r, i)), pl.BlockSpec((tm, tn), lambda i, j, r: (r, j))],
        out_specs=pl.BlockSpec((ta, tn), lambda i, j, r: (i, j)),
        out_shape=jax.ShapeDtypeStruct((k, n), F32),
        compiler_params=_cparams(3),
    )(a, b)


def ada_fwd(name, c8, w_ada, b_ada):
    def body(c_ref, w_ref, b_ref, o_ref):
        o_ref[...] = mm(_silu(c_ref[...]), w_ref[...]) + b_ref[...]

    return pl.pallas_call(
        body, name=name, out_shape=jax.ShapeDtypeStruct((8, 3 * D_MODEL), F32),
        compiler_params=pltpu.CompilerParams(vmem_limit_bytes=VMEM_LIMIT),
    )(c8, w_ada, b_ada)


def ada_bwd(name, c8, dmod8):
    tn = 768

    def body(c_ref, d_ref, o_ref):
        row0 = lax.broadcasted_iota(jnp.int32, (8, 1), 0) == 0
        sc = jnp.where(row0, _silu(c_ref[...]), 0.0)
        o_ref[...] = mm_tn(sc, d_ref[...])

    return pl.pallas_call(
        body, name=name, grid=(3 * D_MODEL // tn,),
        in_specs=[pl.BlockSpec((8, D_MODEL), lambda j: (0, 0)), pl.BlockSpec((8, tn), lambda j: (0, j))],
        out_specs=pl.BlockSpec((D_MODEL, tn), lambda j: (0, j)),
        out_shape=jax.ShapeDtypeStruct((D_MODEL, 3 * D_MODEL), F32),
        compiler_params=_cparams(1),
    )(c8, dmod8)


def _f_attn(first_block, q, za, kc, vc, kp, vp, qg, kg, sinks):
    w = WINDOW
    lane = lax.broadcasted_iota(jnp.int32, (1, 128), 1)
    halves = [lane < 64, lane >= 64]

    def rms_halves(x, g):
        x2 = x * x
        s0 = jnp.sum(jnp.where(halves[0], x2, 0.0), axis=-1, keepdims=True)
        s1 = jnp.sum(jnp.where(halves[1], x2, 0.0), axis=-1, keepdims=True)
        r = jnp.where(halves[0], lax.rsqrt(s0 / 64.0 + EPS), lax.rsqrt(s1 / 64.0 + EPS))
        return x * r * g

    kcat = rms_halves(jnp.concatenate([kp, kc], axis=0), kg)
    vcat = jnp.concatenate([vp, vc], axis=0)
    qi = lax.broadcasted_iota(jnp.int32, (w, 2 * w), 0)
    kj = lax.broadcasted_iota(jnp.int32, (w, 2 * w), 1)
    dist = qi + w - kj
    valid = (dist >= 0) & (dist < w) & (jnp.logical_not(first_block) | (kj >= w))
    distf = dist.astype(F32)
    outs = []
    for grp in range(4):
        qn = rms_halves(q[:, 128 * grp:128 * grp + 128], qg) * (ATT_HEAD_DIM ** -0.5)
        o_grp = jnp.zeros((w, 128), F32)
        for half in range(2):
            head = HEAD_ORDER[2 * grp + half]
            slope = 2.0 ** (-8.0 * (head + 1) / ATT_HEADS)
            sink = jnp.sum(jnp.where(lane == head, sinks, 0.0), axis=-1, keepdims=True)
            s = mm_nt(jnp.where(halves[half], qn, 0.0), kcat) - slope * distf
            s = jnp.where(valid, s, NEG_INF)
            m = lax.stop_gradient(jnp.maximum(jnp.max(s, axis=-1, keepdims=True), sink))
            p = jnp.exp(s - m)
            denom = jnp.sum(p, axis=-1, keepdims=True) + jnp.exp(sink - m)
            o_grp = o_grp + mm(p / denom, jnp.where(halves[half], vcat, 0.0))
        outs.append(o_grp)
    return jnp.concatenate(outs, axis=1) * _silu(za)


def attn_fwd(name, proj, qg, kg, sinks):
    t = proj.shape[0]
    nb = t // WINDOW

    def body(q_ref, za_ref, kc_ref, vc_ref, kp_ref, vp_ref, qg_ref, kg_ref, s_ref, o_ref):
        first = pl.program_id(0) == 0
        o_ref[...] = _f_attn(first, q_ref[...], za_ref[...], kc_ref[...], vc_ref[...], kp_ref[...], vp_ref[...],
                             qg_ref[...], kg_ref[...], s_ref[...])

    cur = lambda cb: (lambda i: (i, cb))
    prev = lambda cb: (lambda i: (jnp.maximum(i - 1, 0), cb))
    return pl.pallas_call(
        body, name=name, grid=(nb,),
        in_specs=[pl.BlockSpec((WINDOW, 512), cur(P_QA // 512)), pl.BlockSpec((WINDOW, 512), cur(P_ZA // 512)),
                  pl.BlockSpec((WINDOW, 128), cur(P_KA // 128)), pl.BlockSpec((WINDOW, 128), cur(P_VA // 128)),
                  pl.BlockSpec((WINDOW, 128), prev(P_KA // 128)), pl.BlockSpec((WINDOW, 128), prev(P_VA // 128)),
                  _const_spec((1, 128)), _const_spec((1, 128)), _const_spec((1, 128))],
        out_specs=pl.BlockSpec((WINDOW, 512), lambda i: (i, 0)),
        out_shape=jax.ShapeDtypeStruct((t, 512), F32),
        compiler_params=_cparams(1),
    )(proj, proj, proj, proj, proj, proj, qg, kg, sinks)


def attn_bwd(name, proj, qg, kg, sinks, dya):
    t = proj.shape[0]
    nb = t // WINDOW

    def body(q_ref, za_ref, kc_ref, vc_ref, kp_ref, vp_ref, qg_ref, kg_ref, s_ref, dy_ref,
             dqz_ref, dkv_ref, dqg_ref, dkg_ref, ds_ref, carry_ref):
        j = pl.program_id(0)
        first = j == nb - 1

        @pl.when(j == 0)
        def _():
            carry_ref[...] = jnp.zeros_like(carry_ref)
            dqg_ref[...] = jnp.zeros_like(dqg_ref)
            dkg_ref[...] = jnp.zeros_like(dkg_ref)
            ds_ref[...] = jnp.zeros_like(ds_ref)

        ins = [r[...] for r in (q_ref, za_ref, kc_ref, vc_ref, kp_ref, vp_ref, qg_ref, kg_ref, s_ref)]
        _, vjp = jax.vjp(functools.partial(_f_attn, first), *ins)
        dq, dza, dkc, dvc, dkp, dvp, dqg, dkg, dsk = vjp(dy_ref[...])
        dqz_ref[:, 0:512] = dq.astype(dqz_ref.dtype)
        dqz_ref[:, 512:1024] = dza.astype(dqz_ref.dtype)
        dkv_ref[:, 0:128] = (dkc + carry_ref[0]).astype(dkv_ref.dtype)
        dkv_ref[:, 128:256] = (dvc + carry_ref[1]).astype(dkv_ref.dtype)
        carry_ref[0] = dkp
        carry_ref[1] = dvp
        dqg_ref[...] += dqg
        dkg_ref[...] += dkg
        ds_ref[...] += dsk

    cur = lambda cb: (lambda j: (nb - 1 - j, cb))
    prev = lambda cb: (lambda j: (jnp.maximum(nb - 2 - j, 0), cb))
    return pl.pallas_call(
        body, name=name, grid=(nb,),
        in_specs=[pl.BlockSpec((WINDOW, 512), cur(P_QA // 512)), pl.BlockSpec((WINDOW, 512), cur(P_ZA // 512)),
                  pl.BlockSpec((WINDOW, 128), cur(P_KA // 128)), pl.BlockSpec((WINDOW, 128), cur(P_VA // 128)),
                  pl.BlockSpec((WINDOW, 128), prev(P_KA // 128)), pl.BlockSpec((WINDOW, 128), prev(P_VA // 128)),
                  _const_spec((1, 128)), _const_spec((1, 128)), _const_spec((1, 128)),
                  pl.BlockSpec((WINDOW, 512), cur(0))],
        out_specs=[pl.BlockSpec((WINDOW, 1024), cur(0)), pl.BlockSpec((WINDOW, 256), cur(0)),
                   _const_spec((1, 128)), _const_spec((1, 128)), _const_spec((1, 128))],
        out_shape=[jax.ShapeDtypeStruct((t, 1024), BF16), jax.ShapeDtypeStruct((t, 256), BF16),
                   jax.ShapeDtypeStruct((1, 128), F32), jax.ShapeDtypeStruct((1, 128), F32),
                   jax.ShapeDtypeStruct((1, 128), F32)],
        scratch_shapes=[pltpu.VMEM((2, WINDOW, 128), F32)],
        compiler_params=_cparams(1),
    )(proj, proj, proj, proj, proj, proj, qg, kg, sinks, dya)


CONV_ROWS = 256


def _conv_taps(src_ref, w_ref, n_taps, base, t):
    for r0 in range(0, t, CONV_ROWS):
        acc = w_ref[0:1, :] * src_ref[pl.ds(r0 + base, CONV_ROWS), :]
        for k in range(1, n_taps):
            acc = acc + w_ref[k:k + 1, :] * src_ref[pl.ds(r0 + base + k, CONV_ROWS), :]
        yield r0, acc


def _conv_wgrad(dy_ref, src_ref, n_taps, base, t, dy_base=0):
    out = []
    for k in range(n_taps):
        acc = jnp.zeros((8, 128), F32)
        for r0 in range(0, t, CONV_ROWS):
            prod = dy_ref[pl.ds(r0 + dy_base, CONV_ROWS), :] * src_ref[pl.ds(r0 + base + k, CONV_ROWS), :]
            acc = acc + jnp.sum(prod.reshape(CONV_ROWS // 8, 8, 128), axis=0)
        out.append(jnp.sum(acc, axis=0, keepdims=True))
    return out


def glu_conv_fwd(name, proj, w32, bias):
    t = proj.shape[0]
    pad = 32

    def body(x_ref, w_ref, b_ref, o_ref, u_ref):
        u_ref[0:pad, :] = jnp.zeros((pad, 128), F32)
        u_ref[pad:pad + t, :] = x_ref[:, 0:128] * _sigmoid(x_ref[:, 128:256])
        for r0, acc in _conv_taps(u_ref, w_ref, CONV_K, pad - (CONV_K - 1), t):
            o_ref[pl.ds(r0, CONV_ROWS), :] = acc + b_ref[...]

    return pl.pallas_call(
        body, name=name, grid=(4,),
        in_specs=[pl.BlockSpec((t, 256), lambda cb: (0, P_GLU // 256 + cb)), pl.BlockSpec((32, 128), lambda cb: (0, cb)),
                  pl.BlockSpec((1, 128), lambda cb: (0, cb))],
        out_specs=pl.BlockSpec((t, 128), lambda cb: (0, cb)),
        out_shape=jax.ShapeDtypeStruct((t, 512), F32),
        scratch_shapes=[pltpu.VMEM((t + pad, 128), F32)],
        compiler_params=_cparams(1),
    )(proj, w32, bias)


def glu_conv_bwd(name, proj, w32, dub):
    t = proj.shape[0]
    pad = 32
    k1 = CONV_K - 1

    def body(x_ref, w_ref, dy_ref, dx_ref, dw_ref, db_ref, u_ref, dyp_ref, wrev_ref):
        val = x_ref[:, 0:128]
        sg = _sigmoid(x_ref[:, 128:256])
        u_ref[0:pad, :] = jnp.zeros((pad, 128), F32)
        u_ref[pad:pad + t, :] = val * sg
        dyp_ref[0:t, :] = dy_ref[...]
        dyp_ref[t:t + pad, :] = jnp.zeros((pad, 128), F32)
        for k in range(CONV_K):
            wrev_ref[k:k + 1, :] = w_ref[k1 - k:k1 - k + 1, :]
        wrev_ref[CONV_K:32, :] = jnp.zeros((32 - CONV_K, 128), F32)
        for r0, du in _conv_taps(dyp_ref, wrev_ref, CONV_K, 0, t):
            v = x_ref[pl.ds(r0, CONV_ROWS), 0:128]
            s = _sigmoid(x_ref[pl.ds(r0, CONV_ROWS), 128:256])
            dx_ref[pl.ds(r0, CONV_ROWS), 0:128] = (du * s).astype(dx_ref.dtype)
            dx_ref[pl.ds(r0, CONV_ROWS), 128:256] = (du * v * s * (1.0 - s)).astype(dx_ref.dtype)
        dws = _conv_wgrad(dyp_ref, u_ref, CONV_K, pad - k1, t)
        for k in range(CONV_K):
            dw_ref[k:k + 1, :] = dws[k]
        dw_ref[CONV_K:32, :] = jnp.zeros((32 - CONV_K, 128), F32)
        db_ref[...] = jnp.sum(dy_ref[...], axis=0, keepdims=True)

    return pl.pallas_call(
        body, name=name, grid=(4,),
        in_specs=[pl.BlockSpec((t, 256), lambda cb: (0, P_GLU // 256 + cb)), pl.BlockSpec((32, 128), lambda cb: (0, cb)),
                  pl.BlockSpec((t, 128), lambda cb: (0, cb))],
        out_specs=[pl.BlockSpec((t, 256), lambda cb: (0, cb)), pl.BlockSpec((32, 128), lambda cb: (0, cb)),
                   pl.BlockSpec((1, 128), lambda cb: (0, cb))],
        out_shape=[jax.ShapeDtypeStruct((t, 1024), BF16), jax.ShapeDtypeStruct((32, 512), F32),
                   jax.ShapeDtypeStruct((1, 512), F32)],
        scratch_shapes=[pltpu.VMEM((t + pad, 128), F32), pltpu.VMEM((t + pad, 128), F32), pltpu.VMEM((32, 128), F32)],
        compiler_params=_cparams(1),
    )(proj, w32, dub)


def sconv_fwd(name, proj, w8):
    t = proj.shape[0]
    pad = 8
    k1 = DN_CONV_K - 1

    def body(x_ref, w_ref, o_ref, xp_ref):
        xp_ref[0:pad, :] = jnp.zeros((pad, 128), F32)
        xp_ref[pad:pad + t, :] = x_ref[...]
        for r0, acc in _conv_taps(xp_ref, w_ref, DN_CONV_K, pad - k1, t):
            o_ref[pl.ds(r0, CONV_ROWS), :] = _silu(acc)

    return pl.pallas_call(
        body, name=name, grid=(12,),
        in_specs=[pl.BlockSpec((t, 128), lambda cb: (0, P_QKV // 128 + cb)), pl.BlockSpec((8, 128), lambda cb: (0, cb))],
        out_specs=pl.BlockSpec((t, 128), lambda cb: (0, cb)),
        out_shape=jax.ShapeDtypeStruct((t, 1536), F32),
        scratch_shapes=[pltpu.VMEM((t + pad, 128), F32)],
        compiler_params=_cparams(1),
    )(proj, w8)


def sconv_bwd(name, proj, w8, dqkv):
    t = proj.shape[0]
    pad = 8
    k1 = DN_CONV_K - 1

    def body(x_ref, w_ref, dy_ref, dx_ref, dw_ref, xp_ref, dpp_ref, wrev_ref):
        xp_ref[0:pad, :] = jnp.zeros((pad, 128), F32)
        xp_ref[pad:pad + t, :] = x_ref[...]
        for r0, pre in _conv_taps(xp_ref, w_ref, DN_CONV_K, pad - k1, t):
            s = _sigmoid(pre)
            dpp_ref[pl.ds(r0, CONV_ROWS), :] = dy_ref[pl.ds(r0, CONV_ROWS), :] * (s * (1.0 + pre * (1.0 - s)))
        dpp_ref[t:t + pad, :] = jnp.zeros((pad, 128), F32)
        for k in range(DN_CONV_K):
            wrev_ref[k:k + 1, :] = w_ref[k1 - k:k1 - k + 1, :]
        wrev_ref[DN_CONV_K:8, :] = jnp.zeros((8 - DN_CONV_K, 128), F32)
        for r0, dx in _conv_taps(dpp_ref, wrev_ref, DN_CONV_K, 0, t):
            dx_ref[pl.ds(r0, CONV_ROWS), :] = dx.astype(dx_ref.dtype)
        dws = _conv_wgrad(dpp_ref, xp_ref, DN_CONV_K, pad - k1, t)
        for k in range(DN_CONV_K):
            dw_ref[k:k + 1, :] = dws[k]
        dw_ref[DN_CONV_K:8, :] = jnp.zeros((8 - DN_CONV_K, 128), F32)

    return pl.pallas_call(
        body, name=name, grid=(12,),
        in_specs=[pl.BlockSpec((t, 128), lambda cb: (0, P_QKV // 128 + cb)), pl.BlockSpec((8, 128), lambda cb: (0, cb)),
                  pl.BlockSpec((t, 128), lambda cb: (0, cb))],
        out_specs=[pl.BlockSpec((t, 128), lambda cb: (0, cb)), pl.BlockSpec((8, 128), lambda cb: (0, cb))],
        out_shape=[jax.ShapeDtypeStruct((t, 1536), BF16), jax.ShapeDtypeStruct((8, 1536), F32)],
        scratch_shapes=[pltpu.VMEM((t + pad, 128), F32), pltpu.VMEM((t + pad, 128), F32), pltpu.VMEM((8, 128), F32)],
        compiler_params=_cparams(1),
    )(proj, w8, dqkv)


def _f_delta(qkv, ab, zc, s0, s1, s2, s3, a_log, dt_bias, dn_g):
    cs = DN_CHUNK
    n = 2 * cs
    states = (s0, s1, s2, s3)
    lane = lax.broadcasted_iota(jnp.int32, (1, 128), 1)
    ri = lax.broadcasted_iota(jnp.int32, (n, n), 0)
    ci = lax.broadcasted_iota(jnp.int32, (n, n), 1)
    same = (ri // cs) == (ci // cs)
    lower = same & (ri >= ci)
    strict = same & (ri > ci)
    lower_f = jnp.where(lower, 1.0, 0.0)
    upper_f = jnp.where(same & (ri <= ci), 1.0, 0.0)
    same_f = jnp.where(same, 1.0, 0.0)
    eye = jnp.where(ri == ci, 1.0, 0.0)
    sum_top = jnp.where(ci < cs, 1.0, 0.0)
    sum_bot = jnp.where(ci >= cs, 1.0, 0.0)
    top = lax.broadcasted_iota(jnp.int32, (n, 1), 0) < cs

    def pick(row, idx):
        return jnp.sum(jnp.where(lane == idx, row, 0.0), axis=-1, keepdims=True)

    def l2n(x):
        return x * lax.rsqrt(jnp.sum(x * x, axis=-1, keepdims=True) + EPS)

    ys, new_states = [], []
    for pair in range(2):
        hs = (2 * pair, 2 * pair + 1)
        stack = lambda f: jnp.concatenate([f(hs[0]), f(hs[1])], axis=0)
        qd = l2n(stack(lambda h: qkv[:, 128 * h:128 * h + 128])) * (128 ** -0.5)
        kd = l2n(stack(lambda h: qkv[:, 512 + 128 * h:512 + 128 * h + 128]))
        vd = stack(lambda h: qkv[:, 1024 + 128 * h:1024 + 128 * h + 128])
        beta = _sigmoid(stack(lambda h: pick(ab, 4 + h)))
        g = stack(lambda h: -jnp.exp(pick(a_log, h)) * _softplus(pick(ab, h) + pick(dt_bias, h)))
        g_b = g * jnp.ones((1, n), F32)
        gc_col = xmm(lower_f, g_b)
        gc_row = xmm_tn(g_b, upper_f)
        gl_b = xmm(same_f, g_b)
        decay = jnp.where(lower, jnp.exp(jnp.where(lower, gc_col - gc_row, 0.0)), 0.0)
        kb = kd * beta
        vb = vd * beta
        a = jnp.where(strict, mm_nt(kb, kd) * decay, 0.0)
        tmat = eye - a
        pw = a
        for _ in range(5):
            pw = xmm(pw, pw)
            tmat = tmat + xmm(tmat, pw)
        egc = jnp.exp(gc_col)
        u = mm(tmat, vb)
        wm = mm(tmat, kb * egc)
        intra = jnp.where(lower, mm_nt(qd, kd) * decay, 0.0)
        qe = qd * egc
        ke = kd * jnp.exp(gl_b - gc_col)
        st = (states[hs[0]], states[hs[1]])
        v_new = u - jnp.concatenate([mm(wm[:cs], st[0]), mm(wm[cs:], st[1])], axis=0)
        o = jnp.concatenate([mm(qe[:cs], st[0]), mm(qe[cs:], st[1])], axis=0) + mm(intra, v_new)
        g_end = [xmm(sum_top, g_b), xmm(sum_bot, g_b)]
        new_states.append(st[0] * jnp.exp(g_end[0]) + mm_tn(jnp.where(top, ke, 0.0), v_new))
        new_states.append(st[1] * jnp.exp(g_end[1]) + mm_tn(jnp.where(top, 0.0, ke), v_new))
        od = o * lax.rsqrt(jnp.mean(o * o, axis=-1, keepdims=True) + EPS) * dn_g
        ys += [od[:cs] * _silu(zc[:, 128 * hs[0]:128 * hs[0] + 128]), od[cs:] * _silu(zc[:, 128 * hs[1]:128 * hs[1] + 128])]
    y = jnp.concatenate([ys[0], ys[1], ys[2], ys[3]], axis=1)
    return (y, *new_states)


def delta_fwd(name, qkv, proj, a_log, dt_bias, dn_g):
    t = qkv.shape[0]
    nc = t // DN_CHUNK

    def body(qkv_ref, ab_ref, zc_ref, al_ref, dt_ref, g_ref, y_ref, ssave_ref, s_ref):
        @pl.when(pl.program_id(0) == 0)
        def _():
            s_ref[...] = jnp.zeros_like(s_ref)

        ssave_ref[0] = s_ref[...]
        st = [s_ref[128 * h:128 * h + 128, :] for h in range(4)]
        y, *ns = _f_delta(qkv_ref[...], ab_ref[...], zc_ref[...], *st, al_ref[...], dt_ref[...], g_ref[...])
        y_ref[...] = y
        for h in range(4):
            s_ref[128 * h:128 * h + 128, :] = ns[h]

    return pl.pallas_call(
        body, name=name, grid=(nc,),
        in_specs=[pl.BlockSpec((DN_CHUNK, 1536), lambda i: (i, 0)), pl.BlockSpec((DN_CHUNK, 128), lambda i: (i, P_AB // 128)),
                  pl.BlockSpec((DN_CHUNK, 512), lambda i: (i, P_ZC // 512)),
                  _const_spec((1, 128)), _const_spec((1, 128)), _const_spec((1, 128))],
        out_specs=[pl.BlockSpec((DN_CHUNK, 512), lambda i: (i, 0)), pl.BlockSpec((1, 512, 128), lambda i: (i, 0, 0))],
        out_shape=[jax.ShapeDtypeStruct((t, 512), F32), jax.ShapeDtypeStruct((nc, 512, 128), F32)],
        scratch_shapes=[pltpu.VMEM((512, 128), F32)],
        compiler_params=_cparams(1),
    )(qkv, proj, proj, a_log, dt_bias, dn_g)


def delta_bwd(name, qkv, proj, ssave, a_log, dt_bias, dn_g, dyc):
    t = qkv.shape[0]
    nc = t // DN_CHUNK

    def body(qkv_ref, ab_ref, zc_ref, ss_ref, al_ref, dt_ref, g_ref, dy_ref,
             dqkv_ref, dab_ref, dzc_ref, dal_ref, ddt_ref, dg_ref, ds_ref):
        @pl.when(pl.program_id(0) == 0)
        def _():
            ds_ref[...] = jnp.zeros_like(ds_ref)
            dal_ref[...] = jnp.zeros_like(dal_ref)
            ddt_ref[...] = jnp.zeros_like(ddt_ref)
            dg_ref[...] = jnp.zeros_like(dg_ref)

        st = [ss_ref[0, 128 * h:128 * h + 128, :] for h in range(4)]
        _, vjp = jax.vjp(_f_delta, qkv_ref[...], ab_ref[...], zc_ref[...], *st, al_ref[...], dt_ref[...], g_ref[...])
        dst = tuple(ds_ref[128 * h:128 * h + 128, :] for h in range(4))
        dqkv, dab, dzc, d0, d1, d2, d3, dal, ddt, dg = vjp((dy_ref[...], *dst))
        dqkv_ref[...] = dqkv
        dab_ref[...] = dab.astype(dab_ref.dtype)
        dzc_ref[...] = dzc.astype(dzc_ref.dtype)
        for h, d in enumerate((d0, d1, d2, d3)):
            ds_ref[128 * h:128 * h + 128, :] = d
        dal_ref[...] += dal
        ddt_ref[...] += ddt
        dg_ref[...] += dg

    rev = lambda cb: (lambda j: (nc - 1 - j, cb))
    return pl.pallas_call(
        body, name=name, grid=(nc,),
        in_specs=[pl.BlockSpec((DN_CHUNK, 1536), rev(0)), pl.BlockSpec((DN_CHUNK, 128), rev(P_AB // 128)),
                  pl.BlockSpec((DN_CHUNK, 512), rev(P_ZC // 512)), pl.BlockSpec((1, 512, 128), lambda j: (nc - 1 - j, 0, 0)),
                  _const_spec((1, 128)), _const_spec((1, 128)), _const_spec((1, 128)),
                  pl.BlockSpec((DN_CHUNK, 512), rev(0))],
        out_specs=[pl.BlockSpec((DN_CHUNK, 1536), rev(0)), pl.BlockSpec((DN_CHUNK, 128), rev(0)),
                   pl.BlockSpec((DN_CHUNK, 512), rev(0)),
                   _const_spec((1, 128)), _const_spec((1, 128)), _const_spec((1, 128))],
        out_shape=[jax.ShapeDtypeStruct((t, 1536), F32), jax.ShapeDtypeStruct((t, 128), BF16),
                   jax.ShapeDtypeStruct((t, 512), BF16),
                   jax.ShapeDtypeStruct((1, 128), F32), jax.ShapeDtypeStruct((1, 128), F32), jax.ShapeDtypeStruct((1, 128), F32)],
        scratch_shapes=[pltpu.VMEM((512, 128), F32)],
        compiler_params=_cparams(1),
    )(qkv, proj, proj, ssave, a_log, dt_bias, dn_g, dyc)


def loss_head(name, y, target, tm):
    t, d = y.shape

    def body(y_ref, t_ref, dy_ref, l_ref):
        err = y_ref[...] - t_ref[...]
        dy_ref[...] = err * (1.0 / d)
        part = 0.5 * jnp.sum(jnp.sum(err * err, axis=-1, keepdims=True) * (1.0 / d), axis=0, keepdims=True)

        @pl.when(pl.program_id(0) == 0)
        def _():
            l_ref[...] = part

        @pl.when(pl.program_id(0) > 0)
        def _():
            l_ref[...] += part

    return pl.pallas_call(
        body, name=name, grid=(t // tm,),
        in_specs=[_row_spec(tm, d, 0), _row_spec(tm, d, 0)],
        out_specs=[_row_spec(tm, d, 0), _const_spec((1, 1))],
        out_shape=[jax.ShapeDtypeStruct((t, d), F32), jax.ShapeDtypeStruct((1, 1), F32)],
        compiler_params=_cparams(1),
    )(y, target)


TM = 512
TM_MERGE = 256
TN_IN = 1152


def _lane_pad(v, n=128):
    return jnp.pad(v.astype(F32), (0, n - v.shape[0]))[None, :]


def f_norm_mod_res(x, g, scale, shift):
    return f_norm_mod(x, g, scale, shift), x


def prep_layer(w):
    p = dict(w)
    p["wp"] = _pad_w_in_from_shards(w["w_in"])
    p["wpt"] = p["wp"].T
    p["wpa"] = _perm_heads_rows(w["w_proj_a"])
    p["dw32"] = jnp.pad(w["dw_w"], ((0, 32 - CONV_K), (0, 0)))
    p["sconv8"] = jnp.pad(w["sconv_w"], ((0, 8 - DN_CONV_K), (0, 0)))
    p["qg"] = jnp.tile(w["q_norm_g"], 2)[None, :]
    p["kg"] = jnp.tile(w["k_norm_g"], 2)[None, :]
    p["sinks128"] = _lane_pad(w["sinks"])
    p["al"] = _lane_pad(w["a_log"])
    p["dtb"] = _lane_pad(w["dt_bias"])
    p["dng"] = w["dn_norm_g"][None, :]
    return p


def layer_fwd(tag, x, c8, p):
    mod = ada_fwd(f"ada_fwd{tag}", c8, p["w_ada"], p["b_ada"][None, :])[0:1]
    d = D_MODEL
    shift, scale, gate = mod[:, :d], mod[:, d:2 * d], mod[:, 2 * d:]
    g = p["norm_g"][None, :]
    (h,) = rowwise_fwd(f"norm_fwd{tag}", f_norm_mod, [(x, d, 0)], [g, scale, shift], [(d, BF16)], TM)
    proj = matmul_nn(f"inproj_fwd{tag}", h, p["wp"], F32, TM, TN_IN, d)
    ya = attn_fwd(f"attn_fwd{tag}", proj, p["qg"], p["kg"], p["sinks128"])
    ub = glu_conv_fwd(f"glu_conv_fwd{tag}", proj, p["dw32"], p["dw_b"][None, :])
    conf_consts = [p["ln_g"][None, :], p["ln_b"][None, :], p["pw2_w"], p["pw2_b"][None, :]]
    (yb,) = rowwise_fwd(f"conf_fwd{tag}", f_conf_tail, [(ub, 512, 0), (proj, 512, P_ZB // 512)], conf_consts, [(512, F32)], TM)
    qkv = sconv_fwd(f"sconv_fwd{tag}", proj, p["sconv8"])
    yc, ssave = delta_fwd(f"delta_fwd{tag}", qkv, proj, p["al"], p["dtb"], p["dng"])
    merge_consts = [gate, p["wpa"], p["w_proj_b"], p["w_proj_c"], p["w_out"]]
    merge_rows = [(ya, 512, 0), (yb, 512, 0), (yc, 512, 0), (proj, 3 * d, P_MG // (3 * d)), (x, d, 0)]
    (xn,) = rowwise_fwd(f"merge_fwd{tag}", f_merge, merge_rows, merge_consts, [(d, F32)], TM_MERGE)
    saved = dict(x=x, h=h, proj=proj, ub=ub, qkv=qkv, ssave=ssave, norm_consts=[g, scale, shift],
                 conf_consts=conf_consts, merge_consts=merge_consts, merge_rows=merge_rows)
    return xn, saved


def layer_bwd(tag, dxn, c8, p, s):
    d = D_MODEL
    proj = s["proj"]
    dya, dyb, dyc, dmg, dgate, dwpa, dwpb, dwpc, dwout = rowwise_bwd(
        f"merge_bwd{tag}", f_merge, s["merge_rows"], s["merge_consts"], [(dxn, d, 0)], [F32, F32, F32, BF16, None], TM_MERGE)
    dqz, dkv, dqg, dkg, dsinks = attn_bwd(f"attn_bwd{tag}", proj, p["qg"], p["kg"], p["sinks128"], dya)
    dub, dzb, dln_g, dln_b, dpw2_w, dpw2_b = rowwise_bwd(
        f"conf_bwd{tag}", f_conf_tail, [(s["ub"], 512, 0), (proj, 512, P_ZB // 512)], s["conf_consts"], [(dyb, 512, 0)],
        [F32, BF16], TM)
    dglu, ddw32, ddw_b = glu_conv_bwd(f"glu_conv_bwd{tag}", proj, p["dw32"], dub)
    dqkv, dab, dzc, dal, ddtb, ddng = delta_bwd(f"delta_bwd{tag}", s["qkv"], proj, s["ssave"], p["al"], p["dtb"], p["dng"], dyc)
    dqkv_pre, dsconv8 = sconv_bwd(f"sconv_bwd{tag}", proj, p["sconv8"], dqkv)
    dproj = jnp.concatenate([dqz, dglu, dzb, dzc, dmg, dqkv_pre, dkv, dab], axis=1)
    dh = matmul_nn(f"inproj_bwd_dh{tag}", dproj, p["wpt"], F32, TM, d, TN_IN)
    dwp = matmul_tn(f"inproj_bwd_dw{tag}", s["h"], dproj, 512, TN_IN, TM)
    dx, dnorm_g, dscale, dshift = rowwise_bwd(
        f"norm_bwd{tag}", f_norm_mod_res, [(s["x"], d, 0)], s["norm_consts"], [(dh, d, 0), (dxn, d, 0)], [F32], TM)
    dmod = jnp.concatenate([dshift, dscale, dgate], axis=1)
    dw_ada = ada_bwd(f"ada_bwd{tag}", c8, jnp.pad(dmod, ((0, 7), (0, 0))))
    grads = dict(
        w_ada=dw_ada, b_ada=dmod[0], norm_g=dnorm_g[0], w_in=_unpad_w_in_to_shards(dwp),
        q_norm_g=dqg[0, :64] + dqg[0, 64:], k_norm_g=dkg[0, :64] + dkg[0, 64:], sinks=dsinks[0, :ATT_HEADS],
        dw_w=ddw32[:CONV_K], dw_b=ddw_b[0], ln_g=dln_g[0], ln_b=dln_b[0], pw2_w=dpw2_w, pw2_b=dpw2_b[0],
        sconv_w=dsconv8[:DN_CONV_K], a_log=dal[0, :DN_HEADS], dt_bias=ddtb[0, :DN_HEADS], dn_norm_g=ddng[0],
        w_proj_a=_unperm_heads_rows(dwpa), w_proj_b=dwpb, w_proj_c=dwpc, w_out=dwout)
    return dx, grads


SHARDED = {"w_ada": 2, "w_in": 2, "dw_w": 2, "pw2_w": 1, "sconv_w": 2, "w_proj_a": 2, "w_proj_b": 2, "w_proj_c": 2,
           "w_out": 1}
GATHER_F32 = ("dw_w", "sconv_w")
REDUCE_BIG = tuple(n for n in SHARDED if n not in GATHER_F32)
SMALL = ("b_ada", "norm_g", "q_norm_g", "k_norm_g", "sinks", "dw_b", "ln_g", "ln_b", "pw2_b", "a_log", "dt_bias",
         "dn_norm_g")
SMALL_ROWS = 104
SMALL_GRAD_ROWS = 448
W_IN_SHARD = D_IN // N_CHIPS
SUM_TILE = 256


def _shard_cols(shards, start, n):
    parts = []
    while n > 0:
        k, o = divmod(start, W_IN_SHARD)
        m = min(n, W_IN_SHARD - o)
        parts.append(shards[k][:, o:o + m])
        start, n = start + m, n - m
    return parts


def _pad_w_in_from_shards(shards):
    parts = []
    for s, n in _in_pieces():
        parts += _shard_cols(shards, s, n)
    parts.append(jnp.zeros((shards.shape[1], P_TOTAL - D_IN), shards.dtype))
    return jnp.concatenate(parts, axis=1)


def _unpad_w_in_to_shards(wp):
    pieces = _in_pieces()
    starts = np.cumsum([0] + [n for _, n in pieces])[:-1]
    order = sorted(range(len(pieces)), key=lambda i: pieces[i][0])
    shards = []
    for k in range(N_CHIPS):
        lo, hi = k * W_IN_SHARD, (k + 1) * W_IN_SHARD
        parts = []
        for i in order:
            s, n = pieces[i]
            a, b = max(s, lo), min(s + n, hi)
            if a < b:
                parts.append(wp[:, int(starts[i]) + a - s:int(starts[i]) + b - s])
        shards.append(jnp.concatenate(parts, axis=1))
    return jnp.stack(shards)


def _join_layer(v, axis):
    if axis == 2:
        return jnp.transpose(v, (1, 0, 2)).reshape(v.shape[1], N_CHIPS * v.shape[2])
    return v.reshape(N_CHIPS * v.shape[1], v.shape[2])


def _split_layer(v, axis):
    a, b = v.shape
    if axis == 2:
        return jnp.transpose(v.reshape(a, N_CHIPS, b // N_CHIPS), (1, 0, 2))
    return v.reshape(N_CHIPS, a // N_CHIPS, b)


def pack_small(vals, names, rows):
    flat = jnp.concatenate([vals[n].astype(F32).reshape(-1) for n in names])
    return jnp.pad(flat, (0, rows * 128 - flat.shape[0])).reshape(rows, 128)


def unpack_small(packed, names, shapes):
    flat = packed.reshape(-1)
    out, off = {}, 0
    for n in names:
        k = int(np.prod(shapes[n]))
        out[n] = flat[off:off + k].reshape(shapes[n])
        off += k
    return out


ANY = pl.BlockSpec(memory_space=pl.ANY)


def _place():
    x, y, c = lax.axis_index("x"), lax.axis_index("y"), lax.axis_index("c")
    chips = [(1 - x, y), (x, 1 - y), (1 - x, 1 - y)]
    return x, y, c, chips


def _remote(src, dst, send_sem, recv_sem, to):
    return pltpu.make_async_remote_copy(src_ref=src, dst_ref=dst, send_sem=send_sem, recv_sem=recv_sem, device_id=to,
                                        device_id_type=MESH)


def weights_allgather(shards):
    n = len(shards)

    def body(*refs):
        src, out = refs[:n], refs[n:2 * n]
        send_sems, recv_sems, local_sems = refs[2 * n:]
        x, y, c, chips = _place()
        me, sibling, my_slot = (x, y, c), (x, y, 1 - c), 2 * x + y
        local = [pltpu.make_async_copy(src[t], out[t].at[my_slot], local_sems.at[t]) for t in range(n)]
        for cp in local:
            cp.start()
        sends = []
        for j, chip in enumerate(chips):
            for t in range(n):
                sends.append(_remote(src[t].at[c], out[t].at[my_slot, c], send_sems.at[t, j], recv_sems.at[t, j], (*chip, c)))
                sends[-1].start()
        for j, chip in enumerate(chips):
            for t in range(n):
                land = out[t].at[2 * chip[0] + chip[1], c]
                _remote(land, land, send_sems.at[t, j], recv_sems.at[t, j], me).wait_recv()
                sends.append(_remote(land, land, send_sems.at[t, 3 + j], recv_sems.at[t, 3 + j], sibling))
                sends[-1].start()
        for j, chip in enumerate(chips):
            for t in range(n):
                land = out[t].at[2 * chip[0] + chip[1], 1 - c]
                _remote(land, land, send_sems.at[t, 3 + j], recv_sems.at[t, 3 + j], me).wait_recv()
        for cp in sends:
            cp.wait_send()
        for cp in local:
            cp.wait()

    return pl.pallas_call(
        body, name="weights_allgather", out_shape=[jax.ShapeDtypeStruct((N_CHIPS,) + s.shape, s.dtype) for s in shards],
        in_specs=[ANY] * n, out_specs=[ANY] * n,
        scratch_shapes=[pltpu.SemaphoreType.DMA((n, 6)), pltpu.SemaphoreType.DMA((n, 6)), pltpu.SemaphoreType.DMA((n,))],
    )(*shards)


def grads_pair_exchange(gs):
    n = len(gs)

    def body(*refs):
        g, recv = refs[:n], refs[n:2 * n]
        send_sems, recv_sems = refs[2 * n:]
        x, y, c, _ = _place()
        cps = [_remote(g[t].at[:, 1 - c], recv[t], send_sems.at[t], recv_sems.at[t], (x, y, 1 - c)) for t in range(n)]
        for cp in cps:
            cp.start()
        for cp in cps:
            cp.wait()

    return pl.pallas_call(
        body, name="grads_pair_exchange",
        out_shape=[jax.ShapeDtypeStruct((N_CHIPS,) + g.shape[2:], g.dtype) for g in gs],
        in_specs=[ANY] * n, out_specs=[ANY] * n,
        scratch_shapes=[pltpu.SemaphoreType.DMA((n,)), pltpu.SemaphoreType.DMA((n,))],
    )(*gs)


def grads_pair_sum(name, g, recv):
    _, a, b = recv.shape
    ta = min(a, SUM_TILE)

    def body(a_ref, b_ref, o_ref):
        o_ref[...] = a_ref[...] + b_ref[...]

    return pl.pallas_call(
        body, name=name, grid=(N_CHIPS, a // ta),
        in_specs=[pl.BlockSpec((None, None, ta, b), lambda s, i: (s, lax.axis_index("c"), i, 0)),
                  pl.BlockSpec((None, ta, b), lambda s, i: (s, i, 0))],
        out_specs=pl.BlockSpec((None, ta, b), lambda s, i: (s, i, 0)),
        out_shape=jax.ShapeDtypeStruct(recv.shape, F32),
        compiler_params=_cparams(2),
    )(g, recv)


def grads_chip_exchange(ps):
    n = len(ps)

    def body(*refs):
        p, recv = refs[:n], refs[n:2 * n]
        send_sems, recv_sems = refs[2 * n:]
        x, y, c, chips = _place()
        cps = [_remote(p[t].at[2 * chip[0] + chip[1]], recv[t].at[j], send_sems.at[t, j], recv_sems.at[t, j], (*chip, c))
               for j, chip in enumerate(chips) for t in range(n)]
        for cp in cps:
            cp.start()
        for cp in cps:
            cp.wait()

    return pl.pallas_call(
        body, name="grads_chip_exchange", out_shape=[jax.ShapeDtypeStruct((3,) + p.shape[1:], p.dtype) for p in ps],
        in_specs=[ANY] * n, out_specs=[ANY] * n,
        scratch_shapes=[pltpu.SemaphoreType.DMA((n, 3)), pltpu.SemaphoreType.DMA((n, 3))],
    )(*ps)


def grads_chip_sum(name, p, recv):
    _, a, b = p.shape
    ta = min(a, SUM_TILE)

    def body(a_ref, b_ref, o_ref):
        o_ref[...] = ((a_ref[...] + b_ref[0]) + b_ref[1]) + b_ref[2]

    return pl.pallas_call(
        body, name=name, grid=(a // ta,),
        in_specs=[pl.BlockSpec((None, ta, b), lambda i: (2 * lax.axis_index("x") + lax.axis_index("y"), i, 0)),
                  pl.BlockSpec((3, ta, b), lambda i: (0, i, 0))],
        out_specs=pl.BlockSpec((ta, b), lambda i: (i, 0)),
        out_shape=jax.ShapeDtypeStruct((a, b), F32),
        compiler_params=_cparams(1),
    )(p, recv)


def grads_pair_gather(reds):
    n = len(reds)

    def body(*refs):
        red, out = refs[:n], refs[n:2 * n]
        send_sems, recv_sems, local_sems = refs[2 * n:]
        x, y, c, _ = _place()
        sibling = (x, y, 1 - c)
        local = [pltpu.make_async_copy(red[t], out[t].at[c], local_sems.at[t]) for t in range(n)]
        cps = [_remote(red[t], out[t].at[c], send_sems.at[t], recv_sems.at[t], sibling) for t in range(n)]
        for cp in local + cps:
            cp.start()
        for t in range(n):
            _remote(red[t], out[t].at[1 - c], send_sems.at[t], recv_sems.at[t], sibling).wait_recv()
        for cp in cps:
            cp.wait_send()
        for cp in local:
            cp.wait()

    return pl.pallas_call(
        body, name="grads_pair_gather", out_shape=[jax.ShapeDtypeStruct((DEPTH,) + r.shape, r.dtype) for r in reds],
        in_specs=[ANY] * n, out_specs=[ANY] * n,
        scratch_shapes=[pltpu.SemaphoreType.DMA((n,)), pltpu.SemaphoreType.DMA((n,)), pltpu.SemaphoreType.DMA((n,))],
    )(*reds)


def small_allreduce(v):
    m, n = v.shape

    def body(x_ref, sum_ref, all_ref, send_sems, recv_sems, local_sem):
        x, y, c, chips = _place()
        me, sibling = (x, y, c), (x, y, 1 - c)

        def rows(px, py, pc):
            return all_ref.at[pl.ds((4 * px + 2 * py + pc) * m, m), :]

        def copy(k, block, to, src=None):
            return pltpu.make_async_remote_copy(src_ref=rows(*block) if src is None else src, dst_ref=rows(*block),
                                                send_sem=send_sems.at[k], recv_sem=recv_sems.at[k],
                                                device_id=to, device_id_type=MESH)

        mine = pltpu.make_async_copy(x_ref, rows(*me), local_sem)
        mine.start()
        first = [copy(0, me, sibling, src=x_ref)]
        first += [copy(1 + j, me, (*chip, c), src=x_ref) for j, chip in enumerate(chips)]
        for cp in first:
            cp.start()
        passed = [copy(4 + j, (*chip, c), sibling) for j, chip in enumerate(chips)]
        for j, chip in enumerate(chips):
            copy(1 + j, (*chip, c), me).wait_recv()
            passed[j].start()
        copy(0, sibling, me).wait_recv()
        for j, chip in enumerate(chips):
            copy(4 + j, (*chip, 1 - c), me).wait_recv()
        for cp in first + passed:
            cp.wait_send()
        mine.wait()
        acc = all_ref[0:m, :]
        for dev in range(1, 8):
            acc = acc + all_ref[dev * m:(dev + 1) * m, :]
        sum_ref[...] = acc

    vm = pl.BlockSpec(memory_space=pltpu.VMEM)
    return pl.pallas_call(
        body, name="small_allreduce",
        out_shape=[jax.ShapeDtypeStruct((m, n), F32), jax.ShapeDtypeStruct((8 * m, n), F32)],
        in_specs=[vm], out_specs=[vm, vm],
        scratch_shapes=[pltpu.SemaphoreType.DMA((7,)), pltpu.SemaphoreType.DMA((7,)), pltpu.SemaphoreType.DMA],
    )(v)[0]


def reduce_scatter_grads(names, gs):
    recv = grads_pair_exchange(gs)
    parts = [grads_pair_sum("grads_pair_sum_" + n, g, r) for n, g, r in zip(names, gs, recv)]
    recv2 = grads_chip_exchange(parts)
    reds = [grads_chip_sum("grads_chip_sum_" + n, p, r) for n, p, r in zip(names, parts, recv2)]
    return grads_pair_gather(reds)


def adamw(name, w, g, m, v, tr):
    r, cols = w.shape

    def body(w_ref, g_ref, m_ref, v_ref, d_ref, nm_ref, nv_ref):
        gv = g_ref[...]
        nm = ADAM_B1 * m_ref[...] + (1.0 - ADAM_B1) * gv
        nv = ADAM_B2 * v_ref[...] + (1.0 - ADAM_B2) * (gv * gv)
        m_hat = nm / (1.0 - ADAM_B1 ** ADAM_STEP)
        v_hat = nv / (1.0 - ADAM_B2 ** ADAM_STEP)
        d_ref[...] = -ADAM_LR * (m_hat / (jnp.sqrt(v_hat) + ADAM_EPS) + ADAM_WD * w_ref[...])
        nm_ref[...] = nm
        nv_ref[...] = nv

    spec = pl.BlockSpec((tr, cols), lambda i: (i, 0))
    return pl.pallas_call(
        body, name=name, grid=(r // tr,), in_specs=[spec] * 4, out_specs=[spec] * 3,
        out_shape=[jax.ShapeDtypeStruct((r, cols), F32)] * 3, compiler_params=_cparams(1),
    )(w, g, m, v)


ADAM_ROWS = {"w_ada": 512, "w_in": 256, "dw_w": 62, "pw2_w": 256, "sconv_w": 8, "w_proj_a": 512, "w_proj_b": 512,
             "w_proj_c": 512, "w_out": 256}

WEIGHT_NAMES = ("w_ada", "b_ada", "norm_g", "w_in", "q_norm_g", "k_norm_g", "sinks", "dw_w", "dw_b", "ln_g", "ln_b",
                "pw2_w", "pw2_b", "sconv_w", "a_log", "dt_bias", "dn_norm_g", "w_proj_a", "w_proj_b", "w_proj_c", "w_out")


def kernel(x, c, w_ada, b_ada, norm_g, w_in, q_norm_g, k_norm_g, sinks, dw_w, dw_b, ln_g, ln_b, pw2_w, pw2_b, sconv_w, a_log, dt_bias, dn_norm_g, w_proj_a, w_proj_b, w_proj_c, w_out, loss_target, m_w_ada, m_b_ada, m_norm_g, m_w_in, m_q_norm_g, m_k_norm_g, m_sinks, m_dw_w, m_dw_b, m_ln_g, m_ln_b, m_pw2_w, m_pw2_b, m_sconv_w, m_a_log, m_dt_bias, m_dn_norm_g, m_w_proj_a, m_w_proj_b, m_w_proj_c, m_w_out, v_w_ada, v_b_ada, v_norm_g, v_w_in, v_q_norm_g, v_k_norm_g, v_sinks, v_dw_w, v_dw_b, v_ln_g, v_ln_b, v_pw2_w, v_pw2_b, v_sconv_w, v_a_log, v_dt_bias, v_dn_norm_g, v_w_proj_a, v_w_proj_b, v_w_proj_c, v_w_out):
    args = dict(locals())
    w = {n: args[n] for n in WEIGHT_NAMES}
    mom = {n: args["m_" + n] for n in WEIGHT_NAMES}
    var = {n: args["v_" + n] for n in WEIGHT_NAMES}

    gathered = weights_allgather([w[n] if n in GATHER_F32 else w[n].astype(BF16) for n in SHARDED])
    gathered = dict(zip(SHARDED, gathered))
    layers = []
    for l in range(DEPTH):
        lw = {n: w[n][l] for n in SMALL}
        for n, axis in SHARDED.items():
            lw[n] = gathered[n][:, l] if n == "w_in" else _join_layer(gathered[n][:, l], axis)
        layers.append(prep_layer(lw))

    c8 = jnp.tile(c, (8, 1))
    act, saved = x[0], []
    for l in range(DEPTH):
        act, s = layer_fwd(str(l), act, c8, layers[l])
        saved.append(s)
    dact, loss_part = loss_head("loss_head", act, loss_target[0], TM)
    loss = lax.psum(loss_part[0, 0], ("x", "y", "c"))
    layer_grads = [None] * DEPTH
    for l in reversed(range(DEPTH)):
        dact, layer_grads[l] = layer_bwd(str(l), dact, c8, layers[l], saved[l])

    by_chip = [jnp.stack([layer_grads[l][n] if n == "w_in" else _split_layer(layer_grads[l][n], SHARDED[n])
                          for l in range(DEPTH)], axis=1) for n in REDUCE_BIG]
    final_grads = dict(zip(REDUCE_BIG, reduce_scatter_grads(REDUCE_BIG, by_chip)))
    small_names = SMALL + GATHER_F32
    small_shapes = {n: (DEPTH,) + layer_grads[0][n].shape for n in small_names}
    small_full = {n: jnp.stack([layer_grads[l][n] for l in range(DEPTH)]) for n in small_names}
    small_sum = unpack_small(small_allreduce(pack_small(small_full, small_names, SMALL_GRAD_ROWS)), small_names, small_shapes)
    chip = 2 * lax.axis_index("x") + lax.axis_index("y")
    for n in GATHER_F32:
        width = w[n].shape[2]
        final_grads[n] = lax.dynamic_slice_in_dim(small_sum[n], chip * width, width, axis=2)
    final_grads.update({n: small_sum[n] for n in SMALL})
    small_grads = pack_small(final_grads, SMALL, SMALL_ROWS)

    delta, new_m, new_v = {}, {}, {}
    for n in SHARDED:
        shp = w[n].shape
        two_d = lambda a, shp=shp: a.reshape(shp[0] * shp[1], shp[2])
        d, nm, nv = adamw("adamw_" + n, two_d(w[n]), two_d(final_grads[n]), two_d(mom[n]), two_d(var[n]), ADAM_ROWS[n])
        delta[n], new_m[n], new_v[n] = d.reshape(shp), nm.reshape(shp), nv.reshape(shp)
    d, nm, nv = adamw("adamw_small", pack_small(w, SMALL, SMALL_ROWS), small_grads, pack_small(mom, SMALL, SMALL_ROWS),
                      pack_small(var, SMALL, SMALL_ROWS), SMALL_ROWS)
    delta.update(unpack_small(d, SMALL, small_shapes))
    new_m.update(unpack_small(nm, SMALL, small_shapes))
    new_v.update(unpack_small(nv, SMALL, small_shapes))

    return (loss, dact[None], *[final_grads[n] for n in WEIGHT_NAMES], *[delta[n] for n in WEIGHT_NAMES],
            *[new_m[n] for n in WEIGHT_NAMES], *[new_v[n] for n in WEIGHT_NAMES])
```

```python
import functools

import numpy as np
import jax
import jax.numpy as jnp
from jax import lax
from jax.experimental import pallas as pl
from jax.experimental.pallas import tpu as pltpu

F32 = jnp.float32
BF16 = jnp.bfloat16
MESH = pl.DeviceIdType.MESH

D_MODEL = 1024
DEPTH = 2
ATT_HEADS = 8
ATT_HEAD_DIM = 64
WINDOW = 128
CONV_K = 31
DN_HEADS = 4
DN_CONV_K = 4
DN_CHUNK = 64
EPS = 1e-6
NEG_INF = -1e30
N_CHIPS = 4
D_IN = 7944

ADAM_LR = 0.001
ADAM_B1 = 0.9
ADAM_B2 = 0.999
ADAM_EPS = 1e-08
ADAM_WD = 0.01
ADAM_STEP = 10

VMEM_LIMIT = 56 * 1024 * 1024

P_QA, P_ZA, P_GLU, P_ZB, P_ZC, P_MG, P_QKV, P_KA, P_VA, P_AB, P_TOTAL = (
    0, 512, 1024, 2048, 2560, 3072, 6144, 7680, 7808, 7936, 8064)
HEAD_ORDER = (0, 4, 1, 5, 2, 6, 3, 7)


def _in_pieces():
    p = [(0 + 64 * h, 64) for h in HEAD_ORDER]
    p += [(768 + 64 * h, 64) for h in HEAD_ORDER]
    for g in range(4):
        p += [(1280 + 128 * g, 128), (1792 + 128 * g, 128)]
    p += [(2304, 512), (4360, 512), (4872, 3072), (2816, 1536), (512, 128), (640, 128), (4352, 8)]
    return p


def _perm_heads_rows(w):
    return jnp.concatenate([w[64 * h:64 * h + 64] for h in HEAD_ORDER], axis=0)


def _unperm_heads_rows(w):
    inv = [HEAD_ORDER.index(h) for h in range(8)]
    return jnp.concatenate([w[64 * s:64 * s + 64] for s in inv], axis=0)


def _split_bf16(a, terms):
    out, rest = [], a.astype(F32)
    for _ in range(terms - 1):
        out.append(rest.astype(BF16))
        rest = rest - out[-1].astype(F32)
    return out + [rest.astype(BF16)]


def _dot(a, b, dims, exact):
    d = lambda p, q: lax.dot_general(p, q, (dims, ((), ())), preferred_element_type=F32)
    if exact:
        (ah, al), (bh, bl) = _split_bf16(a, 2), _split_bf16(b, 2)
        return d(ah, bh) + (d(ah, bl) + d(al, bh))
    return d(a.astype(BF16), b.astype(BF16))


def _make_mm(exact):
    @jax.custom_vjp
    def nn(a, b):
        return _dot(a, b, ((1,), (0,)), exact)

    @jax.custom_vjp
    def nt(a, b):
        return _dot(a, b, ((1,), (1,)), exact)

    @jax.custom_vjp
    def tn(a, b):
        return _dot(a, b, ((0,), (0,)), exact)

    nn.defvjp(lambda a, b: (nn(a, b), (a, b)),
              lambda r, g: (nt(g, r[1]).astype(r[0].dtype), tn(r[0], g).astype(r[1].dtype)))
    nt.defvjp(lambda a, b: (nt(a, b), (a, b)),
              lambda r, g: (nn(g, r[1]).astype(r[0].dtype), tn(g, r[0]).astype(r[1].dtype)))
    tn.defvjp(lambda a, b: (tn(a, b), (a, b)),
              lambda r, g: (nt(r[1], g).astype(r[0].dtype), nn(r[0], g).astype(r[1].dtype)))
    return nn, nt, tn


mm, mm_nt, mm_tn = _make_mm(False)
xmm, xmm_nt, xmm_tn = _make_mm(True)


@jax.custom_vjp
def sel_mm(m, g):
    mb = m.astype(BF16)
    parts = [jnp.dot(mb, p, preferred_element_type=F32) for p in _split_bf16(g, 3)]
    return parts[0] + (parts[1] + parts[2])


def _sel_mm_bwd(m, dy):
    mb = m.astype(BF16)
    parts = [lax.dot_general(mb, p, (((0,), (0,)), ((), ())), preferred_element_type=F32) for p in _split_bf16(dy, 3)]
    return jnp.zeros_like(m), parts[0] + (parts[1] + parts[2])


sel_mm.defvjp(lambda m, g: (sel_mm(m, g), m), _sel_mm_bwd)


@jax.custom_vjp
def tri_inv(a):
    n = a.shape[0]
    eye = jnp.where(lax.broadcasted_iota(jnp.int32, (n, n), 0) == lax.broadcasted_iota(jnp.int32, (n, n), 1), 1.0, 0.0)
    t = eye - a
    pw = a
    for _ in range(5):
        pw = xmm(pw, pw)
        t = t + xmm(t, pw)
    return t


def _tri_inv_bwd(t, dt):
    return (-xmm_tn(t, xmm_nt(dt, t)),)


tri_inv.defvjp(lambda a: (tri_inv(a),) * 2, _tri_inv_bwd)


def _sigmoid(x):
    return 1.0 / (1.0 + jnp.exp(-x))


def _silu(x):
    return x * _sigmoid(x)


def _softplus(x):
    return jnp.maximum(x, 0.0) + jnp.log(1.0 + jnp.exp(-jnp.abs(x)))


def _cparams(n_grid):
    return pltpu.CompilerParams(dimension_semantics=("arbitrary",) * n_grid, vmem_limit_bytes=VMEM_LIMIT)


def _row_spec(tm, width, colblk):
    return pl.BlockSpec((tm, width), lambda i, cb=colblk: (i, cb))


def _const_spec(shape):
    nd = len(shape)
    return pl.BlockSpec(tuple(shape), lambda i, nd=nd: (0,) * nd)


def rowwise_fwd(name, f, rows, consts, outs, tm):
    n_r, n_c = len(rows), len(consts)
    t = rows[0][0].shape[0]

    def body(*refs):
        vals = [r[...] for r in refs[:n_r + n_c]]
        res = f(*vals)
        if not isinstance(res, (tuple, list)):
            res = (res,)
        for o_ref, v in zip(refs[n_r + n_c:], res):
            o_ref[...] = v.astype(o_ref.dtype)

    return pl.pallas_call(
        body, name=name, grid=(t // tm,),
        in_specs=[_row_spec(tm, w, cb) for _, w, cb in rows] + [_const_spec(c.shape) for c in consts],
        out_specs=[_row_spec(tm, w, 0) for w, _ in outs],
        out_shape=[jax.ShapeDtypeStruct((t, w), dt) for w, dt in outs],
        compiler_params=_cparams(1),
    )(*[a for a, _, _ in rows], *consts)


def rowwise_bwd(name, f, rows, consts, cts, row_grad_dtypes, tm):
    n_r, n_c, n_ct = len(rows), len(consts), len(cts)
    t = rows[0][0].shape[0]
    keep = [k for k, dt in enumerate(row_grad_dtypes) if dt is not None]

    def body(*refs):
        ins = [r[...].astype(F32) for r in refs[:n_r + n_c]]
        g_out = [r[...].astype(F32) for r in refs[n_r + n_c:n_r + n_c + n_ct]]
        out_refs = refs[n_r + n_c + n_ct:]

        def fw(*a):
            res = f(*a)
            return tuple(res) if isinstance(res, (tuple, list)) else (res,)

        _, vjp = jax.vjp(fw, *ins)
        grads = vjp(tuple(g_out))
        for o_ref, k in zip(out_refs[:len(keep)], keep):
            o_ref[...] = grads[k].astype(o_ref.dtype)
        first = pl.program_id(0) == 0
        for o_ref, g in zip(out_refs[len(keep):], grads[n_r:]):
            @pl.when(first)
            def _(o_ref=o_ref, g=g):
                o_ref[...] = g

            @pl.when(jnp.logical_not(first))
            def _(o_ref=o_ref, g=g):
                o_ref[...] += g

    return pl.pallas_call(
        body, name=name, grid=(t // tm,),
        in_specs=[_row_spec(tm, w, cb) for _, w, cb in rows] + [_const_spec(c.shape) for c in consts]
        + [_row_spec(tm, w, cb) for _, w, cb in cts],
        out_specs=[_row_spec(tm, rows[k][1], 0) for k in keep] + [_const_spec(c.shape) for c in consts],
        out_shape=[jax.ShapeDtypeStruct((t, rows[k][1]), row_grad_dtypes[k]) for k in keep]
        + [jax.ShapeDtypeStruct(c.shape, F32) for c in consts],
        compiler_params=_cparams(1),
    )(*[a for a, _, _ in rows], *consts, *[a for a, _, _ in cts])


def f_norm_mod(x, g, scale, shift):
    y = x * lax.rsqrt(jnp.mean(x * x, axis=-1, keepdims=True) + EPS) * g
    return y * (1.0 + scale) + shift


def f_conf_tail(u, zb, ln_g, ln_b, pw2_w, pw2_b):
    mu = jnp.mean(u, axis=-1, keepdims=True)
    xc = u - mu
    var = jnp.mean(xc * xc, axis=-1, keepdims=True)
    y = _silu(xc * lax.rsqrt(var + EPS) * ln_g + ln_b)
    return (mm(y, pw2_w) + pw2_b) * _silu(zb)


def f_merge(ya, yb, yc, mg, x, gate, wpa, wpb, wpc, wout):
    d = D_MODEL
    merged = (_sigmoid(mg[:, :d]) * mm(ya, wpa) + _sigmoid(mg[:, d:2 * d]) * mm(yb, wpb)
              + _sigmoid(mg[:, 2 * d:]) * mm(yc, wpc))
    return x + gate * mm(merged, wout)


def matmul_nn(name, a, b, out_dtype, tm, tn, tk):
    m, k = a.shape
    n = b.shape[1]
    nk = k // tk

    def body(a_ref, b_ref, o_ref, acc_ref):
        kk = pl.program_id(2)
        part = jnp.dot(a_ref[...].astype(BF16), b_ref[...].astype(BF16), preferred_element_type=F32)

        @pl.when(kk == 0)
        def _():
            acc_ref[...] = part

        @pl.when(kk > 0)
        def _():
            acc_ref[...] += part

        @pl.when(kk == nk - 1)
        def _():
            o_ref[...] = acc_ref[...].astype(o_ref.dtype)

    return pl.pallas_call(
        body, name=name, grid=(m // tm, n // tn, nk),
        in_specs=[pl.BlockSpec((tm, tk), lambda i, j, kk: (i, kk)), pl.BlockSpec((tk, tn), lambda i, j, kk: (kk, j))],
        out_specs=pl.BlockSpec((tm, tn), lambda i, j, kk: (i, j)),
        out_shape=jax.ShapeDtypeStruct((m, n), out_dtype),
        scratch_shapes=[pltpu.VMEM((tm, tn), F32)],
        compiler_params=_cparams(3),
    )(a, b)


def matmul_tn(name, a, b, ta, tn, tm):
    m, k = a.shape
    n = b.shape[1]
    nm = m // tm

    def body(a_ref, b_ref, o_ref):
        mm_ = pl.program_id(2)
        part = lax.dot_general(a_ref[...].astype(BF16), b_ref[...].astype(BF16), (((0,), (0,)), ((), ())),
                               preferred_element_type=F32)

        @pl.when(mm_ == 0)
        def _():
            o_ref[...] = part

        @pl.when(mm_ > 0)
        def _():
            o_ref[...] += part

    return pl.pallas_call(
        body, name=name, grid=(k // ta, n // tn, nm),
        in_specs=[pl.BlockSpec((tm, ta), lambda i, j, r: (r, i)), pl.BlockSpec((tm, tn), lambda i, j, r: (r, j))],
        out_specs=pl.BlockSpec((ta, tn), lambda i, j, r: (i, j)),
        out_shape=jax.ShapeDtypeStruct((k, n), F32),
        compiler_params=_cparams(3),
    )(a, b)


def ada_fwd(name, c8, w_ada, b_ada):
    def body(c_ref, w_ref, b_ref, o_ref):
        o_ref[...] = mm(_silu(c_ref[...]), w_ref[...]) + b_ref[...]

    return pl.pallas_call(
        body, name=name, out_shape=jax.ShapeDtypeStruct((8, 3 * D_MODEL), F32),
        compiler_params=pltpu.CompilerParams(vmem_limit_bytes=VMEM_LIMIT),
    )(c8, w_ada, b_ada)


def ada_bwd(name, c8, dmod8):
    tn = 768

    def body(c_ref, d_ref, o_ref):
        row0 = lax.broadcasted_iota(jnp.int32, (8, 1), 0) == 0
        sc = jnp.where(row0, _silu(c_ref[...]), 0.0)
        o_ref[...] = mm_tn(sc, d_ref[...])

    return pl.pallas_call(
        body, name=name, grid=(3 * D_MODEL // tn,),
        in_specs=[pl.BlockSpec((8, D_MODEL), lambda j: (0, 0)), pl.BlockSpec((8, tn), lambda j: (0, j))],
        out_specs=pl.BlockSpec((D_MODEL, tn), lambda j: (0, j)),
        out_shape=jax.ShapeDtypeStruct((D_MODEL, 3 * D_MODEL), F32),
        compiler_params=_cparams(1),
    )(c8, dmod8)


def _f_attn(first_block, q, za, kc, vc, kp, vp, qg, kg, sinks):
    w = WINDOW
    lane = lax.broadcasted_iota(jnp.int32, (1, 128), 1)
    halves = [lane < 64, lane >= 64]

    def rms_halves(x, g):
        x2 = x * x
        s0 = jnp.sum(jnp.where(halves[0], x2, 0.0), axis=-1, keepdims=True)
        s1 = jnp.sum(jnp.where(halves[1], x2, 0.0), axis=-1, keepdims=True)
        r = jnp.where(halves[0], lax.rsqrt(s0 / 64.0 + EPS), lax.rsqrt(s1 / 64.0 + EPS))
        return x * r * g

    kcat = rms_halves(jnp.concatenate([kp, kc], axis=0), kg)
    vcat = jnp.concatenate([vp, vc], axis=0)
    qi = lax.broadcasted_iota(jnp.int32, (w, 2 * w), 0)
    kj = lax.broadcasted_iota(jnp.int32, (w, 2 * w), 1)
    dist = qi + w - kj
    valid = (dist >= 0) & (dist < w) & (jnp.logical_not(first_block) | (kj >= w))
    distf = dist.astype(F32)
    outs = []
    for grp in range(4):
        qn = rms_halves(q[:, 128 * grp:128 * grp + 128], qg) * (ATT_HEAD_DIM ** -0.5)
        o_grp = jnp.zeros((w, 128), F32)
        for half in range(2):
            head = HEAD_ORDER[2 * grp + half]
            slope = 2.0 ** (-8.0 * (head + 1) / ATT_HEADS)
            sink = jnp.sum(jnp.where(lane == head, sinks, 0.0), axis=-1, keepdims=True)
            s = mm_nt(jnp.where(halves[half], qn, 0.0), kcat) - slope * distf
            s = jnp.where(valid, s, NEG_INF)
            m = lax.stop_gradient(jnp.maximum(jnp.max(s, axis=-1, keepdims=True), sink))
            p = jnp.exp(s - m)
            denom = jnp.sum(p, axis=-1, keepdims=True) + jnp.exp(sink - m)
            o_grp = o_grp + mm(p / denom, jnp.where(halves[half], vcat, 0.0))
        outs.append(o_grp)
    return jnp.concatenate(outs, axis=1) * _silu(za)


def attn_fwd(name, proj, qg, kg, sinks):
    t = proj.shape[0]
    nb = t // WINDOW

    def body(q_ref, za_ref, kc_ref, vc_ref, kp_ref, vp_ref, qg_ref, kg_ref, s_ref, o_ref):
        first = pl.program_id(0) == 0
        o_ref[...] = _f_attn(first, q_ref[...], za_ref[...], kc_ref[...], vc_ref[...], kp_ref[...], vp_ref[...],
                             qg_ref[...], kg_ref[...], s_ref[...])

    cur = lambda cb: (lambda i: (i, cb))
    prev = lambda cb: (lambda i: (jnp.maximum(i - 1, 0), cb))
    return pl.pallas_call(
        body, name=name, grid=(nb,),
        in_specs=[pl.BlockSpec((WINDOW, 512), cur(P_QA // 512)), pl.BlockSpec((WINDOW, 512), cur(P_ZA // 512)),
                  pl.BlockSpec((WINDOW, 128), cur(P_KA // 128)), pl.BlockSpec((WINDOW, 128), cur(P_VA // 128)),
                  pl.BlockSpec((WINDOW, 128), prev(P_KA // 128)), pl.BlockSpec((WINDOW, 128), prev(P_VA // 128)),
                  _const_spec((1, 128)), _const_spec((1, 128)), _const_spec((1, 128))],
        out_specs=pl.BlockSpec((WINDOW, 512), lambda i: (i, 0)),
        out_shape=jax.ShapeDtypeStruct((t, 512), F32),
        compiler_params=_cparams(1),
    )(proj, proj, proj, proj, proj, proj, qg, kg, sinks)


def attn_bwd(name, proj, qg, kg, sinks, dya):
    t = proj.shape[0]
    nb = t // WINDOW

    def body(q_ref, za_ref, kc_ref, vc_ref, kp_ref, vp_ref, qg_ref, kg_ref, s_ref, dy_ref,
             dqz_ref, dkv_ref, dqg_ref, dkg_ref, ds_ref, carry_ref):
        j = pl.program_id(0)
        first = j == nb - 1

        @pl.when(j == 0)
        def _():
            carry_ref[...] = jnp.zeros_like(carry_ref)
            dqg_ref[...] = jnp.zeros_like(dqg_ref)
            dkg_ref[...] = jnp.zeros_like(dkg_ref)
            ds_ref[...] = jnp.zeros_like(ds_ref)

        ins = [r[...] for r in (q_ref, za_ref, kc_ref, vc_ref, kp_ref, vp_ref, qg_ref, kg_ref, s_ref)]
        _, vjp = jax.vjp(functools.partial(_f_attn, first), *ins)
        dq, dza, dkc, dvc, dkp, dvp, dqg, dkg, dsk = vjp(dy_ref[...])
        dqz_ref[:, 0:512] = dq.astype(dqz_ref.dtype)
        dqz_ref[:, 512:1024] = dza.astype(dqz_ref.dtype)
        dkv_ref[:, 0:128] = (dkc + carry_ref[0]).astype(dkv_ref.dtype)
        dkv_ref[:, 128:256] = (dvc + carry_ref[1]).astype(dkv_ref.dtype)
        carry_ref[0] = dkp
        carry_ref[1] = dvp
        dqg_ref[...] += dqg
        dkg_ref[...] += dkg
        ds_ref[...] += dsk

    cur = lambda cb: (lambda j: (nb - 1 - j, cb))
    prev = lambda cb: (lambda j: (jnp.maximum(nb - 2 - j, 0), cb))
    return pl.pallas_call(
        body, name=name, grid=(nb,),
        in_specs=[pl.BlockSpec((WINDOW, 512), cur(P_QA // 512)), pl.BlockSpec((WINDOW, 512), cur(P_ZA // 512)),
                  pl.BlockSpec((WINDOW, 128), cur(P_KA // 128)), pl.BlockSpec((WINDOW, 128), cur(P_VA // 128)),
                  pl.BlockSpec((WINDOW, 128), prev(P_KA // 128)), pl.BlockSpec((WINDOW, 128), prev(P_VA // 128)),
                  _const_spec((1, 128)), _const_spec((1, 128)), _const_spec((1, 128)),
                  pl.BlockSpec((WINDOW, 512), cur(0))],
        out_specs=[pl.BlockSpec((WINDOW, 1024), cur(0)), pl.BlockSpec((WINDOW, 256), cur(0)),
                   _const_spec((1, 128)), _const_spec((1, 128)), _const_spec((1, 128))],
        out_shape=[jax.ShapeDtypeStruct((t, 1024), BF16), jax.ShapeDtypeStruct((t, 256), BF16),
                   jax.ShapeDtypeStruct((1, 128), F32), jax.ShapeDtypeStruct((1, 128), F32),
                   jax.ShapeDtypeStruct((1, 128), F32)],
        scratch_shapes=[pltpu.VMEM((2, WINDOW, 128), F32)],
        compiler_params=_cparams(1),
    )(proj, proj, proj, proj, proj, proj, qg, kg, sinks, dya)


CONV_ROWS = 256


def _conv_taps(src_ref, w_ref, n_taps, base, t):
    for r0 in range(0, t, CONV_ROWS):
        acc = w_ref[0:1, :] * src_ref[pl.ds(r0 + base, CONV_ROWS), :]
        for k in range(1, n_taps):
            acc = acc + w_ref[k:k + 1, :] * src_ref[pl.ds(r0 + base + k, CONV_ROWS), :]
        yield r0, acc


def _conv_wgrad(dy_ref, src_ref, n_taps, base, t, dy_base=0):
    out = []
    for k in range(n_taps):
        acc = jnp.zeros((8, 128), F32)
        for r0 in range(0, t, CONV_ROWS):
            prod = dy_ref[pl.ds(r0 + dy_base, CONV_ROWS), :] * src_ref[pl.ds(r0 + base + k, CONV_ROWS), :]
            acc = acc + jnp.sum(prod.reshape(CONV_ROWS // 8, 8, 128), axis=0)
        out.append(jnp.sum(acc, axis=0, keepdims=True))
    return out


def glu_conv_fwd(name, proj, w32, bias):
    t = proj.shape[0]
    pad = 32

    def body(x_ref, w_ref, b_ref, o_ref, u_ref):
        u_ref[0:pad, :] = jnp.zeros((pad, 128), F32)
        u_ref[pad:pad + t, :] = x_ref[:, 0:128] * _sigmoid(x_ref[:, 128:256])
        for r0, acc in _conv_taps(u_ref, w_ref, CONV_K, pad - (CONV_K - 1), t):
            o_ref[pl.ds(r0, CONV_ROWS), :] = acc + b_ref[...]

    return pl.pallas_call(
        body, name=name, grid=(4,),
        in_specs=[pl.BlockSpec((t, 256), lambda cb: (0, P_GLU // 256 + cb)), pl.BlockSpec((32, 128), lambda cb: (0, cb)),
                  pl.BlockSpec((1, 128), lambda cb: (0, cb))],
        out_specs=pl.BlockSpec((t, 128), lambda cb: (0, cb)),
        out_shape=jax.ShapeDtypeStruct((t, 512), F32),
        scratch_shapes=[pltpu.VMEM((t + pad, 128), F32)],
        compiler_params=_cparams(1),
    )(proj, w32, bias)


def glu_conv_bwd(name, proj, w32, dub):
    t = proj.shape[0]
    pad = 32
    k1 = CONV_K - 1

    def body(x_ref, w_ref, dy_ref, dx_ref, dw_ref, db_ref, u_ref, dyp_ref, wrev_ref):
        val = x_ref[:, 0:128]
        sg = _sigmoid(x_ref[:, 128:256])
        u_ref[0:pad, :] = jnp.zeros((pad, 128), F32)
        u_ref[pad:pad + t, :] = val * sg
        dyp_ref[0:t, :] = dy_ref[...]
        dyp_ref[t:t + pad, :] = jnp.zeros((pad, 128), F32)
        for k in range(CONV_K):
            wrev_ref[k:k + 1, :] = w_ref[k1 - k:k1 - k + 1, :]
        wrev_ref[CONV_K:32, :] = jnp.zeros((32 - CONV_K, 128), F32)
        for r0, du in _conv_taps(dyp_ref, wrev_ref, CONV_K, 0, t):
            v = x_ref[pl.ds(r0, CONV_ROWS), 0:128]
            s = _sigmoid(x_ref[pl.ds(r0, CONV_ROWS), 128:256])
            dx_ref[pl.ds(r0, CONV_ROWS), 0:128] = (du * s).astype(dx_ref.dtype)
            dx_ref[pl.ds(r0, CONV_ROWS), 128:256] = (du * v * s * (1.0 - s)).astype(dx_ref.dtype)
        dws = _conv_wgrad(dyp_ref, u_ref, CONV_K, pad - k1, t)
        for k in range(CONV_K):
            dw_ref[k:k + 1, :] = dws[k]
        dw_ref[CONV_K:32, :] = jnp.zeros((32 - CONV_K, 128), F32)
        db_ref[...] = jnp.sum(dy_ref[...], axis=0, keepdims=True)

    return pl.pallas_call(
        body, name=name, grid=(4,),
        in_specs=[pl.BlockSpec((t, 256), lambda cb: (0, P_GLU // 256 + cb)), pl.BlockSpec((32, 128), lambda cb: (0, cb)),
                  pl.BlockSpec((t, 128), lambda cb: (0, cb))],
        out_specs=[pl.BlockSpec((t, 256), lambda cb: (0, cb)), pl.BlockSpec((32, 128), lambda cb: (0, cb)),
                   pl.BlockSpec((1, 128), lambda cb: (0, cb))],
        out_shape=[jax.ShapeDtypeStruct((t, 1024), BF16), jax.ShapeDtypeStruct((32, 512), F32),
                   jax.ShapeDtypeStruct((1, 512), F32)],
        scratch_shapes=[pltpu.VMEM((t + pad, 128), F32), pltpu.VMEM((t + pad, 128), F32), pltpu.VMEM((32, 128), F32)],
        compiler_params=_cparams(1),
    )(proj, w32, dub)


def sconv_fwd(name, proj, w8):
    t = proj.shape[0]
    pad = 8
    k1 = DN_CONV_K - 1

    def body(x_ref, w_ref, o_ref, xp_ref):
        xp_ref[0:pad, :] = jnp.zeros((pad, 128), F32)
        xp_ref[pad:pad + t, :] = x_ref[...]
        for r0, acc in _conv_taps(xp_ref, w_ref, DN_CONV_K, pad - k1, t):
            o_ref[pl.ds(r0, CONV_ROWS), :] = _silu(acc)

    return pl.pallas_call(
        body, name=name, grid=(12,),
        in_specs=[pl.BlockSpec((t, 128), lambda cb: (0, P_QKV // 128 + cb)), pl.BlockSpec((8, 128), lambda cb: (0, cb))],
        out_specs=pl.BlockSpec((t, 128), lambda cb: (0, cb)),
        out_shape=jax.ShapeDtypeStruct((t, 1536), F32),
        scratch_shapes=[pltpu.VMEM((t + pad, 128), F32)],
        compiler_params=_cparams(1),
    )(proj, w8)


def sconv_bwd(name, proj, w8, dqkv):
    t = proj.shape[0]
    pad = 8
    k1 = DN_CONV_K - 1

    def body(x_ref, w_ref, dy_ref, dx_ref, dw_ref, xp_ref, dpp_ref, wrev_ref):
        xp_ref[0:pad, :] = jnp.zeros((pad, 128), F32)
        xp_ref[pad:pad + t, :] = x_ref[...]
        for r0, pre in _conv_taps(xp_ref, w_ref, DN_CONV_K, pad - k1, t):
            s = _sigmoid(pre)
            dpp_ref[pl.ds(r0, CONV_ROWS), :] = dy_ref[pl.ds(r0, CONV_ROWS), :] * (s * (1.0 + pre * (1.0 - s)))
        dpp_ref[t:t + pad, :] = jnp.zeros((pad, 128), F32)
        for k in range(DN_CONV_K):
            wrev_ref[k:k + 1, :] = w_ref[k1 - k:k1 - k + 1, :]
        wrev_ref[DN_CONV_K:8, :] = jnp.zeros((8 - DN_CONV_K, 128), F32)
        for r0, dx in _conv_taps(dpp_ref, wrev_ref, DN_CONV_K, 0, t):
            dx_ref[pl.ds(r0, CONV_ROWS), :] = dx.astype(dx_ref.dtype)
        dws = _conv_wgrad(dpp_ref, xp_ref, DN_CONV_K, pad - k1, t)
        for k in range(DN_CONV_K):
            dw_ref[k:k + 1, :] = dws[k]
        dw_ref[DN_CONV_K:8, :] = jnp.zeros((8 - DN_CONV_K, 128), F32)

    return pl.pallas_call(
        body, name=name, grid=(12,),
        in_specs=[pl.BlockSpec((t, 128), lambda cb: (0, P_QKV // 128 + cb)), pl.BlockSpec((8, 128), lambda cb: (0, cb)),
                  pl.BlockSpec((t, 128), lambda cb: (0, cb))],
        out_specs=[pl.BlockSpec((t, 128), lambda cb: (0, cb)), pl.BlockSpec((8, 128), lambda cb: (0, cb))],
        out_shape=[jax.ShapeDtypeStruct((t, 1536), BF16), jax.ShapeDtypeStruct((8, 1536), F32)],
        scratch_shapes=[pltpu.VMEM((t + pad, 128), F32), pltpu.VMEM((t + pad, 128), F32), pltpu.VMEM((8, 128), F32)],
        compiler_params=_cparams(1),
    )(proj, w8, dqkv)


def _f_delta(qkv, ab, zc, s0, s1, s2, s3, a_log, dt_bias, dn_g):
    cs = DN_CHUNK
    n = 2 * cs
    states = (s0, s1, s2, s3)
    lane = lax.broadcasted_iota(jnp.int32, (1, 128), 1)
    ri = lax.broadcasted_iota(jnp.int32, (n, n), 0)
    ci = lax.broadcasted_iota(jnp.int32, (n, n), 1)
    same = (ri // cs) == (ci // cs)
    lower = same & (ri >= ci)
    strict = same & (ri > ci)
    sums = jnp.concatenate([jnp.where(lower, 1.0, 0.0), jnp.where(same, 1.0, 0.0), jnp.where(ci < cs, 1.0, 0.0),
                            jnp.where(ci >= cs, 1.0, 0.0)], axis=0)
    top = lax.broadcasted_iota(jnp.int32, (n, 1), 0) < cs

    def pick(row, idx):
        return jnp.sum(jnp.where(lane == idx, row, 0.0), axis=-1, keepdims=True)

    def l2n(x):
        return x * lax.rsqrt(jnp.sum(x * x, axis=-1, keepdims=True) + EPS)

    ys, new_states = [], []
    for pair in range(2):
        hs = (2 * pair, 2 * pair + 1)
        stack = lambda f: jnp.concatenate([f(hs[0]), f(hs[1])], axis=0)
        qd = l2n(stack(lambda h: qkv[:, 128 * h:128 * h + 128])) * (128 ** -0.5)
        kd = l2n(stack(lambda h: qkv[:, 512 + 128 * h:512 + 128 * h + 128]))
        vd = stack(lambda h: qkv[:, 1024 + 128 * h:1024 + 128 * h + 128])
        beta = _sigmoid(stack(lambda h: pick(ab, 4 + h)))
        g = stack(lambda h: -jnp.exp(pick(a_log, h)) * _softplus(pick(ab, h) + pick(dt_bias, h)))
        g_b = g * jnp.ones((1, n), F32)
        g_sums = sel_mm(sums, g_b)
        gc_col = g_sums[0:n]
        gc_row = gc_col.T
        gl_b = g_sums[n:2 * n]
        g_end = (g_sums[2 * n:3 * n], g_sums[3 * n:])
        decay = jnp.where(lower, jnp.exp(jnp.where(lower, gc_col - gc_row, 0.0)), 0.0)
        kb = kd * beta
        vb = vd * beta
        tmat = tri_inv(jnp.where(strict, mm_nt(kb, kd) * decay, 0.0))
        egc = jnp.exp(gc_col)
        u = mm(tmat, vb)
        wm = mm(tmat, kb * egc)
        intra = jnp.where(lower, mm_nt(qd, kd) * decay, 0.0)
        qe = qd * egc
        ke = kd * jnp.exp(gl_b - gc_col)
        st = (states[hs[0]], states[hs[1]])
        v_new = u - jnp.concatenate([mm(wm[:cs], st[0]), mm(wm[cs:], st[1])], axis=0)
        o = jnp.concatenate([mm(qe[:cs], st[0]), mm(qe[cs:], st[1])], axis=0) + mm(intra, v_new)
        new_states.append(st[0] * jnp.exp(g_end[0]) + mm_tn(jnp.where(top, ke, 0.0), v_new))
        new_states.append(st[1] * jnp.exp(g_end[1]) + mm_tn(jnp.where(top, 0.0, ke), v_new))
        od = o * lax.rsqrt(jnp.mean(o * o, axis=-1, keepdims=True) + EPS) * dn_g
        ys += [od[:cs] * _silu(zc[:, 128 * hs[0]:128 * hs[0] + 128]), od[cs:] * _silu(zc[:, 128 * hs[1]:128 * hs[1] + 128])]
    y = jnp.concatenate([ys[0], ys[1], ys[2], ys[3]], axis=1)
    return (y, *new_states)


def delta_fwd(name, qkv, proj, a_log, dt_bias, dn_g):
    t = qkv.shape[0]
    nc = t // DN_CHUNK

    def body(qkv_ref, ab_ref, zc_ref, al_ref, dt_ref, g_ref, y_ref, ssave_ref, s_ref):
        @pl.when(pl.program_id(0) == 0)
        def _():
            s_ref[...] = jnp.zeros_like(s_ref)

        ssave_ref[0] = s_ref[...]
        st = [s_ref[128 * h:128 * h + 128, :] for h in range(4)]
        y, *ns = _f_delta(qkv_ref[...], ab_ref[...], zc_ref[...], *st, al_ref[...], dt_ref[...], g_ref[...])
        y_ref[...] = y
        for h in range(4):
            s_ref[128 * h:128 * h + 128, :] = ns[h]

    return pl.pallas_call(
        body, name=name, grid=(nc,),
        in_specs=[pl.BlockSpec((DN_CHUNK, 1536), lambda i: (i, 0)), pl.BlockSpec((DN_CHUNK, 128), lambda i: (i, P_AB // 128)),
                  pl.BlockSpec((DN_CHUNK, 512), lambda i: (i, P_ZC // 512)),
                  _const_spec((1, 128)), _const_spec((1, 128)), _const_spec((1, 128))],
        out_specs=[pl.BlockSpec((DN_CHUNK, 512), lambda i: (i, 0)), pl.BlockSpec((1, 512, 128), lambda i: (i, 0, 0))],
        out_shape=[jax.ShapeDtypeStruct((t, 512), F32), jax.ShapeDtypeStruct((nc, 512, 128), F32)],
        scratch_shapes=[pltpu.VMEM((512, 128), F32)],
        compiler_params=_cparams(1),
    )(qkv, proj, proj, a_log, dt_bias, dn_g)


def delta_bwd(name, qkv, proj, ssave, a_log, dt_bias, dn_g, dyc):
    t = qkv.shape[0]
    nc = t // DN_CHUNK

    def body(qkv_ref, ab_ref, zc_ref, ss_ref, al_ref, dt_ref, g_ref, dy_ref,
             dqkv_ref, dab_ref, dzc_ref, dal_ref, ddt_ref, dg_ref, ds_ref):
        @pl.when(pl.program_id(0) == 0)
        def _():
            ds_ref[...] = jnp.zeros_like(ds_ref)
            dal_ref[...] = jnp.zeros_like(dal_ref)
            ddt_ref[...] = jnp.zeros_like(ddt_ref)
            dg_ref[...] = jnp.zeros_like(dg_ref)

        st = [ss_ref[0, 128 * h:128 * h + 128, :] for h in range(4)]
        _, vjp = jax.vjp(_f_delta, qkv_ref[...], ab_ref[...], zc_ref[...], *st, al_ref[...], dt_ref[...], g_ref[...])
        dst = tuple(ds_ref[128 * h:128 * h + 128, :] for h in range(4))
        dqkv, dab, dzc, d0, d1, d2, d3, dal, ddt, dg = vjp((dy_ref[...], *dst))
        dqkv_ref[...] = dqkv
        dab_ref[...] = dab.astype(dab_ref.dtype)
        dzc_ref[...] = dzc.astype(dzc_ref.dtype)
        for h, d in enumerate((d0, d1, d2, d3)):
            ds_ref[128 * h:128 * h + 128, :] = d
        dal_ref[...] += dal
        ddt_ref[...] += ddt
        dg_ref[...] += dg

    rev = lambda cb: (lambda j: (nc - 1 - j, cb))
    return pl.pallas_call(
        body, name=name, grid=(nc,),
        in_specs=[pl.BlockSpec((DN_CHUNK, 1536), rev(0)), pl.BlockSpec((DN_CHUNK, 128), rev(P_AB // 128)),
                  pl.BlockSpec((DN_CHUNK, 512), rev(P_ZC // 512)), pl.BlockSpec((1, 512, 128), lambda j: (nc - 1 - j, 0, 0)),
                  _const_spec((1, 128)), _const_spec((1, 128)), _const_spec((1, 128)),
                  pl.BlockSpec((DN_CHUNK, 512), rev(0))],
        out_specs=[pl.BlockSpec((DN_CHUNK, 1536), rev(0)), pl.BlockSpec((DN_CHUNK, 128), rev(0)),
                   pl.BlockSpec((DN_CHUNK, 512), rev(0)),
                   _const_spec((1, 128)), _const_spec((1, 128)), _const_spec((1, 128))],
        out_shape=[jax.ShapeDtypeStruct((t, 1536), F32), jax.ShapeDtypeStruct((t, 128), BF16),
                   jax.ShapeDtypeStruct((t, 512), BF16),
                   jax.ShapeDtypeStruct((1, 128), F32), jax.ShapeDtypeStruct((1, 128), F32), jax.ShapeDtypeStruct((1, 128), F32)],
        scratch_shapes=[pltpu.VMEM((512, 128), F32)],
        compiler_params=_cparams(1),
    )(qkv, proj, proj, ssave, a_log, dt_bias, dn_g, dyc)


def loss_head(name, y, target, tm):
    t, d = y.shape

    def body(y_ref, t_ref, dy_ref, l_ref):
        err = y_ref[...] - t_ref[...]
        dy_ref[...] = err * (1.0 / d)
        part = 0.5 * jnp.sum(jnp.sum(err * err, axis=-1, keepdims=True) * (1.0 / d), axis=0, keepdims=True)

        @pl.when(pl.program_id(0) == 0)
        def _():
            l_ref[...] = part

        @pl.when(pl.program_id(0) > 0)
        def _():
            l_ref[...] += part

    return pl.pallas_call(
        body, name=name, grid=(t // tm,),
        in_specs=[_row_spec(tm, d, 0), _row_spec(tm, d, 0)],
        out_specs=[_row_spec(tm, d, 0), _const_spec((1, 1))],
        out_shape=[jax.ShapeDtypeStruct((t, d), F32), jax.ShapeDtypeStruct((1, 1), F32)],
        compiler_params=_cparams(1),
    )(y, target)


TM = 512
TM_MERGE = 256
TN_IN = 1152


def _lane_pad(v, n=128):
    return jnp.pad(v.astype(F32), (0, n - v.shape[0]))[None, :]


def f_norm_mod_res(x, g, scale, shift):
    return f_norm_mod(x, g, scale, shift), x


def prep_layer(w):
    p = dict(w)
    p["wp"] = _pad_w_in_from_shards(w["w_in"])
    p["wpt"] = p["wp"].T
    p["wpa"] = _perm_heads_rows(w["w_proj_a"])
    p["dw32"] = jnp.pad(w["dw_w"], ((0, 32 - CONV_K), (0, 0)))
    p["sconv8"] = jnp.pad(w["sconv_w"], ((0, 8 - DN_CONV_K), (0, 0)))
    p["qg"] = jnp.tile(w["q_norm_g"], 2)[None, :]
    p["kg"] = jnp.tile(w["k_norm_g"], 2)[None, :]
    p["sinks128"] = _lane_pad(w["sinks"])
    p["al"] = _lane_pad(w["a_log"])
    p["dtb"] = _lane_pad(w["dt_bias"])
    p["dng"] = w["dn_norm_g"][None, :]
    return p


def layer_fwd(tag, x, c8, p):
    mod = ada_fwd(f"ada_fwd{tag}", c8, p["w_ada"], p["b_ada"][None, :])[0:1]
    d = D_MODEL
    shift, scale, gate = mod[:, :d], mod[:, d:2 * d], mod[:, 2 * d:]
    g = p["norm_g"][None, :]
    (h,) = rowwise_fwd(f"norm_fwd{tag}", f_norm_mod, [(x, d, 0)], [g, scale, shift], [(d, BF16)], TM)
    proj = matmul_nn(f"inproj_fwd{tag}", h, p["wp"], F32, TM, TN_IN, d)
    ya = attn_fwd(f"attn_fwd{tag}", proj, p["qg"], p["kg"], p["sinks128"])
    ub = glu_conv_fwd(f"glu_conv_fwd{tag}", proj, p["dw32"], p["dw_b"][None, :])
    conf_consts = [p["ln_g"][None, :], p["ln_b"][None, :], p["pw2_w"], p["pw2_b"][None, :]]
    (yb,) = rowwise_fwd(f"conf_fwd{tag}", f_conf_tail, [(ub, 512, 0), (proj, 512, P_ZB // 512)], conf_consts, [(512, F32)], TM)
    qkv = sconv_fwd(f"sconv_fwd{tag}", proj, p["sconv8"])
    yc, ssave = delta_fwd(f"delta_fwd{tag}", qkv, proj, p["al"], p["dtb"], p["dng"])
    merge_consts = [gate, p["wpa"], p["w_proj_b"], p["w_proj_c"], p["w_out"]]
    merge_rows = [(ya, 512, 0), (yb, 512, 0), (yc, 512, 0), (proj, 3 * d, P_MG // (3 * d)), (x, d, 0)]
    (xn,) = rowwise_fwd(f"merge_fwd{tag}", f_merge, merge_rows, merge_consts, [(d, F32)], TM_MERGE)
    saved = dict(x=x, h=h, proj=proj, ub=ub, qkv=qkv, ssave=ssave, norm_consts=[g, scale, shift],
                 conf_consts=conf_consts, merge_consts=merge_consts, merge_rows=merge_rows)
    return xn, saved


def layer_bwd(tag, dxn, c8, p, s):
    d = D_MODEL
    proj = s["proj"]
    dya, dyb, dyc, dmg, dgate, dwpa, dwpb, dwpc, dwout = rowwise_bwd(
        f"merge_bwd{tag}", f_merge, s["merge_rows"], s["merge_consts"], [(dxn, d, 0)], [F32, F32, F32, BF16, None], TM_MERGE)
    dqz, dkv, dqg, dkg, dsinks = attn_bwd(f"attn_bwd{tag}", proj, p["qg"], p["kg"], p["sinks128"], dya)
    dub, dzb, dln_g, dln_b, dpw2_w, dpw2_b = rowwise_bwd(
        f"conf_bwd{tag}", f_conf_tail, [(s["ub"], 512, 0), (proj, 512, P_ZB // 512)], s["conf_consts"], [(dyb, 512, 0)],
        [F32, BF16], TM)
    dglu, ddw32, ddw_b = glu_conv_bwd(f"glu_conv_bwd{tag}", proj, p["dw32"], dub)
    dqkv, dab, dzc, dal, ddtb, ddng = delta_bwd(f"delta_bwd{tag}", s["qkv"], proj, s["ssave"], p["al"], p["dtb"], p["dng"], dyc)
    dqkv_pre, dsconv8 = sconv_bwd(f"sconv_bwd{tag}", proj, p["sconv8"], dqkv)
    dproj = jnp.concatenate([dqz, dglu, dzb, dzc, dmg, dqkv_pre, dkv, dab], axis=1)
    dh = matmul_nn(f"inproj_bwd_dh{tag}", dproj, p["wpt"], F32, TM, d, TN_IN)
    dwp = matmul_tn(f"inproj_bwd_dw{tag}", s["h"], dproj, 512, TN_IN, TM)
    dx, dnorm_g, dscale, dshift = rowwise_bwd(
        f"norm_bwd{tag}", f_norm_mod_res, [(s["x"], d, 0)], s["norm_consts"], [(dh, d, 0), (dxn, d, 0)], [F32], TM)
    dmod = jnp.concatenate([dshift, dscale, dgate], axis=1)
    dw_ada = ada_bwd(f"ada_bwd{tag}", c8, jnp.pad(dmod, ((0, 7), (0, 0))))
    grads = dict(
        w_ada=dw_ada, b_ada=dmod[0], norm_g=dnorm_g[0], w_in=_unpad_w_in_to_shards(dwp),
        q_norm_g=dqg[0, :64] + dqg[0, 64:], k_norm_g=dkg[0, :64] + dkg[0, 64:], sinks=dsinks[0, :ATT_HEADS],
        dw_w=ddw32[:CONV_K], dw_b=ddw_b[0], ln_g=dln_g[0], ln_b=dln_b[0], pw2_w=dpw2_w, pw2_b=dpw2_b[0],
        sconv_w=dsconv8[:DN_CONV_K], a_log=dal[0, :DN_HEADS], dt_bias=ddtb[0, :DN_HEADS], dn_norm_g=ddng[0],
        w_proj_a=_unperm_heads_rows(dwpa), w_proj_b=dwpb, w_proj_c=dwpc, w_out=dwout)
    return dx, grads


SHARDED = {"w_ada": 2, "w_in": 2, "dw_w": 2, "pw2_w": 1, "sconv_w": 2, "w_proj_a": 2, "w_proj_b": 2, "w_proj_c": 2,
           "w_out": 1}
GATHER_F32 = ("dw_w", "sconv_w")
REDUCE_BIG = tuple(n for n in SHARDED if n not in GATHER_F32)
SMALL = ("b_ada", "norm_g", "q_norm_g", "k_norm_g", "sinks", "dw_b", "ln_g", "ln_b", "pw2_b", "a_log", "dt_bias",
         "dn_norm_g")
SMALL_ROWS = 104
SMALL_GRAD_ROWS = 448
W_IN_SHARD = D_IN // N_CHIPS
SUM_TILE = 256


def _shard_cols(shards, start, n):
    parts = []
    while n > 0:
        k, o = divmod(start, W_IN_SHARD)
        m = min(n, W_IN_SHARD - o)
        parts.append(shards[k][:, o:o + m])
        start, n = start + m, n - m
    return parts


def _pad_w_in_from_shards(shards):
    parts = []
    for s, n in _in_pieces():
        parts += _shard_cols(shards, s, n)
    parts.append(jnp.zeros((shards.shape[1], P_TOTAL - D_IN), shards.dtype))
    return jnp.concatenate(parts, axis=1)


def _unpad_w_in_to_shards(wp):
    pieces = _in_pieces()
    starts = np.cumsum([0] + [n for _, n in pieces])[:-1]
    order = sorted(range(len(pieces)), key=lambda i: pieces[i][0])
    shards = []
    for k in range(N_CHIPS):
        lo, hi = k * W_IN_SHARD, (k + 1) * W_IN_SHARD
        parts = []
        for i in order:
            s, n = pieces[i]
            a, b = max(s, lo), min(s + n, hi)
            if a < b:
                parts.append(wp[:, int(starts[i]) + a - s:int(starts[i]) + b - s])
        shards.append(jnp.concatenate(parts, axis=1))
    return jnp.stack(shards)


def _join_layer(v, axis):
    if axis == 2:
        return jnp.transpose(v, (1, 0, 2)).reshape(v.shape[1], N_CHIPS * v.shape[2])
    return v.reshape(N_CHIPS * v.shape[1], v.shape[2])


def _split_layer(v, axis):
    a, b = v.shape
    if axis == 2:
        return jnp.transpose(v.reshape(a, N_CHIPS, b // N_CHIPS), (1, 0, 2))
    return v.reshape(N_CHIPS, a // N_CHIPS, b)


def pack_small(vals, names, rows):
    flat = jnp.concatenate([vals[n].astype(F32).reshape(-1) for n in names])
    return jnp.pad(flat, (0, rows * 128 - flat.shape[0])).reshape(rows, 128)


def unpack_small(packed, names, shapes):
    flat = packed.reshape(-1)
    out, off = {}, 0
    for n in names:
        k = int(np.prod(shapes[n]))
        out[n] = flat[off:off + k].reshape(shapes[n])
        off += k
    return out


ANY = pl.BlockSpec(memory_space=pl.ANY)


def _place():
    x, y, c = lax.axis_index("x"), lax.axis_index("y"), lax.axis_index("c")
    chips = [(1 - x, y), (x, 1 - y), (1 - x, 1 - y)]
    return x, y, c, chips


def _remote(src, dst, send_sem, recv_sem, to):
    return pltpu.make_async_remote_copy(src_ref=src, dst_ref=dst, send_sem=send_sem, recv_sem=recv_sem, device_id=to,
                                        device_id_type=MESH)


def weights_allgather(shards):
    n = len(shards)

    def body(*refs):
        src, out = refs[:n], refs[n:2 * n]
        send_sems, recv_sems, local_sems = refs[2 * n:]
        x, y, c, chips = _place()
        me, sibling, my_slot = (x, y, c), (x, y, 1 - c), 2 * x + y
        local = [pltpu.make_async_copy(src[t], out[t].at[my_slot], local_sems.at[t]) for t in range(n)]
        for cp in local:
            cp.start()
        sends = []
        for j, chip in enumerate(chips):
            for t in range(n):
                sends.append(_remote(src[t].at[c], out[t].at[my_slot, c], send_sems.at[t, j], recv_sems.at[t, j], (*chip, c)))
                sends[-1].start()
        for j, chip in enumerate(chips):
            for t in range(n):
                land = out[t].at[2 * chip[0] + chip[1], c]
                _remote(land, land, send_sems.at[t, j], recv_sems.at[t, j], me).wait_recv()
                sends.append(_remote(land, land, send_sems.at[t, 3 + j], recv_sems.at[t, 3 + j], sibling))
                sends[-1].start()
        for j, chip in enumerate(chips):
            for t in range(n):
                land = out[t].at[2 * chip[0] + chip[1], 1 - c]
                _remote(land, land, send_sems.at[t, 3 + j], recv_sems.at[t, 3 + j], me).wait_recv()
        for cp in sends:
            cp.wait_send()
        for cp in local:
            cp.wait()

    return pl.pallas_call(
        body, name="weights_allgather", out_shape=[jax.ShapeDtypeStruct((N_CHIPS,) + s.shape, s.dtype) for s in shards],
        in_specs=[ANY] * n, out_specs=[ANY] * n,
        scratch_shapes=[pltpu.SemaphoreType.DMA((n, 6)), pltpu.SemaphoreType.DMA((n, 6)), pltpu.SemaphoreType.DMA((n,))],
    )(*shards)


def grads_pair_exchange(gs):
    n = len(gs)

    def body(*refs):
        g, recv = refs[:n], refs[n:2 * n]
        send_sems, recv_sems = refs[2 * n:]
        x, y, c, _ = _place()
        cps = [_remote(g[t].at[:, 1 - c], recv[t], send_sems.at[t], recv_sems.at[t], (x, y, 1 - c)) for t in range(n)]
        for cp in cps:
            cp.start()
        for cp in cps:
            cp.wait()

    return pl.pallas_call(
        body, name="grads_pair_exchange",
        out_shape=[jax.ShapeDtypeStruct((N_CHIPS,) + g.shape[2:], g.dtype) for g in gs],
        in_specs=[ANY] * n, out_specs=[ANY] * n,
        scratch_shapes=[pltpu.SemaphoreType.DMA((n,)), pltpu.SemaphoreType.DMA((n,))],
    )(*gs)


def grads_pair_sum(name, g, recv):
    _, a, b = recv.shape
    ta = min(a, SUM_TILE)

    def body(a_ref, b_ref, o_ref):
        o_ref[...] = (a_ref[...] + b_ref[...]).astype(o_ref.dtype)

    return pl.pallas_call(
        body, name=name, grid=(N_CHIPS, a // ta),
        in_specs=[pl.BlockSpec((None, None, ta, b), lambda s, i: (s, lax.axis_index("c"), i, 0)),
                  pl.BlockSpec((None, ta, b), lambda s, i: (s, i, 0))],
        out_specs=pl.BlockSpec((None, ta, b), lambda s, i: (s, i, 0)),
        out_shape=jax.ShapeDtypeStruct(recv.shape, BF16),
        compiler_params=_cparams(2),
    )(g, recv)


def grads_chip_exchange(ps):
    n = len(ps)

    def body(*refs):
        p, recv = refs[:n], refs[n:2 * n]
        send_sems, recv_sems = refs[2 * n:]
        x, y, c, chips = _place()
        cps = [_remote(p[t].at[2 * chip[0] + chip[1]], recv[t].at[j], send_sems.at[t, j], recv_sems.at[t, j], (*chip, c))
               for j, chip in enumerate(chips) for t in range(n)]
        for cp in cps:
            cp.start()
        for cp in cps:
            cp.wait()

    return pl.pallas_call(
        body, name="grads_chip_exchange", out_shape=[jax.ShapeDtypeStruct((3,) + p.shape[1:], p.dtype) for p in ps],
        in_specs=[ANY] * n, out_specs=[ANY] * n,
        scratch_shapes=[pltpu.SemaphoreType.DMA((n, 3)), pltpu.SemaphoreType.DMA((n, 3))],
    )(*ps)


def grads_chip_sum(name, g, recv, recv2):
    _, a, b = recv.shape
    ta = min(a, SUM_TILE)
    my_slot = lambda: 2 * lax.axis_index("x") + lax.axis_index("y")

    def body(g_ref, r_ref, r2_ref, o_ref):
        own = g_ref[...] + r_ref[...]
        o_ref[...] = ((own + r2_ref[0].astype(F32)) + r2_ref[1].astype(F32)) + r2_ref[2].astype(F32)

    return pl.pallas_call(
        body, name=name, grid=(a // ta,),
        in_specs=[pl.BlockSpec((None, None, ta, b), lambda i: (my_slot(), lax.axis_index("c"), i, 0)),
                  pl.BlockSpec((None, ta, b), lambda i: (my_slot(), i, 0)),
                  pl.BlockSpec((3, ta, b), lambda i: (0, i, 0))],
        out_specs=pl.BlockSpec((ta, b), lambda i: (i, 0)),
        out_shape=jax.ShapeDtypeStruct((a, b), F32),
        compiler_params=_cparams(1),
    )(g, recv, recv2)


def grads_pair_gather(reds):
    n = len(reds)

    def body(*refs):
        red, out = refs[:n], refs[n:2 * n]
        send_sems, recv_sems, local_sems = refs[2 * n:]
        x, y, c, _ = _place()
        sibling = (x, y, 1 - c)
        local = [pltpu.make_async_copy(red[t], out[t].at[c], local_sems.at[t]) for t in range(n)]
        cps = [_remote(red[t], out[t].at[c], send_sems.at[t], recv_sems.at[t], sibling) for t in range(n)]
        for cp in local + cps:
            cp.start()
        for t in range(n):
            _remote(red[t], out[t].at[1 - c], send_sems.at[t], recv_sems.at[t], sibling).wait_recv()
        for cp in cps:
            cp.wait_send()
        for cp in local:
            cp.wait()

    return pl.pallas_call(
        body, name="grads_pair_gather", out_shape=[jax.ShapeDtypeStruct((DEPTH,) + r.shape, r.dtype) for r in reds],
        in_specs=[ANY] * n, out_specs=[ANY] * n,
        scratch_shapes=[pltpu.SemaphoreType.DMA((n,)), pltpu.SemaphoreType.DMA((n,)), pltpu.SemaphoreType.DMA((n,))],
    )(*reds)


def small_allreduce(v):
    m, n = v.shape

    def body(x_ref, sum_ref, all_ref, send_sems, recv_sems, local_sem):
        x, y, c, chips = _place()
        me, sibling = (x, y, c), (x, y, 1 - c)

        def rows(px, py, pc):
            return all_ref.at[pl.ds((4 * px + 2 * py + pc) * m, m), :]

        def copy(k, block, to, src=None):
            return pltpu.make_async_remote_copy(src_ref=rows(*block) if src is None else src, dst_ref=rows(*block),
                                                send_sem=send_sems.at[k], recv_sem=recv_sems.at[k],
                                                device_id=to, device_id_type=MESH)

        mine = pltpu.make_async_copy(x_ref, rows(*me), local_sem)
        mine.start()
        first = [copy(0, me, sibling, src=x_ref)]
        first += [copy(1 + j, me, (*chip, c), src=x_ref) for j, chip in enumerate(chips)]
        for cp in first:
            cp.start()
        passed = [copy(4 + j, (*chip, c), sibling) for j, chip in enumerate(chips)]
        for j, chip in enumerate(chips):
            copy(1 + j, (*chip, c), me).wait_recv()
            passed[j].start()
        copy(0, sibling, me).wait_recv()
        for j, chip in enumerate(chips):
            copy(4 + j, (*chip, 1 - c), me).wait_recv()
        for cp in first + passed:
            cp.wait_send()
        mine.wait()
        acc = all_ref[0:m, :]
        for dev in range(1, 8):
            acc = acc + all_ref[dev * m:(dev + 1) * m, :]
        sum_ref[...] = acc

    vm = pl.BlockSpec(memory_space=pltpu.VMEM)
    return pl.pallas_call(
        body, name="small_allreduce",
        out_shape=[jax.ShapeDtypeStruct((m, n), F32), jax.ShapeDtypeStruct((8 * m, n), F32)],
        in_specs=[vm], out_specs=[vm, vm],
        scratch_shapes=[pltpu.SemaphoreType.DMA((7,)), pltpu.SemaphoreType.DMA((7,)), pltpu.SemaphoreType.DMA],
    )(v)[0]


def reduce_scatter_grads(names, gs):
    recv = grads_pair_exchange(gs)
    parts = [grads_pair_sum("grads_pair_sum_" + n, g, r) for n, g, r in zip(names, gs, recv)]
    recv2 = grads_chip_exchange(parts)
    reds = [grads_chip_sum("grads_chip_sum_" + n, g, r, r2) for n, g, r, r2 in zip(names, gs, recv, recv2)]
    return grads_pair_gather(reds)


def adamw(name, w, g, m, v, tr):
    r, cols = w.shape

    def body(w_ref, g_ref, m_ref, v_ref, d_ref, nm_ref, nv_ref):
        gv = g_ref[...]
        nm = ADAM_B1 * m_ref[...] + (1.0 - ADAM_B1) * gv
        nv = ADAM_B2 * v_ref[...] + (1.0 - ADAM_B2) * (gv * gv)
        m_hat = nm / (1.0 - ADAM_B1 ** ADAM_STEP)
        v_hat = nv / (1.0 - ADAM_B2 ** ADAM_STEP)
        d_ref[...] = -ADAM_LR * (m_hat / (jnp.sqrt(v_hat) + ADAM_EPS) + ADAM_WD * w_ref[...])
        nm_ref[...] = nm
        nv_ref[...] = nv

    spec = pl.BlockSpec((tr, cols), lambda i: (i, 0))
    return pl.pallas_call(
        body, name=name, grid=(r // tr,), in_specs=[spec] * 4, out_specs=[spec] * 3,
        out_shape=[jax.ShapeDtypeStruct((r, cols), F32)] * 3, compiler_params=_cparams(1),
    )(w, g, m, v)


ADAM_ROWS = {"w_ada": 512, "w_in": 256, "dw_w": 62, "pw2_w": 256, "sconv_w": 8, "w_proj_a": 512, "w_proj_b": 512,
             "w_proj_c": 512, "w_out": 256}

WEIGHT_NAMES = ("w_ada", "b_ada", "norm_g", "w_in", "q_norm_g", "k_norm_g", "sinks", "dw_w", "dw_b", "ln_g", "ln_b",
                "pw2_w", "pw2_b", "sconv_w", "a_log", "dt_bias", "dn_norm_g", "w_proj_a", "w_proj_b", "w_proj_c", "w_out")


def kernel(x, c, w_ada, b_ada, norm_g, w_in, q_norm_g, k_norm_g, sinks, dw_w, dw_b, ln_g, ln_b, pw2_w, pw2_b, sconv_w, a_log, dt_bias, dn_norm_g, w_proj_a, w_proj_b, w_proj_c, w_out, loss_target, m_w_ada, m_b_ada, m_norm_g, m_w_in, m_q_norm_g, m_k_norm_g, m_sinks, m_dw_w, m_dw_b, m_ln_g, m_ln_b, m_pw2_w, m_pw2_b, m_sconv_w, m_a_log, m_dt_bias, m_dn_norm_g, m_w_proj_a, m_w_proj_b, m_w_proj_c, m_w_out, v_w_ada, v_b_ada, v_norm_g, v_w_in, v_q_norm_g, v_k_norm_g, v_sinks, v_dw_w, v_dw_b, v_ln_g, v_ln_b, v_pw2_w, v_pw2_b, v_sconv_w, v_a_log, v_dt_bias, v_dn_norm_g, v_w_proj_a, v_w_proj_b, v_w_proj_c, v_w_out):
    args = dict(locals())
    w = {n: args[n] for n in WEIGHT_NAMES}
    mom = {n: args["m_" + n] for n in WEIGHT_NAMES}
    var = {n: args["v_" + n] for n in WEIGHT_NAMES}

    gathered = weights_allgather([w[n] if n in GATHER_F32 else w[n].astype(BF16) for n in SHARDED])
    gathered = dict(zip(SHARDED, gathered))
    layers = []
    for l in range(DEPTH):
        lw = {n: w[n][l] for n in SMALL}
        for n, axis in SHARDED.items():
            lw[n] = gathered[n][:, l] if n == "w_in" else _join_layer(gathered[n][:, l], axis)
        layers.append(prep_layer(lw))

    c8 = jnp.tile(c, (8, 1))
    act, saved = x[0], []
    for l in range(DEPTH):
        act, s = layer_fwd(str(l), act, c8, layers[l])
        saved.append(s)
    dact, loss_part = loss_head("loss_head", act, loss_target[0], TM)
    loss = lax.psum(loss_part[0, 0], ("x", "y", "c"))
    layer_grads = [None] * DEPTH
    for l in reversed(range(DEPTH)):
        dact, layer_grads[l] = layer_bwd(str(l), dact, c8, layers[l], saved[l])

    by_chip = [jnp.stack([layer_grads[l][n] if n == "w_in" else _split_layer(layer_grads[l][n], SHARDED[n])
                          for l in range(DEPTH)], axis=1) for n in REDUCE_BIG]
    final_grads = dict(zip(REDUCE_BIG, reduce_scatter_grads(REDUCE_BIG, by_chip)))
    small_names = SMALL + GATHER_F32
    small_shapes = {n: (DEPTH,) + layer_grads[0][n].shape for n in small_names}
    small_full = {n: jnp.stack([layer_grads[l][n] for l in range(DEPTH)]) for n in small_names}
    small_sum = unpack_small(small_allreduce(pack_small(small_full, small_names, SMALL_GRAD_ROWS)), small_names, small_shapes)
    chip = 2 * lax.axis_index("x") + lax.axis_index("y")
    for n in GATHER_F32:
        width = w[n].shape[2]
        final_grads[n] = lax.dynamic_slice_in_dim(small_sum[n], chip * width, width, axis=2)
    final_grads.update({n: small_sum[n] for n in SMALL})
    small_grads = pack_small(final_grads, SMALL, SMALL_ROWS)

    delta, new_m, new_v = {}, {}, {}
    for n in SHARDED:
        shp = w[n].shape
        two_d = lambda a, shp=shp: a.reshape(shp[0] * shp[1], shp[2])
        d, nm, nv = adamw("adamw_" + n, two_d(w[n]), two_d(final_grads[n]), two_d(mom[n]), two_d(var[n]), ADAM_ROWS[n])
        delta[n], new_m[n], new_v[n] = d.reshape(shp), nm.reshape(shp), nv.reshape(shp)
    d, nm, nv = adamw("adamw_small", pack_small(w, SMALL, SMALL_ROWS), small_grads, pack_small(mom, SMALL, SMALL_ROWS),
                      pack_small(var, SMALL, SMALL_ROWS), SMALL_ROWS)
    delta.update(unpack_small(d, SMALL, small_shapes))
    new_m.update(unpack_small(nm, SMALL, small_shapes))
    new_v.update(unpack_small(nv, SMALL, small_shapes))

    return (loss, dact[None], *[final_grads[n] for n in WEIGHT_NAMES], *[delta[n] for n in WEIGHT_NAMES],
            *[new_m[n] for n in WEIGHT_NAMES], *[new_v[n] for n in WEIGHT_NAMES])
```

```python
import functools

import numpy as np
import jax
import jax.numpy as jnp
from jax import lax
from jax.experimental import pallas as pl
from jax.experimental.pallas import tpu as pltpu

F32 = jnp.float32
BF16 = jnp.bfloat16
MESH = pl.DeviceIdType.MESH

D_MODEL = 1024
DEPTH = 2
ATT_HEADS = 8
ATT_HEAD_DIM = 64
WINDOW = 128
CONV_K = 31
DN_HEADS = 4
DN_CONV_K = 4
DN_CHUNK = 64
EPS = 1e-6
NEG_INF = -1e30
N_CHIPS = 4
D_IN = 7944

ADAM_LR = 0.001
ADAM_B1 = 0.9
ADAM_B2 = 0.999
ADAM_EPS = 1e-08
ADAM_WD = 0.01
ADAM_STEP = 10

VMEM_LIMIT = 56 * 1024 * 1024

P_QA, P_ZA, P_GLU, P_ZB, P_ZC, P_MG, P_QKV, P_KA, P_VA, P_AB, P_TOTAL = (
    0, 512, 1024, 2048, 2560, 3072, 6144, 7680, 7808, 7936, 8064)
HEAD_ORDER = (0, 4, 1, 5, 2, 6, 3, 7)


def _in_pieces():
    p = [(0 + 64 * h, 64) for h in HEAD_ORDER]
    p += [(768 + 64 * h, 64) for h in HEAD_ORDER]
    for g in range(4):
        p += [(1280 + 128 * g, 128), (1792 + 128 * g, 128)]
    p += [(2304, 512), (4360, 512), (4872, 3072), (2816, 1536), (512, 128), (640, 128), (4352, 8)]
    return p


def _perm_heads_rows(w):
    return jnp.concatenate([w[64 * h:64 * h + 64] for h in HEAD_ORDER], axis=0)


def _unperm_heads_rows(w):
    inv = [HEAD_ORDER.index(h) for h in range(8)]
    return jnp.concatenate([w[64 * s:64 * s + 64] for s in inv], axis=0)


def _split_bf16(a, terms):
    out, rest = [], a.astype(F32)
    for _ in range(terms - 1):
        out.append(rest.astype(BF16))
        rest = rest - out[-1].astype(F32)
    return out + [rest.astype(BF16)]


def _dot(a, b, dims, exact):
    d = lambda p, q: lax.dot_general(p, q, (dims, ((), ())), preferred_element_type=F32)
    if exact:
        (ah, al), (bh, bl) = _split_bf16(a, 2), _split_bf16(b, 2)
        return d(ah, bh) + (d(ah, bl) + d(al, bh))
    return d(a.astype(BF16), b.astype(BF16))


def _make_mm(exact):
    @jax.custom_vjp
    def nn(a, b):
        return _dot(a, b, ((1,), (0,)), exact)

    @jax.custom_vjp
    def nt(a, b):
        return _dot(a, b, ((1,), (1,)), exact)

    @jax.custom_vjp
    def tn(a, b):
        return _dot(a, b, ((0,), (0,)), exact)

    nn.defvjp(lambda a, b: (nn(a, b), (a, b)),
              lambda r, g: (nt(g, r[1]).astype(r[0].dtype), tn(r[0], g).astype(r[1].dtype)))
    nt.defvjp(lambda a, b: (nt(a, b), (a, b)),
              lambda r, g: (nn(g, r[1]).astype(r[0].dtype), tn(g, r[0]).astype(r[1].dtype)))
    tn.defvjp(lambda a, b: (tn(a, b), (a, b)),
              lambda r, g: (nt(r[1], g).astype(r[0].dtype), nn(r[0], g).astype(r[1].dtype)))
    return nn, nt, tn


mm, mm_nt, mm_tn = _make_mm(False)
xmm, xmm_nt, xmm_tn = _make_mm(True)


@jax.custom_vjp
def sel_mm(m, g):
    mb = m.astype(BF16)
    parts = [jnp.dot(mb, p, preferred_element_type=F32) for p in _split_bf16(g, 3)]
    return parts[0] + (parts[1] + parts[2])


def _sel_mm_bwd(m, dy):
    mb = m.astype(BF16)
    parts = [lax.dot_general(mb, p, (((0,), (0,)), ((), ())), preferred_element_type=F32) for p in _split_bf16(dy, 3)]
    return jnp.zeros_like(m), parts[0] + (parts[1] + parts[2])


sel_mm.defvjp(lambda m, g: (sel_mm(m, g), m), _sel_mm_bwd)


@jax.custom_vjp
def tri_inv(a):
    n = a.shape[0]
    eye = jnp.where(lax.broadcasted_iota(jnp.int32, (n, n), 0) == lax.broadcasted_iota(jnp.int32, (n, n), 1), 1.0, 0.0)
    t = eye - a
    pw = a
    for _ in range(5):
        pw = xmm(pw, pw)
        t = t + xmm(t, pw)
    return t


def _tri_inv_bwd(t, dt):
    return (-xmm_tn(t, xmm_nt(dt, t)),)


tri_inv.defvjp(lambda a: (tri_inv(a),) * 2, _tri_inv_bwd)


def _sigmoid(x):
    return 1.0 / (1.0 + jnp.exp(-x))


def _silu(x):
    return x * _sigmoid(x)


def _softplus(x):
    return jnp.maximum(x, 0.0) + jnp.log(1.0 + jnp.exp(-jnp.abs(x)))


def _cparams(n_grid):
    return pltpu.CompilerParams(dimension_semantics=("arbitrary",) * n_grid, vmem_limit_bytes=VMEM_LIMIT)


def _row_spec(tm, width, colblk):
    return pl.BlockSpec((tm, width), lambda i, cb=colblk: (i, cb))


def _const_spec(shape):
    nd = len(shape)
    return pl.BlockSpec(tuple(shape), lambda i, nd=nd: (0,) * nd)


def rowwise_fwd(name, f, rows, consts, outs, tm):
    n_r, n_c = len(rows), len(consts)
    t = rows[0][0].shape[0]

    def body(*refs):
        vals = [r[...] for r in refs[:n_r + n_c]]
        res = f(*vals)
        if not isinstance(res, (tuple, list)):
            res = (res,)
        for o_ref, v in zip(refs[n_r + n_c:], res):
            o_ref[...] = v.astype(o_ref.dtype)

    return pl.pallas_call(
        body, name=name, grid=(t // tm,),
        in_specs=[_row_spec(tm, w, cb) for _, w, cb in rows] + [_const_spec(c.shape) for c in consts],
        out_specs=[_row_spec(tm, w, 0) for w, _ in outs],
        out_shape=[jax.ShapeDtypeStruct((t, w), dt) for w, dt in outs],
        compiler_params=_cparams(1),
    )(*[a for a, _, _ in rows], *consts)


def rowwise_bwd(name, f, rows, consts, cts, row_grad_dtypes, tm):
    n_r, n_c, n_ct = len(rows), len(consts), len(cts)
    t = rows[0][0].shape[0]
    keep = [k for k, dt in enumerate(row_grad_dtypes) if dt is not None]

    def body(*refs):
        ins = [r[...].astype(F32) for r in refs[:n_r + n_c]]
        g_out = [r[...].astype(F32) for r in refs[n_r + n_c:n_r + n_c + n_ct]]
        out_refs = refs[n_r + n_c + n_ct:]

        def fw(*a):
            res = f(*a)
            return tuple(res) if isinstance(res, (tuple, list)) else (res,)

        _, vjp = jax.vjp(fw, *ins)
        grads = vjp(tuple(g_out))
        for o_ref, k in zip(out_refs[:len(keep)], keep):
            o_ref[...] = grads[k].astype(o_ref.dtype)
        first = pl.program_id(0) == 0
        for o_ref, g in zip(out_refs[len(keep):], grads[n_r:]):
            @pl.when(first)
            def _(o_ref=o_ref, g=g):
                o_ref[...] = g

            @pl.when(jnp.logical_not(first))
            def _(o_ref=o_ref, g=g):
                o_ref[...] += g

    return pl.pallas_call(
        body, name=name, grid=(t // tm,),
        in_specs=[_row_spec(tm, w, cb) for _, w, cb in rows] + [_const_spec(c.shape) for c in consts]
        + [_row_spec(tm, w, cb) for _, w, cb in cts],
        out_specs=[_row_spec(tm, rows[k][1], 0) for k in keep] + [_const_spec(c.shape) for c in consts],
        out_shape=[jax.ShapeDtypeStruct((t, rows[k][1]), row_grad_dtypes[k]) for k in keep]
        + [jax.ShapeDtypeStruct(c.shape, F32) for c in consts],
        compiler_params=_cparams(1),
    )(*[a for a, _, _ in rows], *consts, *[a for a, _, _ in cts])


def f_norm_mod(x, g, scale, shift):
    y = x * lax.rsqrt(jnp.mean(x * x, axis=-1, keepdims=True) + EPS) * g
    return y * (1.0 + scale) + shift


def f_conf_tail(u, zb, ln_g, ln_b, pw2_w, pw2_b):
    mu = jnp.mean(u, axis=-1, keepdims=True)
    xc = u - mu
    var = jnp.mean(xc * xc, axis=-1, keepdims=True)
    y = _silu(xc * lax.rsqrt(var + EPS) * ln_g + ln_b)
    return (mm(y, pw2_w) + pw2_b) * _silu(zb)


def f_merge(ya, yb, yc, mg, x, gate, wpa, wpb, wpc, wout):
    d = D_MODEL
    merged = (_sigmoid(mg[:, :d]) * mm(ya, wpa) + _sigmoid(mg[:, d:2 * d]) * mm(yb, wpb)
              + _sigmoid(mg[:, 2 * d:]) * mm(yc, wpc))
    return x + gate * mm(merged, wout)


def matmul_nn(name, a, b, out_dtype, tm, tn, tk):
    m, k = a.shape
    n = b.shape[1]
    nk = k // tk

    def body(a_ref, b_ref, o_ref, acc_ref):
        kk = pl.program_id(2)
        part = jnp.dot(a_ref[...].astype(BF16), b_ref[...].astype(BF16), preferred_element_type=F32)

        @pl.when(kk == 0)
        def _():
            acc_ref[...] = part

        @pl.when(kk > 0)
        def _():
            acc_ref[...] += part

        @pl.when(kk == nk - 1)
        def _():
            o_ref[...] = acc_ref[...].astype(o_ref.dtype)

    return pl.pallas_call(
        body, name=name, grid=(m // tm, n // tn, nk),
        in_specs=[pl.BlockSpec((tm, tk), lambda i, j, kk: (i, kk)), pl.BlockSpec((tk, tn), lambda i, j, kk: (kk, j))],
        out_specs=pl.BlockSpec((tm, tn), lambda i, j, kk: (i, j)),
        out_shape=jax.ShapeDtypeStruct((m, n), out_dtype),
        scratch_shapes=[pltpu.VMEM((tm, tn), F32)],
        compiler_params=_cparams(3),
    )(a, b)


def matmul_tn(name, a, b, ta, tn, tm):
    m, k = a.shape
    n = b.shape[1]
    nm = m // tm

    def body(a_ref, b_ref, o_ref):
        mm_ = pl.program_id(2)
        part = lax.dot_general(a_ref[...].astype(BF16), b_ref[...].astype(BF16), (((0,), (0,)), ((), ())),
                               preferred_element_type=F32)

        @pl.when(mm_ == 0)
        def _():
            o_ref[...] = part

        @pl.when(mm_ > 0)
        def _():
            o_ref[...] += part

    return pl.pallas_call(
        body, name=name, grid=(k // ta, n // tn, nm),
        in_specs=[pl.BlockSpec((tm, ta), lambda i, j, r: (r, i)), pl.BlockSpec((tm, tn), lambda i, j, r: (r, j))],
        out_specs=pl.BlockSpec((ta, tn), lambda i, j, r: (i, j)),
        out_shape=jax.ShapeDtypeStruct((k, n), F32),
        compiler_params=_cparams(3),
    )(a, b)


def ada_fwd(name, c8, w_ada, b_ada):
    def body(c_ref, w_ref, b_ref, o_ref):
        o_ref[...] = mm(_silu(c_ref[...]), w_ref[...]) + b_ref[...]

    return pl.pallas_call(
        body, name=name, out_shape=jax.ShapeDtypeStruct((8, 3 * D_MODEL), F32),
        compiler_params=pltpu.CompilerParams(vmem_limit_bytes=VMEM_LIMIT),
    )(c8, w_ada, b_ada)


def ada_bwd(name, c8, dmod8):
    tn = 768

    def body(c_ref, d_ref, o_ref):
        row0 = lax.broadcasted_iota(jnp.int32, (8, 1), 0) == 0
        sc = jnp.where(row0, _silu(c_ref[...]), 0.0)
        o_ref[...] = mm_tn(sc, d_ref[...])

    return pl.pallas_call(
        body, name=name, grid=(3 * D_MODEL // tn,),
        in_specs=[pl.BlockSpec((8, D_MODEL), lambda j: (0, 0)), pl.BlockSpec((8, tn), lambda j: (0, j))],
        out_specs=pl.BlockSpec((D_MODEL, tn), lambda j: (0, j)),
        out_shape=jax.ShapeDtypeStruct((D_MODEL, 3 * D_MODEL), F32),
        compiler_params=_cparams(1),
    )(c8, dmod8)


def _f_attn(first_block, q, za, kc, vc, kp, vp, qg, kg, sinks):
    w = WINDOW
    lane = lax.broadcasted_iota(jnp.int32, (1, 128), 1)
    halves = [lane < 64, lane >= 64]

    def rms_halves(x, g):
        x2 = x * x
        s0 = jnp.sum(jnp.where(halves[0], x2, 0.0), axis=-1, keepdims=True)
        s1 = jnp.sum(jnp.where(halves[1], x2, 0.0), axis=-1, keepdims=True)
        r = jnp.where(halves[0], lax.rsqrt(s0 / 64.0 + EPS), lax.rsqrt(s1 / 64.0 + EPS))
        return x * r * g

    kcat = rms_halves(jnp.concatenate([kp, kc], axis=0), kg)
    vcat = jnp.concatenate([vp, vc], axis=0)
    qi = lax.broadcasted_iota(jnp.int32, (w, 2 * w), 0)
    kj = lax.broadcasted_iota(jnp.int32, (w, 2 * w), 1)
    dist = qi + w - kj
    valid = (dist >= 0) & (dist < w) & (jnp.logical_not(first_block) | (kj >= w))
    distf = dist.astype(F32)
    outs = []
    for grp in range(4):
        qn = rms_halves(q[:, 128 * grp:128 * grp + 128], qg) * (ATT_HEAD_DIM ** -0.5)
        o_grp = jnp.zeros((w, 128), F32)
        for half in range(2):
            head = HEAD_ORDER[2 * grp + half]
            slope = 2.0 ** (-8.0 * (head + 1) / ATT_HEADS)
            sink = jnp.sum(jnp.where(lane == head, sinks, 0.0), axis=-1, keepdims=True)
            s = mm_nt(jnp.where(halves[half], qn, 0.0), kcat) - slope * distf
            s = jnp.where(valid, s, NEG_INF)
            m = lax.stop_gradient(jnp.maximum(jnp.max(s, axis=-1, keepdims=True), sink))
            p = jnp.exp(s - m)
            denom = jnp.sum(p, axis=-1, keepdims=True) + jnp.exp(sink - m)
            o_grp = o_grp + mm(p / denom, jnp.where(halves[half], vcat, 0.0))
        outs.append(o_grp)
    return jnp.concatenate(outs, axis=1) * _silu(za)


def attn_fwd(name, proj, qg, kg, sinks):
    t = proj.shape[0]
    nb = t // WINDOW

    def body(q_ref, za_ref, kc_ref, vc_ref, kp_ref, vp_ref, qg_ref, kg_ref, s_ref, o_ref):
        first = pl.program_id(0) == 0
        o_ref[...] = _f_attn(first, q_ref[...], za_ref[...], kc_ref[...], vc_ref[...], kp_ref[...], vp_ref[...],
                             qg_ref[...], kg_ref[...], s_ref[...])

    cur = lambda cb: (lambda i: (i, cb))
    prev = lambda cb: (lambda i: (jnp.maximum(i - 1, 0), cb))
    return pl.pallas_call(
        body, name=name, grid=(nb,),
        in_specs=[pl.BlockSpec((WINDOW, 512), cur(P_QA // 512)), pl.BlockSpec((WINDOW, 512), cur(P_ZA // 512)),
                  pl.BlockSpec((WINDOW, 128), cur(P_KA // 128)), pl.BlockSpec((WINDOW, 128), cur(P_VA // 128)),
                  pl.BlockSpec((WINDOW, 128), prev(P_KA // 128)), pl.BlockSpec((WINDOW, 128), prev(P_VA // 128)),
                  _const_spec((1, 128)), _const_spec((1, 128)), _const_spec((1, 128))],
        out_specs=pl.BlockSpec((WINDOW, 512), lambda i: (i, 0)),
        out_shape=jax.ShapeDtypeStruct((t, 512), F32),
        compiler_params=_cparams(1),
    )(proj, proj, proj, proj, proj, proj, qg, kg, sinks)


def attn_bwd(name, proj, qg, kg, sinks, dya):
    t = proj.shape[0]
    nb = t // WINDOW

    def body(q_ref, za_ref, kc_ref, vc_ref, kp_ref, vp_ref, qg_ref, kg_ref, s_ref, dy_ref,
             dqz_ref, dkv_ref, dqg_ref, dkg_ref, ds_ref, carry_ref):
        j = pl.program_id(0)
        first = j == nb - 1

        @pl.when(j == 0)
        def _():
            carry_ref[...] = jnp.zeros_like(carry_ref)
            dqg_ref[...] = jnp.zeros_like(dqg_ref)
            dkg_ref[...] = jnp.zeros_like(dkg_ref)
            ds_ref[...] = jnp.zeros_like(ds_ref)

        ins = [r[...] for r in (q_ref, za_ref, kc_ref, vc_ref, kp_ref, vp_ref, qg_ref, kg_ref, s_ref)]
        _, vjp = jax.vjp(functools.partial(_f_attn, first), *ins)
        dq, dza, dkc, dvc, dkp, dvp, dqg, dkg, dsk = vjp(dy_ref[...])
        dqz_ref[:, 0:512] = dq.astype(dqz_ref.dtype)
        dqz_ref[:, 512:1024] = dza.astype(dqz_ref.dtype)
        dkv_ref[:, 0:128] = (dkc + carry_ref[0]).astype(dkv_ref.dtype)
        dkv_ref[:, 128:256] = (dvc + carry_ref[1]).astype(dkv_ref.dtype)
        carry_ref[0] = dkp
        carry_ref[1] = dvp
        dqg_ref[...] += dqg
        dkg_ref[...] += dkg
        ds_ref[...] += dsk

    cur = lambda cb: (lambda j: (nb - 1 - j, cb))
    prev = lambda cb: (lambda j: (jnp.maximum(nb - 2 - j, 0), cb))
    return pl.pallas_call(
        body, name=name, grid=(nb,),
        in_specs=[pl.BlockSpec((WINDOW, 512), cur(P_QA // 512)), pl.BlockSpec((WINDOW, 512), cur(P_ZA // 512)),
                  pl.BlockSpec((WINDOW, 128), cur(P_KA // 128)), pl.BlockSpec((WINDOW, 128), cur(P_VA // 128)),
                  pl.BlockSpec((WINDOW, 128), prev(P_KA // 128)), pl.BlockSpec((WINDOW, 128), prev(P_VA // 128)),
                  _const_spec((1, 128)), _const_spec((1, 128)), _const_spec((1, 128)),
                  pl.BlockSpec((WINDOW, 512), cur(0))],
        out_specs=[pl.BlockSpec((WINDOW, 1024), cur(0)), pl.BlockSpec((WINDOW, 256), cur(0)),
                   _const_spec((1, 128)), _const_spec((1, 128)), _const_spec((1, 128))],
        out_shape=[jax.ShapeDtypeStruct((t, 1024), BF16), jax.ShapeDtypeStruct((t, 256), BF16),
                   jax.ShapeDtypeStruct((1, 128), F32), jax.ShapeDtypeStruct((1, 128), F32),
                   jax.ShapeDtypeStruct((1, 128), F32)],
        scratch_shapes=[pltpu.VMEM((2, WINDOW, 128), F32)],
        compiler_params=_cparams(1),
    )(proj, proj, proj, proj, proj, proj, qg, kg, sinks, dya)


CONV_ROWS = 256


def _conv_taps(src_ref, w_ref, n_taps, base, t):
    for r0 in range(0, t, CONV_ROWS):
        acc = w_ref[0:1, :] * src_ref[pl.ds(r0 + base, CONV_ROWS), :]
        for k in range(1, n_taps):
            acc = acc + w_ref[k:k + 1, :] * src_ref[pl.ds(r0 + base + k, CONV_ROWS), :]
        yield r0, acc


def _conv_wgrad(dy_ref, src_ref, n_taps, base, t, dy_base=0):
    out = []
    for k in range(n_taps):
        acc = jnp.zeros((8, 128), F32)
        for r0 in range(0, t, CONV_ROWS):
            prod = dy_ref[pl.ds(r0 + dy_base, CONV_ROWS), :] * src_ref[pl.ds(r0 + base + k, CONV_ROWS), :]
            acc = acc + jnp.sum(prod.reshape(CONV_ROWS // 8, 8, 128), axis=0)
        out.append(jnp.sum(acc, axis=0, keepdims=True))
    return out


def glu_conv_fwd(name, proj, w32, bias):
    t = proj.shape[0]
    pad = 32

    def body(x_ref, w_ref, b_ref, o_ref, u_ref):
        u_ref[0:pad, :] = jnp.zeros((pad, 128), F32)
        u_ref[pad:pad + t, :] = x_ref[:, 0:128] * _sigmoid(x_ref[:, 128:256])
        for r0, acc in _conv_taps(u_ref, w_ref, CONV_K, pad - (CONV_K - 1), t):
            o_ref[pl.ds(r0, CONV_ROWS), :] = acc + b_ref[...]

    return pl.pallas_call(
        body, name=name, grid=(4,),
        in_specs=[pl.BlockSpec((t, 256), lambda cb: (0, P_GLU // 256 + cb)), pl.BlockSpec((32, 128), lambda cb: (0, cb)),
                  pl.BlockSpec((1, 128), lambda cb: (0, cb))],
        out_specs=pl.BlockSpec((t, 128), lambda cb: (0, cb)),
        out_shape=jax.ShapeDtypeStruct((t, 512), F32),
        scratch_shapes=[pltpu.VMEM((t + pad, 128), F32)],
        compiler_params=_cparams(1),
    )(proj, w32, bias)


def glu_conv_bwd(name, proj, w32, dub):
    t = proj.shape[0]
    pad = 32
    k1 = CONV_K - 1

    def body(x_ref, w_ref, dy_ref, dx_ref, dw_ref, db_ref, u_ref, dyp_ref, wrev_ref):
        val = x_ref[:, 0:128]
        sg = _sigmoid(x_ref[:, 128:256])
        u_ref[0:pad, :] = jnp.zeros((pad, 128), F32)
        u_ref[pad:pad + t, :] = val * sg
        dyp_ref[0:t, :] = dy_ref[...]
        dyp_ref[t:t + pad, :] = jnp.zeros((pad, 128), F32)
        for k in range(CONV_K):
            wrev_ref[k:k + 1, :] = w_ref[k1 - k:k1 - k + 1, :]
        wrev_ref[CONV_K:32, :] = jnp.zeros((32 - CONV_K, 128), F32)
        for r0, du in _conv_taps(dyp_ref, wrev_ref, CONV_K, 0, t):
            v = x_ref[pl.ds(r0, CONV_ROWS), 0:128]
            s = _sigmoid(x_ref[pl.ds(r0, CONV_ROWS), 128:256])
            dx_ref[pl.ds(r0, CONV_ROWS), 0:128] = (du * s).astype(dx_ref.dtype)
            dx_ref[pl.ds(r0, CONV_ROWS), 128:256] = (du * v * s * (1.0 - s)).astype(dx_ref.dtype)
        dws = _conv_wgrad(dyp_ref, u_ref, CONV_K, pad - k1, t)
        for k in range(CONV_K):
            dw_ref[k:k + 1, :] = dws[k]
        dw_ref[CONV_K:32, :] = jnp.zeros((32 - CONV_K, 128), F32)
        db_ref[...] = jnp.sum(dy_ref[...], axis=0, keepdims=True)

    return pl.pallas_call(
        body, name=name, grid=(4,),
        in_specs=[pl.BlockSpec((t, 256), lambda cb: (0, P_GLU // 256 + cb)), pl.BlockSpec((32, 128), lambda cb: (0, cb)),
                  pl.BlockSpec((t, 128), lambda cb: (0, cb))],
        out_specs=[pl.BlockSpec((t, 256), lambda cb: (0, cb)), pl.BlockSpec((32, 128), lambda cb: (0, cb)),
                   pl.BlockSpec((1, 128), lambda cb: (0, cb))],
        out_shape=[jax.ShapeDtypeStruct((t, 1024), BF16), jax.ShapeDtypeStruct((32, 512), F32),
                   jax.ShapeDtypeStruct((1, 512), F32)],
        scratch_shapes=[pltpu.VMEM((t + pad, 128), F32), pltpu.VMEM((t + pad, 128), F32), pltpu.VMEM((32, 128), F32)],
        compiler_params=_cparams(1),
    )(proj, w32, dub)


def sconv_fwd(name, proj, w8):
    t = proj.shape[0]
    pad = 8
    k1 = DN_CONV_K - 1

    def body(x_ref, w_ref, o_ref, xp_ref):
        xp_ref[0:pad, :] = jnp.zeros((pad, 128), F32)
        xp_ref[pad:pad + t, :] = x_ref[...]
        for r0, acc in _conv_taps(xp_ref, w_ref, DN_CONV_K, pad - k1, t):
            o_ref[pl.ds(r0, CONV_ROWS), :] = _silu(acc)

    return pl.pallas_call(
        body, name=name, grid=(12,),
        in_specs=[pl.BlockSpec((t, 128), lambda cb: (0, P_QKV // 128 + cb)), pl.BlockSpec((8, 128), lambda cb: (0, cb))],
        out_specs=pl.BlockSpec((t, 128), lambda cb: (0, cb)),
        out_shape=jax.ShapeDtypeStruct((t, 1536), F32),
        scratch_shapes=[pltpu.VMEM((t + pad, 128), F32)],
        compiler_params=_cparams(1),
    )(proj, w8)


def sconv_bwd(name, proj, w8, dqkv):
    t = proj.shape[0]
    pad = 8
    k1 = DN_CONV_K - 1

    def body(x_ref, w_ref, dy_ref, dx_ref, dw_ref, xp_ref, dpp_ref, wrev_ref):
        xp_ref[0:pad, :] = jnp.zeros((pad, 128), F32)
        xp_ref[pad:pad + t, :] = x_ref[...]
        for r0, pre in _conv_taps(xp_ref, w_ref, DN_CONV_K, pad - k1, t):
            s = _sigmoid(pre)
            dpp_ref[pl.ds(r0, CONV_ROWS), :] = dy_ref[pl.ds(r0, CONV_ROWS), :] * (s * (1.0 + pre * (1.0 - s)))
        dpp_ref[t:t + pad, :] = jnp.zeros((pad, 128), F32)
        for k in range(DN_CONV_K):
            wrev_ref[k:k + 1, :] = w_ref[k1 - k:k1 - k + 1, :]
        wrev_ref[DN_CONV_K:8, :] = jnp.zeros((8 - DN_CONV_K, 128), F32)
        for r0, dx in _conv_taps(dpp_ref, wrev_ref, DN_CONV_K, 0, t):
            dx_ref[pl.ds(r0, CONV_ROWS), :] = dx.astype(dx_ref.dtype)
        dws = _conv_wgrad(dpp_ref, xp_ref, DN_CONV_K, pad - k1, t)
        for k in range(DN_CONV_K):
            dw_ref[k:k + 1, :] = dws[k]
        dw_ref[DN_CONV_K:8, :] = jnp.zeros((8 - DN_CONV_K, 128), F32)

    return pl.pallas_call(
        body, name=name, grid=(12,),
        in_specs=[pl.BlockSpec((t, 128), lambda cb: (0, P_QKV // 128 + cb)), pl.BlockSpec((8, 128), lambda cb: (0, cb)),
                  pl.BlockSpec((t, 128), lambda cb: (0, cb))],
        out_specs=[pl.BlockSpec((t, 128), lambda cb: (0, cb)), pl.BlockSpec((8, 128), lambda cb: (0, cb))],
        out_shape=[jax.ShapeDtypeStruct((t, 1536), BF16), jax.ShapeDtypeStruct((8, 1536), F32)],
        scratch_shapes=[pltpu.VMEM((t + pad, 128), F32), pltpu.VMEM((t + pad, 128), F32), pltpu.VMEM((8, 128), F32)],
        compiler_params=_cparams(1),
    )(proj, w8, dqkv)


def _f_delta(qkv, ab, zc, s0, s1, s2, s3, a_log, dt_bias, dn_g):
    cs = DN_CHUNK
    n = 2 * cs
    states = (s0, s1, s2, s3)
    lane = lax.broadcasted_iota(jnp.int32, (1, 128), 1)
    ri = lax.broadcasted_iota(jnp.int32, (n, n), 0)
    ci = lax.broadcasted_iota(jnp.int32, (n, n), 1)
    same = (ri // cs) == (ci // cs)
    lower = same & (ri >= ci)
    strict = same & (ri > ci)
    sums = jnp.concatenate([jnp.where(lower, 1.0, 0.0), jnp.where(same, 1.0, 0.0), jnp.where(ci < cs, 1.0, 0.0),
                            jnp.where(ci >= cs, 1.0, 0.0)], axis=0)
    top = lax.broadcasted_iota(jnp.int32, (n, 1), 0) < cs

    def pick(row, idx):
        return jnp.sum(jnp.where(lane == idx, row, 0.0), axis=-1, keepdims=True)

    def l2n(x):
        return x * lax.rsqrt(jnp.sum(x * x, axis=-1, keepdims=True) + EPS)

    ys, new_states = [], []
    for pair in range(2):
        hs = (2 * pair, 2 * pair + 1)
        stack = lambda f: jnp.concatenate([f(hs[0]), f(hs[1])], axis=0)
        qd = l2n(stack(lambda h: qkv[:, 128 * h:128 * h + 128])) * (128 ** -0.5)
        kd = l2n(stack(lambda h: qkv[:, 512 + 128 * h:512 + 128 * h + 128]))
        vd = stack(lambda h: qkv[:, 1024 + 128 * h:1024 + 128 * h + 128])
        beta = _sigmoid(stack(lambda h: pick(ab, 4 + h)))
        g = stack(lambda h: -jnp.exp(pick(a_log, h)) * _softplus(pick(ab, h) + pick(dt_bias, h)))
        g_b = g * jnp.ones((1, n), F32)
        g_sums = sel_mm(sums, g_b)
        gc_col = g_sums[0:n]
        gc_row = gc_col.T
        gl_b = g_sums[n:2 * n]
        g_end = (g_sums[2 * n:3 * n], g_sums[3 * n:])
        decay = jnp.where(lower, jnp.exp(jnp.where(lower, gc_col - gc_row, 0.0)), 0.0)
        kb = kd * beta
        vb = vd * beta
        tmat = tri_inv(jnp.where(strict, mm_nt(kb, kd) * decay, 0.0))
        egc = jnp.exp(gc_col)
        u = mm(tmat, vb)
        wm = mm(tmat, kb * egc)
        intra = jnp.where(lower, mm_nt(qd, kd) * decay, 0.0)
        qe = qd * egc
        ke = kd * jnp.exp(gl_b - gc_col)
        st = (states[hs[0]], states[hs[1]])
        v_new = u - jnp.concatenate([mm(wm[:cs], st[0]), mm(wm[cs:], st[1])], axis=0)
        o = jnp.concatenate([mm(qe[:cs], st[0]), mm(qe[cs:], st[1])], axis=0) + mm(intra, v_new)
        new_states.append(st[0] * jnp.exp(g_end[0]) + mm_tn(jnp.where(top, ke, 0.0), v_new))
        new_states.append(st[1] * jnp.exp(g_end[1]) + mm_tn(jnp.where(top, 0.0, ke), v_new))
        od = o * lax.rsqrt(jnp.mean(o * o, axis=-1, keepdims=True) + EPS) * dn_g
        ys += [od[:cs] * _silu(zc[:, 128 * hs[0]:128 * hs[0] + 128]), od[cs:] * _silu(zc[:, 128 * hs[1]:128 * hs[1] + 128])]
    y = jnp.concatenate([ys[0], ys[1], ys[2], ys[3]], axis=1)
    return (y, *new_states)


def delta_fwd(name, qkv, proj, a_log, dt_bias, dn_g):
    t = qkv.shape[0]
    nc = t // DN_CHUNK

    def body(qkv_ref, ab_ref, zc_ref, al_ref, dt_ref, g_ref, y_ref, ssave_ref, s_ref):
        @pl.when(pl.program_id(0) == 0)
        def _():
            s_ref[...] = jnp.zeros_like(s_ref)

        ssave_ref[0] = s_ref[...]
        st = [s_ref[128 * h:128 * h + 128, :] for h in range(4)]
        y, *ns = _f_delta(qkv_ref[...], ab_ref[...], zc_ref[...], *st, al_ref[...], dt_ref[...], g_ref[...])
        y_ref[...] = y
        for h in range(4):
            s_ref[128 * h:128 * h + 128, :] = ns[h]

    return pl.pallas_call(
        body, name=name, grid=(nc,),
        in_specs=[pl.BlockSpec((DN_CHUNK, 1536), lambda i: (i, 0)), pl.BlockSpec((DN_CHUNK, 128), lambda i: (i, P_AB // 128)),
                  pl.BlockSpec((DN_CHUNK, 512), lambda i: (i, P_ZC // 512)),
                  _const_spec((1, 128)), _const_spec((1, 128)), _const_spec((1, 128))],
        out_specs=[pl.BlockSpec((DN_CHUNK, 512), lambda i: (i, 0)), pl.BlockSpec((1, 512, 128), lambda i: (i, 0, 0))],
        out_shape=[jax.ShapeDtypeStruct((t, 512), F32), jax.ShapeDtypeStruct((nc, 512, 128), F32)],
        scratch_shapes=[pltpu.VMEM((512, 128), F32)],
        compiler_params=_cparams(1),
    )(qkv, proj, proj, a_log, dt_bias, dn_g)


def delta_bwd(name, qkv, proj, ssave, a_log, dt_bias, dn_g, dyc):
    t = qkv.shape[0]
    nc = t // DN_CHUNK

    def body(qkv_ref, ab_ref, zc_ref, ss_ref, al_ref, dt_ref, g_ref, dy_ref,
             dqkv_ref, dab_ref, dzc_ref, dal_ref, ddt_ref, dg_ref, ds_ref):
        @pl.when(pl.program_id(0) == 0)
        def _():
            ds_ref[...] = jnp.zeros_like(ds_ref)
            dal_ref[...] = jnp.zeros_like(dal_ref)
            ddt_ref[...] = jnp.zeros_like(ddt_ref)
            dg_ref[...] = jnp.zeros_like(dg_ref)

        st = [ss_ref[0, 128 * h:128 * h + 128, :] for h in range(4)]
        _, vjp = jax.vjp(_f_delta, qkv_ref[...], ab_ref[...], zc_ref[...], *st, al_ref[...], dt_ref[...], g_ref[...])
        dst = tuple(ds_ref[128 * h:128 * h + 128, :] for h in range(4))
        dqkv, dab, dzc, d0, d1, d2, d3, dal, ddt, dg = vjp((dy_ref[...], *dst))
        dqkv_ref[...] = dqkv
        dab_ref[...] = dab.astype(dab_ref.dtype)
        dzc_ref[...] = dzc.astype(dzc_ref.dtype)
        for h, d in enumerate((d0, d1, d2, d3)):
            ds_ref[128 * h:128 * h + 128, :] = d
        dal_ref[...] += dal
        ddt_ref[...] += ddt
        dg_ref[...] += dg

    rev = lambda cb: (lambda j: (nc - 1 - j, cb))
    return pl.pallas_call(
        body, name=name, grid=(nc,),
        in_specs=[pl.BlockSpec((DN_CHUNK, 1536), rev(0)), pl.BlockSpec((DN_CHUNK, 128), rev(P_AB // 128)),
                  pl.BlockSpec((DN_CHUNK, 512), rev(P_ZC // 512)), pl.BlockSpec((1, 512, 128), lambda j: (nc - 1 - j, 0, 0)),
                  _const_spec((1, 128)), _const_spec((1, 128)), _const_spec((1, 128)),
                  pl.BlockSpec((DN_CHUNK, 512), rev(0))],
        out_specs=[pl.BlockSpec((DN_CHUNK, 1536), rev(0)), pl.BlockSpec((DN_CHUNK, 128), rev(0)),
                   pl.BlockSpec((DN_CHUNK, 512), rev(0)),
                   _const_spec((1, 128)), _const_spec((1, 128)), _const_spec((1, 128))],
        out_shape=[jax.ShapeDtypeStruct((t, 1536), F32), jax.ShapeDtypeStruct((t, 128), BF16),
                   jax.ShapeDtypeStruct((t, 512), BF16),
                   jax.ShapeDtypeStruct((1, 128), F32), jax.ShapeDtypeStruct((1, 128), F32), jax.ShapeDtypeStruct((1, 128), F32)],
        scratch_shapes=[pltpu.VMEM((512, 128), F32)],
        compiler_params=_cparams(1),
    )(qkv, proj, proj, ssave, a_log, dt_bias, dn_g, dyc)


def loss_head(name, y, target, tm):
    t, d = y.shape

    def body(y_ref, t_ref, dy_ref, l_ref):
        err = y_ref[...] - t_ref[...]
        dy_ref[...] = err * (1.0 / d)
        part = 0.5 * jnp.sum(jnp.sum(err * err, axis=-1, keepdims=True) * (1.0 / d), axis=0, keepdims=True)

        @pl.when(pl.program_id(0) == 0)
        def _():
            l_ref[...] = part

        @pl.when(pl.program_id(0) > 0)
        def _():
            l_ref[...] += part

    return pl.pallas_call(
        body, name=name, grid=(t // tm,),
        in_specs=[_row_spec(tm, d, 0), _row_spec(tm, d, 0)],
        out_specs=[_row_spec(tm, d, 0), _const_spec((1, 1))],
        out_shape=[jax.ShapeDtypeStruct((t, d), F32), jax.ShapeDtypeStruct((1, 1), F32)],
        compiler_params=_cparams(1),
    )(y, target)


TM = 512
TM_MERGE = 256
TN_IN = 1152


def _lane_pad(v, n=128):
    return jnp.pad(v.astype(F32), (0, n - v.shape[0]))[None, :]


def f_norm_mod_res(x, g, scale, shift):
    return f_norm_mod(x, g, scale, shift), x


def prep_layer(w):
    p = dict(w)
    p["wp"] = _pad_w_in_from_shards(w["w_in"])
    p["wpt"] = p["wp"].T
    p["wpa"] = _perm_heads_rows(w["w_proj_a"])
    p["dw32"] = jnp.pad(w["dw_w"], ((0, 32 - CONV_K), (0, 0)))
    p["sconv8"] = jnp.pad(w["sconv_w"], ((0, 8 - DN_CONV_K), (0, 0)))
    p["qg"] = jnp.tile(w["q_norm_g"], 2)[None, :]
    p["kg"] = jnp.tile(w["k_norm_g"], 2)[None, :]
    p["sinks128"] = _lane_pad(w["sinks"])
    p["al"] = _lane_pad(w["a_log"])
    p["dtb"] = _lane_pad(w["dt_bias"])
    p["dng"] = w["dn_norm_g"][None, :]
    return p


def layer_fwd(tag, x, c8, p):
    mod = ada_fwd(f"ada_fwd{tag}", c8, p["w_ada"], p["b_ada"][None, :])[0:1]
    d = D_MODEL
    shift, scale, gate = mod[:, :d], mod[:, d:2 * d], mod[:, 2 * d:]
    g = p["norm_g"][None, :]
    (h,) = rowwise_fwd(f"norm_fwd{tag}", f_norm_mod, [(x, d, 0)], [g, scale, shift], [(d, BF16)], TM)
    proj = matmul_nn(f"inproj_fwd{tag}", h, p["wp"], F32, TM, TN_IN, d)
    ya = attn_fwd(f"attn_fwd{tag}", proj, p["qg"], p["kg"], p["sinks128"])
    ub = glu_conv_fwd(f"glu_conv_fwd{tag}", proj, p["dw32"], p["dw_b"][None, :])
    conf_consts = [p["ln_g"][None, :], p["ln_b"][None, :], p["pw2_w"], p["pw2_b"][None, :]]
    (yb,) = rowwise_fwd(f"conf_fwd{tag}", f_conf_tail, [(ub, 512, 0), (proj, 512, P_ZB // 512)], conf_consts, [(512, F32)], TM)
    qkv = sconv_fwd(f"sconv_fwd{tag}", proj, p["sconv8"])
    yc, ssave = delta_fwd(f"delta_fwd{tag}", qkv, proj, p["al"], p["dtb"], p["dng"])
    merge_consts = [gate, p["wpa"], p["w_proj_b"], p["w_proj_c"], p["w_out"]]
    merge_rows = [(ya, 512, 0), (yb, 512, 0), (yc, 512, 0), (proj, 3 * d, P_MG // (3 * d)), (x, d, 0)]
    (xn,) = rowwise_fwd(f"merge_fwd{tag}", f_merge, merge_rows, merge_consts, [(d, F32)], TM_MERGE)
    saved = dict(x=x, h=h, proj=proj, ub=ub, qkv=qkv, ssave=ssave, norm_consts=[g, scale, shift],
                 conf_consts=conf_consts, merge_consts=merge_consts, merge_rows=merge_rows)
    return xn, saved


def layer_bwd(tag, dxn, c8, p, s):
    d = D_MODEL
    proj = s["proj"]
    dya, dyb, dyc, dmg, dgate, dwpa, dwpb, dwpc, dwout = rowwise_bwd(
        f"merge_bwd{tag}", f_merge, s["merge_rows"], s["merge_consts"], [(dxn, d, 0)], [F32, F32, F32, BF16, None], TM_MERGE)
    dqz, dkv, dqg, dkg, dsinks = attn_bwd(f"attn_bwd{tag}", proj, p["qg"], p["kg"], p["sinks128"], dya)
    dub, dzb, dln_g, dln_b, dpw2_w, dpw2_b = rowwise_bwd(
        f"conf_bwd{tag}", f_conf_tail, [(s["ub"], 512, 0), (proj, 512, P_ZB // 512)], s["conf_consts"], [(dyb, 512, 0)],
        [F32, BF16], TM)
    dglu, ddw32, ddw_b = glu_conv_bwd(f"glu_conv_bwd{tag}", proj, p["dw32"], dub)
    dqkv, dab, dzc, dal, ddtb, ddng = delta_bwd(f"delta_bwd{tag}", s["qkv"], proj, s["ssave"], p["al"], p["dtb"], p["dng"], dyc)
    dqkv_pre, dsconv8 = sconv_bwd(f"sconv_bwd{tag}", proj, p["sconv8"], dqkv)
    dproj = jnp.concatenate([dqz, dglu, dzb, dzc, dmg, dqkv_pre, dkv, dab], axis=1)
    dh = matmul_nn(f"inproj_bwd_dh{tag}", dproj, p["wpt"], F32, TM, d, TN_IN)
    dwp = matmul_tn(f"inproj_bwd_dw{tag}", s["h"], dproj, 512, TN_IN, TM)
    dx, dnorm_g, dscale, dshift = rowwise_bwd(
        f"norm_bwd{tag}", f_norm_mod_res, [(s["x"], d, 0)], s["norm_consts"], [(dh, d, 0), (dxn, d, 0)], [F32], TM)
    dmod = jnp.concatenate([dshift, dscale, dgate], axis=1)
    dw_ada = ada_bwd(f"ada_bwd{tag}", c8, jnp.pad(dmod, ((0, 7), (0, 0))))
    grads = dict(
        w_ada=dw_ada, b_ada=dmod[0], norm_g=dnorm_g[0], w_in=_unpad_w_in_to_shards(dwp),
        q_norm_g=dqg[0, :64] + dqg[0, 64:], k_norm_g=dkg[0, :64] + dkg[0, 64:], sinks=dsinks[0, :ATT_HEADS],
        dw_w=ddw32[:CONV_K], dw_b=ddw_b[0], ln_g=dln_g[0], ln_b=dln_b[0], pw2_w=dpw2_w, pw2_b=dpw2_b[0],
        sconv_w=dsconv8[:DN_CONV_K], a_log=dal[0, :DN_HEADS], dt_bias=ddtb[0, :DN_HEADS], dn_norm_g=ddng[0],
        w_proj_a=_unperm_heads_rows(dwpa), w_proj_b=dwpb, w_proj_c=dwpc, w_out=dwout)
    return dx, grads


SHARDED = {"w_ada": 2, "w_in": 2, "dw_w": 2, "pw2_w": 1, "sconv_w": 2, "w_proj_a": 2, "w_proj_b": 2, "w_proj_c": 2,
           "w_out": 1}
GATHER_F32 = ("dw_w", "sconv_w")
REDUCE_BIG = tuple(n for n in SHARDED if n not in GATHER_F32)
SMALL = ("b_ada", "norm_g", "q_norm_g", "k_norm_g", "sinks", "dw_b", "ln_g", "ln_b", "pw2_b", "a_log", "dt_bias",
         "dn_norm_g")
SMALL_ROWS = 104
SMALL_GRAD_ROWS = 448
W_IN_SHARD = D_IN // N_CHIPS
SUM_TILE = 256


def _shard_cols(shards, start, n):
    parts = []
    while n > 0:
        k, o = divmod(start, W_IN_SHARD)
        m = min(n, W_IN_SHARD - o)
        parts.append(shards[k][:, o:o + m])
        start, n = start + m, n - m
    return parts


def _pad_w_in_from_shards(shards):
    parts = []
    for s, n in _in_pieces():
        parts += _shard_cols(shards, s, n)
    parts.append(jnp.zeros((shards.shape[1], P_TOTAL - D_IN), shards.dtype))
    return jnp.concatenate(parts, axis=1)


def _unpad_w_in_to_shards(wp):
    pieces = _in_pieces()
    starts = np.cumsum([0] + [n for _, n in pieces])[:-1]
    order = sorted(range(len(pieces)), key=lambda i: pieces[i][0])
    shards = []
    for k in range(N_CHIPS):
        lo, hi = k * W_IN_SHARD, (k + 1) * W_IN_SHARD
        parts = []
        for i in order:
            s, n = pieces[i]
            a, b = max(s, lo), min(s + n, hi)
            if a < b:
                parts.append(wp[:, int(starts[i]) + a - s:int(starts[i]) + b - s])
        shards.append(jnp.concatenate(parts, axis=1))
    return jnp.stack(shards)


def _join_layer(v, axis):
    if axis == 2:
        return jnp.transpose(v, (1, 0, 2)).reshape(v.shape[1], N_CHIPS * v.shape[2])
    return v.reshape(N_CHIPS * v.shape[1], v.shape[2])


def _split_layer(v, axis):
    a, b = v.shape
    if axis == 2:
        return jnp.transpose(v.reshape(a, N_CHIPS, b // N_CHIPS), (1, 0, 2))
    return v.reshape(N_CHIPS, a // N_CHIPS, b)


def pack_small(vals, names, rows):
    flat = jnp.concatenate([vals[n].astype(F32).reshape(-1) for n in names])
    return jnp.pad(flat, (0, rows * 128 - flat.shape[0])).reshape(rows, 128)


def unpack_small(packed, names, shapes):
    flat = packed.reshape(-1)
    out, off = {}, 0
    for n in names:
        k = int(np.prod(shapes[n]))
        out[n] = flat[off:off + k].reshape(shapes[n])
        off += k
    return out


ANY = pl.BlockSpec(memory_space=pl.ANY)


def _place():
    x, y, c = lax.axis_index("x"), lax.axis_index("y"), lax.axis_index("c")
    chips = [(1 - x, y), (x, 1 - y), (1 - x, 1 - y)]
    return x, y, c, chips


def _remote(src, dst, send_sem, recv_sem, to):
    return pltpu.make_async_remote_copy(src_ref=src, dst_ref=dst, send_sem=send_sem, recv_sem=recv_sem, device_id=to,
                                        device_id_type=MESH)


def weights_allgather(slots):
    n = len(slots)

    def body(*refs):
        out = refs[n:2 * n]
        send_sems, recv_sems = refs[2 * n:]
        x, y, c, chips = _place()
        me, sibling, my_slot = (x, y, c), (x, y, 1 - c), 2 * x + y
        sends = []
        for j, chip in enumerate(chips):
            for t in range(n):
                mine = out[t].at[my_slot, c]
                sends.append(_remote(mine, mine, send_sems.at[t, j], recv_sems.at[t, j], (*chip, c)))
                sends[-1].start()
        for j, chip in enumerate(chips):
            for t in range(n):
                land = out[t].at[2 * chip[0] + chip[1], c]
                _remote(land, land, send_sems.at[t, j], recv_sems.at[t, j], me).wait_recv()
                sends.append(_remote(land, land, send_sems.at[t, 3 + j], recv_sems.at[t, 3 + j], sibling))
                sends[-1].start()
        for j, chip in enumerate(chips):
            for t in range(n):
                land = out[t].at[2 * chip[0] + chip[1], 1 - c]
                _remote(land, land, send_sems.at[t, 3 + j], recv_sems.at[t, 3 + j], me).wait_recv()
        for cp in sends:
            cp.wait_send()

    return pl.pallas_call(
        body, name="weights_allgather", out_shape=[jax.ShapeDtypeStruct(s.shape, s.dtype) for s in slots],
        in_specs=[ANY] * n, out_specs=[ANY] * n, input_output_aliases={t: t for t in range(n)},
        scratch_shapes=[pltpu.SemaphoreType.DMA((n, 6)), pltpu.SemaphoreType.DMA((n, 6))],
    )(*slots)


def grads_pair_exchange(gs):
    n = len(gs)

    def body(*refs):
        g, recv = refs[:n], refs[n:2 * n]
        send_sems, recv_sems = refs[2 * n:]
        x, y, c, _ = _place()
        cps = [_remote(g[t].at[:, 1 - c], recv[t], send_sems.at[t], recv_sems.at[t], (x, y, 1 - c)) for t in range(n)]
        for cp in cps:
            cp.start()
        for cp in cps:
            cp.wait()

    return pl.pallas_call(
        body, name="grads_pair_exchange",
        out_shape=[jax.ShapeDtypeStruct((N_CHIPS,) + g.shape[2:], g.dtype) for g in gs],
        in_specs=[ANY] * n, out_specs=[ANY] * n,
        scratch_shapes=[pltpu.SemaphoreType.DMA((n,)), pltpu.SemaphoreType.DMA((n,))],
    )(*gs)


def grads_pair_sum(name, g, recv):
    _, a, b = recv.shape
    ta = min(a, SUM_TILE)

    def body(a_ref, b_ref, o_ref):
        o_ref[...] = (a_ref[...] + b_ref[...]).astype(o_ref.dtype)

    return pl.pallas_call(
        body, name=name, grid=(N_CHIPS, a // ta),
        in_specs=[pl.BlockSpec((None, None, ta, b), lambda s, i: (s, lax.axis_index("c"), i, 0)),
                  pl.BlockSpec((None, ta, b), lambda s, i: (s, i, 0))],
        out_specs=pl.BlockSpec((None, ta, b), lambda s, i: (s, i, 0)),
        out_shape=jax.ShapeDtypeStruct(recv.shape, BF16),
        compiler_params=_cparams(2),
    )(g, recv)


def grads_chip_exchange(ps):
    n = len(ps)

    def body(*refs):
        p, recv = refs[:n], refs[n:2 * n]
        send_sems, recv_sems = refs[2 * n:]
        x, y, c, chips = _place()
        cps = [_remote(p[t].at[2 * chip[0] + chip[1]], recv[t].at[j], send_sems.at[t, j], recv_sems.at[t, j], (*chip, c))
               for j, chip in enumerate(chips) for t in range(n)]
        for cp in cps:
            cp.start()
        for cp in cps:
            cp.wait()

    return pl.pallas_call(
        body, name="grads_chip_exchange", out_shape=[jax.ShapeDtypeStruct((3,) + p.shape[1:], p.dtype) for p in ps],
        in_specs=[ANY] * n, out_specs=[ANY] * n,
        scratch_shapes=[pltpu.SemaphoreType.DMA((n, 3)), pltpu.SemaphoreType.DMA((n, 3))],
    )(*ps)


def grads_chip_sum(name, g, recv, recv2):
    _, a, b = recv.shape
    ta = min(a, SUM_TILE)
    my_slot = lambda: 2 * lax.axis_index("x") + lax.axis_index("y")

    def body(g_ref, r_ref, r2_ref, o_ref):
        own = g_ref[...] + r_ref[...]
        o_ref[...] = ((own + r2_ref[0].astype(F32)) + r2_ref[1].astype(F32)) + r2_ref[2].astype(F32)

    return pl.pallas_call(
        body, name=name, grid=(a // ta,),
        in_specs=[pl.BlockSpec((None, None, ta, b), lambda i: (my_slot(), lax.axis_index("c"), i, 0)),
                  pl.BlockSpec((None, ta, b), lambda i: (my_slot(), i, 0)),
                  pl.BlockSpec((3, ta, b), lambda i: (0, i, 0))],
        out_specs=pl.BlockSpec((None, ta, b), lambda i: (lax.axis_index("c"), i, 0)),
        out_shape=jax.ShapeDtypeStruct((DEPTH, a, b), F32),
        compiler_params=_cparams(1),
    )(g, recv, recv2)


def grads_pair_gather(reds):
    n = len(reds)

    def body(*refs):
        buf = refs[n:2 * n]
        send_sems, recv_sems = refs[2 * n:]
        x, y, c, _ = _place()
        sibling = (x, y, 1 - c)
        cps = [_remote(buf[t].at[c], buf[t].at[c], send_sems.at[t], recv_sems.at[t], sibling) for t in range(n)]
        for cp in cps:
            cp.start()
        for t in range(n):
            _remote(buf[t].at[c], buf[t].at[1 - c], send_sems.at[t], recv_sems.at[t], sibling).wait_recv()
        for cp in cps:
            cp.wait_send()

    return pl.pallas_call(
        body, name="grads_pair_gather", out_shape=[jax.ShapeDtypeStruct(r.shape, r.dtype) for r in reds],
        in_specs=[ANY] * n, out_specs=[ANY] * n, input_output_aliases={t: t for t in range(n)},
        scratch_shapes=[pltpu.SemaphoreType.DMA((n,)), pltpu.SemaphoreType.DMA((n,))],
    )(*reds)


def small_allreduce(v):
    m, n = v.shape

    def body(x_ref, sum_ref, all_ref, send_sems, recv_sems, local_sem):
        x, y, c, chips = _place()
        me, sibling = (x, y, c), (x, y, 1 - c)

        def rows(px, py, pc):
            return all_ref.at[pl.ds((4 * px + 2 * py + pc) * m, m), :]

        def copy(k, block, to, src=None):
            return pltpu.make_async_remote_copy(src_ref=rows(*block) if src is None else src, dst_ref=rows(*block),
                                                send_sem=send_sems.at[k], recv_sem=recv_sems.at[k],
                                                device_id=to, device_id_type=MESH)

        mine = pltpu.make_async_copy(x_ref, rows(*me), local_sem)
        mine.start()
        first = [copy(0, me, sibling, src=x_ref)]
        first += [copy(1 + j, me, (*chip, c), src=x_ref) for j, chip in enumerate(chips)]
        for cp in first:
            cp.start()
        passed = [copy(4 + j, (*chip, c), sibling) for j, chip in enumerate(chips)]
        for j, chip in enumerate(chips):
            copy(1 + j, (*chip, c), me).wait_recv()
            passed[j].start()
        copy(0, sibling, me).wait_recv()
        for j, chip in enumerate(chips):
            copy(4 + j, (*chip, 1 - c), me).wait_recv()
        for cp in first + passed:
            cp.wait_send()
        mine.wait()
        acc = all_ref[0:m, :]
        for dev in range(1, 8):
            acc = acc + all_ref[dev * m:(dev + 1) * m, :]
        sum_ref[...] = acc

    vm = pl.BlockSpec(memory_space=pltpu.VMEM)
    return pl.pallas_call(
        body, name="small_allreduce",
        out_shape=[jax.ShapeDtypeStruct((m, n), F32), jax.ShapeDtypeStruct((8 * m, n), F32)],
        in_specs=[vm], out_specs=[vm, vm],
        scratch_shapes=[pltpu.SemaphoreType.DMA((7,)), pltpu.SemaphoreType.DMA((7,)), pltpu.SemaphoreType.DMA],
    )(v)[0]


def reduce_scatter_grads(names, gs):
    recv = grads_pair_exchange(gs)
    parts = [grads_pair_sum("grads_pair_sum_" + n, g, r) for n, g, r in zip(names, gs, recv)]
    recv2 = grads_chip_exchange(parts)
    reds = [grads_chip_sum("grads_chip_sum_" + n, g, r, r2) for n, g, r, r2 in zip(names, gs, recv, recv2)]
    return grads_pair_gather(reds)


def adamw(name, w, g, m, v, tr, tc=None):
    r, cols = w.shape
    tc = cols if tc is None else tc

    def body(w_ref, g_ref, m_ref, v_ref, d_ref, nm_ref, nv_ref):
        gv = g_ref[...]
        nm = ADAM_B1 * m_ref[...] + (1.0 - ADAM_B1) * gv
        nv = ADAM_B2 * v_ref[...] + (1.0 - ADAM_B2) * (gv * gv)
        m_hat = nm / (1.0 - ADAM_B1 ** ADAM_STEP)
        v_hat = nv / (1.0 - ADAM_B2 ** ADAM_STEP)
        d_ref[...] = -ADAM_LR * (m_hat / (jnp.sqrt(v_hat) + ADAM_EPS) + ADAM_WD * w_ref[...])
        nm_ref[...] = nm
        nv_ref[...] = nv

    spec = pl.BlockSpec((tr, tc), lambda i, j: (i, j))
    return pl.pallas_call(
        body, name=name, grid=(r // tr, cols // tc), in_specs=[spec] * 4, out_specs=[spec] * 3,
        out_shape=[jax.ShapeDtypeStruct((r, cols), F32)] * 3, compiler_params=_cparams(2),
    )(w, g, m, v)


ADAM_ROWS = {"w_ada": 512, "dw_w": 62, "pw2_w": 256, "sconv_w": 8, "w_proj_a": 512, "w_proj_b": 512, "w_proj_c": 512,
             "w_out": 256}
ADAM_W_IN_COLS = 256

WEIGHT_NAMES = ("w_ada", "b_ada", "norm_g", "w_in", "q_norm_g", "k_norm_g", "sinks", "dw_w", "dw_b", "ln_g", "ln_b",
                "pw2_w", "pw2_b", "sconv_w", "a_log", "dt_bias", "dn_norm_g", "w_proj_a", "w_proj_b", "w_proj_c", "w_out")


def kernel(x, c, w_ada, b_ada, norm_g, w_in, q_norm_g, k_norm_g, sinks, dw_w, dw_b, ln_g, ln_b, pw2_w, pw2_b, sconv_w, a_log, dt_bias, dn_norm_g, w_proj_a, w_proj_b, w_proj_c, w_out, loss_target, m_w_ada, m_b_ada, m_norm_g, m_w_in, m_q_norm_g, m_k_norm_g, m_sinks, m_dw_w, m_dw_b, m_ln_g, m_ln_b, m_pw2_w, m_pw2_b, m_sconv_w, m_a_log, m_dt_bias, m_dn_norm_g, m_w_proj_a, m_w_proj_b, m_w_proj_c, m_w_out, v_w_ada, v_b_ada, v_norm_g, v_w_in, v_q_norm_g, v_k_norm_g, v_sinks, v_dw_w, v_dw_b, v_ln_g, v_ln_b, v_pw2_w, v_pw2_b, v_sconv_w, v_a_log, v_dt_bias, v_dn_norm_g, v_w_proj_a, v_w_proj_b, v_w_proj_c, v_w_out):
    args = dict(locals())
    w = {n: args[n] for n in WEIGHT_NAMES}
    mom = {n: args["m_" + n] for n in WEIGHT_NAMES}
    var = {n: args["v_" + n] for n in WEIGHT_NAMES}

    chip = 2 * lax.axis_index("x") + lax.axis_index("y")
    slots = []
    for n in SHARDED:
        own = w[n] if n in GATHER_F32 else w[n].astype(BF16)
        slots.append(lax.dynamic_update_slice(lax.empty((N_CHIPS,) + own.shape, own.dtype), own[None], (chip, 0, 0, 0)))
    gathered = dict(zip(SHARDED, weights_allgather(slots)))
    layers = []
    for l in range(DEPTH):
        lw = {n: w[n][l] for n in SMALL}
        for n, axis in SHARDED.items():
            lw[n] = gathered[n][:, l] if n == "w_in" else _join_layer(gathered[n][:, l], axis)
        layers.append(prep_layer(lw))

    c8 = jnp.tile(c, (8, 1))
    act, saved = x[0], []
    for l in range(DEPTH):
        act, s = layer_fwd(str(l), act, c8, layers[l])
        saved.append(s)
    dact, loss_part = loss_head("loss_head", act, loss_target[0], TM)
    loss = lax.psum(loss_part[0, 0], ("x", "y", "c"))
    layer_grads = [None] * DEPTH
    for l in reversed(range(DEPTH)):
        dact, layer_grads[l] = layer_bwd(str(l), dact, c8, layers[l], saved[l])

    by_chip = [jnp.stack([layer_grads[l][n] if n == "w_in" else _split_layer(layer_grads[l][n], SHARDED[n])
                          for l in range(DEPTH)], axis=1) for n in REDUCE_BIG]
    final_grads = dict(zip(REDUCE_BIG, reduce_scatter_grads(REDUCE_BIG, by_chip)))
    small_names = SMALL + GATHER_F32
    small_shapes = {n: (DEPTH,) + layer_grads[0][n].shape for n in small_names}
    small_full = {n: jnp.stack([layer_grads[l][n] for l in range(DEPTH)]) for n in small_names}
    small_sum = unpack_small(small_allreduce(pack_small(small_full, small_names, SMALL_GRAD_ROWS)), small_names, small_shapes)
    for n in GATHER_F32:
        width = w[n].shape[2]
        final_grads[n] = lax.dynamic_slice_in_dim(small_sum[n], chip * width, width, axis=2)
    final_grads.update({n: small_sum[n] for n in SMALL})
    small_grads = pack_small(final_grads, SMALL, SMALL_ROWS)

    delta, new_m, new_v = {}, {}, {}
    for n in SHARDED:
        shp = w[n].shape
        if n == "w_in":
            two_d = lambda a: jnp.transpose(a, (2, 0, 1)).reshape(shp[2], shp[0] * shp[1])
            back = lambda a: jnp.transpose(a.reshape(shp[2], shp[0], shp[1]), (1, 2, 0))
            g2 = two_d(final_grads[n])
            final_grads[n] = back(g2)
            d, nm, nv = adamw("adamw_" + n, two_d(w[n]), g2, two_d(mom[n]), two_d(var[n]), shp[2], ADAM_W_IN_COLS)
        else:
            two_d = lambda a, shp=shp: a.reshape(shp[0] * shp[1], shp[2])
            back = lambda a, shp=shp: a.reshape(shp)
            d, nm, nv = adamw("adamw_" + n, two_d(w[n]), two_d(final_grads[n]), two_d(mom[n]), two_d(var[n]), ADAM_ROWS[n])
        delta[n], new_m[n], new_v[n] = back(d), back(nm), back(nv)
    d, nm, nv = adamw("adamw_small", pack_small(w, SMALL, SMALL_ROWS), small_grads, pack_small(mom, SMALL, SMALL_ROWS),
                      pack_small(var, SMALL, SMALL_ROWS), SMALL_ROWS)
    delta.update(unpack_small(d, SMALL, small_shapes))
    new_m.update(unpack_small(nm, SMALL, small_shapes))
    new_v.update(unpack_small(nv, SMALL, small_shapes))

    return (loss, dact[None], *[final_grads[n] for n in WEIGHT_NAMES], *[delta[n] for n in WEIGHT_NAMES],
            *[new_m[n] for n in WEIGHT_NAMES], *[new_v[n] for n in WEIGHT_NAMES])
```

```python
import functools

import numpy as np
import jax
import jax.numpy as jnp
from jax import lax
from jax.experimental import pallas as pl
from jax.experimental.pallas import tpu as pltpu

F32 = jnp.float32
BF16 = jnp.bfloat16
MESH = pl.DeviceIdType.MESH

D_MODEL = 1024
DEPTH = 2
ATT_HEADS = 8
ATT_HEAD_DIM = 64
WINDOW = 128
CONV_K = 31
DN_HEADS = 4
DN_CONV_K = 4
DN_CHUNK = 64
EPS = 1e-6
NEG_INF = -1e30
N_CHIPS = 4
D_IN = 7944

ADAM_LR = 0.001
ADAM_B1 = 0.9
ADAM_B2 = 0.999
ADAM_EPS = 1e-08
ADAM_WD = 0.01
ADAM_STEP = 10

VMEM_LIMIT = 56 * 1024 * 1024

P_QA, P_ZA, P_GLU, P_ZB, P_ZC, P_MG, P_QKV, P_KA, P_VA, P_AB, P_TOTAL = (
    0, 512, 1024, 2048, 2560, 3072, 6144, 7680, 7808, 7936, 8064)
HEAD_ORDER = (0, 4, 1, 5, 2, 6, 3, 7)


def _in_pieces():
    p = [(0 + 64 * h, 64) for h in HEAD_ORDER]
    p += [(768 + 64 * h, 64) for h in HEAD_ORDER]
    for g in range(4):
        p += [(1280 + 128 * g, 128), (1792 + 128 * g, 128)]
    p += [(2304, 512), (4360, 512), (4872, 3072), (2816, 1536), (512, 128), (640, 128), (4352, 8)]
    return p


def _perm_heads_rows(w):
    return jnp.concatenate([w[64 * h:64 * h + 64] for h in HEAD_ORDER], axis=0)


def _unperm_heads_rows(w):
    inv = [HEAD_ORDER.index(h) for h in range(8)]
    return jnp.concatenate([w[64 * s:64 * s + 64] for s in inv], axis=0)


def _split_bf16(a, terms):
    out, rest = [], a.astype(F32)
    for _ in range(terms - 1):
        out.append(rest.astype(BF16))
        rest = rest - out[-1].astype(F32)
    return out + [rest.astype(BF16)]


def _dot(a, b, dims, exact):
    d = lambda p, q: lax.dot_general(p, q, (dims, ((), ())), preferred_element_type=F32)
    if exact:
        (ah, al), (bh, bl) = _split_bf16(a, 2), _split_bf16(b, 2)
        return d(ah, bh) + (d(ah, bl) + d(al, bh))
    return d(a.astype(BF16), b.astype(BF16))


def _make_mm(exact):
    @jax.custom_vjp
    def nn(a, b):
        return _dot(a, b, ((1,), (0,)), exact)

    @jax.custom_vjp
    def nt(a, b):
        return _dot(a, b, ((1,), (1,)), exact)

    @jax.custom_vjp
    def tn(a, b):
        return _dot(a, b, ((0,), (0,)), exact)

    nn.defvjp(lambda a, b: (nn(a, b), (a, b)),
              lambda r, g: (nt(g, r[1]).astype(r[0].dtype), tn(r[0], g).astype(r[1].dtype)))
    nt.defvjp(lambda a, b: (nt(a, b), (a, b)),
              lambda r, g: (nn(g, r[1]).astype(r[0].dtype), tn(g, r[0]).astype(r[1].dtype)))
    tn.defvjp(lambda a, b: (tn(a, b), (a, b)),
              lambda r, g: (nt(r[1], g).astype(r[0].dtype), nn(r[0], g).astype(r[1].dtype)))
    return nn, nt, tn


mm, mm_nt, mm_tn = _make_mm(False)
xmm, xmm_nt, xmm_tn = _make_mm(True)


@jax.custom_vjp
def sel_mm(m, g):
    mb = m.astype(BF16)
    parts = [jnp.dot(mb, p, preferred_element_type=F32) for p in _split_bf16(g, 3)]
    return parts[0] + (parts[1] + parts[2])


def _sel_mm_bwd(m, dy):
    mb = m.astype(BF16)
    parts = [lax.dot_general(mb, p, (((0,), (0,)), ((), ())), preferred_element_type=F32) for p in _split_bf16(dy, 3)]
    return jnp.zeros_like(m), parts[0] + (parts[1] + parts[2])


sel_mm.defvjp(lambda m, g: (sel_mm(m, g), m), _sel_mm_bwd)


@jax.custom_vjp
def tri_inv(a):
    n = a.shape[0]
    eye = jnp.where(lax.broadcasted_iota(jnp.int32, (n, n), 0) == lax.broadcasted_iota(jnp.int32, (n, n), 1), 1.0, 0.0)
    t = eye - a
    pw = a
    for _ in range(5):
        pw = xmm(pw, pw)
        t = t + xmm(t, pw)
    return t


def _tri_inv_bwd(t, dt):
    return (-xmm_tn(t, xmm_nt(dt, t)),)


tri_inv.defvjp(lambda a: (tri_inv(a),) * 2, _tri_inv_bwd)


def _sigmoid(x):
    return 1.0 / (1.0 + jnp.exp(-x))


def _silu(x):
    return x * _sigmoid(x)


def _softplus(x):
    return jnp.maximum(x, 0.0) + jnp.log(1.0 + jnp.exp(-jnp.abs(x)))


def _cparams(n_grid):
    return pltpu.CompilerParams(dimension_semantics=("arbitrary",) * n_grid, vmem_limit_bytes=VMEM_LIMIT)


def _row_spec(tm, width, colblk):
    return pl.BlockSpec((tm, width), lambda i, cb=colblk: (i, cb))


def _const_spec(shape):
    nd = len(shape)
    return pl.BlockSpec(tuple(shape), lambda i, nd=nd: (0,) * nd)


def rowwise_fwd(name, f, rows, consts, outs, tm):
    n_r, n_c = len(rows), len(consts)
    t = rows[0][0].shape[0]

    def body(*refs):
        vals = [r[...] for r in refs[:n_r + n_c]]
        res = f(*vals)
        if not isinstance(res, (tuple, list)):
            res = (res,)
        for o_ref, v in zip(refs[n_r + n_c:], res):
            o_ref[...] = v.astype(o_ref.dtype)

    return pl.pallas_call(
        body, name=name, grid=(t // tm,),
        in_specs=[_row_spec(tm, w, cb) for _, w, cb in rows] + [_const_spec(c.shape) for c in consts],
        out_specs=[_row_spec(tm, w, 0) for w, _ in outs],
        out_shape=[jax.ShapeDtypeStruct((t, w), dt) for w, dt in outs],
        compiler_params=_cparams(1),
    )(*[a for a, _, _ in rows], *consts)


def rowwise_bwd(name, f, rows, consts, cts, row_grad_dtypes, tm):
    n_r, n_c, n_ct = len(rows), len(consts), len(cts)
    t = rows[0][0].shape[0]
    keep = [k for k, dt in enumerate(row_grad_dtypes) if dt is not None]

    def body(*refs):
        ins = [r[...].astype(F32) for r in refs[:n_r + n_c]]
        g_out = [r[...].astype(F32) for r in refs[n_r + n_c:n_r + n_c + n_ct]]
        out_refs = refs[n_r + n_c + n_ct:]

        def fw(*a):
            res = f(*a)
            return tuple(res) if isinstance(res, (tuple, list)) else (res,)

        _, vjp = jax.vjp(fw, *ins)
        grads = vjp(tuple(g_out))
        for o_ref, k in zip(out_refs[:len(keep)], keep):
            o_ref[...] = grads[k].astype(o_ref.dtype)
        first = pl.program_id(0) == 0
        for o_ref, g in zip(out_refs[len(keep):], grads[n_r:]):
            @pl.when(first)
            def _(o_ref=o_ref, g=g):
                o_ref[...] = g

            @pl.when(jnp.logical_not(first))
            def _(o_ref=o_ref, g=g):
                o_ref[...] += g

    return pl.pallas_call(
        body, name=name, grid=(t // tm,),
        in_specs=[_row_spec(tm, w, cb) for _, w, cb in rows] + [_const_spec(c.shape) for c in consts]
        + [_row_spec(tm, w, cb) for _, w, cb in cts],
        out_specs=[_row_spec(tm, rows[k][1], 0) for k in keep] + [_const_spec(c.shape) for c in consts],
        out_shape=[jax.ShapeDtypeStruct((t, rows[k][1]), row_grad_dtypes[k]) for k in keep]
        + [jax.ShapeDtypeStruct(c.shape, F32) for c in consts],
        compiler_params=_cparams(1),
    )(*[a for a, _, _ in rows], *consts, *[a for a, _, _ in cts])


def f_norm_mod(x, g, scale, shift):
    y = x * lax.rsqrt(jnp.mean(x * x, axis=-1, keepdims=True) + EPS) * g
    return y * (1.0 + scale) + shift


def f_conf_tail(u, zb, ln_g, ln_b, pw2_w, pw2_b):
    mu = jnp.mean(u, axis=-1, keepdims=True)
    xc = u - mu
    var = jnp.mean(xc * xc, axis=-1, keepdims=True)
    y = _silu(xc * lax.rsqrt(var + EPS) * ln_g + ln_b)
    return (mm(y, pw2_w) + pw2_b) * _silu(zb)


def f_merge(ya, yb, yc, mg, x, gate, wpa, wpb, wpc, wout):
    d = D_MODEL
    merged = (_sigmoid(mg[:, :d]) * mm(ya, wpa) + _sigmoid(mg[:, d:2 * d]) * mm(yb, wpb)
              + _sigmoid(mg[:, 2 * d:]) * mm(yc, wpc))
    return x + gate * mm(merged, wout)


def matmul_nn(name, a, b, out_dtype, tm, tn, tk):
    m, k = a.shape
    n = b.shape[1]
    nk = k // tk

    def body(a_ref, b_ref, o_ref, *acc):
        part = jnp.dot(a_ref[...].astype(BF16), b_ref[...].astype(BF16), preferred_element_type=F32)
        if nk == 1:
            o_ref[...] = part.astype(o_ref.dtype)
            return
        kk = pl.program_id(2)
        acc_ref = acc[0]

        @pl.when(kk == 0)
        def _():
            acc_ref[...] = part

        @pl.when(kk > 0)
        def _():
            acc_ref[...] += part

        @pl.when(kk == nk - 1)
        def _():
            o_ref[...] = acc_ref[...].astype(o_ref.dtype)

    return pl.pallas_call(
        body, name=name, grid=(m // tm, n // tn, nk),
        in_specs=[pl.BlockSpec((tm, tk), lambda i, j, kk: (i, kk)), pl.BlockSpec((tk, tn), lambda i, j, kk: (kk, j))],
        out_specs=pl.BlockSpec((tm, tn), lambda i, j, kk: (i, j)),
        out_shape=jax.ShapeDtypeStruct((m, n), out_dtype),
        scratch_shapes=[] if nk == 1 else [pltpu.VMEM((tm, tn), F32)],
        compiler_params=_cparams(3),
    )(a, b)


def ada_fwd(name, c8, w_ada, b_ada):
    def body(c_ref, w_ref, b_ref, o_ref):
        o_ref[...] = mm(_silu(c_ref[...]), w_ref[...]) + b_ref[...]

    return pl.pallas_call(
        body, name=name, out_shape=jax.ShapeDtypeStruct((8, 3 * D_MODEL), F32),
        compiler_params=pltpu.CompilerParams(vmem_limit_bytes=VMEM_LIMIT),
    )(c8, w_ada, b_ada)


def ada_bwd(name, c8, dmod8):
    tn = 768

    def body(c_ref, d_ref, o_ref):
        row0 = lax.broadcasted_iota(jnp.int32, (8, 1), 0) == 0
        sc = jnp.where(row0, _silu(c_ref[...]), 0.0)
        o_ref[...] = mm_tn(sc, d_ref[...])

    return pl.pallas_call(
        body, name=name, grid=(3 * D_MODEL // tn,),
        in_specs=[pl.BlockSpec((8, D_MODEL), lambda j: (0, 0)), pl.BlockSpec((8, tn), lambda j: (0, j))],
        out_specs=pl.BlockSpec((D_MODEL, tn), lambda j: (0, j)),
        out_shape=jax.ShapeDtypeStruct((D_MODEL, 3 * D_MODEL), F32),
        compiler_params=_cparams(1),
    )(c8, dmod8)


def _f_attn(first_block, q, za, kc, vc, kp, vp, qg, kg, sinks):
    w = WINDOW
    lane = lax.broadcasted_iota(jnp.int32, (1, 128), 1)
    halves = [lane < 64, lane >= 64]

    def rms_halves(x, g):
        x2 = x * x
        s0 = jnp.sum(jnp.where(halves[0], x2, 0.0), axis=-1, keepdims=True)
        s1 = jnp.sum(jnp.where(halves[1], x2, 0.0), axis=-1, keepdims=True)
        r = jnp.where(halves[0], lax.rsqrt(s0 / 64.0 + EPS), lax.rsqrt(s1 / 64.0 + EPS))
        return x * r * g

    kcat = rms_halves(jnp.concatenate([kp, kc], axis=0), kg)
    vcat = jnp.concatenate([vp, vc], axis=0)
    qi = lax.broadcasted_iota(jnp.int32, (w, 2 * w), 0)
    kj = lax.broadcasted_iota(jnp.int32, (w, 2 * w), 1)
    dist = qi + w - kj
    valid = (dist >= 0) & (dist < w) & (jnp.logical_not(first_block) | (kj >= w))
    distf = dist.astype(F32)
    outs = []
    for grp in range(4):
        qn = rms_halves(q[:, 128 * grp:128 * grp + 128], qg) * (ATT_HEAD_DIM ** -0.5)
        o_grp = jnp.zeros((w, 128), F32)
        for half in range(2):
            head = HEAD_ORDER[2 * grp + half]
            slope = 2.0 ** (-8.0 * (head + 1) / ATT_HEADS)
            sink = jnp.sum(jnp.where(lane == head, sinks, 0.0), axis=-1, keepdims=True)
            s = mm_nt(jnp.where(halves[half], qn, 0.0), kcat) - slope * distf
            s = jnp.where(valid, s, NEG_INF)
            m = lax.stop_gradient(jnp.maximum(jnp.max(s, axis=-1, keepdims=True), sink))
            p = jnp.exp(s - m)
            denom = jnp.sum(p, axis=-1, keepdims=True) + jnp.exp(sink - m)
            o_grp = o_grp + mm(p / denom, jnp.where(halves[half], vcat, 0.0))
        outs.append(o_grp)
    return jnp.concatenate(outs, axis=1) * _silu(za)


def attn_fwd(name, proj, qg, kg, sinks):
    t = proj.shape[0]
    nb = t // WINDOW

    def body(q_ref, za_ref, kc_ref, vc_ref, kp_ref, vp_ref, qg_ref, kg_ref, s_ref, o_ref):
        first = pl.program_id(0) == 0
        o_ref[...] = _f_attn(first, q_ref[...], za_ref[...], kc_ref[...], vc_ref[...], kp_ref[...], vp_ref[...],
                             qg_ref[...], kg_ref[...], s_ref[...])

    cur = lambda cb: (lambda i: (i, cb))
    prev = lambda cb: (lambda i: (jnp.maximum(i - 1, 0), cb))
    return pl.pallas_call(
        body, name=name, grid=(nb,),
        in_specs=[pl.BlockSpec((WINDOW, 512), cur(P_QA // 512)), pl.BlockSpec((WINDOW, 512), cur(P_ZA // 512)),
                  pl.BlockSpec((WINDOW, 128), cur(P_KA // 128)), pl.BlockSpec((WINDOW, 128), cur(P_VA // 128)),
                  pl.BlockSpec((WINDOW, 128), prev(P_KA // 128)), pl.BlockSpec((WINDOW, 128), prev(P_VA // 128)),
                  _const_spec((1, 128)), _const_spec((1, 128)), _const_spec((1, 128))],
        out_specs=pl.BlockSpec((WINDOW, 512), lambda i: (i, 0)),
        out_shape=jax.ShapeDtypeStruct((t, 512), F32),
        compiler_params=_cparams(1),
    )(proj, proj, proj, proj, proj, proj, qg, kg, sinks)


def attn_bwd(name, proj, qg, kg, sinks, dya):
    t = proj.shape[0]
    nb = t // WINDOW

    def body(q_ref, za_ref, kc_ref, vc_ref, kp_ref, vp_ref, qg_ref, kg_ref, s_ref, dy_ref,
             dqz_ref, dkv_ref, dqg_ref, dkg_ref, ds_ref, carry_ref):
        j = pl.program_id(0)
        first = j == nb - 1

        @pl.when(j == 0)
        def _():
            carry_ref[...] = jnp.zeros_like(carry_ref)
            dqg_ref[...] = jnp.zeros_like(dqg_ref)
            dkg_ref[...] = jnp.zeros_like(dkg_ref)
            ds_ref[...] = jnp.zeros_like(ds_ref)

        ins = [r[...] for r in (q_ref, za_ref, kc_ref, vc_ref, kp_ref, vp_ref, qg_ref, kg_ref, s_ref)]
        _, vjp = jax.vjp(functools.partial(_f_attn, first), *ins)
        dq, dza, dkc, dvc, dkp, dvp, dqg, dkg, dsk = vjp(dy_ref[...])
        dqz_ref[:, 0:512] = dq.astype(dqz_ref.dtype)
        dqz_ref[:, 512:1024] = dza.astype(dqz_ref.dtype)
        dkv_ref[:, 0:128] = (dkc + carry_ref[0]).astype(dkv_ref.dtype)
        dkv_ref[:, 128:256] = (dvc + carry_ref[1]).astype(dkv_ref.dtype)
        carry_ref[0] = dkp
        carry_ref[1] = dvp
        dqg_ref[...] += dqg
        dkg_ref[...] += dkg
        ds_ref[...] += dsk

    cur = lambda cb: (lambda j: (nb - 1 - j, cb))
    prev = lambda cb: (lambda j: (jnp.maximum(nb - 2 - j, 0), cb))
    return pl.pallas_call(
        body, name=name, grid=(nb,),
        in_specs=[pl.BlockSpec((WINDOW, 512), cur(P_QA // 512)), pl.BlockSpec((WINDOW, 512), cur(P_ZA // 512)),
                  pl.BlockSpec((WINDOW, 128), cur(P_KA // 128)), pl.BlockSpec((WINDOW, 128), cur(P_VA // 128)),
                  pl.BlockSpec((WINDOW, 128), prev(P_KA // 128)), pl.BlockSpec((WINDOW, 128), prev(P_VA // 128)),
                  _const_spec((1, 128)), _const_spec((1, 128)), _const_spec((1, 128)),
                  pl.BlockSpec((WINDOW, 512), cur(0))],
        out_specs=[pl.BlockSpec((WINDOW, 1024), cur(0)), pl.BlockSpec((WINDOW, 256), cur(0)),
                   _const_spec((1, 128)), _const_spec((1, 128)), _const_spec((1, 128))],
        out_shape=[jax.ShapeDtypeStruct((t, 1024), BF16), jax.ShapeDtypeStruct((t, 256), BF16),
                   jax.ShapeDtypeStruct((1, 128), F32), jax.ShapeDtypeStruct((1, 128), F32),
                   jax.ShapeDtypeStruct((1, 128), F32)],
        scratch_shapes=[pltpu.VMEM((2, WINDOW, 128), F32)],
        compiler_params=_cparams(1),
    )(proj, proj, proj, proj, proj, proj, qg, kg, sinks, dya)


CONV_ROWS = 256


def _conv_taps(src_ref, w_ref, n_taps, base, t):
    for r0 in range(0, t, CONV_ROWS):
        acc = w_ref[0:1, :] * src_ref[pl.ds(r0 + base, CONV_ROWS), :]
        for k in range(1, n_taps):
            acc = acc + w_ref[k:k + 1, :] * src_ref[pl.ds(r0 + base + k, CONV_ROWS), :]
        yield r0, acc


def _conv_wgrad(dy_ref, src_ref, n_taps, base, t, dy_base=0):
    out = []
    for k in range(n_taps):
        acc = jnp.zeros((8, 128), F32)
        for r0 in range(0, t, CONV_ROWS):
            prod = dy_ref[pl.ds(r0 + dy_base, CONV_ROWS), :] * src_ref[pl.ds(r0 + base + k, CONV_ROWS), :]
            acc = acc + jnp.sum(prod.reshape(CONV_ROWS // 8, 8, 128), axis=0)
        out.append(jnp.sum(acc, axis=0, keepdims=True))
    return out


def glu_conv_fwd(name, proj, w32, bias):
    t = proj.shape[0]
    pad = 32

    def body(x_ref, w_ref, b_ref, o_ref, u_ref):
        u_ref[0:pad, :] = jnp.zeros((pad, 128), F32)
        u_ref[pad:pad + t, :] = x_ref[:, 0:128] * _sigmoid(x_ref[:, 128:256])
        for r0, acc in _conv_taps(u_ref, w_ref, CONV_K, pad - (CONV_K - 1), t):
            o_ref[pl.ds(r0, CONV_ROWS), :] = acc + b_ref[...]

    return pl.pallas_call(
        body, name=name, grid=(4,),
        in_specs=[pl.BlockSpec((t, 256), lambda cb: (0, P_GLU // 256 + cb)), pl.BlockSpec((32, 128), lambda cb: (0, cb)),
                  pl.BlockSpec((1, 128), lambda cb: (0, cb))],
        out_specs=pl.BlockSpec((t, 128), lambda cb: (0, cb)),
        out_shape=jax.ShapeDtypeStruct((t, 512), F32),
        scratch_shapes=[pltpu.VMEM((t + pad, 128), F32)],
        compiler_params=_cparams(1),
    )(proj, w32, bias)


def glu_conv_bwd(name, proj, w32, dub):
    t = proj.shape[0]
    pad = 32
    k1 = CONV_K - 1

    def body(x_ref, w_ref, dy_ref, dx_ref, dw_ref, db_ref, u_ref, dyp_ref, wrev_ref):
        val = x_ref[:, 0:128]
        sg = _sigmoid(x_ref[:, 128:256])
        u_ref[0:pad, :] = jnp.zeros((pad, 128), F32)
        u_ref[pad:pad + t, :] = val * sg
        dyp_ref[0:t, :] = dy_ref[...]
        dyp_ref[t:t + pad, :] = jnp.zeros((pad, 128), F32)
        for k in range(CONV_K):
            wrev_ref[k:k + 1, :] = w_ref[k1 - k:k1 - k + 1, :]
        wrev_ref[CONV_K:32, :] = jnp.zeros((32 - CONV_K, 128), F32)
        for r0, du in _conv_taps(dyp_ref, wrev_ref, CONV_K, 0, t):
            v = x_ref[pl.ds(r0, CONV_ROWS), 0:128]
            s = _sigmoid(x_ref[pl.ds(r0, CONV_ROWS), 128:256])
            dx_ref[pl.ds(r0, CONV_ROWS), 0:128] = (du * s).astype(dx_ref.dtype)
            dx_ref[pl.ds(r0, CONV_ROWS), 128:256] = (du * v * s * (1.0 - s)).astype(dx_ref.dtype)
        dws = _conv_wgrad(dyp_ref, u_ref, CONV_K, pad - k1, t)
        for k in range(CONV_K):
            dw_ref[k:k + 1, :] = dws[k]
        dw_ref[CONV_K:32, :] = jnp.zeros((32 - CONV_K, 128), F32)
        db_ref[...] = jnp.sum(dy_ref[...], axis=0, keepdims=True)

    return pl.pallas_call(
        body, name=name, grid=(4,),
        in_specs=[pl.BlockSpec((t, 256), lambda cb: (0, P_GLU // 256 + cb)), pl.BlockSpec((32, 128), lambda cb: (0, cb)),
                  pl.BlockSpec((t, 128), lambda cb: (0, cb))],
        out_specs=[pl.BlockSpec((t, 256), lambda cb: (0, cb)), pl.BlockSpec((32, 128), lambda cb: (0, cb)),
                   pl.BlockSpec((1, 128), lambda cb: (0, cb))],
        out_shape=[jax.ShapeDtypeStruct((t, 1024), BF16), jax.ShapeDtypeStruct((32, 512), F32),
                   jax.ShapeDtypeStruct((1, 512), F32)],
        scratch_shapes=[pltpu.VMEM((t + pad, 128), F32), pltpu.VMEM((t + pad, 128), F32), pltpu.VMEM((32, 128), F32)],
        compiler_params=_cparams(1),
    )(proj, w32, dub)


def sconv_fwd(name, proj, w8):
    t = proj.shape[0]
    pad = 8
    k1 = DN_CONV_K - 1

    def body(x_ref, w_ref, o_ref, xp_ref):
        xp_ref[0:pad, :] = jnp.zeros((pad, 128), F32)
        xp_ref[pad:pad + t, :] = x_ref[...]
        for r0, acc in _conv_taps(xp_ref, w_ref, DN_CONV_K, pad - k1, t):
            o_ref[pl.ds(r0, CONV_ROWS), :] = _silu(acc)

    return pl.pallas_call(
        body, name=name, grid=(12,),
        in_specs=[pl.BlockSpec((t, 128), lambda cb: (0, P_QKV // 128 + cb)), pl.BlockSpec((8, 128), lambda cb: (0, cb))],
        out_specs=pl.BlockSpec((t, 128), lambda cb: (0, cb)),
        out_shape=jax.ShapeDtypeStruct((t, 1536), F32),
        scratch_shapes=[pltpu.VMEM((t + pad, 128), F32)],
        compiler_params=_cparams(1),
    )(proj, w8)


def sconv_bwd(name, proj, w8, dqkv):
    t = proj.shape[0]
    pad = 8
    k1 = DN_CONV_K - 1

    def body(x_ref, w_ref, dy_ref, dx_ref, dw_ref, xp_ref, dpp_ref, wrev_ref):
        xp_ref[0:pad, :] = jnp.zeros((pad, 128), F32)
        xp_ref[pad:pad + t, :] = x_ref[...]
        for r0, pre in _conv_taps(xp_ref, w_ref, DN_CONV_K, pad - k1, t):
            s = _sigmoid(pre)
            dpp_ref[pl.ds(r0, CONV_ROWS), :] = dy_ref[pl.ds(r0, CONV_ROWS), :] * (s * (1.0 + pre * (1.0 - s)))
        dpp_ref[t:t + pad, :] = jnp.zeros((pad, 128), F32)
        for k in range(DN_CONV_K):
            wrev_ref[k:k + 1, :] = w_ref[k1 - k:k1 - k + 1, :]
        wrev_ref[DN_CONV_K:8, :] = jnp.zeros((8 - DN_CONV_K, 128), F32)
        for r0, dx in _conv_taps(dpp_ref, wrev_ref, DN_CONV_K, 0, t):
            dx_ref[pl.ds(r0, CONV_ROWS), :] = dx.astype(dx_ref.dtype)
        dws = _conv_wgrad(dpp_ref, xp_ref, DN_CONV_K, pad - k1, t)
        for k in range(DN_CONV_K):
            dw_ref[k:k + 1, :] = dws[k]
        dw_ref[DN_CONV_K:8, :] = jnp.zeros((8 - DN_CONV_K, 128), F32)

    return pl.pallas_call(
        body, name=name, grid=(12,),
        in_specs=[pl.BlockSpec((t, 128), lambda cb: (0, P_QKV // 128 + cb)), pl.BlockSpec((8, 128), lambda cb: (0, cb)),
                  pl.BlockSpec((t, 128), lambda cb: (0, cb))],
        out_specs=[pl.BlockSpec((t, 128), lambda cb: (0, cb)), pl.BlockSpec((8, 128), lambda cb: (0, cb))],
        out_shape=[jax.ShapeDtypeStruct((t, 1536), BF16), jax.ShapeDtypeStruct((8, 1536), F32)],
        scratch_shapes=[pltpu.VMEM((t + pad, 128), F32), pltpu.VMEM((t + pad, 128), F32), pltpu.VMEM((8, 128), F32)],
        compiler_params=_cparams(1),
    )(proj, w8, dqkv)


def _f_delta(qkv, ab, zc, s0, s1, s2, s3, a_log, dt_bias, dn_g):
    cs = DN_CHUNK
    n = 2 * cs
    states = (s0, s1, s2, s3)
    lane = lax.broadcasted_iota(jnp.int32, (1, 128), 1)
    ri = lax.broadcasted_iota(jnp.int32, (n, n), 0)
    ci = lax.broadcasted_iota(jnp.int32, (n, n), 1)
    same = (ri // cs) == (ci // cs)
    lower = same & (ri >= ci)
    strict = same & (ri > ci)
    sums = jnp.concatenate([jnp.where(lower, 1.0, 0.0), jnp.where(same, 1.0, 0.0), jnp.where(ci < cs, 1.0, 0.0),
                            jnp.where(ci >= cs, 1.0, 0.0)], axis=0)
    top = lax.broadcasted_iota(jnp.int32, (n, 1), 0) < cs

    def pick(row, idx):
        return jnp.sum(jnp.where(lane == idx, row, 0.0), axis=-1, keepdims=True)

    def l2n(x):
        return x * lax.rsqrt(jnp.sum(x * x, axis=-1, keepdims=True) + EPS)

    ys, new_states = [], []
    for pair in range(2):
        hs = (2 * pair, 2 * pair + 1)
        stack = lambda f: jnp.concatenate([f(hs[0]), f(hs[1])], axis=0)
        qd = l2n(stack(lambda h: qkv[:, 128 * h:128 * h + 128])) * (128 ** -0.5)
        kd = l2n(stack(lambda h: qkv[:, 512 + 128 * h:512 + 128 * h + 128]))
        vd = stack(lambda h: qkv[:, 1024 + 128 * h:1024 + 128 * h + 128])
        beta = _sigmoid(stack(lambda h: pick(ab, 4 + h)))
        g = stack(lambda h: -jnp.exp(pick(a_log, h)) * _softplus(pick(ab, h) + pick(dt_bias, h)))
        g_b = g * jnp.ones((1, n), F32)
        g_sums = sel_mm(sums, g_b)
        gc_col = g_sums[0:n]
        gc_row = gc_col.T
        gl_b = g_sums[n:2 * n]
        g_end = (g_sums[2 * n:3 * n], g_sums[3 * n:])
        decay = jnp.where(lower, jnp.exp(jnp.where(lower, gc_col - gc_row, 0.0)), 0.0)
        kb = kd * beta
        vb = vd * beta
        tmat = tri_inv(jnp.where(strict, mm_nt(kb, kd) * decay, 0.0))
        egc = jnp.exp(gc_col)
        u = mm(tmat, vb)
        wm = mm(tmat, kb * egc)
        intra = jnp.where(lower, mm_nt(qd, kd) * decay, 0.0)
        qe = qd * egc
        ke = kd * jnp.exp(gl_b - gc_col)
        st = (states[hs[0]], states[hs[1]])
        v_new = u - jnp.concatenate([mm(wm[:cs], st[0]), mm(wm[cs:], st[1])], axis=0)
        o = jnp.concatenate([mm(qe[:cs], st[0]), mm(qe[cs:], st[1])], axis=0) + mm(intra, v_new)
        new_states.append(st[0] * jnp.exp(g_end[0]) + mm_tn(jnp.where(top, ke, 0.0), v_new))
        new_states.append(st[1] * jnp.exp(g_end[1]) + mm_tn(jnp.where(top, 0.0, ke), v_new))
        od = o * lax.rsqrt(jnp.mean(o * o, axis=-1, keepdims=True) + EPS) * dn_g
        ys += [od[:cs] * _silu(zc[:, 128 * hs[0]:128 * hs[0] + 128]), od[cs:] * _silu(zc[:, 128 * hs[1]:128 * hs[1] + 128])]
    y = jnp.concatenate([ys[0], ys[1], ys[2], ys[3]], axis=1)
    return (y, *new_states)


DELTA_ROWS = 2 * DN_CHUNK


def _f_delta_step(qkv, ab, zc, s0, s1, s2, s3, a_log, dt_bias, dn_g):
    st, ys = (s0, s1, s2, s3), []
    for k in range(qkv.shape[0] // DN_CHUNK):
        r = slice(k * DN_CHUNK, (k + 1) * DN_CHUNK)
        y, *st = _f_delta(qkv[r], ab[r], zc[r], *st, a_log, dt_bias, dn_g)
        ys.append(y)
    return (jnp.concatenate(ys, axis=0), *st)


def delta_fwd(name, qkv, proj, a_log, dt_bias, dn_g):
    t = qkv.shape[0]
    nc = t // DELTA_ROWS

    def body(qkv_ref, ab_ref, zc_ref, al_ref, dt_ref, g_ref, y_ref, ssave_ref, s_ref):
        @pl.when(pl.program_id(0) == 0)
        def _():
            s_ref[...] = jnp.zeros_like(s_ref)

        ssave_ref[0] = s_ref[...]
        st = [s_ref[128 * h:128 * h + 128, :] for h in range(4)]
        y, *ns = _f_delta_step(qkv_ref[...], ab_ref[...], zc_ref[...], *st, al_ref[...], dt_ref[...], g_ref[...])
        y_ref[...] = y
        for h in range(4):
            s_ref[128 * h:128 * h + 128, :] = ns[h]

    return pl.pallas_call(
        body, name=name, grid=(nc,),
        in_specs=[pl.BlockSpec((DELTA_ROWS, 1536), lambda i: (i, 0)), pl.BlockSpec((DELTA_ROWS, 128), lambda i: (i, P_AB // 128)),
                  pl.BlockSpec((DELTA_ROWS, 512), lambda i: (i, P_ZC // 512)),
                  _const_spec((1, 128)), _const_spec((1, 128)), _const_spec((1, 128))],
        out_specs=[pl.BlockSpec((DELTA_ROWS, 512), lambda i: (i, 0)), pl.BlockSpec((1, 512, 128), lambda i: (i, 0, 0))],
        out_shape=[jax.ShapeDtypeStruct((t, 512), F32), jax.ShapeDtypeStruct((nc, 512, 128), F32)],
        scratch_shapes=[pltpu.VMEM((512, 128), F32)],
        compiler_params=_cparams(1),
    )(qkv, proj, proj, a_log, dt_bias, dn_g)


def delta_bwd(name, qkv, proj, ssave, a_log, dt_bias, dn_g, dyc):
    t = qkv.shape[0]
    nc = t // DELTA_ROWS

    def body(qkv_ref, ab_ref, zc_ref, ss_ref, al_ref, dt_ref, g_ref, dy_ref,
             dqkv_ref, dab_ref, dzc_ref, dal_ref, ddt_ref, dg_ref, ds_ref):
        @pl.when(pl.program_id(0) == 0)
        def _():
            ds_ref[...] = jnp.zeros_like(ds_ref)
            dal_ref[...] = jnp.zeros_like(dal_ref)
            ddt_ref[...] = jnp.zeros_like(ddt_ref)
            dg_ref[...] = jnp.zeros_like(dg_ref)

        st = [ss_ref[0, 128 * h:128 * h + 128, :] for h in range(4)]
        _, vjp = jax.vjp(_f_delta_step, qkv_ref[...], ab_ref[...], zc_ref[...], *st, al_ref[...], dt_ref[...], g_ref[...])
        dst = tuple(ds_ref[128 * h:128 * h + 128, :] for h in range(4))
        dqkv, dab, dzc, d0, d1, d2, d3, dal, ddt, dg = vjp((dy_ref[...], *dst))
        dqkv_ref[...] = dqkv
        dab_ref[...] = dab.astype(dab_ref.dtype)
        dzc_ref[...] = dzc.astype(dzc_ref.dtype)
        for h, d in enumerate((d0, d1, d2, d3)):
            ds_ref[128 * h:128 * h + 128, :] = d
        dal_ref[...] += dal
        ddt_ref[...] += ddt
        dg_ref[...] += dg

    rev = lambda cb: (lambda j: (nc - 1 - j, cb))
    return pl.pallas_call(
        body, name=name, grid=(nc,),
        in_specs=[pl.BlockSpec((DELTA_ROWS, 1536), rev(0)), pl.BlockSpec((DELTA_ROWS, 128), rev(P_AB // 128)),
                  pl.BlockSpec((DELTA_ROWS, 512), rev(P_ZC // 512)), pl.BlockSpec((1, 512, 128), lambda j: (nc - 1 - j, 0, 0)),
                  _const_spec((1, 128)), _const_spec((1, 128)), _const_spec((1, 128)),
                  pl.BlockSpec((DELTA_ROWS, 512), rev(0))],
        out_specs=[pl.BlockSpec((DELTA_ROWS, 1536), rev(0)), pl.BlockSpec((DELTA_ROWS, 128), rev(0)),
                   pl.BlockSpec((DELTA_ROWS, 512), rev(0)),
                   _const_spec((1, 128)), _const_spec((1, 128)), _const_spec((1, 128))],
        out_shape=[jax.ShapeDtypeStruct((t, 1536), F32), jax.ShapeDtypeStruct((t, 128), BF16),
                   jax.ShapeDtypeStruct((t, 512), BF16),
                   jax.ShapeDtypeStruct((1, 128), F32), jax.ShapeDtypeStruct((1, 128), F32), jax.ShapeDtypeStruct((1, 128), F32)],
        scratch_shapes=[pltpu.VMEM((512, 128), F32)],
        compiler_params=_cparams(1),
    )(qkv, proj, proj, ssave, a_log, dt_bias, dn_g, dyc)


def loss_head(name, y, target, tm):
    t, d = y.shape

    def body(y_ref, t_ref, dy_ref, l_ref):
        err = y_ref[...] - t_ref[...]
        dy_ref[...] = err * (1.0 / d)
        part = 0.5 * jnp.sum(jnp.sum(err * err, axis=-1, keepdims=True) * (1.0 / d), axis=0, keepdims=True)

        @pl.when(pl.program_id(0) == 0)
        def _():
            l_ref[...] = part

        @pl.when(pl.program_id(0) > 0)
        def _():
            l_ref[...] += part

    return pl.pallas_call(
        body, name=name, grid=(t // tm,),
        in_specs=[_row_spec(tm, d, 0), _row_spec(tm, d, 0)],
        out_specs=[_row_spec(tm, d, 0), _const_spec((1, 1))],
        out_shape=[jax.ShapeDtypeStruct((t, d), F32), jax.ShapeDtypeStruct((1, 1), F32)],
        compiler_params=_cparams(1),
    )(y, target)


TM = 512
TM_MERGE = 256
TM_IN = 1024
TN_IN = 1152


def _lane_pad(v, n=128):
    return jnp.pad(v.astype(F32), (0, n - v.shape[0]))[None, :]


def f_norm_mod_res(x, g, scale, shift):
    return f_norm_mod(x, g, scale, shift), x


def prep_layer(w):
    p = dict(w)
    p["wp"] = _pad_w_in_from_shards(w["w_in"])
    p["wpt"] = p["wp"].T
    p["wpa"] = _perm_heads_rows(w["w_proj_a"])
    p["dw32"] = jnp.pad(w["dw_w"], ((0, 32 - CONV_K), (0, 0)))
    p["sconv8"] = jnp.pad(w["sconv_w"], ((0, 8 - DN_CONV_K), (0, 0)))
    p["qg"] = jnp.tile(w["q_norm_g"], 2)[None, :]
    p["kg"] = jnp.tile(w["k_norm_g"], 2)[None, :]
    p["sinks128"] = _lane_pad(w["sinks"])
    p["al"] = _lane_pad(w["a_log"])
    p["dtb"] = _lane_pad(w["dt_bias"])
    p["dng"] = w["dn_norm_g"][None, :]
    return p


def layer_fwd(tag, x, c8, p):
    mod = ada_fwd(f"ada_fwd{tag}", c8, p["w_ada"], p["b_ada"][None, :])[0:1]
    d = D_MODEL
    shift, scale, gate = mod[:, :d], mod[:, d:2 * d], mod[:, 2 * d:]
    g = p["norm_g"][None, :]
    (h,) = rowwise_fwd(f"norm_fwd{tag}", f_norm_mod, [(x, d, 0)], [g, scale, shift], [(d, BF16)], TM)
    proj = matmul_nn(f"inproj_fwd{tag}", h, p["wp"], F32, TM_IN, TN_IN, d)
    ya = attn_fwd(f"attn_fwd{tag}", proj, p["qg"], p["kg"], p["sinks128"])
    ub = glu_conv_fwd(f"glu_conv_fwd{tag}", proj, p["dw32"], p["dw_b"][None, :])
    conf_consts = [p["ln_g"][None, :], p["ln_b"][None, :], p["pw2_w"], p["pw2_b"][None, :]]
    (yb,) = rowwise_fwd(f"conf_fwd{tag}", f_conf_tail, [(ub, 512, 0), (proj, 512, P_ZB // 512)], conf_consts, [(512, F32)], TM)
    qkv = sconv_fwd(f"sconv_fwd{tag}", proj, p["sconv8"])
    yc, ssave = delta_fwd(f"delta_fwd{tag}", qkv, proj, p["al"], p["dtb"], p["dng"])
    merge_consts = [gate, p["wpa"], p["w_proj_b"], p["w_proj_c"], p["w_out"]]
    merge_rows = [(ya, 512, 0), (yb, 512, 0), (yc, 512, 0), (proj, 3 * d, P_MG // (3 * d)), (x, d, 0)]
    (xn,) = rowwise_fwd(f"merge_fwd{tag}", f_merge, merge_rows, merge_consts, [(d, F32)], TM_MERGE)
    saved = dict(x=x, h=h, proj=proj, ub=ub, qkv=qkv, ssave=ssave, norm_consts=[g, scale, shift],
                 conf_consts=conf_consts, merge_consts=merge_consts, merge_rows=merge_rows)
    return xn, saved


def layer_bwd(tag, dxn, c8, p, s):
    d = D_MODEL
    proj = s["proj"]
    dya, dyb, dyc, dmg, dgate, dwpa, dwpb, dwpc, dwout = rowwise_bwd(
        f"merge_bwd{tag}", f_merge, s["merge_rows"], s["merge_consts"], [(dxn, d, 0)], [F32, F32, F32, BF16, None], TM_MERGE)
    dqz, dkv, dqg, dkg, dsinks = attn_bwd(f"attn_bwd{tag}", proj, p["qg"], p["kg"], p["sinks128"], dya)
    dub, dzb, dln_g, dln_b, dpw2_w, dpw2_b = rowwise_bwd(
        f"conf_bwd{tag}", f_conf_tail, [(s["ub"], 512, 0), (proj, 512, P_ZB // 512)], s["conf_consts"], [(dyb, 512, 0)],
        [F32, BF16], TM)
    dglu, ddw32, ddw_b = glu_conv_bwd(f"glu_conv_bwd{tag}", proj, p["dw32"], dub)
    dqkv, dab, dzc, dal, ddtb, ddng = delta_bwd(f"delta_bwd{tag}", s["qkv"], proj, s["ssave"], p["al"], p["dtb"], p["dng"], dyc)
    dqkv_pre, dsconv8 = sconv_bwd(f"sconv_bwd{tag}", proj, p["sconv8"], dqkv)
    dproj = jnp.concatenate([dqz, dglu, dzb, dzc, dmg, dqkv_pre, dkv, dab], axis=1)
    dh = matmul_nn(f"inproj_bwd_dh{tag}", dproj, p["wpt"], F32, TM_IN, d, TN_IN)
    dwp = matmul_nn(f"inproj_bwd_dw{tag}", s["h"].T, dproj, F32, d, TN_IN, TM)
    dx, dnorm_g, dscale, dshift = rowwise_bwd(
        f"norm_bwd{tag}", f_norm_mod_res, [(s["x"], d, 0)], s["norm_consts"], [(dh, d, 0), (dxn, d, 0)], [F32], TM)
    dmod = jnp.concatenate([dshift, dscale, dgate], axis=1)
    dw_ada = ada_bwd(f"ada_bwd{tag}", c8, jnp.pad(dmod, ((0, 7), (0, 0))))
    grads = dict(
        w_ada=dw_ada, b_ada=dmod[0], norm_g=dnorm_g[0], w_in=_unpad_w_in_to_shards(dwp),
        q_norm_g=dqg[0, :64] + dqg[0, 64:], k_norm_g=dkg[0, :64] + dkg[0, 64:], sinks=dsinks[0, :ATT_HEADS],
        dw_w=ddw32[:CONV_K], dw_b=ddw_b[0], ln_g=dln_g[0], ln_b=dln_b[0], pw2_w=dpw2_w, pw2_b=dpw2_b[0],
        sconv_w=dsconv8[:DN_CONV_K], a_log=dal[0, :DN_HEADS], dt_bias=ddtb[0, :DN_HEADS], dn_norm_g=ddng[0],
        w_proj_a=_unperm_heads_rows(dwpa), w_proj_b=dwpb, w_proj_c=dwpc, w_out=dwout)
    return dx, grads


SHARDED = {"w_ada": 2, "w_in": 2, "dw_w": 2, "pw2_w": 1, "sconv_w": 2, "w_proj_a": 2, "w_proj_b": 2, "w_proj_c": 2,
           "w_out": 1}
GATHER_F32 = ("dw_w", "sconv_w")
REDUCE_BIG = tuple(n for n in SHARDED if n not in GATHER_F32)
SMALL = ("b_ada", "norm_g", "q_norm_g", "k_norm_g", "sinks", "dw_b", "ln_g", "ln_b", "pw2_b", "a_log", "dt_bias",
         "dn_norm_g")
SMALL_ROWS = 104
SMALL_GRAD_ROWS = 448
W_IN_SHARD = D_IN // N_CHIPS
SUM_TILE = 256


def _shard_cols(shards, start, n):
    parts = []
    while n > 0:
        k, o = divmod(start, W_IN_SHARD)
        m = min(n, W_IN_SHARD - o)
        parts.append(shards[k][:, o:o + m])
        start, n = start + m, n - m
    return parts


def _pad_w_in_from_shards(shards):
    parts = []
    for s, n in _in_pieces():
        parts += _shard_cols(shards, s, n)
    parts.append(jnp.zeros((shards.shape[1], P_TOTAL - D_IN), shards.dtype))
    return jnp.concatenate(parts, axis=1)


def _unpad_w_in_to_shards(wp):
    pieces = _in_pieces()
    starts = np.cumsum([0] + [n for _, n in pieces])[:-1]
    order = sorted(range(len(pieces)), key=lambda i: pieces[i][0])
    shards = []
    for k in range(N_CHIPS):
        lo, hi = k * W_IN_SHARD, (k + 1) * W_IN_SHARD
        parts = []
        for i in order:
            s, n = pieces[i]
            a, b = max(s, lo), min(s + n, hi)
            if a < b:
                parts.append(wp[:, int(starts[i]) + a - s:int(starts[i]) + b - s])
        shards.append(jnp.concatenate(parts, axis=1))
    return jnp.stack(shards)


def _join_layer(v, axis):
    if axis == 2:
        return jnp.transpose(v, (1, 0, 2)).reshape(v.shape[1], N_CHIPS * v.shape[2])
    return v.reshape(N_CHIPS * v.shape[1], v.shape[2])


def _split_layer(v, axis):
    a, b = v.shape
    if axis == 2:
        return jnp.transpose(v.reshape(a, N_CHIPS, b // N_CHIPS), (1, 0, 2))
    return v.reshape(N_CHIPS, a // N_CHIPS, b)


def pack_small(vals, names, rows):
    flat = jnp.concatenate([vals[n].astype(F32).reshape(-1) for n in names])
    return jnp.pad(flat, (0, rows * 128 - flat.shape[0])).reshape(rows, 128)


def unpack_small(packed, names, shapes):
    flat = packed.reshape(-1)
    out, off = {}, 0
    for n in names:
        k = int(np.prod(shapes[n]))
        out[n] = flat[off:off + k].reshape(shapes[n])
        off += k
    return out


ANY = pl.BlockSpec(memory_space=pl.ANY)


def _place():
    x, y, c = lax.axis_index("x"), lax.axis_index("y"), lax.axis_index("c")
    chips = [(1 - x, y), (x, 1 - y), (1 - x, 1 - y)]
    return x, y, c, chips


def _remote(src, dst, send_sem, recv_sem, to):
    return pltpu.make_async_remote_copy(src_ref=src, dst_ref=dst, send_sem=send_sem, recv_sem=recv_sem, device_id=to,
                                        device_id_type=MESH)


def weights_allgather(slots):
    n = len(slots)

    def body(*refs):
        out = refs[n:2 * n]
        send_sems, recv_sems = refs[2 * n:]
        x, y, c, chips = _place()
        me, sibling, my_slot = (x, y, c), (x, y, 1 - c), 2 * x + y
        sends = []
        for j, chip in enumerate(chips):
            for t in range(n):
                mine = out[t].at[my_slot, c]
                sends.append(_remote(mine, mine, send_sems.at[t, j], recv_sems.at[t, j], (*chip, c)))
                sends[-1].start()
        for j, chip in enumerate(chips):
            for t in range(n):
                land = out[t].at[2 * chip[0] + chip[1], c]
                _remote(land, land, send_sems.at[t, j], recv_sems.at[t, j], me).wait_recv()
                sends.append(_remote(land, land, send_sems.at[t, 3 + j], recv_sems.at[t, 3 + j], sibling))
                sends[-1].start()
        for j, chip in enumerate(chips):
            for t in range(n):
                land = out[t].at[2 * chip[0] + chip[1], 1 - c]
                _remote(land, land, send_sems.at[t, 3 + j], recv_sems.at[t, 3 + j], me).wait_recv()
        for cp in sends:
            cp.wait_send()

    return pl.pallas_call(
        body, name="weights_allgather", out_shape=[jax.ShapeDtypeStruct(s.shape, s.dtype) for s in slots],
        in_specs=[ANY] * n, out_specs=[ANY] * n, input_output_aliases={t: t for t in range(n)},
        scratch_shapes=[pltpu.SemaphoreType.DMA((n, 6)), pltpu.SemaphoreType.DMA((n, 6))],
    )(*slots)


def grads_pair_exchange(gs):
    n = len(gs)

    def body(*refs):
        g, recv = refs[:n], refs[n:2 * n]
        send_sems, recv_sems = refs[2 * n:]
        x, y, c, _ = _place()
        cps = [_remote(g[t].at[:, 1 - c], recv[t], send_sems.at[t], recv_sems.at[t], (x, y, 1 - c)) for t in range(n)]
        for cp in cps:
            cp.start()
        for cp in cps:
            cp.wait()

    return pl.pallas_call(
        body, name="grads_pair_exchange",
        out_shape=[jax.ShapeDtypeStruct((N_CHIPS,) + g.shape[2:], g.dtype) for g in gs],
        in_specs=[ANY] * n, out_specs=[ANY] * n,
        scratch_shapes=[pltpu.SemaphoreType.DMA((n,)), pltpu.SemaphoreType.DMA((n,))],
    )(*gs)


def grads_pair_sum(name, g, recv):
    _, a, b = recv.shape
    ta = min(a, SUM_TILE)

    def body(a_ref, b_ref, o_ref):
        o_ref[...] = (a_ref[...] + b_ref[...]).astype(o_ref.dtype)

    return pl.pallas_call(
        body, name=name, grid=(N_CHIPS, a // ta),
        in_specs=[pl.BlockSpec((None, None, ta, b), lambda s, i: (s, lax.axis_index("c"), i, 0)),
                  pl.BlockSpec((None, ta, b), lambda s, i: (s, i, 0))],
        out_specs=pl.BlockSpec((None, ta, b), lambda s, i: (s, i, 0)),
        out_shape=jax.ShapeDtypeStruct(recv.shape, BF16),
        compiler_params=_cparams(2),
    )(g, recv)


def grads_chip_exchange(ps):
    n = len(ps)

    def body(*refs):
        p, recv = refs[:n], refs[n:2 * n]
        send_sems, recv_sems = refs[2 * n:]
        x, y, c, chips = _place()
        cps = [_remote(p[t].at[2 * chip[0] + chip[1]], recv[t].at[j], send_sems.at[t, j], recv_sems.at[t, j], (*chip, c))
               for j, chip in enumerate(chips) for t in range(n)]
        for cp in cps:
            cp.start()
        for cp in cps:
            cp.wait()

    return pl.pallas_call(
        body, name="grads_chip_exchange", out_shape=[jax.ShapeDtypeStruct((3,) + p.shape[1:], p.dtype) for p in ps],
        in_specs=[ANY] * n, out_specs=[ANY] * n,
        scratch_shapes=[pltpu.SemaphoreType.DMA((n, 3)), pltpu.SemaphoreType.DMA((n, 3))],
    )(*ps)


def grads_chip_sum(name, g, recv, recv2):
    _, a, b = recv.shape
    ta = min(a, SUM_TILE)
    my_slot = lambda: 2 * lax.axis_index("x") + lax.axis_index("y")

    def body(g_ref, r_ref, r2_ref, o_ref):
        own = g_ref[...] + r_ref[...]
        o_ref[...] = ((own + r2_ref[0].astype(F32)) + r2_ref[1].astype(F32)) + r2_ref[2].astype(F32)

    return pl.pallas_call(
        body, name=name, grid=(a // ta,),
        in_specs=[pl.BlockSpec((None, None, ta, b), lambda i: (my_slot(), lax.axis_index("c"), i, 0)),
                  pl.BlockSpec((None, ta, b), lambda i: (my_slot(), i, 0)),
                  pl.BlockSpec((3, ta, b), lambda i: (0, i, 0))],
        out_specs=pl.BlockSpec((None, ta, b), lambda i: (lax.axis_index("c"), i, 0)),
        out_shape=jax.ShapeDtypeStruct((DEPTH, a, b), F32),
        compiler_params=_cparams(1),
    )(g, recv, recv2)


def grads_pair_gather(reds):
    n = len(reds)

    def body(*refs):
        buf = refs[n:2 * n]
        send_sems, recv_sems = refs[2 * n:]
        x, y, c, _ = _place()
        sibling = (x, y, 1 - c)
        cps = [_remote(buf[t].at[c], buf[t].at[c], send_sems.at[t], recv_sems.at[t], sibling) for t in range(n)]
        for cp in cps:
            cp.start()
        for t in range(n):
            _remote(buf[t].at[c], buf[t].at[1 - c], send_sems.at[t], recv_sems.at[t], sibling).wait_recv()
        for cp in cps:
            cp.wait_send()

    return pl.pallas_call(
        body, name="grads_pair_gather", out_shape=[jax.ShapeDtypeStruct(r.shape, r.dtype) for r in reds],
        in_specs=[ANY] * n, out_specs=[ANY] * n, input_output_aliases={t: t for t in range(n)},
        scratch_shapes=[pltpu.SemaphoreType.DMA((n,)), pltpu.SemaphoreType.DMA((n,))],
    )(*reds)


def small_allreduce(v):
    m, n = v.shape

    def body(x_ref, sum_ref, all_ref, send_sems, recv_sems, local_sem):
        x, y, c, chips = _place()
        me, sibling = (x, y, c), (x, y, 1 - c)

        def rows(px, py, pc):
            return all_ref.at[pl.ds((4 * px + 2 * py + pc) * m, m), :]

        def copy(k, block, to, src=None):
            return pltpu.make_async_remote_copy(src_ref=rows(*block) if src is None else src, dst_ref=rows(*block),
                                                send_sem=send_sems.at[k], recv_sem=recv_sems.at[k],
                                                device_id=to, device_id_type=MESH)

        mine = pltpu.make_async_copy(x_ref, rows(*me), local_sem)
        mine.start()
        first = [copy(0, me, sibling, src=x_ref)]
        first += [copy(1 + j, me, (*chip, c), src=x_ref) for j, chip in enumerate(chips)]
        for cp in first:
            cp.start()
        passed = [copy(4 + j, (*chip, c), sibling) for j, chip in enumerate(chips)]
        for j, chip in enumerate(chips):
            copy(1 + j, (*chip, c), me).wait_recv()
            passed[j].start()
        copy(0, sibling, me).wait_recv()
        for j, chip in enumerate(chips):
            copy(4 + j, (*chip, 1 - c), me).wait_recv()
        for cp in first + passed:
            cp.wait_send()
        mine.wait()
        acc = all_ref[0:m, :]
        for dev in range(1, 8):
            acc = acc + all_ref[dev * m:(dev + 1) * m, :]
        sum_ref[...] = acc

    vm = pl.BlockSpec(memory_space=pltpu.VMEM)
    return pl.pallas_call(
        body, name="small_allreduce",
        out_shape=[jax.ShapeDtypeStruct((m, n), F32), jax.ShapeDtypeStruct((8 * m, n), F32)],
        in_specs=[vm], out_specs=[vm, vm],
        scratch_shapes=[pltpu.SemaphoreType.DMA((7,)), pltpu.SemaphoreType.DMA((7,)), pltpu.SemaphoreType.DMA],
    )(v)[0]


def reduce_scatter_grads(names, gs):
    recv = grads_pair_exchange(gs)
    parts = [grads_pair_sum("grads_pair_sum_" + n, g, r) for n, g, r in zip(names, gs, recv)]
    recv2 = grads_chip_exchange(parts)
    reds = [grads_chip_sum("grads_chip_sum_" + n, g, r, r2) for n, g, r, r2 in zip(names, gs, recv, recv2)]
    return grads_pair_gather(reds)


def adamw(name, w, g, m, v, tr, tc=None):
    r, cols = w.shape
    tc = cols if tc is None else tc

    def body(w_ref, g_ref, m_ref, v_ref, d_ref, nm_ref, nv_ref):
        gv = g_ref[...]
        nm = ADAM_B1 * m_ref[...] + (1.0 - ADAM_B1) * gv
        nv = ADAM_B2 * v_ref[...] + (1.0 - ADAM_B2) * (gv * gv)
        m_hat = nm / (1.0 - ADAM_B1 ** ADAM_STEP)
        v_hat = nv / (1.0 - ADAM_B2 ** ADAM_STEP)
        d_ref[...] = -ADAM_LR * (m_hat / (jnp.sqrt(v_hat) + ADAM_EPS) + ADAM_WD * w_ref[...])
        nm_ref[...] = nm
        nv_ref[...] = nv

    spec = pl.BlockSpec((tr, tc), lambda i, j: (i, j))
    return pl.pallas_call(
        body, name=name, grid=(r // tr, cols // tc), in_specs=[spec] * 4, out_specs=[spec] * 3,
        out_shape=[jax.ShapeDtypeStruct((r, cols), F32)] * 3, compiler_params=_cparams(2),
    )(w, g, m, v)


ADAM_ROWS = {"w_ada": 512, "dw_w": 62, "pw2_w": 256, "sconv_w": 8, "w_proj_a": 512, "w_proj_b": 512, "w_proj_c": 512,
             "w_out": 256}
ADAM_W_IN_COLS = 256

WEIGHT_NAMES = ("w_ada", "b_ada", "norm_g", "w_in", "q_norm_g", "k_norm_g", "sinks", "dw_w", "dw_b", "ln_g", "ln_b",
                "pw2_w", "pw2_b", "sconv_w", "a_log", "dt_bias", "dn_norm_g", "w_proj_a", "w_proj_b", "w_proj_c", "w_out")


def kernel(x, c, w_ada, b_ada, norm_g, w_in, q_norm_g, k_norm_g, sinks, dw_w, dw_b, ln_g, ln_b, pw2_w, pw2_b, sconv_w, a_log, dt_bias, dn_norm_g, w_proj_a, w_proj_b, w_proj_c, w_out, loss_target, m_w_ada, m_b_ada, m_norm_g, m_w_in, m_q_norm_g, m_k_norm_g, m_sinks, m_dw_w, m_dw_b, m_ln_g, m_ln_b, m_pw2_w, m_pw2_b, m_sconv_w, m_a_log, m_dt_bias, m_dn_norm_g, m_w_proj_a, m_w_proj_b, m_w_proj_c, m_w_out, v_w_ada, v_b_ada, v_norm_g, v_w_in, v_q_norm_g, v_k_norm_g, v_sinks, v_dw_w, v_dw_b, v_ln_g, v_ln_b, v_pw2_w, v_pw2_b, v_sconv_w, v_a_log, v_dt_bias, v_dn_norm_g, v_w_proj_a, v_w_proj_b, v_w_proj_c, v_w_out):
    args = dict(locals())
    w = {n: args[n] for n in WEIGHT_NAMES}
    mom = {n: args["m_" + n] for n in WEIGHT_NAMES}
    var = {n: args["v_" + n] for n in WEIGHT_NAMES}

    chip = 2 * lax.axis_index("x") + lax.axis_index("y")
    slots = []
    for n in SHARDED:
        own = w[n] if n in GATHER_F32 else w[n].astype(BF16)
        slots.append(lax.dynamic_update_slice(lax.empty((N_CHIPS,) + own.shape, own.dtype), own[None], (chip, 0, 0, 0)))
    gathered = dict(zip(SHARDED, weights_allgather(slots)))
    layers = []
    for l in range(DEPTH):
        lw = {n: w[n][l] for n in SMALL}
        for n, axis in SHARDED.items():
            lw[n] = gathered[n][:, l] if n == "w_in" else _join_layer(gathered[n][:, l], axis)
        layers.append(prep_layer(lw))

    c8 = jnp.tile(c, (8, 1))
    act, saved = x[0], []
    for l in range(DEPTH):
        act, s = layer_fwd(str(l), act, c8, layers[l])
        saved.append(s)
    dact, loss_part = loss_head("loss_head", act, loss_target[0], TM)
    loss = lax.psum(loss_part[0, 0], ("x", "y", "c"))
    layer_grads = [None] * DEPTH
    for l in reversed(range(DEPTH)):
        dact, layer_grads[l] = layer_bwd(str(l), dact, c8, layers[l], saved[l])

    by_chip = [jnp.stack([layer_grads[l][n] if n == "w_in" else _split_layer(layer_grads[l][n], SHARDED[n])
                          for l in range(DEPTH)], axis=1) for n in REDUCE_BIG]
    final_grads = dict(zip(REDUCE_BIG, reduce_scatter_grads(REDUCE_BIG, by_chip)))
    small_names = SMALL + GATHER_F32
    small_shapes = {n: (DEPTH,) + layer_grads[0][n].shape for n in small_names}
    small_full = {n: jnp.stack([layer_grads[l][n] for l in range(DEPTH)]) for n in small_names}
    small_sum = unpack_small(small_allreduce(pack_small(small_full, small_names, SMALL_GRAD_ROWS)), small_names, small_shapes)
    for n in GATHER_F32:
        width = w[n].shape[2]
        final_grads[n] = lax.dynamic_slice_in_dim(small_sum[n], chip * width, width, axis=2)
    final_grads.update({n: small_sum[n] for n in SMALL})
    small_grads = pack_small(final_grads, SMALL, SMALL_ROWS)

    delta, new_m, new_v = {}, {}, {}
    for n in SHARDED:
        shp = w[n].shape
        if n == "w_in":
            two_d = lambda a: jnp.transpose(a, (2, 0, 1)).reshape(shp[2], shp[0] * shp[1])
            back = lambda a: jnp.transpose(a.reshape(shp[2], shp[0], shp[1]), (1, 2, 0))
            g2 = two_d(final_grads[n])
            final_grads[n] = back(g2)
            d, nm, nv = adamw("adamw_" + n, two_d(w[n]), g2, two_d(mom[n]), two_d(var[n]), shp[2], ADAM_W_IN_COLS)
        else:
            two_d = lambda a, shp=shp: a.reshape(shp[0] * shp[1], shp[2])
            back = lambda a, shp=shp: a.reshape(shp)
            d, nm, nv = adamw("adamw_" + n, two_d(w[n]), two_d(final_grads[n]), two_d(mom[n]), two_d(var[n]), ADAM_ROWS[n])
        delta[n], new_m[n], new_v[n] = back(d), back(nm), back(nv)
    d, nm, nv = adamw("adamw_small", pack_small(w, SMALL, SMALL_ROWS), small_grads, pack_small(mom, SMALL, SMALL_ROWS),
                      pack_small(var, SMALL, SMALL_ROWS), SMALL_ROWS)
    delta.update(unpack_small(d, SMALL, small_shapes))
    new_m.update(unpack_small(nm, SMALL, small_shapes))
    new_v.update(unpack_small(nv, SMALL, small_shapes))

    return (loss, dact[None], *[final_grads[n] for n in WEIGHT_NAMES], *[delta[n] for n in WEIGHT_NAMES],
            *[new_m[n] for n in WEIGHT_NAMES], *[new_v[n] for n in WEIGHT_NAMES])
```

```python
import functools

import numpy as np
import jax
import jax.numpy as jnp
from jax import lax
from jax.experimental import pallas as pl
from jax.experimental.pallas import tpu as pltpu

F32 = jnp.float32
BF16 = jnp.bfloat16
MESH = pl.DeviceIdType.MESH

D_MODEL = 1024
DEPTH = 2
ATT_HEADS = 8
ATT_HEAD_DIM = 64
WINDOW = 128
CONV_K = 31
DN_HEADS = 4
DN_CONV_K = 4
DN_CHUNK = 64
EPS = 1e-6
NEG_INF = -1e30
N_CHIPS = 4
D_IN = 7944

ADAM_LR = 0.001
ADAM_B1 = 0.9
ADAM_B2 = 0.999
ADAM_EPS = 1e-08
ADAM_WD = 0.01
ADAM_STEP = 10

VMEM_LIMIT = 56 * 1024 * 1024

P_QA, P_ZA, P_GLU, P_ZB, P_ZC, P_MG, P_QKV, P_KA, P_VA, P_AB, P_TOTAL = (
    0, 512, 1024, 2048, 2560, 3072, 6144, 7680, 7808, 7936, 8064)
HEAD_ORDER = (0, 4, 1, 5, 2, 6, 3, 7)


def _in_pieces():
    p = [(0 + 64 * h, 64) for h in HEAD_ORDER]
    p += [(768 + 64 * h, 64) for h in HEAD_ORDER]
    for g in range(4):
        p += [(1280 + 128 * g, 128), (1792 + 128 * g, 128)]
    p += [(2304, 512), (4360, 512), (4872, 3072), (2816, 1536), (512, 128), (640, 128), (4352, 8)]
    return p


def _perm_heads_rows(w):
    return jnp.concatenate([w[64 * h:64 * h + 64] for h in HEAD_ORDER], axis=0)


def _unperm_heads_rows(w):
    inv = [HEAD_ORDER.index(h) for h in range(8)]
    return jnp.concatenate([w[64 * s:64 * s + 64] for s in inv], axis=0)


def _split_bf16(a, terms):
    out, rest = [], a.astype(F32)
    for _ in range(terms - 1):
        out.append(rest.astype(BF16))
        rest = rest - out[-1].astype(F32)
    return out + [rest.astype(BF16)]


def _dot(a, b, dims, exact):
    d = lambda p, q: lax.dot_general(p, q, (dims, ((), ())), preferred_element_type=F32)
    if exact:
        (ah, al), (bh, bl) = _split_bf16(a, 2), _split_bf16(b, 2)
        return d(ah, bh) + (d(ah, bl) + d(al, bh))
    return d(a.astype(BF16), b.astype(BF16))


def _make_mm(exact):
    @jax.custom_vjp
    def nn(a, b):
        return _dot(a, b, ((1,), (0,)), exact)

    @jax.custom_vjp
    def nt(a, b):
        return _dot(a, b, ((1,), (1,)), exact)

    @jax.custom_vjp
    def tn(a, b):
        return _dot(a, b, ((0,), (0,)), exact)

    nn.defvjp(lambda a, b: (nn(a, b), (a, b)),
              lambda r, g: (nt(g, r[1]).astype(r[0].dtype), tn(r[0], g).astype(r[1].dtype)))
    nt.defvjp(lambda a, b: (nt(a, b), (a, b)),
              lambda r, g: (nn(g, r[1]).astype(r[0].dtype), tn(g, r[0]).astype(r[1].dtype)))
    tn.defvjp(lambda a, b: (tn(a, b), (a, b)),
              lambda r, g: (nt(r[1], g).astype(r[0].dtype), nn(r[0], g).astype(r[1].dtype)))
    return nn, nt, tn


mm, mm_nt, mm_tn = _make_mm(False)
xmm, xmm_nt, xmm_tn = _make_mm(True)


@jax.custom_vjp
def sel_mm(m, g):
    mb = m.astype(BF16)
    parts = [jnp.dot(mb, p, preferred_element_type=F32) for p in _split_bf16(g, 3)]
    return parts[0] + (parts[1] + parts[2])


def _sel_mm_bwd(m, dy):
    mb = m.astype(BF16)
    parts = [lax.dot_general(mb, p, (((0,), (0,)), ((), ())), preferred_element_type=F32) for p in _split_bf16(dy, 3)]
    return jnp.zeros_like(m), parts[0] + (parts[1] + parts[2])


sel_mm.defvjp(lambda m, g: (sel_mm(m, g), m), _sel_mm_bwd)


@jax.custom_vjp
def tri_inv(*mats):
    n = mats[0].shape[0]
    eye = jnp.where(lax.broadcasted_iota(jnp.int32, (n, n), 0) == lax.broadcasted_iota(jnp.int32, (n, n), 1), 1.0, 0.0)
    ts = [eye - a for a in mats]
    pws = list(mats)
    for _ in range(5):
        pws = [xmm(pw, pw) for pw in pws]
        ts = [t + xmm(t, pw) for t, pw in zip(ts, pws)]
    return tuple(ts)


def _tri_inv_bwd(ts, dts):
    inner = [xmm_nt(dt, t) for t, dt in zip(ts, dts)]
    return tuple(-xmm_tn(t, m) for t, m in zip(ts, inner))


tri_inv.defvjp(lambda *mats: (tri_inv(*mats),) * 2, _tri_inv_bwd)


def _sigmoid(x):
    return 1.0 / (1.0 + jnp.exp(-x))


def _silu(x):
    return x * _sigmoid(x)


def _softplus(x):
    return jnp.maximum(x, 0.0) + jnp.log(1.0 + jnp.exp(-jnp.abs(x)))


def _cparams(n_grid):
    return pltpu.CompilerParams(dimension_semantics=("arbitrary",) * n_grid, vmem_limit_bytes=VMEM_LIMIT)


def _row_spec(tm, width, colblk):
    return pl.BlockSpec((tm, width), lambda i, cb=colblk: (i, cb))


def _const_spec(shape):
    nd = len(shape)
    return pl.BlockSpec(tuple(shape), lambda i, nd=nd: (0,) * nd)


def rowwise_fwd(name, f, rows, consts, outs, tm):
    n_r, n_c = len(rows), len(consts)
    t = rows[0][0].shape[0]

    def body(*refs):
        vals = [r[...] for r in refs[:n_r + n_c]]
        res = f(*vals)
        if not isinstance(res, (tuple, list)):
            res = (res,)
        for o_ref, v in zip(refs[n_r + n_c:], res):
            o_ref[...] = v.astype(o_ref.dtype)

    return pl.pallas_call(
        body, name=name, grid=(t // tm,),
        in_specs=[_row_spec(tm, w, cb) for _, w, cb in rows] + [_const_spec(c.shape) for c in consts],
        out_specs=[_row_spec(tm, w, 0) for w, _ in outs],
        out_shape=[jax.ShapeDtypeStruct((t, w), dt) for w, dt in outs],
        compiler_params=_cparams(1),
    )(*[a for a, _, _ in rows], *consts)


def rowwise_bwd(name, f, rows, consts, cts, row_grad_dtypes, tm):
    n_r, n_c, n_ct = len(rows), len(consts), len(cts)
    t = rows[0][0].shape[0]
    keep = [k for k, dt in enumerate(row_grad_dtypes) if dt is not None]

    def body(*refs):
        ins = [r[...].astype(F32) for r in refs[:n_r + n_c]]
        g_out = [r[...].astype(F32) for r in refs[n_r + n_c:n_r + n_c + n_ct]]
        out_refs = refs[n_r + n_c + n_ct:]

        def fw(*a):
            res = f(*a)
            return tuple(res) if isinstance(res, (tuple, list)) else (res,)

        _, vjp = jax.vjp(fw, *ins)
        grads = vjp(tuple(g_out))
        for o_ref, k in zip(out_refs[:len(keep)], keep):
            o_ref[...] = grads[k].astype(o_ref.dtype)
        first = pl.program_id(0) == 0
        for o_ref, g in zip(out_refs[len(keep):], grads[n_r:]):
            @pl.when(first)
            def _(o_ref=o_ref, g=g):
                o_ref[...] = g

            @pl.when(jnp.logical_not(first))
            def _(o_ref=o_ref, g=g):
                o_ref[...] += g

    return pl.pallas_call(
        body, name=name, grid=(t // tm,),
        in_specs=[_row_spec(tm, w, cb) for _, w, cb in rows] + [_const_spec(c.shape) for c in consts]
        + [_row_spec(tm, w, cb) for _, w, cb in cts],
        out_specs=[_row_spec(tm, rows[k][1], 0) for k in keep] + [_const_spec(c.shape) for c in consts],
        out_shape=[jax.ShapeDtypeStruct((t, rows[k][1]), row_grad_dtypes[k]) for k in keep]
        + [jax.ShapeDtypeStruct(c.shape, F32) for c in consts],
        compiler_params=_cparams(1),
    )(*[a for a, _, _ in rows], *consts, *[a for a, _, _ in cts])


def f_norm_mod(x, g, scale, shift):
    y = x * lax.rsqrt(jnp.mean(x * x, axis=-1, keepdims=True) + EPS) * g
    return y * (1.0 + scale) + shift


def f_conf_tail(u, zb, ln_g, ln_b, pw2_w, pw2_b):
    mu = jnp.mean(u, axis=-1, keepdims=True)
    xc = u - mu
    var = jnp.mean(xc * xc, axis=-1, keepdims=True)
    y = _silu(xc * lax.rsqrt(var + EPS) * ln_g + ln_b)
    return (mm(y, pw2_w) + pw2_b) * _silu(zb)


def f_merge(ya, yb, yc, mg, x, gate, wpa, wpb, wpc, wout):
    d = D_MODEL
    merged = (_sigmoid(mg[:, :d]) * mm(ya, wpa) + _sigmoid(mg[:, d:2 * d]) * mm(yb, wpb)
              + _sigmoid(mg[:, 2 * d:]) * mm(yc, wpc))
    return x + gate * mm(merged, wout)


def matmul_nn(name, a, b, out_dtype, tm, tn, tk):
    m, k = a.shape
    n = b.shape[1]
    nk = k // tk

    def body(a_ref, b_ref, o_ref, *acc):
        part = jnp.dot(a_ref[...].astype(BF16), b_ref[...].astype(BF16), preferred_element_type=F32)
        if nk == 1:
            o_ref[...] = part.astype(o_ref.dtype)
            return
        kk = pl.program_id(2)
        acc_ref = acc[0]

        @pl.when(kk == 0)
        def _():
            acc_ref[...] = part

        @pl.when(kk > 0)
        def _():
            acc_ref[...] += part

        @pl.when(kk == nk - 1)
        def _():
            o_ref[...] = acc_ref[...].astype(o_ref.dtype)

    return pl.pallas_call(
        body, name=name, grid=(m // tm, n // tn, nk),
        in_specs=[pl.BlockSpec((tm, tk), lambda i, j, kk: (i, kk)), pl.BlockSpec((tk, tn), lambda i, j, kk: (kk, j))],
        out_specs=pl.BlockSpec((tm, tn), lambda i, j, kk: (i, j)),
        out_shape=jax.ShapeDtypeStruct((m, n), out_dtype),
        scratch_shapes=[] if nk == 1 else [pltpu.VMEM((tm, tn), F32)],
        compiler_params=_cparams(3),
    )(a, b)


def ada_fwd(name, c8, w_ada, b_ada):
    def body(c_ref, w_ref, b_ref, o_ref):
        o_ref[...] = mm(_silu(c_ref[...]), w_ref[...]) + b_ref[...]

    return pl.pallas_call(
        body, name=name, out_shape=jax.ShapeDtypeStruct((8, 3 * D_MODEL), F32),
        compiler_params=pltpu.CompilerParams(vmem_limit_bytes=VMEM_LIMIT),
    )(c8, w_ada, b_ada)


def ada_bwd(name, c8, dmod8):
    tn = 768

    def body(c_ref, d_ref, o_ref):
        row0 = lax.broadcasted_iota(jnp.int32, (8, 1), 0) == 0
        sc = jnp.where(row0, _silu(c_ref[...]), 0.0)
        o_ref[...] = mm_tn(sc, d_ref[...])

    return pl.pallas_call(
        body, name=name, grid=(3 * D_MODEL // tn,),
        in_specs=[pl.BlockSpec((8, D_MODEL), lambda j: (0, 0)), pl.BlockSpec((8, tn), lambda j: (0, j))],
        out_specs=pl.BlockSpec((D_MODEL, tn), lambda j: (0, j)),
        out_shape=jax.ShapeDtypeStruct((D_MODEL, 3 * D_MODEL), F32),
        compiler_params=_cparams(1),
    )(c8, dmod8)


def _f_attn(first_block, q, za, kc, vc, kp, vp, qg, kg, sinks):
    w = WINDOW
    lane = lax.broadcasted_iota(jnp.int32, (1, 128), 1)
    halves = [lane < 64, lane >= 64]

    def rms_halves(x, g):
        x2 = x * x
        s0 = jnp.sum(jnp.where(halves[0], x2, 0.0), axis=-1, keepdims=True)
        s1 = jnp.sum(jnp.where(halves[1], x2, 0.0), axis=-1, keepdims=True)
        r = jnp.where(halves[0], lax.rsqrt(s0 / 64.0 + EPS), lax.rsqrt(s1 / 64.0 + EPS))
        return x * r * g

    kcat = rms_halves(jnp.concatenate([kp, kc], axis=0), kg)
    vcat = jnp.concatenate([vp, vc], axis=0)
    qi = lax.broadcasted_iota(jnp.int32, (w, 2 * w), 0)
    kj = lax.broadcasted_iota(jnp.int32, (w, 2 * w), 1)
    dist = qi + w - kj
    valid = (dist >= 0) & (dist < w) & (jnp.logical_not(first_block) | (kj >= w))
    distf = dist.astype(F32)
    outs = []
    for grp in range(4):
        qn = rms_halves(q[:, 128 * grp:128 * grp + 128], qg) * (ATT_HEAD_DIM ** -0.5)
        o_grp = jnp.zeros((w, 128), F32)
        for half in range(2):
            head = HEAD_ORDER[2 * grp + half]
            slope = 2.0 ** (-8.0 * (head + 1) / ATT_HEADS)
            sink = jnp.sum(jnp.where(lane == head, sinks, 0.0), axis=-1, keepdims=True)
            s = mm_nt(jnp.where(halves[half], qn, 0.0), kcat) - slope * distf
            s = jnp.where(valid, s, NEG_INF)
            m = lax.stop_gradient(jnp.maximum(jnp.max(s, axis=-1, keepdims=True), sink))
            p = jnp.exp(s - m)
            denom = jnp.sum(p, axis=-1, keepdims=True) + jnp.exp(sink - m)
            o_grp = o_grp + mm(p / denom, jnp.where(halves[half], vcat, 0.0))
        outs.append(o_grp)
    return jnp.concatenate(outs, axis=1) * _silu(za)


def attn_fwd(name, proj, qg, kg, sinks):
    t = proj.shape[0]
    nb = t // WINDOW

    def body(q_ref, za_ref, kc_ref, vc_ref, kp_ref, vp_ref, qg_ref, kg_ref, s_ref, o_ref):
        first = pl.program_id(0) == 0
        o_ref[...] = _f_attn(first, q_ref[...], za_ref[...], kc_ref[...], vc_ref[...], kp_ref[...], vp_ref[...],
                             qg_ref[...], kg_ref[...], s_ref[...])

    cur = lambda cb: (lambda i: (i, cb))
    prev = lambda cb: (lambda i: (jnp.maximum(i - 1, 0), cb))
    return pl.pallas_call(
        body, name=name, grid=(nb,),
        in_specs=[pl.BlockSpec((WINDOW, 512), cur(P_QA // 512)), pl.BlockSpec((WINDOW, 512), cur(P_ZA // 512)),
                  pl.BlockSpec((WINDOW, 128), cur(P_KA // 128)), pl.BlockSpec((WINDOW, 128), cur(P_VA // 128)),
                  pl.BlockSpec((WINDOW, 128), prev(P_KA // 128)), pl.BlockSpec((WINDOW, 128), prev(P_VA // 128)),
                  _const_spec((1, 128)), _const_spec((1, 128)), _const_spec((1, 128))],
        out_specs=pl.BlockSpec((WINDOW, 512), lambda i: (i, 0)),
        out_shape=jax.ShapeDtypeStruct((t, 512), F32),
        compiler_params=_cparams(1),
    )(proj, proj, proj, proj, proj, proj, qg, kg, sinks)


def attn_bwd(name, proj, qg, kg, sinks, dya):
    t = proj.shape[0]
    nb = t // WINDOW

    def body(q_ref, za_ref, kc_ref, vc_ref, kp_ref, vp_ref, qg_ref, kg_ref, s_ref, dy_ref,
             dqz_ref, dkv_ref, dqg_ref, dkg_ref, ds_ref, carry_ref):
        j = pl.program_id(0)
        first = j == nb - 1

        @pl.when(j == 0)
        def _():
            carry_ref[...] = jnp.zeros_like(carry_ref)
            dqg_ref[...] = jnp.zeros_like(dqg_ref)
            dkg_ref[...] = jnp.zeros_like(dkg_ref)
            ds_ref[...] = jnp.zeros_like(ds_ref)

        ins = [r[...] for r in (q_ref, za_ref, kc_ref, vc_ref, kp_ref, vp_ref, qg_ref, kg_ref, s_ref)]
        _, vjp = jax.vjp(functools.partial(_f_attn, first), *ins)
        dq, dza, dkc, dvc, dkp, dvp, dqg, dkg, dsk = vjp(dy_ref[...])
        dqz_ref[:, 0:512] = dq.astype(dqz_ref.dtype)
        dqz_ref[:, 512:1024] = dza.astype(dqz_ref.dtype)
        dkv_ref[:, 0:128] = (dkc + carry_ref[0]).astype(dkv_ref.dtype)
        dkv_ref[:, 128:256] = (dvc + carry_ref[1]).astype(dkv_ref.dtype)
        carry_ref[0] = dkp
        carry_ref[1] = dvp
        dqg_ref[...] += dqg
        dkg_ref[...] += dkg
        ds_ref[...] += dsk

    cur = lambda cb: (lambda j: (nb - 1 - j, cb))
    prev = lambda cb: (lambda j: (jnp.maximum(nb - 2 - j, 0), cb))
    return pl.pallas_call(
        body, name=name, grid=(nb,),
        in_specs=[pl.BlockSpec((WINDOW, 512), cur(P_QA // 512)), pl.BlockSpec((WINDOW, 512), cur(P_ZA // 512)),
                  pl.BlockSpec((WINDOW, 128), cur(P_KA // 128)), pl.BlockSpec((WINDOW, 128), cur(P_VA // 128)),
                  pl.BlockSpec((WINDOW, 128), prev(P_KA // 128)), pl.BlockSpec((WINDOW, 128), prev(P_VA // 128)),
                  _const_spec((1, 128)), _const_spec((1, 128)), _const_spec((1, 128)),
                  pl.BlockSpec((WINDOW, 512), cur(0))],
        out_specs=[pl.BlockSpec((WINDOW, 1024), cur(0)), pl.BlockSpec((WINDOW, 256), cur(0)),
                   _const_spec((1, 128)), _const_spec((1, 128)), _const_spec((1, 128))],
        out_shape=[jax.ShapeDtypeStruct((t, 1024), BF16), jax.ShapeDtypeStruct((t, 256), BF16),
                   jax.ShapeDtypeStruct((1, 128), F32), jax.ShapeDtypeStruct((1, 128), F32),
                   jax.ShapeDtypeStruct((1, 128), F32)],
        scratch_shapes=[pltpu.VMEM((2, WINDOW, 128), F32)],
        compiler_params=_cparams(1),
    )(proj, proj, proj, proj, proj, proj, qg, kg, sinks, dya)


CONV_ROWS = 256


def _conv_taps(src_ref, w_ref, n_taps, base, t):
    for r0 in range(0, t, CONV_ROWS):
        acc = w_ref[0:1, :] * src_ref[pl.ds(r0 + base, CONV_ROWS), :]
        for k in range(1, n_taps):
            acc = acc + w_ref[k:k + 1, :] * src_ref[pl.ds(r0 + base + k, CONV_ROWS), :]
        yield r0, acc


def _conv_wgrad(dy_ref, src_ref, n_taps, base, t, dy_base=0):
    out = []
    for k in range(n_taps):
        acc = jnp.zeros((8, 128), F32)
        for r0 in range(0, t, CONV_ROWS):
            prod = dy_ref[pl.ds(r0 + dy_base, CONV_ROWS), :] * src_ref[pl.ds(r0 + base + k, CONV_ROWS), :]
            acc = acc + jnp.sum(prod.reshape(CONV_ROWS // 8, 8, 128), axis=0)
        out.append(jnp.sum(acc, axis=0, keepdims=True))
    return out


def glu_conv_fwd(name, proj, w32, bias):
    t = proj.shape[0]
    pad = 32

    def body(x_ref, w_ref, b_ref, o_ref, u_ref):
        u_ref[0:pad, :] = jnp.zeros((pad, 128), F32)
        u_ref[pad:pad + t, :] = x_ref[:, 0:128] * _sigmoid(x_ref[:, 128:256])
        for r0, acc in _conv_taps(u_ref, w_ref, CONV_K, pad - (CONV_K - 1), t):
            o_ref[pl.ds(r0, CONV_ROWS), :] = acc + b_ref[...]

    return pl.pallas_call(
        body, name=name, grid=(4,),
        in_specs=[pl.BlockSpec((t, 256), lambda cb: (0, P_GLU // 256 + cb)), pl.BlockSpec((32, 128), lambda cb: (0, cb)),
                  pl.BlockSpec((1, 128), lambda cb: (0, cb))],
        out_specs=pl.BlockSpec((t, 128), lambda cb: (0, cb)),
        out_shape=jax.ShapeDtypeStruct((t, 512), F32),
        scratch_shapes=[pltpu.VMEM((t + pad, 128), F32)],
        compiler_params=_cparams(1),
    )(proj, w32, bias)


def glu_conv_bwd(name, proj, w32, dub):
    t = proj.shape[0]
    pad = 32
    k1 = CONV_K - 1

    def body(x_ref, w_ref, dy_ref, dx_ref, dw_ref, db_ref, u_ref, dyp_ref, wrev_ref):
        val = x_ref[:, 0:128]
        sg = _sigmoid(x_ref[:, 128:256])
        u_ref[0:pad, :] = jnp.zeros((pad, 128), F32)
        u_ref[pad:pad + t, :] = val * sg
        dyp_ref[0:t, :] = dy_ref[...]
        dyp_ref[t:t + pad, :] = jnp.zeros((pad, 128), F32)
        for k in range(CONV_K):
            wrev_ref[k:k + 1, :] = w_ref[k1 - k:k1 - k + 1, :]
        wrev_ref[CONV_K:32, :] = jnp.zeros((32 - CONV_K, 128), F32)
        for r0, du in _conv_taps(dyp_ref, wrev_ref, CONV_K, 0, t):
            v = x_ref[pl.ds(r0, CONV_ROWS), 0:128]
            s = _sigmoid(x_ref[pl.ds(r0, CONV_ROWS), 128:256])
            dx_ref[pl.ds(r0, CONV_ROWS), 0:128] = (du * s).astype(dx_ref.dtype)
            dx_ref[pl.ds(r0, CONV_ROWS), 128:256] = (du * v * s * (1.0 - s)).astype(dx_ref.dtype)
        dws = _conv_wgrad(dyp_ref, u_ref, CONV_K, pad - k1, t)
        for k in range(CONV_K):
            dw_ref[k:k + 1, :] = dws[k]
        dw_ref[CONV_K:32, :] = jnp.zeros((32 - CONV_K, 128), F32)
        db_ref[...] = jnp.sum(dy_ref[...], axis=0, keepdims=True)

    return pl.pallas_call(
        body, name=name, grid=(4,),
        in_specs=[pl.BlockSpec((t, 256), lambda cb: (0, P_GLU // 256 + cb)), pl.BlockSpec((32, 128), lambda cb: (0, cb)),
                  pl.BlockSpec((t, 128), lambda cb: (0, cb))],
        out_specs=[pl.BlockSpec((t, 256), lambda cb: (0, cb)), pl.BlockSpec((32, 128), lambda cb: (0, cb)),
                   pl.BlockSpec((1, 128), lambda cb: (0, cb))],
        out_shape=[jax.ShapeDtypeStruct((t, 1024), BF16), jax.ShapeDtypeStruct((32, 512), F32),
                   jax.ShapeDtypeStruct((1, 512), F32)],
        scratch_shapes=[pltpu.VMEM((t + pad, 128), F32), pltpu.VMEM((t + pad, 128), F32), pltpu.VMEM((32, 128), F32)],
        compiler_params=_cparams(1),
    )(proj, w32, dub)


def sconv_fwd(name, proj, w8):
    t = proj.shape[0]
    pad = 8
    k1 = DN_CONV_K - 1

    def body(x_ref, w_ref, o_ref, xp_ref):
        xp_ref[0:pad, :] = jnp.zeros((pad, 128), F32)
        xp_ref[pad:pad + t, :] = x_ref[...]
        for r0, acc in _conv_taps(xp_ref, w_ref, DN_CONV_K, pad - k1, t):
            o_ref[pl.ds(r0, CONV_ROWS), :] = _silu(acc)

    return pl.pallas_call(
        body, name=name, grid=(12,),
        in_specs=[pl.BlockSpec((t, 128), lambda cb: (0, P_QKV // 128 + cb)), pl.BlockSpec((8, 128), lambda cb: (0, cb))],
        out_specs=pl.BlockSpec((t, 128), lambda cb: (0, cb)),
        out_shape=jax.ShapeDtypeStruct((t, 1536), F32),
        scratch_shapes=[pltpu.VMEM((t + pad, 128), F32)],
        compiler_params=_cparams(1),
    )(proj, w8)


def sconv_bwd(name, proj, w8, dqkv):
    t = proj.shape[0]
    pad = 8
    k1 = DN_CONV_K - 1

    def body(x_ref, w_ref, dy_ref, dx_ref, dw_ref, xp_ref, dpp_ref, wrev_ref):
        xp_ref[0:pad, :] = jnp.zeros((pad, 128), F32)
        xp_ref[pad:pad + t, :] = x_ref[...]
        for r0, pre in _conv_taps(xp_ref, w_ref, DN_CONV_K, pad - k1, t):
            s = _sigmoid(pre)
            dpp_ref[pl.ds(r0, CONV_ROWS), :] = dy_ref[pl.ds(r0, CONV_ROWS), :] * (s * (1.0 + pre * (1.0 - s)))
        dpp_ref[t:t + pad, :] = jnp.zeros((pad, 128), F32)
        for k in range(DN_CONV_K):
            wrev_ref[k:k + 1, :] = w_ref[k1 - k:k1 - k + 1, :]
        wrev_ref[DN_CONV_K:8, :] = jnp.zeros((8 - DN_CONV_K, 128), F32)
        for r0, dx in _conv_taps(dpp_ref, wrev_ref, DN_CONV_K, 0, t):
            dx_ref[pl.ds(r0, CONV_ROWS), :] = dx.astype(dx_ref.dtype)
        dws = _conv_wgrad(dpp_ref, xp_ref, DN_CONV_K, pad - k1, t)
        for k in range(DN_CONV_K):
            dw_ref[k:k + 1, :] = dws[k]
        dw_ref[DN_CONV_K:8, :] = jnp.zeros((8 - DN_CONV_K, 128), F32)

    return pl.pallas_call(
        body, name=name, grid=(12,),
        in_specs=[pl.BlockSpec((t, 128), lambda cb: (0, P_QKV // 128 + cb)), pl.BlockSpec((8, 128), lambda cb: (0, cb)),
                  pl.BlockSpec((t, 128), lambda cb: (0, cb))],
        out_specs=[pl.BlockSpec((t, 128), lambda cb: (0, cb)), pl.BlockSpec((8, 128), lambda cb: (0, cb))],
        out_shape=[jax.ShapeDtypeStruct((t, 1536), BF16), jax.ShapeDtypeStruct((8, 1536), F32)],
        scratch_shapes=[pltpu.VMEM((t + pad, 128), F32), pltpu.VMEM((t + pad, 128), F32), pltpu.VMEM((8, 128), F32)],
        compiler_params=_cparams(1),
    )(proj, w8, dqkv)


def _f_delta_step(qkv, ab, zc, s0, s1, s2, s3, a_log, dt_bias, dn_g):
    cs = DN_CHUNK
    n = 2 * cs
    states = (s0, s1, s2, s3)
    lane = lax.broadcasted_iota(jnp.int32, (1, 128), 1)
    ri = lax.broadcasted_iota(jnp.int32, (n, n), 0)
    ci = lax.broadcasted_iota(jnp.int32, (n, n), 1)
    same = (ri // cs) == (ci // cs)
    lower = same & (ri >= ci)
    strict = same & (ri > ci)
    sums = jnp.concatenate([jnp.where(lower, 1.0, 0.0), jnp.where(same, 1.0, 0.0), jnp.where(ci < cs, 1.0, 0.0),
                            jnp.where(ci >= cs, 1.0, 0.0)], axis=0)
    top = lax.broadcasted_iota(jnp.int32, (n, 1), 0) < cs

    def pick(row, idx):
        return jnp.sum(jnp.where(lane == idx, row, 0.0), axis=-1, keepdims=True)

    def l2n(x):
        return x * lax.rsqrt(jnp.sum(x * x, axis=-1, keepdims=True) + EPS)

    n_chunks = qkv.shape[0] // cs
    units = [(k, pair) for k in range(n_chunks) for pair in range(2)]

    pre = []
    for k, pair in units:
        hs = (2 * pair, 2 * pair + 1)
        rows = slice(k * cs, (k + 1) * cs)
        stack = lambda f: jnp.concatenate([f(hs[0]), f(hs[1])], axis=0)
        qd = l2n(stack(lambda h: qkv[rows, 128 * h:128 * h + 128])) * (128 ** -0.5)
        kd = l2n(stack(lambda h: qkv[rows, 512 + 128 * h:512 + 128 * h + 128]))
        vd = stack(lambda h: qkv[rows, 1024 + 128 * h:1024 + 128 * h + 128])
        beta = _sigmoid(stack(lambda h: pick(ab[rows], 4 + h)))
        g = stack(lambda h: -jnp.exp(pick(a_log, h)) * _softplus(pick(ab[rows], h) + pick(dt_bias, h)))
        g_sums = sel_mm(sums, g * jnp.ones((1, n), F32))
        gc_col = g_sums[0:n]
        gl_b = g_sums[n:2 * n]
        g_end = (g_sums[2 * n:3 * n], g_sums[3 * n:])
        decay = jnp.where(lower, jnp.exp(jnp.where(lower, gc_col - gc_col.T, 0.0)), 0.0)
        kb = kd * beta
        pre.append(dict(qd=qd, kd=kd, vb=vd * beta, kb=kb, gc_col=gc_col, gl_b=gl_b, g_end=g_end, decay=decay,
                        a=jnp.where(strict, mm_nt(kb, kd) * decay, 0.0)))
    tmats = tri_inv(*[p["a"] for p in pre])

    mid = []
    for p, tmat in zip(pre, tmats):
        egc = jnp.exp(p["gc_col"])
        mid.append(dict(u=mm(tmat, p["vb"]), wm=mm(tmat, p["kb"] * egc), qe=p["qd"] * egc,
                        intra=jnp.where(lower, mm_nt(p["qd"], p["kd"]) * p["decay"], 0.0),
                        ke=p["kd"] * jnp.exp(p["gl_b"] - p["gc_col"]), g_end=p["g_end"]))

    ys = []
    for k in range(n_chunks):
        rows = slice(k * cs, (k + 1) * cs)
        new_states, y_heads = [], []
        for pair in range(2):
            m = mid[2 * k + pair]
            hs = (2 * pair, 2 * pair + 1)
            st = (states[hs[0]], states[hs[1]])
            v_new = m["u"] - jnp.concatenate([mm(m["wm"][:cs], st[0]), mm(m["wm"][cs:], st[1])], axis=0)
            o = jnp.concatenate([mm(m["qe"][:cs], st[0]), mm(m["qe"][cs:], st[1])], axis=0) + mm(m["intra"], v_new)
            new_states.append(st[0] * jnp.exp(m["g_end"][0]) + mm_tn(jnp.where(top, m["ke"], 0.0), v_new))
            new_states.append(st[1] * jnp.exp(m["g_end"][1]) + mm_tn(jnp.where(top, 0.0, m["ke"]), v_new))
            od = o * lax.rsqrt(jnp.mean(o * o, axis=-1, keepdims=True) + EPS) * dn_g
            y_heads += [od[:cs] * _silu(zc[rows, 128 * hs[0]:128 * hs[0] + 128]),
                        od[cs:] * _silu(zc[rows, 128 * hs[1]:128 * hs[1] + 128])]
        states = tuple(new_states)
        ys.append(jnp.concatenate(y_heads, axis=1))
    return (jnp.concatenate(ys, axis=0), *states)


DELTA_ROWS = 4 * DN_CHUNK


def delta_fwd(name, qkv, proj, a_log, dt_bias, dn_g):
    t = qkv.shape[0]
    nc = t // DELTA_ROWS

    def body(qkv_ref, ab_ref, zc_ref, al_ref, dt_ref, g_ref, y_ref, ssave_ref, s_ref):
        @pl.when(pl.program_id(0) == 0)
        def _():
            s_ref[...] = jnp.zeros_like(s_ref)

        ssave_ref[0] = s_ref[...]
        st = [s_ref[128 * h:128 * h + 128, :] for h in range(4)]
        y, *ns = _f_delta_step(qkv_ref[...], ab_ref[...], zc_ref[...], *st, al_ref[...], dt_ref[...], g_ref[...])
        y_ref[...] = y
        for h in range(4):
            s_ref[128 * h:128 * h + 128, :] = ns[h]

    return pl.pallas_call(
        body, name=name, grid=(nc,),
        in_specs=[pl.BlockSpec((DELTA_ROWS, 1536), lambda i: (i, 0)), pl.BlockSpec((DELTA_ROWS, 128), lambda i: (i, P_AB // 128)),
                  pl.BlockSpec((DELTA_ROWS, 512), lambda i: (i, P_ZC // 512)),
                  _const_spec((1, 128)), _const_spec((1, 128)), _const_spec((1, 128))],
        out_specs=[pl.BlockSpec((DELTA_ROWS, 512), lambda i: (i, 0)), pl.BlockSpec((1, 512, 128), lambda i: (i, 0, 0))],
        out_shape=[jax.ShapeDtypeStruct((t, 512), F32), jax.ShapeDtypeStruct((nc, 512, 128), F32)],
        scratch_shapes=[pltpu.VMEM((512, 128), F32)],
        compiler_params=_cparams(1),
    )(qkv, proj, proj, a_log, dt_bias, dn_g)


def delta_bwd(name, qkv, proj, ssave, a_log, dt_bias, dn_g, dyc):
    t = qkv.shape[0]
    nc = t // DELTA_ROWS

    def body(qkv_ref, ab_ref, zc_ref, ss_ref, al_ref, dt_ref, g_ref, dy_ref,
             dqkv_ref, dab_ref, dzc_ref, dal_ref, ddt_ref, dg_ref, ds_ref):
        @pl.when(pl.program_id(0) == 0)
        def _():
            ds_ref[...] = jnp.zeros_like(ds_ref)
            dal_ref[...] = jnp.zeros_like(dal_ref)
            ddt_ref[...] = jnp.zeros_like(ddt_ref)
            dg_ref[...] = jnp.zeros_like(dg_ref)

        st = [ss_ref[0, 128 * h:128 * h + 128, :] for h in range(4)]
        _, vjp = jax.vjp(_f_delta_step, qkv_ref[...], ab_ref[...], zc_ref[...], *st, al_ref[...], dt_ref[...], g_ref[...])
        dst = tuple(ds_ref[128 * h:128 * h + 128, :] for h in range(4))
        dqkv, dab, dzc, d0, d1, d2, d3, dal, ddt, dg = vjp((dy_ref[...], *dst))
        dqkv_ref[...] = dqkv
        dab_ref[...] = dab.astype(dab_ref.dtype)
        dzc_ref[...] = dzc.astype(dzc_ref.dtype)
        for h, d in enumerate((d0, d1, d2, d3)):
            ds_ref[128 * h:128 * h + 128, :] = d
        dal_ref[...] += dal
        ddt_ref[...] += ddt
        dg_ref[...] += dg

    rev = lambda cb: (lambda j: (nc - 1 - j, cb))
    return pl.pallas_call(
        body, name=name, grid=(nc,),
        in_specs=[pl.BlockSpec((DELTA_ROWS, 1536), rev(0)), pl.BlockSpec((DELTA_ROWS, 128), rev(P_AB // 128)),
                  pl.BlockSpec((DELTA_ROWS, 512), rev(P_ZC // 512)), pl.BlockSpec((1, 512, 128), lambda j: (nc - 1 - j, 0, 0)),
                  _const_spec((1, 128)), _const_spec((1, 128)), _const_spec((1, 128)),
                  pl.BlockSpec((DELTA_ROWS, 512), rev(0))],
        out_specs=[pl.BlockSpec((DELTA_ROWS, 1536), rev(0)), pl.BlockSpec((DELTA_ROWS, 128), rev(0)),
                   pl.BlockSpec((DELTA_ROWS, 512), rev(0)),
                   _const_spec((1, 128)), _const_spec((1, 128)), _const_spec((1, 128))],
        out_shape=[jax.ShapeDtypeStruct((t, 1536), F32), jax.ShapeDtypeStruct((t, 128), BF16),
                   jax.ShapeDtypeStruct((t, 512), BF16),
                   jax.ShapeDtypeStruct((1, 128), F32), jax.ShapeDtypeStruct((1, 128), F32), jax.ShapeDtypeStruct((1, 128), F32)],
        scratch_shapes=[pltpu.VMEM((512, 128), F32)],
        compiler_params=_cparams(1),
    )(qkv, proj, proj, ssave, a_log, dt_bias, dn_g, dyc)


def loss_head(name, y, target, tm):
    t, d = y.shape

    def body(y_ref, t_ref, dy_ref, l_ref):
        err = y_ref[...] - t_ref[...]
        dy_ref[...] = err * (1.0 / d)
        part = 0.5 * jnp.sum(jnp.sum(err * err, axis=-1, keepdims=True) * (1.0 / d), axis=0, keepdims=True)

        @pl.when(pl.program_id(0) == 0)
        def _():
            l_ref[...] = part

        @pl.when(pl.program_id(0) > 0)
        def _():
            l_ref[...] += part

    return pl.pallas_call(
        body, name=name, grid=(t // tm,),
        in_specs=[_row_spec(tm, d, 0), _row_spec(tm, d, 0)],
        out_specs=[_row_spec(tm, d, 0), _const_spec((1, 1))],
        out_shape=[jax.ShapeDtypeStruct((t, d), F32), jax.ShapeDtypeStruct((1, 1), F32)],
        compiler_params=_cparams(1),
    )(y, target)


TM = 512
TM_MERGE = 256
TM_IN = 1024
TN_IN = 1152


def _lane_pad(v, n=128):
    return jnp.pad(v.astype(F32), (0, n - v.shape[0]))[None, :]


def f_norm_mod_res(x, g, scale, shift):
    return f_norm_mod(x, g, scale, shift), x


def prep_layer(w):
    p = dict(w)
    p["wp"] = _pad_w_in_from_shards(w["w_in"])
    p["wpt"] = p["wp"].T
    p["wpa"] = _perm_heads_rows(w["w_proj_a"])
    p["dw32"] = jnp.pad(w["dw_w"], ((0, 32 - CONV_K), (0, 0)))
    p["sconv8"] = jnp.pad(w["sconv_w"], ((0, 8 - DN_CONV_K), (0, 0)))
    p["qg"] = jnp.tile(w["q_norm_g"], 2)[None, :]
    p["kg"] = jnp.tile(w["k_norm_g"], 2)[None, :]
    p["sinks128"] = _lane_pad(w["sinks"])
    p["al"] = _lane_pad(w["a_log"])
    p["dtb"] = _lane_pad(w["dt_bias"])
    p["dng"] = w["dn_norm_g"][None, :]
    return p


def layer_fwd(tag, x, c8, p):
    mod = ada_fwd(f"ada_fwd{tag}", c8, p["w_ada"], p["b_ada"][None, :])[0:1]
    d = D_MODEL
    shift, scale, gate = mod[:, :d], mod[:, d:2 * d], mod[:, 2 * d:]
    g = p["norm_g"][None, :]
    (h,) = rowwise_fwd(f"norm_fwd{tag}", f_norm_mod, [(x, d, 0)], [g, scale, shift], [(d, BF16)], TM)
    proj = matmul_nn(f"inproj_fwd{tag}", h, p["wp"], F32, TM_IN, TN_IN, d)
    ya = attn_fwd(f"attn_fwd{tag}", proj, p["qg"], p["kg"], p["sinks128"])
    ub = glu_conv_fwd(f"glu_conv_fwd{tag}", proj, p["dw32"], p["dw_b"][None, :])
    conf_consts = [p["ln_g"][None, :], p["ln_b"][None, :], p["pw2_w"], p["pw2_b"][None, :]]
    (yb,) = rowwise_fwd(f"conf_fwd{tag}", f_conf_tail, [(ub, 512, 0), (proj, 512, P_ZB // 512)], conf_consts, [(512, F32)], TM)
    qkv = sconv_fwd(f"sconv_fwd{tag}", proj, p["sconv8"])
    yc, ssave = delta_fwd(f"delta_fwd{tag}", qkv, proj, p["al"], p["dtb"], p["dng"])
    merge_consts = [gate, p["wpa"], p["w_proj_b"], p["w_proj_c"], p["w_out"]]
    merge_rows = [(ya, 512, 0), (yb, 512, 0), (yc, 512, 0), (proj, 3 * d, P_MG // (3 * d)), (x, d, 0)]
    (xn,) = rowwise_fwd(f"merge_fwd{tag}", f_merge, merge_rows, merge_consts, [(d, F32)], TM_MERGE)
    saved = dict(x=x, h=h, proj=proj, ub=ub, qkv=qkv, ssave=ssave, norm_consts=[g, scale, shift],
                 conf_consts=conf_consts, merge_consts=merge_consts, merge_rows=merge_rows)
    return xn, saved


def layer_bwd(tag, dxn, c8, p, s):
    d = D_MODEL
    proj = s["proj"]
    dya, dyb, dyc, dmg, dgate, dwpa, dwpb, dwpc, dwout = rowwise_bwd(
        f"merge_bwd{tag}", f_merge, s["merge_rows"], s["merge_consts"], [(dxn, d, 0)], [F32, F32, F32, BF16, None], TM_MERGE)
    dqz, dkv, dqg, dkg, dsinks = attn_bwd(f"attn_bwd{tag}", proj, p["qg"], p["kg"], p["sinks128"], dya)
    dub, dzb, dln_g, dln_b, dpw2_w, dpw2_b = rowwise_bwd(
        f"conf_bwd{tag}", f_conf_tail, [(s["ub"], 512, 0), (proj, 512, P_ZB // 512)], s["conf_consts"], [(dyb, 512, 0)],
        [F32, BF16], TM)
    dglu, ddw32, ddw_b = glu_conv_bwd(f"glu_conv_bwd{tag}", proj, p["dw32"], dub)
    dqkv, dab, dzc, dal, ddtb, ddng = delta_bwd(f"delta_bwd{tag}", s["qkv"], proj, s["ssave"], p["al"], p["dtb"], p["dng"], dyc)
    dqkv_pre, dsconv8 = sconv_bwd(f"sconv_bwd{tag}", proj, p["sconv8"], dqkv)
    dproj = jnp.concatenate([dqz, dglu, dzb, dzc, dmg, dqkv_pre, dkv, dab], axis=1)
    dh = matmul_nn(f"inproj_bwd_dh{tag}", dproj, p["wpt"], F32, TM_IN, d, TN_IN)
    dwp = matmul_nn(f"inproj_bwd_dw{tag}", s["h"].T, dproj, F32, d, TN_IN, TM)
    dx, dnorm_g, dscale, dshift = rowwise_bwd(
        f"norm_bwd{tag}", f_norm_mod_res, [(s["x"], d, 0)], s["norm_consts"], [(dh, d, 0), (dxn, d, 0)], [F32], TM)
    dmod = jnp.concatenate([dshift, dscale, dgate], axis=1)
    dw_ada = ada_bwd(f"ada_bwd{tag}", c8, jnp.pad(dmod, ((0, 7), (0, 0))))
    grads = dict(
        w_ada=dw_ada, b_ada=dmod[0], norm_g=dnorm_g[0], w_in=_unpad_w_in_to_shards(dwp),
        q_norm_g=dqg[0, :64] + dqg[0, 64:], k_norm_g=dkg[0, :64] + dkg[0, 64:], sinks=dsinks[0, :ATT_HEADS],
        dw_w=ddw32[:CONV_K], dw_b=ddw_b[0], ln_g=dln_g[0], ln_b=dln_b[0], pw2_w=dpw2_w, pw2_b=dpw2_b[0],
        sconv_w=dsconv8[:DN_CONV_K], a_log=dal[0, :DN_HEADS], dt_bias=ddtb[0, :DN_HEADS], dn_norm_g=ddng[0],
        w_proj_a=_unperm_heads_rows(dwpa), w_proj_b=dwpb, w_proj_c=dwpc, w_out=dwout)
    return dx, grads


SHARDED = {"w_ada": 2, "w_in": 2, "dw_w": 2, "pw2_w": 1, "sconv_w": 2, "w_proj_a": 2, "w_proj_b": 2, "w_proj_c": 2,
           "w_out": 1}
GATHER_F32 = ("dw_w", "sconv_w")
REDUCE_BIG = tuple(n for n in SHARDED if n not in GATHER_F32)
SMALL = ("b_ada", "norm_g", "q_norm_g", "k_norm_g", "sinks", "dw_b", "ln_g", "ln_b", "pw2_b", "a_log", "dt_bias",
         "dn_norm_g")
SMALL_ROWS = 104
SMALL_GRAD_ROWS = 448
W_IN_SHARD = D_IN // N_CHIPS
SUM_TILE = 256


def _shard_cols(shards, start, n):
    parts = []
    while n > 0:
        k, o = divmod(start, W_IN_SHARD)
        m = min(n, W_IN_SHARD - o)
        parts.append(shards[k][:, o:o + m])
        start, n = start + m, n - m
    return parts


def _pad_w_in_from_shards(shards):
    parts = []
    for s, n in _in_pieces():
        parts += _shard_cols(shards, s, n)
    parts.append(jnp.zeros((shards.shape[1], P_TOTAL - D_IN), shards.dtype))
    return jnp.concatenate(parts, axis=1)


def _unpad_w_in_to_shards(wp):
    pieces = _in_pieces()
    starts = np.cumsum([0] + [n for _, n in pieces])[:-1]
    order = sorted(range(len(pieces)), key=lambda i: pieces[i][0])
    shards = []
    for k in range(N_CHIPS):
        lo, hi = k * W_IN_SHARD, (k + 1) * W_IN_SHARD
        parts = []
        for i in order:
            s, n = pieces[i]
            a, b = max(s, lo), min(s + n, hi)
            if a < b:
                parts.append(wp[:, int(starts[i]) + a - s:int(starts[i]) + b - s])
        shards.append(jnp.concatenate(parts, axis=1))
    return jnp.stack(shards)


def _join_layer(v, axis):
    if axis == 2:
        return jnp.transpose(v, (1, 0, 2)).reshape(v.shape[1], N_CHIPS * v.shape[2])
    return v.reshape(N_CHIPS * v.shape[1], v.shape[2])


def _split_layer(v, axis):
    a, b = v.shape
    if axis == 2:
        return jnp.transpose(v.reshape(a, N_CHIPS, b // N_CHIPS), (1, 0, 2))
    return v.reshape(N_CHIPS, a // N_CHIPS, b)


def pack_small(vals, names, rows):
    flat = jnp.concatenate([vals[n].astype(F32).reshape(-1) for n in names])
    return jnp.pad(flat, (0, rows * 128 - flat.shape[0])).reshape(rows, 128)


def unpack_small(packed, names, shapes):
    flat = packed.reshape(-1)
    out, off = {}, 0
    for n in names:
        k = int(np.prod(shapes[n]))
        out[n] = flat[off:off + k].reshape(shapes[n])
        off += k
    return out


ANY = pl.BlockSpec(memory_space=pl.ANY)


def _place():
    x, y, c = lax.axis_index("x"), lax.axis_index("y"), lax.axis_index("c")
    chips = [(1 - x, y), (x, 1 - y), (1 - x, 1 - y)]
    return x, y, c, chips


def _remote(src, dst, send_sem, recv_sem, to):
    return pltpu.make_async_remote_copy(src_ref=src, dst_ref=dst, send_sem=send_sem, recv_sem=recv_sem, device_id=to,
                                        device_id_type=MESH)


def weights_allgather(slots):
    n = len(slots)

    def body(*refs):
        out = refs[n:2 * n]
        send_sems, recv_sems = refs[2 * n:]
        x, y, c, chips = _place()
        me, sibling, my_slot = (x, y, c), (x, y, 1 - c), 2 * x + y
        sends = []
        for j, chip in enumerate(chips):
            for t in range(n):
                mine = out[t].at[my_slot, c]
                sends.append(_remote(mine, mine, send_sems.at[t, j], recv_sems.at[t, j], (*chip, c)))
                sends[-1].start()
        for j, chip in enumerate(chips):
            for t in range(n):
                land = out[t].at[2 * chip[0] + chip[1], c]
                _remote(land, land, send_sems.at[t, j], recv_sems.at[t, j], me).wait_recv()
                sends.append(_remote(land, land, send_sems.at[t, 3 + j], recv_sems.at[t, 3 + j], sibling))
                sends[-1].start()
        for j, chip in enumerate(chips):
            for t in range(n):
                land = out[t].at[2 * chip[0] + chip[1], 1 - c]
                _remote(land, land, send_sems.at[t, 3 + j], recv_sems.at[t, 3 + j], me).wait_recv()
        for cp in sends:
            cp.wait_send()

    return pl.pallas_call(
        body, name="weights_allgather", out_shape=[jax.ShapeDtypeStruct(s.shape, s.dtype) for s in slots],
        in_specs=[ANY] * n, out_specs=[ANY] * n, input_output_aliases={t: t for t in range(n)},
        scratch_shapes=[pltpu.SemaphoreType.DMA((n, 6)), pltpu.SemaphoreType.DMA((n, 6))],
    )(*slots)


def grads_pair_exchange(gs):
    n = len(gs)

    def body(*refs):
        g, recv = refs[:n], refs[n:2 * n]
        send_sems, recv_sems = refs[2 * n:]
        x, y, c, _ = _place()
        cps = [_remote(g[t].at[:, 1 - c], recv[t], send_sems.at[t], recv_sems.at[t], (x, y, 1 - c)) for t in range(n)]
        for cp in cps:
            cp.start()
        for cp in cps:
            cp.wait()

    return pl.pallas_call(
        body, name="grads_pair_exchange",
        out_shape=[jax.ShapeDtypeStruct((N_CHIPS,) + g.shape[2:], g.dtype) for g in gs],
        in_specs=[ANY] * n, out_specs=[ANY] * n,
        scratch_shapes=[pltpu.SemaphoreType.DMA((n,)), pltpu.SemaphoreType.DMA((n,))],
    )(*gs)


def grads_pair_sum(name, g, recv):
    _, a, b = recv.shape
    ta = min(a, SUM_TILE)

    def body(a_ref, b_ref, o_ref):
        o_ref[...] = (a_ref[...] + b_ref[...]).astype(o_ref.dtype)

    return pl.pallas_call(
        body, name=name, grid=(N_CHIPS, a // ta),
        in_specs=[pl.BlockSpec((None, None, ta, b), lambda s, i: (s, lax.axis_index("c"), i, 0)),
                  pl.BlockSpec((None, ta, b), lambda s, i: (s, i, 0))],
        out_specs=pl.BlockSpec((None, ta, b), lambda s, i: (s, i, 0)),
        out_shape=jax.ShapeDtypeStruct(recv.shape, BF16),
        compiler_params=_cparams(2),
    )(g, recv)


def grads_chip_exchange(ps):
    n = len(ps)

    def body(*refs):
        p, recv = refs[:n], refs[n:2 * n]
        send_sems, recv_sems = refs[2 * n:]
        x, y, c, chips = _place()
        cps = [_remote(p[t].at[2 * chip[0] + chip[1]], recv[t].at[j], send_sems.at[t, j], recv_sems.at[t, j], (*chip, c))
               for j, chip in enumerate(chips) for t in range(n)]
        for cp in cps:
            cp.start()
        for cp in cps:
            cp.wait()

    return pl.pallas_call(
        body, name="grads_chip_exchange", out_shape=[jax.ShapeDtypeStruct((3,) + p.shape[1:], p.dtype) for p in ps],
        in_specs=[ANY] * n, out_specs=[ANY] * n,
        scratch_shapes=[pltpu.SemaphoreType.DMA((n, 3)), pltpu.SemaphoreType.DMA((n, 3))],
    )(*ps)


def grads_chip_sum(name, g, recv, recv2):
    _, a, b = recv.shape
    ta = min(a, SUM_TILE)
    my_slot = lambda: 2 * lax.axis_index("x") + lax.axis_index("y")

    def body(g_ref, r_ref, r2_ref, o_ref):
        own = g_ref[...] + r_ref[...]
        o_ref[...] = ((own + r2_ref[0].astype(F32)) + r2_ref[1].astype(F32)) + r2_ref[2].astype(F32)

    return pl.pallas_call(
        body, name=name, grid=(a // ta,),
        in_specs=[pl.BlockSpec((None, None, ta, b), lambda i: (my_slot(), lax.axis_index("c"), i, 0)),
                  pl.BlockSpec((None, ta, b), lambda i: (my_slot(), i, 0)),
                  pl.BlockSpec((3, ta, b), lambda i: (0, i, 0))],
        out_specs=pl.BlockSpec((None, ta, b), lambda i: (lax.axis_index("c"), i, 0)),
        out_shape=jax.ShapeDtypeStruct((DEPTH, a, b), F32),
        compiler_params=_cparams(1),
    )(g, recv, recv2)


def grads_pair_gather(reds):
    n = len(reds)

    def body(*refs):
        buf = refs[n:2 * n]
        send_sems, recv_sems = refs[2 * n:]
        x, y, c, _ = _place()
        sibling = (x, y, 1 - c)
        cps = [_remote(buf[t].at[c], buf[t].at[c], send_sems.at[t], recv_sems.at[t], sibling) for t in range(n)]
        for cp in cps:
            cp.start()
        for t in range(n):
            _remote(buf[t].at[c], buf[t].at[1 - c], send_sems.at[t], recv_sems.at[t], sibling).wait_recv()
        for cp in cps:
            cp.wait_send()

    return pl.pallas_call(
        body, name="grads_pair_gather", out_shape=[jax.ShapeDtypeStruct(r.shape, r.dtype) for r in reds],
        in_specs=[ANY] * n, out_specs=[ANY] * n, input_output_aliases={t: t for t in range(n)},
        scratch_shapes=[pltpu.SemaphoreType.DMA((n,)), pltpu.SemaphoreType.DMA((n,))],
    )(*reds)


def small_allreduce(v):
    m, n = v.shape

    def body(x_ref, sum_ref, all_ref, send_sems, recv_sems, local_sem):
        x, y, c, chips = _place()
        me, sibling = (x, y, c), (x, y, 1 - c)

        def rows(px, py, pc):
            return all_ref.at[pl.ds((4 * px + 2 * py + pc) * m, m), :]

        def copy(k, block, to, src=None):
            return pltpu.make_async_remote_copy(src_ref=rows(*block) if src is None else src, dst_ref=rows(*block),
                                                send_sem=send_sems.at[k], recv_sem=recv_sems.at[k],
                                                device_id=to, device_id_type=MESH)

        mine = pltpu.make_async_copy(x_ref, rows(*me), local_sem)
        mine.start()
        first = [copy(0, me, sibling, src=x_ref)]
        first += [copy(1 + j, me, (*chip, c), src=x_ref) for j, chip in enumerate(chips)]
        for cp in first:
            cp.start()
        passed = [copy(4 + j, (*chip, c), sibling) for j, chip in enumerate(chips)]
        for j, chip in enumerate(chips):
            copy(1 + j, (*chip, c), me).wait_recv()
            passed[j].start()
        copy(0, sibling, me).wait_recv()
        for j, chip in enumerate(chips):
            copy(4 + j, (*chip, 1 - c), me).wait_recv()
        for cp in first + passed:
            cp.wait_send()
        mine.wait()
        acc = all_ref[0:m, :]
        for dev in range(1, 8):
            acc = acc + all_ref[dev * m:(dev + 1) * m, :]
        sum_ref[...] = acc

    vm = pl.BlockSpec(memory_space=pltpu.VMEM)
    return pl.pallas_call(
        body, name="small_allreduce",
        out_shape=[jax.ShapeDtypeStruct((m, n), F32), jax.ShapeDtypeStruct((8 * m, n), F32)],
        in_specs=[vm], out_specs=[vm, vm],
        scratch_shapes=[pltpu.SemaphoreType.DMA((7,)), pltpu.SemaphoreType.DMA((7,)), pltpu.SemaphoreType.DMA],
    )(v)[0]


def reduce_scatter_grads(names, gs):
    recv = grads_pair_exchange(gs)
    parts = [grads_pair_sum("grads_pair_sum_" + n, g, r) for n, g, r in zip(names, gs, recv)]
    recv2 = grads_chip_exchange(parts)
    reds = [grads_chip_sum("grads_chip_sum_" + n, g, r, r2) for n, g, r, r2 in zip(names, gs, recv, recv2)]
    return grads_pair_gather(reds)


def adamw(name, w, g, m, v, tr, tc=None):
    r, cols = w.shape
    tc = cols if tc is None else tc

    def body(w_ref, g_ref, m_ref, v_ref, d_ref, nm_ref, nv_ref):
        gv = g_ref[...]
        nm = ADAM_B1 * m_ref[...] + (1.0 - ADAM_B1) * gv
        nv = ADAM_B2 * v_ref[...] + (1.0 - ADAM_B2) * (gv * gv)
        m_hat = nm / (1.0 - ADAM_B1 ** ADAM_STEP)
        v_hat = nv / (1.0 - ADAM_B2 ** ADAM_STEP)
        d_ref[...] = -ADAM_LR * (m_hat / (jnp.sqrt(v_hat) + ADAM_EPS) + ADAM_WD * w_ref[...])
        nm_ref[...] = nm
        nv_ref[...] = nv

    spec = pl.BlockSpec((tr, tc), lambda i, j: (i, j))
    return pl.pallas_call(
        body, name=name, grid=(r // tr, cols // tc), in_specs=[spec] * 4, out_specs=[spec] * 3,
        out_shape=[jax.ShapeDtypeStruct((r, cols), F32)] * 3, compiler_params=_cparams(2),
    )(w, g, m, v)


ADAM_ROWS = {"w_ada": 512, "dw_w": 62, "pw2_w": 256, "sconv_w": 8, "w_proj_a": 512, "w_proj_b": 512, "w_proj_c": 512,
             "w_out": 256}
ADAM_W_IN_COLS = 256

WEIGHT_NAMES = ("w_ada", "b_ada", "norm_g", "w_in", "q_norm_g", "k_norm_g", "sinks", "dw_w", "dw_b", "ln_g", "ln_b",
                "pw2_w", "pw2_b", "sconv_w", "a_log", "dt_bias", "dn_norm_g", "w_proj_a", "w_proj_b", "w_proj_c", "w_out")


def kernel(x, c, w_ada, b_ada, norm_g, w_in, q_norm_g, k_norm_g, sinks, dw_w, dw_b, ln_g, ln_b, pw2_w, pw2_b, sconv_w, a_log, dt_bias, dn_norm_g, w_proj_a, w_proj_b, w_proj_c, w_out, loss_target, m_w_ada, m_b_ada, m_norm_g, m_w_in, m_q_norm_g, m_k_norm_g, m_sinks, m_dw_w, m_dw_b, m_ln_g, m_ln_b, m_pw2_w, m_pw2_b, m_sconv_w, m_a_log, m_dt_bias, m_dn_norm_g, m_w_proj_a, m_w_proj_b, m_w_proj_c, m_w_out, v_w_ada, v_b_ada, v_norm_g, v_w_in, v_q_norm_g, v_k_norm_g, v_sinks, v_dw_w, v_dw_b, v_ln_g, v_ln_b, v_pw2_w, v_pw2_b, v_sconv_w, v_a_log, v_dt_bias, v_dn_norm_g, v_w_proj_a, v_w_proj_b, v_w_proj_c, v_w_out):
    args = dict(locals())
    w = {n: args[n] for n in WEIGHT_NAMES}
    mom = {n: args["m_" + n] for n in WEIGHT_NAMES}
    var = {n: args["v_" + n] for n in WEIGHT_NAMES}

    chip = 2 * lax.axis_index("x") + lax.axis_index("y")
    slots = []
    for n in SHARDED:
        own = w[n] if n in GATHER_F32 else w[n].astype(BF16)
        slots.append(lax.dynamic_update_slice(lax.empty((N_CHIPS,) + own.shape, own.dtype), own[None], (chip, 0, 0, 0)))
    gathered = dict(zip(SHARDED, weights_allgather(slots)))
    layers = []
    for l in range(DEPTH):
        lw = {n: w[n][l] for n in SMALL}
        for n, axis in SHARDED.items():
            lw[n] = gathered[n][:, l] if n == "w_in" else _join_layer(gathered[n][:, l], axis)
        layers.append(prep_layer(lw))

    c8 = jnp.tile(c, (8, 1))
    act, saved = x[0], []
    for l in range(DEPTH):
        act, s = layer_fwd(str(l), act, c8, layers[l])
        saved.append(s)
    dact, loss_part = loss_head("loss_head", act, loss_target[0], TM)
    loss = lax.psum(loss_part[0, 0], ("x", "y", "c"))
    layer_grads = [None] * DEPTH
    for l in reversed(range(DEPTH)):
        dact, layer_grads[l] = layer_bwd(str(l), dact, c8, layers[l], saved[l])

    by_chip = [jnp.stack([layer_grads[l][n] if n == "w_in" else _split_layer(layer_grads[l][n], SHARDED[n])
                          for l in range(DEPTH)], axis=1) for n in REDUCE_BIG]
    final_grads = dict(zip(REDUCE_BIG, reduce_scatter_grads(REDUCE_BIG, by_chip)))
    small_names = SMALL + GATHER_F32
    small_shapes = {n: (DEPTH,) + layer_grads[0][n].shape for n in small_names}
    small_full = {n: jnp.stack([layer_grads[l][n] for l in range(DEPTH)]) for n in small_names}
    small_sum = unpack_small(small_allreduce(pack_small(small_full, small_names, SMALL_GRAD_ROWS)), small_names, small_shapes)
    for n in GATHER_F32:
        width = w[n].shape[2]
        final_grads[n] = lax.dynamic_slice_in_dim(small_sum[n], chip * width, width, axis=2)
    final_grads.update({n: small_sum[n] for n in SMALL})
    small_grads = pack_small(final_grads, SMALL, SMALL_ROWS)

    delta, new_m, new_v = {}, {}, {}
    for n in SHARDED:
        shp = w[n].shape
        if n == "w_in":
            two_d = lambda a: jnp.transpose(a, (2, 0, 1)).reshape(shp[2], shp[0] * shp[1])
            back = lambda a: jnp.transpose(a.reshape(shp[2], shp[0], shp[1]), (1, 2, 0))
            g2 = two_d(final_grads[n])
            final_grads[n] = back(g2)
            d, nm, nv = adamw("adamw_" + n, two_d(w[n]), g2, two_d(mom[n]), two_d(var[n]), shp[2], ADAM_W_IN_COLS)
        else:
            two_d = lambda a, shp=shp: a.reshape(shp[0] * shp[1], shp[2])
            back = lambda a, shp=shp: a.reshape(shp)
            d, nm, nv = adamw("adamw_" + n, two_d(w[n]), two_d(final_grads[n]), two_d(mom[n]), two_d(var[n]), ADAM_ROWS[n])
        delta[n], new_m[n], new_v[n] = back(d), back(nm), back(nv)
    d, nm, nv = adamw("adamw_small", pack_small(w, SMALL, SMALL_ROWS), small_grads, pack_small(mom, SMALL, SMALL_ROWS),
                      pack_small(var, SMALL, SMALL_ROWS), SMALL_ROWS)
    delta.update(unpack_small(d, SMALL, small_shapes))
    new_m.update(unpack_small(nm, SMALL, small_shapes))
    new_v.update(unpack_small(nv, SMALL, small_shapes))

    return (loss, dact[None], *[final_grads[n] for n in WEIGHT_NAMES], *[delta[n] for n in WEIGHT_NAMES],
            *[new_m[n] for n in WEIGHT_NAMES], *[new_v[n] for n in WEIGHT_NAMES])
```

```python
import functools

import numpy as np
import jax
import jax.numpy as jnp
from jax import lax
from jax.experimental import pallas as pl
from jax.experimental.pallas import tpu as pltpu

F32 = jnp.float32
BF16 = jnp.bfloat16
MESH = pl.DeviceIdType.MESH

D_MODEL = 1024
DEPTH = 2
ATT_HEADS = 8
ATT_HEAD_DIM = 64
WINDOW = 128
CONV_K = 31
DN_HEADS = 4
DN_CONV_K = 4
DN_CHUNK = 64
EPS = 1e-6
NEG_INF = -1e30
N_CHIPS = 4
D_IN = 7944

ADAM_LR = 0.001
ADAM_B1 = 0.9
ADAM_B2 = 0.999
ADAM_EPS = 1e-08
ADAM_WD = 0.01
ADAM_STEP = 10

VMEM_LIMIT = 56 * 1024 * 1024

P_QA, P_ZA, P_GLU, P_ZB, P_ZC, P_MG, P_QKV, P_KA, P_VA, P_AB, P_TOTAL = (
    0, 512, 1024, 2048, 2560, 3072, 6144, 7680, 7808, 7936, 8064)
HEAD_ORDER = (0, 4, 1, 5, 2, 6, 3, 7)


def _in_pieces():
    p = [(0 + 64 * h, 64) for h in HEAD_ORDER]
    p += [(768 + 64 * h, 64) for h in HEAD_ORDER]
    for g in range(4):
        p += [(1280 + 128 * g, 128), (1792 + 128 * g, 128)]
    p += [(2304, 512), (4360, 512), (4872, 3072), (2816, 1536), (512, 128), (640, 128), (4352, 8)]
    return p


def _perm_heads_rows(w):
    return jnp.concatenate([w[64 * h:64 * h + 64] for h in HEAD_ORDER], axis=0)


def _unperm_heads_rows(w):
    inv = [HEAD_ORDER.index(h) for h in range(8)]
    return jnp.concatenate([w[64 * s:64 * s + 64] for s in inv], axis=0)


def _split_bf16(a, terms):
    out, rest = [], a.astype(F32)
    for _ in range(terms - 1):
        out.append(rest.astype(BF16))
        rest = rest - out[-1].astype(F32)
    return out + [rest.astype(BF16)]


def _dot(a, b, dims, exact):
    d = lambda p, q: lax.dot_general(p, q, (dims, ((), ())), preferred_element_type=F32)
    if exact:
        (ah, al), (bh, bl) = _split_bf16(a, 2), _split_bf16(b, 2)
        return d(ah, bh) + (d(ah, bl) + d(al, bh))
    return d(a.astype(BF16), b.astype(BF16))


def _make_mm(exact):
    @jax.custom_vjp
    def nn(a, b):
        return _dot(a, b, ((1,), (0,)), exact)

    @jax.custom_vjp
    def nt(a, b):
        return _dot(a, b, ((1,), (1,)), exact)

    @jax.custom_vjp
    def tn(a, b):
        return _dot(a, b, ((0,), (0,)), exact)

    nn.defvjp(lambda a, b: (nn(a, b), (a, b)),
              lambda r, g: (nt(g, r[1]).astype(r[0].dtype), tn(r[0], g).astype(r[1].dtype)))
    nt.defvjp(lambda a, b: (nt(a, b), (a, b)),
              lambda r, g: (nn(g, r[1]).astype(r[0].dtype), tn(g, r[0]).astype(r[1].dtype)))
    tn.defvjp(lambda a, b: (tn(a, b), (a, b)),
              lambda r, g: (nt(r[1], g).astype(r[0].dtype), nn(r[0], g).astype(r[1].dtype)))
    return nn, nt, tn


mm, mm_nt, mm_tn = _make_mm(False)
xmm, xmm_nt, xmm_tn = _make_mm(True)


@jax.custom_vjp
def sel_mm(m, g):
    mb = m.astype(BF16)
    parts = [jnp.dot(mb, p, preferred_element_type=F32) for p in _split_bf16(g, 3)]
    return parts[0] + (parts[1] + parts[2])


def _sel_mm_bwd(m, dy):
    mb = m.astype(BF16)
    parts = [lax.dot_general(mb, p, (((0,), (0,)), ((), ())), preferred_element_type=F32) for p in _split_bf16(dy, 3)]
    return jnp.zeros_like(m), parts[0] + (parts[1] + parts[2])


sel_mm.defvjp(lambda m, g: (sel_mm(m, g), m), _sel_mm_bwd)


@jax.custom_vjp
def tri_inv(*mats):
    n = mats[0].shape[0]
    eye = jnp.where(lax.broadcasted_iota(jnp.int32, (n, n), 0) == lax.broadcasted_iota(jnp.int32, (n, n), 1), 1.0, 0.0)
    ts = [eye - a for a in mats]
    pws = list(mats)
    for _ in range(5):
        pws = [xmm(pw, pw) for pw in pws]
        ts = [t + xmm(t, pw) for t, pw in zip(ts, pws)]
    return tuple(ts)


def _tri_inv_bwd(ts, dts):
    inner = [xmm_nt(dt, t) for t, dt in zip(ts, dts)]
    return tuple(-xmm_tn(t, m) for t, m in zip(ts, inner))


tri_inv.defvjp(lambda *mats: (tri_inv(*mats),) * 2, _tri_inv_bwd)


def _sigmoid(x):
    return 1.0 / (1.0 + jnp.exp(-x))


def _silu(x):
    return x * _sigmoid(x)


def _softplus(x):
    return jnp.maximum(x, 0.0) + jnp.log(1.0 + jnp.exp(-jnp.abs(x)))


def _cparams(n_grid):
    return pltpu.CompilerParams(dimension_semantics=("arbitrary",) * n_grid, vmem_limit_bytes=VMEM_LIMIT)


def _row_spec(tm, width, colblk):
    return pl.BlockSpec((tm, width), lambda i, cb=colblk: (i, cb))


def _const_spec(shape):
    nd = len(shape)
    return pl.BlockSpec(tuple(shape), lambda i, nd=nd: (0,) * nd)


def rowwise_fwd(name, f, rows, consts, outs, tm):
    n_r, n_c = len(rows), len(consts)
    t = rows[0][0].shape[0]

    def body(*refs):
        vals = [r[...] for r in refs[:n_r + n_c]]
        res = f(*vals)
        if not isinstance(res, (tuple, list)):
            res = (res,)
        for o_ref, v in zip(refs[n_r + n_c:], res):
            o_ref[...] = v.astype(o_ref.dtype)

    return pl.pallas_call(
        body, name=name, grid=(t // tm,),
        in_specs=[_row_spec(tm, w, cb) for _, w, cb in rows] + [_const_spec(c.shape) for c in consts],
        out_specs=[_row_spec(tm, w, 0) for w, _ in outs],
        out_shape=[jax.ShapeDtypeStruct((t, w), dt) for w, dt in outs],
        compiler_params=_cparams(1),
    )(*[a for a, _, _ in rows], *consts)


def rowwise_bwd(name, f, rows, consts, cts, row_grad_dtypes, tm):
    n_r, n_c, n_ct = len(rows), len(consts), len(cts)
    t = rows[0][0].shape[0]
    keep = [k for k, dt in enumerate(row_grad_dtypes) if dt is not None]

    def body(*refs):
        ins = [r[...].astype(F32) for r in refs[:n_r + n_c]]
        g_out = [r[...].astype(F32) for r in refs[n_r + n_c:n_r + n_c + n_ct]]
        out_refs = refs[n_r + n_c + n_ct:]

        def fw(*a):
            res = f(*a)
            return tuple(res) if isinstance(res, (tuple, list)) else (res,)

        _, vjp = jax.vjp(fw, *ins)
        grads = vjp(tuple(g_out))
        for o_ref, k in zip(out_refs[:len(keep)], keep):
            o_ref[...] = grads[k].astype(o_ref.dtype)
        first = pl.program_id(0) == 0
        for o_ref, g in zip(out_refs[len(keep):], grads[n_r:]):
            @pl.when(first)
            def _(o_ref=o_ref, g=g):
                o_ref[...] = g

            @pl.when(jnp.logical_not(first))
            def _(o_ref=o_ref, g=g):
                o_ref[...] += g

    return pl.pallas_call(
        body, name=name, grid=(t // tm,),
        in_specs=[_row_spec(tm, w, cb) for _, w, cb in rows] + [_const_spec(c.shape) for c in consts]
        + [_row_spec(tm, w, cb) for _, w, cb in cts],
        out_specs=[_row_spec(tm, rows[k][1], 0) for k in keep] + [_const_spec(c.shape) for c in consts],
        out_shape=[jax.ShapeDtypeStruct((t, rows[k][1]), row_grad_dtypes[k]) for k in keep]
        + [jax.ShapeDtypeStruct(c.shape, F32) for c in consts],
        compiler_params=_cparams(1),
    )(*[a for a, _, _ in rows], *consts, *[a for a, _, _ in cts])


def f_norm_mod(x, g, scale, shift):
    y = x * lax.rsqrt(jnp.mean(x * x, axis=-1, keepdims=True) + EPS) * g
    return y * (1.0 + scale) + shift


def f_conf_tail(u, zb, ln_g, ln_b, pw2_w, pw2_b):
    mu = jnp.mean(u, axis=-1, keepdims=True)
    xc = u - mu
    var = jnp.mean(xc * xc, axis=-1, keepdims=True)
    y = _silu(xc * lax.rsqrt(var + EPS) * ln_g + ln_b)
    return (mm(y, pw2_w) + pw2_b) * _silu(zb)


def f_merge(ya, yb, yc, mg, x, gate, wpa, wpb, wpc, wout):
    d = D_MODEL
    merged = (_sigmoid(mg[:, :d]) * mm(ya, wpa) + _sigmoid(mg[:, d:2 * d]) * mm(yb, wpb)
              + _sigmoid(mg[:, 2 * d:]) * mm(yc, wpc))
    return x + gate * mm(merged, wout)


def matmul_nn(name, a, b, out_dtype, tm, tn, tk):
    m, k = a.shape
    n = b.shape[1]
    nk = k // tk

    def body(a_ref, b_ref, o_ref, *acc):
        part = jnp.dot(a_ref[...].astype(BF16), b_ref[...].astype(BF16), preferred_element_type=F32)
        if nk == 1:
            o_ref[...] = part.astype(o_ref.dtype)
            return
        kk = pl.program_id(2)
        acc_ref = acc[0]

        @pl.when(kk == 0)
        def _():
            acc_ref[...] = part

        @pl.when(kk > 0)
        def _():
            acc_ref[...] += part

        @pl.when(kk == nk - 1)
        def _():
            o_ref[...] = acc_ref[...].astype(o_ref.dtype)

    return pl.pallas_call(
        body, name=name, grid=(m // tm, n // tn, nk),
        in_specs=[pl.BlockSpec((tm, tk), lambda i, j, kk: (i, kk)), pl.BlockSpec((tk, tn), lambda i, j, kk: (kk, j))],
        out_specs=pl.BlockSpec((tm, tn), lambda i, j, kk: (i, j)),
        out_shape=jax.ShapeDtypeStruct((m, n), out_dtype),
        scratch_shapes=[] if nk == 1 else [pltpu.VMEM((tm, tn), F32)],
        compiler_params=_cparams(3),
    )(a, b)


def ada_fwd(c8, w_shard, b_ada):
    n_cols = w_shard.shape[2]
    masks = [(m >> 2 & 1, m >> 1 & 1, m & 1) for m in range(1, 8)]

    def body(c_ref, w_ref, b_ref, mod_ref, conds_ref, cbuf, sendbuf, recvbuf, send_sems, recv_sems):
        x, y, c, chips = _place()
        flip = lambda v, bit: 1 - v if bit else v
        peers = [(flip(x, mx), flip(y, my), flip(c, mc)) for mx, my, mc in masks]
        dev = lambda p: 4 * p[0] + 2 * p[1] + p[2]
        cbuf[dev((x, y, c))] = c_ref[...]
        first = [_remote(c_ref, cbuf.at[dev((x, y, c))], send_sems.at[i], recv_sems.at[i], p) for i, p in enumerate(peers)]
        for cp in first:
            cp.start()
        for i, p in enumerate(peers):
            _remote(c_ref, cbuf.at[dev(p)], send_sems.at[i], recv_sems.at[i], p).wait_recv()
        conds = jnp.concatenate([cbuf[d, 0:1, :] for d in range(8)], axis=0)
        conds_ref[...] = conds
        act = _silu(conds)
        parts = [mm(act, w_ref[l]) for l in range(DEPTH)]
        row8 = lax.broadcasted_iota(jnp.int32, (8, 1), 0)

        def tile_for(chip):
            r = 2 * (2 * chip[0] + chip[1]) + c
            rows = [jnp.sum(jnp.where(row8 == r, parts[l], 0.0), axis=0, keepdims=True) for l in range(DEPTH)]
            return jnp.where(row8 == 0, rows[0], jnp.where(row8 == 1, rows[1], 0.0))

        my_slot = 2 * x + y
        recvbuf[my_slot] = tile_for((x, y))
        second = []
        for j, chip in enumerate(chips):
            sendbuf[j] = tile_for(chip)
            second.append(_remote(sendbuf.at[j], recvbuf.at[my_slot], send_sems.at[7 + j], recv_sems.at[7 + j], (*chip, c)))
            second[-1].start()
        for j, chip in enumerate(chips):
            _remote(sendbuf.at[j], recvbuf.at[2 * chip[0] + chip[1]], send_sems.at[7 + j], recv_sems.at[7 + j],
                    (*chip, c)).wait_recv()
        rows = [jnp.concatenate([recvbuf[k, l:l + 1, :] for k in range(N_CHIPS)], axis=1) + b_ref[l:l + 1, :]
                for l in range(DEPTH)]
        mod_ref[...] = jnp.concatenate(rows + [jnp.zeros((8 - DEPTH, N_CHIPS * n_cols), F32)], axis=0)
        for cp in first + second:
            cp.wait_send()

    vm = pl.BlockSpec(memory_space=pltpu.VMEM)
    return pl.pallas_call(
        body, name="ada_fwd",
        out_shape=[jax.ShapeDtypeStruct((8, N_CHIPS * n_cols), F32), jax.ShapeDtypeStruct((8, D_MODEL), F32)],
        in_specs=[vm, vm, vm], out_specs=[vm, vm],
        scratch_shapes=[pltpu.VMEM((8, 8, D_MODEL), F32), pltpu.VMEM((3, 8, n_cols), F32),
                        pltpu.VMEM((N_CHIPS, 8, n_cols), F32), pltpu.SemaphoreType.DMA((10,)), pltpu.SemaphoreType.DMA((10,))],
        compiler_params=pltpu.CompilerParams(vmem_limit_bytes=VMEM_LIMIT),
    )(c8, w_shard, b_ada)


def ada_bwd(conds, dmod):
    def body(c_ref, d_ref, o_ref):
        act = _silu(c_ref[...])
        for l in range(DEPTH):
            o_ref[l] = mm_tn(act, d_ref[l])

    return pl.pallas_call(
        body, name="ada_bwd", out_shape=jax.ShapeDtypeStruct((DEPTH, D_MODEL, dmod.shape[2]), F32),
        compiler_params=pltpu.CompilerParams(vmem_limit_bytes=VMEM_LIMIT),
    )(conds, dmod)


def _f_attn(first_block, q, za, kc, vc, kp, vp, qg, kg, sinks):
    w = WINDOW
    lane = lax.broadcasted_iota(jnp.int32, (1, 128), 1)
    halves = [lane < 64, lane >= 64]

    def rms_halves(x, g):
        x2 = x * x
        s0 = jnp.sum(jnp.where(halves[0], x2, 0.0), axis=-1, keepdims=True)
        s1 = jnp.sum(jnp.where(halves[1], x2, 0.0), axis=-1, keepdims=True)
        r = jnp.where(halves[0], lax.rsqrt(s0 / 64.0 + EPS), lax.rsqrt(s1 / 64.0 + EPS))
        return x * r * g

    kcat = rms_halves(jnp.concatenate([kp, kc], axis=0), kg)
    vcat = jnp.concatenate([vp, vc], axis=0)
    qi = lax.broadcasted_iota(jnp.int32, (w, 2 * w), 0)
    kj = lax.broadcasted_iota(jnp.int32, (w, 2 * w), 1)
    dist = qi + w - kj
    valid = (dist >= 0) & (dist < w) & (jnp.logical_not(first_block) | (kj >= w))
    distf = dist.astype(F32)
    outs = []
    for grp in range(4):
        qn = rms_halves(q[:, 128 * grp:128 * grp + 128], qg) * (ATT_HEAD_DIM ** -0.5)
        o_grp = jnp.zeros((w, 128), F32)
        for half in range(2):
            head = HEAD_ORDER[2 * grp + half]
            slope = 2.0 ** (-8.0 * (head + 1) / ATT_HEADS)
            sink = jnp.sum(jnp.where(lane == head, sinks, 0.0), axis=-1, keepdims=True)
            s = mm_nt(jnp.where(halves[half], qn, 0.0), kcat) - slope * distf
            s = jnp.where(valid, s, NEG_INF)
            m = lax.stop_gradient(jnp.maximum(jnp.max(s, axis=-1, keepdims=True), sink))
            p = jnp.exp(s - m)
            denom = jnp.sum(p, axis=-1, keepdims=True) + jnp.exp(sink - m)
            o_grp = o_grp + mm(p / denom, jnp.where(halves[half], vcat, 0.0))
        outs.append(o_grp)
    return jnp.concatenate(outs, axis=1) * _silu(za)


def attn_fwd(name, proj, qg, kg, sinks):
    t = proj.shape[0]
    nb = t // WINDOW

    def body(q_ref, za_ref, kc_ref, vc_ref, kp_ref, vp_ref, qg_ref, kg_ref, s_ref, o_ref):
        first = pl.program_id(0) == 0
        o_ref[...] = _f_attn(first, q_ref[...], za_ref[...], kc_ref[...], vc_ref[...], kp_ref[...], vp_ref[...],
                             qg_ref[...], kg_ref[...], s_ref[...])

    cur = lambda cb: (lambda i: (i, cb))
    prev = lambda cb: (lambda i: (jnp.maximum(i - 1, 0), cb))
    return pl.pallas_call(
        body, name=name, grid=(nb,),
        in_specs=[pl.BlockSpec((WINDOW, 512), cur(P_QA // 512)), pl.BlockSpec((WINDOW, 512), cur(P_ZA // 512)),
                  pl.BlockSpec((WINDOW, 128), cur(P_KA // 128)), pl.BlockSpec((WINDOW, 128), cur(P_VA // 128)),
                  pl.BlockSpec((WINDOW, 128), prev(P_KA // 128)), pl.BlockSpec((WINDOW, 128), prev(P_VA // 128)),
                  _const_spec((1, 128)), _const_spec((1, 128)), _const_spec((1, 128))],
        out_specs=pl.BlockSpec((WINDOW, 512), lambda i: (i, 0)),
        out_shape=jax.ShapeDtypeStruct((t, 512), F32),
        compiler_params=_cparams(1),
    )(proj, proj, proj, proj, proj, proj, qg, kg, sinks)


def attn_bwd(name, proj, qg, kg, sinks, dya):
    t = proj.shape[0]
    nb = t // WINDOW

    def body(q_ref, za_ref, kc_ref, vc_ref, kp_ref, vp_ref, qg_ref, kg_ref, s_ref, dy_ref,
             dqz_ref, dkv_ref, dqg_ref, dkg_ref, ds_ref, carry_ref):
        j = pl.program_id(0)
        first = j == nb - 1

        @pl.when(j == 0)
        def _():
            carry_ref[...] = jnp.zeros_like(carry_ref)
            dqg_ref[...] = jnp.zeros_like(dqg_ref)
            dkg_ref[...] = jnp.zeros_like(dkg_ref)
            ds_ref[...] = jnp.zeros_like(ds_ref)

        ins = [r[...] for r in (q_ref, za_ref, kc_ref, vc_ref, kp_ref, vp_ref, qg_ref, kg_ref, s_ref)]
        _, vjp = jax.vjp(functools.partial(_f_attn, first), *ins)
        dq, dza, dkc, dvc, dkp, dvp, dqg, dkg, dsk = vjp(dy_ref[...])
        dqz_ref[:, 0:512] = dq.astype(dqz_ref.dtype)
        dqz_ref[:, 512:1024] = dza.astype(dqz_ref.dtype)
        dkv_ref[:, 0:128] = (dkc + carry_ref[0]).astype(dkv_ref.dtype)
        dkv_ref[:, 128:256] = (dvc + carry_ref[1]).astype(dkv_ref.dtype)
        carry_ref[0] = dkp
        carry_ref[1] = dvp
        dqg_ref[...] += dqg
        dkg_ref[...] += dkg
        ds_ref[...] += dsk

    cur = lambda cb: (lambda j: (nb - 1 - j, cb))
    prev = lambda cb: (lambda j: (jnp.maximum(nb - 2 - j, 0), cb))
    return pl.pallas_call(
        body, name=name, grid=(nb,),
        in_specs=[pl.BlockSpec((WINDOW, 512), cur(P_QA // 512)), pl.BlockSpec((WINDOW, 512), cur(P_ZA // 512)),
                  pl.BlockSpec((WINDOW, 128), cur(P_KA // 128)), pl.BlockSpec((WINDOW, 128), cur(P_VA // 128)),
                  pl.BlockSpec((WINDOW, 128), prev(P_KA // 128)), pl.BlockSpec((WINDOW, 128), prev(P_VA // 128)),
                  _const_spec((1, 128)), _const_spec((1, 128)), _const_spec((1, 128)),
                  pl.BlockSpec((WINDOW, 512), cur(0))],
        out_specs=[pl.BlockSpec((WINDOW, 1024), cur(0)), pl.BlockSpec((WINDOW, 256), cur(0)),
                   _const_spec((1, 128)), _const_spec((1, 128)), _const_spec((1, 128))],
        out_shape=[jax.ShapeDtypeStruct((t, 1024), BF16), jax.ShapeDtypeStruct((t, 256), BF16),
                   jax.ShapeDtypeStruct((1, 128), F32), jax.ShapeDtypeStruct((1, 128), F32),
                   jax.ShapeDtypeStruct((1, 128), F32)],
        scratch_shapes=[pltpu.VMEM((2, WINDOW, 128), F32)],
        compiler_params=_cparams(1),
    )(proj, proj, proj, proj, proj, proj, qg, kg, sinks, dya)


CONV_ROWS = 256


def _conv_taps(src_ref, w_ref, n_taps, base, t):
    for r0 in range(0, t, CONV_ROWS):
        acc = w_ref[0:1, :] * src_ref[pl.ds(r0 + base, CONV_ROWS), :]
        for k in range(1, n_taps):
            acc = acc + w_ref[k:k + 1, :] * src_ref[pl.ds(r0 + base + k, CONV_ROWS), :]
        yield r0, acc


def _conv_wgrad(dy_ref, src_ref, n_taps, base, t, dy_base=0):
    out = []
    for k in range(n_taps):
        acc = jnp.zeros((8, 128), F32)
        for r0 in range(0, t, CONV_ROWS):
            prod = dy_ref[pl.ds(r0 + dy_base, CONV_ROWS), :] * src_ref[pl.ds(r0 + base + k, CONV_ROWS), :]
            acc = acc + jnp.sum(prod.reshape(CONV_ROWS // 8, 8, 128), axis=0)
        out.append(jnp.sum(acc, axis=0, keepdims=True))
    return out


def glu_conv_fwd(name, proj, w32, bias):
    t = proj.shape[0]
    pad = 32

    def body(x_ref, w_ref, b_ref, o_ref, u_ref):
        u_ref[0:pad, :] = jnp.zeros((pad, 128), F32)
        u_ref[pad:pad + t, :] = x_ref[:, 0:128] * _sigmoid(x_ref[:, 128:256])
        for r0, acc in _conv_taps(u_ref, w_ref, CONV_K, pad - (CONV_K - 1), t):
            o_ref[pl.ds(r0, CONV_ROWS), :] = acc + b_ref[...]

    return pl.pallas_call(
        body, name=name, grid=(4,),
        in_specs=[pl.BlockSpec((t, 256), lambda cb: (0, P_GLU // 256 + cb)), pl.BlockSpec((32, 128), lambda cb: (0, cb)),
                  pl.BlockSpec((1, 128), lambda cb: (0, cb))],
        out_specs=pl.BlockSpec((t, 128), lambda cb: (0, cb)),
        out_shape=jax.ShapeDtypeStruct((t, 512), F32),
        scratch_shapes=[pltpu.VMEM((t + pad, 128), F32)],
        compiler_params=_cparams(1),
    )(proj, w32, bias)


def glu_conv_bwd(name, proj, w32, dub):
    t = proj.shape[0]
    pad = 32
    k1 = CONV_K - 1

    def body(x_ref, w_ref, dy_ref, dx_ref, dw_ref, db_ref, u_ref, dyp_ref, wrev_ref):
        val = x_ref[:, 0:128]
        sg = _sigmoid(x_ref[:, 128:256])
        u_ref[0:pad, :] = jnp.zeros((pad, 128), F32)
        u_ref[pad:pad + t, :] = val * sg
        dyp_ref[0:t, :] = dy_ref[...]
        dyp_ref[t:t + pad, :] = jnp.zeros((pad, 128), F32)
        for k in range(CONV_K):
            wrev_ref[k:k + 1, :] = w_ref[k1 - k:k1 - k + 1, :]
        wrev_ref[CONV_K:32, :] = jnp.zeros((32 - CONV_K, 128), F32)
        for r0, du in _conv_taps(dyp_ref, wrev_ref, CONV_K, 0, t):
            v = x_ref[pl.ds(r0, CONV_ROWS), 0:128]
            s = _sigmoid(x_ref[pl.ds(r0, CONV_ROWS), 128:256])
            dx_ref[pl.ds(r0, CONV_ROWS), 0:128] = (du * s).astype(dx_ref.dtype)
            dx_ref[pl.ds(r0, CONV_ROWS), 128:256] = (du * v * s * (1.0 - s)).astype(dx_ref.dtype)
        dws = _conv_wgrad(dyp_ref, u_ref, CONV_K, pad - k1, t)
        for k in range(CONV_K):
            dw_ref[k:k + 1, :] = dws[k]
        dw_ref[CONV_K:32, :] = jnp.zeros((32 - CONV_K, 128), F32)
        db_ref[...] = jnp.sum(dy_ref[...], axis=0, keepdims=True)

    return pl.pallas_call(
        body, name=name, grid=(4,),
        in_specs=[pl.BlockSpec((t, 256), lambda cb: (0, P_GLU // 256 + cb)), pl.BlockSpec((32, 128), lambda cb: (0, cb)),
                  pl.BlockSpec((t, 128), lambda cb: (0, cb))],
        out_specs=[pl.BlockSpec((t, 256), lambda cb: (0, cb)), pl.BlockSpec((32, 128), lambda cb: (0, cb)),
                   pl.BlockSpec((1, 128), lambda cb: (0, cb))],
        out_shape=[jax.ShapeDtypeStruct((t, 1024), BF16), jax.ShapeDtypeStruct((32, 512), F32),
                   jax.ShapeDtypeStruct((1, 512), F32)],
        scratch_shapes=[pltpu.VMEM((t + pad, 128), F32), pltpu.VMEM((t + pad, 128), F32), pltpu.VMEM((32, 128), F32)],
        compiler_params=_cparams(1),
    )(proj, w32, dub)


def sconv_fwd(name, proj, w8):
    t = proj.shape[0]
    pad = 8
    k1 = DN_CONV_K - 1

    def body(x_ref, w_ref, o_ref, xp_ref):
        xp_ref[0:pad, :] = jnp.zeros((pad, 128), F32)
        xp_ref[pad:pad + t, :] = x_ref[...]
        for r0, acc in _conv_taps(xp_ref, w_ref, DN_CONV_K, pad - k1, t):
            o_ref[pl.ds(r0, CONV_ROWS), :] = _silu(acc)

    return pl.pallas_call(
        body, name=name, grid=(12,),
        in_specs=[pl.BlockSpec((t, 128), lambda cb: (0, P_QKV // 128 + cb)), pl.BlockSpec((8, 128), lambda cb: (0, cb))],
        out_specs=pl.BlockSpec((t, 128), lambda cb: (0, cb)),
        out_shape=jax.ShapeDtypeStruct((t, 1536), F32),
        scratch_shapes=[pltpu.VMEM((t + pad, 128), F32)],
        compiler_params=_cparams(1),
    )(proj, w8)


def sconv_bwd(name, proj, w8, dqkv):
    t = proj.shape[0]
    pad = 8
    k1 = DN_CONV_K - 1

    def body(x_ref, w_ref, dy_ref, dx_ref, dw_ref, xp_ref, dpp_ref, wrev_ref):
        xp_ref[0:pad, :] = jnp.zeros((pad, 128), F32)
        xp_ref[pad:pad + t, :] = x_ref[...]
        for r0, pre in _conv_taps(xp_ref, w_ref, DN_CONV_K, pad - k1, t):
            s = _sigmoid(pre)
            dpp_ref[pl.ds(r0, CONV_ROWS), :] = dy_ref[pl.ds(r0, CONV_ROWS), :] * (s * (1.0 + pre * (1.0 - s)))
        dpp_ref[t:t + pad, :] = jnp.zeros((pad, 128), F32)
        for k in range(DN_CONV_K):
            wrev_ref[k:k + 1, :] = w_ref[k1 - k:k1 - k + 1, :]
        wrev_ref[DN_CONV_K:8, :] = jnp.zeros((8 - DN_CONV_K, 128), F32)
        for r0, dx in _conv_taps(dpp_ref, wrev_ref, DN_CONV_K, 0, t):
            dx_ref[pl.ds(r0, CONV_ROWS), :] = dx.astype(dx_ref.dtype)
        dws = _conv_wgrad(dpp_ref, xp_ref, DN_CONV_K, pad - k1, t)
        for k in range(DN_CONV_K):
            dw_ref[k:k + 1, :] = dws[k]
        dw_ref[DN_CONV_K:8, :] = jnp.zeros((8 - DN_CONV_K, 128), F32)

    return pl.pallas_call(
        body, name=name, grid=(12,),
        in_specs=[pl.BlockSpec((t, 128), lambda cb: (0, P_QKV // 128 + cb)), pl.BlockSpec((8, 128), lambda cb: (0, cb)),
                  pl.BlockSpec((t, 128), lambda cb: (0, cb))],
        out_specs=[pl.BlockSpec((t, 128), lambda cb: (0, cb)), pl.BlockSpec((8, 128), lambda cb: (0, cb))],
        out_shape=[jax.ShapeDtypeStruct((t, 1536), BF16), jax.ShapeDtypeStruct((8, 1536), F32)],
        scratch_shapes=[pltpu.VMEM((t + pad, 128), F32), pltpu.VMEM((t + pad, 128), F32), pltpu.VMEM((8, 128), F32)],
        compiler_params=_cparams(1),
    )(proj, w8, dqkv)


def _f_delta_step(qkv, ab, zc, s0, s1, s2, s3, a_log, dt_bias, dn_g):
    cs = DN_CHUNK
    n = 2 * cs
    states = (s0, s1, s2, s3)
    lane = lax.broadcasted_iota(jnp.int32, (1, 128), 1)
    ri = lax.broadcasted_iota(jnp.int32, (n, n), 0)
    ci = lax.broadcasted_iota(jnp.int32, (n, n), 1)
    same = (ri // cs) == (ci // cs)
    lower = same & (ri >= ci)
    strict = same & (ri > ci)
    sums = jnp.concatenate([jnp.where(lower, 1.0, 0.0), jnp.where(same, 1.0, 0.0), jnp.where(ci < cs, 1.0, 0.0),
                            jnp.where(ci >= cs, 1.0, 0.0)], axis=0)
    top = lax.broadcasted_iota(jnp.int32, (n, 1), 0) < cs

    def pick(row, idx):
        return jnp.sum(jnp.where(lane == idx, row, 0.0), axis=-1, keepdims=True)

    def l2n(x):
        return x * lax.rsqrt(jnp.sum(x * x, axis=-1, keepdims=True) + EPS)

    n_chunks = qkv.shape[0] // cs
    units = [(k, pair) for k in range(n_chunks) for pair in range(2)]

    pre = []
    for k, pair in units:
        hs = (2 * pair, 2 * pair + 1)
        rows = slice(k * cs, (k + 1) * cs)
        stack = lambda f: jnp.concatenate([f(hs[0]), f(hs[1])], axis=0)
        qd = l2n(stack(lambda h: qkv[rows, 128 * h:128 * h + 128])) * (128 ** -0.5)
        kd = l2n(stack(lambda h: qkv[rows, 512 + 128 * h:512 + 128 * h + 128]))
        vd = stack(lambda h: qkv[rows, 1024 + 128 * h:1024 + 128 * h + 128])
        beta = _sigmoid(stack(lambda h: pick(ab[rows], 4 + h)))
        g = stack(lambda h: -jnp.exp(pick(a_log, h)) * _softplus(pick(ab[rows], h) + pick(dt_bias, h)))
        g_sums = sel_mm(sums, g * jnp.ones((1, n), F32))
        gc_col = g_sums[0:n]
        gl_b = g_sums[n:2 * n]
        g_end = (g_sums[2 * n:3 * n], g_sums[3 * n:])
        decay = jnp.where(lower, jnp.exp(jnp.where(lower, gc_col - gc_col.T, 0.0)), 0.0)
        kb = kd * beta
        pre.append(dict(qd=qd, kd=kd, vb=vd * beta, kb=kb, gc_col=gc_col, gl_b=gl_b, g_end=g_end, decay=decay,
                        a=jnp.where(strict, mm_nt(kb, kd) * decay, 0.0)))
    tmats = tri_inv(*[p["a"] for p in pre])

    mid = []
    for p, tmat in zip(pre, tmats):
        egc = jnp.exp(p["gc_col"])
        mid.append(dict(u=mm(tmat, p["vb"]), wm=mm(tmat, p["kb"] * egc), qe=p["qd"] * egc,
                        intra=jnp.where(lower, mm_nt(p["qd"], p["kd"]) * p["decay"], 0.0),
                        ke=p["kd"] * jnp.exp(p["gl_b"] - p["gc_col"]), g_end=p["g_end"]))

    ys = []
    for k in range(n_chunks):
        rows = slice(k * cs, (k + 1) * cs)
        new_states, y_heads = [], []
        for pair in range(2):
            m = mid[2 * k + pair]
            hs = (2 * pair, 2 * pair + 1)
            st = (states[hs[0]], states[hs[1]])
            v_new = m["u"] - jnp.concatenate([mm(m["wm"][:cs], st[0]), mm(m["wm"][cs:], st[1])], axis=0)
            o = jnp.concatenate([mm(m["qe"][:cs], st[0]), mm(m["qe"][cs:], st[1])], axis=0) + mm(m["intra"], v_new)
            new_states.append(st[0] * jnp.exp(m["g_end"][0]) + mm_tn(jnp.where(top, m["ke"], 0.0), v_new))
            new_states.append(st[1] * jnp.exp(m["g_end"][1]) + mm_tn(jnp.where(top, 0.0, m["ke"]), v_new))
            od = o * lax.rsqrt(jnp.mean(o * o, axis=-1, keepdims=True) + EPS) * dn_g
            y_heads += [od[:cs] * _silu(zc[rows, 128 * hs[0]:128 * hs[0] + 128]),
                        od[cs:] * _silu(zc[rows, 128 * hs[1]:128 * hs[1] + 128])]
        states = tuple(new_states)
        ys.append(jnp.concatenate(y_heads, axis=1))
    return (jnp.concatenate(ys, axis=0), *states)


DELTA_ROWS = 4 * DN_CHUNK


def delta_fwd(name, qkv, proj, a_log, dt_bias, dn_g):
    t = qkv.shape[0]
    nc = t // DELTA_ROWS

    def body(qkv_ref, ab_ref, zc_ref, al_ref, dt_ref, g_ref, y_ref, ssave_ref, s_ref):
        @pl.when(pl.program_id(0) == 0)
        def _():
            s_ref[...] = jnp.zeros_like(s_ref)

        ssave_ref[0] = s_ref[...]
        st = [s_ref[128 * h:128 * h + 128, :] for h in range(4)]
        y, *ns = _f_delta_step(qkv_ref[...], ab_ref[...], zc_ref[...], *st, al_ref[...], dt_ref[...], g_ref[...])
        y_ref[...] = y
        for h in range(4):
            s_ref[128 * h:128 * h + 128, :] = ns[h]

    return pl.pallas_call(
        body, name=name, grid=(nc,),
        in_specs=[pl.BlockSpec((DELTA_ROWS, 1536), lambda i: (i, 0)), pl.BlockSpec((DELTA_ROWS, 128), lambda i: (i, P_AB // 128)),
                  pl.BlockSpec((DELTA_ROWS, 512), lambda i: (i, P_ZC // 512)),
                  _const_spec((1, 128)), _const_spec((1, 128)), _const_spec((1, 128))],
        out_specs=[pl.BlockSpec((DELTA_ROWS, 512), lambda i: (i, 0)), pl.BlockSpec((1, 512, 128), lambda i: (i, 0, 0))],
        out_shape=[jax.ShapeDtypeStruct((t, 512), F32), jax.ShapeDtypeStruct((nc, 512, 128), F32)],
        scratch_shapes=[pltpu.VMEM((512, 128), F32)],
        compiler_params=_cparams(1),
    )(qkv, proj, proj, a_log, dt_bias, dn_g)


def delta_bwd(name, qkv, proj, ssave, a_log, dt_bias, dn_g, dyc):
    t = qkv.shape[0]
    nc = t // DELTA_ROWS

    def body(qkv_ref, ab_ref, zc_ref, ss_ref, al_ref, dt_ref, g_ref, dy_ref,
             dqkv_ref, dab_ref, dzc_ref, dal_ref, ddt_ref, dg_ref, ds_ref):
        @pl.when(pl.program_id(0) == 0)
        def _():
            ds_ref[...] = jnp.zeros_like(ds_ref)
            dal_ref[...] = jnp.zeros_like(dal_ref)
            ddt_ref[...] = jnp.zeros_like(ddt_ref)
            dg_ref[...] = jnp.zeros_like(dg_ref)

        st = [ss_ref[0, 128 * h:128 * h + 128, :] for h in range(4)]
        _, vjp = jax.vjp(_f_delta_step, qkv_ref[...], ab_ref[...], zc_ref[...], *st, al_ref[...], dt_ref[...], g_ref[...])
        dst = tuple(ds_ref[128 * h:128 * h + 128, :] for h in range(4))
        dqkv, dab, dzc, d0, d1, d2, d3, dal, ddt, dg = vjp((dy_ref[...], *dst))
        dqkv_ref[...] = dqkv
        dab_ref[...] = dab.astype(dab_ref.dtype)
        dzc_ref[...] = dzc.astype(dzc_ref.dtype)
        for h, d in enumerate((d0, d1, d2, d3)):
            ds_ref[128 * h:128 * h + 128, :] = d
        dal_ref[...] += dal
        ddt_ref[...] += ddt
        dg_ref[...] += dg

    rev = lambda cb: (lambda j: (nc - 1 - j, cb))
    return pl.pallas_call(
        body, name=name, grid=(nc,),
        in_specs=[pl.BlockSpec((DELTA_ROWS, 1536), rev(0)), pl.BlockSpec((DELTA_ROWS, 128), rev(P_AB // 128)),
                  pl.BlockSpec((DELTA_ROWS, 512), rev(P_ZC // 512)), pl.BlockSpec((1, 512, 128), lambda j: (nc - 1 - j, 0, 0)),
                  _const_spec((1, 128)), _const_spec((1, 128)), _const_spec((1, 128)),
                  pl.BlockSpec((DELTA_ROWS, 512), rev(0))],
        out_specs=[pl.BlockSpec((DELTA_ROWS, 1536), rev(0)), pl.BlockSpec((DELTA_ROWS, 128), rev(0)),
                   pl.BlockSpec((DELTA_ROWS, 512), rev(0)),
                   _const_spec((1, 128)), _const_spec((1, 128)), _const_spec((1, 128))],
        out_shape=[jax.ShapeDtypeStruct((t, 1536), F32), jax.ShapeDtypeStruct((t, 128), BF16),
                   jax.ShapeDtypeStruct((t, 512), BF16),
                   jax.ShapeDtypeStruct((1, 128), F32), jax.ShapeDtypeStruct((1, 128), F32), jax.ShapeDtypeStruct((1, 128), F32)],
        scratch_shapes=[pltpu.VMEM((512, 128), F32)],
        compiler_params=_cparams(1),
    )(qkv, proj, proj, ssave, a_log, dt_bias, dn_g, dyc)


def loss_head(name, y, target, tm):
    t, d = y.shape

    def body(y_ref, t_ref, dy_ref, l_ref):
        err = y_ref[...] - t_ref[...]
        dy_ref[...] = err * (1.0 / d)
        part = 0.5 * jnp.sum(jnp.sum(err * err, axis=-1, keepdims=True) * (1.0 / d), axis=0, keepdims=True)

        @pl.when(pl.program_id(0) == 0)
        def _():
            l_ref[...] = part

        @pl.when(pl.program_id(0) > 0)
        def _():
            l_ref[...] += part

    return pl.pallas_call(
        body, name=name, grid=(t // tm,),
        in_specs=[_row_spec(tm, d, 0), _row_spec(tm, d, 0)],
        out_specs=[_row_spec(tm, d, 0), _const_spec((1, 1))],
        out_shape=[jax.ShapeDtypeStruct((t, d), F32), jax.ShapeDtypeStruct((1, 1), F32)],
        compiler_params=_cparams(1),
    )(y, target)


TM = 512
TM_MERGE = 256
TM_IN = 1024
TN_IN = 1152


def _lane_pad(v, n=128):
    return jnp.pad(v.astype(F32), (0, n - v.shape[0]))[None, :]


def f_norm_mod_res(x, g, scale, shift):
    return f_norm_mod(x, g, scale, shift), x


def prep_layer(w):
    p = dict(w)
    p["wp"] = _pad_w_in_from_shards(w["w_in"])
    p["wpt"] = p["wp"].T
    p["wpa"] = _perm_heads_rows(w["w_proj_a"])
    p["dw32"] = jnp.pad(w["dw_w"], ((0, 32 - CONV_K), (0, 0)))
    p["sconv8"] = jnp.pad(w["sconv_w"], ((0, 8 - DN_CONV_K), (0, 0)))
    p["qg"] = jnp.tile(w["q_norm_g"], 2)[None, :]
    p["kg"] = jnp.tile(w["k_norm_g"], 2)[None, :]
    p["sinks128"] = _lane_pad(w["sinks"])
    p["al"] = _lane_pad(w["a_log"])
    p["dtb"] = _lane_pad(w["dt_bias"])
    p["dng"] = w["dn_norm_g"][None, :]
    return p


def layer_fwd(tag, x, mod, p):
    d = D_MODEL
    shift, scale, gate = mod[:, :d], mod[:, d:2 * d], mod[:, 2 * d:]
    g = p["norm_g"][None, :]
    (h,) = rowwise_fwd(f"norm_fwd{tag}", f_norm_mod, [(x, d, 0)], [g, scale, shift], [(d, BF16)], TM)
    proj = matmul_nn(f"inproj_fwd{tag}", h, p["wp"], F32, TM_IN, TN_IN, d)
    ya = attn_fwd(f"attn_fwd{tag}", proj, p["qg"], p["kg"], p["sinks128"])
    ub = glu_conv_fwd(f"glu_conv_fwd{tag}", proj, p["dw32"], p["dw_b"][None, :])
    conf_consts = [p["ln_g"][None, :], p["ln_b"][None, :], p["pw2_w"], p["pw2_b"][None, :]]
    (yb,) = rowwise_fwd(f"conf_fwd{tag}", f_conf_tail, [(ub, 512, 0), (proj, 512, P_ZB // 512)], conf_consts, [(512, F32)], TM)
    qkv = sconv_fwd(f"sconv_fwd{tag}", proj, p["sconv8"])
    yc, ssave = delta_fwd(f"delta_fwd{tag}", qkv, proj, p["al"], p["dtb"], p["dng"])
    merge_consts = [gate, p["wpa"], p["w_proj_b"], p["w_proj_c"], p["w_out"]]
    merge_rows = [(ya, 512, 0), (yb, 512, 0), (yc, 512, 0), (proj, 3 * d, P_MG // (3 * d)), (x, d, 0)]
    (xn,) = rowwise_fwd(f"merge_fwd{tag}", f_merge, merge_rows, merge_consts, [(d, F32)], TM_MERGE)
    saved = dict(x=x, h=h, proj=proj, ub=ub, qkv=qkv, ssave=ssave, norm_consts=[g, scale, shift],
                 conf_consts=conf_consts, merge_consts=merge_consts, merge_rows=merge_rows)
    return xn, saved


def layer_bwd(tag, dxn, p, s):
    d = D_MODEL
    proj = s["proj"]
    dya, dyb, dyc, dmg, dgate, dwpa, dwpb, dwpc, dwout = rowwise_bwd(
        f"merge_bwd{tag}", f_merge, s["merge_rows"], s["merge_consts"], [(dxn, d, 0)], [F32, F32, F32, BF16, None], TM_MERGE)
    dqz, dkv, dqg, dkg, dsinks = attn_bwd(f"attn_bwd{tag}", proj, p["qg"], p["kg"], p["sinks128"], dya)
    dub, dzb, dln_g, dln_b, dpw2_w, dpw2_b = rowwise_bwd(
        f"conf_bwd{tag}", f_conf_tail, [(s["ub"], 512, 0), (proj, 512, P_ZB // 512)], s["conf_consts"], [(dyb, 512, 0)],
        [F32, BF16], TM)
    dglu, ddw32, ddw_b = glu_conv_bwd(f"glu_conv_bwd{tag}", proj, p["dw32"], dub)
    dqkv, dab, dzc, dal, ddtb, ddng = delta_bwd(f"delta_bwd{tag}", s["qkv"], proj, s["ssave"], p["al"], p["dtb"], p["dng"], dyc)
    dqkv_pre, dsconv8 = sconv_bwd(f"sconv_bwd{tag}", proj, p["sconv8"], dqkv)
    dproj = jnp.concatenate([dqz, dglu, dzb, dzc, dmg, dqkv_pre, dkv, dab], axis=1)
    dh = matmul_nn(f"inproj_bwd_dh{tag}", dproj, p["wpt"], F32, TM_IN, d, TN_IN)
    dwp = matmul_nn(f"inproj_bwd_dw{tag}", s["h"].T, dproj, F32, d, TN_IN, TM)
    dx, dnorm_g, dscale, dshift = rowwise_bwd(
        f"norm_bwd{tag}", f_norm_mod_res, [(s["x"], d, 0)], s["norm_consts"], [(dh, d, 0), (dxn, d, 0)], [F32], TM)
    dmod = jnp.concatenate([dshift, dscale, dgate], axis=1)
    grads = dict(
        b_ada=dmod[0], norm_g=dnorm_g[0], w_in=_unpad_w_in_to_shards(dwp),
        q_norm_g=dqg[0, :64] + dqg[0, 64:], k_norm_g=dkg[0, :64] + dkg[0, 64:], sinks=dsinks[0, :ATT_HEADS],
        dw_w=ddw32[:CONV_K], dw_b=ddw_b[0], ln_g=dln_g[0], ln_b=dln_b[0], pw2_w=dpw2_w, pw2_b=dpw2_b[0],
        sconv_w=dsconv8[:DN_CONV_K], a_log=dal[0, :DN_HEADS], dt_bias=ddtb[0, :DN_HEADS], dn_norm_g=ddng[0],
        w_proj_a=_unperm_heads_rows(dwpa), w_proj_b=dwpb, w_proj_c=dwpc, w_out=dwout)
    return dx, grads


SHARDED = {"w_ada": 2, "w_in": 2, "dw_w": 2, "pw2_w": 1, "sconv_w": 2, "w_proj_a": 2, "w_proj_b": 2, "w_proj_c": 2,
           "w_out": 1}
GATHERED = tuple(n for n in SHARDED if n != "w_ada")
GATHER_F32 = ("dw_w", "sconv_w")
REDUCE_BIG = tuple(n for n in GATHERED if n not in GATHER_F32)
SMALL = ("b_ada", "norm_g", "q_norm_g", "k_norm_g", "sinks", "dw_b", "ln_g", "ln_b", "pw2_b", "a_log", "dt_bias",
         "dn_norm_g")
SMALL_ROWS = 104
SMALL_GRAD_ROWS = 448
W_IN_SHARD = D_IN // N_CHIPS
SUM_TILE = 256


def _shard_cols(shards, start, n):
    parts = []
    while n > 0:
        k, o = divmod(start, W_IN_SHARD)
        m = min(n, W_IN_SHARD - o)
        parts.append(shards[k][:, o:o + m])
        start, n = start + m, n - m
    return parts


def _pad_w_in_from_shards(shards):
    parts = []
    for s, n in _in_pieces():
        parts += _shard_cols(shards, s, n)
    parts.append(jnp.zeros((shards.shape[1], P_TOTAL - D_IN), shards.dtype))
    return jnp.concatenate(parts, axis=1)


def _unpad_w_in_to_shards(wp):
    pieces = _in_pieces()
    starts = np.cumsum([0] + [n for _, n in pieces])[:-1]
    order = sorted(range(len(pieces)), key=lambda i: pieces[i][0])
    shards = []
    for k in range(N_CHIPS):
        lo, hi = k * W_IN_SHARD, (k + 1) * W_IN_SHARD
        parts = []
        for i in order:
            s, n = pieces[i]
            a, b = max(s, lo), min(s + n, hi)
            if a < b:
                parts.append(wp[:, int(starts[i]) + a - s:int(starts[i]) + b - s])
        shards.append(jnp.concatenate(parts, axis=1))
    return jnp.stack(shards)


def _join_layer(v, axis):
    if axis == 2:
        return jnp.transpose(v, (1, 0, 2)).reshape(v.shape[1], N_CHIPS * v.shape[2])
    return v.reshape(N_CHIPS * v.shape[1], v.shape[2])


def _split_layer(v, axis):
    a, b = v.shape
    if axis == 2:
        return jnp.transpose(v.reshape(a, N_CHIPS, b // N_CHIPS), (1, 0, 2))
    return v.reshape(N_CHIPS, a // N_CHIPS, b)


def pack_small(vals, names, rows):
    flat = jnp.concatenate([vals[n].astype(F32).reshape(-1) for n in names])
    return jnp.pad(flat, (0, rows * 128 - flat.shape[0])).reshape(rows, 128)


def unpack_small(packed, names, shapes):
    flat = packed.reshape(-1)
    out, off = {}, 0
    for n in names:
        k = int(np.prod(shapes[n]))
        out[n] = flat[off:off + k].reshape(shapes[n])
        off += k
    return out


ANY = pl.BlockSpec(memory_space=pl.ANY)


def _place():
    x, y, c = lax.axis_index("x"), lax.axis_index("y"), lax.axis_index("c")
    chips = [(1 - x, y), (x, 1 - y), (1 - x, 1 - y)]
    return x, y, c, chips


def _remote(src, dst, send_sem, recv_sem, to):
    return pltpu.make_async_remote_copy(src_ref=src, dst_ref=dst, send_sem=send_sem, recv_sem=recv_sem, device_id=to,
                                        device_id_type=MESH)


def weights_allgather(slots):
    n = len(slots)

    def body(*refs):
        out = refs[n:2 * n]
        send_sems, recv_sems = refs[2 * n:]
        x, y, c, chips = _place()
        me, sibling, my_slot = (x, y, c), (x, y, 1 - c), 2 * x + y
        sends = []
        for j, chip in enumerate(chips):
            for t in range(n):
                mine = out[t].at[my_slot, c]
                sends.append(_remote(mine, mine, send_sems.at[t, j], recv_sems.at[t, j], (*chip, c)))
                sends[-1].start()
        for j, chip in enumerate(chips):
            for t in range(n):
                land = out[t].at[2 * chip[0] + chip[1], c]
                _remote(land, land, send_sems.at[t, j], recv_sems.at[t, j], me).wait_recv()
                sends.append(_remote(land, land, send_sems.at[t, 3 + j], recv_sems.at[t, 3 + j], sibling))
                sends[-1].start()
        for j, chip in enumerate(chips):
            for t in range(n):
                land = out[t].at[2 * chip[0] + chip[1], 1 - c]
                _remote(land, land, send_sems.at[t, 3 + j], recv_sems.at[t, 3 + j], me).wait_recv()
        for cp in sends:
            cp.wait_send()

    return pl.pallas_call(
        body, name="weights_allgather", out_shape=[jax.ShapeDtypeStruct(s.shape, s.dtype) for s in slots],
        in_specs=[ANY] * n, out_specs=[ANY] * n, input_output_aliases={t: t for t in range(n)},
        scratch_shapes=[pltpu.SemaphoreType.DMA((n, 6)), pltpu.SemaphoreType.DMA((n, 6))],
    )(*slots)


def grads_pair_exchange(gs):
    n = len(gs)

    def body(*refs):
        g, recv = refs[:n], refs[n:2 * n]
        send_sems, recv_sems = refs[2 * n:]
        x, y, c, _ = _place()
        cps = [_remote(g[t].at[:, 1 - c], recv[t], send_sems.at[t], recv_sems.at[t], (x, y, 1 - c)) for t in range(n)]
        for cp in cps:
            cp.start()
        for cp in cps:
            cp.wait()

    return pl.pallas_call(
        body, name="grads_pair_exchange",
        out_shape=[jax.ShapeDtypeStruct((N_CHIPS,) + g.shape[2:], g.dtype) for g in gs],
        in_specs=[ANY] * n, out_specs=[ANY] * n,
        scratch_shapes=[pltpu.SemaphoreType.DMA((n,)), pltpu.SemaphoreType.DMA((n,))],
    )(*gs)


def grads_pair_sum(name, g, recv):
    _, a, b = recv.shape
    ta = min(a, SUM_TILE)

    def body(a_ref, b_ref, o_ref):
        o_ref[...] = (a_ref[...] + b_ref[...]).astype(o_ref.dtype)

    return pl.pallas_call(
        body, name=name, grid=(N_CHIPS, a // ta),
        in_specs=[pl.BlockSpec((None, None, ta, b), lambda s, i: (s, lax.axis_index("c"), i, 0)),
                  pl.BlockSpec((None, ta, b), lambda s, i: (s, i, 0))],
        out_specs=pl.BlockSpec((None, ta, b), lambda s, i: (s, i, 0)),
        out_shape=jax.ShapeDtypeStruct(recv.shape, BF16),
        compiler_params=_cparams(2),
    )(g, recv)


def grads_chip_exchange(ps):
    n = len(ps)

    def body(*refs):
        p, recv = refs[:n], refs[n:2 * n]
        send_sems, recv_sems = refs[2 * n:]
        x, y, c, chips = _place()
        cps = [_remote(p[t].at[2 * chip[0] + chip[1]], recv[t].at[j], send_sems.at[t, j], recv_sems.at[t, j], (*chip, c))
               for j, chip in enumerate(chips) for t in range(n)]
        for cp in cps:
            cp.start()
        for cp in cps:
            cp.wait()

    return pl.pallas_call(
        body, name="grads_chip_exchange", out_shape=[jax.ShapeDtypeStruct((3,) + p.shape[1:], p.dtype) for p in ps],
        in_specs=[ANY] * n, out_specs=[ANY] * n,
        scratch_shapes=[pltpu.SemaphoreType.DMA((n, 3)), pltpu.SemaphoreType.DMA((n, 3))],
    )(*ps)


def grads_chip_sum(name, g, recv, recv2):
    _, a, b = recv.shape
    ta = min(a, SUM_TILE)
    my_slot = lambda: 2 * lax.axis_index("x") + lax.axis_index("y")

    def body(g_ref, r_ref, r2_ref, o_ref):
        own = g_ref[...] + r_ref[...]
        o_ref[...] = ((own + r2_ref[0].astype(F32)) + r2_ref[1].astype(F32)) + r2_ref[2].astype(F32)

    return pl.pallas_call(
        body, name=name, grid=(a // ta,),
        in_specs=[pl.BlockSpec((None, None, ta, b), lambda i: (my_slot(), lax.axis_index("c"), i, 0)),
                  pl.BlockSpec((None, ta, b), lambda i: (my_slot(), i, 0)),
                  pl.BlockSpec((3, ta, b), lambda i: (0, i, 0))],
        out_specs=pl.BlockSpec((None, ta, b), lambda i: (lax.axis_index("c"), i, 0)),
        out_shape=jax.ShapeDtypeStruct((DEPTH, a, b), F32),
        compiler_params=_cparams(1),
    )(g, recv, recv2)


def grads_pair_gather(reds):
    n = len(reds)

    def body(*refs):
        buf = refs[n:2 * n]
        send_sems, recv_sems = refs[2 * n:]
        x, y, c, _ = _place()
        sibling = (x, y, 1 - c)
        cps = [_remote(buf[t].at[c], buf[t].at[c], send_sems.at[t], recv_sems.at[t], sibling) for t in range(n)]
        for cp in cps:
            cp.start()
        for t in range(n):
            _remote(buf[t].at[c], buf[t].at[1 - c], send_sems.at[t], recv_sems.at[t], sibling).wait_recv()
        for cp in cps:
            cp.wait_send()

    return pl.pallas_call(
        body, name="grads_pair_gather", out_shape=[jax.ShapeDtypeStruct(r.shape, r.dtype) for r in reds],
        in_specs=[ANY] * n, out_specs=[ANY] * n, input_output_aliases={t: t for t in range(n)},
        scratch_shapes=[pltpu.SemaphoreType.DMA((n,)), pltpu.SemaphoreType.DMA((n,))],
    )(*reds)


def small_allreduce(v):
    m, n = v.shape

    def body(x_ref, sum_ref, all_ref, send_sems, recv_sems, local_sem):
        x, y, c, chips = _place()
        me, sibling = (x, y, c), (x, y, 1 - c)

        def rows(px, py, pc):
            return all_ref.at[pl.ds((4 * px + 2 * py + pc) * m, m), :]

        def copy(k, block, to, src=None):
            return pltpu.make_async_remote_copy(src_ref=rows(*block) if src is None else src, dst_ref=rows(*block),
                                                send_sem=send_sems.at[k], recv_sem=recv_sems.at[k],
                                                device_id=to, device_id_type=MESH)

        mine = pltpu.make_async_copy(x_ref, rows(*me), local_sem)
        mine.start()
        first = [copy(0, me, sibling, src=x_ref)]
        first += [copy(1 + j, me, (*chip, c), src=x_ref) for j, chip in enumerate(chips)]
        for cp in first:
            cp.start()
        passed = [copy(4 + j, (*chip, c), sibling) for j, chip in enumerate(chips)]
        for j, chip in enumerate(chips):
            copy(1 + j, (*chip, c), me).wait_recv()
            passed[j].start()
        copy(0, sibling, me).wait_recv()
        for j, chip in enumerate(chips):
            copy(4 + j, (*chip, 1 - c), me).wait_recv()
        for cp in first + passed:
            cp.wait_send()
        mine.wait()
        acc = all_ref[0:m, :]
        for dev in range(1, 8):
            acc = acc + all_ref[dev * m:(dev + 1) * m, :]
        sum_ref[...] = acc

    vm = pl.BlockSpec(memory_space=pltpu.VMEM)
    return pl.pallas_call(
        body, name="small_allreduce",
        out_shape=[jax.ShapeDtypeStruct((m, n), F32), jax.ShapeDtypeStruct((8 * m, n), F32)],
        in_specs=[vm], out_specs=[vm, vm],
        scratch_shapes=[pltpu.SemaphoreType.DMA((7,)), pltpu.SemaphoreType.DMA((7,)), pltpu.SemaphoreType.DMA],
    )(v)


def reduce_scatter_grads(names, gs):
    recv = grads_pair_exchange(gs)
    parts = [grads_pair_sum("grads_pair_sum_" + n, g, r) for n, g, r in zip(names, gs, recv)]
    recv2 = grads_chip_exchange(parts)
    reds = [grads_chip_sum("grads_chip_sum_" + n, g, r, r2) for n, g, r, r2 in zip(names, gs, recv, recv2)]
    return grads_pair_gather(reds)


def adamw(name, w, g, m, v, tr, tc=None):
    r, cols = w.shape
    tc = cols if tc is None else tc

    def body(w_ref, g_ref, m_ref, v_ref, d_ref, nm_ref, nv_ref):
        gv = g_ref[...]
        nm = ADAM_B1 * m_ref[...] + (1.0 - ADAM_B1) * gv
        nv = ADAM_B2 * v_ref[...] + (1.0 - ADAM_B2) * (gv * gv)
        m_hat = nm / (1.0 - ADAM_B1 ** ADAM_STEP)
        v_hat = nv / (1.0 - ADAM_B2 ** ADAM_STEP)
        d_ref[...] = -ADAM_LR * (m_hat / (jnp.sqrt(v_hat) + ADAM_EPS) + ADAM_WD * w_ref[...])
        nm_ref[...] = nm
        nv_ref[...] = nv

    spec = pl.BlockSpec((tr, tc), lambda i, j: (i, j))
    return pl.pallas_call(
        body, name=name, grid=(r // tr, cols // tc), in_specs=[spec] * 4, out_specs=[spec] * 3,
        out_shape=[jax.ShapeDtypeStruct((r, cols), F32)] * 3, compiler_params=_cparams(2),
    )(w, g, m, v)


ADAM_ROWS = {"w_ada": 512, "dw_w": 62, "pw2_w": 256, "sconv_w": 8, "w_proj_a": 512, "w_proj_b": 512, "w_proj_c": 512,
             "w_out": 256}
ADAM_W_IN_COLS = 256

WEIGHT_NAMES = ("w_ada", "b_ada", "norm_g", "w_in", "q_norm_g", "k_norm_g", "sinks", "dw_w", "dw_b", "ln_g", "ln_b",
                "pw2_w", "pw2_b", "sconv_w", "a_log", "dt_bias", "dn_norm_g", "w_proj_a", "w_proj_b", "w_proj_c", "w_out")


def kernel(x, c, w_ada, b_ada, norm_g, w_in, q_norm_g, k_norm_g, sinks, dw_w, dw_b, ln_g, ln_b, pw2_w, pw2_b, sconv_w, a_log, dt_bias, dn_norm_g, w_proj_a, w_proj_b, w_proj_c, w_out, loss_target, m_w_ada, m_b_ada, m_norm_g, m_w_in, m_q_norm_g, m_k_norm_g, m_sinks, m_dw_w, m_dw_b, m_ln_g, m_ln_b, m_pw2_w, m_pw2_b, m_sconv_w, m_a_log, m_dt_bias, m_dn_norm_g, m_w_proj_a, m_w_proj_b, m_w_proj_c, m_w_out, v_w_ada, v_b_ada, v_norm_g, v_w_in, v_q_norm_g, v_k_norm_g, v_sinks, v_dw_w, v_dw_b, v_ln_g, v_ln_b, v_pw2_w, v_pw2_b, v_sconv_w, v_a_log, v_dt_bias, v_dn_norm_g, v_w_proj_a, v_w_proj_b, v_w_proj_c, v_w_out):
    args = dict(locals())
    w = {n: args[n] for n in WEIGHT_NAMES}
    mom = {n: args["m_" + n] for n in WEIGHT_NAMES}
    var = {n: args["v_" + n] for n in WEIGHT_NAMES}

    chip = 2 * lax.axis_index("x") + lax.axis_index("y")
    slots = []
    for n in GATHERED:
        own = w[n] if n in GATHER_F32 else w[n].astype(BF16)
        slots.append(lax.dynamic_update_slice(lax.empty((N_CHIPS,) + own.shape, own.dtype), own[None], (chip, 0, 0, 0)))
    gathered = dict(zip(GATHERED, weights_allgather(slots)))
    layers = []
    for l in range(DEPTH):
        lw = {n: w[n][l] for n in SMALL}
        for n in GATHERED:
            lw[n] = gathered[n][:, l] if n == "w_in" else _join_layer(gathered[n][:, l], SHARDED[n])
        layers.append(prep_layer(lw))

    mod, conds = ada_fwd(jnp.tile(c, (8, 1)), w["w_ada"], w["b_ada"])
    act, saved = x[0], []
    for l in range(DEPTH):
        act, s = layer_fwd(str(l), act, mod[l:l + 1], layers[l])
        saved.append(s)
    dact, loss_part = loss_head("loss_head", act, loss_target[0], TM)
    loss = lax.psum(loss_part[0, 0], ("x", "y", "c"))
    layer_grads = [None] * DEPTH
    for l in reversed(range(DEPTH)):
        dact, layer_grads[l] = layer_bwd(str(l), dact, layers[l], saved[l])

    by_chip = [jnp.stack([layer_grads[l][n] if n == "w_in" else _split_layer(layer_grads[l][n], SHARDED[n])
                          for l in range(DEPTH)], axis=1) for n in REDUCE_BIG]
    final_grads = dict(zip(REDUCE_BIG, reduce_scatter_grads(REDUCE_BIG, by_chip)))
    small_names = SMALL + GATHER_F32
    small_shapes = {n: (DEPTH,) + layer_grads[0][n].shape for n in small_names}
    small_full = {n: jnp.stack([layer_grads[l][n] for l in range(DEPTH)]) for n in small_names}
    small_sum, small_all = small_allreduce(pack_small(small_full, small_names, SMALL_GRAD_ROWS))
    small_sum = unpack_small(small_sum, small_names, small_shapes)
    for n in GATHER_F32:
        width = w[n].shape[2]
        final_grads[n] = lax.dynamic_slice_in_dim(small_sum[n], chip * width, width, axis=2)
    n_mod = DEPTH * 3 * D_MODEL
    dmod = small_all.reshape(8, -1)[:, :n_mod].reshape(8, DEPTH, 3 * D_MODEL)
    width = w["w_ada"].shape[2]
    dmod = jnp.transpose(lax.dynamic_slice_in_dim(dmod, chip * width, width, axis=2), (1, 0, 2))
    final_grads["w_ada"] = ada_bwd(conds, dmod)
    final_grads.update({n: small_sum[n] for n in SMALL})
    small_grads = pack_small(final_grads, SMALL, SMALL_ROWS)

    delta, new_m, new_v = {}, {}, {}
    for n in SHARDED:
        shp = w[n].shape
        if n == "w_in":
            two_d = lambda a: jnp.transpose(a, (2, 0, 1)).reshape(shp[2], shp[0] * shp[1])
            back = lambda a: jnp.transpose(a.reshape(shp[2], shp[0], shp[1]), (1, 2, 0))
            g2 = two_d(final_grads[n])
            final_grads[n] = back(g2)
            d, nm, nv = adamw("adamw_" + n, two_d(w[n]), g2, two_d(mom[n]), two_d(var[n]), shp[2], ADAM_W_IN_COLS)
        else:
            two_d = lambda a, shp=shp: a.reshape(shp[0] * shp[1], shp[2])
            back = lambda a, shp=shp: a.reshape(shp)
            d, nm, nv = adamw("adamw_" + n, two_d(w[n]), two_d(final_grads[n]), two_d(mom[n]), two_d(var[n]), ADAM_ROWS[n])
        delta[n], new_m[n], new_v[n] = back(d), back(nm), back(nv)
    d, nm, nv = adamw("adamw_small", pack_small(w, SMALL, SMALL_ROWS), small_grads, pack_small(mom, SMALL, SMALL_ROWS),
                      pack_small(var, SMALL, SMALL_ROWS), SMALL_ROWS)
    delta.update(unpack_small(d, SMALL, small_shapes))
    new_m.update(unpack_small(nm, SMALL, small_shapes))
    new_v.update(unpack_small(nv, SMALL, small_shapes))

    return (loss, dact[None], *[final_grads[n] for n in WEIGHT_NAMES], *[delta[n] for n in WEIGHT_NAMES],
            *[new_m[n] for n in WEIGHT_NAMES], *[new_v[n] for n in WEIGHT_NAMES])
```

```python
import functools

import numpy as np
import jax
import jax.numpy as jnp
from jax import lax
from jax.experimental import pallas as pl
from jax.experimental.pallas import tpu as pltpu

F32 = jnp.float32
BF16 = jnp.bfloat16
MESH = pl.DeviceIdType.MESH

D_MODEL = 1024
DEPTH = 2
ATT_HEADS = 8
ATT_HEAD_DIM = 64
WINDOW = 128
CONV_K = 31
DN_HEADS = 4
DN_CONV_K = 4
DN_CHUNK = 64
EPS = 1e-6
NEG_INF = -1e30
N_CHIPS = 4
D_IN = 7944

ADAM_LR = 0.001
ADAM_B1 = 0.9
ADAM_B2 = 0.999
ADAM_EPS = 1e-08
ADAM_WD = 0.01
ADAM_STEP = 10

VMEM_LIMIT = 56 * 1024 * 1024

P_QA, P_ZA, P_GLU, P_ZB, P_ZC, P_MG, P_QKV, P_KA, P_VA, P_AB, P_TOTAL = (
    0, 512, 1024, 2048, 2560, 3072, 6144, 7680, 7808, 7936, 8064)
HEAD_ORDER = (0, 4, 1, 5, 2, 6, 3, 7)


def _in_pieces():
    p = [(0 + 64 * h, 64) for h in HEAD_ORDER]
    p += [(768 + 64 * h, 64) for h in HEAD_ORDER]
    for g in range(4):
        p += [(1280 + 128 * g, 128), (1792 + 128 * g, 128)]
    p += [(2304, 512), (4360, 512), (4872, 3072), (2816, 1536), (512, 128), (640, 128), (4352, 8)]
    return p


def _perm_heads_rows(w):
    return jnp.concatenate([w[64 * h:64 * h + 64] for h in HEAD_ORDER], axis=0)


def _unperm_heads_rows(w):
    inv = [HEAD_ORDER.index(h) for h in range(8)]
    return jnp.concatenate([w[64 * s:64 * s + 64] for s in inv], axis=0)


def _split_bf16(a, terms):
    out, rest = [], a.astype(F32)
    for _ in range(terms - 1):
        out.append(rest.astype(BF16))
        rest = rest - out[-1].astype(F32)
    return out + [rest.astype(BF16)]


def _dot(a, b, dims, exact):
    d = lambda p, q: lax.dot_general(p, q, (dims, ((), ())), preferred_element_type=F32)
    if exact:
        (ah, al), (bh, bl) = _split_bf16(a, 2), _split_bf16(b, 2)
        return d(ah, bh) + (d(ah, bl) + d(al, bh))
    return d(a.astype(BF16), b.astype(BF16))


def _make_mm(exact):
    @jax.custom_vjp
    def nn(a, b):
        return _dot(a, b, ((1,), (0,)), exact)

    @jax.custom_vjp
    def nt(a, b):
        return _dot(a, b, ((1,), (1,)), exact)

    @jax.custom_vjp
    def tn(a, b):
        return _dot(a, b, ((0,), (0,)), exact)

    nn.defvjp(lambda a, b: (nn(a, b), (a, b)),
              lambda r, g: (nt(g, r[1]).astype(r[0].dtype), tn(r[0], g).astype(r[1].dtype)))
    nt.defvjp(lambda a, b: (nt(a, b), (a, b)),
              lambda r, g: (nn(g, r[1]).astype(r[0].dtype), tn(g, r[0]).astype(r[1].dtype)))
    tn.defvjp(lambda a, b: (tn(a, b), (a, b)),
              lambda r, g: (nt(r[1], g).astype(r[0].dtype), nn(r[0], g).astype(r[1].dtype)))
    return nn, nt, tn


mm, mm_nt, mm_tn = _make_mm(False)
xmm, xmm_nt, xmm_tn = _make_mm(True)


@jax.custom_vjp
def sel_mm(m, g):
    mb = m.astype(BF16)
    parts = [jnp.dot(mb, p, preferred_element_type=F32) for p in _split_bf16(g, 3)]
    return parts[0] + (parts[1] + parts[2])


def _sel_mm_bwd(m, dy):
    mb = m.astype(BF16)
    parts = [lax.dot_general(mb, p, (((0,), (0,)), ((), ())), preferred_element_type=F32) for p in _split_bf16(dy, 3)]
    return jnp.zeros_like(m), parts[0] + (parts[1] + parts[2])


sel_mm.defvjp(lambda m, g: (sel_mm(m, g), m), _sel_mm_bwd)


@jax.custom_vjp
def tri_inv(*mats):
    n = mats[0].shape[0]
    eye = jnp.where(lax.broadcasted_iota(jnp.int32, (n, n), 0) == lax.broadcasted_iota(jnp.int32, (n, n), 1), 1.0, 0.0)
    ts = [eye - a for a in mats]
    pws = list(mats)
    for _ in range(5):
        pws = [xmm(pw, pw) for pw in pws]
        ts = [t + xmm(t, pw) for t, pw in zip(ts, pws)]
    return tuple(ts)


def _tri_inv_bwd(ts, dts):
    inner = [xmm_nt(dt, t) for t, dt in zip(ts, dts)]
    return tuple(-xmm_tn(t, m) for t, m in zip(ts, inner))


tri_inv.defvjp(lambda *mats: (tri_inv(*mats),) * 2, _tri_inv_bwd)


def _sigmoid(x):
    return 1.0 / (1.0 + jnp.exp(-x))


def _silu(x):
    return x * _sigmoid(x)


def _softplus(x):
    return jnp.maximum(x, 0.0) + jnp.log(1.0 + jnp.exp(-jnp.abs(x)))


def _cparams(n_grid):
    return pltpu.CompilerParams(dimension_semantics=("arbitrary",) * n_grid, vmem_limit_bytes=VMEM_LIMIT)


def _row_spec(tm, width, colblk):
    return pl.BlockSpec((tm, width), lambda i, cb=colblk: (i, cb))


def _const_spec(shape):
    nd = len(shape)
    return pl.BlockSpec(tuple(shape), lambda i, nd=nd: (0,) * nd)


def rowwise_fwd(name, f, rows, consts, outs, tm):
    n_r, n_c = len(rows), len(consts)
    t = rows[0][0].shape[0]

    def body(*refs):
        vals = [r[...] for r in refs[:n_r + n_c]]
        res = f(*vals)
        if not isinstance(res, (tuple, list)):
            res = (res,)
        for o_ref, v in zip(refs[n_r + n_c:], res):
            o_ref[...] = v.astype(o_ref.dtype)

    return pl.pallas_call(
        body, name=name, grid=(t // tm,),
        in_specs=[_row_spec(tm, w, cb) for _, w, cb in rows] + [_const_spec(c.shape) for c in consts],
        out_specs=[_row_spec(tm, w, 0) for w, _ in outs],
        out_shape=[jax.ShapeDtypeStruct((t, w), dt) for w, dt in outs],
        compiler_params=_cparams(1),
    )(*[a for a, _, _ in rows], *consts)


def rowwise_bwd(name, f, rows, consts, cts, row_grad_dtypes, tm):
    n_r, n_c, n_ct = len(rows), len(consts), len(cts)
    t = rows[0][0].shape[0]
    keep = [k for k, dt in enumerate(row_grad_dtypes) if dt is not None]

    def body(*refs):
        ins = [r[...].astype(F32) for r in refs[:n_r + n_c]]
        g_out = [r[...].astype(F32) for r in refs[n_r + n_c:n_r + n_c + n_ct]]
        out_refs = refs[n_r + n_c + n_ct:]

        def fw(*a):
            res = f(*a)
            return tuple(res) if isinstance(res, (tuple, list)) else (res,)

        _, vjp = jax.vjp(fw, *ins)
        grads = vjp(tuple(g_out))
        for o_ref, k in zip(out_refs[:len(keep)], keep):
            o_ref[...] = grads[k].astype(o_ref.dtype)
        first = pl.program_id(0) == 0
        for o_ref, g in zip(out_refs[len(keep):], grads[n_r:]):
            @pl.when(first)
            def _(o_ref=o_ref, g=g):
                o_ref[...] = g

            @pl.when(jnp.logical_not(first))
            def _(o_ref=o_ref, g=g):
                o_ref[...] += g

    return pl.pallas_call(
        body, name=name, grid=(t // tm,),
        in_specs=[_row_spec(tm, w, cb) for _, w, cb in rows] + [_const_spec(c.shape) for c in consts]
        + [_row_spec(tm, w, cb) for _, w, cb in cts],
        out_specs=[_row_spec(tm, rows[k][1], 0) for k in keep] + [_const_spec(c.shape) for c in consts],
        out_shape=[jax.ShapeDtypeStruct((t, rows[k][1]), row_grad_dtypes[k]) for k in keep]
        + [jax.ShapeDtypeStruct(c.shape, F32) for c in consts],
        compiler_params=_cparams(1),
    )(*[a for a, _, _ in rows], *consts, *[a for a, _, _ in cts])


def f_norm_mod(x, g, scale, shift):
    y = x * lax.rsqrt(jnp.mean(x * x, axis=-1, keepdims=True) + EPS) * g
    return y * (1.0 + scale) + shift


def f_conf_tail(u, zb, ln_g, ln_b, pw2_w, pw2_b):
    mu = jnp.mean(u, axis=-1, keepdims=True)
    xc = u - mu
    var = jnp.mean(xc * xc, axis=-1, keepdims=True)
    y = _silu(xc * lax.rsqrt(var + EPS) * ln_g + ln_b)
    return (mm(y, pw2_w) + pw2_b) * _silu(zb)


def f_merge(ya, yb, yc, mg, x, gate, wpa, wpb, wpc, wout):
    d = D_MODEL
    merged = (_sigmoid(mg[:, :d]) * mm(ya, wpa) + _sigmoid(mg[:, d:2 * d]) * mm(yb, wpb)
              + _sigmoid(mg[:, 2 * d:]) * mm(yc, wpc))
    return x + gate * mm(merged, wout)


def matmul_nn(name, a, b, out_dtype, tm, tn, tk):
    m, k = a.shape
    n = b.shape[1]
    nk = k // tk

    def body(a_ref, b_ref, o_ref, *acc):
        part = jnp.dot(a_ref[...].astype(BF16), b_ref[...].astype(BF16), preferred_element_type=F32)
        if nk == 1:
            o_ref[...] = part.astype(o_ref.dtype)
            return
        kk = pl.program_id(2)
        acc_ref = acc[0]

        @pl.when(kk == 0)
        def _():
            acc_ref[...] = part

        @pl.when(kk > 0)
        def _():
            acc_ref[...] += part

        @pl.when(kk == nk - 1)
        def _():
            o_ref[...] = acc_ref[...].astype(o_ref.dtype)

    return pl.pallas_call(
        body, name=name, grid=(m // tm, n // tn, nk),
        in_specs=[pl.BlockSpec((tm, tk), lambda i, j, kk: (i, kk)), pl.BlockSpec((tk, tn), lambda i, j, kk: (kk, j))],
        out_specs=pl.BlockSpec((tm, tn), lambda i, j, kk: (i, j)),
        out_shape=jax.ShapeDtypeStruct((m, n), out_dtype),
        scratch_shapes=[] if nk == 1 else [pltpu.VMEM((tm, tn), F32)],
        compiler_params=_cparams(3),
    )(a, b)


def ada_fwd(c8, w_shard, b_ada):
    n_cols = w_shard.shape[2]
    masks = [(m >> 2 & 1, m >> 1 & 1, m & 1) for m in range(1, 8)]

    def body(c_ref, w_ref, b_ref, mod_ref, conds_ref, cbuf, sendbuf, recvbuf, send_sems, recv_sems):
        x, y, c, chips = _place()
        flip = lambda v, bit: 1 - v if bit else v
        peers = [(flip(x, mx), flip(y, my), flip(c, mc)) for mx, my, mc in masks]
        dev = lambda p: 4 * p[0] + 2 * p[1] + p[2]
        cbuf[dev((x, y, c))] = c_ref[...]
        first = [_remote(c_ref, cbuf.at[dev((x, y, c))], send_sems.at[i], recv_sems.at[i], p) for i, p in enumerate(peers)]
        for cp in first:
            cp.start()
        for i, p in enumerate(peers):
            _remote(c_ref, cbuf.at[dev(p)], send_sems.at[i], recv_sems.at[i], p).wait_recv()
        conds = jnp.concatenate([cbuf[d, 0:1, :] for d in range(8)], axis=0)
        conds_ref[...] = conds
        act = _silu(conds)
        parts = [mm(act, w_ref[l]) for l in range(DEPTH)]
        row8 = lax.broadcasted_iota(jnp.int32, (8, 1), 0)

        def tile_for(chip):
            r = 2 * (2 * chip[0] + chip[1]) + c
            rows = [jnp.sum(jnp.where(row8 == r, parts[l], 0.0), axis=0, keepdims=True) for l in range(DEPTH)]
            return jnp.where(row8 == 0, rows[0], jnp.where(row8 == 1, rows[1], 0.0))

        my_slot = 2 * x + y
        recvbuf[my_slot] = tile_for((x, y))
        second = []
        for j, chip in enumerate(chips):
            sendbuf[j] = tile_for(chip)
            second.append(_remote(sendbuf.at[j], recvbuf.at[my_slot], send_sems.at[7 + j], recv_sems.at[7 + j], (*chip, c)))
            second[-1].start()
        for j, chip in enumerate(chips):
            _remote(sendbuf.at[j], recvbuf.at[2 * chip[0] + chip[1]], send_sems.at[7 + j], recv_sems.at[7 + j],
                    (*chip, c)).wait_recv()
        rows = [jnp.concatenate([recvbuf[k, l:l + 1, :] for k in range(N_CHIPS)], axis=1) + b_ref[l:l + 1, :]
                for l in range(DEPTH)]
        mod_ref[...] = jnp.concatenate(rows + [jnp.zeros((8 - DEPTH, N_CHIPS * n_cols), F32)], axis=0)
        for cp in first + second:
            cp.wait_send()

    vm = pl.BlockSpec(memory_space=pltpu.VMEM)
    return pl.pallas_call(
        body, name="ada_fwd",
        out_shape=[jax.ShapeDtypeStruct((8, N_CHIPS * n_cols), F32), jax.ShapeDtypeStruct((8, D_MODEL), F32)],
        in_specs=[vm, vm, vm], out_specs=[vm, vm],
        scratch_shapes=[pltpu.VMEM((8, 8, D_MODEL), F32), pltpu.VMEM((3, 8, n_cols), F32),
                        pltpu.VMEM((N_CHIPS, 8, n_cols), F32), pltpu.SemaphoreType.DMA((10,)), pltpu.SemaphoreType.DMA((10,))],
        compiler_params=pltpu.CompilerParams(vmem_limit_bytes=VMEM_LIMIT),
    )(c8, w_shard, b_ada)


def ada_bwd(conds, dmod):
    def body(c_ref, d_ref, o_ref):
        act = _silu(c_ref[...])
        for l in range(DEPTH):
            o_ref[l] = mm_tn(act, d_ref[l])

    return pl.pallas_call(
        body, name="ada_bwd", out_shape=jax.ShapeDtypeStruct((DEPTH, D_MODEL, dmod.shape[2]), F32),
        compiler_params=pltpu.CompilerParams(vmem_limit_bytes=VMEM_LIMIT),
    )(conds, dmod)


def _f_attn(first_block, q, za, kc, vc, kp, vp, qg, kg, sinks):
    w = WINDOW
    lane = lax.broadcasted_iota(jnp.int32, (1, 128), 1)
    halves = [lane < 64, lane >= 64]

    def rms_halves(x, g):
        x2 = x * x
        s0 = jnp.sum(jnp.where(halves[0], x2, 0.0), axis=-1, keepdims=True)
        s1 = jnp.sum(jnp.where(halves[1], x2, 0.0), axis=-1, keepdims=True)
        r = jnp.where(halves[0], lax.rsqrt(s0 / 64.0 + EPS), lax.rsqrt(s1 / 64.0 + EPS))
        return x * r * g

    kcat = rms_halves(jnp.concatenate([kp, kc], axis=0), kg)
    vcat = jnp.concatenate([vp, vc], axis=0)
    qi = lax.broadcasted_iota(jnp.int32, (w, 2 * w), 0)
    kj = lax.broadcasted_iota(jnp.int32, (w, 2 * w), 1)
    dist = qi + w - kj
    valid = (dist >= 0) & (dist < w) & (jnp.logical_not(first_block) | (kj >= w))
    distf = dist.astype(F32)
    units = [(grp, half) for grp in range(4) for half in range(2)]
    qns = [rms_halves(q[:, 128 * grp:128 * grp + 128], qg) * (ATT_HEAD_DIM ** -0.5) for grp in range(4)]
    vhalf = [jnp.where(halves[half], vcat, 0.0) for half in range(2)]
    scores, sinks_h = [], []
    for grp, half in units:
        head = HEAD_ORDER[2 * grp + half]
        slope = 2.0 ** (-8.0 * (head + 1) / ATT_HEADS)
        sinks_h.append(jnp.sum(jnp.where(lane == head, sinks, 0.0), axis=-1, keepdims=True))
        s = mm_nt(jnp.where(halves[half], qns[grp], 0.0), kcat) - slope * distf
        scores.append(jnp.where(valid, s, NEG_INF))
    probs = []
    for s, sink in zip(scores, sinks_h):
        m = lax.stop_gradient(jnp.maximum(jnp.max(s, axis=-1, keepdims=True), sink))
        p = jnp.exp(s - m)
        probs.append(p / (jnp.sum(p, axis=-1, keepdims=True) + jnp.exp(sink - m)))
    outs = [mm(p, vhalf[half]) for p, (grp, half) in zip(probs, units)]
    return jnp.concatenate([outs[2 * grp] + outs[2 * grp + 1] for grp in range(4)], axis=1) * _silu(za)


def attn_fwd(name, proj, qg, kg, sinks):
    t = proj.shape[0]
    nb = t // WINDOW

    def body(q_ref, za_ref, kc_ref, vc_ref, kp_ref, vp_ref, qg_ref, kg_ref, s_ref, o_ref):
        first = pl.program_id(0) == 0
        o_ref[...] = _f_attn(first, q_ref[...], za_ref[...], kc_ref[...], vc_ref[...], kp_ref[...], vp_ref[...],
                             qg_ref[...], kg_ref[...], s_ref[...])

    cur = lambda cb: (lambda i: (i, cb))
    prev = lambda cb: (lambda i: (jnp.maximum(i - 1, 0), cb))
    return pl.pallas_call(
        body, name=name, grid=(nb,),
        in_specs=[pl.BlockSpec((WINDOW, 512), cur(P_QA // 512)), pl.BlockSpec((WINDOW, 512), cur(P_ZA // 512)),
                  pl.BlockSpec((WINDOW, 128), cur(P_KA // 128)), pl.BlockSpec((WINDOW, 128), cur(P_VA // 128)),
                  pl.BlockSpec((WINDOW, 128), prev(P_KA // 128)), pl.BlockSpec((WINDOW, 128), prev(P_VA // 128)),
                  _const_spec((1, 128)), _const_spec((1, 128)), _const_spec((1, 128))],
        out_specs=pl.BlockSpec((WINDOW, 512), lambda i: (i, 0)),
        out_shape=jax.ShapeDtypeStruct((t, 512), F32),
        compiler_params=_cparams(1),
    )(proj, proj, proj, proj, proj, proj, qg, kg, sinks)


def attn_bwd(name, proj, qg, kg, sinks, dya):
    t = proj.shape[0]
    nb = t // WINDOW

    def body(q_ref, za_ref, kc_ref, vc_ref, kp_ref, vp_ref, qg_ref, kg_ref, s_ref, dy_ref,
             dqz_ref, dkv_ref, dqg_ref, dkg_ref, ds_ref, carry_ref):
        j = pl.program_id(0)
        first = j == nb - 1

        @pl.when(j == 0)
        def _():
            carry_ref[...] = jnp.zeros_like(carry_ref)
            dqg_ref[...] = jnp.zeros_like(dqg_ref)
            dkg_ref[...] = jnp.zeros_like(dkg_ref)
            ds_ref[...] = jnp.zeros_like(ds_ref)

        ins = [r[...] for r in (q_ref, za_ref, kc_ref, vc_ref, kp_ref, vp_ref, qg_ref, kg_ref, s_ref)]
        _, vjp = jax.vjp(functools.partial(_f_attn, first), *ins)
        dq, dza, dkc, dvc, dkp, dvp, dqg, dkg, dsk = vjp(dy_ref[...])
        dqz_ref[:, 0:512] = dq.astype(dqz_ref.dtype)
        dqz_ref[:, 512:1024] = dza.astype(dqz_ref.dtype)
        dkv_ref[:, 0:128] = (dkc + carry_ref[0]).astype(dkv_ref.dtype)
        dkv_ref[:, 128:256] = (dvc + carry_ref[1]).astype(dkv_ref.dtype)
        carry_ref[0] = dkp
        carry_ref[1] = dvp
        dqg_ref[...] += dqg
        dkg_ref[...] += dkg
        ds_ref[...] += dsk

    cur = lambda cb: (lambda j: (nb - 1 - j, cb))
    prev = lambda cb: (lambda j: (jnp.maximum(nb - 2 - j, 0), cb))
    return pl.pallas_call(
        body, name=name, grid=(nb,),
        in_specs=[pl.BlockSpec((WINDOW, 512), cur(P_QA // 512)), pl.BlockSpec((WINDOW, 512), cur(P_ZA // 512)),
                  pl.BlockSpec((WINDOW, 128), cur(P_KA // 128)), pl.BlockSpec((WINDOW, 128), cur(P_VA // 128)),
                  pl.BlockSpec((WINDOW, 128), prev(P_KA // 128)), pl.BlockSpec((WINDOW, 128), prev(P_VA // 128)),
                  _const_spec((1, 128)), _const_spec((1, 128)), _const_spec((1, 128)),
                  pl.BlockSpec((WINDOW, 512), cur(0))],
        out_specs=[pl.BlockSpec((WINDOW, 1024), cur(0)), pl.BlockSpec((WINDOW, 256), cur(0)),
                   _const_spec((1, 128)), _const_spec((1, 128)), _const_spec((1, 128))],
        out_shape=[jax.ShapeDtypeStruct((t, 1024), BF16), jax.ShapeDtypeStruct((t, 256), BF16),
                   jax.ShapeDtypeStruct((1, 128), F32), jax.ShapeDtypeStruct((1, 128), F32),
                   jax.ShapeDtypeStruct((1, 128), F32)],
        scratch_shapes=[pltpu.VMEM((2, WINDOW, 128), F32)],
        compiler_params=_cparams(1),
    )(proj, proj, proj, proj, proj, proj, qg, kg, sinks, dya)


CONV_ROWS = 256


def _conv_taps(src_ref, w_ref, n_taps, base, t):
    for r0 in range(0, t, CONV_ROWS):
        acc = w_ref[0:1, :] * src_ref[pl.ds(r0 + base, CONV_ROWS), :]
        for k in range(1, n_taps):
            acc = acc + w_ref[k:k + 1, :] * src_ref[pl.ds(r0 + base + k, CONV_ROWS), :]
        yield r0, acc


def _conv_wgrad(dy_ref, src_ref, n_taps, base, t, dy_base=0):
    out = []
    for k in range(n_taps):
        acc = jnp.zeros((8, 128), F32)
        for r0 in range(0, t, CONV_ROWS):
            prod = dy_ref[pl.ds(r0 + dy_base, CONV_ROWS), :] * src_ref[pl.ds(r0 + base + k, CONV_ROWS), :]
            acc = acc + jnp.sum(prod.reshape(CONV_ROWS // 8, 8, 128), axis=0)
        out.append(jnp.sum(acc, axis=0, keepdims=True))
    return out


def glu_conv_fwd(name, proj, w32, bias):
    t = proj.shape[0]
    pad = 32

    def body(x_ref, w_ref, b_ref, o_ref, u_ref):
        u_ref[0:pad, :] = jnp.zeros((pad, 128), F32)
        u_ref[pad:pad + t, :] = x_ref[:, 0:128] * _sigmoid(x_ref[:, 128:256])
        for r0, acc in _conv_taps(u_ref, w_ref, CONV_K, pad - (CONV_K - 1), t):
            o_ref[pl.ds(r0, CONV_ROWS), :] = acc + b_ref[...]

    return pl.pallas_call(
        body, name=name, grid=(4,),
        in_specs=[pl.BlockSpec((t, 256), lambda cb: (0, P_GLU // 256 + cb)), pl.BlockSpec((32, 128), lambda cb: (0, cb)),
                  pl.BlockSpec((1, 128), lambda cb: (0, cb))],
        out_specs=pl.BlockSpec((t, 128), lambda cb: (0, cb)),
        out_shape=jax.ShapeDtypeStruct((t, 512), F32),
        scratch_shapes=[pltpu.VMEM((t + pad, 128), F32)],
        compiler_params=_cparams(1),
    )(proj, w32, bias)


def glu_conv_bwd(name, proj, w32, dub):
    t = proj.shape[0]
    pad = 32
    k1 = CONV_K - 1

    def body(x_ref, w_ref, dy_ref, dx_ref, dw_ref, db_ref, u_ref, dyp_ref, wrev_ref):
        val = x_ref[:, 0:128]
        sg = _sigmoid(x_ref[:, 128:256])
        u_ref[0:pad, :] = jnp.zeros((pad, 128), F32)
        u_ref[pad:pad + t, :] = val * sg
        dyp_ref[0:t, :] = dy_ref[...]
        dyp_ref[t:t + pad, :] = jnp.zeros((pad, 128), F32)
        for k in range(CONV_K):
            wrev_ref[k:k + 1, :] = w_ref[k1 - k:k1 - k + 1, :]
        wrev_ref[CONV_K:32, :] = jnp.zeros((32 - CONV_K, 128), F32)
        for r0, du in _conv_taps(dyp_ref, wrev_ref, CONV_K, 0, t):
            v = x_ref[pl.ds(r0, CONV_ROWS), 0:128]
            s = _sigmoid(x_ref[pl.ds(r0, CONV_ROWS), 128:256])
            dx_ref[pl.ds(r0, CONV_ROWS), 0:128] = (du * s).astype(dx_ref.dtype)
            dx_ref[pl.ds(r0, CONV_ROWS), 128:256] = (du * v * s * (1.0 - s)).astype(dx_ref.dtype)
        dws = _conv_wgrad(dyp_ref, u_ref, CONV_K, pad - k1, t)
        for k in range(CONV_K):
            dw_ref[k:k + 1, :] = dws[k]
        dw_ref[CONV_K:32, :] = jnp.zeros((32 - CONV_K, 128), F32)
        db_ref[...] = jnp.sum(dy_ref[...], axis=0, keepdims=True)

    return pl.pallas_call(
        body, name=name, grid=(4,),
        in_specs=[pl.BlockSpec((t, 256), lambda cb: (0, P_GLU // 256 + cb)), pl.BlockSpec((32, 128), lambda cb: (0, cb)),
                  pl.BlockSpec((t, 128), lambda cb: (0, cb))],
        out_specs=[pl.BlockSpec((t, 256), lambda cb: (0, cb)), pl.BlockSpec((32, 128), lambda cb: (0, cb)),
                   pl.BlockSpec((1, 128), lambda cb: (0, cb))],
        out_shape=[jax.ShapeDtypeStruct((t, 1024), BF16), jax.ShapeDtypeStruct((32, 512), F32),
                   jax.ShapeDtypeStruct((1, 512), F32)],
        scratch_shapes=[pltpu.VMEM((t + pad, 128), F32), pltpu.VMEM((t + pad, 128), F32), pltpu.VMEM((32, 128), F32)],
        compiler_params=_cparams(1),
    )(proj, w32, dub)


def sconv_fwd(name, proj, w8):
    t = proj.shape[0]
    pad = 8
    k1 = DN_CONV_K - 1

    def body(x_ref, w_ref, o_ref, xp_ref):
        xp_ref[0:pad, :] = jnp.zeros((pad, 128), F32)
        xp_ref[pad:pad + t, :] = x_ref[...]
        for r0, acc in _conv_taps(xp_ref, w_ref, DN_CONV_K, pad - k1, t):
            o_ref[pl.ds(r0, CONV_ROWS), :] = _silu(acc)

    return pl.pallas_call(
        body, name=name, grid=(12,),
        in_specs=[pl.BlockSpec((t, 128), lambda cb: (0, P_QKV // 128 + cb)), pl.BlockSpec((8, 128), lambda cb: (0, cb))],
        out_specs=pl.BlockSpec((t, 128), lambda cb: (0, cb)),
        out_shape=jax.ShapeDtypeStruct((t, 1536), F32),
        scratch_shapes=[pltpu.VMEM((t + pad, 128), F32)],
        compiler_params=_cparams(1),
    )(proj, w8)


def sconv_bwd(name, proj, w8, dqkv):
    t = proj.shape[0]
    pad = 8
    k1 = DN_CONV_K - 1

    def body(x_ref, w_ref, dy_ref, dx_ref, dw_ref, xp_ref, dpp_ref, wrev_ref):
        xp_ref[0:pad, :] = jnp.zeros((pad, 128), F32)
        xp_ref[pad:pad + t, :] = x_ref[...]
        for r0, pre in _conv_taps(xp_ref, w_ref, DN_CONV_K, pad - k1, t):
            s = _sigmoid(pre)
            dpp_ref[pl.ds(r0, CONV_ROWS), :] = dy_ref[pl.ds(r0, CONV_ROWS), :] * (s * (1.0 + pre * (1.0 - s)))
        dpp_ref[t:t + pad, :] = jnp.zeros((pad, 128), F32)
        for k in range(DN_CONV_K):
            wrev_ref[k:k + 1, :] = w_ref[k1 - k:k1 - k + 1, :]
        wrev_ref[DN_CONV_K:8, :] = jnp.zeros((8 - DN_CONV_K, 128), F32)
        for r0, dx in _conv_taps(dpp_ref, wrev_ref, DN_CONV_K, 0, t):
            dx_ref[pl.ds(r0, CONV_ROWS), :] = dx.astype(dx_ref.dtype)
        dws = _conv_wgrad(dpp_ref, xp_ref, DN_CONV_K, pad - k1, t)
        for k in range(DN_CONV_K):
            dw_ref[k:k + 1, :] = dws[k]
        dw_ref[DN_CONV_K:8, :] = jnp.zeros((8 - DN_CONV_K, 128), F32)

    return pl.pallas_call(
        body, name=name, grid=(12,),
        in_specs=[pl.BlockSpec((t, 128), lambda cb: (0, P_QKV // 128 + cb)), pl.BlockSpec((8, 128), lambda cb: (0, cb)),
                  pl.BlockSpec((t, 128), lambda cb: (0, cb))],
        out_specs=[pl.BlockSpec((t, 128), lambda cb: (0, cb)), pl.BlockSpec((8, 128), lambda cb: (0, cb))],
        out_shape=[jax.ShapeDtypeStruct((t, 1536), BF16), jax.ShapeDtypeStruct((8, 1536), F32)],
        scratch_shapes=[pltpu.VMEM((t + pad, 128), F32), pltpu.VMEM((t + pad, 128), F32), pltpu.VMEM((8, 128), F32)],
        compiler_params=_cparams(1),
    )(proj, w8, dqkv)


def _f_delta_step(qkv, ab, zc, s0, s1, s2, s3, a_log, dt_bias, dn_g):
    cs = DN_CHUNK
    n = 2 * cs
    states = (s0, s1, s2, s3)
    lane = lax.broadcasted_iota(jnp.int32, (1, 128), 1)
    ri = lax.broadcasted_iota(jnp.int32, (n, n), 0)
    ci = lax.broadcasted_iota(jnp.int32, (n, n), 1)
    same = (ri // cs) == (ci // cs)
    lower = same & (ri >= ci)
    strict = same & (ri > ci)
    sums = jnp.concatenate([jnp.where(lower, 1.0, 0.0), jnp.where(same, 1.0, 0.0), jnp.where(ci < cs, 1.0, 0.0),
                            jnp.where(ci >= cs, 1.0, 0.0)], axis=0)
    top = lax.broadcasted_iota(jnp.int32, (n, 1), 0) < cs

    def pick(row, idx):
        return jnp.sum(jnp.where(lane == idx, row, 0.0), axis=-1, keepdims=True)

    def l2n(x):
        return x * lax.rsqrt(jnp.sum(x * x, axis=-1, keepdims=True) + EPS)

    n_chunks = qkv.shape[0] // cs
    units = [(k, pair) for k in range(n_chunks) for pair in range(2)]

    pre = []
    for k, pair in units:
        hs = (2 * pair, 2 * pair + 1)
        rows = slice(k * cs, (k + 1) * cs)
        stack = lambda f: jnp.concatenate([f(hs[0]), f(hs[1])], axis=0)
        qd = l2n(stack(lambda h: qkv[rows, 128 * h:128 * h + 128])) * (128 ** -0.5)
        kd = l2n(stack(lambda h: qkv[rows, 512 + 128 * h:512 + 128 * h + 128]))
        vd = stack(lambda h: qkv[rows, 1024 + 128 * h:1024 + 128 * h + 128])
        beta = _sigmoid(stack(lambda h: pick(ab[rows], 4 + h)))
        g = stack(lambda h: -jnp.exp(pick(a_log, h)) * _softplus(pick(ab[rows], h) + pick(dt_bias, h)))
        g_sums = sel_mm(sums, g * jnp.ones((1, n), F32))
        gc_col = g_sums[0:n]
        gl_b = g_sums[n:2 * n]
        g_end = (g_sums[2 * n:3 * n], g_sums[3 * n:])
        decay = jnp.where(lower, jnp.exp(jnp.where(lower, gc_col - gc_col.T, 0.0)), 0.0)
        kb = kd * beta
        pre.append(dict(qd=qd, kd=kd, vb=vd * beta, kb=kb, gc_col=gc_col, gl_b=gl_b, g_end=g_end, decay=decay,
                        a=jnp.where(strict, mm_nt(kb, kd) * decay, 0.0)))
    tmats = tri_inv(*[p["a"] for p in pre])

    mid = []
    for p, tmat in zip(pre, tmats):
        egc = jnp.exp(p["gc_col"])
        mid.append(dict(u=mm(tmat, p["vb"]), wm=mm(tmat, p["kb"] * egc), qe=p["qd"] * egc,
                        intra=jnp.where(lower, mm_nt(p["qd"], p["kd"]) * p["decay"], 0.0),
                        ke=p["kd"] * jnp.exp(p["gl_b"] - p["gc_col"]), g_end=p["g_end"]))

    ys = []
    for k in range(n_chunks):
        rows = slice(k * cs, (k + 1) * cs)
        new_states, y_heads = [], []
        for pair in range(2):
            m = mid[2 * k + pair]
            hs = (2 * pair, 2 * pair + 1)
            st = (states[hs[0]], states[hs[1]])
            v_new = m["u"] - jnp.concatenate([mm(m["wm"][:cs], st[0]), mm(m["wm"][cs:], st[1])], axis=0)
            o = jnp.concatenate([mm(m["qe"][:cs], st[0]), mm(m["qe"][cs:], st[1])], axis=0) + mm(m["intra"], v_new)
            new_states.append(st[0] * jnp.exp(m["g_end"][0]) + mm_tn(jnp.where(top, m["ke"], 0.0), v_new))
            new_states.append(st[1] * jnp.exp(m["g_end"][1]) + mm_tn(jnp.where(top, 0.0, m["ke"]), v_new))
            od = o * lax.rsqrt(jnp.mean(o * o, axis=-1, keepdims=True) + EPS) * dn_g
            y_heads += [od[:cs] * _silu(zc[rows, 128 * hs[0]:128 * hs[0] + 128]),
                        od[cs:] * _silu(zc[rows, 128 * hs[1]:128 * hs[1] + 128])]
        states = tuple(new_states)
        ys.append(jnp.concatenate(y_heads, axis=1))
    return (jnp.concatenate(ys, axis=0), *states)


DELTA_ROWS = 4 * DN_CHUNK


def delta_fwd(name, qkv, proj, a_log, dt_bias, dn_g):
    t = qkv.shape[0]
    nc = t // DELTA_ROWS

    def body(qkv_ref, ab_ref, zc_ref, al_ref, dt_ref, g_ref, y_ref, ssave_ref, s_ref):
        @pl.when(pl.program_id(0) == 0)
        def _():
            s_ref[...] = jnp.zeros_like(s_ref)

        ssave_ref[0] = s_ref[...]
        st = [s_ref[128 * h:128 * h + 128, :] for h in range(4)]
        y, *ns = _f_delta_step(qkv_ref[...], ab_ref[...], zc_ref[...], *st, al_ref[...], dt_ref[...], g_ref[...])
        y_ref[...] = y
        for h in range(4):
            s_ref[128 * h:128 * h + 128, :] = ns[h]

    return pl.pallas_call(
        body, name=name, grid=(nc,),
        in_specs=[pl.BlockSpec((DELTA_ROWS, 1536), lambda i: (i, 0)), pl.BlockSpec((DELTA_ROWS, 128), lambda i: (i, P_AB // 128)),
                  pl.BlockSpec((DELTA_ROWS, 512), lambda i: (i, P_ZC // 512)),
                  _const_spec((1, 128)), _const_spec((1, 128)), _const_spec((1, 128))],
        out_specs=[pl.BlockSpec((DELTA_ROWS, 512), lambda i: (i, 0)), pl.BlockSpec((1, 512, 128), lambda i: (i, 0, 0))],
        out_shape=[jax.ShapeDtypeStruct((t, 512), F32), jax.ShapeDtypeStruct((nc, 512, 128), F32)],
        scratch_shapes=[pltpu.VMEM((512, 128), F32)],
        compiler_params=_cparams(1),
    )(qkv, proj, proj, a_log, dt_bias, dn_g)


def delta_bwd(name, qkv, proj, ssave, a_log, dt_bias, dn_g, dyc):
    t = qkv.shape[0]
    nc = t // DELTA_ROWS

    def body(qkv_ref, ab_ref, zc_ref, ss_ref, al_ref, dt_ref, g_ref, dy_ref,
             dqkv_ref, dab_ref, dzc_ref, dal_ref, ddt_ref, dg_ref, ds_ref):
        @pl.when(pl.program_id(0) == 0)
        def _():
            ds_ref[...] = jnp.zeros_like(ds_ref)
            dal_ref[...] = jnp.zeros_like(dal_ref)
            ddt_ref[...] = jnp.zeros_like(ddt_ref)
            dg_ref[...] = jnp.zeros_like(dg_ref)

        st = [ss_ref[0, 128 * h:128 * h + 128, :] for h in range(4)]
        _, vjp = jax.vjp(_f_delta_step, qkv_ref[...], ab_ref[...], zc_ref[...], *st, al_ref[...], dt_ref[...], g_ref[...])
        dst = tuple(ds_ref[128 * h:128 * h + 128, :] for h in range(4))
        dqkv, dab, dzc, d0, d1, d2, d3, dal, ddt, dg = vjp((dy_ref[...], *dst))
        dqkv_ref[...] = dqkv
        dab_ref[...] = dab.astype(dab_ref.dtype)
        dzc_ref[...] = dzc.astype(dzc_ref.dtype)
        for h, d in enumerate((d0, d1, d2, d3)):
            ds_ref[128 * h:128 * h + 128, :] = d
        dal_ref[...] += dal
        ddt_ref[...] += ddt
        dg_ref[...] += dg

    rev = lambda cb: (lambda j: (nc - 1 - j, cb))
    return pl.pallas_call(
        body, name=name, grid=(nc,),
        in_specs=[pl.BlockSpec((DELTA_ROWS, 1536), rev(0)), pl.BlockSpec((DELTA_ROWS, 128), rev(P_AB // 128)),
                  pl.BlockSpec((DELTA_ROWS, 512), rev(P_ZC // 512)), pl.BlockSpec((1, 512, 128), lambda j: (nc - 1 - j, 0, 0)),
                  _const_spec((1, 128)), _const_spec((1, 128)), _const_spec((1, 128)),
                  pl.BlockSpec((DELTA_ROWS, 512), rev(0))],
        out_specs=[pl.BlockSpec((DELTA_ROWS, 1536), rev(0)), pl.BlockSpec((DELTA_ROWS, 128), rev(0)),
                   pl.BlockSpec((DELTA_ROWS, 512), rev(0)),
                   _const_spec((1, 128)), _const_spec((1, 128)), _const_spec((1, 128))],
        out_shape=[jax.ShapeDtypeStruct((t, 1536), F32), jax.ShapeDtypeStruct((t, 128), BF16),
                   jax.ShapeDtypeStruct((t, 512), BF16),
                   jax.ShapeDtypeStruct((1, 128), F32), jax.ShapeDtypeStruct((1, 128), F32), jax.ShapeDtypeStruct((1, 128), F32)],
        scratch_shapes=[pltpu.VMEM((512, 128), F32)],
        compiler_params=_cparams(1),
    )(qkv, proj, proj, ssave, a_log, dt_bias, dn_g, dyc)


def loss_head(name, y, target, tm):
    t, d = y.shape

    def body(y_ref, t_ref, dy_ref, l_ref):
        err = y_ref[...] - t_ref[...]
        dy_ref[...] = err * (1.0 / d)
        part = 0.5 * jnp.sum(jnp.sum(err * err, axis=-1, keepdims=True) * (1.0 / d), axis=0, keepdims=True)

        @pl.when(pl.program_id(0) == 0)
        def _():
            l_ref[...] = part

        @pl.when(pl.program_id(0) > 0)
        def _():
            l_ref[...] += part

    return pl.pallas_call(
        body, name=name, grid=(t // tm,),
        in_specs=[_row_spec(tm, d, 0), _row_spec(tm, d, 0)],
        out_specs=[_row_spec(tm, d, 0), _const_spec((1, 1))],
        out_shape=[jax.ShapeDtypeStruct((t, d), F32), jax.ShapeDtypeStruct((1, 1), F32)],
        compiler_params=_cparams(1),
    )(y, target)


TM = 512
TM_MERGE = 256
TM_IN = 1024
TN_IN = 1152


def _lane_pad(v, n=128):
    return jnp.pad(v.astype(F32), (0, n - v.shape[0]))[None, :]


def f_norm_mod_res(x, g, scale, shift):
    return f_norm_mod(x, g, scale, shift), x


def prep_layer(w):
    p = dict(w)
    p["wp"] = _pad_w_in_from_shards(w["w_in"])
    p["wpt"] = p["wp"].T
    p["wpa"] = _perm_heads_rows(w["w_proj_a"])
    p["dw32"] = jnp.pad(w["dw_w"], ((0, 32 - CONV_K), (0, 0)))
    p["sconv8"] = jnp.pad(w["sconv_w"], ((0, 8 - DN_CONV_K), (0, 0)))
    p["qg"] = jnp.tile(w["q_norm_g"], 2)[None, :]
    p["kg"] = jnp.tile(w["k_norm_g"], 2)[None, :]
    p["sinks128"] = _lane_pad(w["sinks"])
    p["al"] = _lane_pad(w["a_log"])
    p["dtb"] = _lane_pad(w["dt_bias"])
    p["dng"] = w["dn_norm_g"][None, :]
    return p


def layer_fwd(tag, x, mod, p):
    d = D_MODEL
    shift, scale, gate = mod[:, :d], mod[:, d:2 * d], mod[:, 2 * d:]
    g = p["norm_g"][None, :]
    (h,) = rowwise_fwd(f"norm_fwd{tag}", f_norm_mod, [(x, d, 0)], [g, scale, shift], [(d, BF16)], TM)
    proj = matmul_nn(f"inproj_fwd{tag}", h, p["wp"], F32, TM_IN, TN_IN, d)
    ya = attn_fwd(f"attn_fwd{tag}", proj, p["qg"], p["kg"], p["sinks128"])
    ub = glu_conv_fwd(f"glu_conv_fwd{tag}", proj, p["dw32"], p["dw_b"][None, :])
    conf_consts = [p["ln_g"][None, :], p["ln_b"][None, :], p["pw2_w"], p["pw2_b"][None, :]]
    (yb,) = rowwise_fwd(f"conf_fwd{tag}", f_conf_tail, [(ub, 512, 0), (proj, 512, P_ZB // 512)], conf_consts, [(512, F32)], TM)
    qkv = sconv_fwd(f"sconv_fwd{tag}", proj, p["sconv8"])
    yc, ssave = delta_fwd(f"delta_fwd{tag}", qkv, proj, p["al"], p["dtb"], p["dng"])
    merge_consts = [gate, p["wpa"], p["w_proj_b"], p["w_proj_c"], p["w_out"]]
    merge_rows = [(ya, 512, 0), (yb, 512, 0), (yc, 512, 0), (proj, 3 * d, P_MG // (3 * d)), (x, d, 0)]
    (xn,) = rowwise_fwd(f"merge_fwd{tag}", f_merge, merge_rows, merge_consts, [(d, F32)], TM_MERGE)
    saved = dict(x=x, h=h, proj=proj, ub=ub, qkv=qkv, ssave=ssave, norm_consts=[g, scale, shift],
                 conf_consts=conf_consts, merge_consts=merge_consts, merge_rows=merge_rows)
    return xn, saved


def layer_bwd(tag, dxn, p, s):
    d = D_MODEL
    proj = s["proj"]
    dya, dyb, dyc, dmg, dgate, dwpa, dwpb, dwpc, dwout = rowwise_bwd(
        f"merge_bwd{tag}", f_merge, s["merge_rows"], s["merge_consts"], [(dxn, d, 0)], [F32, F32, F32, BF16, None], TM_MERGE)
    dqz, dkv, dqg, dkg, dsinks = attn_bwd(f"attn_bwd{tag}", proj, p["qg"], p["kg"], p["sinks128"], dya)
    dub, dzb, dln_g, dln_b, dpw2_w, dpw2_b = rowwise_bwd(
        f"conf_bwd{tag}", f_conf_tail, [(s["ub"], 512, 0), (proj, 512, P_ZB // 512)], s["conf_consts"], [(dyb, 512, 0)],
        [F32, BF16], TM)
    dglu, ddw32, ddw_b = glu_conv_bwd(f"glu_conv_bwd{tag}", proj, p["dw32"], dub)
    dqkv, dab, dzc, dal, ddtb, ddng = delta_bwd(f"delta_bwd{tag}", s["qkv"], proj, s["ssave"], p["al"], p["dtb"], p["dng"], dyc)
    dqkv_pre, dsconv8 = sconv_bwd(f"sconv_bwd{tag}", proj, p["sconv8"], dqkv)
    dproj = jnp.concatenate([dqz, dglu, dzb, dzc, dmg, dqkv_pre, dkv, dab], axis=1)
    dh = matmul_nn(f"inproj_bwd_dh{tag}", dproj, p["wpt"], F32, TM_IN, d, P_TOTAL // 3)
    dwp = matmul_nn(f"inproj_bwd_dw{tag}", s["h"].T, dproj, F32, d, TN_IN, 2048)
    dx, dnorm_g, dscale, dshift = rowwise_bwd(
        f"norm_bwd{tag}", f_norm_mod_res, [(s["x"], d, 0)], s["norm_consts"], [(dh, d, 0), (dxn, d, 0)], [F32], TM)
    dmod = jnp.concatenate([dshift, dscale, dgate], axis=1)
    grads = dict(
        b_ada=dmod[0], norm_g=dnorm_g[0], w_in=_unpad_w_in_to_shards(dwp),
        q_norm_g=dqg[0, :64] + dqg[0, 64:], k_norm_g=dkg[0, :64] + dkg[0, 64:], sinks=dsinks[0, :ATT_HEADS],
        dw_w=ddw32[:CONV_K], dw_b=ddw_b[0], ln_g=dln_g[0], ln_b=dln_b[0], pw2_w=dpw2_w, pw2_b=dpw2_b[0],
        sconv_w=dsconv8[:DN_CONV_K], a_log=dal[0, :DN_HEADS], dt_bias=ddtb[0, :DN_HEADS], dn_norm_g=ddng[0],
        w_proj_a=_unperm_heads_rows(dwpa), w_proj_b=dwpb, w_proj_c=dwpc, w_out=dwout)
    return dx, grads


SHARDED = {"w_ada": 2, "w_in": 2, "dw_w": 2, "pw2_w": 1, "sconv_w": 2, "w_proj_a": 2, "w_proj_b": 2, "w_proj_c": 2,
           "w_out": 1}
GATHERED = tuple(n for n in SHARDED if n != "w_ada")
GATHER_F32 = ("dw_w", "sconv_w")
REDUCE_BIG = tuple(n for n in GATHERED if n not in GATHER_F32)
SMALL = ("b_ada", "norm_g", "q_norm_g", "k_norm_g", "sinks", "dw_b", "ln_g", "ln_b", "pw2_b", "a_log", "dt_bias",
         "dn_norm_g")
SMALL_ROWS = 104
SMALL_GRAD_ROWS = 448
W_IN_SHARD = D_IN // N_CHIPS
SUM_TILE = 256


def _shard_cols(shards, start, n):
    parts = []
    while n > 0:
        k, o = divmod(start, W_IN_SHARD)
        m = min(n, W_IN_SHARD - o)
        parts.append(shards[k][:, o:o + m])
        start, n = start + m, n - m
    return parts


def _pad_w_in_from_shards(shards):
    parts = []
    for s, n in _in_pieces():
        parts += _shard_cols(shards, s, n)
    parts.append(jnp.zeros((shards.shape[1], P_TOTAL - D_IN), shards.dtype))
    return jnp.concatenate(parts, axis=1)


def _unpad_w_in_to_shards(wp):
    pieces = _in_pieces()
    starts = np.cumsum([0] + [n for _, n in pieces])[:-1]
    order = sorted(range(len(pieces)), key=lambda i: pieces[i][0])
    shards = []
    for k in range(N_CHIPS):
        lo, hi = k * W_IN_SHARD, (k + 1) * W_IN_SHARD
        parts = []
        for i in order:
            s, n = pieces[i]
            a, b = max(s, lo), min(s + n, hi)
            if a < b:
                parts.append(wp[:, int(starts[i]) + a - s:int(starts[i]) + b - s])
        shards.append(jnp.concatenate(parts, axis=1))
    return jnp.stack(shards)


def _join_layer(v, axis):
    if axis == 2:
        return jnp.transpose(v, (1, 0, 2)).reshape(v.shape[1], N_CHIPS * v.shape[2])
    return v.reshape(N_CHIPS * v.shape[1], v.shape[2])


def _split_layer(v, axis):
    a, b = v.shape
    if axis == 2:
        return jnp.transpose(v.reshape(a, N_CHIPS, b // N_CHIPS), (1, 0, 2))
    return v.reshape(N_CHIPS, a // N_CHIPS, b)


def pack_small(vals, names, rows):
    flat = jnp.concatenate([vals[n].astype(F32).reshape(-1) for n in names])
    return jnp.pad(flat, (0, rows * 128 - flat.shape[0])).reshape(rows, 128)


def unpack_small(packed, names, shapes):
    flat = packed.reshape(-1)
    out, off = {}, 0
    for n in names:
        k = int(np.prod(shapes[n]))
        out[n] = flat[off:off + k].reshape(shapes[n])
        off += k
    return out


ANY = pl.BlockSpec(memory_space=pl.ANY)


def _place():
    x, y, c = lax.axis_index("x"), lax.axis_index("y"), lax.axis_index("c")
    chips = [(1 - x, y), (x, 1 - y), (1 - x, 1 - y)]
    return x, y, c, chips


def _remote(src, dst, send_sem, recv_sem, to):
    return pltpu.make_async_remote_copy(src_ref=src, dst_ref=dst, send_sem=send_sem, recv_sem=recv_sem, device_id=to,
                                        device_id_type=MESH)


def weights_allgather(slots):
    n = len(slots)

    def body(*refs):
        out = refs[n:2 * n]
        send_sems, recv_sems = refs[2 * n:]
        x, y, c, chips = _place()
        me, sibling, my_slot = (x, y, c), (x, y, 1 - c), 2 * x + y
        sends = []
        for j, chip in enumerate(chips):
            for t in range(n):
                mine = out[t].at[my_slot, c]
                sends.append(_remote(mine, mine, send_sems.at[t, j], recv_sems.at[t, j], (*chip, c)))
                sends[-1].start()
        for j, chip in enumerate(chips):
            for t in range(n):
                land = out[t].at[2 * chip[0] + chip[1], c]
                _remote(land, land, send_sems.at[t, j], recv_sems.at[t, j], me).wait_recv()
                sends.append(_remote(land, land, send_sems.at[t, 3 + j], recv_sems.at[t, 3 + j], sibling))
                sends[-1].start()
        for j, chip in enumerate(chips):
            for t in range(n):
                land = out[t].at[2 * chip[0] + chip[1], 1 - c]
                _remote(land, land, send_sems.at[t, 3 + j], recv_sems.at[t, 3 + j], me).wait_recv()
        for cp in sends:
            cp.wait_send()

    return pl.pallas_call(
        body, name="weights_allgather", out_shape=[jax.ShapeDtypeStruct(s.shape, s.dtype) for s in slots],
        in_specs=[ANY] * n, out_specs=[ANY] * n, input_output_aliases={t: t for t in range(n)},
        scratch_shapes=[pltpu.SemaphoreType.DMA((n, 6)), pltpu.SemaphoreType.DMA((n, 6))],
    )(*slots)


def grads_pair_exchange(gs):
    n = len(gs)

    def body(*refs):
        g, recv = refs[:n], refs[n:2 * n]
        send_sems, recv_sems = refs[2 * n:]
        x, y, c, _ = _place()
        cps = [_remote(g[t].at[:, 1 - c], recv[t], send_sems.at[t], recv_sems.at[t], (x, y, 1 - c)) for t in range(n)]
        for cp in cps:
            cp.start()
        for cp in cps:
            cp.wait()

    return pl.pallas_call(
        body, name="grads_pair_exchange",
        out_shape=[jax.ShapeDtypeStruct((N_CHIPS,) + g.shape[2:], g.dtype) for g in gs],
        in_specs=[ANY] * n, out_specs=[ANY] * n,
        scratch_shapes=[pltpu.SemaphoreType.DMA((n,)), pltpu.SemaphoreType.DMA((n,))],
    )(*gs)


def grads_pair_sum(name, g, recv):
    _, a, b = recv.shape
    ta = min(a, SUM_TILE)

    def body(a_ref, b_ref, o_ref):
        o_ref[...] = (a_ref[...] + b_ref[...]).astype(o_ref.dtype)

    return pl.pallas_call(
        body, name=name, grid=(N_CHIPS, a // ta),
        in_specs=[pl.BlockSpec((None, None, ta, b), lambda s, i: (s, lax.axis_index("c"), i, 0)),
                  pl.BlockSpec((None, ta, b), lambda s, i: (s, i, 0))],
        out_specs=pl.BlockSpec((None, ta, b), lambda s, i: (s, i, 0)),
        out_shape=jax.ShapeDtypeStruct(recv.shape, BF16),
        compiler_params=_cparams(2),
    )(g, recv)


def grads_chip_exchange(ps):
    n = len(ps)

    def body(*refs):
        p, recv = refs[:n], refs[n:2 * n]
        send_sems, recv_sems = refs[2 * n:]
        x, y, c, chips = _place()
        cps = [_remote(p[t].at[2 * chip[0] + chip[1]], recv[t].at[j], send_sems.at[t, j], recv_sems.at[t, j], (*chip, c))
               for j, chip in enumerate(chips) for t in range(n)]
        for cp in cps:
            cp.start()
        for cp in cps:
            cp.wait()

    return pl.pallas_call(
        body, name="grads_chip_exchange", out_shape=[jax.ShapeDtypeStruct((3,) + p.shape[1:], p.dtype) for p in ps],
        in_specs=[ANY] * n, out_specs=[ANY] * n,
        scratch_shapes=[pltpu.SemaphoreType.DMA((n, 3)), pltpu.SemaphoreType.DMA((n, 3))],
    )(*ps)


def grads_chip_sum(name, g, recv, recv2):
    _, a, b = recv.shape
    ta = min(a, SUM_TILE)
    my_slot = lambda: 2 * lax.axis_index("x") + lax.axis_index("y")

    def body(g_ref, r_ref, r2_ref, o_ref):
        own = g_ref[...] + r_ref[...]
        o_ref[...] = ((own + r2_ref[0].astype(F32)) + r2_ref[1].astype(F32)) + r2_ref[2].astype(F32)

    return pl.pallas_call(
        body, name=name, grid=(a // ta,),
        in_specs=[pl.BlockSpec((None, None, ta, b), lambda i: (my_slot(), lax.axis_index("c"), i, 0)),
                  pl.BlockSpec((None, ta, b), lambda i: (my_slot(), i, 0)),
                  pl.BlockSpec((3, ta, b), lambda i: (0, i, 0))],
        out_specs=pl.BlockSpec((None, ta, b), lambda i: (lax.axis_index("c"), i, 0)),
        out_shape=jax.ShapeDtypeStruct((DEPTH, a, b), F32),
        compiler_params=_cparams(1),
    )(g, recv, recv2)


def grads_pair_gather(reds):
    n = len(reds)

    def body(*refs):
        buf = refs[n:2 * n]
        send_sems, recv_sems = refs[2 * n:]
        x, y, c, _ = _place()
        sibling = (x, y, 1 - c)
        cps = [_remote(buf[t].at[c], buf[t].at[c], send_sems.at[t], recv_sems.at[t], sibling) for t in range(n)]
        for cp in cps:
            cp.start()
        for t in range(n):
            _remote(buf[t].at[c], buf[t].at[1 - c], send_sems.at[t], recv_sems.at[t], sibling).wait_recv()
        for cp in cps:
            cp.wait_send()

    return pl.pallas_call(
        body, name="grads_pair_gather", out_shape=[jax.ShapeDtypeStruct(r.shape, r.dtype) for r in reds],
        in_specs=[ANY] * n, out_specs=[ANY] * n, input_output_aliases={t: t for t in range(n)},
        scratch_shapes=[pltpu.SemaphoreType.DMA((n,)), pltpu.SemaphoreType.DMA((n,))],
    )(*reds)


def small_allreduce(v):
    m, n = v.shape

    def body(x_ref, sum_ref, all_ref, send_sems, recv_sems, local_sem):
        x, y, c, chips = _place()
        me, sibling = (x, y, c), (x, y, 1 - c)

        def rows(px, py, pc):
            return all_ref.at[pl.ds((4 * px + 2 * py + pc) * m, m), :]

        def copy(k, block, to, src=None):
            return pltpu.make_async_remote_copy(src_ref=rows(*block) if src is None else src, dst_ref=rows(*block),
                                                send_sem=send_sems.at[k], recv_sem=recv_sems.at[k],
                                                device_id=to, device_id_type=MESH)

        mine = pltpu.make_async_copy(x_ref, rows(*me), local_sem)
        mine.start()
        first = [copy(0, me, sibling, src=x_ref)]
        first += [copy(1 + j, me, (*chip, c), src=x_ref) for j, chip in enumerate(chips)]
        for cp in first:
            cp.start()
        passed = [copy(4 + j, (*chip, c), sibling) for j, chip in enumerate(chips)]
        for j, chip in enumerate(chips):
            copy(1 + j, (*chip, c), me).wait_recv()
            passed[j].start()
        copy(0, sibling, me).wait_recv()
        for j, chip in enumerate(chips):
            copy(4 + j, (*chip, 1 - c), me).wait_recv()
        for cp in first + passed:
            cp.wait_send()
        mine.wait()
        acc = all_ref[0:m, :]
        for dev in range(1, 8):
            acc = acc + all_ref[dev * m:(dev + 1) * m, :]
        sum_ref[...] = acc

    vm = pl.BlockSpec(memory_space=pltpu.VMEM)
    return pl.pallas_call(
        body, name="small_allreduce",
        out_shape=[jax.ShapeDtypeStruct((m, n), F32), jax.ShapeDtypeStruct((8 * m, n), F32)],
        in_specs=[vm], out_specs=[vm, vm],
        scratch_shapes=[pltpu.SemaphoreType.DMA((7,)), pltpu.SemaphoreType.DMA((7,)), pltpu.SemaphoreType.DMA],
    )(v)


def reduce_scatter_grads(names, gs):
    recv = grads_pair_exchange(gs)
    parts = [grads_pair_sum("grads_pair_sum_" + n, g, r) for n, g, r in zip(names, gs, recv)]
    recv2 = grads_chip_exchange(parts)
    reds = [grads_chip_sum("grads_chip_sum_" + n, g, r, r2) for n, g, r, r2 in zip(names, gs, recv, recv2)]
    return grads_pair_gather(reds)


def adamw(name, w, g, m, v, block):
    grid = tuple(s // b for s, b in zip(w.shape, block))

    def body(w_ref, g_ref, m_ref, v_ref, d_ref, nm_ref, nv_ref):
        gv = g_ref[...]
        nm = ADAM_B1 * m_ref[...] + (1.0 - ADAM_B1) * gv
        nv = ADAM_B2 * v_ref[...] + (1.0 - ADAM_B2) * (gv * gv)
        m_hat = nm / (1.0 - ADAM_B1 ** ADAM_STEP)
        v_hat = nv / (1.0 - ADAM_B2 ** ADAM_STEP)
        d_ref[...] = -ADAM_LR * (m_hat / (jnp.sqrt(v_hat) + ADAM_EPS) + ADAM_WD * w_ref[...])
        nm_ref[...] = nm
        nv_ref[...] = nv

    spec = pl.BlockSpec(tuple(block), lambda *idx: idx)
    return pl.pallas_call(
        body, name=name, grid=grid, in_specs=[spec] * 4, out_specs=[spec] * 3,
        out_shape=[jax.ShapeDtypeStruct(w.shape, F32)] * 3, compiler_params=_cparams(len(grid)),
    )(w, g, m, v)


ADAM_ROWS = {"w_ada": 512, "dw_w": 62, "pw2_w": 256, "sconv_w": 8, "w_proj_a": 512, "w_proj_b": 512, "w_proj_c": 512,
             "w_out": 256}
ADAM_W_IN_COLS = 331

WEIGHT_NAMES = ("w_ada", "b_ada", "norm_g", "w_in", "q_norm_g", "k_norm_g", "sinks", "dw_w", "dw_b", "ln_g", "ln_b",
                "pw2_w", "pw2_b", "sconv_w", "a_log", "dt_bias", "dn_norm_g", "w_proj_a", "w_proj_b", "w_proj_c", "w_out")


def kernel(x, c, w_ada, b_ada, norm_g, w_in, q_norm_g, k_norm_g, sinks, dw_w, dw_b, ln_g, ln_b, pw2_w, pw2_b, sconv_w, a_log, dt_bias, dn_norm_g, w_proj_a, w_proj_b, w_proj_c, w_out, loss_target, m_w_ada, m_b_ada, m_norm_g, m_w_in, m_q_norm_g, m_k_norm_g, m_sinks, m_dw_w, m_dw_b, m_ln_g, m_ln_b, m_pw2_w, m_pw2_b, m_sconv_w, m_a_log, m_dt_bias, m_dn_norm_g, m_w_proj_a, m_w_proj_b, m_w_proj_c, m_w_out, v_w_ada, v_b_ada, v_norm_g, v_w_in, v_q_norm_g, v_k_norm_g, v_sinks, v_dw_w, v_dw_b, v_ln_g, v_ln_b, v_pw2_w, v_pw2_b, v_sconv_w, v_a_log, v_dt_bias, v_dn_norm_g, v_w_proj_a, v_w_proj_b, v_w_proj_c, v_w_out):
    args = dict(locals())
    w = {n: args[n] for n in WEIGHT_NAMES}
    mom = {n: args["m_" + n] for n in WEIGHT_NAMES}
    var = {n: args["v_" + n] for n in WEIGHT_NAMES}

    chip = 2 * lax.axis_index("x") + lax.axis_index("y")
    slots = []
    for n in GATHERED:
        own = w[n] if n in GATHER_F32 else w[n].astype(BF16)
        slots.append(lax.dynamic_update_slice(lax.empty((N_CHIPS,) + own.shape, own.dtype), own[None], (chip, 0, 0, 0)))
    gathered = dict(zip(GATHERED, weights_allgather(slots)))
    layers = []
    for l in range(DEPTH):
        lw = {n: w[n][l] for n in SMALL}
        for n in GATHERED:
            lw[n] = gathered[n][:, l] if n == "w_in" else _join_layer(gathered[n][:, l], SHARDED[n])
        layers.append(prep_layer(lw))

    mod, conds = ada_fwd(jnp.tile(c, (8, 1)), w["w_ada"], w["b_ada"])
    act, saved = x[0], []
    for l in range(DEPTH):
        act, s = layer_fwd(str(l), act, mod[l:l + 1], layers[l])
        saved.append(s)
    dact, loss_part = loss_head("loss_head", act, loss_target[0], TM)
    loss = lax.psum(loss_part[0, 0], ("x", "y", "c"))
    layer_grads = [None] * DEPTH
    for l in reversed(range(DEPTH)):
        dact, layer_grads[l] = layer_bwd(str(l), dact, layers[l], saved[l])

    by_chip = [jnp.stack([layer_grads[l][n] if n == "w_in" else _split_layer(layer_grads[l][n], SHARDED[n])
                          for l in range(DEPTH)], axis=1) for n in REDUCE_BIG]
    final_grads = dict(zip(REDUCE_BIG, reduce_scatter_grads(REDUCE_BIG, by_chip)))
    small_names = SMALL + GATHER_F32
    small_shapes = {n: (DEPTH,) + layer_grads[0][n].shape for n in small_names}
    small_full = {n: jnp.stack([layer_grads[l][n] for l in range(DEPTH)]) for n in small_names}
    small_sum, small_all = small_allreduce(pack_small(small_full, small_names, SMALL_GRAD_ROWS))
    small_sum = unpack_small(small_sum, small_names, small_shapes)
    for n in GATHER_F32:
        width = w[n].shape[2]
        final_grads[n] = lax.dynamic_slice_in_dim(small_sum[n], chip * width, width, axis=2)
    n_mod = DEPTH * 3 * D_MODEL
    dmod = small_all.reshape(8, -1)[:, :n_mod].reshape(8, DEPTH, 3 * D_MODEL)
    width = w["w_ada"].shape[2]
    dmod = jnp.transpose(lax.dynamic_slice_in_dim(dmod, chip * width, width, axis=2), (1, 0, 2))
    final_grads["w_ada"] = ada_bwd(conds, dmod)
    final_grads.update({n: small_sum[n] for n in SMALL})
    small_grads = pack_small(final_grads, SMALL, SMALL_ROWS)

    delta, new_m, new_v = {}, {}, {}
    for n in SHARDED:
        shp = w[n].shape
        if n == "w_in":
            view = lambda a: jnp.transpose(a, (2, 0, 1))
            back = lambda a: jnp.transpose(a, (1, 2, 0))
            g3 = view(final_grads[n])
            final_grads[n] = back(g3)
            d, nm, nv = adamw("adamw_" + n, view(w[n]), g3, view(mom[n]), view(var[n]), (ADAM_W_IN_COLS, shp[0], shp[1]))
        else:
            view = lambda a, shp=shp: a.reshape(shp[0] * shp[1], shp[2])
            back = lambda a, shp=shp: a.reshape(shp)
            d, nm, nv = adamw("adamw_" + n, view(w[n]), view(final_grads[n]), view(mom[n]), view(var[n]),
                              (ADAM_ROWS[n], shp[2]))
        delta[n], new_m[n], new_v[n] = back(d), back(nm), back(nv)
    d, nm, nv = adamw("adamw_small", pack_small(w, SMALL, SMALL_ROWS), small_grads, pack_small(mom, SMALL, SMALL_ROWS),
                      pack_small(var, SMALL, SMALL_ROWS), (SMALL_ROWS, 128))
    delta.update(unpack_small(d, SMALL, small_shapes))
    new_m.update(unpack_small(nm, SMALL, small_shapes))
    new_v.update(unpack_small(nv, SMALL, small_shapes))

    return (loss, dact[None], *[final_grads[n] for n in WEIGHT_NAMES], *[delta[n] for n in WEIGHT_NAMES],
            *[new_m[n] for n in WEIGHT_NAMES], *[new_v[n] for n in WEIGHT_NAMES])
```

```python
import functools

import numpy as np
import jax
import jax.numpy as jnp
from jax import lax
from jax.experimental import pallas as pl
from jax.experimental.pallas import tpu as pltpu

F32 = jnp.float32
BF16 = jnp.bfloat16
MESH = pl.DeviceIdType.MESH

D_MODEL = 1024
DEPTH = 2
ATT_HEADS = 8
ATT_HEAD_DIM = 64
WINDOW = 128
CONV_K = 31
DN_HEADS = 4
DN_CONV_K = 4
DN_CHUNK = 64
EPS = 1e-6
NEG_INF = -1e30
N_CHIPS = 4
D_IN = 7944

ADAM_LR = 0.001
ADAM_B1 = 0.9
ADAM_B2 = 0.999
ADAM_EPS = 1e-08
ADAM_WD = 0.01
ADAM_STEP = 10

VMEM_LIMIT = 56 * 1024 * 1024

P_QA, P_ZA, P_GLU, P_ZB, P_ZC, P_MG, P_QKV, P_KA, P_VA, P_AB, P_TOTAL = (
    0, 512, 1024, 2048, 2560, 3072, 6144, 7680, 7808, 7936, 8064)
HEAD_ORDER = (0, 4, 1, 5, 2, 6, 3, 7)


def _in_pieces():
    p = [(0 + 64 * h, 64) for h in HEAD_ORDER]
    p += [(768 + 64 * h, 64) for h in HEAD_ORDER]
    for g in range(4):
        p += [(1280 + 128 * g, 128), (1792 + 128 * g, 128)]
    p += [(2304, 512), (4360, 512), (4872, 3072), (2816, 1536), (512, 128), (640, 128), (4352, 8)]
    return p


def _perm_heads_rows(w):
    return jnp.concatenate([w[64 * h:64 * h + 64] for h in HEAD_ORDER], axis=0)


def _unperm_heads_rows(w):
    inv = [HEAD_ORDER.index(h) for h in range(8)]
    return jnp.concatenate([w[64 * s:64 * s + 64] for s in inv], axis=0)


def _split_bf16(a, terms):
    out, rest = [], a.astype(F32)
    for _ in range(terms - 1):
        out.append(rest.astype(BF16))
        rest = rest - out[-1].astype(F32)
    return out + [rest.astype(BF16)]


def _dot(a, b, dims, exact):
    d = lambda p, q: lax.dot_general(p, q, (dims, ((), ())), preferred_element_type=F32)
    if exact:
        (ah, al), (bh, bl) = _split_bf16(a, 2), _split_bf16(b, 2)
        return d(ah, bh) + (d(ah, bl) + d(al, bh))
    return d(a.astype(BF16), b.astype(BF16))


def _make_mm(exact):
    @jax.custom_vjp
    def nn(a, b):
        return _dot(a, b, ((1,), (0,)), exact)

    @jax.custom_vjp
    def nt(a, b):
        return _dot(a, b, ((1,), (1,)), exact)

    @jax.custom_vjp
    def tn(a, b):
        return _dot(a, b, ((0,), (0,)), exact)

    nn.defvjp(lambda a, b: (nn(a, b), (a, b)),
              lambda r, g: (nt(g, r[1]).astype(r[0].dtype), tn(r[0], g).astype(r[1].dtype)))
    nt.defvjp(lambda a, b: (nt(a, b), (a, b)),
              lambda r, g: (nn(g, r[1]).astype(r[0].dtype), tn(g, r[0]).astype(r[1].dtype)))
    tn.defvjp(lambda a, b: (tn(a, b), (a, b)),
              lambda r, g: (nt(r[1], g).astype(r[0].dtype), nn(r[0], g).astype(r[1].dtype)))
    return nn, nt, tn


mm, mm_nt, mm_tn = _make_mm(False)
xmm, xmm_nt, xmm_tn = _make_mm(True)


@jax.custom_vjp
def sel_mm(m, g):
    mb = m.astype(BF16)
    parts = [jnp.dot(mb, p, preferred_element_type=F32) for p in _split_bf16(g, 3)]
    return parts[0] + (parts[1] + parts[2])


def _sel_mm_bwd(m, dy):
    mb = m.astype(BF16)
    parts = [lax.dot_general(mb, p, (((0,), (0,)), ((), ())), preferred_element_type=F32) for p in _split_bf16(dy, 3)]
    return jnp.zeros_like(m), parts[0] + (parts[1] + parts[2])


sel_mm.defvjp(lambda m, g: (sel_mm(m, g), m), _sel_mm_bwd)


@jax.custom_vjp
def tri_inv(*mats):
    n = mats[0].shape[0]
    eye = jnp.where(lax.broadcasted_iota(jnp.int32, (n, n), 0) == lax.broadcasted_iota(jnp.int32, (n, n), 1), 1.0, 0.0)
    ts = [eye - a for a in mats]
    pws = list(mats)
    for _ in range(5):
        pws = [xmm(pw, pw) for pw in pws]
        ts = [t + xmm(t, pw) for t, pw in zip(ts, pws)]
    return tuple(ts)


def _tri_inv_bwd(ts, dts):
    inner = [xmm_nt(dt, t) for t, dt in zip(ts, dts)]
    return tuple(-xmm_tn(t, m) for t, m in zip(ts, inner))


tri_inv.defvjp(lambda *mats: (tri_inv(*mats),) * 2, _tri_inv_bwd)


def _sigmoid(x):
    return 1.0 / (1.0 + jnp.exp(-x))


def _silu(x):
    return x * _sigmoid(x)


def _softplus(x):
    return jnp.maximum(x, 0.0) + jnp.log(1.0 + jnp.exp(-jnp.abs(x)))


def _cparams(n_grid):
    return pltpu.CompilerParams(dimension_semantics=("arbitrary",) * n_grid, vmem_limit_bytes=VMEM_LIMIT)


def _row_spec(tm, width, colblk):
    return pl.BlockSpec((tm, width), lambda i, cb=colblk: (i, cb))


def _const_spec(shape):
    nd = len(shape)
    return pl.BlockSpec(tuple(shape), lambda i, nd=nd: (0,) * nd)


def rowwise_fwd(name, f, rows, consts, outs, tm):
    n_r, n_c = len(rows), len(consts)
    t = rows[0][0].shape[0]

    def body(*refs):
        vals = [r[...] for r in refs[:n_r + n_c]]
        res = f(*vals)
        if not isinstance(res, (tuple, list)):
            res = (res,)
        for o_ref, v in zip(refs[n_r + n_c:], res):
            o_ref[...] = v.astype(o_ref.dtype)

    return pl.pallas_call(
        body, name=name, grid=(t // tm,),
        in_specs=[_row_spec(tm, w, cb) for _, w, cb in rows] + [_const_spec(c.shape) for c in consts],
        out_specs=[_row_spec(tm, w, 0) for w, _ in outs],
        out_shape=[jax.ShapeDtypeStruct((t, w), dt) for w, dt in outs],
        compiler_params=_cparams(1),
    )(*[a for a, _, _ in rows], *consts)


def rowwise_bwd(name, f, rows, consts, cts, row_grad_dtypes, tm):
    n_r, n_c, n_ct = len(rows), len(consts), len(cts)
    t = rows[0][0].shape[0]
    keep = [k for k, dt in enumerate(row_grad_dtypes) if dt is not None]

    def body(*refs):
        ins = [r[...].astype(F32) for r in refs[:n_r + n_c]]
        g_out = [r[...].astype(F32) for r in refs[n_r + n_c:n_r + n_c + n_ct]]
        out_refs = refs[n_r + n_c + n_ct:]

        def fw(*a):
            res = f(*a)
            return tuple(res) if isinstance(res, (tuple, list)) else (res,)

        _, vjp = jax.vjp(fw, *ins)
        grads = vjp(tuple(g_out))
        for o_ref, k in zip(out_refs[:len(keep)], keep):
            o_ref[...] = grads[k].astype(o_ref.dtype)
        first = pl.program_id(0) == 0
        for o_ref, g in zip(out_refs[len(keep):], grads[n_r:]):
            @pl.when(first)
            def _(o_ref=o_ref, g=g):
                o_ref[...] = g

            @pl.when(jnp.logical_not(first))
            def _(o_ref=o_ref, g=g):
                o_ref[...] += g

    return pl.pallas_call(
        body, name=name, grid=(t // tm,),
        in_specs=[_row_spec(tm, w, cb) for _, w, cb in rows] + [_const_spec(c.shape) for c in consts]
        + [_row_spec(tm, w, cb) for _, w, cb in cts],
        out_specs=[_row_spec(tm, rows[k][1], 0) for k in keep] + [_const_spec(c.shape) for c in consts],
        out_shape=[jax.ShapeDtypeStruct((t, rows[k][1]), row_grad_dtypes[k]) for k in keep]
        + [jax.ShapeDtypeStruct(c.shape, F32) for c in consts],
        compiler_params=_cparams(1),
    )(*[a for a, _, _ in rows], *consts, *[a for a, _, _ in cts])


def f_norm_mod(x, g, scale, shift):
    y = x * lax.rsqrt(jnp.mean(x * x, axis=-1, keepdims=True) + EPS) * g
    return y * (1.0 + scale) + shift


def f_conf_tail(u, zb, ln_g, ln_b, pw2_w, pw2_b):
    mu = jnp.mean(u, axis=-1, keepdims=True)
    xc = u - mu
    var = jnp.mean(xc * xc, axis=-1, keepdims=True)
    y = _silu(xc * lax.rsqrt(var + EPS) * ln_g + ln_b)
    return (mm(y, pw2_w) + pw2_b) * _silu(zb)


def f_merge(ya, yb, yc, mg, x, gate, wpa, wpb, wpc, wout):
    d = D_MODEL
    merged = (_sigmoid(mg[:, :d]) * mm(ya, wpa) + _sigmoid(mg[:, d:2 * d]) * mm(yb, wpb)
              + _sigmoid(mg[:, 2 * d:]) * mm(yc, wpc))
    return x + gate * mm(merged, wout)


def matmul_nn(name, a, b, out_dtype, tm, tn, tk):
    m, k = a.shape
    n = b.shape[1]
    nk = k // tk

    def body(a_ref, b_ref, o_ref, *acc):
        part = jnp.dot(a_ref[...].astype(BF16), b_ref[...].astype(BF16), preferred_element_type=F32)
        if nk == 1:
            o_ref[...] = part.astype(o_ref.dtype)
            return
        kk = pl.program_id(2)
        acc_ref = acc[0]

        @pl.when(kk == 0)
        def _():
            acc_ref[...] = part

        @pl.when(kk > 0)
        def _():
            acc_ref[...] += part

        @pl.when(kk == nk - 1)
        def _():
            o_ref[...] = acc_ref[...].astype(o_ref.dtype)

    return pl.pallas_call(
        body, name=name, grid=(m // tm, n // tn, nk),
        in_specs=[pl.BlockSpec((tm, tk), lambda i, j, kk: (i, kk)), pl.BlockSpec((tk, tn), lambda i, j, kk: (kk, j))],
        out_specs=pl.BlockSpec((tm, tn), lambda i, j, kk: (i, j)),
        out_shape=jax.ShapeDtypeStruct((m, n), out_dtype),
        scratch_shapes=[] if nk == 1 else [pltpu.VMEM((tm, tn), F32)],
        compiler_params=_cparams(3),
    )(a, b)


def ada_fwd(c8, w_shard, b_ada):
    n_cols = w_shard.shape[2]
    masks = [(m >> 2 & 1, m >> 1 & 1, m & 1) for m in range(1, 8)]

    def body(c_ref, w_ref, b_ref, mod_ref, conds_ref, cbuf, sendbuf, recvbuf, send_sems, recv_sems):
        x, y, c, chips = _place()
        flip = lambda v, bit: 1 - v if bit else v
        peers = [(flip(x, mx), flip(y, my), flip(c, mc)) for mx, my, mc in masks]
        dev = lambda p: 4 * p[0] + 2 * p[1] + p[2]
        cbuf[dev((x, y, c))] = c_ref[...]
        first = [_remote(c_ref, cbuf.at[dev((x, y, c))], send_sems.at[i], recv_sems.at[i], p) for i, p in enumerate(peers)]
        for cp in first:
            cp.start()
        for i, p in enumerate(peers):
            _remote(c_ref, cbuf.at[dev(p)], send_sems.at[i], recv_sems.at[i], p).wait_recv()
        conds = jnp.concatenate([cbuf[d, 0:1, :] for d in range(8)], axis=0)
        conds_ref[...] = conds
        act = _silu(conds)
        parts = [mm(act, w_ref[l]) for l in range(DEPTH)]
        row8 = lax.broadcasted_iota(jnp.int32, (8, 1), 0)

        def tile_for(chip):
            r = 2 * (2 * chip[0] + chip[1]) + c
            rows = [jnp.sum(jnp.where(row8 == r, parts[l], 0.0), axis=0, keepdims=True) for l in range(DEPTH)]
            return jnp.where(row8 == 0, rows[0], jnp.where(row8 == 1, rows[1], 0.0))

        my_slot = 2 * x + y
        recvbuf[my_slot] = tile_for((x, y))
        second = []
        for j, chip in enumerate(chips):
            sendbuf[j] = tile_for(chip)
            second.append(_remote(sendbuf.at[j], recvbuf.at[my_slot], send_sems.at[7 + j], recv_sems.at[7 + j], (*chip, c)))
            second[-1].start()
        for j, chip in enumerate(chips):
            _remote(sendbuf.at[j], recvbuf.at[2 * chip[0] + chip[1]], send_sems.at[7 + j], recv_sems.at[7 + j],
                    (*chip, c)).wait_recv()
        rows = [jnp.concatenate([recvbuf[k, l:l + 1, :] for k in range(N_CHIPS)], axis=1) + b_ref[l:l + 1, :]
                for l in range(DEPTH)]
        mod_ref[...] = jnp.concatenate(rows + [jnp.zeros((8 - DEPTH, N_CHIPS * n_cols), F32)], axis=0)
        for cp in first + second:
            cp.wait_send()

    vm = pl.BlockSpec(memory_space=pltpu.VMEM)
    return pl.pallas_call(
        body, name="ada_fwd",
        out_shape=[jax.ShapeDtypeStruct((8, N_CHIPS * n_cols), F32), jax.ShapeDtypeStruct((8, D_MODEL), F32)],
        in_specs=[vm, vm, vm], out_specs=[vm, vm],
        scratch_shapes=[pltpu.VMEM((8, 8, D_MODEL), F32), pltpu.VMEM((3, 8, n_cols), F32),
                        pltpu.VMEM((N_CHIPS, 8, n_cols), F32), pltpu.SemaphoreType.DMA((10,)), pltpu.SemaphoreType.DMA((10,))],
        compiler_params=pltpu.CompilerParams(vmem_limit_bytes=VMEM_LIMIT),
    )(c8, w_shard, b_ada)


def ada_bwd(conds, dmod):
    def body(c_ref, d_ref, o_ref):
        act = _silu(c_ref[...])
        for l in range(DEPTH):
            o_ref[l] = mm_tn(act, d_ref[l])

    return pl.pallas_call(
        body, name="ada_bwd", out_shape=jax.ShapeDtypeStruct((DEPTH, D_MODEL, dmod.shape[2]), F32),
        compiler_params=pltpu.CompilerParams(vmem_limit_bytes=VMEM_LIMIT),
    )(conds, dmod)


def _f_attn(first_block, q, za, kc, vc, kp, vp, qg, kg, sinks):
    w = WINDOW
    lane = lax.broadcasted_iota(jnp.int32, (1, 128), 1)
    halves = [lane < 64, lane >= 64]

    def rms_halves(x, g):
        x2 = x * x
        s0 = jnp.sum(jnp.where(halves[0], x2, 0.0), axis=-1, keepdims=True)
        s1 = jnp.sum(jnp.where(halves[1], x2, 0.0), axis=-1, keepdims=True)
        r = jnp.where(halves[0], lax.rsqrt(s0 / 64.0 + EPS), lax.rsqrt(s1 / 64.0 + EPS))
        return x * r * g

    kcat = rms_halves(jnp.concatenate([kp, kc], axis=0), kg)
    vcat = jnp.concatenate([vp, vc], axis=0)
    qi = lax.broadcasted_iota(jnp.int32, (w, 2 * w), 0)
    kj = lax.broadcasted_iota(jnp.int32, (w, 2 * w), 1)
    dist = qi + w - kj
    valid = (dist >= 0) & (dist < w) & (jnp.logical_not(first_block) | (kj >= w))
    distf = dist.astype(F32)
    units = [(grp, half) for grp in range(4) for half in range(2)]
    qns = [rms_halves(q[:, 128 * grp:128 * grp + 128], qg) * (ATT_HEAD_DIM ** -0.5) for grp in range(4)]
    vhalf = [jnp.where(halves[half], vcat, 0.0) for half in range(2)]
    scores, sinks_h = [], []
    for grp, half in units:
        head = HEAD_ORDER[2 * grp + half]
        slope = 2.0 ** (-8.0 * (head + 1) / ATT_HEADS)
        sinks_h.append(jnp.sum(jnp.where(lane == head, sinks, 0.0), axis=-1, keepdims=True))
        s = mm_nt(jnp.where(halves[half], qns[grp], 0.0), kcat) - slope * distf
        scores.append(jnp.where(valid, s, NEG_INF))
    probs = []
    for s, sink in zip(scores, sinks_h):
        m = lax.stop_gradient(jnp.maximum(jnp.max(s, axis=-1, keepdims=True), sink))
        p = jnp.exp(s - m)
        probs.append(p / (jnp.sum(p, axis=-1, keepdims=True) + jnp.exp(sink - m)))
    outs = [mm(p, vhalf[half]) for p, (grp, half) in zip(probs, units)]
    return jnp.concatenate([outs[2 * grp] + outs[2 * grp + 1] for grp in range(4)], axis=1) * _silu(za)


def attn_fwd(name, proj, qg, kg, sinks):
    t = proj.shape[0]
    nb = t // WINDOW

    def body(q_ref, za_ref, kc_ref, vc_ref, kp_ref, vp_ref, qg_ref, kg_ref, s_ref, o_ref):
        first = pl.program_id(0) == 0
        o_ref[...] = _f_attn(first, q_ref[...], za_ref[...], kc_ref[...], vc_ref[...], kp_ref[...], vp_ref[...],
                             qg_ref[...], kg_ref[...], s_ref[...])

    cur = lambda cb: (lambda i: (i, cb))
    prev = lambda cb: (lambda i: (jnp.maximum(i - 1, 0), cb))
    return pl.pallas_call(
        body, name=name, grid=(nb,),
        in_specs=[pl.BlockSpec((WINDOW, 512), cur(P_QA // 512)), pl.BlockSpec((WINDOW, 512), cur(P_ZA // 512)),
                  pl.BlockSpec((WINDOW, 128), cur(P_KA // 128)), pl.BlockSpec((WINDOW, 128), cur(P_VA // 128)),
                  pl.BlockSpec((WINDOW, 128), prev(P_KA // 128)), pl.BlockSpec((WINDOW, 128), prev(P_VA // 128)),
                  _const_spec((1, 128)), _const_spec((1, 128)), _const_spec((1, 128))],
        out_specs=pl.BlockSpec((WINDOW, 512), lambda i: (i, 0)),
        out_shape=jax.ShapeDtypeStruct((t, 512), F32),
        compiler_params=_cparams(1),
    )(proj, proj, proj, proj, proj, proj, qg, kg, sinks)


def attn_bwd(name, proj, qg, kg, sinks, dya):
    t = proj.shape[0]
    nb = t // WINDOW

    def body(q_ref, za_ref, kc_ref, vc_ref, kp_ref, vp_ref, qg_ref, kg_ref, s_ref, dy_ref,
             dqz_ref, dkv_ref, dqg_ref, dkg_ref, ds_ref, carry_ref):
        j = pl.program_id(0)
        first = j == nb - 1

        @pl.when(j == 0)
        def _():
            carry_ref[...] = jnp.zeros_like(carry_ref)
            dqg_ref[...] = jnp.zeros_like(dqg_ref)
            dkg_ref[...] = jnp.zeros_like(dkg_ref)
            ds_ref[...] = jnp.zeros_like(ds_ref)

        ins = [r[...] for r in (q_ref, za_ref, kc_ref, vc_ref, kp_ref, vp_ref, qg_ref, kg_ref, s_ref)]
        _, vjp = jax.vjp(functools.partial(_f_attn, first), *ins)
        dq, dza, dkc, dvc, dkp, dvp, dqg, dkg, dsk = vjp(dy_ref[...])
        dqz_ref[:, 0:512] = dq.astype(dqz_ref.dtype)
        dqz_ref[:, 512:1024] = dza.astype(dqz_ref.dtype)
        dkv_ref[:, 0:128] = (dkc + carry_ref[0]).astype(dkv_ref.dtype)
        dkv_ref[:, 128:256] = (dvc + carry_ref[1]).astype(dkv_ref.dtype)
        carry_ref[0] = dkp
        carry_ref[1] = dvp
        dqg_ref[...] += dqg
        dkg_ref[...] += dkg
        ds_ref[...] += dsk

    cur = lambda cb: (lambda j: (nb - 1 - j, cb))
    prev = lambda cb: (lambda j: (jnp.maximum(nb - 2 - j, 0), cb))
    return pl.pallas_call(
        body, name=name, grid=(nb,),
        in_specs=[pl.BlockSpec((WINDOW, 512), cur(P_QA // 512)), pl.BlockSpec((WINDOW, 512), cur(P_ZA // 512)),
                  pl.BlockSpec((WINDOW, 128), cur(P_KA // 128)), pl.BlockSpec((WINDOW, 128), cur(P_VA // 128)),
                  pl.BlockSpec((WINDOW, 128), prev(P_KA // 128)), pl.BlockSpec((WINDOW, 128), prev(P_VA // 128)),
                  _const_spec((1, 128)), _const_spec((1, 128)), _const_spec((1, 128)),
                  pl.BlockSpec((WINDOW, 512), cur(0))],
        out_specs=[pl.BlockSpec((WINDOW, 1024), cur(0)), pl.BlockSpec((WINDOW, 256), cur(0)),
                   _const_spec((1, 128)), _const_spec((1, 128)), _const_spec((1, 128))],
        out_shape=[jax.ShapeDtypeStruct((t, 1024), BF16), jax.ShapeDtypeStruct((t, 256), BF16),
                   jax.ShapeDtypeStruct((1, 128), F32), jax.ShapeDtypeStruct((1, 128), F32),
                   jax.ShapeDtypeStruct((1, 128), F32)],
        scratch_shapes=[pltpu.VMEM((2, WINDOW, 128), F32)],
        compiler_params=_cparams(1),
    )(proj, proj, proj, proj, proj, proj, qg, kg, sinks, dya)


CONV_ROWS = 256


def _conv_taps(src_ref, w_ref, n_taps, base, t):
    for r0 in range(0, t, CONV_ROWS):
        acc = w_ref[0:1, :] * src_ref[pl.ds(r0 + base, CONV_ROWS), :]
        for k in range(1, n_taps):
            acc = acc + w_ref[k:k + 1, :] * src_ref[pl.ds(r0 + base + k, CONV_ROWS), :]
        yield r0, acc


def _conv_wgrad(dy_ref, src_ref, n_taps, base, t, dy_base=0):
    out = []
    for k in range(n_taps):
        acc = jnp.zeros((8, 128), F32)
        for r0 in range(0, t, CONV_ROWS):
            prod = dy_ref[pl.ds(r0 + dy_base, CONV_ROWS), :] * src_ref[pl.ds(r0 + base + k, CONV_ROWS), :]
            acc = acc + jnp.sum(prod.reshape(CONV_ROWS // 8, 8, 128), axis=0)
        out.append(jnp.sum(acc, axis=0, keepdims=True))
    return out


def glu_conv_fwd(name, proj, w32, bias):
    t = proj.shape[0]
    pad = 32

    def body(x_ref, w_ref, b_ref, o_ref, u_ref):
        u_ref[0:pad, :] = jnp.zeros((pad, 128), F32)
        u_ref[pad:pad + t, :] = x_ref[:, 0:128] * _sigmoid(x_ref[:, 128:256])
        for r0, acc in _conv_taps(u_ref, w_ref, CONV_K, pad - (CONV_K - 1), t):
            o_ref[pl.ds(r0, CONV_ROWS), :] = acc + b_ref[...]

    return pl.pallas_call(
        body, name=name, grid=(4,),
        in_specs=[pl.BlockSpec((t, 256), lambda cb: (0, P_GLU // 256 + cb)), pl.BlockSpec((32, 128), lambda cb: (0, cb)),
                  pl.BlockSpec((1, 128), lambda cb: (0, cb))],
        out_specs=pl.BlockSpec((t, 128), lambda cb: (0, cb)),
        out_shape=jax.ShapeDtypeStruct((t, 512), F32),
        scratch_shapes=[pltpu.VMEM((t + pad, 128), F32)],
        compiler_params=_cparams(1),
    )(proj, w32, bias)


def glu_conv_bwd(name, proj, w32, dub):
    t = proj.shape[0]
    pad = 32
    k1 = CONV_K - 1

    def body(x_ref, w_ref, dy_ref, dx_ref, dw_ref, db_ref, u_ref, dyp_ref, wrev_ref):
        val = x_ref[:, 0:128]
        sg = _sigmoid(x_ref[:, 128:256])
        u_ref[0:pad, :] = jnp.zeros((pad, 128), F32)
        u_ref[pad:pad + t, :] = val * sg
        dyp_ref[0:t, :] = dy_ref[...]
        dyp_ref[t:t + pad, :] = jnp.zeros((pad, 128), F32)
        for k in range(CONV_K):
            wrev_ref[k:k + 1, :] = w_ref[k1 - k:k1 - k + 1, :]
        wrev_ref[CONV_K:32, :] = jnp.zeros((32 - CONV_K, 128), F32)
        for r0, du in _conv_taps(dyp_ref, wrev_ref, CONV_K, 0, t):
            v = x_ref[pl.ds(r0, CONV_ROWS), 0:128]
            s = _sigmoid(x_ref[pl.ds(r0, CONV_ROWS), 128:256])
            dx_ref[pl.ds(r0, CONV_ROWS), 0:128] = (du * s).astype(dx_ref.dtype)
            dx_ref[pl.ds(r0, CONV_ROWS), 128:256] = (du * v * s * (1.0 - s)).astype(dx_ref.dtype)
        dws = _conv_wgrad(dyp_ref, u_ref, CONV_K, pad - k1, t)
        for k in range(CONV_K):
            dw_ref[k:k + 1, :] = dws[k]
        dw_ref[CONV_K:32, :] = jnp.zeros((32 - CONV_K, 128), F32)
        db_ref[...] = jnp.sum(dy_ref[...], axis=0, keepdims=True)

    return pl.pallas_call(
        body, name=name, grid=(4,),
        in_specs=[pl.BlockSpec((t, 256), lambda cb: (0, P_GLU // 256 + cb)), pl.BlockSpec((32, 128), lambda cb: (0, cb)),
                  pl.BlockSpec((t, 128), lambda cb: (0, cb))],
        out_specs=[pl.BlockSpec((t, 256), lambda cb: (0, cb)), pl.BlockSpec((32, 128), lambda cb: (0, cb)),
                   pl.BlockSpec((1, 128), lambda cb: (0, cb))],
        out_shape=[jax.ShapeDtypeStruct((t, 1024), BF16), jax.ShapeDtypeStruct((32, 512), F32),
                   jax.ShapeDtypeStruct((1, 512), F32)],
        scratch_shapes=[pltpu.VMEM((t + pad, 128), F32), pltpu.VMEM((t + pad, 128), F32), pltpu.VMEM((32, 128), F32)],
        compiler_params=_cparams(1),
    )(proj, w32, dub)


def sconv_fwd(name, proj, w8):
    t = proj.shape[0]
    pad = 8
    k1 = DN_CONV_K - 1

    def body(x_ref, w_ref, o_ref, xp_ref):
        xp_ref[0:pad, :] = jnp.zeros((pad, 128), F32)
        xp_ref[pad:pad + t, :] = x_ref[...]
        for r0, acc in _conv_taps(xp_ref, w_ref, DN_CONV_K, pad - k1, t):
            o_ref[pl.ds(r0, CONV_ROWS), :] = _silu(acc)

    return pl.pallas_call(
        body, name=name, grid=(12,),
        in_specs=[pl.BlockSpec((t, 128), lambda cb: (0, P_QKV // 128 + cb)), pl.BlockSpec((8, 128), lambda cb: (0, cb))],
        out_specs=pl.BlockSpec((t, 128), lambda cb: (0, cb)),
        out_shape=jax.ShapeDtypeStruct((t, 1536), F32),
        scratch_shapes=[pltpu.VMEM((t + pad, 128), F32)],
        compiler_params=_cparams(1),
    )(proj, w8)


def sconv_bwd(name, proj, w8, dqkv):
    t = proj.shape[0]
    pad = 8
    k1 = DN_CONV_K - 1

    def body(x_ref, w_ref, dy_ref, dx_ref, dw_ref, xp_ref, dpp_ref, wrev_ref):
        xp_ref[0:pad, :] = jnp.zeros((pad, 128), F32)
        xp_ref[pad:pad + t, :] = x_ref[...]
        for r0, pre in _conv_taps(xp_ref, w_ref, DN_CONV_K, pad - k1, t):
            s = _sigmoid(pre)
            dpp_ref[pl.ds(r0, CONV_ROWS), :] = dy_ref[pl.ds(r0, CONV_ROWS), :] * (s * (1.0 + pre * (1.0 - s)))
        dpp_ref[t:t + pad, :] = jnp.zeros((pad, 128), F32)
        for k in range(DN_CONV_K):
            wrev_ref[k:k + 1, :] = w_ref[k1 - k:k1 - k + 1, :]
        wrev_ref[DN_CONV_K:8, :] = jnp.zeros((8 - DN_CONV_K, 128), F32)
        for r0, dx in _conv_taps(dpp_ref, wrev_ref, DN_CONV_K, 0, t):
            dx_ref[pl.ds(r0, CONV_ROWS), :] = dx.astype(dx_ref.dtype)
        dws = _conv_wgrad(dpp_ref, xp_ref, DN_CONV_K, pad - k1, t)
        for k in range(DN_CONV_K):
            dw_ref[k:k + 1, :] = dws[k]
        dw_ref[DN_CONV_K:8, :] = jnp.zeros((8 - DN_CONV_K, 128), F32)

    return pl.pallas_call(
        body, name=name, grid=(12,),
        in_specs=[pl.BlockSpec((t, 128), lambda cb: (0, P_QKV // 128 + cb)), pl.BlockSpec((8, 128), lambda cb: (0, cb)),
                  pl.BlockSpec((t, 128), lambda cb: (0, cb))],
        out_specs=[pl.BlockSpec((t, 128), lambda cb: (0, cb)), pl.BlockSpec((8, 128), lambda cb: (0, cb))],
        out_shape=[jax.ShapeDtypeStruct((t, 1536), BF16), jax.ShapeDtypeStruct((8, 1536), F32)],
        scratch_shapes=[pltpu.VMEM((t + pad, 128), F32), pltpu.VMEM((t + pad, 128), F32), pltpu.VMEM((8, 128), F32)],
        compiler_params=_cparams(1),
    )(proj, w8, dqkv)


def _f_delta_step(qkv, ab, zc, s0, s1, s2, s3, a_log, dt_bias, dn_g):
    cs = DN_CHUNK
    n = 2 * cs
    states = (s0, s1, s2, s3)
    lane = lax.broadcasted_iota(jnp.int32, (1, 128), 1)
    ri = lax.broadcasted_iota(jnp.int32, (n, n), 0)
    ci = lax.broadcasted_iota(jnp.int32, (n, n), 1)
    same = (ri // cs) == (ci // cs)
    lower = same & (ri >= ci)
    strict = same & (ri > ci)
    sums = jnp.concatenate([jnp.where(lower, 1.0, 0.0), jnp.where(same, 1.0, 0.0), jnp.where(ci < cs, 1.0, 0.0),
                            jnp.where(ci >= cs, 1.0, 0.0)], axis=0)
    top = lax.broadcasted_iota(jnp.int32, (n, 1), 0) < cs

    def pick(row, idx):
        return jnp.sum(jnp.where(lane == idx, row, 0.0), axis=-1, keepdims=True)

    def l2n(x):
        return x * lax.rsqrt(jnp.sum(x * x, axis=-1, keepdims=True) + EPS)

    n_chunks = qkv.shape[0] // cs
    units = [(k, pair) for k in range(n_chunks) for pair in range(2)]

    pre = []
    for k, pair in units:
        hs = (2 * pair, 2 * pair + 1)
        rows = slice(k * cs, (k + 1) * cs)
        stack = lambda f: jnp.concatenate([f(hs[0]), f(hs[1])], axis=0)
        qd = l2n(stack(lambda h: qkv[rows, 128 * h:128 * h + 128])) * (128 ** -0.5)
        kd = l2n(stack(lambda h: qkv[rows, 512 + 128 * h:512 + 128 * h + 128]))
        vd = stack(lambda h: qkv[rows, 1024 + 128 * h:1024 + 128 * h + 128])
        beta = _sigmoid(stack(lambda h: pick(ab[rows], 4 + h)))
        g = stack(lambda h: -jnp.exp(pick(a_log, h)) * _softplus(pick(ab[rows], h) + pick(dt_bias, h)))
        g_sums = sel_mm(sums, g * jnp.ones((1, n), F32))
        gc_col = g_sums[0:n]
        gl_b = g_sums[n:2 * n]
        g_end = (g_sums[2 * n:3 * n], g_sums[3 * n:])
        decay = jnp.where(lower, jnp.exp(jnp.where(lower, gc_col - gc_col.T, 0.0)), 0.0)
        kb = kd * beta
        pre.append(dict(qd=qd, kd=kd, vb=vd * beta, kb=kb, gc_col=gc_col, gl_b=gl_b, g_end=g_end, decay=decay,
                        a=jnp.where(strict, mm_nt(kb, kd) * decay, 0.0)))
    tmats = tri_inv(*[p["a"] for p in pre])

    mid = []
    for p, tmat in zip(pre, tmats):
        egc = jnp.exp(p["gc_col"])
        mid.append(dict(u=mm(tmat, p["vb"]), wm=mm(tmat, p["kb"] * egc), qe=p["qd"] * egc,
                        intra=jnp.where(lower, mm_nt(p["qd"], p["kd"]) * p["decay"], 0.0),
                        ke=p["kd"] * jnp.exp(p["gl_b"] - p["gc_col"]), g_end=p["g_end"]))

    ys = []
    for k in range(n_chunks):
        rows = slice(k * cs, (k + 1) * cs)
        new_states, y_heads = [], []
        for pair in range(2):
            m = mid[2 * k + pair]
            hs = (2 * pair, 2 * pair + 1)
            st = (states[hs[0]], states[hs[1]])
            v_new = m["u"] - jnp.concatenate([mm(m["wm"][:cs], st[0]), mm(m["wm"][cs:], st[1])], axis=0)
            o = jnp.concatenate([mm(m["qe"][:cs], st[0]), mm(m["qe"][cs:], st[1])], axis=0) + mm(m["intra"], v_new)
            new_states.append(st[0] * jnp.exp(m["g_end"][0]) + mm_tn(jnp.where(top, m["ke"], 0.0), v_new))
            new_states.append(st[1] * jnp.exp(m["g_end"][1]) + mm_tn(jnp.where(top, 0.0, m["ke"]), v_new))
            od = o * lax.rsqrt(jnp.mean(o * o, axis=-1, keepdims=True) + EPS) * dn_g
            y_heads += [od[:cs] * _silu(zc[rows, 128 * hs[0]:128 * hs[0] + 128]),
                        od[cs:] * _silu(zc[rows, 128 * hs[1]:128 * hs[1] + 128])]
        states = tuple(new_states)
        ys.append(jnp.concatenate(y_heads, axis=1))
    return (jnp.concatenate(ys, axis=0), *states)


DELTA_ROWS = 4 * DN_CHUNK


def delta_fwd(name, qkv, proj, a_log, dt_bias, dn_g):
    t = qkv.shape[0]
    nc = t // DELTA_ROWS

    def body(qkv_ref, ab_ref, zc_ref, al_ref, dt_ref, g_ref, y_ref, ssave_ref, s_ref):
        @pl.when(pl.program_id(0) == 0)
        def _():
            s_ref[...] = jnp.zeros_like(s_ref)

        ssave_ref[0] = s_ref[...]
        st = [s_ref[128 * h:128 * h + 128, :] for h in range(4)]
        y, *ns = _f_delta_step(qkv_ref[...], ab_ref[...], zc_ref[...], *st, al_ref[...], dt_ref[...], g_ref[...])
        y_ref[...] = y
        for h in range(4):
            s_ref[128 * h:128 * h + 128, :] = ns[h]

    return pl.pallas_call(
        body, name=name, grid=(nc,),
        in_specs=[pl.BlockSpec((DELTA_ROWS, 1536), lambda i: (i, 0)), pl.BlockSpec((DELTA_ROWS, 128), lambda i: (i, P_AB // 128)),
                  pl.BlockSpec((DELTA_ROWS, 512), lambda i: (i, P_ZC // 512)),
                  _const_spec((1, 128)), _const_spec((1, 128)), _const_spec((1, 128))],
        out_specs=[pl.BlockSpec((DELTA_ROWS, 512), lambda i: (i, 0)), pl.BlockSpec((1, 512, 128), lambda i: (i, 0, 0))],
        out_shape=[jax.ShapeDtypeStruct((t, 512), F32), jax.ShapeDtypeStruct((nc, 512, 128), F32)],
        scratch_shapes=[pltpu.VMEM((512, 128), F32)],
        compiler_params=_cparams(1),
    )(qkv, proj, proj, a_log, dt_bias, dn_g)


def delta_bwd(name, qkv, proj, ssave, a_log, dt_bias, dn_g, dyc):
    t = qkv.shape[0]
    nc = t // DELTA_ROWS

    def body(qkv_ref, ab_ref, zc_ref, ss_ref, al_ref, dt_ref, g_ref, dy_ref,
             dqkv_ref, dab_ref, dzc_ref, dal_ref, ddt_ref, dg_ref, ds_ref):
        @pl.when(pl.program_id(0) == 0)
        def _():
            ds_ref[...] = jnp.zeros_like(ds_ref)
            dal_ref[...] = jnp.zeros_like(dal_ref)
            ddt_ref[...] = jnp.zeros_like(ddt_ref)
            dg_ref[...] = jnp.zeros_like(dg_ref)

        st = [ss_ref[0, 128 * h:128 * h + 128, :] for h in range(4)]
        _, vjp = jax.vjp(_f_delta_step, qkv_ref[...], ab_ref[...], zc_ref[...], *st, al_ref[...], dt_ref[...], g_ref[...])
        dst = tuple(ds_ref[128 * h:128 * h + 128, :] for h in range(4))
        dqkv, dab, dzc, d0, d1, d2, d3, dal, ddt, dg = vjp((dy_ref[...], *dst))
        dqkv_ref[...] = dqkv
        dab_ref[...] = dab.astype(dab_ref.dtype)
        dzc_ref[...] = dzc.astype(dzc_ref.dtype)
        for h, d in enumerate((d0, d1, d2, d3)):
            ds_ref[128 * h:128 * h + 128, :] = d
        dal_ref[...] += dal
        ddt_ref[...] += ddt
        dg_ref[...] += dg

    rev = lambda cb: (lambda j: (nc - 1 - j, cb))
    return pl.pallas_call(
        body, name=name, grid=(nc,),
        in_specs=[pl.BlockSpec((DELTA_ROWS, 1536), rev(0)), pl.BlockSpec((DELTA_ROWS, 128), rev(P_AB // 128)),
                  pl.BlockSpec((DELTA_ROWS, 512), rev(P_ZC // 512)), pl.BlockSpec((1, 512, 128), lambda j: (nc - 1 - j, 0, 0)),
                  _const_spec((1, 128)), _const_spec((1, 128)), _const_spec((1, 128)),
                  pl.BlockSpec((DELTA_ROWS, 512), rev(0))],
        out_specs=[pl.BlockSpec((DELTA_ROWS, 1536), rev(0)), pl.BlockSpec((DELTA_ROWS, 128), rev(0)),
                   pl.BlockSpec((DELTA_ROWS, 512), rev(0)),
                   _const_spec((1, 128)), _const_spec((1, 128)), _const_spec((1, 128))],
        out_shape=[jax.ShapeDtypeStruct((t, 1536), F32), jax.ShapeDtypeStruct((t, 128), BF16),
                   jax.ShapeDtypeStruct((t, 512), BF16),
                   jax.ShapeDtypeStruct((1, 128), F32), jax.ShapeDtypeStruct((1, 128), F32), jax.ShapeDtypeStruct((1, 128), F32)],
        scratch_shapes=[pltpu.VMEM((512, 128), F32)],
        compiler_params=_cparams(1),
    )(qkv, proj, proj, ssave, a_log, dt_bias, dn_g, dyc)


def loss_head(name, y, target, tm):
    t, d = y.shape

    def body(y_ref, t_ref, dy_ref, l_ref):
        err = y_ref[...] - t_ref[...]
        dy_ref[...] = err * (1.0 / d)
        part = 0.5 * jnp.sum(jnp.sum(err * err, axis=-1, keepdims=True) * (1.0 / d), axis=0, keepdims=True)

        @pl.when(pl.program_id(0) == 0)
        def _():
            l_ref[...] = part

        @pl.when(pl.program_id(0) > 0)
        def _():
            l_ref[...] += part

    return pl.pallas_call(
        body, name=name, grid=(t // tm,),
        in_specs=[_row_spec(tm, d, 0), _row_spec(tm, d, 0)],
        out_specs=[_row_spec(tm, d, 0), _const_spec((1, 1))],
        out_shape=[jax.ShapeDtypeStruct((t, d), F32), jax.ShapeDtypeStruct((1, 1), F32)],
        compiler_params=_cparams(1),
    )(y, target)


TM = 512
TM_MERGE = 256
TM_IN = 1024
TN_IN = 1152


def _lane_pad(v, n=128):
    return jnp.pad(v.astype(F32), (0, n - v.shape[0]))[None, :]


def f_norm_mod_res(x, g, scale, shift):
    return f_norm_mod(x, g, scale, shift), x


def prep_layer(w):
    p = dict(w)
    p["wp"] = _w_in_assemble(w["w_in"])
    p["wpt"] = p["wp"].T
    p["wpa"] = _perm_heads_rows(w["w_proj_a"])
    p["dw32"] = jnp.pad(w["dw_w"], ((0, 32 - CONV_K), (0, 0)))
    p["sconv8"] = jnp.pad(w["sconv_w"], ((0, 8 - DN_CONV_K), (0, 0)))
    p["qg"] = jnp.tile(w["q_norm_g"], 2)[None, :]
    p["kg"] = jnp.tile(w["k_norm_g"], 2)[None, :]
    p["sinks128"] = _lane_pad(w["sinks"])
    p["al"] = _lane_pad(w["a_log"])
    p["dtb"] = _lane_pad(w["dt_bias"])
    p["dng"] = w["dn_norm_g"][None, :]
    return p


def layer_fwd(tag, x, mod, p):
    d = D_MODEL
    shift, scale, gate = mod[:, :d], mod[:, d:2 * d], mod[:, 2 * d:]
    g = p["norm_g"][None, :]
    (h,) = rowwise_fwd(f"norm_fwd{tag}", f_norm_mod, [(x, d, 0)], [g, scale, shift], [(d, BF16)], TM)
    proj = matmul_nn(f"inproj_fwd{tag}", h, p["wp"], F32, TM_IN, TN_IN, d)
    ya = attn_fwd(f"attn_fwd{tag}", proj, p["qg"], p["kg"], p["sinks128"])
    ub = glu_conv_fwd(f"glu_conv_fwd{tag}", proj, p["dw32"], p["dw_b"][None, :])
    conf_consts = [p["ln_g"][None, :], p["ln_b"][None, :], p["pw2_w"], p["pw2_b"][None, :]]
    (yb,) = rowwise_fwd(f"conf_fwd{tag}", f_conf_tail, [(ub, 512, 0), (proj, 512, P_ZB // 512)], conf_consts, [(512, F32)], TM)
    qkv = sconv_fwd(f"sconv_fwd{tag}", proj, p["sconv8"])
    yc, ssave = delta_fwd(f"delta_fwd{tag}", qkv, proj, p["al"], p["dtb"], p["dng"])
    merge_consts = [gate, p["wpa"], p["w_proj_b"], p["w_proj_c"], p["w_out"]]
    merge_rows = [(ya, 512, 0), (yb, 512, 0), (yc, 512, 0), (proj, 3 * d, P_MG // (3 * d)), (x, d, 0)]
    (xn,) = rowwise_fwd(f"merge_fwd{tag}", f_merge, merge_rows, merge_consts, [(d, F32)], TM_MERGE)
    saved = dict(x=x, h=h, proj=proj, ub=ub, qkv=qkv, ssave=ssave, norm_consts=[g, scale, shift],
                 conf_consts=conf_consts, merge_consts=merge_consts, merge_rows=merge_rows)
    return xn, saved


def layer_bwd(tag, dxn, p, s):
    d = D_MODEL
    proj = s["proj"]
    dya, dyb, dyc, dmg, dgate, dwpa, dwpb, dwpc, dwout = rowwise_bwd(
        f"merge_bwd{tag}", f_merge, s["merge_rows"], s["merge_consts"], [(dxn, d, 0)], [F32, F32, F32, BF16, None], TM_MERGE)
    dqz, dkv, dqg, dkg, dsinks = attn_bwd(f"attn_bwd{tag}", proj, p["qg"], p["kg"], p["sinks128"], dya)
    dub, dzb, dln_g, dln_b, dpw2_w, dpw2_b = rowwise_bwd(
        f"conf_bwd{tag}", f_conf_tail, [(s["ub"], 512, 0), (proj, 512, P_ZB // 512)], s["conf_consts"], [(dyb, 512, 0)],
        [F32, BF16], TM)
    dglu, ddw32, ddw_b = glu_conv_bwd(f"glu_conv_bwd{tag}", proj, p["dw32"], dub)
    dqkv, dab, dzc, dal, ddtb, ddng = delta_bwd(f"delta_bwd{tag}", s["qkv"], proj, s["ssave"], p["al"], p["dtb"], p["dng"], dyc)
    dqkv_pre, dsconv8 = sconv_bwd(f"sconv_bwd{tag}", proj, p["sconv8"], dqkv)
    dproj = jnp.concatenate([dqz, dglu, dzb, dzc, dmg, dqkv_pre, dkv, dab], axis=1)
    dh = matmul_nn(f"inproj_bwd_dh{tag}", dproj, p["wpt"], F32, TM_IN, d, P_TOTAL // 3)
    dwp = matmul_nn(f"inproj_bwd_dw{tag}", s["h"].T, dproj, F32, d, TN_IN, 2048)
    dx, dnorm_g, dscale, dshift = rowwise_bwd(
        f"norm_bwd{tag}", f_norm_mod_res, [(s["x"], d, 0)], s["norm_consts"], [(dh, d, 0), (dxn, d, 0)], [F32], TM)
    dmod = jnp.concatenate([dshift, dscale, dgate], axis=1)
    grads = dict(
        b_ada=dmod[0], norm_g=dnorm_g[0], w_in=_w_in_grad_blocks(dwp),
        q_norm_g=dqg[0, :64] + dqg[0, 64:], k_norm_g=dkg[0, :64] + dkg[0, 64:], sinks=dsinks[0, :ATT_HEADS],
        dw_w=ddw32[:CONV_K], dw_b=ddw_b[0], ln_g=dln_g[0], ln_b=dln_b[0], pw2_w=dpw2_w, pw2_b=dpw2_b[0],
        sconv_w=dsconv8[:DN_CONV_K], a_log=dal[0, :DN_HEADS], dt_bias=ddtb[0, :DN_HEADS], dn_norm_g=ddng[0],
        w_proj_a=_unperm_heads_rows(dwpa), w_proj_b=dwpb, w_proj_c=dwpc, w_out=dwout)
    return dx, grads


SHARDED = {"w_ada": 2, "w_in": 2, "dw_w": 2, "pw2_w": 1, "sconv_w": 2, "w_proj_a": 2, "w_proj_b": 2, "w_proj_c": 2,
           "w_out": 1}
GATHERED = tuple(n for n in SHARDED if n != "w_ada")
GATHER_F32 = ("dw_w", "sconv_w")
REDUCE_BIG = tuple(n for n in GATHERED if n not in GATHER_F32)
SMALL = ("b_ada", "norm_g", "q_norm_g", "k_norm_g", "sinks", "dw_b", "ln_g", "ln_b", "pw2_b", "a_log", "dt_bias",
         "dn_norm_g")
SMALL_ROWS = 104
SMALL_GRAD_ROWS = 448
W_IN_SHARD = D_IN // N_CHIPS
SUM_TILE = 256


def _w_in_orig():
    orig = np.full(P_TOTAL, -1, np.int64)
    p = 0
    for s, n in _in_pieces():
        orig[p:p + n] = np.arange(s, s + n)
        p += n
    return orig


def _w_in_blocks(k):
    orig = _w_in_orig().reshape(-1, 128)
    lo, hi = k * W_IN_SHARD, (k + 1) * W_IN_SHARD
    return [b for b in range(orig.shape[0]) if np.any((orig[b] >= lo) & (orig[b] < hi))]


W_IN_BLOCKS = max(len(_w_in_blocks(k)) for k in range(N_CHIPS))


def _runs(idx):
    out, i = [], 0
    while i < len(idx):
        j = i + 1
        while j < len(idx) and ((idx[i] < 0 and idx[j] < 0) or (idx[i] >= 0 and idx[j] == idx[j - 1] + 1)):
            j += 1
        out.append((int(idx[i]) if idx[i] >= 0 else -1, j - i))
        i = j
    return out


def _take_cols(a, idx):
    parts = [jnp.zeros(a.shape[:-1] + (n,), a.dtype) if s < 0 else a[..., s:s + n] for s, n in _runs(idx)]
    return parts[0] if len(parts) == 1 else jnp.concatenate(parts, axis=-1)


def _w_in_send(k, shard):
    orig = _w_in_orig().reshape(-1, 128)
    lo, hi = k * W_IN_SHARD, (k + 1) * W_IN_SHARD
    idx = np.concatenate([np.where((orig[b] >= lo) & (orig[b] < hi), orig[b] - lo, -1) for b in _w_in_blocks(k)])
    idx = np.concatenate([idx, np.full((W_IN_BLOCKS - len(_w_in_blocks(k))) * 128, -1)])
    return _take_cols(shard, idx)


def _w_in_assemble(blocks):
    where = [{b: i for i, b in enumerate(_w_in_blocks(k))} for k in range(N_CHIPS)]
    n_blocks = P_TOTAL // 128
    owners = [[(k, where[k][b]) for k in range(N_CHIPS) if b in where[k]] for b in range(n_blocks)]
    parts, b = [], 0
    while b < n_blocks:
        if len(owners[b]) == 1:
            k, pos = owners[b][0]
            e = b + 1
            while e < n_blocks and owners[e] == [(k, pos + e - b)]:
                e += 1
            parts.append(blocks[k][:, pos * 128:(pos + e - b) * 128])
            b = e
        else:
            parts.append(functools.reduce(jnp.add, [blocks[k][:, pos * 128:(pos + 1) * 128] for k, pos in owners[b]]))
            b += 1
    return jnp.concatenate(parts, axis=1)


def _w_in_grad_blocks(wp):
    out = []
    for k in range(N_CHIPS):
        idx = np.concatenate([np.arange(128 * b, 128 * b + 128) for b in _w_in_blocks(k)])
        idx = np.concatenate([idx, np.full((W_IN_BLOCKS - len(_w_in_blocks(k))) * 128, -1)])
        out.append(_take_cols(wp, idx))
    return jnp.stack(out)


def _w_in_receive_grad(k, blocks):
    orig = _w_in_orig()
    inv = np.zeros(D_IN, np.int64)
    inv[orig[orig >= 0]] = np.nonzero(orig >= 0)[0]
    where = {b: i for i, b in enumerate(_w_in_blocks(k))}
    cols = inv[k * W_IN_SHARD:(k + 1) * W_IN_SHARD]
    return _take_cols(blocks, np.array([where[c // 128] * 128 + c % 128 for c in cols]))


def _join_layer(v, axis):
    if axis == 2:
        return jnp.transpose(v, (1, 0, 2)).reshape(v.shape[1], N_CHIPS * v.shape[2])
    return v.reshape(N_CHIPS * v.shape[1], v.shape[2])


def _split_layer(v, axis):
    a, b = v.shape
    if axis == 2:
        return jnp.transpose(v.reshape(a, N_CHIPS, b // N_CHIPS), (1, 0, 2))
    return v.reshape(N_CHIPS, a // N_CHIPS, b)


def pack_small(vals, names, rows):
    flat = jnp.concatenate([vals[n].astype(F32).reshape(-1) for n in names])
    return jnp.pad(flat, (0, rows * 128 - flat.shape[0])).reshape(rows, 128)


def unpack_small(packed, names, shapes):
    flat = packed.reshape(-1)
    out, off = {}, 0
    for n in names:
        k = int(np.prod(shapes[n]))
        out[n] = flat[off:off + k].reshape(shapes[n])
        off += k
    return out


ANY = pl.BlockSpec(memory_space=pl.ANY)


def _place():
    x, y, c = lax.axis_index("x"), lax.axis_index("y"), lax.axis_index("c")
    chips = [(1 - x, y), (x, 1 - y), (1 - x, 1 - y)]
    return x, y, c, chips


def _remote(src, dst, send_sem, recv_sem, to):
    return pltpu.make_async_remote_copy(src_ref=src, dst_ref=dst, send_sem=send_sem, recv_sem=recv_sem, device_id=to,
                                        device_id_type=MESH)


def weights_allgather(slots):
    n = len(slots)

    def body(*refs):
        out = refs[n:2 * n]
        send_sems, recv_sems = refs[2 * n:]
        x, y, c, chips = _place()
        me, sibling, my_slot = (x, y, c), (x, y, 1 - c), 2 * x + y
        sends = []
        for j, chip in enumerate(chips):
            for t in range(n):
                mine = out[t].at[my_slot, c]
                sends.append(_remote(mine, mine, send_sems.at[t, j], recv_sems.at[t, j], (*chip, c)))
                sends[-1].start()
        for j, chip in enumerate(chips):
            for t in range(n):
                land = out[t].at[2 * chip[0] + chip[1], c]
                _remote(land, land, send_sems.at[t, j], recv_sems.at[t, j], me).wait_recv()
                sends.append(_remote(land, land, send_sems.at[t, 3 + j], recv_sems.at[t, 3 + j], sibling))
                sends[-1].start()
        for j, chip in enumerate(chips):
            for t in range(n):
                land = out[t].at[2 * chip[0] + chip[1], 1 - c]
                _remote(land, land, send_sems.at[t, 3 + j], recv_sems.at[t, 3 + j], me).wait_recv()
        for cp in sends:
            cp.wait_send()

    return pl.pallas_call(
        body, name="weights_allgather", out_shape=[jax.ShapeDtypeStruct(s.shape, s.dtype) for s in slots],
        in_specs=[ANY] * n, out_specs=[ANY] * n, input_output_aliases={t: t for t in range(n)},
        scratch_shapes=[pltpu.SemaphoreType.DMA((n, 6)), pltpu.SemaphoreType.DMA((n, 6))],
    )(*slots)


def grads_pair_exchange(gs):
    n = len(gs)

    def body(*refs):
        g, recv = refs[:n], refs[n:2 * n]
        send_sems, recv_sems = refs[2 * n:]
        x, y, c, _ = _place()
        cps = [_remote(g[t].at[:, 1 - c], recv[t], send_sems.at[t], recv_sems.at[t], (x, y, 1 - c)) for t in range(n)]
        for cp in cps:
            cp.start()
        for cp in cps:
            cp.wait()

    return pl.pallas_call(
        body, name="grads_pair_exchange",
        out_shape=[jax.ShapeDtypeStruct((N_CHIPS,) + g.shape[2:], g.dtype) for g in gs],
        in_specs=[ANY] * n, out_specs=[ANY] * n,
        scratch_shapes=[pltpu.SemaphoreType.DMA((n,)), pltpu.SemaphoreType.DMA((n,))],
    )(*gs)


def grads_pair_sum(name, g, recv):
    _, a, b = recv.shape
    ta = min(a, SUM_TILE)

    def body(a_ref, b_ref, o_ref):
        o_ref[...] = (a_ref[...] + b_ref[...]).astype(o_ref.dtype)

    return pl.pallas_call(
        body, name=name, grid=(N_CHIPS, a // ta),
        in_specs=[pl.BlockSpec((None, None, ta, b), lambda s, i: (s, lax.axis_index("c"), i, 0)),
                  pl.BlockSpec((None, ta, b), lambda s, i: (s, i, 0))],
        out_specs=pl.BlockSpec((None, ta, b), lambda s, i: (s, i, 0)),
        out_shape=jax.ShapeDtypeStruct(recv.shape, BF16),
        compiler_params=_cparams(2),
    )(g, recv)


def grads_chip_exchange(ps):
    n = len(ps)

    def body(*refs):
        p, recv = refs[:n], refs[n:2 * n]
        send_sems, recv_sems = refs[2 * n:]
        x, y, c, chips = _place()
        cps = [_remote(p[t].at[2 * chip[0] + chip[1]], recv[t].at[j], send_sems.at[t, j], recv_sems.at[t, j], (*chip, c))
               for j, chip in enumerate(chips) for t in range(n)]
        for cp in cps:
            cp.start()
        for cp in cps:
            cp.wait()

    return pl.pallas_call(
        body, name="grads_chip_exchange", out_shape=[jax.ShapeDtypeStruct((3,) + p.shape[1:], p.dtype) for p in ps],
        in_specs=[ANY] * n, out_specs=[ANY] * n,
        scratch_shapes=[pltpu.SemaphoreType.DMA((n, 3)), pltpu.SemaphoreType.DMA((n, 3))],
    )(*ps)


def grads_chip_sum(name, g, recv, recv2):
    _, a, b = recv.shape
    ta = min(a, SUM_TILE)
    my_slot = lambda: 2 * lax.axis_index("x") + lax.axis_index("y")

    def body(g_ref, r_ref, r2_ref, o_ref):
        own = g_ref[...] + r_ref[...]
        o_ref[...] = ((own + r2_ref[0].astype(F32)) + r2_ref[1].astype(F32)) + r2_ref[2].astype(F32)

    return pl.pallas_call(
        body, name=name, grid=(a // ta,),
        in_specs=[pl.BlockSpec((None, None, ta, b), lambda i: (my_slot(), lax.axis_index("c"), i, 0)),
                  pl.BlockSpec((None, ta, b), lambda i: (my_slot(), i, 0)),
                  pl.BlockSpec((3, ta, b), lambda i: (0, i, 0))],
        out_specs=pl.BlockSpec((None, ta, b), lambda i: (lax.axis_index("c"), i, 0)),
        out_shape=jax.ShapeDtypeStruct((DEPTH, a, b), F32),
        compiler_params=_cparams(1),
    )(g, recv, recv2)


def grads_pair_gather(reds):
    n = len(reds)

    def body(*refs):
        buf = refs[n:2 * n]
        send_sems, recv_sems = refs[2 * n:]
        x, y, c, _ = _place()
        sibling = (x, y, 1 - c)
        cps = [_remote(buf[t].at[c], buf[t].at[c], send_sems.at[t], recv_sems.at[t], sibling) for t in range(n)]
        for cp in cps:
            cp.start()
        for t in range(n):
            _remote(buf[t].at[c], buf[t].at[1 - c], send_sems.at[t], recv_sems.at[t], sibling).wait_recv()
        for cp in cps:
            cp.wait_send()

    return pl.pallas_call(
        body, name="grads_pair_gather", out_shape=[jax.ShapeDtypeStruct(r.shape, r.dtype) for r in reds],
        in_specs=[ANY] * n, out_specs=[ANY] * n, input_output_aliases={t: t for t in range(n)},
        scratch_shapes=[pltpu.SemaphoreType.DMA((n,)), pltpu.SemaphoreType.DMA((n,))],
    )(*reds)


def small_allreduce(v):
    m, n = v.shape

    def body(x_ref, sum_ref, all_ref, send_sems, recv_sems, local_sem):
        x, y, c, chips = _place()
        me, sibling = (x, y, c), (x, y, 1 - c)

        def rows(px, py, pc):
            return all_ref.at[pl.ds((4 * px + 2 * py + pc) * m, m), :]

        def copy(k, block, to, src=None):
            return pltpu.make_async_remote_copy(src_ref=rows(*block) if src is None else src, dst_ref=rows(*block),
                                                send_sem=send_sems.at[k], recv_sem=recv_sems.at[k],
                                                device_id=to, device_id_type=MESH)

        mine = pltpu.make_async_copy(x_ref, rows(*me), local_sem)
        mine.start()
        first = [copy(0, me, sibling, src=x_ref)]
        first += [copy(1 + j, me, (*chip, c), src=x_ref) for j, chip in enumerate(chips)]
        for cp in first:
            cp.start()
        passed = [copy(4 + j, (*chip, c), sibling) for j, chip in enumerate(chips)]
        for j, chip in enumerate(chips):
            copy(1 + j, (*chip, c), me).wait_recv()
            passed[j].start()
        copy(0, sibling, me).wait_recv()
        for j, chip in enumerate(chips):
            copy(4 + j, (*chip, 1 - c), me).wait_recv()
        for cp in first + passed:
            cp.wait_send()
        mine.wait()
        acc = all_ref[0:m, :]
        for dev in range(1, 8):
            acc = acc + all_ref[dev * m:(dev + 1) * m, :]
        sum_ref[...] = acc

    vm = pl.BlockSpec(memory_space=pltpu.VMEM)
    return pl.pallas_call(
        body, name="small_allreduce",
        out_shape=[jax.ShapeDtypeStruct((m, n), F32), jax.ShapeDtypeStruct((8 * m, n), F32)],
        in_specs=[vm], out_specs=[vm, vm],
        scratch_shapes=[pltpu.SemaphoreType.DMA((7,)), pltpu.SemaphoreType.DMA((7,)), pltpu.SemaphoreType.DMA],
    )(v)


def reduce_scatter_grads(names, gs):
    recv = grads_pair_exchange(gs)
    parts = [grads_pair_sum("grads_pair_sum_" + n, g, r) for n, g, r in zip(names, gs, recv)]
    recv2 = grads_chip_exchange(parts)
    reds = [grads_chip_sum("grads_chip_sum_" + n, g, r, r2) for n, g, r, r2 in zip(names, gs, recv, recv2)]
    return grads_pair_gather(reds)


def adamw(name, w, g, m, v, block):
    grid = tuple(s // b for s, b in zip(w.shape, block))

    def body(w_ref, g_ref, m_ref, v_ref, d_ref, nm_ref, nv_ref):
        gv = g_ref[...]
        nm = ADAM_B1 * m_ref[...] + (1.0 - ADAM_B1) * gv
        nv = ADAM_B2 * v_ref[...] + (1.0 - ADAM_B2) * (gv * gv)
        m_hat = nm / (1.0 - ADAM_B1 ** ADAM_STEP)
        v_hat = nv / (1.0 - ADAM_B2 ** ADAM_STEP)
        d_ref[...] = -ADAM_LR * (m_hat / (jnp.sqrt(v_hat) + ADAM_EPS) + ADAM_WD * w_ref[...])
        nm_ref[...] = nm
        nv_ref[...] = nv

    spec = pl.BlockSpec(tuple(block), lambda *idx: idx)
    return pl.pallas_call(
        body, name=name, grid=grid, in_specs=[spec] * 4, out_specs=[spec] * 3,
        out_shape=[jax.ShapeDtypeStruct(w.shape, F32)] * 3, compiler_params=_cparams(len(grid)),
    )(w, g, m, v)


ADAM_ROWS = {"w_ada": 512, "dw_w": 62, "pw2_w": 256, "sconv_w": 8, "w_proj_a": 512, "w_proj_b": 512, "w_proj_c": 512,
             "w_out": 256}
ADAM_W_IN_COLS = 331

WEIGHT_NAMES = ("w_ada", "b_ada", "norm_g", "w_in", "q_norm_g", "k_norm_g", "sinks", "dw_w", "dw_b", "ln_g", "ln_b",
                "pw2_w", "pw2_b", "sconv_w", "a_log", "dt_bias", "dn_norm_g", "w_proj_a", "w_proj_b", "w_proj_c", "w_out")


def kernel(x, c, w_ada, b_ada, norm_g, w_in, q_norm_g, k_norm_g, sinks, dw_w, dw_b, ln_g, ln_b, pw2_w, pw2_b, sconv_w, a_log, dt_bias, dn_norm_g, w_proj_a, w_proj_b, w_proj_c, w_out, loss_target, m_w_ada, m_b_ada, m_norm_g, m_w_in, m_q_norm_g, m_k_norm_g, m_sinks, m_dw_w, m_dw_b, m_ln_g, m_ln_b, m_pw2_w, m_pw2_b, m_sconv_w, m_a_log, m_dt_bias, m_dn_norm_g, m_w_proj_a, m_w_proj_b, m_w_proj_c, m_w_out, v_w_ada, v_b_ada, v_norm_g, v_w_in, v_q_norm_g, v_k_norm_g, v_sinks, v_dw_w, v_dw_b, v_ln_g, v_ln_b, v_pw2_w, v_pw2_b, v_sconv_w, v_a_log, v_dt_bias, v_dn_norm_g, v_w_proj_a, v_w_proj_b, v_w_proj_c, v_w_out):
    args = dict(locals())
    w = {n: args[n] for n in WEIGHT_NAMES}
    mom = {n: args["m_" + n] for n in WEIGHT_NAMES}
    var = {n: args["v_" + n] for n in WEIGHT_NAMES}

    chip = 2 * lax.axis_index("x") + lax.axis_index("y")
    slots = []
    for n in GATHERED:
        own = w[n] if n in GATHER_F32 else w[n].astype(BF16)
        if n == "w_in":
            own = lax.switch(chip, [functools.partial(_w_in_send, k) for k in range(N_CHIPS)], own)
        slots.append(lax.dynamic_update_slice(lax.empty((N_CHIPS,) + own.shape, own.dtype), own[None], (chip, 0, 0, 0)))
    gathered = dict(zip(GATHERED, weights_allgather(slots)))
    layers = []
    for l in range(DEPTH):
        lw = {n: w[n][l] for n in SMALL}
        for n in GATHERED:
            lw[n] = gathered[n][:, l] if n == "w_in" else _join_layer(gathered[n][:, l], SHARDED[n])
        layers.append(prep_layer(lw))

    mod, conds = ada_fwd(jnp.tile(c, (8, 1)), w["w_ada"], w["b_ada"])
    act, saved = x[0], []
    for l in range(DEPTH):
        act, s = layer_fwd(str(l), act, mod[l:l + 1], layers[l])
        saved.append(s)
    dact, loss_part = loss_head("loss_head", act, loss_target[0], TM)
    loss = lax.psum(loss_part[0, 0], ("x", "y", "c"))
    layer_grads = [None] * DEPTH
    for l in reversed(range(DEPTH)):
        dact, layer_grads[l] = layer_bwd(str(l), dact, layers[l], saved[l])

    by_chip = [jnp.stack([layer_grads[l][n] if n == "w_in" else _split_layer(layer_grads[l][n], SHARDED[n])
                          for l in range(DEPTH)], axis=1) for n in REDUCE_BIG]
    final_grads = dict(zip(REDUCE_BIG, reduce_scatter_grads(REDUCE_BIG, by_chip)))
    final_grads["w_in"] = lax.switch(chip, [functools.partial(_w_in_receive_grad, k) for k in range(N_CHIPS)],
                                     final_grads["w_in"])
    small_names = SMALL + GATHER_F32
    small_shapes = {n: (DEPTH,) + layer_grads[0][n].shape for n in small_names}
    small_full = {n: jnp.stack([layer_grads[l][n] for l in range(DEPTH)]) for n in small_names}
    small_sum, small_all = small_allreduce(pack_small(small_full, small_names, SMALL_GRAD_ROWS))
    small_sum = unpack_small(small_sum, small_names, small_shapes)
    for n in GATHER_F32:
        width = w[n].shape[2]
        final_grads[n] = lax.dynamic_slice_in_dim(small_sum[n], chip * width, width, axis=2)
    n_mod = DEPTH * 3 * D_MODEL
    dmod = small_all.reshape(8, -1)[:, :n_mod].reshape(8, DEPTH, 3 * D_MODEL)
    width = w["w_ada"].shape[2]
    dmod = jnp.transpose(lax.dynamic_slice_in_dim(dmod, chip * width, width, axis=2), (1, 0, 2))
    final_grads["w_ada"] = ada_bwd(conds, dmod)
    final_grads.update({n: small_sum[n] for n in SMALL})
    small_grads = pack_small(final_grads, SMALL, SMALL_ROWS)

    delta, new_m, new_v = {}, {}, {}
    for n in SHARDED:
        shp = w[n].shape
        if n == "w_in":
            view = lambda a: jnp.transpose(a, (2, 0, 1))
            back = lambda a: jnp.transpose(a, (1, 2, 0))
            g3 = view(final_grads[n])
            final_grads[n] = back(g3)
            d, nm, nv = adamw("adamw_" + n, view(w[n]), g3, view(mom[n]), view(var[n]), (ADAM_W_IN_COLS, shp[0], shp[1]))
        else:
            view = lambda a, shp=shp: a.reshape(shp[0] * shp[1], shp[2])
            back = lambda a, shp=shp: a.reshape(shp)
            d, nm, nv = adamw("adamw_" + n, view(w[n]), view(final_grads[n]), view(mom[n]), view(var[n]),
                              (ADAM_ROWS[n], shp[2]))
        delta[n], new_m[n], new_v[n] = back(d), back(nm), back(nv)
    d, nm, nv = adamw("adamw_small", pack_small(w, SMALL, SMALL_ROWS), small_grads, pack_small(mom, SMALL, SMALL_ROWS),
                      pack_small(var, SMALL, SMALL_ROWS), (SMALL_ROWS, 128))
    delta.update(unpack_small(d, SMALL, small_shapes))
    new_m.update(unpack_small(nm, SMALL, small_shapes))
    new_v.update(unpack_small(nv, SMALL, small_shapes))

    return (loss, dact[None], *[final_grads[n] for n in WEIGHT_NAMES], *[delta[n] for n in WEIGHT_NAMES],
            *[new_m[n] for n in WEIGHT_NAMES], *[new_v[n] for n in WEIGHT_NAMES])
```

```python
import functools

import numpy as np
import jax
import jax.numpy as jnp
from jax import lax
from jax.experimental import pallas as pl
from jax.experimental.pallas import tpu as pltpu

F32 = jnp.float32
BF16 = jnp.bfloat16
MESH = pl.DeviceIdType.MESH

D_MODEL = 1024
DEPTH = 2
ATT_HEADS = 8
ATT_HEAD_DIM = 64
WINDOW = 128
CONV_K = 31
DN_HEADS = 4
DN_CONV_K = 4
DN_CHUNK = 64
EPS = 1e-6
NEG_INF = -1e30
N_CHIPS = 4
D_IN = 7944

ADAM_LR = 0.001
ADAM_B1 = 0.9
ADAM_B2 = 0.999
ADAM_EPS = 1e-08
ADAM_WD = 0.01
ADAM_STEP = 10

VMEM_LIMIT = 56 * 1024 * 1024

P_QA, P_ZA, P_GLU, P_ZB, P_ZC, P_MG, P_QKV, P_KA, P_VA, P_AB, P_TOTAL = (
    0, 512, 1024, 2048, 2560, 3072, 6144, 7680, 7808, 7936, 8064)
HEAD_ORDER = (0, 4, 1, 5, 2, 6, 3, 7)


def _in_pieces():
    p = [(0 + 64 * h, 64) for h in HEAD_ORDER]
    p += [(768 + 64 * h, 64) for h in HEAD_ORDER]
    for g in range(4):
        p += [(1280 + 128 * g, 128), (1792 + 128 * g, 128)]
    p += [(2304, 512), (4360, 512), (4872, 3072), (2816, 1536), (512, 128), (640, 128), (4352, 8)]
    return p


def _perm_heads_rows(w):
    return jnp.concatenate([w[64 * h:64 * h + 64] for h in HEAD_ORDER], axis=0)


def _unperm_heads_rows(w):
    inv = [HEAD_ORDER.index(h) for h in range(8)]
    return jnp.concatenate([w[64 * s:64 * s + 64] for s in inv], axis=0)


def _split_bf16(a, terms):
    out, rest = [], a.astype(F32)
    for _ in range(terms - 1):
        out.append(rest.astype(BF16))
        rest = rest - out[-1].astype(F32)
    return out + [rest.astype(BF16)]


def _dot(a, b, dims, exact):
    d = lambda p, q: lax.dot_general(p, q, (dims, ((), ())), preferred_element_type=F32)
    if exact:
        (ah, al), (bh, bl) = _split_bf16(a, 2), _split_bf16(b, 2)
        return d(ah, bh) + (d(ah, bl) + d(al, bh))
    return d(a.astype(BF16), b.astype(BF16))


def _make_mm(exact):
    @jax.custom_vjp
    def nn(a, b):
        return _dot(a, b, ((1,), (0,)), exact)

    @jax.custom_vjp
    def nt(a, b):
        return _dot(a, b, ((1,), (1,)), exact)

    @jax.custom_vjp
    def tn(a, b):
        return _dot(a, b, ((0,), (0,)), exact)

    nn.defvjp(lambda a, b: (nn(a, b), (a, b)),
              lambda r, g: (nt(g, r[1]).astype(r[0].dtype), tn(r[0], g).astype(r[1].dtype)))
    nt.defvjp(lambda a, b: (nt(a, b), (a, b)),
              lambda r, g: (nn(g, r[1]).astype(r[0].dtype), tn(g, r[0]).astype(r[1].dtype)))
    tn.defvjp(lambda a, b: (tn(a, b), (a, b)),
              lambda r, g: (nt(r[1], g).astype(r[0].dtype), nn(r[0], g).astype(r[1].dtype)))
    return nn, nt, tn


mm, mm_nt, mm_tn = _make_mm(False)
xmm, xmm_nt, xmm_tn = _make_mm(True)


@jax.custom_vjp
def sel_mm(m, g):
    mb = m.astype(BF16)
    parts = [jnp.dot(mb, p, preferred_element_type=F32) for p in _split_bf16(g, 3)]
    return parts[0] + (parts[1] + parts[2])


def _sel_mm_bwd(m, dy):
    mb = m.astype(BF16)
    parts = [lax.dot_general(mb, p, (((0,), (0,)), ((), ())), preferred_element_type=F32) for p in _split_bf16(dy, 3)]
    return jnp.zeros_like(m), parts[0] + (parts[1] + parts[2])


sel_mm.defvjp(lambda m, g: (sel_mm(m, g), m), _sel_mm_bwd)


@jax.custom_vjp
def tri_inv(*mats):
    n = mats[0].shape[0]
    eye = jnp.where(lax.broadcasted_iota(jnp.int32, (n, n), 0) == lax.broadcasted_iota(jnp.int32, (n, n), 1), 1.0, 0.0)
    ts = [eye - a for a in mats]
    pws = list(mats)
    for _ in range(5):
        pws = [xmm(pw, pw) for pw in pws]
        ts = [t + xmm(t, pw) for t, pw in zip(ts, pws)]
    return tuple(ts)


def _tri_inv_bwd(ts, dts):
    inner = [xmm_nt(dt, t) for t, dt in zip(ts, dts)]
    return tuple(-xmm_tn(t, m) for t, m in zip(ts, inner))


tri_inv.defvjp(lambda *mats: (tri_inv(*mats),) * 2, _tri_inv_bwd)


@jax.custom_vjp
def tri_inv_known(a, t):
    return t


tri_inv_known.defvjp(lambda a, t: (t, t), lambda t, dt: (_tri_inv_bwd((t,), (dt,))[0], jnp.zeros_like(t)))


def _sigmoid(x):
    return 1.0 / (1.0 + jnp.exp(-x))


def _silu(x):
    return x * _sigmoid(x)


def _softplus(x):
    return jnp.maximum(x, 0.0) + jnp.log(1.0 + jnp.exp(-jnp.abs(x)))


def _cparams(n_grid):
    return pltpu.CompilerParams(dimension_semantics=("arbitrary",) * n_grid, vmem_limit_bytes=VMEM_LIMIT)


def _row_spec(tm, width, colblk):
    return pl.BlockSpec((tm, width), lambda i, cb=colblk: (i, cb))


def _const_spec(shape):
    nd = len(shape)
    return pl.BlockSpec(tuple(shape), lambda i, nd=nd: (0,) * nd)


def rowwise_fwd(name, f, rows, consts, outs, tm):
    n_r, n_c = len(rows), len(consts)
    t = rows[0][0].shape[0]

    def body(*refs):
        vals = [r[...] for r in refs[:n_r + n_c]]
        res = f(*vals)
        if not isinstance(res, (tuple, list)):
            res = (res,)
        for o_ref, v in zip(refs[n_r + n_c:], res):
            o_ref[...] = v.astype(o_ref.dtype)

    return pl.pallas_call(
        body, name=name, grid=(t // tm,),
        in_specs=[_row_spec(tm, w, cb) for _, w, cb in rows] + [_const_spec(c.shape) for c in consts],
        out_specs=[_row_spec(tm, w, 0) for w, _ in outs],
        out_shape=[jax.ShapeDtypeStruct((t, w), dt) for w, dt in outs],
        compiler_params=_cparams(1),
    )(*[a for a, _, _ in rows], *consts)


def rowwise_bwd(name, f, rows, consts, cts, row_grad_dtypes, tm):
    n_r, n_c, n_ct = len(rows), len(consts), len(cts)
    t = rows[0][0].shape[0]
    keep = [k for k, dt in enumerate(row_grad_dtypes) if dt is not None]

    def body(*refs):
        ins = [r[...].astype(F32) for r in refs[:n_r + n_c]]
        g_out = [r[...].astype(F32) for r in refs[n_r + n_c:n_r + n_c + n_ct]]
        out_refs = refs[n_r + n_c + n_ct:]

        def fw(*a):
            res = f(*a)
            return tuple(res) if isinstance(res, (tuple, list)) else (res,)

        _, vjp = jax.vjp(fw, *ins)
        grads = vjp(tuple(g_out))
        for o_ref, k in zip(out_refs[:len(keep)], keep):
            o_ref[...] = grads[k].astype(o_ref.dtype)
        first = pl.program_id(0) == 0
        for o_ref, g in zip(out_refs[len(keep):], grads[n_r:]):
            @pl.when(first)
            def _(o_ref=o_ref, g=g):
                o_ref[...] = g

            @pl.when(jnp.logical_not(first))
            def _(o_ref=o_ref, g=g):
                o_ref[...] += g

    return pl.pallas_call(
        body, name=name, grid=(t // tm,),
        in_specs=[_row_spec(tm, w, cb) for _, w, cb in rows] + [_const_spec(c.shape) for c in consts]
        + [_row_spec(tm, w, cb) for _, w, cb in cts],
        out_specs=[_row_spec(tm, rows[k][1], 0) for k in keep] + [_const_spec(c.shape) for c in consts],
        out_shape=[jax.ShapeDtypeStruct((t, rows[k][1]), row_grad_dtypes[k]) for k in keep]
        + [jax.ShapeDtypeStruct(c.shape, F32) for c in consts],
        compiler_params=_cparams(1),
    )(*[a for a, _, _ in rows], *consts, *[a for a, _, _ in cts])


def f_norm_mod(x, g, scale, shift):
    y = x * lax.rsqrt(jnp.mean(x * x, axis=-1, keepdims=True) + EPS) * g
    return y * (1.0 + scale) + shift


def f_conf_tail(u, zb, ln_g, ln_b, pw2_w, pw2_b):
    mu = jnp.mean(u, axis=-1, keepdims=True)
    xc = u - mu
    var = jnp.mean(xc * xc, axis=-1, keepdims=True)
    y = _silu(xc * lax.rsqrt(var + EPS) * ln_g + ln_b)
    return (mm(y, pw2_w) + pw2_b) * _silu(zb)


def f_merge(ya, yb, yc, mg, x, gate, wpa, wpb, wpc, wout):
    d = D_MODEL
    merged = (_sigmoid(mg[:, :d]) * mm(ya, wpa) + _sigmoid(mg[:, d:2 * d]) * mm(yb, wpb)
              + _sigmoid(mg[:, 2 * d:]) * mm(yc, wpc))
    return x + gate * mm(merged, wout)


def matmul_nn(name, a, b, out_dtype, tm, tn, tk, b_transposed=False):
    m, k = a.shape
    n = b.shape[0] if b_transposed else b.shape[1]
    nk = k // tk
    b_spec = (pl.BlockSpec((tn, tk), lambda i, j, kk: (j, kk)) if b_transposed
              else pl.BlockSpec((tk, tn), lambda i, j, kk: (kk, j)))

    def body(a_ref, b_ref, o_ref, *acc):
        part = lax.dot_general(a_ref[...].astype(BF16), b_ref[...].astype(BF16),
                               (((1,), (1 if b_transposed else 0,)), ((), ())), preferred_element_type=F32)
        if nk == 1:
            o_ref[...] = part.astype(o_ref.dtype)
            return
        kk = pl.program_id(2)
        acc_ref = acc[0]

        @pl.when(kk == 0)
        def _():
            acc_ref[...] = part

        @pl.when(kk > 0)
        def _():
            acc_ref[...] += part

        @pl.when(kk == nk - 1)
        def _():
            o_ref[...] = acc_ref[...].astype(o_ref.dtype)

    return pl.pallas_call(
        body, name=name, grid=(m // tm, n // tn, nk),
        in_specs=[pl.BlockSpec((tm, tk), lambda i, j, kk: (i, kk)), b_spec],
        out_specs=pl.BlockSpec((tm, tn), lambda i, j, kk: (i, j)),
        out_shape=jax.ShapeDtypeStruct((m, n), out_dtype),
        scratch_shapes=[] if nk == 1 else [pltpu.VMEM((tm, tn), F32)],
        compiler_params=_cparams(3),
    )(a, b)


def ada_fwd(c8, w_shard, b_ada):
    n_cols = w_shard.shape[2]
    masks = [(m >> 2 & 1, m >> 1 & 1, m & 1) for m in range(1, 8)]

    def body(c_ref, w_ref, b_ref, mod_ref, conds_ref, cbuf, sendbuf, recvbuf, send_sems, recv_sems):
        x, y, c, chips = _place()
        flip = lambda v, bit: 1 - v if bit else v
        peers = [(flip(x, mx), flip(y, my), flip(c, mc)) for mx, my, mc in masks]
        dev = lambda p: 4 * p[0] + 2 * p[1] + p[2]
        cbuf[dev((x, y, c))] = c_ref[...]
        first = [_remote(c_ref, cbuf.at[dev((x, y, c))], send_sems.at[i], recv_sems.at[i], p) for i, p in enumerate(peers)]
        for cp in first:
            cp.start()
        for i, p in enumerate(peers):
            _remote(c_ref, cbuf.at[dev(p)], send_sems.at[i], recv_sems.at[i], p).wait_recv()
        conds = jnp.concatenate([cbuf[d, 0:1, :] for d in range(8)], axis=0)
        conds_ref[...] = conds
        act = _silu(conds)
        parts = [mm(act, w_ref[l]) for l in range(DEPTH)]
        row8 = lax.broadcasted_iota(jnp.int32, (8, 1), 0)

        def tile_for(chip):
            r = 2 * (2 * chip[0] + chip[1]) + c
            rows = [jnp.sum(jnp.where(row8 == r, parts[l], 0.0), axis=0, keepdims=True) for l in range(DEPTH)]
            return jnp.where(row8 == 0, rows[0], jnp.where(row8 == 1, rows[1], 0.0))

        my_slot = 2 * x + y
        recvbuf[my_slot] = tile_for((x, y))
        second = []
        for j, chip in enumerate(chips):
            sendbuf[j] = tile_for(chip)
            second.append(_remote(sendbuf.at[j], recvbuf.at[my_slot], send_sems.at[7 + j], recv_sems.at[7 + j], (*chip, c)))
            second[-1].start()
        for j, chip in enumerate(chips):
            _remote(sendbuf.at[j], recvbuf.at[2 * chip[0] + chip[1]], send_sems.at[7 + j], recv_sems.at[7 + j],
                    (*chip, c)).wait_recv()
        rows = [jnp.concatenate([recvbuf[k, l:l + 1, :] for k in range(N_CHIPS)], axis=1) + b_ref[l:l + 1, :]
                for l in range(DEPTH)]
        mod_ref[...] = jnp.concatenate(rows + [jnp.zeros((8 - DEPTH, N_CHIPS * n_cols), F32)], axis=0)
        for cp in first + second:
            cp.wait_send()

    vm = pl.BlockSpec(memory_space=pltpu.VMEM)
    return pl.pallas_call(
        body, name="ada_fwd",
        out_shape=[jax.ShapeDtypeStruct((8, N_CHIPS * n_cols), F32), jax.ShapeDtypeStruct((8, D_MODEL), F32)],
        in_specs=[vm, vm, vm], out_specs=[vm, vm],
        scratch_shapes=[pltpu.VMEM((8, 8, D_MODEL), F32), pltpu.VMEM((3, 8, n_cols), F32),
                        pltpu.VMEM((N_CHIPS, 8, n_cols), F32), pltpu.SemaphoreType.DMA((10,)), pltpu.SemaphoreType.DMA((10,))],
        compiler_params=pltpu.CompilerParams(vmem_limit_bytes=VMEM_LIMIT),
    )(c8, w_shard, b_ada)


def ada_bwd(conds, dmod):
    def body(c_ref, d_ref, o_ref):
        act = _silu(c_ref[...])
        for l in range(DEPTH):
            o_ref[l] = mm_tn(act, d_ref[l])

    return pl.pallas_call(
        body, name="ada_bwd", out_shape=jax.ShapeDtypeStruct((DEPTH, D_MODEL, dmod.shape[2]), F32),
        compiler_params=pltpu.CompilerParams(vmem_limit_bytes=VMEM_LIMIT),
    )(conds, dmod)


def _f_attn(first_block, q, za, kc, vc, kp, vp, qg, kg, sinks):
    w = WINDOW
    lane = lax.broadcasted_iota(jnp.int32, (1, 128), 1)
    halves = [lane < 64, lane >= 64]

    def rms_halves(x, g):
        x2 = x * x
        s0 = jnp.sum(jnp.where(halves[0], x2, 0.0), axis=-1, keepdims=True)
        s1 = jnp.sum(jnp.where(halves[1], x2, 0.0), axis=-1, keepdims=True)
        r = jnp.where(halves[0], lax.rsqrt(s0 / 64.0 + EPS), lax.rsqrt(s1 / 64.0 + EPS))
        return x * r * g

    kcat = rms_halves(jnp.concatenate([kp, kc], axis=0), kg)
    vcat = jnp.concatenate([vp, vc], axis=0)
    qi = lax.broadcasted_iota(jnp.int32, (w, 2 * w), 0)
    kj = lax.broadcasted_iota(jnp.int32, (w, 2 * w), 1)
    dist = qi + w - kj
    valid = (dist >= 0) & (dist < w) & (jnp.logical_not(first_block) | (kj >= w))
    distf = dist.astype(F32)
    units = [(grp, half) for grp in range(4) for half in range(2)]
    qns = [rms_halves(q[:, 128 * grp:128 * grp + 128], qg) * (ATT_HEAD_DIM ** -0.5) for grp in range(4)]
    vhalf = [jnp.where(halves[half], vcat, 0.0) for half in range(2)]
    scores, sinks_h = [], []
    for grp, half in units:
        head = HEAD_ORDER[2 * grp + half]
        slope = 2.0 ** (-8.0 * (head + 1) / ATT_HEADS)
        sinks_h.append(jnp.sum(jnp.where(lane == head, sinks, 0.0), axis=-1, keepdims=True))
        s = mm_nt(jnp.where(halves[half], qns[grp], 0.0), kcat) - slope * distf
        scores.append(jnp.where(valid, s, NEG_INF))
    probs = []
    for s, sink in zip(scores, sinks_h):
        m = lax.stop_gradient(jnp.maximum(jnp.max(s, axis=-1, keepdims=True), sink))
        p = jnp.exp(s - m)
        probs.append(p / (jnp.sum(p, axis=-1, keepdims=True) + jnp.exp(sink - m)))
    outs = [mm(p, vhalf[half]) for p, (grp, half) in zip(probs, units)]
    return jnp.concatenate([outs[2 * grp] + outs[2 * grp + 1] for grp in range(4)], axis=1) * _silu(za)


def attn_fwd(name, proj, qg, kg, sinks):
    t = proj.shape[0]
    nb = t // WINDOW

    def body(q_ref, za_ref, kc_ref, vc_ref, kp_ref, vp_ref, qg_ref, kg_ref, s_ref, o_ref):
        first = pl.program_id(0) == 0
        o_ref[...] = _f_attn(first, q_ref[...], za_ref[...], kc_ref[...], vc_ref[...], kp_ref[...], vp_ref[...],
                             qg_ref[...], kg_ref[...], s_ref[...])

    cur = lambda cb: (lambda i: (i, cb))
    prev = lambda cb: (lambda i: (jnp.maximum(i - 1, 0), cb))
    return pl.pallas_call(
        body, name=name, grid=(nb,),
        in_specs=[pl.BlockSpec((WINDOW, 512), cur(P_QA // 512)), pl.BlockSpec((WINDOW, 512), cur(P_ZA // 512)),
                  pl.BlockSpec((WINDOW, 128), cur(P_KA // 128)), pl.BlockSpec((WINDOW, 128), cur(P_VA // 128)),
                  pl.BlockSpec((WINDOW, 128), prev(P_KA // 128)), pl.BlockSpec((WINDOW, 128), prev(P_VA // 128)),
                  _const_spec((1, 128)), _const_spec((1, 128)), _const_spec((1, 128))],
        out_specs=pl.BlockSpec((WINDOW, 512), lambda i: (i, 0)),
        out_shape=jax.ShapeDtypeStruct((t, 512), F32),
        compiler_params=_cparams(1),
    )(proj, proj, proj, proj, proj, proj, qg, kg, sinks)


def attn_bwd(name, proj, qg, kg, sinks, dya):
    t = proj.shape[0]
    nb = t // WINDOW

    def body(q_ref, za_ref, kc_ref, vc_ref, kp_ref, vp_ref, qg_ref, kg_ref, s_ref, dy_ref,
             dqz_ref, dkv_ref, dqg_ref, dkg_ref, ds_ref, carry_ref):
        j = pl.program_id(0)
        first = j == nb - 1

        @pl.when(j == 0)
        def _():
            carry_ref[...] = jnp.zeros_like(carry_ref)
            dqg_ref[...] = jnp.zeros_like(dqg_ref)
            dkg_ref[...] = jnp.zeros_like(dkg_ref)
            ds_ref[...] = jnp.zeros_like(ds_ref)

        ins = [r[...] for r in (q_ref, za_ref, kc_ref, vc_ref, kp_ref, vp_ref, qg_ref, kg_ref, s_ref)]
        _, vjp = jax.vjp(functools.partial(_f_attn, first), *ins)
        dq, dza, dkc, dvc, dkp, dvp, dqg, dkg, dsk = vjp(dy_ref[...])
        dqz_ref[:, 0:512] = dq.astype(dqz_ref.dtype)
        dqz_ref[:, 512:1024] = dza.astype(dqz_ref.dtype)
        dkv_ref[:, 0:128] = (dkc + carry_ref[0]).astype(dkv_ref.dtype)
        dkv_ref[:, 128:256] = (dvc + carry_ref[1]).astype(dkv_ref.dtype)
        carry_ref[0] = dkp
        carry_ref[1] = dvp
        dqg_ref[...] += dqg
        dkg_ref[...] += dkg
        ds_ref[...] += dsk

    cur = lambda cb: (lambda j: (nb - 1 - j, cb))
    prev = lambda cb: (lambda j: (jnp.maximum(nb - 2 - j, 0), cb))
    return pl.pallas_call(
        body, name=name, grid=(nb,),
        in_specs=[pl.BlockSpec((WINDOW, 512), cur(P_QA // 512)), pl.BlockSpec((WINDOW, 512), cur(P_ZA // 512)),
                  pl.BlockSpec((WINDOW, 128), cur(P_KA // 128)), pl.BlockSpec((WINDOW, 128), cur(P_VA // 128)),
                  pl.BlockSpec((WINDOW, 128), prev(P_KA // 128)), pl.BlockSpec((WINDOW, 128), prev(P_VA // 128)),
                  _const_spec((1, 128)), _const_spec((1, 128)), _const_spec((1, 128)),
                  pl.BlockSpec((WINDOW, 512), cur(0))],
        out_specs=[pl.BlockSpec((WINDOW, 1024), cur(0)), pl.BlockSpec((WINDOW, 256), cur(0)),
                   _const_spec((1, 128)), _const_spec((1, 128)), _const_spec((1, 128))],
        out_shape=[jax.ShapeDtypeStruct((t, 1024), BF16), jax.ShapeDtypeStruct((t, 256), BF16),
                   jax.ShapeDtypeStruct((1, 128), F32), jax.ShapeDtypeStruct((1, 128), F32),
                   jax.ShapeDtypeStruct((1, 128), F32)],
        scratch_shapes=[pltpu.VMEM((2, WINDOW, 128), F32)],
        compiler_params=_cparams(1),
    )(proj, proj, proj, proj, proj, proj, qg, kg, sinks, dya)


CONV_ROWS = 256


def _conv_taps(src_ref, w_ref, n_taps, base, t):
    for r0 in range(0, t, CONV_ROWS):
        acc = w_ref[0:1, :] * src_ref[pl.ds(r0 + base, CONV_ROWS), :]
        for k in range(1, n_taps):
            acc = acc + w_ref[k:k + 1, :] * src_ref[pl.ds(r0 + base + k, CONV_ROWS), :]
        yield r0, acc


def _conv_wgrad(dy_ref, src_ref, n_taps, base, t, dy_base=0):
    out = []
    for k in range(n_taps):
        acc = jnp.zeros((8, 128), F32)
        for r0 in range(0, t, CONV_ROWS):
            prod = dy_ref[pl.ds(r0 + dy_base, CONV_ROWS), :] * src_ref[pl.ds(r0 + base + k, CONV_ROWS), :]
            acc = acc + jnp.sum(prod.reshape(CONV_ROWS // 8, 8, 128), axis=0)
        out.append(jnp.sum(acc, axis=0, keepdims=True))
    return out


def glu_conv_fwd(name, proj, w32, bias):
    t = proj.shape[0]
    pad = 32

    def body(x_ref, w_ref, b_ref, o_ref, u_ref):
        u_ref[0:pad, :] = jnp.zeros((pad, 128), F32)
        u_ref[pad:pad + t, :] = x_ref[:, 0:128] * _sigmoid(x_ref[:, 128:256])
        for r0, acc in _conv_taps(u_ref, w_ref, CONV_K, pad - (CONV_K - 1), t):
            o_ref[pl.ds(r0, CONV_ROWS), :] = acc + b_ref[...]

    return pl.pallas_call(
        body, name=name, grid=(4,),
        in_specs=[pl.BlockSpec((t, 256), lambda cb: (0, P_GLU // 256 + cb)), pl.BlockSpec((32, 128), lambda cb: (0, cb)),
                  pl.BlockSpec((1, 128), lambda cb: (0, cb))],
        out_specs=pl.BlockSpec((t, 128), lambda cb: (0, cb)),
        out_shape=jax.ShapeDtypeStruct((t, 512), F32),
        scratch_shapes=[pltpu.VMEM((t + pad, 128), F32)],
        compiler_params=_cparams(1),
    )(proj, w32, bias)


def glu_conv_bwd(name, proj, w32, dub):
    t = proj.shape[0]
    pad = 32
    k1 = CONV_K - 1

    def body(x_ref, w_ref, dy_ref, dx_ref, dw_ref, db_ref, u_ref, dyp_ref, wrev_ref):
        val = x_ref[:, 0:128]
        sg = _sigmoid(x_ref[:, 128:256])
        u_ref[0:pad, :] = jnp.zeros((pad, 128), F32)
        u_ref[pad:pad + t, :] = val * sg
        dyp_ref[0:t, :] = dy_ref[...]
        dyp_ref[t:t + pad, :] = jnp.zeros((pad, 128), F32)
        for k in range(CONV_K):
            wrev_ref[k:k + 1, :] = w_ref[k1 - k:k1 - k + 1, :]
        wrev_ref[CONV_K:32, :] = jnp.zeros((32 - CONV_K, 128), F32)
        for r0, du in _conv_taps(dyp_ref, wrev_ref, CONV_K, 0, t):
            v = x_ref[pl.ds(r0, CONV_ROWS), 0:128]
            s = _sigmoid(x_ref[pl.ds(r0, CONV_ROWS), 128:256])
            dx_ref[pl.ds(r0, CONV_ROWS), 0:128] = (du * s).astype(dx_ref.dtype)
            dx_ref[pl.ds(r0, CONV_ROWS), 128:256] = (du * v * s * (1.0 - s)).astype(dx_ref.dtype)
        dws = _conv_wgrad(dyp_ref, u_ref, CONV_K, pad - k1, t)
        for k in range(CONV_K):
            dw_ref[k:k + 1, :] = dws[k]
        dw_ref[CONV_K:32, :] = jnp.zeros((32 - CONV_K, 128), F32)
        db_ref[...] = jnp.sum(dy_ref[...], axis=0, keepdims=True)

    return pl.pallas_call(
        body, name=name, grid=(4,),
        in_specs=[pl.BlockSpec((t, 256), lambda cb: (0, P_GLU // 256 + cb)), pl.BlockSpec((32, 128), lambda cb: (0, cb)),
                  pl.BlockSpec((t, 128), lambda cb: (0, cb))],
        out_specs=[pl.BlockSpec((t, 256), lambda cb: (0, cb)), pl.BlockSpec((32, 128), lambda cb: (0, cb)),
                   pl.BlockSpec((1, 128), lambda cb: (0, cb))],
        out_shape=[jax.ShapeDtypeStruct((t, 1024), BF16), jax.ShapeDtypeStruct((32, 512), F32),
                   jax.ShapeDtypeStruct((1, 512), F32)],
        scratch_shapes=[pltpu.VMEM((t + pad, 128), F32), pltpu.VMEM((t + pad, 128), F32), pltpu.VMEM((32, 128), F32)],
        compiler_params=_cparams(1),
    )(proj, w32, dub)


def sconv_fwd(name, proj, w8):
    t = proj.shape[0]
    pad = 8
    k1 = DN_CONV_K - 1

    def body(x_ref, w_ref, o_ref, xp_ref):
        xp_ref[0:pad, :] = jnp.zeros((pad, 128), F32)
        xp_ref[pad:pad + t, :] = x_ref[...]
        for r0, acc in _conv_taps(xp_ref, w_ref, DN_CONV_K, pad - k1, t):
            o_ref[pl.ds(r0, CONV_ROWS), :] = _silu(acc)

    return pl.pallas_call(
        body, name=name, grid=(12,),
        in_specs=[pl.BlockSpec((t, 128), lambda cb: (0, P_QKV // 128 + cb)), pl.BlockSpec((8, 128), lambda cb: (0, cb))],
        out_specs=pl.BlockSpec((t, 128), lambda cb: (0, cb)),
        out_shape=jax.ShapeDtypeStruct((t, 1536), F32),
        scratch_shapes=[pltpu.VMEM((t + pad, 128), F32)],
        compiler_params=_cparams(1),
    )(proj, w8)


def sconv_bwd(name, proj, w8, dqkv):
    t = proj.shape[0]
    pad = 8
    k1 = DN_CONV_K - 1

    def body(x_ref, w_ref, dy_ref, dx_ref, dw_ref, xp_ref, dpp_ref, wrev_ref):
        xp_ref[0:pad, :] = jnp.zeros((pad, 128), F32)
        xp_ref[pad:pad + t, :] = x_ref[...]
        for r0, pre in _conv_taps(xp_ref, w_ref, DN_CONV_K, pad - k1, t):
            s = _sigmoid(pre)
            dpp_ref[pl.ds(r0, CONV_ROWS), :] = dy_ref[pl.ds(r0, CONV_ROWS), :] * (s * (1.0 + pre * (1.0 - s)))
        dpp_ref[t:t + pad, :] = jnp.zeros((pad, 128), F32)
        for k in range(DN_CONV_K):
            wrev_ref[k:k + 1, :] = w_ref[k1 - k:k1 - k + 1, :]
        wrev_ref[DN_CONV_K:8, :] = jnp.zeros((8 - DN_CONV_K, 128), F32)
        for r0, dx in _conv_taps(dpp_ref, wrev_ref, DN_CONV_K, 0, t):
            dx_ref[pl.ds(r0, CONV_ROWS), :] = dx.astype(dx_ref.dtype)
        dws = _conv_wgrad(dpp_ref, xp_ref, DN_CONV_K, pad - k1, t)
        for k in range(DN_CONV_K):
            dw_ref[k:k + 1, :] = dws[k]
        dw_ref[DN_CONV_K:8, :] = jnp.zeros((8 - DN_CONV_K, 128), F32)

    return pl.pallas_call(
        body, name=name, grid=(12,),
        in_specs=[pl.BlockSpec((t, 128), lambda cb: (0, P_QKV // 128 + cb)), pl.BlockSpec((8, 128), lambda cb: (0, cb)),
                  pl.BlockSpec((t, 128), lambda cb: (0, cb))],
        out_specs=[pl.BlockSpec((t, 128), lambda cb: (0, cb)), pl.BlockSpec((8, 128), lambda cb: (0, cb))],
        out_shape=[jax.ShapeDtypeStruct((t, 1536), BF16), jax.ShapeDtypeStruct((8, 1536), F32)],
        scratch_shapes=[pltpu.VMEM((t + pad, 128), F32), pltpu.VMEM((t + pad, 128), F32), pltpu.VMEM((8, 128), F32)],
        compiler_params=_cparams(1),
    )(proj, w8, dqkv)


def _f_delta_step(qkv, ab, zc, s0, s1, s2, s3, a_log, dt_bias, dn_g, inverses=None, with_inverses=False):
    cs = DN_CHUNK
    n = 2 * cs
    states = (s0, s1, s2, s3)
    lane = lax.broadcasted_iota(jnp.int32, (1, 128), 1)
    ri = lax.broadcasted_iota(jnp.int32, (n, n), 0)
    ci = lax.broadcasted_iota(jnp.int32, (n, n), 1)
    same = (ri // cs) == (ci // cs)
    lower = same & (ri >= ci)
    strict = same & (ri > ci)
    sums = jnp.concatenate([jnp.where(lower, 1.0, 0.0), jnp.where(same, 1.0, 0.0), jnp.where(ci < cs, 1.0, 0.0),
                            jnp.where(ci >= cs, 1.0, 0.0)], axis=0)
    top = lax.broadcasted_iota(jnp.int32, (n, 1), 0) < cs

    def pick(row, idx):
        return jnp.sum(jnp.where(lane == idx, row, 0.0), axis=-1, keepdims=True)

    def l2n(x):
        return x * lax.rsqrt(jnp.sum(x * x, axis=-1, keepdims=True) + EPS)

    n_chunks = qkv.shape[0] // cs
    units = [(k, pair) for k in range(n_chunks) for pair in range(2)]

    pre = []
    for k, pair in units:
        hs = (2 * pair, 2 * pair + 1)
        rows = slice(k * cs, (k + 1) * cs)
        stack = lambda f: jnp.concatenate([f(hs[0]), f(hs[1])], axis=0)
        qd = l2n(stack(lambda h: qkv[rows, 128 * h:128 * h + 128])) * (128 ** -0.5)
        kd = l2n(stack(lambda h: qkv[rows, 512 + 128 * h:512 + 128 * h + 128]))
        vd = stack(lambda h: qkv[rows, 1024 + 128 * h:1024 + 128 * h + 128])
        beta = _sigmoid(stack(lambda h: pick(ab[rows], 4 + h)))
        g = stack(lambda h: -jnp.exp(pick(a_log, h)) * _softplus(pick(ab[rows], h) + pick(dt_bias, h)))
        g_sums = sel_mm(sums, g * jnp.ones((1, n), F32))
        gc_col = g_sums[0:n]
        gl_b = g_sums[n:2 * n]
        g_end = (g_sums[2 * n:3 * n], g_sums[3 * n:])
        decay = jnp.where(lower, jnp.exp(jnp.where(lower, gc_col - gc_col.T, 0.0)), 0.0)
        kb = kd * beta
        pre.append(dict(qd=qd, kd=kd, vb=vd * beta, kb=kb, gc_col=gc_col, gl_b=gl_b, g_end=g_end, decay=decay,
                        a=jnp.where(strict, mm_nt(kb, kd) * decay, 0.0)))
    if inverses is None:
        tmats = tri_inv(*[p["a"] for p in pre])
    else:
        tmats = [tri_inv_known(p["a"], t) for p, t in zip(pre, inverses)]

    mid = []
    for p, tmat in zip(pre, tmats):
        egc = jnp.exp(p["gc_col"])
        mid.append(dict(u=mm(tmat, p["vb"]), wm=mm(tmat, p["kb"] * egc), qe=p["qd"] * egc,
                        intra=jnp.where(lower, mm_nt(p["qd"], p["kd"]) * p["decay"], 0.0),
                        ke=p["kd"] * jnp.exp(p["gl_b"] - p["gc_col"]), g_end=p["g_end"]))

    ys = []
    for k in range(n_chunks):
        rows = slice(k * cs, (k + 1) * cs)
        new_states, y_heads = [], []
        for pair in range(2):
            m = mid[2 * k + pair]
            hs = (2 * pair, 2 * pair + 1)
            st = (states[hs[0]], states[hs[1]])
            v_new = m["u"] - jnp.concatenate([mm(m["wm"][:cs], st[0]), mm(m["wm"][cs:], st[1])], axis=0)
            o = jnp.concatenate([mm(m["qe"][:cs], st[0]), mm(m["qe"][cs:], st[1])], axis=0) + mm(m["intra"], v_new)
            new_states.append(st[0] * jnp.exp(m["g_end"][0]) + mm_tn(jnp.where(top, m["ke"], 0.0), v_new))
            new_states.append(st[1] * jnp.exp(m["g_end"][1]) + mm_tn(jnp.where(top, 0.0, m["ke"]), v_new))
            od = o * lax.rsqrt(jnp.mean(o * o, axis=-1, keepdims=True) + EPS) * dn_g
            y_heads += [od[:cs] * _silu(zc[rows, 128 * hs[0]:128 * hs[0] + 128]),
                        od[cs:] * _silu(zc[rows, 128 * hs[1]:128 * hs[1] + 128])]
        states = tuple(new_states)
        ys.append(jnp.concatenate(y_heads, axis=1))
    if with_inverses:
        return (jnp.concatenate(ys, axis=0), *states), tmats
    return (jnp.concatenate(ys, axis=0), *states)


DELTA_ROWS = 4 * DN_CHUNK
DELTA_UNITS = 2 * DELTA_ROWS // DN_CHUNK


def delta_fwd(name, qkv, proj, a_log, dt_bias, dn_g):
    t = qkv.shape[0]
    nc = t // DELTA_ROWS

    def body(qkv_ref, ab_ref, zc_ref, al_ref, dt_ref, g_ref, y_ref, ssave_ref, tsave_ref, s_ref):
        @pl.when(pl.program_id(0) == 0)
        def _():
            s_ref[...] = jnp.zeros_like(s_ref)

        ssave_ref[0] = s_ref[...]
        st = [s_ref[128 * h:128 * h + 128, :] for h in range(4)]
        (y, *ns), tmats = _f_delta_step(qkv_ref[...], ab_ref[...], zc_ref[...], *st, al_ref[...], dt_ref[...], g_ref[...],
                                        with_inverses=True)
        y_ref[...] = y
        for h in range(4):
            s_ref[128 * h:128 * h + 128, :] = ns[h]
        for u, tm in enumerate(tmats):
            tsave_ref[0, 128 * u:128 * u + 128, :] = tm

    return pl.pallas_call(
        body, name=name, grid=(nc,),
        in_specs=[pl.BlockSpec((DELTA_ROWS, 1536), lambda i: (i, 0)), pl.BlockSpec((DELTA_ROWS, 128), lambda i: (i, P_AB // 128)),
                  pl.BlockSpec((DELTA_ROWS, 512), lambda i: (i, P_ZC // 512)),
                  _const_spec((1, 128)), _const_spec((1, 128)), _const_spec((1, 128))],
        out_specs=[pl.BlockSpec((DELTA_ROWS, 512), lambda i: (i, 0)), pl.BlockSpec((1, 512, 128), lambda i: (i, 0, 0)),
                   pl.BlockSpec((1, DELTA_UNITS * 128, 128), lambda i: (i, 0, 0))],
        out_shape=[jax.ShapeDtypeStruct((t, 512), F32), jax.ShapeDtypeStruct((nc, 512, 128), F32),
                   jax.ShapeDtypeStruct((nc, DELTA_UNITS * 128, 128), F32)],
        scratch_shapes=[pltpu.VMEM((512, 128), F32)],
        compiler_params=_cparams(1),
    )(qkv, proj, proj, a_log, dt_bias, dn_g)


def delta_bwd(name, qkv, proj, ssave, tsave, a_log, dt_bias, dn_g, dyc):
    t = qkv.shape[0]
    nc = t // DELTA_ROWS

    def body(qkv_ref, ab_ref, zc_ref, ss_ref, ts_ref, al_ref, dt_ref, g_ref, dy_ref,
             dqkv_ref, dab_ref, dzc_ref, dal_ref, ddt_ref, dg_ref, ds_ref):
        @pl.when(pl.program_id(0) == 0)
        def _():
            ds_ref[...] = jnp.zeros_like(ds_ref)
            dal_ref[...] = jnp.zeros_like(dal_ref)
            ddt_ref[...] = jnp.zeros_like(ddt_ref)
            dg_ref[...] = jnp.zeros_like(dg_ref)

        st = [ss_ref[0, 128 * h:128 * h + 128, :] for h in range(4)]
        known = [ts_ref[0, 128 * u:128 * u + 128, :] for u in range(DELTA_UNITS)]
        _, vjp = jax.vjp(functools.partial(_f_delta_step, inverses=known), qkv_ref[...], ab_ref[...], zc_ref[...], *st,
                         al_ref[...], dt_ref[...], g_ref[...])
        dst = tuple(ds_ref[128 * h:128 * h + 128, :] for h in range(4))
        dqkv, dab, dzc, d0, d1, d2, d3, dal, ddt, dg = vjp((dy_ref[...], *dst))
        dqkv_ref[...] = dqkv
        dab_ref[...] = dab.astype(dab_ref.dtype)
        dzc_ref[...] = dzc.astype(dzc_ref.dtype)
        for h, d in enumerate((d0, d1, d2, d3)):
            ds_ref[128 * h:128 * h + 128, :] = d
        dal_ref[...] += dal
        ddt_ref[...] += ddt
        dg_ref[...] += dg

    rev = lambda cb: (lambda j: (nc - 1 - j, cb))
    return pl.pallas_call(
        body, name=name, grid=(nc,),
        in_specs=[pl.BlockSpec((DELTA_ROWS, 1536), rev(0)), pl.BlockSpec((DELTA_ROWS, 128), rev(P_AB // 128)),
                  pl.BlockSpec((DELTA_ROWS, 512), rev(P_ZC // 512)), pl.BlockSpec((1, 512, 128), lambda j: (nc - 1 - j, 0, 0)),
                  pl.BlockSpec((1, DELTA_UNITS * 128, 128), lambda j: (nc - 1 - j, 0, 0)),
                  _const_spec((1, 128)), _const_spec((1, 128)), _const_spec((1, 128)),
                  pl.BlockSpec((DELTA_ROWS, 512), rev(0))],
        out_specs=[pl.BlockSpec((DELTA_ROWS, 1536), rev(0)), pl.BlockSpec((DELTA_ROWS, 128), rev(0)),
                   pl.BlockSpec((DELTA_ROWS, 512), rev(0)),
                   _const_spec((1, 128)), _const_spec((1, 128)), _const_spec((1, 128))],
        out_shape=[jax.ShapeDtypeStruct((t, 1536), F32), jax.ShapeDtypeStruct((t, 128), BF16),
                   jax.ShapeDtypeStruct((t, 512), BF16),
                   jax.ShapeDtypeStruct((1, 128), F32), jax.ShapeDtypeStruct((1, 128), F32), jax.ShapeDtypeStruct((1, 128), F32)],
        scratch_shapes=[pltpu.VMEM((512, 128), F32)],
        compiler_params=_cparams(1),
    )(qkv, proj, proj, ssave, tsave, a_log, dt_bias, dn_g, dyc)


def loss_head(name, y, target, tm):
    t, d = y.shape

    def body(y_ref, t_ref, dy_ref, l_ref):
        err = y_ref[...] - t_ref[...]
        dy_ref[...] = err * (1.0 / d)
        part = 0.5 * jnp.sum(jnp.sum(err * err, axis=-1, keepdims=True) * (1.0 / d), axis=0, keepdims=True)

        @pl.when(pl.program_id(0) == 0)
        def _():
            l_ref[...] = part

        @pl.when(pl.program_id(0) > 0)
        def _():
            l_ref[...] += part

    return pl.pallas_call(
        body, name=name, grid=(t // tm,),
        in_specs=[_row_spec(tm, d, 0), _row_spec(tm, d, 0)],
        out_specs=[_row_spec(tm, d, 0), _const_spec((1, 1))],
        out_shape=[jax.ShapeDtypeStruct((t, d), F32), jax.ShapeDtypeStruct((1, 1), F32)],
        compiler_params=_cparams(1),
    )(y, target)


TM = 512
TM_MERGE = 256
TM_IN = 1024
TN_IN = 1152


def _lane_pad(v, n=128):
    return jnp.pad(v.astype(F32), (0, n - v.shape[0]))[None, :]


def f_norm_mod_res(x, g, scale, shift):
    return f_norm_mod(x, g, scale, shift), x


def prep_layer(w):
    p = dict(w)
    p["wp"] = _w_in_assemble(w["w_in"])
    p["wpa"] = _perm_heads_rows(w["w_proj_a"])
    p["dw32"] = jnp.pad(w["dw_w"], ((0, 32 - CONV_K), (0, 0)))
    p["sconv8"] = jnp.pad(w["sconv_w"], ((0, 8 - DN_CONV_K), (0, 0)))
    p["qg"] = jnp.tile(w["q_norm_g"], 2)[None, :]
    p["kg"] = jnp.tile(w["k_norm_g"], 2)[None, :]
    p["sinks128"] = _lane_pad(w["sinks"])
    p["al"] = _lane_pad(w["a_log"])
    p["dtb"] = _lane_pad(w["dt_bias"])
    p["dng"] = w["dn_norm_g"][None, :]
    return p


def layer_fwd(tag, x, mod, p):
    d = D_MODEL
    shift, scale, gate = mod[:, :d], mod[:, d:2 * d], mod[:, 2 * d:]
    g = p["norm_g"][None, :]
    (h,) = rowwise_fwd(f"norm_fwd{tag}", f_norm_mod, [(x, d, 0)], [g, scale, shift], [(d, BF16)], TM)
    proj = matmul_nn(f"inproj_fwd{tag}", h, p["wp"], F32, TM_IN, TN_IN, d)
    ya = attn_fwd(f"attn_fwd{tag}", proj, p["qg"], p["kg"], p["sinks128"])
    ub = glu_conv_fwd(f"glu_conv_fwd{tag}", proj, p["dw32"], p["dw_b"][None, :])
    conf_consts = [p["ln_g"][None, :], p["ln_b"][None, :], p["pw2_w"], p["pw2_b"][None, :]]
    (yb,) = rowwise_fwd(f"conf_fwd{tag}", f_conf_tail, [(ub, 512, 0), (proj, 512, P_ZB // 512)], conf_consts, [(512, F32)], TM)
    qkv = sconv_fwd(f"sconv_fwd{tag}", proj, p["sconv8"])
    yc, ssave, tsave = delta_fwd(f"delta_fwd{tag}", qkv, proj, p["al"], p["dtb"], p["dng"])
    merge_consts = [gate, p["wpa"], p["w_proj_b"], p["w_proj_c"], p["w_out"]]
    merge_rows = [(ya, 512, 0), (yb, 512, 0), (yc, 512, 0), (proj, 3 * d, P_MG // (3 * d)), (x, d, 0)]
    (xn,) = rowwise_fwd(f"merge_fwd{tag}", f_merge, merge_rows, merge_consts, [(d, F32)], TM_MERGE)
    saved = dict(x=x, h=h, proj=proj, ub=ub, qkv=qkv, ssave=ssave, tsave=tsave, norm_consts=[g, scale, shift],
                 conf_consts=conf_consts, merge_consts=merge_consts, merge_rows=merge_rows)
    return xn, saved


def layer_bwd(tag, dxn, p, s):
    d = D_MODEL
    proj = s["proj"]
    dya, dyb, dyc, dmg, dgate, dwpa, dwpb, dwpc, dwout = rowwise_bwd(
        f"merge_bwd{tag}", f_merge, s["merge_rows"], s["merge_consts"], [(dxn, d, 0)], [F32, F32, F32, BF16, None], TM_MERGE)
    dqz, dkv, dqg, dkg, dsinks = attn_bwd(f"attn_bwd{tag}", proj, p["qg"], p["kg"], p["sinks128"], dya)
    dub, dzb, dln_g, dln_b, dpw2_w, dpw2_b = rowwise_bwd(
        f"conf_bwd{tag}", f_conf_tail, [(s["ub"], 512, 0), (proj, 512, P_ZB // 512)], s["conf_consts"], [(dyb, 512, 0)],
        [F32, BF16], TM)
    dglu, ddw32, ddw_b = glu_conv_bwd(f"glu_conv_bwd{tag}", proj, p["dw32"], dub)
    dqkv, dab, dzc, dal, ddtb, ddng = delta_bwd(f"delta_bwd{tag}", s["qkv"], proj, s["ssave"], s["tsave"], p["al"], p["dtb"],
                                                p["dng"], dyc)
    dqkv_pre, dsconv8 = sconv_bwd(f"sconv_bwd{tag}", proj, p["sconv8"], dqkv)
    dproj = jnp.concatenate([dqz, dglu, dzb, dzc, dmg, dqkv_pre, dkv, dab], axis=1)
    dh = matmul_nn(f"inproj_bwd_dh{tag}", dproj, p["wp"], F32, TM_IN, d, P_TOTAL // 3, b_transposed=True)
    dwp = matmul_nn(f"inproj_bwd_dw{tag}", s["h"].T, dproj, F32, d, TN_IN, 2048)
    dx, dnorm_g, dscale, dshift = rowwise_bwd(
        f"norm_bwd{tag}", f_norm_mod_res, [(s["x"], d, 0)], s["norm_consts"], [(dh, d, 0), (dxn, d, 0)], [F32], TM)
    dmod = jnp.concatenate([dshift, dscale, dgate], axis=1)
    grads = dict(
        b_ada=dmod[0], norm_g=dnorm_g[0], w_in=_w_in_grad_blocks(dwp),
        q_norm_g=dqg[0, :64] + dqg[0, 64:], k_norm_g=dkg[0, :64] + dkg[0, 64:], sinks=dsinks[0, :ATT_HEADS],
        dw_w=ddw32[:CONV_K], dw_b=ddw_b[0], ln_g=dln_g[0], ln_b=dln_b[0], pw2_w=dpw2_w, pw2_b=dpw2_b[0],
        sconv_w=dsconv8[:DN_CONV_K], a_log=dal[0, :DN_HEADS], dt_bias=ddtb[0, :DN_HEADS], dn_norm_g=ddng[0],
        w_proj_a=_unperm_heads_rows(dwpa), w_proj_b=dwpb, w_proj_c=dwpc, w_out=dwout)
    return dx, grads


SHARDED = {"w_ada": 2, "w_in": 2, "dw_w": 2, "pw2_w": 1, "sconv_w": 2, "w_proj_a": 2, "w_proj_b": 2, "w_proj_c": 2,
           "w_out": 1}
GATHERED = tuple(n for n in SHARDED if n != "w_ada")
GATHER_F32 = ("dw_w", "sconv_w")
REDUCE_BIG = tuple(n for n in GATHERED if n not in GATHER_F32)
SMALL = ("b_ada", "norm_g", "q_norm_g", "k_norm_g", "sinks", "dw_b", "ln_g", "ln_b", "pw2_b", "a_log", "dt_bias",
         "dn_norm_g")
SMALL_ROWS = 104
SMALL_GRAD_ROWS = 448
W_IN_SHARD = D_IN // N_CHIPS
SUM_TILE = 256


def _w_in_orig():
    orig = np.full(P_TOTAL, -1, np.int64)
    p = 0
    for s, n in _in_pieces():
        orig[p:p + n] = np.arange(s, s + n)
        p += n
    return orig


def _w_in_blocks(k):
    orig = _w_in_orig().reshape(-1, 128)
    lo, hi = k * W_IN_SHARD, (k + 1) * W_IN_SHARD
    return [b for b in range(orig.shape[0]) if np.any((orig[b] >= lo) & (orig[b] < hi))]


W_IN_BLOCKS = max(len(_w_in_blocks(k)) for k in range(N_CHIPS))


def _runs(idx):
    out, i = [], 0
    while i < len(idx):
        j = i + 1
        while j < len(idx) and ((idx[i] < 0 and idx[j] < 0) or (idx[i] >= 0 and idx[j] == idx[j - 1] + 1)):
            j += 1
        out.append((int(idx[i]) if idx[i] >= 0 else -1, j - i))
        i = j
    return out


def _take_cols(a, idx):
    parts = [jnp.zeros(a.shape[:-1] + (n,), a.dtype) if s < 0 else a[..., s:s + n] for s, n in _runs(idx)]
    return parts[0] if len(parts) == 1 else jnp.concatenate(parts, axis=-1)


def _w_in_send(k, shard):
    orig = _w_in_orig().reshape(-1, 128)
    lo, hi = k * W_IN_SHARD, (k + 1) * W_IN_SHARD
    idx = np.concatenate([np.where((orig[b] >= lo) & (orig[b] < hi), orig[b] - lo, -1) for b in _w_in_blocks(k)])
    idx = np.concatenate([idx, np.full((W_IN_BLOCKS - len(_w_in_blocks(k))) * 128, -1)])
    return _take_cols(shard, idx)


def _w_in_assemble(blocks):
    where = [{b: i for i, b in enumerate(_w_in_blocks(k))} for k in range(N_CHIPS)]
    n_blocks = P_TOTAL // 128
    owners = [[(k, where[k][b]) for k in range(N_CHIPS) if b in where[k]] for b in range(n_blocks)]
    parts, b = [], 0
    while b < n_blocks:
        if len(owners[b]) == 1:
            k, pos = owners[b][0]
            e = b + 1
            while e < n_blocks and owners[e] == [(k, pos + e - b)]:
                e += 1
            parts.append(blocks[k][:, pos * 128:(pos + e - b) * 128])
            b = e
        else:
            parts.append(functools.reduce(jnp.add, [blocks[k][:, pos * 128:(pos + 1) * 128] for k, pos in owners[b]]))
            b += 1
    return jnp.concatenate(parts, axis=1)


def _w_in_grad_blocks(wp):
    out = []
    for k in range(N_CHIPS):
        idx = np.concatenate([np.arange(128 * b, 128 * b + 128) for b in _w_in_blocks(k)])
        idx = np.concatenate([idx, np.full((W_IN_BLOCKS - len(_w_in_blocks(k))) * 128, -1)])
        out.append(_take_cols(wp, idx))
    return jnp.stack(out)


def _w_in_receive_grad(k, blocks):
    orig = _w_in_orig()
    inv = np.zeros(D_IN, np.int64)
    inv[orig[orig >= 0]] = np.nonzero(orig >= 0)[0]
    where = {b: i for i, b in enumerate(_w_in_blocks(k))}
    cols = inv[k * W_IN_SHARD:(k + 1) * W_IN_SHARD]
    return _take_cols(blocks, np.array([where[c // 128] * 128 + c % 128 for c in cols]))


def _join_layer(v, axis):
    if axis == 2:
        return jnp.transpose(v, (1, 0, 2)).reshape(v.shape[1], N_CHIPS * v.shape[2])
    return v.reshape(N_CHIPS * v.shape[1], v.shape[2])


def _split_layer(v, axis):
    a, b = v.shape
    if axis == 2:
        return jnp.transpose(v.reshape(a, N_CHIPS, b // N_CHIPS), (1, 0, 2))
    return v.reshape(N_CHIPS, a // N_CHIPS, b)


def pack_small(vals, names, rows):
    flat = jnp.concatenate([vals[n].astype(F32).reshape(-1) for n in names])
    return jnp.pad(flat, (0, rows * 128 - flat.shape[0])).reshape(rows, 128)


def unpack_small(packed, names, shapes):
    flat = packed.reshape(-1)
    out, off = {}, 0
    for n in names:
        k = int(np.prod(shapes[n]))
        out[n] = flat[off:off + k].reshape(shapes[n])
        off += k
    return out


ANY = pl.BlockSpec(memory_space=pl.ANY)


def _place():
    x, y, c = lax.axis_index("x"), lax.axis_index("y"), lax.axis_index("c")
    chips = [(1 - x, y), (x, 1 - y), (1 - x, 1 - y)]
    return x, y, c, chips


def _remote(src, dst, send_sem, recv_sem, to):
    return pltpu.make_async_remote_copy(src_ref=src, dst_ref=dst, send_sem=send_sem, recv_sem=recv_sem, device_id=to,
                                        device_id_type=MESH)


def weights_allgather(slots):
    n = len(slots)

    def body(*refs):
        out = refs[n:2 * n]
        send_sems, recv_sems = refs[2 * n:]
        x, y, c, chips = _place()
        me, sibling, my_slot = (x, y, c), (x, y, 1 - c), 2 * x + y
        sends = []
        for j, chip in enumerate(chips):
            for t in range(n):
                mine = out[t].at[my_slot, c]
                sends.append(_remote(mine, mine, send_sems.at[t, j], recv_sems.at[t, j], (*chip, c)))
                sends[-1].start()
        for j, chip in enumerate(chips):
            for t in range(n):
                land = out[t].at[2 * chip[0] + chip[1], c]
                _remote(land, land, send_sems.at[t, j], recv_sems.at[t, j], me).wait_recv()
                sends.append(_remote(land, land, send_sems.at[t, 3 + j], recv_sems.at[t, 3 + j], sibling))
                sends[-1].start()
        for j, chip in enumerate(chips):
            for t in range(n):
                land = out[t].at[2 * chip[0] + chip[1], 1 - c]
                _remote(land, land, send_sems.at[t, 3 + j], recv_sems.at[t, 3 + j], me).wait_recv()
        for cp in sends:
            cp.wait_send()

    return pl.pallas_call(
        body, name="weights_allgather", out_shape=[jax.ShapeDtypeStruct(s.shape, s.dtype) for s in slots],
        in_specs=[ANY] * n, out_specs=[ANY] * n, input_output_aliases={t: t for t in range(n)},
        scratch_shapes=[pltpu.SemaphoreType.DMA((n, 6)), pltpu.SemaphoreType.DMA((n, 6))],
    )(*slots)


def grads_pair_exchange(gs):
    n = len(gs)

    def body(*refs):
        g, recv = refs[:n], refs[n:2 * n]
        send_sems, recv_sems = refs[2 * n:]
        x, y, c, _ = _place()
        cps = [_remote(g[t].at[:, 1 - c], recv[t], send_sems.at[t], recv_sems.at[t], (x, y, 1 - c)) for t in range(n)]
        for cp in cps:
            cp.start()
        for cp in cps:
            cp.wait()

    return pl.pallas_call(
        body, name="grads_pair_exchange",
        out_shape=[jax.ShapeDtypeStruct((N_CHIPS,) + g.shape[2:], g.dtype) for g in gs],
        in_specs=[ANY] * n, out_specs=[ANY] * n,
        scratch_shapes=[pltpu.SemaphoreType.DMA((n,)), pltpu.SemaphoreType.DMA((n,))],
    )(*gs)


def grads_pair_sum(name, g, recv):
    _, a, b = recv.shape
    ta = min(a, SUM_TILE)

    def body(a_ref, b_ref, o_ref):
        o_ref[...] = (a_ref[...] + b_ref[...]).astype(o_ref.dtype)

    return pl.pallas_call(
        body, name=name, grid=(N_CHIPS, a // ta),
        in_specs=[pl.BlockSpec((None, None, ta, b), lambda s, i: (s, lax.axis_index("c"), i, 0)),
                  pl.BlockSpec((None, ta, b), lambda s, i: (s, i, 0))],
        out_specs=pl.BlockSpec((None, ta, b), lambda s, i: (s, i, 0)),
        out_shape=jax.ShapeDtypeStruct(recv.shape, BF16),
        compiler_params=_cparams(2),
    )(g, recv)


def grads_chip_exchange(ps):
    n = len(ps)

    def body(*refs):
        p, recv = refs[:n], refs[n:2 * n]
        send_sems, recv_sems = refs[2 * n:]
        x, y, c, chips = _place()
        cps = [_remote(p[t].at[2 * chip[0] + chip[1]], recv[t].at[j], send_sems.at[t, j], recv_sems.at[t, j], (*chip, c))
               for j, chip in enumerate(chips) for t in range(n)]
        for cp in cps:
            cp.start()
        for cp in cps:
            cp.wait()

    return pl.pallas_call(
        body, name="grads_chip_exchange", out_shape=[jax.ShapeDtypeStruct((3,) + p.shape[1:], p.dtype) for p in ps],
        in_specs=[ANY] * n, out_specs=[ANY] * n,
        scratch_shapes=[pltpu.SemaphoreType.DMA((n, 3)), pltpu.SemaphoreType.DMA((n, 3))],
    )(*ps)


def grads_chip_sum(name, g, recv, recv2):
    _, a, b = recv.shape
    ta = min(a, SUM_TILE)
    my_slot = lambda: 2 * lax.axis_index("x") + lax.axis_index("y")

    def body(g_ref, r_ref, r2_ref, o_ref):
        own = g_ref[...] + r_ref[...]
        o_ref[...] = ((own + r2_ref[0].astype(F32)) + r2_ref[1].astype(F32)) + r2_ref[2].astype(F32)

    return pl.pallas_call(
        body, name=name, grid=(a // ta,),
        in_specs=[pl.BlockSpec((None, None, ta, b), lambda i: (my_slot(), lax.axis_index("c"), i, 0)),
                  pl.BlockSpec((None, ta, b), lambda i: (my_slot(), i, 0)),
                  pl.BlockSpec((3, ta, b), lambda i: (0, i, 0))],
        out_specs=pl.BlockSpec((None, ta, b), lambda i: (lax.axis_index("c"), i, 0)),
        out_shape=jax.ShapeDtypeStruct((DEPTH, a, b), F32),
        compiler_params=_cparams(1),
    )(g, recv, recv2)


def grads_pair_gather(reds):
    n = len(reds)

    def body(*refs):
        buf = refs[n:2 * n]
        send_sems, recv_sems = refs[2 * n:]
        x, y, c, _ = _place()
        sibling = (x, y, 1 - c)
        cps = [_remote(buf[t].at[c], buf[t].at[c], send_sems.at[t], recv_sems.at[t], sibling) for t in range(n)]
        for cp in cps:
            cp.start()
        for t in range(n):
            _remote(buf[t].at[c], buf[t].at[1 - c], send_sems.at[t], recv_sems.at[t], sibling).wait_recv()
        for cp in cps:
            cp.wait_send()

    return pl.pallas_call(
        body, name="grads_pair_gather", out_shape=[jax.ShapeDtypeStruct(r.shape, r.dtype) for r in reds],
        in_specs=[ANY] * n, out_specs=[ANY] * n, input_output_aliases={t: t for t in range(n)},
        scratch_shapes=[pltpu.SemaphoreType.DMA((n,)), pltpu.SemaphoreType.DMA((n,))],
    )(*reds)


def small_allreduce(v):
    m, n = v.shape

    def body(x_ref, sum_ref, all_ref, send_sems, recv_sems, local_sem):
        x, y, c, chips = _place()
        me, sibling = (x, y, c), (x, y, 1 - c)

        def rows(px, py, pc):
            return all_ref.at[pl.ds((4 * px + 2 * py + pc) * m, m), :]

        def copy(k, block, to, src=None):
            return pltpu.make_async_remote_copy(src_ref=rows(*block) if src is None else src, dst_ref=rows(*block),
                                                send_sem=send_sems.at[k], recv_sem=recv_sems.at[k],
                                                device_id=to, device_id_type=MESH)

        mine = pltpu.make_async_copy(x_ref, rows(*me), local_sem)
        mine.start()
        first = [copy(0, me, sibling, src=x_ref)]
        first += [copy(1 + j, me, (*chip, c), src=x_ref) for j, chip in enumerate(chips)]
        for cp in first:
            cp.start()
        passed = [copy(4 + j, (*chip, c), sibling) for j, chip in enumerate(chips)]
        for j, chip in enumerate(chips):
            copy(1 + j, (*chip, c), me).wait_recv()
            passed[j].start()
        copy(0, sibling, me).wait_recv()
        for j, chip in enumerate(chips):
            copy(4 + j, (*chip, 1 - c), me).wait_recv()
        for cp in first + passed:
            cp.wait_send()
        mine.wait()
        acc = all_ref[0:m, :]
        for dev in range(1, 8):
            acc = acc + all_ref[dev * m:(dev + 1) * m, :]
        sum_ref[...] = acc

    vm = pl.BlockSpec(memory_space=pltpu.VMEM)
    return pl.pallas_call(
        body, name="small_allreduce",
        out_shape=[jax.ShapeDtypeStruct((m, n), F32), jax.ShapeDtypeStruct((8 * m, n), F32)],
        in_specs=[vm], out_specs=[vm, vm],
        scratch_shapes=[pltpu.SemaphoreType.DMA((7,)), pltpu.SemaphoreType.DMA((7,)), pltpu.SemaphoreType.DMA],
    )(v)


def reduce_scatter_grads(names, gs):
    recv = grads_pair_exchange(gs)
    parts = [grads_pair_sum("grads_pair_sum_" + n, g, r) for n, g, r in zip(names, gs, recv)]
    recv2 = grads_chip_exchange(parts)
    reds = [grads_chip_sum("grads_chip_sum_" + n, g, r, r2) for n, g, r, r2 in zip(names, gs, recv, recv2)]
    return grads_pair_gather(reds)


def adamw(name, w, g, m, v, block):
    grid = tuple(s // b for s, b in zip(w.shape, block))

    def body(w_ref, g_ref, m_ref, v_ref, d_ref, nm_ref, nv_ref):
        gv = g_ref[...]
        nm = ADAM_B1 * m_ref[...] + (1.0 - ADAM_B1) * gv
        nv = ADAM_B2 * v_ref[...] + (1.0 - ADAM_B2) * (gv * gv)
        m_hat = nm / (1.0 - ADAM_B1 ** ADAM_STEP)
        v_hat = nv / (1.0 - ADAM_B2 ** ADAM_STEP)
        d_ref[...] = -ADAM_LR * (m_hat / (jnp.sqrt(v_hat) + ADAM_EPS) + ADAM_WD * w_ref[...])
        nm_ref[...] = nm
        nv_ref[...] = nv

    spec = pl.BlockSpec(tuple(block), lambda *idx: idx)
    return pl.pallas_call(
        body, name=name, grid=grid, in_specs=[spec] * 4, out_specs=[spec] * 3,
        out_shape=[jax.ShapeDtypeStruct(w.shape, F32)] * 3, compiler_params=_cparams(len(grid)),
    )(w, g, m, v)


ADAM_ROWS = {"w_ada": 512, "dw_w": 62, "pw2_w": 256, "sconv_w": 8, "w_proj_a": 512, "w_proj_b": 512, "w_proj_c": 512,
             "w_out": 256}
ADAM_W_IN_COLS = 331

WEIGHT_NAMES = ("w_ada", "b_ada", "norm_g", "w_in", "q_norm_g", "k_norm_g", "sinks", "dw_w", "dw_b", "ln_g", "ln_b",
                "pw2_w", "pw2_b", "sconv_w", "a_log", "dt_bias", "dn_norm_g", "w_proj_a", "w_proj_b", "w_proj_c", "w_out")


def kernel(x, c, w_ada, b_ada, norm_g, w_in, q_norm_g, k_norm_g, sinks, dw_w, dw_b, ln_g, ln_b, pw2_w, pw2_b, sconv_w, a_log, dt_bias, dn_norm_g, w_proj_a, w_proj_b, w_proj_c, w_out, loss_target, m_w_ada, m_b_ada, m_norm_g, m_w_in, m_q_norm_g, m_k_norm_g, m_sinks, m_dw_w, m_dw_b, m_ln_g, m_ln_b, m_pw2_w, m_pw2_b, m_sconv_w, m_a_log, m_dt_bias, m_dn_norm_g, m_w_proj_a, m_w_proj_b, m_w_proj_c, m_w_out, v_w_ada, v_b_ada, v_norm_g, v_w_in, v_q_norm_g, v_k_norm_g, v_sinks, v_dw_w, v_dw_b, v_ln_g, v_ln_b, v_pw2_w, v_pw2_b, v_sconv_w, v_a_log, v_dt_bias, v_dn_norm_g, v_w_proj_a, v_w_proj_b, v_w_proj_c, v_w_out):
    args = dict(locals())
    w = {n: args[n] for n in WEIGHT_NAMES}
    mom = {n: args["m_" + n] for n in WEIGHT_NAMES}
    var = {n: args["v_" + n] for n in WEIGHT_NAMES}

    chip = 2 * lax.axis_index("x") + lax.axis_index("y")
    slots = []
    for n in GATHERED:
        own = w[n] if n in GATHER_F32 else w[n].astype(BF16)
        if n == "w_in":
            own = lax.switch(chip, [functools.partial(_w_in_send, k) for k in range(N_CHIPS)], own)
        slots.append(lax.dynamic_update_slice(lax.empty((N_CHIPS,) + own.shape, own.dtype), own[None], (chip, 0, 0, 0)))
    gathered = dict(zip(GATHERED, weights_allgather(slots)))
    layers = []
    for l in range(DEPTH):
        lw = {n: w[n][l] for n in SMALL}
        for n in GATHERED:
            lw[n] = gathered[n][:, l] if n == "w_in" else _join_layer(gathered[n][:, l], SHARDED[n])
        layers.append(prep_layer(lw))

    mod, conds = ada_fwd(jnp.tile(c, (8, 1)), w["w_ada"], w["b_ada"])
    act, saved = x[0], []
    for l in range(DEPTH):
        act, s = layer_fwd(str(l), act, mod[l:l + 1], layers[l])
        saved.append(s)
    dact, loss_part = loss_head("loss_head", act, loss_target[0], TM)
    loss = lax.psum(loss_part[0, 0], ("x", "y", "c"))
    layer_grads = [None] * DEPTH
    for l in reversed(range(DEPTH)):
        dact, layer_grads[l] = layer_bwd(str(l), dact, layers[l], saved[l])

    by_chip = [jnp.stack([layer_grads[l][n] if n == "w_in" else _split_layer(layer_grads[l][n], SHARDED[n])
                          for l in range(DEPTH)], axis=1) for n in REDUCE_BIG]
    final_grads = dict(zip(REDUCE_BIG, reduce_scatter_grads(REDUCE_BIG, by_chip)))
    final_grads["w_in"] = lax.switch(chip, [functools.partial(_w_in_receive_grad, k) for k in range(N_CHIPS)],
                                     final_grads["w_in"])
    small_names = SMALL + GATHER_F32
    small_shapes = {n: (DEPTH,) + layer_grads[0][n].shape for n in small_names}
    small_full = {n: jnp.stack([layer_grads[l][n] for l in range(DEPTH)]) for n in small_names}
    small_sum, small_all = small_allreduce(pack_small(small_full, small_names, SMALL_GRAD_ROWS))
    small_sum = unpack_small(small_sum, small_names, small_shapes)
    for n in GATHER_F32:
        width = w[n].shape[2]
        final_grads[n] = lax.dynamic_slice_in_dim(small_sum[n], chip * width, width, axis=2)
    n_mod = DEPTH * 3 * D_MODEL
    dmod = small_all.reshape(8, -1)[:, :n_mod].reshape(8, DEPTH, 3 * D_MODEL)
    width = w["w_ada"].shape[2]
    dmod = jnp.transpose(lax.dynamic_slice_in_dim(dmod, chip * width, width, axis=2), (1, 0, 2))
    final_grads["w_ada"] = ada_bwd(conds, dmod)
    final_grads.update({n: small_sum[n] for n in SMALL})
    small_grads = pack_small(final_grads, SMALL, SMALL_ROWS)

    delta, new_m, new_v = {}, {}, {}
    for n in SHARDED:
        shp = w[n].shape
        if n == "w_in":
            view = lambda a: jnp.transpose(a, (2, 0, 1))
            back = lambda a: jnp.transpose(a, (1, 2, 0))
            g3 = view(final_grads[n])
            final_grads[n] = back(g3)
            d, nm, nv = adamw("adamw_" + n, view(w[n]), g3, view(mom[n]), view(var[n]), (ADAM_W_IN_COLS, shp[0], shp[1]))
        else:
            view = lambda a, shp=shp: a.reshape(shp[0] * shp[1], shp[2])
            back = lambda a, shp=shp: a.reshape(shp)
            d, nm, nv = adamw("adamw_" + n, view(w[n]), view(final_grads[n]), view(mom[n]), view(var[n]),
                              (ADAM_ROWS[n], shp[2]))
        delta[n], new_m[n], new_v[n] = back(d), back(nm), back(nv)
    d, nm, nv = adamw("adamw_small", pack_small(w, SMALL, SMALL_ROWS), small_grads, pack_small(mom, SMALL, SMALL_ROWS),
                      pack_small(var, SMALL, SMALL_ROWS), (SMALL_ROWS, 128))
    delta.update(unpack_small(d, SMALL, small_shapes))
    new_m.update(unpack_small(nm, SMALL, small_shapes))
    new_v.update(unpack_small(nv, SMALL, small_shapes))

    return (loss, dact[None], *[final_grads[n] for n in WEIGHT_NAMES], *[delta[n] for n in WEIGHT_NAMES],
            *[new_m[n] for n in WEIGHT_NAMES], *[new_v[n] for n in WEIGHT_NAMES])
```

```python
import functools

import numpy as np
import jax
import jax.numpy as jnp
from jax import lax
from jax.experimental import pallas as pl
from jax.experimental.pallas import tpu as pltpu

F32 = jnp.float32
BF16 = jnp.bfloat16
MESH = pl.DeviceIdType.MESH

D_MODEL = 1024
DEPTH = 2
ATT_HEADS = 8
ATT_HEAD_DIM = 64
WINDOW = 128
CONV_K = 31
DN_HEADS = 4
DN_CONV_K = 4
DN_CHUNK = 64
EPS = 1e-6
NEG_INF = -1e30
N_CHIPS = 4
D_IN = 7944

ADAM_LR = 0.001
ADAM_B1 = 0.9
ADAM_B2 = 0.999
ADAM_EPS = 1e-08
ADAM_WD = 0.01
ADAM_STEP = 10

VMEM_LIMIT = 56 * 1024 * 1024

P_QA, P_ZA, P_GLU, P_ZB, P_ZC, P_MG, P_QKV, P_KA, P_VA, P_AB, P_TOTAL = (
    0, 512, 1024, 2048, 2560, 3072, 6144, 7680, 7808, 7936, 8064)
HEAD_ORDER = (0, 4, 1, 5, 2, 6, 3, 7)


def _in_pieces():
    p = [(0 + 64 * h, 64) for h in HEAD_ORDER]
    p += [(768 + 64 * h, 64) for h in HEAD_ORDER]
    for g in range(4):
        p += [(1280 + 128 * g, 128), (1792 + 128 * g, 128)]
    p += [(2304, 512), (4360, 512), (4872, 3072), (2816, 1536), (512, 128), (640, 128), (4352, 8)]
    return p


def _perm_heads_rows(w):
    return jnp.concatenate([w[64 * h:64 * h + 64] for h in HEAD_ORDER], axis=0)


def _unperm_heads_rows(w):
    inv = [HEAD_ORDER.index(h) for h in range(8)]
    return jnp.concatenate([w[64 * s:64 * s + 64] for s in inv], axis=0)


def _split_bf16(a, terms):
    out, rest = [], a.astype(F32)
    for _ in range(terms - 1):
        out.append(rest.astype(BF16))
        rest = rest - out[-1].astype(F32)
    return out + [rest.astype(BF16)]


def _dot(a, b, dims, exact):
    d = lambda p, q: lax.dot_general(p, q, (dims, ((), ())), preferred_element_type=F32)
    if exact:
        (ah, al), (bh, bl) = _split_bf16(a, 2), _split_bf16(b, 2)
        return d(ah, bh) + (d(ah, bl) + d(al, bh))
    return d(a.astype(BF16), b.astype(BF16))


def _make_mm(exact):
    @jax.custom_vjp
    def nn(a, b):
        return _dot(a, b, ((1,), (0,)), exact)

    @jax.custom_vjp
    def nt(a, b):
        return _dot(a, b, ((1,), (1,)), exact)

    @jax.custom_vjp
    def tn(a, b):
        return _dot(a, b, ((0,), (0,)), exact)

    nn.defvjp(lambda a, b: (nn(a, b), (a, b)),
              lambda r, g: (nt(g, r[1]).astype(r[0].dtype), tn(r[0], g).astype(r[1].dtype)))
    nt.defvjp(lambda a, b: (nt(a, b), (a, b)),
              lambda r, g: (nn(g, r[1]).astype(r[0].dtype), tn(g, r[0]).astype(r[1].dtype)))
    tn.defvjp(lambda a, b: (tn(a, b), (a, b)),
              lambda r, g: (nt(r[1], g).astype(r[0].dtype), nn(r[0], g).astype(r[1].dtype)))
    return nn, nt, tn


mm, mm_nt, mm_tn = _make_mm(False)
xmm, xmm_nt, xmm_tn = _make_mm(True)


@jax.custom_vjp
def sel_mm(m, g):
    mb = m.astype(BF16)
    parts = [jnp.dot(mb, p, preferred_element_type=F32) for p in _split_bf16(g, 3)]
    return parts[0] + (parts[1] + parts[2])


def _sel_mm_bwd(m, dy):
    mb = m.astype(BF16)
    parts = [lax.dot_general(mb, p, (((0,), (0,)), ((), ())), preferred_element_type=F32) for p in _split_bf16(dy, 3)]
    return jnp.zeros_like(m), parts[0] + (parts[1] + parts[2])


sel_mm.defvjp(lambda m, g: (sel_mm(m, g), m), _sel_mm_bwd)


@jax.custom_vjp
def tri_inv(*mats):
    n = mats[0].shape[0]
    eye = jnp.where(lax.broadcasted_iota(jnp.int32, (n, n), 0) == lax.broadcasted_iota(jnp.int32, (n, n), 1), 1.0, 0.0)
    ts = [eye - a for a in mats]
    pws = list(mats)
    for _ in range(5):
        pws = [xmm(pw, pw) for pw in pws]
        ts = [t + xmm(t, pw) for t, pw in zip(ts, pws)]
    return tuple(ts)


def _tri_inv_bwd(ts, dts):
    inner = [xmm_nt(dt, t) for t, dt in zip(ts, dts)]
    return tuple(-xmm_tn(t, m) for t, m in zip(ts, inner))


tri_inv.defvjp(lambda *mats: (tri_inv(*mats),) * 2, _tri_inv_bwd)


@jax.custom_vjp
def tri_inv_known(a, t):
    return t


tri_inv_known.defvjp(lambda a, t: (t, t), lambda t, dt: (_tri_inv_bwd((t,), (dt,))[0], jnp.zeros_like(t)))


def _sigmoid(x):
    return 1.0 / (1.0 + jnp.exp(-x))


def _silu(x):
    return x * _sigmoid(x)


def _softplus(x):
    return jnp.maximum(x, 0.0) + jnp.log(1.0 + jnp.exp(-jnp.abs(x)))


def _cparams(n_grid):
    return pltpu.CompilerParams(dimension_semantics=("arbitrary",) * n_grid, vmem_limit_bytes=VMEM_LIMIT)


def _row_spec(tm, width, colblk):
    return pl.BlockSpec((tm, width), lambda i, cb=colblk: (i, cb))


def _const_spec(shape):
    nd = len(shape)
    return pl.BlockSpec(tuple(shape), lambda i, nd=nd: (0,) * nd)


def rowwise_fwd(name, f, rows, consts, outs, tm):
    n_r, n_c = len(rows), len(consts)
    t = rows[0][0].shape[0]

    def body(*refs):
        vals = [r[...] for r in refs[:n_r + n_c]]
        res = f(*vals)
        if not isinstance(res, (tuple, list)):
            res = (res,)
        for o_ref, v in zip(refs[n_r + n_c:], res):
            o_ref[...] = v.astype(o_ref.dtype)

    return pl.pallas_call(
        body, name=name, grid=(t // tm,),
        in_specs=[_row_spec(tm, w, cb) for _, w, cb in rows] + [_const_spec(c.shape) for c in consts],
        out_specs=[_row_spec(tm, w, 0) for w, _ in outs],
        out_shape=[jax.ShapeDtypeStruct((t, w), dt) for w, dt in outs],
        compiler_params=_cparams(1),
    )(*[a for a, _, _ in rows], *consts)


def rowwise_bwd(name, f, rows, consts, cts, row_grad_dtypes, tm):
    n_r, n_c, n_ct = len(rows), len(consts), len(cts)
    t = rows[0][0].shape[0]
    keep = [k for k, dt in enumerate(row_grad_dtypes) if dt is not None]

    def body(*refs):
        ins = [r[...].astype(F32) for r in refs[:n_r + n_c]]
        g_out = [r[...].astype(F32) for r in refs[n_r + n_c:n_r + n_c + n_ct]]
        out_refs = refs[n_r + n_c + n_ct:]

        def fw(*a):
            res = f(*a)
            return tuple(res) if isinstance(res, (tuple, list)) else (res,)

        _, vjp = jax.vjp(fw, *ins)
        grads = vjp(tuple(g_out))
        for o_ref, k in zip(out_refs[:len(keep)], keep):
            o_ref[...] = grads[k].astype(o_ref.dtype)
        first = pl.program_id(0) == 0
        for o_ref, g in zip(out_refs[len(keep):], grads[n_r:]):
            @pl.when(first)
            def _(o_ref=o_ref, g=g):
                o_ref[...] = g

            @pl.when(jnp.logical_not(first))
            def _(o_ref=o_ref, g=g):
                o_ref[...] += g

    return pl.pallas_call(
        body, name=name, grid=(t // tm,),
        in_specs=[_row_spec(tm, w, cb) for _, w, cb in rows] + [_const_spec(c.shape) for c in consts]
        + [_row_spec(tm, w, cb) for _, w, cb in cts],
        out_specs=[_row_spec(tm, rows[k][1], 0) for k in keep] + [_const_spec(c.shape) for c in consts],
        out_shape=[jax.ShapeDtypeStruct((t, rows[k][1]), row_grad_dtypes[k]) for k in keep]
        + [jax.ShapeDtypeStruct(c.shape, F32) for c in consts],
        compiler_params=_cparams(1),
    )(*[a for a, _, _ in rows], *consts, *[a for a, _, _ in cts])


def f_norm_mod(x, g, scale, shift):
    y = x * lax.rsqrt(jnp.mean(x * x, axis=-1, keepdims=True) + EPS) * g
    return y * (1.0 + scale) + shift


def f_conf_tail(u, zb, ln_g, ln_b, pw2_w, pw2_b):
    mu = jnp.mean(u, axis=-1, keepdims=True)
    xc = u - mu
    var = jnp.mean(xc * xc, axis=-1, keepdims=True)
    y = _silu(xc * lax.rsqrt(var + EPS) * ln_g + ln_b)
    return (mm(y, pw2_w) + pw2_b) * _silu(zb)


def f_merge(ya, yb, yc, mg, x, gate, wpa, wpb, wpc, wout):
    d = D_MODEL
    merged = (_sigmoid(mg[:, :d]) * mm(ya, wpa) + _sigmoid(mg[:, d:2 * d]) * mm(yb, wpb)
              + _sigmoid(mg[:, 2 * d:]) * mm(yc, wpc))
    return x + gate * mm(merged, wout)


def matmul_nn(name, a, b, out_dtype, tm, tn, tk, b_transposed=False):
    m, k = a.shape
    n = b.shape[0] if b_transposed else b.shape[1]
    nk = k // tk
    b_spec = (pl.BlockSpec((tn, tk), lambda i, j, kk: (j, kk)) if b_transposed
              else pl.BlockSpec((tk, tn), lambda i, j, kk: (kk, j)))

    def body(a_ref, b_ref, o_ref, *acc):
        part = lax.dot_general(a_ref[...].astype(BF16), b_ref[...].astype(BF16),
                               (((1,), (1 if b_transposed else 0,)), ((), ())), preferred_element_type=F32)
        if nk == 1:
            o_ref[...] = part.astype(o_ref.dtype)
            return
        kk = pl.program_id(2)
        acc_ref = acc[0]

        @pl.when(kk == 0)
        def _():
            acc_ref[...] = part

        @pl.when(kk > 0)
        def _():
            acc_ref[...] += part

        @pl.when(kk == nk - 1)
        def _():
            o_ref[...] = acc_ref[...].astype(o_ref.dtype)

    return pl.pallas_call(
        body, name=name, grid=(m // tm, n // tn, nk),
        in_specs=[pl.BlockSpec((tm, tk), lambda i, j, kk: (i, kk)), b_spec],
        out_specs=pl.BlockSpec((tm, tn), lambda i, j, kk: (i, j)),
        out_shape=jax.ShapeDtypeStruct((m, n), out_dtype),
        scratch_shapes=[] if nk == 1 else [pltpu.VMEM((tm, tn), F32)],
        compiler_params=_cparams(3),
    )(a, b)


def ada_fwd(c8, w_shard, b_ada):
    n_cols = w_shard.shape[2]
    masks = [(m >> 2 & 1, m >> 1 & 1, m & 1) for m in range(1, 8)]

    def body(c_ref, w_ref, b_ref, mod_ref, conds_ref, cbuf, sendbuf, recvbuf, send_sems, recv_sems):
        x, y, c, chips = _place()
        flip = lambda v, bit: 1 - v if bit else v
        peers = [(flip(x, mx), flip(y, my), flip(c, mc)) for mx, my, mc in masks]
        dev = lambda p: 4 * p[0] + 2 * p[1] + p[2]
        cbuf[dev((x, y, c))] = c_ref[...]
        first = [_remote(c_ref, cbuf.at[dev((x, y, c))], send_sems.at[i], recv_sems.at[i], p) for i, p in enumerate(peers)]
        for cp in first:
            cp.start()
        for i, p in enumerate(peers):
            _remote(c_ref, cbuf.at[dev(p)], send_sems.at[i], recv_sems.at[i], p).wait_recv()
        conds = jnp.concatenate([cbuf[d, 0:1, :] for d in range(8)], axis=0)
        conds_ref[...] = conds
        act = _silu(conds)
        parts = [mm(act, w_ref[l]) for l in range(DEPTH)]
        row8 = lax.broadcasted_iota(jnp.int32, (8, 1), 0)

        def tile_for(chip):
            r = 2 * (2 * chip[0] + chip[1]) + c
            rows = [jnp.sum(jnp.where(row8 == r, parts[l], 0.0), axis=0, keepdims=True) for l in range(DEPTH)]
            return jnp.where(row8 == 0, rows[0], jnp.where(row8 == 1, rows[1], 0.0))

        my_slot = 2 * x + y
        recvbuf[my_slot] = tile_for((x, y))
        second = []
        for j, chip in enumerate(chips):
            sendbuf[j] = tile_for(chip)
            second.append(_remote(sendbuf.at[j], recvbuf.at[my_slot], send_sems.at[7 + j], recv_sems.at[7 + j], (*chip, c)))
            second[-1].start()
        for j, chip in enumerate(chips):
            _remote(sendbuf.at[j], recvbuf.at[2 * chip[0] + chip[1]], send_sems.at[7 + j], recv_sems.at[7 + j],
                    (*chip, c)).wait_recv()
        rows = [jnp.concatenate([recvbuf[k, l:l + 1, :] for k in range(N_CHIPS)], axis=1) + b_ref[l:l + 1, :]
                for l in range(DEPTH)]
        mod_ref[...] = jnp.concatenate(rows + [jnp.zeros((8 - DEPTH, N_CHIPS * n_cols), F32)], axis=0)
        for cp in first + second:
            cp.wait_send()

    vm = pl.BlockSpec(memory_space=pltpu.VMEM)
    return pl.pallas_call(
        body, name="ada_fwd",
        out_shape=[jax.ShapeDtypeStruct((8, N_CHIPS * n_cols), F32), jax.ShapeDtypeStruct((8, D_MODEL), F32)],
        in_specs=[vm, vm, vm], out_specs=[vm, vm],
        scratch_shapes=[pltpu.VMEM((8, 8, D_MODEL), F32), pltpu.VMEM((3, 8, n_cols), F32),
                        pltpu.VMEM((N_CHIPS, 8, n_cols), F32), pltpu.SemaphoreType.DMA((10,)), pltpu.SemaphoreType.DMA((10,))],
        compiler_params=pltpu.CompilerParams(vmem_limit_bytes=VMEM_LIMIT),
    )(c8, w_shard, b_ada)


def ada_bwd(conds, dmod):
    def body(c_ref, d_ref, o_ref):
        act = _silu(c_ref[...])
        for l in range(DEPTH):
            o_ref[l] = mm_tn(act, d_ref[l])

    return pl.pallas_call(
        body, name="ada_bwd", out_shape=jax.ShapeDtypeStruct((DEPTH, D_MODEL, dmod.shape[2]), F32),
        compiler_params=pltpu.CompilerParams(vmem_limit_bytes=VMEM_LIMIT),
    )(conds, dmod)


def _f_attn(first_block, q, za, kc, vc, kp, vp, qg, kg, sinks):
    w = WINDOW
    lane = lax.broadcasted_iota(jnp.int32, (1, 128), 1)
    halves = [lane < 64, lane >= 64]

    def rms_halves(x, g):
        x2 = x * x
        s0 = jnp.sum(jnp.where(halves[0], x2, 0.0), axis=-1, keepdims=True)
        s1 = jnp.sum(jnp.where(halves[1], x2, 0.0), axis=-1, keepdims=True)
        r = jnp.where(halves[0], lax.rsqrt(s0 / 64.0 + EPS), lax.rsqrt(s1 / 64.0 + EPS))
        return x * r * g

    kcat = rms_halves(jnp.concatenate([kp, kc], axis=0), kg)
    vcat = jnp.concatenate([vp, vc], axis=0)
    qi = lax.broadcasted_iota(jnp.int32, (w, 2 * w), 0)
    kj = lax.broadcasted_iota(jnp.int32, (w, 2 * w), 1)
    dist = qi + w - kj
    valid = (dist >= 0) & (dist < w) & (jnp.logical_not(first_block) | (kj >= w))
    distf = dist.astype(F32)
    units = [(grp, half) for grp in range(4) for half in range(2)]
    qns = [rms_halves(q[:, 128 * grp:128 * grp + 128], qg) * (ATT_HEAD_DIM ** -0.5) for grp in range(4)]
    vhalf = [jnp.where(halves[half], vcat, 0.0) for half in range(2)]
    scores, sinks_h = [], []
    for grp, half in units:
        head = HEAD_ORDER[2 * grp + half]
        slope = 2.0 ** (-8.0 * (head + 1) / ATT_HEADS)
        sinks_h.append(jnp.sum(jnp.where(lane == head, sinks, 0.0), axis=-1, keepdims=True))
        s = mm_nt(jnp.where(halves[half], qns[grp], 0.0), kcat) - slope * distf
        scores.append(jnp.where(valid, s, NEG_INF))
    probs = []
    for s, sink in zip(scores, sinks_h):
        m = lax.stop_gradient(jnp.maximum(jnp.max(s, axis=-1, keepdims=True), sink))
        p = jnp.exp(s - m)
        probs.append(p / (jnp.sum(p, axis=-1, keepdims=True) + jnp.exp(sink - m)))
    outs = [mm(p, vhalf[half]) for p, (grp, half) in zip(probs, units)]
    return jnp.concatenate([outs[2 * grp] + outs[2 * grp + 1] for grp in range(4)], axis=1) * _silu(za)


def attn_fwd(name, proj, qg, kg, sinks):
    t = proj.shape[0]
    nb = t // WINDOW

    def body(q_ref, za_ref, kc_ref, vc_ref, kp_ref, vp_ref, qg_ref, kg_ref, s_ref, o_ref):
        first = pl.program_id(0) == 0
        o_ref[...] = _f_attn(first, q_ref[...], za_ref[...], kc_ref[...], vc_ref[...], kp_ref[...], vp_ref[...],
                             qg_ref[...], kg_ref[...], s_ref[...])

    cur = lambda cb: (lambda i: (i, cb))
    prev = lambda cb: (lambda i: (jnp.maximum(i - 1, 0), cb))
    return pl.pallas_call(
        body, name=name, grid=(nb,),
        in_specs=[pl.BlockSpec((WINDOW, 512), cur(P_QA // 512)), pl.BlockSpec((WINDOW, 512), cur(P_ZA // 512)),
                  pl.BlockSpec((WINDOW, 128), cur(P_KA // 128)), pl.BlockSpec((WINDOW, 128), cur(P_VA // 128)),
                  pl.BlockSpec((WINDOW, 128), prev(P_KA // 128)), pl.BlockSpec((WINDOW, 128), prev(P_VA // 128)),
                  _const_spec((1, 128)), _const_spec((1, 128)), _const_spec((1, 128))],
        out_specs=pl.BlockSpec((WINDOW, 512), lambda i: (i, 0)),
        out_shape=jax.ShapeDtypeStruct((t, 512), F32),
        compiler_params=_cparams(1),
    )(proj, proj, proj, proj, proj, proj, qg, kg, sinks)


def attn_bwd(name, proj, qg, kg, sinks, dya):
    t = proj.shape[0]
    nb = t // WINDOW

    def body(q_ref, za_ref, kc_ref, vc_ref, kp_ref, vp_ref, qg_ref, kg_ref, s_ref, dy_ref,
             dqz_ref, dkv_ref, dqg_ref, dkg_ref, ds_ref, carry_ref):
        j = pl.program_id(0)
        first = j == nb - 1

        @pl.when(j == 0)
        def _():
            carry_ref[...] = jnp.zeros_like(carry_ref)
            dqg_ref[...] = jnp.zeros_like(dqg_ref)
            dkg_ref[...] = jnp.zeros_like(dkg_ref)
            ds_ref[...] = jnp.zeros_like(ds_ref)

        ins = [r[...] for r in (q_ref, za_ref, kc_ref, vc_ref, kp_ref, vp_ref, qg_ref, kg_ref, s_ref)]
        _, vjp = jax.vjp(functools.partial(_f_attn, first), *ins)
        dq, dza, dkc, dvc, dkp, dvp, dqg, dkg, dsk = vjp(dy_ref[...])
        dqz_ref[:, 0:512] = dq.astype(dqz_ref.dtype)
        dqz_ref[:, 512:1024] = dza.astype(dqz_ref.dtype)
        dkv_ref[:, 0:128] = (dkc + carry_ref[0]).astype(dkv_ref.dtype)
        dkv_ref[:, 128:256] = (dvc + carry_ref[1]).astype(dkv_ref.dtype)
        carry_ref[0] = dkp
        carry_ref[1] = dvp
        dqg_ref[...] += dqg
        dkg_ref[...] += dkg
        ds_ref[...] += dsk

    cur = lambda cb: (lambda j: (nb - 1 - j, cb))
    prev = lambda cb: (lambda j: (jnp.maximum(nb - 2 - j, 0), cb))
    return pl.pallas_call(
        body, name=name, grid=(nb,),
        in_specs=[pl.BlockSpec((WINDOW, 512), cur(P_QA // 512)), pl.BlockSpec((WINDOW, 512), cur(P_ZA // 512)),
                  pl.BlockSpec((WINDOW, 128), cur(P_KA // 128)), pl.BlockSpec((WINDOW, 128), cur(P_VA // 128)),
                  pl.BlockSpec((WINDOW, 128), prev(P_KA // 128)), pl.BlockSpec((WINDOW, 128), prev(P_VA // 128)),
                  _const_spec((1, 128)), _const_spec((1, 128)), _const_spec((1, 128)),
                  pl.BlockSpec((WINDOW, 512), cur(0))],
        out_specs=[pl.BlockSpec((WINDOW, 1024), cur(0)), pl.BlockSpec((WINDOW, 256), cur(0)),
                   _const_spec((1, 128)), _const_spec((1, 128)), _const_spec((1, 128))],
        out_shape=[jax.ShapeDtypeStruct((t, 1024), BF16), jax.ShapeDtypeStruct((t, 256), BF16),
                   jax.ShapeDtypeStruct((1, 128), F32), jax.ShapeDtypeStruct((1, 128), F32),
                   jax.ShapeDtypeStruct((1, 128), F32)],
        scratch_shapes=[pltpu.VMEM((2, WINDOW, 128), F32)],
        compiler_params=_cparams(1),
    )(proj, proj, proj, proj, proj, proj, qg, kg, sinks, dya)


CONV_ROWS = 256


def _conv_taps(src_ref, w_ref, n_taps, base, t):
    for r0 in range(0, t, CONV_ROWS):
        acc = w_ref[0:1, :] * src_ref[pl.ds(r0 + base, CONV_ROWS), :]
        for k in range(1, n_taps):
            acc = acc + w_ref[k:k + 1, :] * src_ref[pl.ds(r0 + base + k, CONV_ROWS), :]
        yield r0, acc


def _conv_wgrad(dy_ref, src_ref, n_taps, base, t, dy_base=0):
    out = []
    for k in range(n_taps):
        acc = jnp.zeros((8, 128), F32)
        for r0 in range(0, t, CONV_ROWS):
            prod = dy_ref[pl.ds(r0 + dy_base, CONV_ROWS), :] * src_ref[pl.ds(r0 + base + k, CONV_ROWS), :]
            acc = acc + jnp.sum(prod.reshape(CONV_ROWS // 8, 8, 128), axis=0)
        out.append(jnp.sum(acc, axis=0, keepdims=True))
    return out


def glu_conv_fwd(name, proj, w32, bias):
    t = proj.shape[0]
    pad = 32

    def body(x_ref, w_ref, b_ref, o_ref, u_ref):
        u_ref[0:pad, :] = jnp.zeros((pad, 128), F32)
        u_ref[pad:pad + t, :] = x_ref[:, 0:128] * _sigmoid(x_ref[:, 128:256])
        for r0, acc in _conv_taps(u_ref, w_ref, CONV_K, pad - (CONV_K - 1), t):
            o_ref[pl.ds(r0, CONV_ROWS), :] = acc + b_ref[...]

    return pl.pallas_call(
        body, name=name, grid=(4,),
        in_specs=[pl.BlockSpec((t, 256), lambda cb: (0, P_GLU // 256 + cb)), pl.BlockSpec((32, 128), lambda cb: (0, cb)),
                  pl.BlockSpec((1, 128), lambda cb: (0, cb))],
        out_specs=pl.BlockSpec((t, 128), lambda cb: (0, cb)),
        out_shape=jax.ShapeDtypeStruct((t, 512), F32),
        scratch_shapes=[pltpu.VMEM((t + pad, 128), F32)],
        compiler_params=_cparams(1),
    )(proj, w32, bias)


def glu_conv_bwd(name, proj, w32, dub):
    t = proj.shape[0]
    pad = 32
    k1 = CONV_K - 1

    def body(x_ref, w_ref, dy_ref, dx_ref, dw_ref, db_ref, u_ref, dyp_ref, wrev_ref):
        val = x_ref[:, 0:128]
        sg = _sigmoid(x_ref[:, 128:256])
        u_ref[0:pad, :] = jnp.zeros((pad, 128), F32)
        u_ref[pad:pad + t, :] = val * sg
        dyp_ref[0:t, :] = dy_ref[...]
        dyp_ref[t:t + pad, :] = jnp.zeros((pad, 128), F32)
        for k in range(CONV_K):
            wrev_ref[k:k + 1, :] = w_ref[k1 - k:k1 - k + 1, :]
        wrev_ref[CONV_K:32, :] = jnp.zeros((32 - CONV_K, 128), F32)
        for r0, du in _conv_taps(dyp_ref, wrev_ref, CONV_K, 0, t):
            v = x_ref[pl.ds(r0, CONV_ROWS), 0:128]
            s = _sigmoid(x_ref[pl.ds(r0, CONV_ROWS), 128:256])
            dx_ref[pl.ds(r0, CONV_ROWS), 0:128] = (du * s).astype(dx_ref.dtype)
            dx_ref[pl.ds(r0, CONV_ROWS), 128:256] = (du * v * s * (1.0 - s)).astype(dx_ref.dtype)
        dws = _conv_wgrad(dyp_ref, u_ref, CONV_K, pad - k1, t)
        for k in range(CONV_K):
            dw_ref[k:k + 1, :] = dws[k]
        dw_ref[CONV_K:32, :] = jnp.zeros((32 - CONV_K, 128), F32)
        db_ref[...] = jnp.sum(dy_ref[...], axis=0, keepdims=True)

    return pl.pallas_call(
        body, name=name, grid=(4,),
        in_specs=[pl.BlockSpec((t, 256), lambda cb: (0, P_GLU // 256 + cb)), pl.BlockSpec((32, 128), lambda cb: (0, cb)),
                  pl.BlockSpec((t, 128), lambda cb: (0, cb))],
        out_specs=[pl.BlockSpec((t, 256), lambda cb: (0, cb)), pl.BlockSpec((32, 128), lambda cb: (0, cb)),
                   pl.BlockSpec((1, 128), lambda cb: (0, cb))],
        out_shape=[jax.ShapeDtypeStruct((t, 1024), BF16), jax.ShapeDtypeStruct((32, 512), F32),
                   jax.ShapeDtypeStruct((1, 512), F32)],
        scratch_shapes=[pltpu.VMEM((t + pad, 128), F32), pltpu.VMEM((t + pad, 128), F32), pltpu.VMEM((32, 128), F32)],
        compiler_params=_cparams(1),
    )(proj, w32, dub)


def sconv_fwd(name, proj, w8):
    t = proj.shape[0]
    pad = 8
    k1 = DN_CONV_K - 1

    def body(x_ref, w_ref, o_ref, xp_ref):
        xp_ref[0:pad, :] = jnp.zeros((pad, 128), F32)
        xp_ref[pad:pad + t, :] = x_ref[...]
        for r0, acc in _conv_taps(xp_ref, w_ref, DN_CONV_K, pad - k1, t):
            o_ref[pl.ds(r0, CONV_ROWS), :] = _silu(acc)

    return pl.pallas_call(
        body, name=name, grid=(12,),
        in_specs=[pl.BlockSpec((t, 128), lambda cb: (0, P_QKV // 128 + cb)), pl.BlockSpec((8, 128), lambda cb: (0, cb))],
        out_specs=pl.BlockSpec((t, 128), lambda cb: (0, cb)),
        out_shape=jax.ShapeDtypeStruct((t, 1536), F32),
        scratch_shapes=[pltpu.VMEM((t + pad, 128), F32)],
        compiler_params=_cparams(1),
    )(proj, w8)


def sconv_bwd(name, proj, w8, dqkv):
    t = proj.shape[0]
    pad = 8
    k1 = DN_CONV_K - 1

    def body(x_ref, w_ref, dy_ref, dx_ref, dw_ref, xp_ref, dpp_ref, wrev_ref):
        xp_ref[0:pad, :] = jnp.zeros((pad, 128), F32)
        xp_ref[pad:pad + t, :] = x_ref[...]
        for r0, pre in _conv_taps(xp_ref, w_ref, DN_CONV_K, pad - k1, t):
            s = _sigmoid(pre)
            dpp_ref[pl.ds(r0, CONV_ROWS), :] = dy_ref[pl.ds(r0, CONV_ROWS), :] * (s * (1.0 + pre * (1.0 - s)))
        dpp_ref[t:t + pad, :] = jnp.zeros((pad, 128), F32)
        for k in range(DN_CONV_K):
            wrev_ref[k:k + 1, :] = w_ref[k1 - k:k1 - k + 1, :]
        wrev_ref[DN_CONV_K:8, :] = jnp.zeros((8 - DN_CONV_K, 128), F32)
        for r0, dx in _conv_taps(dpp_ref, wrev_ref, DN_CONV_K, 0, t):
            dx_ref[pl.ds(r0, CONV_ROWS), :] = dx.astype(dx_ref.dtype)
        dws = _conv_wgrad(dpp_ref, xp_ref, DN_CONV_K, pad - k1, t)
        for k in range(DN_CONV_K):
            dw_ref[k:k + 1, :] = dws[k]
        dw_ref[DN_CONV_K:8, :] = jnp.zeros((8 - DN_CONV_K, 128), F32)

    return pl.pallas_call(
        body, name=name, grid=(12,),
        in_specs=[pl.BlockSpec((t, 128), lambda cb: (0, P_QKV // 128 + cb)), pl.BlockSpec((8, 128), lambda cb: (0, cb)),
                  pl.BlockSpec((t, 128), lambda cb: (0, cb))],
        out_specs=[pl.BlockSpec((t, 128), lambda cb: (0, cb)), pl.BlockSpec((8, 128), lambda cb: (0, cb))],
        out_shape=[jax.ShapeDtypeStruct((t, 1536), BF16), jax.ShapeDtypeStruct((8, 1536), F32)],
        scratch_shapes=[pltpu.VMEM((t + pad, 128), F32), pltpu.VMEM((t + pad, 128), F32), pltpu.VMEM((8, 128), F32)],
        compiler_params=_cparams(1),
    )(proj, w8, dqkv)


def _f_delta_step(qkv, ab, zc, s0, s1, s2, s3, a_log, dt_bias, dn_g, inverses=None, with_inverses=False):
    cs = DN_CHUNK
    n = 2 * cs
    states = (s0, s1, s2, s3)
    lane = lax.broadcasted_iota(jnp.int32, (1, 128), 1)
    ri = lax.broadcasted_iota(jnp.int32, (n, n), 0)
    ci = lax.broadcasted_iota(jnp.int32, (n, n), 1)
    same = (ri // cs) == (ci // cs)
    lower = same & (ri >= ci)
    strict = same & (ri > ci)
    sums = jnp.concatenate([jnp.where(lower, 1.0, 0.0), jnp.where(same, 1.0, 0.0), jnp.where(ci < cs, 1.0, 0.0),
                            jnp.where(ci >= cs, 1.0, 0.0)], axis=0)
    top = lax.broadcasted_iota(jnp.int32, (n, 1), 0) < cs

    def pick(row, idx):
        return jnp.sum(jnp.where(lane == idx, row, 0.0), axis=-1, keepdims=True)

    def l2n(x):
        return x * lax.rsqrt(jnp.sum(x * x, axis=-1, keepdims=True) + EPS)

    n_chunks = qkv.shape[0] // cs
    units = [(k, pair) for k in range(n_chunks) for pair in range(2)]

    pre = []
    for k, pair in units:
        hs = (2 * pair, 2 * pair + 1)
        rows = slice(k * cs, (k + 1) * cs)
        stack = lambda f: jnp.concatenate([f(hs[0]), f(hs[1])], axis=0)
        qd = l2n(stack(lambda h: qkv[rows, 128 * h:128 * h + 128])) * (128 ** -0.5)
        kd = l2n(stack(lambda h: qkv[rows, 512 + 128 * h:512 + 128 * h + 128]))
        vd = stack(lambda h: qkv[rows, 1024 + 128 * h:1024 + 128 * h + 128])
        beta = _sigmoid(stack(lambda h: pick(ab[rows], 4 + h)))
        g = stack(lambda h: -jnp.exp(pick(a_log, h)) * _softplus(pick(ab[rows], h) + pick(dt_bias, h)))
        g_sums = sel_mm(sums, g * jnp.ones((1, n), F32))
        gc_col = g_sums[0:n]
        gl_b = g_sums[n:2 * n]
        g_end = (g_sums[2 * n:3 * n], g_sums[3 * n:])
        decay = jnp.where(lower, jnp.exp(jnp.where(lower, gc_col - gc_col.T, 0.0)), 0.0)
        kb = kd * beta
        pre.append(dict(qd=qd, kd=kd, vb=vd * beta, kb=kb, gc_col=gc_col, gl_b=gl_b, g_end=g_end, decay=decay,
                        a=jnp.where(strict, mm_nt(kb, kd) * decay, 0.0)))
    if inverses is None:
        tmats = tri_inv(*[p["a"] for p in pre])
    else:
        tmats = [tri_inv_known(p["a"], t) for p, t in zip(pre, inverses)]

    mid = []
    for p, tmat in zip(pre, tmats):
        egc = jnp.exp(p["gc_col"])
        mid.append(dict(u=mm(tmat, p["vb"]), wm=mm(tmat, p["kb"] * egc), qe=p["qd"] * egc,
                        intra=jnp.where(lower, mm_nt(p["qd"], p["kd"]) * p["decay"], 0.0),
                        ke=p["kd"] * jnp.exp(p["gl_b"] - p["gc_col"]), g_end=p["g_end"]))

    ys = []
    for k in range(n_chunks):
        rows = slice(k * cs, (k + 1) * cs)
        new_states, y_heads = [], []
        for pair in range(2):
            m = mid[2 * k + pair]
            hs = (2 * pair, 2 * pair + 1)
            st = (states[hs[0]], states[hs[1]])
            v_new = m["u"] - jnp.concatenate([mm(m["wm"][:cs], st[0]), mm(m["wm"][cs:], st[1])], axis=0)
            o = jnp.concatenate([mm(m["qe"][:cs], st[0]), mm(m["qe"][cs:], st[1])], axis=0) + mm(m["intra"], v_new)
            new_states.append(st[0] * jnp.exp(m["g_end"][0]) + mm_tn(jnp.where(top, m["ke"], 0.0), v_new))
            new_states.append(st[1] * jnp.exp(m["g_end"][1]) + mm_tn(jnp.where(top, 0.0, m["ke"]), v_new))
            od = o * lax.rsqrt(jnp.mean(o * o, axis=-1, keepdims=True) + EPS) * dn_g
            y_heads += [od[:cs] * _silu(zc[rows, 128 * hs[0]:128 * hs[0] + 128]),
                        od[cs:] * _silu(zc[rows, 128 * hs[1]:128 * hs[1] + 128])]
        states = tuple(new_states)
        ys.append(jnp.concatenate(y_heads, axis=1))
    if with_inverses:
        return (jnp.concatenate(ys, axis=0), *states), tmats
    return (jnp.concatenate(ys, axis=0), *states)


DELTA_ROWS = 4 * DN_CHUNK
DELTA_UNITS = 2 * DELTA_ROWS // DN_CHUNK


def delta_fwd(name, qkv, proj, a_log, dt_bias, dn_g):
    t = qkv.shape[0]
    nc = t // DELTA_ROWS

    def body(qkv_ref, ab_ref, zc_ref, al_ref, dt_ref, g_ref, y_ref, ssave_ref, tsave_ref, s_ref):
        @pl.when(pl.program_id(0) == 0)
        def _():
            s_ref[...] = jnp.zeros_like(s_ref)

        ssave_ref[0] = s_ref[...]
        st = [s_ref[128 * h:128 * h + 128, :] for h in range(4)]
        (y, *ns), tmats = _f_delta_step(qkv_ref[...], ab_ref[...], zc_ref[...], *st, al_ref[...], dt_ref[...], g_ref[...],
                                        with_inverses=True)
        y_ref[...] = y
        for h in range(4):
            s_ref[128 * h:128 * h + 128, :] = ns[h]
        for u, tm in enumerate(tmats):
            tsave_ref[0, 128 * u:128 * u + 128, :] = tm

    return pl.pallas_call(
        body, name=name, grid=(nc,),
        in_specs=[pl.BlockSpec((DELTA_ROWS, 1536), lambda i: (i, 0)), pl.BlockSpec((DELTA_ROWS, 128), lambda i: (i, P_AB // 128)),
                  pl.BlockSpec((DELTA_ROWS, 512), lambda i: (i, P_ZC // 512)),
                  _const_spec((1, 128)), _const_spec((1, 128)), _const_spec((1, 128))],
        out_specs=[pl.BlockSpec((DELTA_ROWS, 512), lambda i: (i, 0)), pl.BlockSpec((1, 512, 128), lambda i: (i, 0, 0)),
                   pl.BlockSpec((1, DELTA_UNITS * 128, 128), lambda i: (i, 0, 0))],
        out_shape=[jax.ShapeDtypeStruct((t, 512), F32), jax.ShapeDtypeStruct((nc, 512, 128), F32),
                   jax.ShapeDtypeStruct((nc, DELTA_UNITS * 128, 128), F32)],
        scratch_shapes=[pltpu.VMEM((512, 128), F32)],
        compiler_params=_cparams(1),
    )(qkv, proj, proj, a_log, dt_bias, dn_g)


def delta_bwd(name, qkv, proj, ssave, tsave, a_log, dt_bias, dn_g, dyc, exchange=None):
    t = qkv.shape[0]
    nc = t // DELTA_ROWS
    ex_layer, ps = exchange if exchange is not None else (None, [])
    n_ex = len(ps)

    def body(*refs):
        qkv_ref, ab_ref, zc_ref, ss_ref, ts_ref, al_ref, dt_ref, g_ref, dy_ref = refs[:9]
        p_refs = refs[9:9 + n_ex]
        dqkv_ref, dab_ref, dzc_ref, dal_ref, ddt_ref, dg_ref = refs[9 + n_ex:15 + n_ex]
        r2_refs = refs[15 + n_ex:15 + 2 * n_ex]
        ds_ref = refs[15 + 2 * n_ex]

        @pl.when(pl.program_id(0) == 0)
        def _():
            ds_ref[...] = jnp.zeros_like(ds_ref)
            dal_ref[...] = jnp.zeros_like(dal_ref)
            ddt_ref[...] = jnp.zeros_like(ddt_ref)
            dg_ref[...] = jnp.zeros_like(dg_ref)

        if n_ex:
            owner = lax.axis_index("c") == ex_layer

            @pl.when((pl.program_id(0) == 0) & owner)
            def _():
                for cp in _chip_exchange_copies(ex_layer, p_refs, r2_refs, *refs[16 + 2 * n_ex:]):
                    cp.start()

        st = [ss_ref[0, 128 * h:128 * h + 128, :] for h in range(4)]
        known = [ts_ref[0, 128 * u:128 * u + 128, :] for u in range(DELTA_UNITS)]
        _, vjp = jax.vjp(functools.partial(_f_delta_step, inverses=known), qkv_ref[...], ab_ref[...], zc_ref[...], *st,
                         al_ref[...], dt_ref[...], g_ref[...])
        dst = tuple(ds_ref[128 * h:128 * h + 128, :] for h in range(4))
        dqkv, dab, dzc, d0, d1, d2, d3, dal, ddt, dg = vjp((dy_ref[...], *dst))
        dqkv_ref[...] = dqkv
        dab_ref[...] = dab.astype(dab_ref.dtype)
        dzc_ref[...] = dzc.astype(dzc_ref.dtype)
        for h, d in enumerate((d0, d1, d2, d3)):
            ds_ref[128 * h:128 * h + 128, :] = d
        dal_ref[...] += dal
        ddt_ref[...] += ddt
        dg_ref[...] += dg

        if n_ex:
            @pl.when((pl.program_id(0) == nc - 1) & owner)
            def _():
                for cp in _chip_exchange_copies(ex_layer, p_refs, r2_refs, *refs[16 + 2 * n_ex:]):
                    cp.wait()

    rev = lambda cb: (lambda j: (nc - 1 - j, cb))
    return pl.pallas_call(
        body, name=name, grid=(nc,),
        in_specs=[pl.BlockSpec((DELTA_ROWS, 1536), rev(0)), pl.BlockSpec((DELTA_ROWS, 128), rev(P_AB // 128)),
                  pl.BlockSpec((DELTA_ROWS, 512), rev(P_ZC // 512)), pl.BlockSpec((1, 512, 128), lambda j: (nc - 1 - j, 0, 0)),
                  pl.BlockSpec((1, DELTA_UNITS * 128, 128), lambda j: (nc - 1 - j, 0, 0)),
                  _const_spec((1, 128)), _const_spec((1, 128)), _const_spec((1, 128)),
                  pl.BlockSpec((DELTA_ROWS, 512), rev(0))] + [ANY] * n_ex,
        out_specs=[pl.BlockSpec((DELTA_ROWS, 1536), rev(0)), pl.BlockSpec((DELTA_ROWS, 128), rev(0)),
                   pl.BlockSpec((DELTA_ROWS, 512), rev(0)),
                   _const_spec((1, 128)), _const_spec((1, 128)), _const_spec((1, 128))] + [ANY] * n_ex,
        out_shape=[jax.ShapeDtypeStruct((t, 1536), F32), jax.ShapeDtypeStruct((t, 128), BF16),
                   jax.ShapeDtypeStruct((t, 512), BF16),
                   jax.ShapeDtypeStruct((1, 128), F32), jax.ShapeDtypeStruct((1, 128), F32), jax.ShapeDtypeStruct((1, 128), F32)]
        + [jax.ShapeDtypeStruct((3,) + p.shape[1:], p.dtype) for p in ps],
        scratch_shapes=[pltpu.VMEM((512, 128), F32)]
        + ([pltpu.SemaphoreType.DMA((n_ex, 3)), pltpu.SemaphoreType.DMA((n_ex, 3))] if n_ex else []),
        compiler_params=_cparams(1),
    )(qkv, proj, proj, ssave, tsave, a_log, dt_bias, dn_g, dyc, *ps)


def loss_head(name, y, target, tm):
    t, d = y.shape

    def body(y_ref, t_ref, dy_ref, l_ref):
        err = y_ref[...] - t_ref[...]
        dy_ref[...] = err * (1.0 / d)
        part = 0.5 * jnp.sum(jnp.sum(err * err, axis=-1, keepdims=True) * (1.0 / d), axis=0, keepdims=True)

        @pl.when(pl.program_id(0) == 0)
        def _():
            l_ref[...] = part

        @pl.when(pl.program_id(0) > 0)
        def _():
            l_ref[...] += part

    return pl.pallas_call(
        body, name=name, grid=(t // tm,),
        in_specs=[_row_spec(tm, d, 0), _row_spec(tm, d, 0)],
        out_specs=[_row_spec(tm, d, 0), _const_spec((1, 1))],
        out_shape=[jax.ShapeDtypeStruct((t, d), F32), jax.ShapeDtypeStruct((1, 1), F32)],
        compiler_params=_cparams(1),
    )(y, target)


TM = 512
TM_MERGE = 256
TM_IN = 1024
TN_IN = 1152


def _lane_pad(v, n=128):
    return jnp.pad(v.astype(F32), (0, n - v.shape[0]))[None, :]


def f_norm_mod_res(x, g, scale, shift):
    return f_norm_mod(x, g, scale, shift), x


def prep_layer(w):
    p = dict(w)
    p["wp"] = _w_in_assemble(w["w_in"])
    p["wpa"] = _perm_heads_rows(w["w_proj_a"])
    p["dw32"] = jnp.pad(w["dw_w"], ((0, 32 - CONV_K), (0, 0)))
    p["sconv8"] = jnp.pad(w["sconv_w"], ((0, 8 - DN_CONV_K), (0, 0)))
    p["qg"] = jnp.tile(w["q_norm_g"], 2)[None, :]
    p["kg"] = jnp.tile(w["k_norm_g"], 2)[None, :]
    p["sinks128"] = _lane_pad(w["sinks"])
    p["al"] = _lane_pad(w["a_log"])
    p["dtb"] = _lane_pad(w["dt_bias"])
    p["dng"] = w["dn_norm_g"][None, :]
    return p


def layer_fwd(tag, x, mod, p):
    d = D_MODEL
    shift, scale, gate = mod[:, :d], mod[:, d:2 * d], mod[:, 2 * d:]
    g = p["norm_g"][None, :]
    (h,) = rowwise_fwd(f"norm_fwd{tag}", f_norm_mod, [(x, d, 0)], [g, scale, shift], [(d, BF16)], TM)
    proj = matmul_nn(f"inproj_fwd{tag}", h, p["wp"], F32, TM_IN, TN_IN, d)
    ya = attn_fwd(f"attn_fwd{tag}", proj, p["qg"], p["kg"], p["sinks128"])
    ub = glu_conv_fwd(f"glu_conv_fwd{tag}", proj, p["dw32"], p["dw_b"][None, :])
    conf_consts = [p["ln_g"][None, :], p["ln_b"][None, :], p["pw2_w"], p["pw2_b"][None, :]]
    (yb,) = rowwise_fwd(f"conf_fwd{tag}", f_conf_tail, [(ub, 512, 0), (proj, 512, P_ZB // 512)], conf_consts, [(512, F32)], TM)
    qkv = sconv_fwd(f"sconv_fwd{tag}", proj, p["sconv8"])
    yc, ssave, tsave = delta_fwd(f"delta_fwd{tag}", qkv, proj, p["al"], p["dtb"], p["dng"])
    merge_consts = [gate, p["wpa"], p["w_proj_b"], p["w_proj_c"], p["w_out"]]
    merge_rows = [(ya, 512, 0), (yb, 512, 0), (yc, 512, 0), (proj, 3 * d, P_MG // (3 * d)), (x, d, 0)]
    (xn,) = rowwise_fwd(f"merge_fwd{tag}", f_merge, merge_rows, merge_consts, [(d, F32)], TM_MERGE)
    saved = dict(x=x, h=h, proj=proj, ub=ub, qkv=qkv, ssave=ssave, tsave=tsave, norm_consts=[g, scale, shift],
                 conf_consts=conf_consts, merge_consts=merge_consts, merge_rows=merge_rows)
    return xn, saved


def layer_bwd(tag, dxn, p, s, exchange=None):
    d = D_MODEL
    proj = s["proj"]
    dya, dyb, dyc, dmg, dgate, dwpa, dwpb, dwpc, dwout = rowwise_bwd(
        f"merge_bwd{tag}", f_merge, s["merge_rows"], s["merge_consts"], [(dxn, d, 0)], [F32, F32, F32, BF16, None], TM_MERGE)
    dqz, dkv, dqg, dkg, dsinks = attn_bwd(f"attn_bwd{tag}", proj, p["qg"], p["kg"], p["sinks128"], dya)
    dub, dzb, dln_g, dln_b, dpw2_w, dpw2_b = rowwise_bwd(
        f"conf_bwd{tag}", f_conf_tail, [(s["ub"], 512, 0), (proj, 512, P_ZB // 512)], s["conf_consts"], [(dyb, 512, 0)],
        [F32, BF16], TM)
    dglu, ddw32, ddw_b = glu_conv_bwd(f"glu_conv_bwd{tag}", proj, p["dw32"], dub)
    dqkv, dab, dzc, dal, ddtb, ddng, *received = delta_bwd(f"delta_bwd{tag}", s["qkv"], proj, s["ssave"], s["tsave"], p["al"],
                                                           p["dtb"], p["dng"], dyc, exchange)
    dqkv_pre, dsconv8 = sconv_bwd(f"sconv_bwd{tag}", proj, p["sconv8"], dqkv)
    dproj = jnp.concatenate([dqz, dglu, dzb, dzc, dmg, dqkv_pre, dkv, dab], axis=1)
    dh = matmul_nn(f"inproj_bwd_dh{tag}", dproj, p["wp"], F32, TM_IN, d, P_TOTAL // 3, b_transposed=True)
    dwp = matmul_nn(f"inproj_bwd_dw{tag}", s["h"].T, dproj, F32, d, TN_IN, 2048)
    dx, dnorm_g, dscale, dshift = rowwise_bwd(
        f"norm_bwd{tag}", f_norm_mod_res, [(s["x"], d, 0)], s["norm_consts"], [(dh, d, 0), (dxn, d, 0)], [F32], TM)
    dmod = jnp.concatenate([dshift, dscale, dgate], axis=1)
    grads = dict(
        b_ada=dmod[0], norm_g=dnorm_g[0], w_in=_w_in_grad_blocks(dwp),
        q_norm_g=dqg[0, :64] + dqg[0, 64:], k_norm_g=dkg[0, :64] + dkg[0, 64:], sinks=dsinks[0, :ATT_HEADS],
        dw_w=ddw32[:CONV_K], dw_b=ddw_b[0], ln_g=dln_g[0], ln_b=dln_b[0], pw2_w=dpw2_w, pw2_b=dpw2_b[0],
        sconv_w=dsconv8[:DN_CONV_K], a_log=dal[0, :DN_HEADS], dt_bias=ddtb[0, :DN_HEADS], dn_norm_g=ddng[0],
        w_proj_a=_unperm_heads_rows(dwpa), w_proj_b=dwpb, w_proj_c=dwpc, w_out=dwout)
    return dx, grads, received


SHARDED = {"w_ada": 2, "w_in": 2, "dw_w": 2, "pw2_w": 1, "sconv_w": 2, "w_proj_a": 2, "w_proj_b": 2, "w_proj_c": 2,
           "w_out": 1}
GATHERED = tuple(n for n in SHARDED if n != "w_ada")
GATHER_F32 = ("dw_w", "sconv_w")
REDUCE_BIG = tuple(n for n in GATHERED if n not in GATHER_F32)
SMALL = ("b_ada", "norm_g", "q_norm_g", "k_norm_g", "sinks", "dw_b", "ln_g", "ln_b", "pw2_b", "a_log", "dt_bias",
         "dn_norm_g")
SMALL_ROWS = 104
SMALL_GRAD_ROWS = 448
W_IN_SHARD = D_IN // N_CHIPS
SUM_TILE = 256


def _w_in_orig():
    orig = np.full(P_TOTAL, -1, np.int64)
    p = 0
    for s, n in _in_pieces():
        orig[p:p + n] = np.arange(s, s + n)
        p += n
    return orig


def _w_in_blocks(k):
    orig = _w_in_orig().reshape(-1, 128)
    lo, hi = k * W_IN_SHARD, (k + 1) * W_IN_SHARD
    return [b for b in range(orig.shape[0]) if np.any((orig[b] >= lo) & (orig[b] < hi))]


W_IN_BLOCKS = max(len(_w_in_blocks(k)) for k in range(N_CHIPS))


def _runs(idx):
    out, i = [], 0
    while i < len(idx):
        j = i + 1
        while j < len(idx) and ((idx[i] < 0 and idx[j] < 0) or (idx[i] >= 0 and idx[j] == idx[j - 1] + 1)):
            j += 1
        out.append((int(idx[i]) if idx[i] >= 0 else -1, j - i))
        i = j
    return out


def _take_cols(a, idx):
    parts = [jnp.zeros(a.shape[:-1] + (n,), a.dtype) if s < 0 else a[..., s:s + n] for s, n in _runs(idx)]
    return parts[0] if len(parts) == 1 else jnp.concatenate(parts, axis=-1)


def _w_in_send(k, shard):
    orig = _w_in_orig().reshape(-1, 128)
    lo, hi = k * W_IN_SHARD, (k + 1) * W_IN_SHARD
    idx = np.concatenate([np.where((orig[b] >= lo) & (orig[b] < hi), orig[b] - lo, -1) for b in _w_in_blocks(k)])
    idx = np.concatenate([idx, np.full((W_IN_BLOCKS - len(_w_in_blocks(k))) * 128, -1)])
    return _take_cols(shard, idx)


def _w_in_assemble(blocks):
    where = [{b: i for i, b in enumerate(_w_in_blocks(k))} for k in range(N_CHIPS)]
    n_blocks = P_TOTAL // 128
    owners = [[(k, where[k][b]) for k in range(N_CHIPS) if b in where[k]] for b in range(n_blocks)]
    parts, b = [], 0
    while b < n_blocks:
        if len(owners[b]) == 1:
            k, pos = owners[b][0]
            e = b + 1
            while e < n_blocks and owners[e] == [(k, pos + e - b)]:
                e += 1
            parts.append(blocks[k][:, pos * 128:(pos + e - b) * 128])
            b = e
        else:
            parts.append(functools.reduce(jnp.add, [blocks[k][:, pos * 128:(pos + 1) * 128] for k, pos in owners[b]]))
            b += 1
    return jnp.concatenate(parts, axis=1)


def _w_in_grad_blocks(wp):
    out = []
    for k in range(N_CHIPS):
        idx = np.concatenate([np.arange(128 * b, 128 * b + 128) for b in _w_in_blocks(k)])
        idx = np.concatenate([idx, np.full((W_IN_BLOCKS - len(_w_in_blocks(k))) * 128, -1)])
        out.append(_take_cols(wp, idx))
    return jnp.stack(out)


def _w_in_receive_grad(k, blocks):
    orig = _w_in_orig()
    inv = np.zeros(D_IN, np.int64)
    inv[orig[orig >= 0]] = np.nonzero(orig >= 0)[0]
    where = {b: i for i, b in enumerate(_w_in_blocks(k))}
    cols = inv[k * W_IN_SHARD:(k + 1) * W_IN_SHARD]
    return _take_cols(blocks, np.array([where[c // 128] * 128 + c % 128 for c in cols]))


def _join_layer(v, axis):
    if axis == 2:
        return jnp.transpose(v, (1, 0, 2)).reshape(v.shape[1], N_CHIPS * v.shape[2])
    return v.reshape(N_CHIPS * v.shape[1], v.shape[2])


def _split_layer(v, axis):
    a, b = v.shape
    if axis == 2:
        return jnp.transpose(v.reshape(a, N_CHIPS, b // N_CHIPS), (1, 0, 2))
    return v.reshape(N_CHIPS, a // N_CHIPS, b)


def pack_small(vals, names, rows):
    flat = jnp.concatenate([vals[n].astype(F32).reshape(-1) for n in names])
    return jnp.pad(flat, (0, rows * 128 - flat.shape[0])).reshape(rows, 128)


def unpack_small(packed, names, shapes):
    flat = packed.reshape(-1)
    out, off = {}, 0
    for n in names:
        k = int(np.prod(shapes[n]))
        out[n] = flat[off:off + k].reshape(shapes[n])
        off += k
    return out


ANY = pl.BlockSpec(memory_space=pl.ANY)


def _place():
    x, y, c = lax.axis_index("x"), lax.axis_index("y"), lax.axis_index("c")
    chips = [(1 - x, y), (x, 1 - y), (1 - x, 1 - y)]
    return x, y, c, chips


def _remote(src, dst, send_sem, recv_sem, to):
    return pltpu.make_async_remote_copy(src_ref=src, dst_ref=dst, send_sem=send_sem, recv_sem=recv_sem, device_id=to,
                                        device_id_type=MESH)


def weights_allgather(slots):
    n = len(slots)

    def body(*refs):
        out = refs[n:2 * n]
        send_sems, recv_sems = refs[2 * n:]
        x, y, c, chips = _place()
        me, sibling, my_slot = (x, y, c), (x, y, 1 - c), 2 * x + y
        sends = []
        for j, chip in enumerate(chips):
            for t in range(n):
                mine = out[t].at[my_slot, c]
                sends.append(_remote(mine, mine, send_sems.at[t, j], recv_sems.at[t, j], (*chip, c)))
                sends[-1].start()
        for j, chip in enumerate(chips):
            for t in range(n):
                land = out[t].at[2 * chip[0] + chip[1], c]
                _remote(land, land, send_sems.at[t, j], recv_sems.at[t, j], me).wait_recv()
                sends.append(_remote(land, land, send_sems.at[t, 3 + j], recv_sems.at[t, 3 + j], sibling))
                sends[-1].start()
        for j, chip in enumerate(chips):
            for t in range(n):
                land = out[t].at[2 * chip[0] + chip[1], 1 - c]
                _remote(land, land, send_sems.at[t, 3 + j], recv_sems.at[t, 3 + j], me).wait_recv()
        for cp in sends:
            cp.wait_send()

    return pl.pallas_call(
        body, name="weights_allgather", out_shape=[jax.ShapeDtypeStruct(s.shape, s.dtype) for s in slots],
        in_specs=[ANY] * n, out_specs=[ANY] * n, input_output_aliases={t: t for t in range(n)},
        scratch_shapes=[pltpu.SemaphoreType.DMA((n, 6)), pltpu.SemaphoreType.DMA((n, 6))],
    )(*slots)


def grads_pair_send(layer, gs):
    n = len(gs)

    def body(*refs):
        g, recv = refs[:n], refs[n:2 * n]
        send_sems, recv_sems = refs[2 * n:]
        x, y, c, _ = _place()
        cps = [_remote(g[t], recv[t], send_sems.at[t], recv_sems.at[t], (x, y, 1 - c)) for t in range(n)]

        @pl.when(c != layer)
        def _():
            for cp in cps:
                cp.start()
            for cp in cps:
                cp.wait_send()

        @pl.when(c == layer)
        def _():
            for cp in cps:
                cp.wait_recv()

    return pl.pallas_call(
        body, name=f"grads_pair_send{layer}", out_shape=[jax.ShapeDtypeStruct(g.shape, g.dtype) for g in gs],
        in_specs=[ANY] * n, out_specs=[ANY] * n,
        scratch_shapes=[pltpu.SemaphoreType.DMA((n,)), pltpu.SemaphoreType.DMA((n,))],
    )(*gs)


def grads_pair_sum(name, g, recv):
    _, a, b = recv.shape
    ta = min(a, SUM_TILE)

    def body(a_ref, b_ref, o_ref):
        o_ref[...] = (a_ref[...] + b_ref[...]).astype(o_ref.dtype)

    spec = pl.BlockSpec((None, ta, b), lambda s, i: (s, i, 0))
    return pl.pallas_call(
        body, name=name, grid=(N_CHIPS, a // ta), in_specs=[spec, spec], out_specs=spec,
        out_shape=jax.ShapeDtypeStruct(recv.shape, BF16), compiler_params=_cparams(2),
    )(g, recv)


def _chip_exchange_copies(layer_owner_c, p, recv, send_sems, recv_sems):
    _, _, _, chips = _place()
    return [_remote(p[t].at[2 * chip[0] + chip[1]], recv[t].at[j], send_sems.at[t, j], recv_sems.at[t, j],
                    (*chip, layer_owner_c)) for j, chip in enumerate(chips) for t in range(len(p))]


def grads_chip_exchange(layer, ps):
    n = len(ps)

    def body(*refs):
        p, recv = refs[:n], refs[n:2 * n]
        send_sems, recv_sems = refs[2 * n:]

        @pl.when(lax.axis_index("c") == layer)
        def _():
            cps = _chip_exchange_copies(layer, p, recv, send_sems, recv_sems)
            for cp in cps:
                cp.start()
            for cp in cps:
                cp.wait()

    return pl.pallas_call(
        body, name=f"grads_chip_exchange{layer}", out_shape=[jax.ShapeDtypeStruct((3,) + p.shape[1:], p.dtype) for p in ps],
        in_specs=[ANY] * n, out_specs=[ANY] * n,
        scratch_shapes=[pltpu.SemaphoreType.DMA((n, 3)), pltpu.SemaphoreType.DMA((n, 3))],
    )(*ps)


def grads_chip_sum(name, layer, g, recv, recv2, into=None):
    _, a, b = recv.shape
    ta = min(a, SUM_TILE)
    my_slot = lambda: 2 * lax.axis_index("x") + lax.axis_index("y")

    def body(g_ref, r_ref, r2_ref, *rest):
        o_ref = rest[-1]
        own = g_ref[...] + r_ref[...]
        o_ref[...] = ((own + r2_ref[0].astype(F32)) + r2_ref[1].astype(F32)) + r2_ref[2].astype(F32)

    own_spec = pl.BlockSpec((None, ta, b), lambda i: (my_slot(), i, 0))
    return pl.pallas_call(
        body, name=name, grid=(a // ta,),
        in_specs=[own_spec, own_spec, pl.BlockSpec((3, ta, b), lambda i: (0, i, 0))] + ([] if into is None else [ANY]),
        out_specs=pl.BlockSpec((None, ta, b), lambda i: (layer, i, 0)),
        out_shape=jax.ShapeDtypeStruct((DEPTH, a, b), F32),
        input_output_aliases={} if into is None else {3: 0},
        compiler_params=_cparams(1),
    )(g, recv, recv2, *([] if into is None else [into]))


def grads_pair_gather(reds):
    n = len(reds)

    def body(*refs):
        buf = refs[n:2 * n]
        send_sems, recv_sems = refs[2 * n:]
        x, y, c, _ = _place()
        sibling = (x, y, 1 - c)
        cps = [_remote(buf[t].at[c], buf[t].at[c], send_sems.at[t], recv_sems.at[t], sibling) for t in range(n)]
        for cp in cps:
            cp.start()
        for t in range(n):
            _remote(buf[t].at[c], buf[t].at[1 - c], send_sems.at[t], recv_sems.at[t], sibling).wait_recv()
        for cp in cps:
            cp.wait_send()

    return pl.pallas_call(
        body, name="grads_pair_gather", out_shape=[jax.ShapeDtypeStruct(r.shape, r.dtype) for r in reds],
        in_specs=[ANY] * n, out_specs=[ANY] * n, input_output_aliases={t: t for t in range(n)},
        scratch_shapes=[pltpu.SemaphoreType.DMA((n,)), pltpu.SemaphoreType.DMA((n,))],
    )(*reds)


def small_allreduce(v):
    m, n = v.shape

    def body(x_ref, sum_ref, all_ref, send_sems, recv_sems, local_sem):
        x, y, c, chips = _place()
        me, sibling = (x, y, c), (x, y, 1 - c)

        def rows(px, py, pc):
            return all_ref.at[pl.ds((4 * px + 2 * py + pc) * m, m), :]

        def copy(k, block, to, src=None):
            return pltpu.make_async_remote_copy(src_ref=rows(*block) if src is None else src, dst_ref=rows(*block),
                                                send_sem=send_sems.at[k], recv_sem=recv_sems.at[k],
                                                device_id=to, device_id_type=MESH)

        mine = pltpu.make_async_copy(x_ref, rows(*me), local_sem)
        mine.start()
        first = [copy(0, me, sibling, src=x_ref)]
        first += [copy(1 + j, me, (*chip, c), src=x_ref) for j, chip in enumerate(chips)]
        for cp in first:
            cp.start()
        passed = [copy(4 + j, (*chip, c), sibling) for j, chip in enumerate(chips)]
        for j, chip in enumerate(chips):
            copy(1 + j, (*chip, c), me).wait_recv()
            passed[j].start()
        copy(0, sibling, me).wait_recv()
        for j, chip in enumerate(chips):
            copy(4 + j, (*chip, 1 - c), me).wait_recv()
        for cp in first + passed:
            cp.wait_send()
        mine.wait()
        acc = all_ref[0:m, :]
        for dev in range(1, 8):
            acc = acc + all_ref[dev * m:(dev + 1) * m, :]
        sum_ref[...] = acc

    vm = pl.BlockSpec(memory_space=pltpu.VMEM)
    return pl.pallas_call(
        body, name="small_allreduce",
        out_shape=[jax.ShapeDtypeStruct((m, n), F32), jax.ShapeDtypeStruct((8 * m, n), F32)],
        in_specs=[vm], out_specs=[vm, vm],
        scratch_shapes=[pltpu.SemaphoreType.DMA((7,)), pltpu.SemaphoreType.DMA((7,)), pltpu.SemaphoreType.DMA],
    )(v)


def grads_by_chip(layer_grads):
    return [layer_grads[n] if n == "w_in" else _split_layer(layer_grads[n], SHARDED[n]) for n in REDUCE_BIG]


def grads_pair_stage(layer, gs):
    recv = grads_pair_send(layer, gs)
    return recv, [grads_pair_sum(f"grads_pair_sum{layer}_{n}", g, r) for n, g, r in zip(REDUCE_BIG, gs, recv)]


def adamw(name, w, g, m, v, block):
    grid = tuple(s // b for s, b in zip(w.shape, block))

    def body(w_ref, g_ref, m_ref, v_ref, d_ref, nm_ref, nv_ref):
        gv = g_ref[...]
        nm = ADAM_B1 * m_ref[...] + (1.0 - ADAM_B1) * gv
        nv = ADAM_B2 * v_ref[...] + (1.0 - ADAM_B2) * (gv * gv)
        m_hat = nm / (1.0 - ADAM_B1 ** ADAM_STEP)
        v_hat = nv / (1.0 - ADAM_B2 ** ADAM_STEP)
        d_ref[...] = -ADAM_LR * (m_hat / (jnp.sqrt(v_hat) + ADAM_EPS) + ADAM_WD * w_ref[...])
        nm_ref[...] = nm
        nv_ref[...] = nv

    spec = pl.BlockSpec(tuple(block), lambda *idx: idx)
    return pl.pallas_call(
        body, name=name, grid=grid, in_specs=[spec] * 4, out_specs=[spec] * 3,
        out_shape=[jax.ShapeDtypeStruct(w.shape, F32)] * 3, compiler_params=_cparams(len(grid)),
    )(w, g, m, v)


ADAM_ROWS = {"w_ada": 512, "dw_w": 62, "pw2_w": 256, "sconv_w": 8, "w_proj_a": 512, "w_proj_b": 512, "w_proj_c": 512,
             "w_out": 256}
ADAM_W_IN_COLS = 331

WEIGHT_NAMES = ("w_ada", "b_ada", "norm_g", "w_in", "q_norm_g", "k_norm_g", "sinks", "dw_w", "dw_b", "ln_g", "ln_b",
                "pw2_w", "pw2_b", "sconv_w", "a_log", "dt_bias", "dn_norm_g", "w_proj_a", "w_proj_b", "w_proj_c", "w_out")


def kernel(x, c, w_ada, b_ada, norm_g, w_in, q_norm_g, k_norm_g, sinks, dw_w, dw_b, ln_g, ln_b, pw2_w, pw2_b, sconv_w, a_log, dt_bias, dn_norm_g, w_proj_a, w_proj_b, w_proj_c, w_out, loss_target, m_w_ada, m_b_ada, m_norm_g, m_w_in, m_q_norm_g, m_k_norm_g, m_sinks, m_dw_w, m_dw_b, m_ln_g, m_ln_b, m_pw2_w, m_pw2_b, m_sconv_w, m_a_log, m_dt_bias, m_dn_norm_g, m_w_proj_a, m_w_proj_b, m_w_proj_c, m_w_out, v_w_ada, v_b_ada, v_norm_g, v_w_in, v_q_norm_g, v_k_norm_g, v_sinks, v_dw_w, v_dw_b, v_ln_g, v_ln_b, v_pw2_w, v_pw2_b, v_sconv_w, v_a_log, v_dt_bias, v_dn_norm_g, v_w_proj_a, v_w_proj_b, v_w_proj_c, v_w_out):
    args = dict(locals())
    w = {n: args[n] for n in WEIGHT_NAMES}
    mom = {n: args["m_" + n] for n in WEIGHT_NAMES}
    var = {n: args["v_" + n] for n in WEIGHT_NAMES}

    chip = 2 * lax.axis_index("x") + lax.axis_index("y")
    slots = []
    for n in GATHERED:
        own = w[n] if n in GATHER_F32 else w[n].astype(BF16)
        if n == "w_in":
            own = lax.switch(chip, [functools.partial(_w_in_send, k) for k in range(N_CHIPS)], own)
        slots.append(lax.dynamic_update_slice(lax.empty((N_CHIPS,) + own.shape, own.dtype), own[None], (chip, 0, 0, 0)))
    gathered = dict(zip(GATHERED, weights_allgather(slots)))
    layers = []
    for l in range(DEPTH):
        lw = {n: w[n][l] for n in SMALL}
        for n in GATHERED:
            lw[n] = gathered[n][:, l] if n == "w_in" else _join_layer(gathered[n][:, l], SHARDED[n])
        layers.append(prep_layer(lw))

    mod, conds = ada_fwd(jnp.tile(c, (8, 1)), w["w_ada"], w["b_ada"])
    act, saved = x[0], []
    for l in range(DEPTH):
        act, s = layer_fwd(str(l), act, mod[l:l + 1], layers[l])
        saved.append(s)
    dact, loss_part = loss_head("loss_head", act, loss_target[0], TM)
    loss = lax.psum(loss_part[0, 0], ("x", "y", "c"))
    layer_grads = [None] * DEPTH
    dact, layer_grads[1], _ = layer_bwd("1", dact, layers[1], saved[1])
    gs1 = grads_by_chip(layer_grads[1])
    recv1, parts1 = grads_pair_stage(1, gs1)
    dact, layer_grads[0], got1 = layer_bwd("0", dact, layers[0], saved[0], exchange=(1, parts1))

    gs0 = grads_by_chip(layer_grads[0])
    recv0, parts0 = grads_pair_stage(0, gs0)
    got0 = grads_chip_exchange(0, parts0)
    reds = [grads_chip_sum(f"grads_chip_sum1_{n}", 1, g, r, r2) for n, g, r, r2 in zip(REDUCE_BIG, gs1, recv1, got1)]
    reds = [grads_chip_sum(f"grads_chip_sum0_{n}", 0, g, r, r2, into=red)
            for n, g, r, r2, red in zip(REDUCE_BIG, gs0, recv0, got0, reds)]
    final_grads = dict(zip(REDUCE_BIG, grads_pair_gather(reds)))
    final_grads["w_in"] = lax.switch(chip, [functools.partial(_w_in_receive_grad, k) for k in range(N_CHIPS)],
                                     final_grads["w_in"])
    small_names = SMALL + GATHER_F32
    small_shapes = {n: (DEPTH,) + layer_grads[0][n].shape for n in small_names}
    small_full = {n: jnp.stack([layer_grads[l][n] for l in range(DEPTH)]) for n in small_names}
    small_sum, small_all = small_allreduce(pack_small(small_full, small_names, SMALL_GRAD_ROWS))
    small_sum = unpack_small(small_sum, small_names, small_shapes)
    for n in GATHER_F32:
        width = w[n].shape[2]
        final_grads[n] = lax.dynamic_slice_in_dim(small_sum[n], chip * width, width, axis=2)
    n_mod = DEPTH * 3 * D_MODEL
    dmod = small_all.reshape(8, -1)[:, :n_mod].reshape(8, DEPTH, 3 * D_MODEL)
    width = w["w_ada"].shape[2]
    dmod = jnp.transpose(lax.dynamic_slice_in_dim(dmod, chip * width, width, axis=2), (1, 0, 2))
    final_grads["w_ada"] = ada_bwd(conds, dmod)
    final_grads.update({n: small_sum[n] for n in SMALL})
    small_grads = pack_small(final_grads, SMALL, SMALL_ROWS)

    delta, new_m, new_v = {}, {}, {}
    for n in SHARDED:
        shp = w[n].shape
        if n == "w_in":
            view = lambda a: jnp.transpose(a, (2, 0, 1))
            back = lambda a: jnp.transpose(a, (1, 2, 0))
            g3 = view(final_grads[n])
            final_grads[n] = back(g3)
            d, nm, nv = adamw("adamw_" + n, view(w[n]), g3, view(mom[n]), view(var[n]), (ADAM_W_IN_COLS, shp[0], shp[1]))
        else:
            view = lambda a, shp=shp: a.reshape(shp[0] * shp[1], shp[2])
            back = lambda a, shp=shp: a.reshape(shp)
            d, nm, nv = adamw("adamw_" + n, view(w[n]), view(final_grads[n]), view(mom[n]), view(var[n]),
                              (ADAM_ROWS[n], shp[2]))
        delta[n], new_m[n], new_v[n] = back(d), back(nm), back(nv)
    d, nm, nv = adamw("adamw_small", pack_small(w, SMALL, SMALL_ROWS), small_grads, pack_small(mom, SMALL, SMALL_ROWS),
                      pack_small(var, SMALL, SMALL_ROWS), (SMALL_ROWS, 128))
    delta.update(unpack_small(d, SMALL, small_shapes))
    new_m.update(unpack_small(nm, SMALL, small_shapes))
    new_v.update(unpack_small(nv, SMALL, small_shapes))

    return (loss, dact[None], *[final_grads[n] for n in WEIGHT_NAMES], *[delta[n] for n in WEIGHT_NAMES],
            *[new_m[n] for n in WEIGHT_NAMES], *[new_v[n] for n in WEIGHT_NAMES])
```

```python
import functools

import numpy as np
import jax
import jax.numpy as jnp
from jax import lax
from jax.experimental import pallas as pl
from jax.experimental.pallas import tpu as pltpu

F32 = jnp.float32
BF16 = jnp.bfloat16
MESH = pl.DeviceIdType.MESH

D_MODEL = 1024
DEPTH = 2
ATT_HEADS = 8
ATT_HEAD_DIM = 64
WINDOW = 128
CONV_K = 31
DN_HEADS = 4
DN_CONV_K = 4
DN_CHUNK = 64
EPS = 1e-6
NEG_INF = -1e30
N_CHIPS = 4
D_IN = 7944

ADAM_LR = 0.001
ADAM_B1 = 0.9
ADAM_B2 = 0.999
ADAM_EPS = 1e-08
ADAM_WD = 0.01
ADAM_STEP = 10

VMEM_LIMIT = 56 * 1024 * 1024

P_QA, P_ZA, P_GLU, P_ZB, P_ZC, P_MG, P_QKV, P_KA, P_VA, P_AB, P_TOTAL = (
    0, 512, 1024, 2048, 2560, 3072, 6144, 7680, 7808, 7936, 8064)
HEAD_ORDER = (0, 4, 1, 5, 2, 6, 3, 7)


def _in_pieces():
    p = [(0 + 64 * h, 64) for h in HEAD_ORDER]
    p += [(768 + 64 * h, 64) for h in HEAD_ORDER]
    for g in range(4):
        p += [(1280 + 128 * g, 128), (1792 + 128 * g, 128)]
    p += [(2304, 512), (4360, 512), (4872, 3072), (2816, 1536), (512, 128), (640, 128), (4352, 8)]
    return p


def _perm_heads_rows(w):
    return jnp.concatenate([w[64 * h:64 * h + 64] for h in HEAD_ORDER], axis=0)


def _unperm_heads_rows(w):
    inv = [HEAD_ORDER.index(h) for h in range(8)]
    return jnp.concatenate([w[64 * s:64 * s + 64] for s in inv], axis=0)


def _split_bf16(a, terms):
    out, rest = [], a.astype(F32)
    for _ in range(terms - 1):
        out.append(rest.astype(BF16))
        rest = rest - out[-1].astype(F32)
    return out + [rest.astype(BF16)]


def _dot(a, b, dims, exact):
    d = lambda p, q: lax.dot_general(p, q, (dims, ((), ())), preferred_element_type=F32)
    if exact:
        (ah, al), (bh, bl) = _split_bf16(a, 2), _split_bf16(b, 2)
        return d(ah, bh) + (d(ah, bl) + d(al, bh))
    return d(a.astype(BF16), b.astype(BF16))


def _make_mm(exact):
    @jax.custom_vjp
    def nn(a, b):
        return _dot(a, b, ((1,), (0,)), exact)

    @jax.custom_vjp
    def nt(a, b):
        return _dot(a, b, ((1,), (1,)), exact)

    @jax.custom_vjp
    def tn(a, b):
        return _dot(a, b, ((0,), (0,)), exact)

    nn.defvjp(lambda a, b: (nn(a, b), (a, b)),
              lambda r, g: (nt(g, r[1]).astype(r[0].dtype), tn(r[0], g).astype(r[1].dtype)))
    nt.defvjp(lambda a, b: (nt(a, b), (a, b)),
              lambda r, g: (nn(g, r[1]).astype(r[0].dtype), tn(g, r[0]).astype(r[1].dtype)))
    tn.defvjp(lambda a, b: (tn(a, b), (a, b)),
              lambda r, g: (nt(r[1], g).astype(r[0].dtype), nn(r[0], g).astype(r[1].dtype)))
    return nn, nt, tn


mm, mm_nt, mm_tn = _make_mm(False)
xmm, xmm_nt, xmm_tn = _make_mm(True)


@jax.custom_vjp
def sel_mm(m, g):
    mb = m.astype(BF16)
    parts = [jnp.dot(mb, p, preferred_element_type=F32) for p in _split_bf16(g, 3)]
    return parts[0] + (parts[1] + parts[2])


def _sel_mm_bwd(m, dy):
    mb = m.astype(BF16)
    parts = [lax.dot_general(mb, p, (((0,), (0,)), ((), ())), preferred_element_type=F32) for p in _split_bf16(dy, 3)]
    return jnp.zeros_like(m), parts[0] + (parts[1] + parts[2])


sel_mm.defvjp(lambda m, g: (sel_mm(m, g), m), _sel_mm_bwd)


@jax.custom_vjp
def tri_inv(*mats):
    n = mats[0].shape[0]
    eye = jnp.where(lax.broadcasted_iota(jnp.int32, (n, n), 0) == lax.broadcasted_iota(jnp.int32, (n, n), 1), 1.0, 0.0)
    ts = [eye - a for a in mats]
    pws = list(mats)
    for _ in range(5):
        pws = [xmm(pw, pw) for pw in pws]
        ts = [t + xmm(t, pw) for t, pw in zip(ts, pws)]
    return tuple(ts)


def _tri_inv_bwd(ts, dts):
    inner = [xmm_nt(dt, t) for t, dt in zip(ts, dts)]
    return tuple(-xmm_tn(t, m) for t, m in zip(ts, inner))


tri_inv.defvjp(lambda *mats: (tri_inv(*mats),) * 2, _tri_inv_bwd)


@jax.custom_vjp
def tri_inv_known(a, t):
    return t


tri_inv_known.defvjp(lambda a, t: (t, t), lambda t, dt: (_tri_inv_bwd((t,), (dt,))[0], jnp.zeros_like(t)))


def _sigmoid(x):
    return 1.0 / (1.0 + jnp.exp(-x))


def _silu(x):
    return x * _sigmoid(x)


def _softplus(x):
    return jnp.maximum(x, 0.0) + jnp.log(1.0 + jnp.exp(-jnp.abs(x)))


def _cparams(n_grid):
    return pltpu.CompilerParams(dimension_semantics=("arbitrary",) * n_grid, vmem_limit_bytes=VMEM_LIMIT)


def _row_spec(tm, width, colblk):
    return pl.BlockSpec((tm, width), lambda i, cb=colblk: (i, cb))


def _const_spec(shape):
    nd = len(shape)
    return pl.BlockSpec(tuple(shape), lambda i, nd=nd: (0,) * nd)


def rowwise_fwd(name, f, rows, consts, outs, tm):
    n_r, n_c = len(rows), len(consts)
    t = rows[0][0].shape[0]

    def body(*refs):
        vals = [r[...] for r in refs[:n_r + n_c]]
        res = f(*vals)
        if not isinstance(res, (tuple, list)):
            res = (res,)
        for o_ref, v in zip(refs[n_r + n_c:], res):
            o_ref[...] = v.astype(o_ref.dtype)

    return pl.pallas_call(
        body, name=name, grid=(t // tm,),
        in_specs=[_row_spec(tm, w, cb) for _, w, cb in rows] + [_const_spec(c.shape) for c in consts],
        out_specs=[_row_spec(tm, w, 0) for w, _ in outs],
        out_shape=[jax.ShapeDtypeStruct((t, w), dt) for w, dt in outs],
        compiler_params=_cparams(1),
    )(*[a for a, _, _ in rows], *consts)


def rowwise_bwd(name, f, rows, consts, cts, row_grad_dtypes, tm, carry=None):
    n_r, n_c, n_ct = len(rows), len(consts), len(cts)
    t = rows[0][0].shape[0]
    keep = [k for k, dt in enumerate(row_grad_dtypes) if dt is not None]
    c_ins, c_in_specs, c_outs, c_out_specs, c_scratch = _host(carry)
    n_in, n_out = n_r + n_c + n_ct, len(keep) + n_c

    def body(*refs):
        ins = [r[...].astype(F32) for r in refs[:n_r + n_c]]
        g_out = [r[...].astype(F32) for r in refs[n_r + n_c:n_in]]
        out_refs = refs[n_in + len(c_ins):n_in + len(c_ins) + n_out]
        carried = (refs[n_in:n_in + len(c_ins)], refs[n_in + len(c_ins) + n_out:n_in + len(c_ins) + n_out + len(c_outs)],
                   *refs[n_in + len(c_ins) + n_out + len(c_outs):])
        if carry is not None:
            carry.emit_start(pl.program_id(0) == 0, *carried)

        def fw(*a):
            res = f(*a)
            return tuple(res) if isinstance(res, (tuple, list)) else (res,)

        _, vjp = jax.vjp(fw, *ins)
        grads = vjp(tuple(g_out))
        for o_ref, k in zip(out_refs[:len(keep)], keep):
            o_ref[...] = grads[k].astype(o_ref.dtype)
        first = pl.program_id(0) == 0
        for o_ref, g in zip(out_refs[len(keep):], grads[n_r:]):
            @pl.when(first)
            def _(o_ref=o_ref, g=g):
                o_ref[...] = g

            @pl.when(jnp.logical_not(first))
            def _(o_ref=o_ref, g=g):
                o_ref[...] += g
        if carry is not None:
            carry.emit_finish(pl.program_id(0) == t // tm - 1, *carried)

    return pl.pallas_call(
        body, name=name, grid=(t // tm,),
        in_specs=[_row_spec(tm, w, cb) for _, w, cb in rows] + [_const_spec(c.shape) for c in consts]
        + [_row_spec(tm, w, cb) for _, w, cb in cts] + c_in_specs,
        out_specs=[_row_spec(tm, rows[k][1], 0) for k in keep] + [_const_spec(c.shape) for c in consts] + c_out_specs,
        out_shape=[jax.ShapeDtypeStruct((t, rows[k][1]), row_grad_dtypes[k]) for k in keep]
        + [jax.ShapeDtypeStruct(c.shape, F32) for c in consts] + c_outs,
        scratch_shapes=c_scratch,
        compiler_params=_cparams(1),
    )(*[a for a, _, _ in rows], *consts, *[a for a, _, _ in cts], *c_ins)


def f_norm_mod(x, g, scale, shift):
    y = x * lax.rsqrt(jnp.mean(x * x, axis=-1, keepdims=True) + EPS) * g
    return y * (1.0 + scale) + shift


def f_conf_tail(u, zb, ln_g, ln_b, pw2_w, pw2_b):
    mu = jnp.mean(u, axis=-1, keepdims=True)
    xc = u - mu
    var = jnp.mean(xc * xc, axis=-1, keepdims=True)
    y = _silu(xc * lax.rsqrt(var + EPS) * ln_g + ln_b)
    return (mm(y, pw2_w) + pw2_b) * _silu(zb)


def f_merge(ya, yb, yc, mg, x, gate, wpa, wpb, wpc, wout):
    d = D_MODEL
    merged = (_sigmoid(mg[:, :d]) * mm(ya, wpa) + _sigmoid(mg[:, d:2 * d]) * mm(yb, wpb)
              + _sigmoid(mg[:, 2 * d:]) * mm(yc, wpc))
    return x + gate * mm(merged, wout)


def matmul_nn(name, a, b, out_dtype, tm, tn, tk, b_transposed=False, carry=None):
    m, k = a.shape
    n = b.shape[0] if b_transposed else b.shape[1]
    nk = k // tk
    grid = (m // tm, n // tn, nk)
    b_spec = (pl.BlockSpec((tn, tk), lambda i, j, kk: (j, kk)) if b_transposed
              else pl.BlockSpec((tk, tn), lambda i, j, kk: (kk, j)))
    c_ins, c_in_specs, c_outs, c_out_specs, c_scratch = _host(carry)
    n_ci, n_co = len(c_ins), len(c_outs)

    def body(*refs):
        a_ref, b_ref, o_ref = refs[0], refs[1], refs[2 + n_ci]
        carried = (refs[2:2 + n_ci], refs[3 + n_ci:3 + n_ci + n_co], *refs[3 + n_ci + n_co:3 + n_ci + n_co + len(c_scratch)])
        at = lambda step: functools.reduce(jnp.logical_and, [pl.program_id(d) == s for d, s in enumerate(step)])
        if carry is not None:
            carry.emit_start(at((0, 0, 0)), *carried)
        part = lax.dot_general(a_ref[...].astype(BF16), b_ref[...].astype(BF16),
                               (((1,), (1 if b_transposed else 0,)), ((), ())), preferred_element_type=F32)
        if nk == 1:
            o_ref[...] = part.astype(o_ref.dtype)
        else:
            kk = pl.program_id(2)
            acc_ref = refs[-1]

            @pl.when(kk == 0)
            def _():
                acc_ref[...] = part

            @pl.when(kk > 0)
            def _():
                acc_ref[...] += part

            @pl.when(kk == nk - 1)
            def _():
                o_ref[...] = acc_ref[...].astype(o_ref.dtype)
        if carry is not None:
            carry.emit_finish(at(tuple(g - 1 for g in grid)), *carried)

    res = pl.pallas_call(
        body, name=name, grid=grid,
        in_specs=[pl.BlockSpec((tm, tk), lambda i, j, kk: (i, kk)), b_spec] + c_in_specs,
        out_specs=[pl.BlockSpec((tm, tn), lambda i, j, kk: (i, j))] + c_out_specs,
        out_shape=[jax.ShapeDtypeStruct((m, n), out_dtype)] + c_outs,
        input_output_aliases={} if carry is None else carry.aliases(2, 1),
        scratch_shapes=c_scratch + ([] if nk == 1 else [pltpu.VMEM((tm, tn), F32)]),
        compiler_params=_cparams(3),
    )(a, b, *c_ins)
    return res[0] if carry is None else res


def ada_fwd(c8, w_shard, b_ada):
    n_cols = w_shard.shape[2]
    masks = [(m >> 2 & 1, m >> 1 & 1, m & 1) for m in range(1, 8)]

    def body(c_ref, w_ref, b_ref, mod_ref, conds_ref, cbuf, sendbuf, recvbuf, send_sems, recv_sems):
        x, y, c, chips = _place()
        flip = lambda v, bit: 1 - v if bit else v
        peers = [(flip(x, mx), flip(y, my), flip(c, mc)) for mx, my, mc in masks]
        dev = lambda p: 4 * p[0] + 2 * p[1] + p[2]
        cbuf[dev((x, y, c))] = c_ref[...]
        first = [_remote(c_ref, cbuf.at[dev((x, y, c))], send_sems.at[i], recv_sems.at[i], p) for i, p in enumerate(peers)]
        for cp in first:
            cp.start()
        for i, p in enumerate(peers):
            _remote(c_ref, cbuf.at[dev(p)], send_sems.at[i], recv_sems.at[i], p).wait_recv()
        conds = jnp.concatenate([cbuf[d, 0:1, :] for d in range(8)], axis=0)
        conds_ref[...] = conds
        act = _silu(conds)
        parts = [mm(act, w_ref[l]) for l in range(DEPTH)]
        row8 = lax.broadcasted_iota(jnp.int32, (8, 1), 0)

        def tile_for(chip):
            r = 2 * (2 * chip[0] + chip[1]) + c
            rows = [jnp.sum(jnp.where(row8 == r, parts[l], 0.0), axis=0, keepdims=True) for l in range(DEPTH)]
            return jnp.where(row8 == 0, rows[0], jnp.where(row8 == 1, rows[1], 0.0))

        my_slot = 2 * x + y
        recvbuf[my_slot] = tile_for((x, y))
        second = []
        for j, chip in enumerate(chips):
            sendbuf[j] = tile_for(chip)
            second.append(_remote(sendbuf.at[j], recvbuf.at[my_slot], send_sems.at[7 + j], recv_sems.at[7 + j], (*chip, c)))
            second[-1].start()
        for j, chip in enumerate(chips):
            _remote(sendbuf.at[j], recvbuf.at[2 * chip[0] + chip[1]], send_sems.at[7 + j], recv_sems.at[7 + j],
                    (*chip, c)).wait_recv()
        rows = [jnp.concatenate([recvbuf[k, l:l + 1, :] for k in range(N_CHIPS)], axis=1) + b_ref[l:l + 1, :]
                for l in range(DEPTH)]
        mod_ref[...] = jnp.concatenate(rows + [jnp.zeros((8 - DEPTH, N_CHIPS * n_cols), F32)], axis=0)
        for cp in first + second:
            cp.wait_send()

    vm = pl.BlockSpec(memory_space=pltpu.VMEM)
    return pl.pallas_call(
        body, name="ada_fwd",
        out_shape=[jax.ShapeDtypeStruct((8, N_CHIPS * n_cols), F32), jax.ShapeDtypeStruct((8, D_MODEL), F32)],
        in_specs=[vm, vm, vm], out_specs=[vm, vm],
        scratch_shapes=[pltpu.VMEM((8, 8, D_MODEL), F32), pltpu.VMEM((3, 8, n_cols), F32),
                        pltpu.VMEM((N_CHIPS, 8, n_cols), F32), pltpu.SemaphoreType.DMA((10,)), pltpu.SemaphoreType.DMA((10,))],
        compiler_params=pltpu.CompilerParams(vmem_limit_bytes=VMEM_LIMIT),
    )(c8, w_shard, b_ada)


def ada_bwd(conds, dmod):
    def body(c_ref, d_ref, o_ref):
        act = _silu(c_ref[...])
        for l in range(DEPTH):
            o_ref[l] = mm_tn(act, d_ref[l])

    return pl.pallas_call(
        body, name="ada_bwd", out_shape=jax.ShapeDtypeStruct((DEPTH, D_MODEL, dmod.shape[2]), F32),
        compiler_params=pltpu.CompilerParams(vmem_limit_bytes=VMEM_LIMIT),
    )(conds, dmod)


def _f_attn(first_block, q, za, kc, vc, kp, vp, qg, kg, sinks):
    w = WINDOW
    lane = lax.broadcasted_iota(jnp.int32, (1, 128), 1)
    halves = [lane < 64, lane >= 64]

    def rms_halves(x, g):
        x2 = x * x
        s0 = jnp.sum(jnp.where(halves[0], x2, 0.0), axis=-1, keepdims=True)
        s1 = jnp.sum(jnp.where(halves[1], x2, 0.0), axis=-1, keepdims=True)
        r = jnp.where(halves[0], lax.rsqrt(s0 / 64.0 + EPS), lax.rsqrt(s1 / 64.0 + EPS))
        return x * r * g

    kcat = rms_halves(jnp.concatenate([kp, kc], axis=0), kg)
    vcat = jnp.concatenate([vp, vc], axis=0)
    qi = lax.broadcasted_iota(jnp.int32, (w, 2 * w), 0)
    kj = lax.broadcasted_iota(jnp.int32, (w, 2 * w), 1)
    dist = qi + w - kj
    valid = (dist >= 0) & (dist < w) & (jnp.logical_not(first_block) | (kj >= w))
    distf = dist.astype(F32)
    units = [(grp, half) for grp in range(4) for half in range(2)]
    qns = [rms_halves(q[:, 128 * grp:128 * grp + 128], qg) * (ATT_HEAD_DIM ** -0.5) for grp in range(4)]
    vhalf = [jnp.where(halves[half], vcat, 0.0) for half in range(2)]
    scores, sinks_h = [], []
    for grp, half in units:
        head = HEAD_ORDER[2 * grp + half]
        slope = 2.0 ** (-8.0 * (head + 1) / ATT_HEADS)
        sinks_h.append(jnp.sum(jnp.where(lane == head, sinks, 0.0), axis=-1, keepdims=True))
        s = mm_nt(jnp.where(halves[half], qns[grp], 0.0), kcat) - slope * distf
        scores.append(jnp.where(valid, s, NEG_INF))
    probs = []
    for s, sink in zip(scores, sinks_h):
        m = lax.stop_gradient(jnp.maximum(jnp.max(s, axis=-1, keepdims=True), sink))
        p = jnp.exp(s - m)
        probs.append(p / (jnp.sum(p, axis=-1, keepdims=True) + jnp.exp(sink - m)))
    outs = [mm(p, vhalf[half]) for p, (grp, half) in zip(probs, units)]
    return jnp.concatenate([outs[2 * grp] + outs[2 * grp + 1] for grp in range(4)], axis=1) * _silu(za)


def attn_fwd(name, proj, qg, kg, sinks):
    t = proj.shape[0]
    nb = t // WINDOW

    def body(q_ref, za_ref, kc_ref, vc_ref, kp_ref, vp_ref, qg_ref, kg_ref, s_ref, o_ref):
        first = pl.program_id(0) == 0
        o_ref[...] = _f_attn(first, q_ref[...], za_ref[...], kc_ref[...], vc_ref[...], kp_ref[...], vp_ref[...],
                             qg_ref[...], kg_ref[...], s_ref[...])

    cur = lambda cb: (lambda i: (i, cb))
    prev = lambda cb: (lambda i: (jnp.maximum(i - 1, 0), cb))
    return pl.pallas_call(
        body, name=name, grid=(nb,),
        in_specs=[pl.BlockSpec((WINDOW, 512), cur(P_QA // 512)), pl.BlockSpec((WINDOW, 512), cur(P_ZA // 512)),
                  pl.BlockSpec((WINDOW, 128), cur(P_KA // 128)), pl.BlockSpec((WINDOW, 128), cur(P_VA // 128)),
                  pl.BlockSpec((WINDOW, 128), prev(P_KA // 128)), pl.BlockSpec((WINDOW, 128), prev(P_VA // 128)),
                  _const_spec((1, 128)), _const_spec((1, 128)), _const_spec((1, 128))],
        out_specs=pl.BlockSpec((WINDOW, 512), lambda i: (i, 0)),
        out_shape=jax.ShapeDtypeStruct((t, 512), F32),
        compiler_params=_cparams(1),
    )(proj, proj, proj, proj, proj, proj, qg, kg, sinks)


def attn_bwd(name, proj, qg, kg, sinks, dya):
    t = proj.shape[0]
    nb = t // WINDOW

    def body(q_ref, za_ref, kc_ref, vc_ref, kp_ref, vp_ref, qg_ref, kg_ref, s_ref, dy_ref,
             dqz_ref, dkv_ref, dqg_ref, dkg_ref, ds_ref, carry_ref):
        j = pl.program_id(0)
        first = j == nb - 1

        @pl.when(j == 0)
        def _():
            carry_ref[...] = jnp.zeros_like(carry_ref)
            dqg_ref[...] = jnp.zeros_like(dqg_ref)
            dkg_ref[...] = jnp.zeros_like(dkg_ref)
            ds_ref[...] = jnp.zeros_like(ds_ref)

        ins = [r[...] for r in (q_ref, za_ref, kc_ref, vc_ref, kp_ref, vp_ref, qg_ref, kg_ref, s_ref)]
        _, vjp = jax.vjp(functools.partial(_f_attn, first), *ins)
        dq, dza, dkc, dvc, dkp, dvp, dqg, dkg, dsk = vjp(dy_ref[...])
        dqz_ref[:, 0:512] = dq.astype(dqz_ref.dtype)
        dqz_ref[:, 512:1024] = dza.astype(dqz_ref.dtype)
        dkv_ref[:, 0:128] = (dkc + carry_ref[0]).astype(dkv_ref.dtype)
        dkv_ref[:, 128:256] = (dvc + carry_ref[1]).astype(dkv_ref.dtype)
        carry_ref[0] = dkp
        carry_ref[1] = dvp
        dqg_ref[...] += dqg
        dkg_ref[...] += dkg
        ds_ref[...] += dsk

    cur = lambda cb: (lambda j: (nb - 1 - j, cb))
    prev = lambda cb: (lambda j: (jnp.maximum(nb - 2 - j, 0), cb))
    return pl.pallas_call(
        body, name=name, grid=(nb,),
        in_specs=[pl.BlockSpec((WINDOW, 512), cur(P_QA // 512)), pl.BlockSpec((WINDOW, 512), cur(P_ZA // 512)),
                  pl.BlockSpec((WINDOW, 128), cur(P_KA // 128)), pl.BlockSpec((WINDOW, 128), cur(P_VA // 128)),
                  pl.BlockSpec((WINDOW, 128), prev(P_KA // 128)), pl.BlockSpec((WINDOW, 128), prev(P_VA // 128)),
                  _const_spec((1, 128)), _const_spec((1, 128)), _const_spec((1, 128)),
                  pl.BlockSpec((WINDOW, 512), cur(0))],
        out_specs=[pl.BlockSpec((WINDOW, 1024), cur(0)), pl.BlockSpec((WINDOW, 256), cur(0)),
                   _const_spec((1, 128)), _const_spec((1, 128)), _const_spec((1, 128))],
        out_shape=[jax.ShapeDtypeStruct((t, 1024), BF16), jax.ShapeDtypeStruct((t, 256), BF16),
                   jax.ShapeDtypeStruct((1, 128), F32), jax.ShapeDtypeStruct((1, 128), F32),
                   jax.ShapeDtypeStruct((1, 128), F32)],
        scratch_shapes=[pltpu.VMEM((2, WINDOW, 128), F32)],
        compiler_params=_cparams(1),
    )(proj, proj, proj, proj, proj, proj, qg, kg, sinks, dya)


CONV_ROWS = 256


def _conv_taps(src_ref, w_ref, n_taps, base, t):
    for r0 in range(0, t, CONV_ROWS):
        acc = w_ref[0:1, :] * src_ref[pl.ds(r0 + base, CONV_ROWS), :]
        for k in range(1, n_taps):
            acc = acc + w_ref[k:k + 1, :] * src_ref[pl.ds(r0 + base + k, CONV_ROWS), :]
        yield r0, acc


def _conv_wgrad(dy_ref, src_ref, n_taps, base, t, dy_base=0):
    out = []
    for k in range(n_taps):
        acc = jnp.zeros((8, 128), F32)
        for r0 in range(0, t, CONV_ROWS):
            prod = dy_ref[pl.ds(r0 + dy_base, CONV_ROWS), :] * src_ref[pl.ds(r0 + base + k, CONV_ROWS), :]
            acc = acc + jnp.sum(prod.reshape(CONV_ROWS // 8, 8, 128), axis=0)
        out.append(jnp.sum(acc, axis=0, keepdims=True))
    return out


def glu_conv_fwd(name, proj, w32, bias):
    t = proj.shape[0]
    pad = 32

    def body(x_ref, w_ref, b_ref, o_ref, u_ref):
        u_ref[0:pad, :] = jnp.zeros((pad, 128), F32)
        u_ref[pad:pad + t, :] = x_ref[:, 0:128] * _sigmoid(x_ref[:, 128:256])
        for r0, acc in _conv_taps(u_ref, w_ref, CONV_K, pad - (CONV_K - 1), t):
            o_ref[pl.ds(r0, CONV_ROWS), :] = acc + b_ref[...]

    return pl.pallas_call(
        body, name=name, grid=(4,),
        in_specs=[pl.BlockSpec((t, 256), lambda cb: (0, P_GLU // 256 + cb)), pl.BlockSpec((32, 128), lambda cb: (0, cb)),
                  pl.BlockSpec((1, 128), lambda cb: (0, cb))],
        out_specs=pl.BlockSpec((t, 128), lambda cb: (0, cb)),
        out_shape=jax.ShapeDtypeStruct((t, 512), F32),
        scratch_shapes=[pltpu.VMEM((t + pad, 128), F32)],
        compiler_params=_cparams(1),
    )(proj, w32, bias)


def glu_conv_bwd(name, proj, w32, dub):
    t = proj.shape[0]
    pad = 32
    k1 = CONV_K - 1

    def body(x_ref, w_ref, dy_ref, dx_ref, dw_ref, db_ref, u_ref, dyp_ref, wrev_ref):
        val = x_ref[:, 0:128]
        sg = _sigmoid(x_ref[:, 128:256])
        u_ref[0:pad, :] = jnp.zeros((pad, 128), F32)
        u_ref[pad:pad + t, :] = val * sg
        dyp_ref[0:t, :] = dy_ref[...]
        dyp_ref[t:t + pad, :] = jnp.zeros((pad, 128), F32)
        for k in range(CONV_K):
            wrev_ref[k:k + 1, :] = w_ref[k1 - k:k1 - k + 1, :]
        wrev_ref[CONV_K:32, :] = jnp.zeros((32 - CONV_K, 128), F32)
        for r0, du in _conv_taps(dyp_ref, wrev_ref, CONV_K, 0, t):
            v = x_ref[pl.ds(r0, CONV_ROWS), 0:128]
            s = _sigmoid(x_ref[pl.ds(r0, CONV_ROWS), 128:256])
            dx_ref[pl.ds(r0, CONV_ROWS), 0:128] = (du * s).astype(dx_ref.dtype)
            dx_ref[pl.ds(r0, CONV_ROWS), 128:256] = (du * v * s * (1.0 - s)).astype(dx_ref.dtype)
        dws = _conv_wgrad(dyp_ref, u_ref, CONV_K, pad - k1, t)
        for k in range(CONV_K):
            dw_ref[k:k + 1, :] = dws[k]
        dw_ref[CONV_K:32, :] = jnp.zeros((32 - CONV_K, 128), F32)
        db_ref[...] = jnp.sum(dy_ref[...], axis=0, keepdims=True)

    return pl.pallas_call(
        body, name=name, grid=(4,),
        in_specs=[pl.BlockSpec((t, 256), lambda cb: (0, P_GLU // 256 + cb)), pl.BlockSpec((32, 128), lambda cb: (0, cb)),
                  pl.BlockSpec((t, 128), lambda cb: (0, cb))],
        out_specs=[pl.BlockSpec((t, 256), lambda cb: (0, cb)), pl.BlockSpec((32, 128), lambda cb: (0, cb)),
                   pl.BlockSpec((1, 128), lambda cb: (0, cb))],
        out_shape=[jax.ShapeDtypeStruct((t, 1024), BF16), jax.ShapeDtypeStruct((32, 512), F32),
                   jax.ShapeDtypeStruct((1, 512), F32)],
        scratch_shapes=[pltpu.VMEM((t + pad, 128), F32), pltpu.VMEM((t + pad, 128), F32), pltpu.VMEM((32, 128), F32)],
        compiler_params=_cparams(1),
    )(proj, w32, dub)


def sconv_fwd(name, proj, w8):
    t = proj.shape[0]
    pad = 8
    k1 = DN_CONV_K - 1

    def body(x_ref, w_ref, o_ref, xp_ref):
        xp_ref[0:pad, :] = jnp.zeros((pad, 128), F32)
        xp_ref[pad:pad + t, :] = x_ref[...]
        for r0, acc in _conv_taps(xp_ref, w_ref, DN_CONV_K, pad - k1, t):
            o_ref[pl.ds(r0, CONV_ROWS), :] = _silu(acc)

    return pl.pallas_call(
        body, name=name, grid=(12,),
        in_specs=[pl.BlockSpec((t, 128), lambda cb: (0, P_QKV // 128 + cb)), pl.BlockSpec((8, 128), lambda cb: (0, cb))],
        out_specs=pl.BlockSpec((t, 128), lambda cb: (0, cb)),
        out_shape=jax.ShapeDtypeStruct((t, 1536), F32),
        scratch_shapes=[pltpu.VMEM((t + pad, 128), F32)],
        compiler_params=_cparams(1),
    )(proj, w8)


def sconv_bwd(name, proj, w8, dqkv):
    t = proj.shape[0]
    pad = 8
    k1 = DN_CONV_K - 1

    def body(x_ref, w_ref, dy_ref, dx_ref, dw_ref, xp_ref, dpp_ref, wrev_ref):
        xp_ref[0:pad, :] = jnp.zeros((pad, 128), F32)
        xp_ref[pad:pad + t, :] = x_ref[...]
        for r0, pre in _conv_taps(xp_ref, w_ref, DN_CONV_K, pad - k1, t):
            s = _sigmoid(pre)
            dpp_ref[pl.ds(r0, CONV_ROWS), :] = dy_ref[pl.ds(r0, CONV_ROWS), :] * (s * (1.0 + pre * (1.0 - s)))
        dpp_ref[t:t + pad, :] = jnp.zeros((pad, 128), F32)
        for k in range(DN_CONV_K):
            wrev_ref[k:k + 1, :] = w_ref[k1 - k:k1 - k + 1, :]
        wrev_ref[DN_CONV_K:8, :] = jnp.zeros((8 - DN_CONV_K, 128), F32)
        for r0, dx in _conv_taps(dpp_ref, wrev_ref, DN_CONV_K, 0, t):
            dx_ref[pl.ds(r0, CONV_ROWS), :] = dx.astype(dx_ref.dtype)
        dws = _conv_wgrad(dpp_ref, xp_ref, DN_CONV_K, pad - k1, t)
        for k in range(DN_CONV_K):
            dw_ref[k:k + 1, :] = dws[k]
        dw_ref[DN_CONV_K:8, :] = jnp.zeros((8 - DN_CONV_K, 128), F32)

    return pl.pallas_call(
        body, name=name, grid=(12,),
        in_specs=[pl.BlockSpec((t, 128), lambda cb: (0, P_QKV // 128 + cb)), pl.BlockSpec((8, 128), lambda cb: (0, cb)),
                  pl.BlockSpec((t, 128), lambda cb: (0, cb))],
        out_specs=[pl.BlockSpec((t, 128), lambda cb: (0, cb)), pl.BlockSpec((8, 128), lambda cb: (0, cb))],
        out_shape=[jax.ShapeDtypeStruct((t, 1536), BF16), jax.ShapeDtypeStruct((8, 1536), F32)],
        scratch_shapes=[pltpu.VMEM((t + pad, 128), F32), pltpu.VMEM((t + pad, 128), F32), pltpu.VMEM((8, 128), F32)],
        compiler_params=_cparams(1),
    )(proj, w8, dqkv)


def _f_delta_step(qkv, ab, zc, s0, s1, s2, s3, a_log, dt_bias, dn_g, inverses=None, with_inverses=False):
    cs = DN_CHUNK
    n = 2 * cs
    states = (s0, s1, s2, s3)
    lane = lax.broadcasted_iota(jnp.int32, (1, 128), 1)
    ri = lax.broadcasted_iota(jnp.int32, (n, n), 0)
    ci = lax.broadcasted_iota(jnp.int32, (n, n), 1)
    same = (ri // cs) == (ci // cs)
    lower = same & (ri >= ci)
    strict = same & (ri > ci)
    sums = jnp.concatenate([jnp.where(lower, 1.0, 0.0), jnp.where(same, 1.0, 0.0), jnp.where(ci < cs, 1.0, 0.0),
                            jnp.where(ci >= cs, 1.0, 0.0)], axis=0)
    top = lax.broadcasted_iota(jnp.int32, (n, 1), 0) < cs

    def pick(row, idx):
        return jnp.sum(jnp.where(lane == idx, row, 0.0), axis=-1, keepdims=True)

    def l2n(x):
        return x * lax.rsqrt(jnp.sum(x * x, axis=-1, keepdims=True) + EPS)

    n_chunks = qkv.shape[0] // cs
    units = [(k, pair) for k in range(n_chunks) for pair in range(2)]

    pre = []
    for k, pair in units:
        hs = (2 * pair, 2 * pair + 1)
        rows = slice(k * cs, (k + 1) * cs)
        stack = lambda f: jnp.concatenate([f(hs[0]), f(hs[1])], axis=0)
        qd = l2n(stack(lambda h: qkv[rows, 128 * h:128 * h + 128])) * (128 ** -0.5)
        kd = l2n(stack(lambda h: qkv[rows, 512 + 128 * h:512 + 128 * h + 128]))
        vd = stack(lambda h: qkv[rows, 1024 + 128 * h:1024 + 128 * h + 128])
        beta = _sigmoid(stack(lambda h: pick(ab[rows], 4 + h)))
        g = stack(lambda h: -jnp.exp(pick(a_log, h)) * _softplus(pick(ab[rows], h) + pick(dt_bias, h)))
        g_sums = sel_mm(sums, g * jnp.ones((1, n), F32))
        gc_col = g_sums[0:n]
        gl_b = g_sums[n:2 * n]
        g_end = (g_sums[2 * n:3 * n], g_sums[3 * n:])
        decay = jnp.where(lower, jnp.exp(jnp.where(lower, gc_col - gc_col.T, 0.0)), 0.0)
        kb = kd * beta
        pre.append(dict(qd=qd, kd=kd, vb=vd * beta, kb=kb, gc_col=gc_col, gl_b=gl_b, g_end=g_end, decay=decay,
                        a=jnp.where(strict, mm_nt(kb, kd) * decay, 0.0)))
    if inverses is None:
        tmats = tri_inv(*[p["a"] for p in pre])
    else:
        tmats = [tri_inv_known(p["a"], t) for p, t in zip(pre, inverses)]

    mid = []
    for p, tmat in zip(pre, tmats):
        egc = jnp.exp(p["gc_col"])
        mid.append(dict(u=mm(tmat, p["vb"]), wm=mm(tmat, p["kb"] * egc), qe=p["qd"] * egc,
                        intra=jnp.where(lower, mm_nt(p["qd"], p["kd"]) * p["decay"], 0.0),
                        ke=p["kd"] * jnp.exp(p["gl_b"] - p["gc_col"]), g_end=p["g_end"]))

    ys = []
    for k in range(n_chunks):
        rows = slice(k * cs, (k + 1) * cs)
        new_states, y_heads = [], []
        for pair in range(2):
            m = mid[2 * k + pair]
            hs = (2 * pair, 2 * pair + 1)
            st = (states[hs[0]], states[hs[1]])
            v_new = m["u"] - jnp.concatenate([mm(m["wm"][:cs], st[0]), mm(m["wm"][cs:], st[1])], axis=0)
            o = jnp.concatenate([mm(m["qe"][:cs], st[0]), mm(m["qe"][cs:], st[1])], axis=0) + mm(m["intra"], v_new)
            new_states.append(st[0] * jnp.exp(m["g_end"][0]) + mm_tn(jnp.where(top, m["ke"], 0.0), v_new))
            new_states.append(st[1] * jnp.exp(m["g_end"][1]) + mm_tn(jnp.where(top, 0.0, m["ke"]), v_new))
            od = o * lax.rsqrt(jnp.mean(o * o, axis=-1, keepdims=True) + EPS) * dn_g
            y_heads += [od[:cs] * _silu(zc[rows, 128 * hs[0]:128 * hs[0] + 128]),
                        od[cs:] * _silu(zc[rows, 128 * hs[1]:128 * hs[1] + 128])]
        states = tuple(new_states)
        ys.append(jnp.concatenate(y_heads, axis=1))
    if with_inverses:
        return (jnp.concatenate(ys, axis=0), *states), tmats
    return (jnp.concatenate(ys, axis=0), *states)


DELTA_ROWS = 4 * DN_CHUNK
DELTA_UNITS = 2 * DELTA_ROWS // DN_CHUNK


def delta_fwd(name, qkv, proj, a_log, dt_bias, dn_g):
    t = qkv.shape[0]
    nc = t // DELTA_ROWS

    def body(qkv_ref, ab_ref, zc_ref, al_ref, dt_ref, g_ref, y_ref, ssave_ref, tsave_ref, s_ref):
        @pl.when(pl.program_id(0) == 0)
        def _():
            s_ref[...] = jnp.zeros_like(s_ref)

        ssave_ref[0] = s_ref[...]
        st = [s_ref[128 * h:128 * h + 128, :] for h in range(4)]
        (y, *ns), tmats = _f_delta_step(qkv_ref[...], ab_ref[...], zc_ref[...], *st, al_ref[...], dt_ref[...], g_ref[...],
                                        with_inverses=True)
        y_ref[...] = y
        for h in range(4):
            s_ref[128 * h:128 * h + 128, :] = ns[h]
        for u, tm in enumerate(tmats):
            tsave_ref[0, 128 * u:128 * u + 128, :] = tm

    return pl.pallas_call(
        body, name=name, grid=(nc,),
        in_specs=[pl.BlockSpec((DELTA_ROWS, 1536), lambda i: (i, 0)), pl.BlockSpec((DELTA_ROWS, 128), lambda i: (i, P_AB // 128)),
                  pl.BlockSpec((DELTA_ROWS, 512), lambda i: (i, P_ZC // 512)),
                  _const_spec((1, 128)), _const_spec((1, 128)), _const_spec((1, 128))],
        out_specs=[pl.BlockSpec((DELTA_ROWS, 512), lambda i: (i, 0)), pl.BlockSpec((1, 512, 128), lambda i: (i, 0, 0)),
                   pl.BlockSpec((1, DELTA_UNITS * 128, 128), lambda i: (i, 0, 0))],
        out_shape=[jax.ShapeDtypeStruct((t, 512), F32), jax.ShapeDtypeStruct((nc, 512, 128), F32),
                   jax.ShapeDtypeStruct((nc, DELTA_UNITS * 128, 128), F32)],
        scratch_shapes=[pltpu.VMEM((512, 128), F32)],
        compiler_params=_cparams(1),
    )(qkv, proj, proj, a_log, dt_bias, dn_g)


def delta_bwd(name, qkv, proj, ssave, tsave, a_log, dt_bias, dn_g, dyc, carry=None):
    t = qkv.shape[0]
    nc = t // DELTA_ROWS
    c_ins, c_in_specs, c_outs, c_out_specs, c_scratch = _host(carry)
    n_ci, n_co = len(c_ins), len(c_outs)

    def body(*refs):
        qkv_ref, ab_ref, zc_ref, ss_ref, ts_ref, al_ref, dt_ref, g_ref, dy_ref = refs[:9]
        dqkv_ref, dab_ref, dzc_ref, dal_ref, ddt_ref, dg_ref = refs[9 + n_ci:15 + n_ci]
        ds_ref = refs[15 + n_ci + n_co]
        carried = (refs[9:9 + n_ci], refs[15 + n_ci:15 + n_ci + n_co], *refs[16 + n_ci + n_co:])

        @pl.when(pl.program_id(0) == 0)
        def _():
            ds_ref[...] = jnp.zeros_like(ds_ref)
            dal_ref[...] = jnp.zeros_like(dal_ref)
            ddt_ref[...] = jnp.zeros_like(ddt_ref)
            dg_ref[...] = jnp.zeros_like(dg_ref)

        if carry is not None:
            carry.emit_start(pl.program_id(0) == 0, *carried)

        st = [ss_ref[0, 128 * h:128 * h + 128, :] for h in range(4)]
        known = [ts_ref[0, 128 * u:128 * u + 128, :] for u in range(DELTA_UNITS)]
        _, vjp = jax.vjp(functools.partial(_f_delta_step, inverses=known), qkv_ref[...], ab_ref[...], zc_ref[...], *st,
                         al_ref[...], dt_ref[...], g_ref[...])
        dst = tuple(ds_ref[128 * h:128 * h + 128, :] for h in range(4))
        dqkv, dab, dzc, d0, d1, d2, d3, dal, ddt, dg = vjp((dy_ref[...], *dst))
        dqkv_ref[...] = dqkv
        dab_ref[...] = dab.astype(dab_ref.dtype)
        dzc_ref[...] = dzc.astype(dzc_ref.dtype)
        for h, d in enumerate((d0, d1, d2, d3)):
            ds_ref[128 * h:128 * h + 128, :] = d
        dal_ref[...] += dal
        ddt_ref[...] += ddt
        dg_ref[...] += dg

        if carry is not None:
            carry.emit_finish(pl.program_id(0) == nc - 1, *carried)

    rev = lambda cb: (lambda j: (nc - 1 - j, cb))
    return pl.pallas_call(
        body, name=name, grid=(nc,),
        in_specs=[pl.BlockSpec((DELTA_ROWS, 1536), rev(0)), pl.BlockSpec((DELTA_ROWS, 128), rev(P_AB // 128)),
                  pl.BlockSpec((DELTA_ROWS, 512), rev(P_ZC // 512)), pl.BlockSpec((1, 512, 128), lambda j: (nc - 1 - j, 0, 0)),
                  pl.BlockSpec((1, DELTA_UNITS * 128, 128), lambda j: (nc - 1 - j, 0, 0)),
                  _const_spec((1, 128)), _const_spec((1, 128)), _const_spec((1, 128)),
                  pl.BlockSpec((DELTA_ROWS, 512), rev(0))] + c_in_specs,
        out_specs=[pl.BlockSpec((DELTA_ROWS, 1536), rev(0)), pl.BlockSpec((DELTA_ROWS, 128), rev(0)),
                   pl.BlockSpec((DELTA_ROWS, 512), rev(0)),
                   _const_spec((1, 128)), _const_spec((1, 128)), _const_spec((1, 128))] + c_out_specs,
        out_shape=[jax.ShapeDtypeStruct((t, 1536), F32), jax.ShapeDtypeStruct((t, 128), BF16),
                   jax.ShapeDtypeStruct((t, 512), BF16),
                   jax.ShapeDtypeStruct((1, 128), F32), jax.ShapeDtypeStruct((1, 128), F32), jax.ShapeDtypeStruct((1, 128), F32)]
        + c_outs,
        scratch_shapes=[pltpu.VMEM((512, 128), F32)] + c_scratch,
        compiler_params=_cparams(1),
    )(qkv, proj, proj, ssave, tsave, a_log, dt_bias, dn_g, dyc, *c_ins)


def loss_head(name, y, target, tm):
    t, d = y.shape

    def body(y_ref, t_ref, dy_ref, l_ref):
        err = y_ref[...] - t_ref[...]
        dy_ref[...] = err * (1.0 / d)
        part = 0.5 * jnp.sum(jnp.sum(err * err, axis=-1, keepdims=True) * (1.0 / d), axis=0, keepdims=True)

        @pl.when(pl.program_id(0) == 0)
        def _():
            l_ref[...] = part

        @pl.when(pl.program_id(0) > 0)
        def _():
            l_ref[...] += part

    return pl.pallas_call(
        body, name=name, grid=(t // tm,),
        in_specs=[_row_spec(tm, d, 0), _row_spec(tm, d, 0)],
        out_specs=[_row_spec(tm, d, 0), _const_spec((1, 1))],
        out_shape=[jax.ShapeDtypeStruct((t, d), F32), jax.ShapeDtypeStruct((1, 1), F32)],
        compiler_params=_cparams(1),
    )(y, target)


TM = 512
TM_MERGE = 256
TM_IN = 1024
TN_IN = 1152


def _lane_pad(v, n=128):
    return jnp.pad(v.astype(F32), (0, n - v.shape[0]))[None, :]


def f_norm_mod_res(x, g, scale, shift):
    return f_norm_mod(x, g, scale, shift), x


def prep_layer(w):
    p = dict(w)
    p["wp"] = _w_in_assemble(w["w_in"])
    p["wpa"] = _perm_heads_rows(w["w_proj_a"])
    p["dw32"] = jnp.pad(w["dw_w"], ((0, 32 - CONV_K), (0, 0)))
    p["sconv8"] = jnp.pad(w["sconv_w"], ((0, 8 - DN_CONV_K), (0, 0)))
    p["qg"] = jnp.tile(w["q_norm_g"], 2)[None, :]
    p["kg"] = jnp.tile(w["k_norm_g"], 2)[None, :]
    p["sinks128"] = _lane_pad(w["sinks"])
    p["al"] = _lane_pad(w["a_log"])
    p["dtb"] = _lane_pad(w["dt_bias"])
    p["dng"] = w["dn_norm_g"][None, :]
    return p


def layer_fwd(tag, x, mod, p, carry=None):
    d = D_MODEL
    shift, scale, gate = mod[:, :d], mod[:, d:2 * d], mod[:, 2 * d:]
    g = p["norm_g"][None, :]
    (h,) = rowwise_fwd(f"norm_fwd{tag}", f_norm_mod, [(x, d, 0)], [g, scale, shift], [(d, BF16)], TM)
    proj = matmul_nn(f"inproj_fwd{tag}", h, p["wp"], F32, TM_IN, TN_IN, d, carry=carry)
    proj, carried = (proj, []) if carry is None else (proj[0], proj[1:])
    ya = attn_fwd(f"attn_fwd{tag}", proj, p["qg"], p["kg"], p["sinks128"])
    ub = glu_conv_fwd(f"glu_conv_fwd{tag}", proj, p["dw32"], p["dw_b"][None, :])
    conf_consts = [p["ln_g"][None, :], p["ln_b"][None, :], p["pw2_w"], p["pw2_b"][None, :]]
    (yb,) = rowwise_fwd(f"conf_fwd{tag}", f_conf_tail, [(ub, 512, 0), (proj, 512, P_ZB // 512)], conf_consts, [(512, F32)], TM)
    qkv = sconv_fwd(f"sconv_fwd{tag}", proj, p["sconv8"])
    yc, ssave, tsave = delta_fwd(f"delta_fwd{tag}", qkv, proj, p["al"], p["dtb"], p["dng"])
    merge_consts = [gate, p["wpa"], p["w_proj_b"], p["w_proj_c"], p["w_out"]]
    merge_rows = [(ya, 512, 0), (yb, 512, 0), (yc, 512, 0), (proj, 3 * d, P_MG // (3 * d)), (x, d, 0)]
    (xn,) = rowwise_fwd(f"merge_fwd{tag}", f_merge, merge_rows, merge_consts, [(d, F32)], TM_MERGE)
    saved = dict(x=x, h=h, proj=proj, ub=ub, qkv=qkv, ssave=ssave, tsave=tsave, norm_consts=[g, scale, shift],
                 conf_consts=conf_consts, merge_consts=merge_consts, merge_rows=merge_rows)
    return xn, saved, carried


def layer_bwd(tag, dxn, p, s, carry_merge=None, carry_delta=None):
    d = D_MODEL
    proj = s["proj"]
    dya, dyb, dyc, dmg, dgate, dwpa, dwpb, dwpc, dwout, *got_merge = rowwise_bwd(
        f"merge_bwd{tag}", f_merge, s["merge_rows"], s["merge_consts"], [(dxn, d, 0)], [F32, F32, F32, BF16, None], TM_MERGE,
        carry=carry_merge)
    carry_delta = None if carry_delta is None else carry_delta(got_merge)
    dqz, dkv, dqg, dkg, dsinks = attn_bwd(f"attn_bwd{tag}", proj, p["qg"], p["kg"], p["sinks128"], dya)
    dub, dzb, dln_g, dln_b, dpw2_w, dpw2_b = rowwise_bwd(
        f"conf_bwd{tag}", f_conf_tail, [(s["ub"], 512, 0), (proj, 512, P_ZB // 512)], s["conf_consts"], [(dyb, 512, 0)],
        [F32, BF16], TM)
    dglu, ddw32, ddw_b = glu_conv_bwd(f"glu_conv_bwd{tag}", proj, p["dw32"], dub)
    dqkv, dab, dzc, dal, ddtb, ddng, *got_delta = delta_bwd(f"delta_bwd{tag}", s["qkv"], proj, s["ssave"], s["tsave"], p["al"],
                                                            p["dtb"], p["dng"], dyc, carry_delta)
    dqkv_pre, dsconv8 = sconv_bwd(f"sconv_bwd{tag}", proj, p["sconv8"], dqkv)
    dproj = jnp.concatenate([dqz, dglu, dzb, dzc, dmg, dqkv_pre, dkv, dab], axis=1)
    dh = matmul_nn(f"inproj_bwd_dh{tag}", dproj, p["wp"], F32, TM_IN, d, P_TOTAL // 3, b_transposed=True)
    dwp = matmul_nn(f"inproj_bwd_dw{tag}", s["h"].T, dproj, F32, d, TN_IN, 2048)
    dx, dnorm_g, dscale, dshift = rowwise_bwd(
        f"norm_bwd{tag}", f_norm_mod_res, [(s["x"], d, 0)], s["norm_consts"], [(dh, d, 0), (dxn, d, 0)], [F32], TM)
    dmod = jnp.concatenate([dshift, dscale, dgate], axis=1)
    grads = dict(
        b_ada=dmod[0], norm_g=dnorm_g[0], w_in=_w_in_grad_blocks(dwp),
        q_norm_g=dqg[0, :64] + dqg[0, 64:], k_norm_g=dkg[0, :64] + dkg[0, 64:], sinks=dsinks[0, :ATT_HEADS],
        dw_w=ddw32[:CONV_K], dw_b=ddw_b[0], ln_g=dln_g[0], ln_b=dln_b[0], pw2_w=dpw2_w, pw2_b=dpw2_b[0],
        sconv_w=dsconv8[:DN_CONV_K], a_log=dal[0, :DN_HEADS], dt_bias=ddtb[0, :DN_HEADS], dn_norm_g=ddng[0],
        w_proj_a=_unperm_heads_rows(dwpa), w_proj_b=dwpb, w_proj_c=dwpc, w_out=dwout)
    return dx, grads, got_merge, got_delta


SHARDED = {"w_ada": 2, "w_in": 2, "dw_w": 2, "pw2_w": 1, "sconv_w": 2, "w_proj_a": 2, "w_proj_b": 2, "w_proj_c": 2,
           "w_out": 1}
GATHERED = tuple(n for n in SHARDED if n != "w_ada")
GATHER_F32 = ("dw_w", "sconv_w")
REDUCE_BIG = tuple(n for n in GATHERED if n not in GATHER_F32)
SMALL = ("b_ada", "norm_g", "q_norm_g", "k_norm_g", "sinks", "dw_b", "ln_g", "ln_b", "pw2_b", "a_log", "dt_bias",
         "dn_norm_g")
SMALL_ROWS = 104
SMALL_GRAD_ROWS = 448
W_IN_SHARD = D_IN // N_CHIPS
SUM_TILE = 256


def _w_in_orig():
    orig = np.full(P_TOTAL, -1, np.int64)
    p = 0
    for s, n in _in_pieces():
        orig[p:p + n] = np.arange(s, s + n)
        p += n
    return orig


def _w_in_blocks(k):
    orig = _w_in_orig().reshape(-1, 128)
    lo, hi = k * W_IN_SHARD, (k + 1) * W_IN_SHARD
    return [b for b in range(orig.shape[0]) if np.any((orig[b] >= lo) & (orig[b] < hi))]


W_IN_BLOCKS = max(len(_w_in_blocks(k)) for k in range(N_CHIPS))


def _runs(idx):
    out, i = [], 0
    while i < len(idx):
        j = i + 1
        while j < len(idx) and ((idx[i] < 0 and idx[j] < 0) or (idx[i] >= 0 and idx[j] == idx[j - 1] + 1)):
            j += 1
        out.append((int(idx[i]) if idx[i] >= 0 else -1, j - i))
        i = j
    return out


def _take_cols(a, idx):
    parts = [jnp.zeros(a.shape[:-1] + (n,), a.dtype) if s < 0 else a[..., s:s + n] for s, n in _runs(idx)]
    return parts[0] if len(parts) == 1 else jnp.concatenate(parts, axis=-1)


def _w_in_send(k, shard):
    orig = _w_in_orig().reshape(-1, 128)
    lo, hi = k * W_IN_SHARD, (k + 1) * W_IN_SHARD
    idx = np.concatenate([np.where((orig[b] >= lo) & (orig[b] < hi), orig[b] - lo, -1) for b in _w_in_blocks(k)])
    idx = np.concatenate([idx, np.full((W_IN_BLOCKS - len(_w_in_blocks(k))) * 128, -1)])
    return _take_cols(shard, idx)


def _w_in_assemble(blocks):
    where = [{b: i for i, b in enumerate(_w_in_blocks(k))} for k in range(N_CHIPS)]
    n_blocks = P_TOTAL // 128
    owners = [[(k, where[k][b]) for k in range(N_CHIPS) if b in where[k]] for b in range(n_blocks)]
    parts, b = [], 0
    while b < n_blocks:
        if len(owners[b]) == 1:
            k, pos = owners[b][0]
            e = b + 1
            while e < n_blocks and owners[e] == [(k, pos + e - b)]:
                e += 1
            parts.append(blocks[k][:, pos * 128:(pos + e - b) * 128])
            b = e
        else:
            parts.append(functools.reduce(jnp.add, [blocks[k][:, pos * 128:(pos + 1) * 128] for k, pos in owners[b]]))
            b += 1
    return jnp.concatenate(parts, axis=1)


def _w_in_grad_blocks(wp):
    out = []
    for k in range(N_CHIPS):
        idx = np.concatenate([np.arange(128 * b, 128 * b + 128) for b in _w_in_blocks(k)])
        idx = np.concatenate([idx, np.full((W_IN_BLOCKS - len(_w_in_blocks(k))) * 128, -1)])
        out.append(_take_cols(wp, idx))
    return jnp.stack(out)


def _w_in_receive_grad(k, blocks):
    orig = _w_in_orig()
    inv = np.zeros(D_IN, np.int64)
    inv[orig[orig >= 0]] = np.nonzero(orig >= 0)[0]
    where = {b: i for i, b in enumerate(_w_in_blocks(k))}
    cols = inv[k * W_IN_SHARD:(k + 1) * W_IN_SHARD]
    return _take_cols(blocks, np.array([where[c // 128] * 128 + c % 128 for c in cols]))


def _join_layer(v, axis):
    if axis == 2:
        return jnp.transpose(v, (1, 0, 2)).reshape(v.shape[1], N_CHIPS * v.shape[2])
    return v.reshape(N_CHIPS * v.shape[1], v.shape[2])


def _split_layer(v, axis):
    a, b = v.shape
    if axis == 2:
        return jnp.transpose(v.reshape(a, N_CHIPS, b // N_CHIPS), (1, 0, 2))
    return v.reshape(N_CHIPS, a // N_CHIPS, b)


def pack_small(vals, names, rows):
    flat = jnp.concatenate([vals[n].astype(F32).reshape(-1) for n in names])
    return jnp.pad(flat, (0, rows * 128 - flat.shape[0])).reshape(rows, 128)


def unpack_small(packed, names, shapes):
    flat = packed.reshape(-1)
    out, off = {}, 0
    for n in names:
        k = int(np.prod(shapes[n]))
        out[n] = flat[off:off + k].reshape(shapes[n])
        off += k
    return out


ANY = pl.BlockSpec(memory_space=pl.ANY)


def _place():
    x, y, c = lax.axis_index("x"), lax.axis_index("y"), lax.axis_index("c")
    chips = [(1 - x, y), (x, 1 - y), (1 - x, 1 - y)]
    return x, y, c, chips


def _remote(src, dst, send_sem, recv_sem, to):
    return pltpu.make_async_remote_copy(src_ref=src, dst_ref=dst, send_sem=send_sem, recv_sem=recv_sem, device_id=to,
                                        device_id_type=MESH)


class Carry:
    def __init__(self, ins, out_shapes, sems, start, finish, in_place=False):
        self.ins, self.out_shapes, self.sems, self.start, self.finish, self.in_place = (
            list(ins), list(out_shapes), sems, start, finish, in_place)

    def scratch(self):
        return [pltpu.SemaphoreType.DMA(self.sems), pltpu.SemaphoreType.DMA(self.sems)]

    def aliases(self, first_in, first_out):
        return {first_in + i: first_out + i for i in range(len(self.ins))} if self.in_place else {}

    def emit_start(self, first, in_refs, out_refs, send_sems, recv_sems):
        @pl.when(first)
        def _():
            self.start(in_refs, out_refs, send_sems, recv_sems)

    def emit_finish(self, last, in_refs, out_refs, send_sems, recv_sems):
        @pl.when(last)
        def _():
            self.finish(in_refs, out_refs, send_sems, recv_sems)


def _host(carry):
    if carry is None:
        return [], [], [], [], []
    return carry.ins, [ANY] * len(carry.ins), carry.out_shapes, [ANY] * len(carry.out_shapes), carry.scratch()


def run_carry(name, carry):
    n_in, n_out = len(carry.ins), len(carry.out_shapes)

    def body(*refs):
        ins, outs, sems = refs[:n_in], refs[n_in:n_in + n_out], refs[n_in + n_out:]
        carry.start(ins, outs, *sems)
        carry.finish(ins, outs, *sems)

    return pl.pallas_call(
        body, name=name, out_shape=carry.out_shapes, in_specs=[ANY] * n_in, out_specs=[ANY] * n_out,
        input_output_aliases=carry.aliases(0, 0), scratch_shapes=carry.scratch(),
    )(*carry.ins)


def carry_allgather(layer, slots):
    n = len(slots)

    def copies(out, send_sems, recv_sems, only_ici_out=False):
        x, y, c, chips = _place()
        ici_out, ici_in, d2d_out, d2d_in = [], [], [], []
        for j, chip in enumerate(chips):
            for t in range(n):
                mine, land = out[t].at[2 * x + y], out[t].at[2 * chip[0] + chip[1]]
                ici_out.append(_remote(mine, mine, send_sems.at[t, j], recv_sems.at[t, j], (*chip, layer)))
                if only_ici_out:
                    continue
                ici_in.append(_remote(land, land, send_sems.at[t, j], recv_sems.at[t, j], (*chip, layer)))
                d2d_out.append(_remote(land, land, send_sems.at[t, 3 + j], recv_sems.at[t, 3 + j], (x, y, 1 - layer)))
                d2d_in.append(_remote(land, land, send_sems.at[t, 3 + j], recv_sems.at[t, 3 + j], (x, y, layer)))
        return c, ici_out, ici_in, d2d_out, d2d_in

    def start(ins, out, send_sems, recv_sems):
        c, ici_out, _, _, _ = copies(out, send_sems, recv_sems, only_ici_out=True)

        @pl.when(c == layer)
        def _():
            for cp in ici_out:
                cp.start()

    def finish(ins, out, send_sems, recv_sems):
        c, ici_out, ici_in, d2d_out, d2d_in = copies(out, send_sems, recv_sems)

        @pl.when(c == layer)
        def _():
            for arrived, onward in zip(ici_in, d2d_out):
                arrived.wait_recv()
                onward.start()
            for cp in ici_out + d2d_out:
                cp.wait_send()

        @pl.when(c != layer)
        def _():
            for cp in d2d_in:
                cp.wait_recv()

    return Carry(slots, [jax.ShapeDtypeStruct(s.shape, s.dtype) for s in slots], (n, 6), start, finish, in_place=True)


def carry_pair_send(layer, gs):
    def copies(g, recv, send_sems, recv_sems):
        x, y, c, _ = _place()
        return c, [_remote(g[t], recv[t], send_sems.at[t], recv_sems.at[t], (x, y, 1 - c)) for t in range(len(gs))]

    def start(g, recv, send_sems, recv_sems):
        c, cps = copies(g, recv, send_sems, recv_sems)

        @pl.when(c != layer)
        def _():
            for cp in cps:
                cp.start()

    def finish(g, recv, send_sems, recv_sems):
        c, cps = copies(g, recv, send_sems, recv_sems)

        @pl.when(c != layer)
        def _():
            for cp in cps:
                cp.wait_send()

        @pl.when(c == layer)
        def _():
            for cp in cps:
                cp.wait_recv()

    return Carry(gs, [jax.ShapeDtypeStruct(g.shape, g.dtype) for g in gs], (len(gs),), start, finish)


def grads_pair_sum(name, g, recv):
    _, a, b = recv.shape
    ta = min(a, SUM_TILE)

    def body(a_ref, b_ref, o_ref):
        o_ref[...] = (a_ref[...] + b_ref[...]).astype(o_ref.dtype)

    spec = pl.BlockSpec((None, ta, b), lambda s, i: (s, i, 0))
    return pl.pallas_call(
        body, name=name, grid=(N_CHIPS, a // ta), in_specs=[spec, spec], out_specs=spec,
        out_shape=jax.ShapeDtypeStruct(recv.shape, BF16), compiler_params=_cparams(2),
    )(g, recv)


def carry_chip_exchange(layer, ps):
    def copies(p, recv, send_sems, recv_sems):
        _, _, c, chips = _place()
        return c, [_remote(p[t].at[2 * chip[0] + chip[1]], recv[t].at[j], send_sems.at[t, j], recv_sems.at[t, j],
                           (*chip, layer)) for j, chip in enumerate(chips) for t in range(len(ps))]

    def start(p, recv, send_sems, recv_sems):
        c, cps = copies(p, recv, send_sems, recv_sems)

        @pl.when(c == layer)
        def _():
            for cp in cps:
                cp.start()

    def finish(p, recv, send_sems, recv_sems):
        c, cps = copies(p, recv, send_sems, recv_sems)

        @pl.when(c == layer)
        def _():
            for cp in cps:
                cp.wait()

    return Carry(ps, [jax.ShapeDtypeStruct((3,) + p.shape[1:], p.dtype) for p in ps], (len(ps), 3), start, finish)


def grads_chip_sum(name, layer, g, recv, recv2, into=None):
    _, a, b = recv.shape
    ta = min(a, SUM_TILE)
    my_slot = lambda: 2 * lax.axis_index("x") + lax.axis_index("y")

    def body(g_ref, r_ref, r2_ref, *rest):
        o_ref = rest[-1]
        own = g_ref[...] + r_ref[...]
        o_ref[...] = ((own + r2_ref[0].astype(F32)) + r2_ref[1].astype(F32)) + r2_ref[2].astype(F32)

    own_spec = pl.BlockSpec((None, ta, b), lambda i: (my_slot(), i, 0))
    return pl.pallas_call(
        body, name=name, grid=(a // ta,),
        in_specs=[own_spec, own_spec, pl.BlockSpec((3, ta, b), lambda i: (0, i, 0))] + ([] if into is None else [ANY]),
        out_specs=pl.BlockSpec((None, ta, b), lambda i: (layer, i, 0)),
        out_shape=jax.ShapeDtypeStruct((DEPTH, a, b), F32),
        input_output_aliases={} if into is None else {3: 0},
        compiler_params=_cparams(1),
    )(g, recv, recv2, *([] if into is None else [into]))


def grads_pair_gather(reds):
    n = len(reds)

    def body(*refs):
        buf = refs[n:2 * n]
        send_sems, recv_sems = refs[2 * n:]
        x, y, c, _ = _place()
        sibling = (x, y, 1 - c)
        cps = [_remote(buf[t].at[c], buf[t].at[c], send_sems.at[t], recv_sems.at[t], sibling) for t in range(n)]
        for cp in cps:
            cp.start()
        for t in range(n):
            _remote(buf[t].at[c], buf[t].at[1 - c], send_sems.at[t], recv_sems.at[t], sibling).wait_recv()
        for cp in cps:
            cp.wait_send()

    return pl.pallas_call(
        body, name="grads_pair_gather", out_shape=[jax.ShapeDtypeStruct(r.shape, r.dtype) for r in reds],
        in_specs=[ANY] * n, out_specs=[ANY] * n, input_output_aliases={t: t for t in range(n)},
        scratch_shapes=[pltpu.SemaphoreType.DMA((n,)), pltpu.SemaphoreType.DMA((n,))],
    )(*reds)


def small_allreduce(v):
    m, n = v.shape

    def body(x_ref, sum_ref, all_ref, send_sems, recv_sems, local_sem):
        x, y, c, chips = _place()
        me, sibling = (x, y, c), (x, y, 1 - c)

        def rows(px, py, pc):
            return all_ref.at[pl.ds((4 * px + 2 * py + pc) * m, m), :]

        def copy(k, block, to, src=None):
            return pltpu.make_async_remote_copy(src_ref=rows(*block) if src is None else src, dst_ref=rows(*block),
                                                send_sem=send_sems.at[k], recv_sem=recv_sems.at[k],
                                                device_id=to, device_id_type=MESH)

        mine = pltpu.make_async_copy(x_ref, rows(*me), local_sem)
        mine.start()
        first = [copy(0, me, sibling, src=x_ref)]
        first += [copy(1 + j, me, (*chip, c), src=x_ref) for j, chip in enumerate(chips)]
        for cp in first:
            cp.start()
        passed = [copy(4 + j, (*chip, c), sibling) for j, chip in enumerate(chips)]
        for j, chip in enumerate(chips):
            copy(1 + j, (*chip, c), me).wait_recv()
            passed[j].start()
        copy(0, sibling, me).wait_recv()
        for j, chip in enumerate(chips):
            copy(4 + j, (*chip, 1 - c), me).wait_recv()
        for cp in first + passed:
            cp.wait_send()
        mine.wait()
        acc = all_ref[0:m, :]
        for dev in range(1, 8):
            acc = acc + all_ref[dev * m:(dev + 1) * m, :]
        sum_ref[...] = acc

    vm = pl.BlockSpec(memory_space=pltpu.VMEM)
    return pl.pallas_call(
        body, name="small_allreduce",
        out_shape=[jax.ShapeDtypeStruct((m, n), F32), jax.ShapeDtypeStruct((8 * m, n), F32)],
        in_specs=[vm], out_specs=[vm, vm],
        scratch_shapes=[pltpu.SemaphoreType.DMA((7,)), pltpu.SemaphoreType.DMA((7,)), pltpu.SemaphoreType.DMA],
    )(v)


def grads_by_chip(layer_grads):
    return [layer_grads[n] if n == "w_in" else _split_layer(layer_grads[n], SHARDED[n]) for n in REDUCE_BIG]


def grads_pair_sums(layer, gs, recv):
    return [grads_pair_sum(f"grads_pair_sum{layer}_{n}", g, r) for n, g, r in zip(REDUCE_BIG, gs, recv)]


def adamw(name, w, g, m, v, block):
    grid = tuple(s // b for s, b in zip(w.shape, block))

    def body(w_ref, g_ref, m_ref, v_ref, d_ref, nm_ref, nv_ref):
        gv = g_ref[...]
        nm = ADAM_B1 * m_ref[...] + (1.0 - ADAM_B1) * gv
        nv = ADAM_B2 * v_ref[...] + (1.0 - ADAM_B2) * (gv * gv)
        m_hat = nm / (1.0 - ADAM_B1 ** ADAM_STEP)
        v_hat = nv / (1.0 - ADAM_B2 ** ADAM_STEP)
        d_ref[...] = -ADAM_LR * (m_hat / (jnp.sqrt(v_hat) + ADAM_EPS) + ADAM_WD * w_ref[...])
        nm_ref[...] = nm
        nv_ref[...] = nv

    spec = pl.BlockSpec(tuple(block), lambda *idx: idx)
    return pl.pallas_call(
        body, name=name, grid=grid, in_specs=[spec] * 4, out_specs=[spec] * 3,
        out_shape=[jax.ShapeDtypeStruct(w.shape, F32)] * 3, compiler_params=_cparams(len(grid)),
    )(w, g, m, v)


ADAM_ROWS = {"w_ada": 512, "dw_w": 62, "pw2_w": 256, "sconv_w": 8, "w_proj_a": 512, "w_proj_b": 512, "w_proj_c": 512,
             "w_out": 256}
ADAM_W_IN_COLS = 331

WEIGHT_NAMES = ("w_ada", "b_ada", "norm_g", "w_in", "q_norm_g", "k_norm_g", "sinks", "dw_w", "dw_b", "ln_g", "ln_b",
                "pw2_w", "pw2_b", "sconv_w", "a_log", "dt_bias", "dn_norm_g", "w_proj_a", "w_proj_b", "w_proj_c", "w_out")


def kernel(x, c, w_ada, b_ada, norm_g, w_in, q_norm_g, k_norm_g, sinks, dw_w, dw_b, ln_g, ln_b, pw2_w, pw2_b, sconv_w, a_log, dt_bias, dn_norm_g, w_proj_a, w_proj_b, w_proj_c, w_out, loss_target, m_w_ada, m_b_ada, m_norm_g, m_w_in, m_q_norm_g, m_k_norm_g, m_sinks, m_dw_w, m_dw_b, m_ln_g, m_ln_b, m_pw2_w, m_pw2_b, m_sconv_w, m_a_log, m_dt_bias, m_dn_norm_g, m_w_proj_a, m_w_proj_b, m_w_proj_c, m_w_out, v_w_ada, v_b_ada, v_norm_g, v_w_in, v_q_norm_g, v_k_norm_g, v_sinks, v_dw_w, v_dw_b, v_ln_g, v_ln_b, v_pw2_w, v_pw2_b, v_sconv_w, v_a_log, v_dt_bias, v_dn_norm_g, v_w_proj_a, v_w_proj_b, v_w_proj_c, v_w_out):
    args = dict(locals())
    w = {n: args[n] for n in WEIGHT_NAMES}
    mom = {n: args["m_" + n] for n in WEIGHT_NAMES}
    var = {n: args["v_" + n] for n in WEIGHT_NAMES}

    chip = 2 * lax.axis_index("x") + lax.axis_index("y")
    own = {n: w[n] if n in GATHER_F32 else w[n].astype(BF16) for n in GATHERED}
    own["w_in"] = lax.switch(chip, [functools.partial(_w_in_send, k) for k in range(N_CHIPS)], own["w_in"])
    slots = [[lax.dynamic_update_slice(lax.empty((N_CHIPS,) + own[n].shape[1:], own[n].dtype), own[n][l][None], (chip, 0, 0))
              for n in GATHERED] for l in range(DEPTH)]

    def layer_operands(l, gathered):
        lw = {n: w[n][l] for n in SMALL}
        lw.update({n: g if n == "w_in" else _join_layer(g, SHARDED[n]) for n, g in zip(GATHERED, gathered)})
        return prep_layer(lw)

    layers = [layer_operands(0, run_carry("weights_allgather0", carry_allgather(0, slots[0]))), None]

    mod, conds = ada_fwd(jnp.tile(c, (8, 1)), w["w_ada"], w["b_ada"])
    saved = [None] * DEPTH
    act, saved[0], gathered1 = layer_fwd("0", x[0], mod[0:1], layers[0], carry=carry_allgather(1, slots[1]))
    layers[1] = layer_operands(1, gathered1)
    act, saved[1], _ = layer_fwd("1", act, mod[1:2], layers[1])
    dact, loss_part = loss_head("loss_head", act, loss_target[0], TM)
    loss = lax.psum(loss_part[0, 0], ("x", "y", "c"))
    layer_grads = [None] * DEPTH
    dact, layer_grads[1], _, _ = layer_bwd("1", dact, layers[1], saved[1])
    gs1 = grads_by_chip(layer_grads[1])
    dact, layer_grads[0], recv1, got1 = layer_bwd(
        "0", dact, layers[0], saved[0], carry_merge=carry_pair_send(1, gs1),
        carry_delta=lambda recv: carry_chip_exchange(1, grads_pair_sums(1, gs1, recv)))

    gs0 = grads_by_chip(layer_grads[0])
    recv0 = run_carry("grads_pair_send0", carry_pair_send(0, gs0))
    got0 = run_carry("grads_chip_exchange0", carry_chip_exchange(0, grads_pair_sums(0, gs0, recv0)))
    reds = [grads_chip_sum(f"grads_chip_sum1_{n}", 1, g, r, r2) for n, g, r, r2 in zip(REDUCE_BIG, gs1, recv1, got1)]
    reds = [grads_chip_sum(f"grads_chip_sum0_{n}", 0, g, r, r2, into=red)
            for n, g, r, r2, red in zip(REDUCE_BIG, gs0, recv0, got0, reds)]
    final_grads = dict(zip(REDUCE_BIG, grads_pair_gather(reds)))
    final_grads["w_in"] = lax.switch(chip, [functools.partial(_w_in_receive_grad, k) for k in range(N_CHIPS)],
                                     final_grads["w_in"])
    small_names = SMALL + GATHER_F32
    small_shapes = {n: (DEPTH,) + layer_grads[0][n].shape for n in small_names}
    small_full = {n: jnp.stack([layer_grads[l][n] for l in range(DEPTH)]) for n in small_names}
    small_sum, small_all = small_allreduce(pack_small(small_full, small_names, SMALL_GRAD_ROWS))
    small_sum = unpack_small(small_sum, small_names, small_shapes)
    for n in GATHER_F32:
        width = w[n].shape[2]
        final_grads[n] = lax.dynamic_slice_in_dim(small_sum[n], chip * width, width, axis=2)
    n_mod = DEPTH * 3 * D_MODEL
    dmod = small_all.reshape(8, -1)[:, :n_mod].reshape(8, DEPTH, 3 * D_MODEL)
    width = w["w_ada"].shape[2]
    dmod = jnp.transpose(lax.dynamic_slice_in_dim(dmod, chip * width, width, axis=2), (1, 0, 2))
    final_grads["w_ada"] = ada_bwd(conds, dmod)
    final_grads.update({n: small_sum[n] for n in SMALL})
    small_grads = pack_small(final_grads, SMALL, SMALL_ROWS)

    delta, new_m, new_v = {}, {}, {}
    for n in SHARDED:
        shp = w[n].shape
        if n == "w_in":
            view = lambda a: jnp.transpose(a, (2, 0, 1))
            back = lambda a: jnp.transpose(a, (1, 2, 0))
            g3 = view(final_grads[n])
            final_grads[n] = back(g3)
            d, nm, nv = adamw("adamw_" + n, view(w[n]), g3, view(mom[n]), view(var[n]), (ADAM_W_IN_COLS, shp[0], shp[1]))
        else:
            view = lambda a, shp=shp: a.reshape(shp[0] * shp[1], shp[2])
            back = lambda a, shp=shp: a.reshape(shp)
            d, nm, nv = adamw("adamw_" + n, view(w[n]), view(final_grads[n]), view(mom[n]), view(var[n]),
                              (ADAM_ROWS[n], shp[2]))
        delta[n], new_m[n], new_v[n] = back(d), back(nm), back(nv)
    d, nm, nv = adamw("adamw_small", pack_small(w, SMALL, SMALL_ROWS), small_grads, pack_small(mom, SMALL, SMALL_ROWS),
                      pack_small(var, SMALL, SMALL_ROWS), (SMALL_ROWS, 128))
    delta.update(unpack_small(d, SMALL, small_shapes))
    new_m.update(unpack_small(nm, SMALL, small_shapes))
    new_v.update(unpack_small(nv, SMALL, small_shapes))

    return (loss, dact[None], *[final_grads[n] for n in WEIGHT_NAMES], *[delta[n] for n in WEIGHT_NAMES],
            *[new_m[n] for n in WEIGHT_NAMES], *[new_v[n] for n in WEIGHT_NAMES])
```

```python
import functools

import numpy as np
import jax
import jax.numpy as jnp
from jax import lax
from jax.experimental import pallas as pl
from jax.experimental.pallas import tpu as pltpu

F32 = jnp.float32
BF16 = jnp.bfloat16
MESH = pl.DeviceIdType.MESH

D_MODEL = 1024
DEPTH = 2
ATT_HEADS = 8
ATT_HEAD_DIM = 64
WINDOW = 128
CONV_K = 31
DN_HEADS = 4
DN_CONV_K = 4
DN_CHUNK = 64
EPS = 1e-6
NEG_INF = -1e30
N_CHIPS = 4
D_IN = 7944

ADAM_LR = 0.001
ADAM_B1 = 0.9
ADAM_B2 = 0.999
ADAM_EPS = 1e-08
ADAM_WD = 0.01
ADAM_STEP = 10

VMEM_LIMIT = 56 * 1024 * 1024

P_QA, P_ZA, P_GLU, P_ZB, P_ZC, P_MG, P_QKV, P_KA, P_VA, P_AB, P_TOTAL = (
    0, 512, 1024, 2048, 2560, 3072, 6144, 7680, 7808, 7936, 8064)
HEAD_ORDER = (0, 4, 1, 5, 2, 6, 3, 7)


def _in_pieces():
    p = [(0 + 64 * h, 64) for h in HEAD_ORDER]
    p += [(768 + 64 * h, 64) for h in HEAD_ORDER]
    for g in range(4):
        p += [(1280 + 128 * g, 128), (1792 + 128 * g, 128)]
    p += [(2304, 512), (4360, 512), (4872, 3072), (2816, 1536), (512, 128), (640, 128), (4352, 8)]
    return p


def _perm_heads_rows(w):
    return jnp.concatenate([w[64 * h:64 * h + 64] for h in HEAD_ORDER], axis=0)


def _unperm_heads_rows(w):
    inv = [HEAD_ORDER.index(h) for h in range(8)]
    return jnp.concatenate([w[64 * s:64 * s + 64] for s in inv], axis=0)


def _split_bf16(a, terms):
    out, rest = [], a.astype(F32)
    for _ in range(terms - 1):
        out.append(rest.astype(BF16))
        rest = rest - out[-1].astype(F32)
    return out + [rest.astype(BF16)]


def _dot(a, b, dims, exact):
    d = lambda p, q: lax.dot_general(p, q, (dims, ((), ())), preferred_element_type=F32)
    if exact:
        (ah, al), (bh, bl) = _split_bf16(a, 2), _split_bf16(b, 2)
        return d(ah, bh) + (d(ah, bl) + d(al, bh))
    return d(a.astype(BF16), b.astype(BF16))


def _make_mm(exact):
    @jax.custom_vjp
    def nn(a, b):
        return _dot(a, b, ((1,), (0,)), exact)

    @jax.custom_vjp
    def nt(a, b):
        return _dot(a, b, ((1,), (1,)), exact)

    @jax.custom_vjp
    def tn(a, b):
        return _dot(a, b, ((0,), (0,)), exact)

    nn.defvjp(lambda a, b: (nn(a, b), (a, b)),
              lambda r, g: (nt(g, r[1]).astype(r[0].dtype), tn(r[0], g).astype(r[1].dtype)))
    nt.defvjp(lambda a, b: (nt(a, b), (a, b)),
              lambda r, g: (nn(g, r[1]).astype(r[0].dtype), tn(g, r[0]).astype(r[1].dtype)))
    tn.defvjp(lambda a, b: (tn(a, b), (a, b)),
              lambda r, g: (nt(r[1], g).astype(r[0].dtype), nn(r[0], g).astype(r[1].dtype)))
    return nn, nt, tn


mm, mm_nt, mm_tn = _make_mm(False)
xmm, xmm_nt, xmm_tn = _make_mm(True)


@jax.custom_vjp
def sel_mm(m, g):
    mb = m.astype(BF16)
    parts = [jnp.dot(mb, p, preferred_element_type=F32) for p in _split_bf16(g, 3)]
    return parts[0] + (parts[1] + parts[2])


def _sel_mm_bwd(m, dy):
    mb = m.astype(BF16)
    parts = [lax.dot_general(mb, p, (((0,), (0,)), ((), ())), preferred_element_type=F32) for p in _split_bf16(dy, 3)]
    return jnp.zeros_like(m), parts[0] + (parts[1] + parts[2])


sel_mm.defvjp(lambda m, g: (sel_mm(m, g), m), _sel_mm_bwd)


@jax.custom_vjp
def tri_inv(*mats):
    n = mats[0].shape[0]
    eye = jnp.where(lax.broadcasted_iota(jnp.int32, (n, n), 0) == lax.broadcasted_iota(jnp.int32, (n, n), 1), 1.0, 0.0)
    ts = [eye - a for a in mats]
    pws = list(mats)
    for _ in range(5):
        pws = [xmm(pw, pw) for pw in pws]
        ts = [t + xmm(t, pw) for t, pw in zip(ts, pws)]
    return tuple(ts)


def _tri_inv_bwd(ts, dts):
    inner = [xmm_nt(dt, t) for t, dt in zip(ts, dts)]
    return tuple(-xmm_tn(t, m) for t, m in zip(ts, inner))


tri_inv.defvjp(lambda *mats: (tri_inv(*mats),) * 2, _tri_inv_bwd)


@jax.custom_vjp
def tri_inv_known(a, t):
    return t


tri_inv_known.defvjp(lambda a, t: (t, t), lambda t, dt: (_tri_inv_bwd((t,), (dt,))[0], jnp.zeros_like(t)))


def _sigmoid(x):
    return 1.0 / (1.0 + jnp.exp(-x))


def _silu(x):
    return x * _sigmoid(x)


def _softplus(x):
    return jnp.maximum(x, 0.0) + jnp.log(1.0 + jnp.exp(-jnp.abs(x)))


def _cparams(n_grid):
    return pltpu.CompilerParams(dimension_semantics=("arbitrary",) * n_grid, vmem_limit_bytes=VMEM_LIMIT)


def _row_spec(tm, width, colblk):
    return pl.BlockSpec((tm, width), lambda i, cb=colblk: (i, cb))


def _const_spec(shape):
    nd = len(shape)
    return pl.BlockSpec(tuple(shape), lambda i, nd=nd: (0,) * nd)


def rowwise_fwd(name, f, rows, consts, outs, tm, carry=None):
    n_r, n_c = len(rows), len(consts)
    t = rows[0][0].shape[0]
    c_ins, c_in_specs, c_outs, c_out_specs, c_scratch = _host(carry)
    n_in, n_ci, n_co = n_r + n_c, len(c_ins), len(c_outs)

    def body(*refs):
        carried = (refs[n_in:n_in + n_ci], refs[n_in + n_ci + len(outs):n_in + n_ci + len(outs) + n_co],
                   *refs[n_in + n_ci + len(outs) + n_co:])
        if carry is not None:
            carry.emit_start(pl.program_id(0) == 0, *carried)
        vals = [r[...] for r in refs[:n_in]]
        res = f(*vals)
        if not isinstance(res, (tuple, list)):
            res = (res,)
        for o_ref, v in zip(refs[n_in + n_ci:n_in + n_ci + len(outs)], res):
            o_ref[...] = v.astype(o_ref.dtype)
        if carry is not None:
            carry.emit_finish(pl.program_id(0) == t // tm - 1, *carried)

    return pl.pallas_call(
        body, name=name, grid=(t // tm,),
        in_specs=[_row_spec(tm, w, cb) for _, w, cb in rows] + [_const_spec(c.shape) for c in consts] + c_in_specs,
        out_specs=[_row_spec(tm, w, 0) for w, _ in outs] + c_out_specs,
        out_shape=[jax.ShapeDtypeStruct((t, w), dt) for w, dt in outs] + c_outs,
        input_output_aliases={} if carry is None else carry.aliases(n_in, len(outs)),
        scratch_shapes=c_scratch,
        compiler_params=_cparams(1),
    )(*[a for a, _, _ in rows], *consts, *c_ins)


def rowwise_bwd(name, f, rows, consts, cts, row_grad_dtypes, tm, carry=None):
    n_r, n_c, n_ct = len(rows), len(consts), len(cts)
    t = rows[0][0].shape[0]
    keep = [k for k, dt in enumerate(row_grad_dtypes) if dt is not None]
    c_ins, c_in_specs, c_outs, c_out_specs, c_scratch = _host(carry)
    n_in, n_out = n_r + n_c + n_ct, len(keep) + n_c

    def body(*refs):
        ins = [r[...].astype(F32) for r in refs[:n_r + n_c]]
        g_out = [r[...].astype(F32) for r in refs[n_r + n_c:n_in]]
        out_refs = refs[n_in + len(c_ins):n_in + len(c_ins) + n_out]
        carried = (refs[n_in:n_in + len(c_ins)], refs[n_in + len(c_ins) + n_out:n_in + len(c_ins) + n_out + len(c_outs)],
                   *refs[n_in + len(c_ins) + n_out + len(c_outs):])
        if carry is not None:
            carry.emit_start(pl.program_id(0) == 0, *carried)

        def fw(*a):
            res = f(*a)
            return tuple(res) if isinstance(res, (tuple, list)) else (res,)

        _, vjp = jax.vjp(fw, *ins)
        grads = vjp(tuple(g_out))
        for o_ref, k in zip(out_refs[:len(keep)], keep):
            o_ref[...] = grads[k].astype(o_ref.dtype)
        first = pl.program_id(0) == 0
        for o_ref, g in zip(out_refs[len(keep):], grads[n_r:]):
            @pl.when(first)
            def _(o_ref=o_ref, g=g):
                o_ref[...] = g

            @pl.when(jnp.logical_not(first))
            def _(o_ref=o_ref, g=g):
                o_ref[...] += g
        if carry is not None:
            carry.emit_finish(pl.program_id(0) == t // tm - 1, *carried)

    return pl.pallas_call(
        body, name=name, grid=(t // tm,),
        in_specs=[_row_spec(tm, w, cb) for _, w, cb in rows] + [_const_spec(c.shape) for c in consts]
        + [_row_spec(tm, w, cb) for _, w, cb in cts] + c_in_specs,
        out_specs=[_row_spec(tm, rows[k][1], 0) for k in keep] + [_const_spec(c.shape) for c in consts] + c_out_specs,
        out_shape=[jax.ShapeDtypeStruct((t, rows[k][1]), row_grad_dtypes[k]) for k in keep]
        + [jax.ShapeDtypeStruct(c.shape, F32) for c in consts] + c_outs,
        scratch_shapes=c_scratch,
        compiler_params=_cparams(1),
    )(*[a for a, _, _ in rows], *consts, *[a for a, _, _ in cts], *c_ins)


def f_norm_mod(x, g, scale, shift):
    y = x * lax.rsqrt(jnp.mean(x * x, axis=-1, keepdims=True) + EPS) * g
    return y * (1.0 + scale) + shift


def f_conf_tail(u, zb, ln_g, ln_b, pw2_w, pw2_b):
    mu = jnp.mean(u, axis=-1, keepdims=True)
    xc = u - mu
    var = jnp.mean(xc * xc, axis=-1, keepdims=True)
    y = _silu(xc * lax.rsqrt(var + EPS) * ln_g + ln_b)
    return (mm(y, pw2_w) + pw2_b) * _silu(zb)


def f_merge(ya, yb, yc, mg, x, gate, wpa, wpb, wpc, wout):
    d = D_MODEL
    merged = (_sigmoid(mg[:, :d]) * mm(ya, wpa) + _sigmoid(mg[:, d:2 * d]) * mm(yb, wpb)
              + _sigmoid(mg[:, 2 * d:]) * mm(yc, wpc))
    return x + gate * mm(merged, wout)


def matmul_nn(name, a, b, out_dtype, tm, tn, tk, b_transposed=False, carry=None):
    m, k = a.shape
    n = b.shape[0] if b_transposed else b.shape[1]
    nk = k // tk
    grid = (m // tm, n // tn, nk)
    b_spec = (pl.BlockSpec((tn, tk), lambda i, j, kk: (j, kk)) if b_transposed
              else pl.BlockSpec((tk, tn), lambda i, j, kk: (kk, j)))
    c_ins, c_in_specs, c_outs, c_out_specs, c_scratch = _host(carry)
    n_ci, n_co = len(c_ins), len(c_outs)

    def body(*refs):
        a_ref, b_ref, o_ref = refs[0], refs[1], refs[2 + n_ci]
        carried = (refs[2:2 + n_ci], refs[3 + n_ci:3 + n_ci + n_co], *refs[3 + n_ci + n_co:3 + n_ci + n_co + len(c_scratch)])
        at = lambda step: functools.reduce(jnp.logical_and, [pl.program_id(d) == s for d, s in enumerate(step)])
        if carry is not None:
            carry.emit_start(at((0, 0, 0)), *carried)
        part = lax.dot_general(a_ref[...].astype(BF16), b_ref[...].astype(BF16),
                               (((1,), (1 if b_transposed else 0,)), ((), ())), preferred_element_type=F32)
        if nk == 1:
            o_ref[...] = part.astype(o_ref.dtype)
        else:
            kk = pl.program_id(2)
            acc_ref = refs[-1]

            @pl.when(kk == 0)
            def _():
                acc_ref[...] = part

            @pl.when(kk > 0)
            def _():
                acc_ref[...] += part

            @pl.when(kk == nk - 1)
            def _():
                o_ref[...] = acc_ref[...].astype(o_ref.dtype)
        if carry is not None:
            carry.emit_finish(at(tuple(g - 1 for g in grid)), *carried)

    res = pl.pallas_call(
        body, name=name, grid=grid,
        in_specs=[pl.BlockSpec((tm, tk), lambda i, j, kk: (i, kk)), b_spec] + c_in_specs,
        out_specs=[pl.BlockSpec((tm, tn), lambda i, j, kk: (i, j))] + c_out_specs,
        out_shape=[jax.ShapeDtypeStruct((m, n), out_dtype)] + c_outs,
        input_output_aliases={} if carry is None else carry.aliases(2, 1),
        scratch_shapes=c_scratch + ([] if nk == 1 else [pltpu.VMEM((tm, tn), F32)]),
        compiler_params=_cparams(3),
    )(a, b, *c_ins)
    return res[0] if carry is None else res


def ada_fwd(c8, w_shard, b_ada):
    n_cols = w_shard.shape[2]
    masks = [(m >> 2 & 1, m >> 1 & 1, m & 1) for m in range(1, 8)]

    def body(c_ref, w_ref, b_ref, mod_ref, conds_ref, cbuf, sendbuf, recvbuf, send_sems, recv_sems):
        x, y, c, chips = _place()
        flip = lambda v, bit: 1 - v if bit else v
        peers = [(flip(x, mx), flip(y, my), flip(c, mc)) for mx, my, mc in masks]
        dev = lambda p: 4 * p[0] + 2 * p[1] + p[2]
        cbuf[dev((x, y, c))] = c_ref[...]
        first = [_remote(c_ref, cbuf.at[dev((x, y, c))], send_sems.at[i], recv_sems.at[i], p) for i, p in enumerate(peers)]
        for cp in first:
            cp.start()
        for i, p in enumerate(peers):
            _remote(c_ref, cbuf.at[dev(p)], send_sems.at[i], recv_sems.at[i], p).wait_recv()
        conds = jnp.concatenate([cbuf[d, 0:1, :] for d in range(8)], axis=0)
        conds_ref[...] = conds
        act = _silu(conds)
        parts = [mm(act, w_ref[l]) for l in range(DEPTH)]
        row8 = lax.broadcasted_iota(jnp.int32, (8, 1), 0)

        def tile_for(chip):
            r = 2 * (2 * chip[0] + chip[1]) + c
            rows = [jnp.sum(jnp.where(row8 == r, parts[l], 0.0), axis=0, keepdims=True) for l in range(DEPTH)]
            return jnp.where(row8 == 0, rows[0], jnp.where(row8 == 1, rows[1], 0.0))

        my_slot = 2 * x + y
        recvbuf[my_slot] = tile_for((x, y))
        second = []
        for j, chip in enumerate(chips):
            sendbuf[j] = tile_for(chip)
            second.append(_remote(sendbuf.at[j], recvbuf.at[my_slot], send_sems.at[7 + j], recv_sems.at[7 + j], (*chip, c)))
            second[-1].start()
        for j, chip in enumerate(chips):
            _remote(sendbuf.at[j], recvbuf.at[2 * chip[0] + chip[1]], send_sems.at[7 + j], recv_sems.at[7 + j],
                    (*chip, c)).wait_recv()
        rows = [jnp.concatenate([recvbuf[k, l:l + 1, :] for k in range(N_CHIPS)], axis=1) + b_ref[l:l + 1, :]
                for l in range(DEPTH)]
        mod_ref[...] = jnp.concatenate(rows + [jnp.zeros((8 - DEPTH, N_CHIPS * n_cols), F32)], axis=0)
        for cp in first + second:
            cp.wait_send()

    vm = pl.BlockSpec(memory_space=pltpu.VMEM)
    return pl.pallas_call(
        body, name="ada_fwd",
        out_shape=[jax.ShapeDtypeStruct((8, N_CHIPS * n_cols), F32), jax.ShapeDtypeStruct((8, D_MODEL), F32)],
        in_specs=[vm, vm, vm], out_specs=[vm, vm],
        scratch_shapes=[pltpu.VMEM((8, 8, D_MODEL), F32), pltpu.VMEM((3, 8, n_cols), F32),
                        pltpu.VMEM((N_CHIPS, 8, n_cols), F32), pltpu.SemaphoreType.DMA((10,)), pltpu.SemaphoreType.DMA((10,))],
        compiler_params=pltpu.CompilerParams(vmem_limit_bytes=VMEM_LIMIT),
    )(c8, w_shard, b_ada)


def ada_bwd(conds, dmod):
    def body(c_ref, d_ref, o_ref):
        act = _silu(c_ref[...])
        for l in range(DEPTH):
            o_ref[l] = mm_tn(act, d_ref[l])

    return pl.pallas_call(
        body, name="ada_bwd", out_shape=jax.ShapeDtypeStruct((DEPTH, D_MODEL, dmod.shape[2]), F32),
        compiler_params=pltpu.CompilerParams(vmem_limit_bytes=VMEM_LIMIT),
    )(conds, dmod)


def _f_attn(first_block, q, za, kc, vc, kp, vp, qg, kg, sinks):
    w = WINDOW
    lane = lax.broadcasted_iota(jnp.int32, (1, 128), 1)
    halves = [lane < 64, lane >= 64]

    def rms_halves(x, g):
        x2 = x * x
        s0 = jnp.sum(jnp.where(halves[0], x2, 0.0), axis=-1, keepdims=True)
        s1 = jnp.sum(jnp.where(halves[1], x2, 0.0), axis=-1, keepdims=True)
        r = jnp.where(halves[0], lax.rsqrt(s0 / 64.0 + EPS), lax.rsqrt(s1 / 64.0 + EPS))
        return x * r * g

    kcat = rms_halves(jnp.concatenate([kp, kc], axis=0), kg)
    vcat = jnp.concatenate([vp, vc], axis=0)
    qi = lax.broadcasted_iota(jnp.int32, (w, 2 * w), 0)
    kj = lax.broadcasted_iota(jnp.int32, (w, 2 * w), 1)
    dist = qi + w - kj
    valid = (dist >= 0) & (dist < w) & (jnp.logical_not(first_block) | (kj >= w))
    distf = dist.astype(F32)
    units = [(grp, half) for grp in range(4) for half in range(2)]
    qns = [rms_halves(q[:, 128 * grp:128 * grp + 128], qg) * (ATT_HEAD_DIM ** -0.5) for grp in range(4)]
    vhalf = [jnp.where(halves[half], vcat, 0.0) for half in range(2)]
    scores, sinks_h = [], []
    for grp, half in units:
        head = HEAD_ORDER[2 * grp + half]
        slope = 2.0 ** (-8.0 * (head + 1) / ATT_HEADS)
        sinks_h.append(jnp.sum(jnp.where(lane == head, sinks, 0.0), axis=-1, keepdims=True))
        s = mm_nt(jnp.where(halves[half], qns[grp], 0.0), kcat) - slope * distf
        scores.append(jnp.where(valid, s, NEG_INF))
    probs = []
    for s, sink in zip(scores, sinks_h):
        m = lax.stop_gradient(jnp.maximum(jnp.max(s, axis=-1, keepdims=True), sink))
        p = jnp.exp(s - m)
        probs.append(p / (jnp.sum(p, axis=-1, keepdims=True) + jnp.exp(sink - m)))
    outs = [mm(p, vhalf[half]) for p, (grp, half) in zip(probs, units)]
    return jnp.concatenate([outs[2 * grp] + outs[2 * grp + 1] for grp in range(4)], axis=1) * _silu(za)


def attn_fwd(name, proj, qg, kg, sinks):
    t = proj.shape[0]
    nb = t // WINDOW

    def body(q_ref, za_ref, kc_ref, vc_ref, kp_ref, vp_ref, qg_ref, kg_ref, s_ref, o_ref):
        first = pl.program_id(0) == 0
        o_ref[...] = _f_attn(first, q_ref[...], za_ref[...], kc_ref[...], vc_ref[...], kp_ref[...], vp_ref[...],
                             qg_ref[...], kg_ref[...], s_ref[...])

    cur = lambda cb: (lambda i: (i, cb))
    prev = lambda cb: (lambda i: (jnp.maximum(i - 1, 0), cb))
    return pl.pallas_call(
        body, name=name, grid=(nb,),
        in_specs=[pl.BlockSpec((WINDOW, 512), cur(P_QA // 512)), pl.BlockSpec((WINDOW, 512), cur(P_ZA // 512)),
                  pl.BlockSpec((WINDOW, 128), cur(P_KA // 128)), pl.BlockSpec((WINDOW, 128), cur(P_VA // 128)),
                  pl.BlockSpec((WINDOW, 128), prev(P_KA // 128)), pl.BlockSpec((WINDOW, 128), prev(P_VA // 128)),
                  _const_spec((1, 128)), _const_spec((1, 128)), _const_spec((1, 128))],
        out_specs=pl.BlockSpec((WINDOW, 512), lambda i: (i, 0)),
        out_shape=jax.ShapeDtypeStruct((t, 512), F32),
        compiler_params=_cparams(1),
    )(proj, proj, proj, proj, proj, proj, qg, kg, sinks)


def attn_bwd(name, proj, qg, kg, sinks, dya):
    t = proj.shape[0]
    nb = t // WINDOW

    def body(q_ref, za_ref, kc_ref, vc_ref, kp_ref, vp_ref, qg_ref, kg_ref, s_ref, dy_ref,
             dqz_ref, dkv_ref, dqg_ref, dkg_ref, ds_ref, carry_ref):
        j = pl.program_id(0)
        first = j == nb - 1

        @pl.when(j == 0)
        def _():
            carry_ref[...] = jnp.zeros_like(carry_ref)
            dqg_ref[...] = jnp.zeros_like(dqg_ref)
            dkg_ref[...] = jnp.zeros_like(dkg_ref)
            ds_ref[...] = jnp.zeros_like(ds_ref)

        ins = [r[...] for r in (q_ref, za_ref, kc_ref, vc_ref, kp_ref, vp_ref, qg_ref, kg_ref, s_ref)]
        _, vjp = jax.vjp(functools.partial(_f_attn, first), *ins)
        dq, dza, dkc, dvc, dkp, dvp, dqg, dkg, dsk = vjp(dy_ref[...])
        dqz_ref[:, 0:512] = dq.astype(dqz_ref.dtype)
        dqz_ref[:, 512:1024] = dza.astype(dqz_ref.dtype)
        dkv_ref[:, 0:128] = (dkc + carry_ref[0]).astype(dkv_ref.dtype)
        dkv_ref[:, 128:256] = (dvc + carry_ref[1]).astype(dkv_ref.dtype)
        carry_ref[0] = dkp
        carry_ref[1] = dvp
        dqg_ref[...] += dqg
        dkg_ref[...] += dkg
        ds_ref[...] += dsk

    cur = lambda cb: (lambda j: (nb - 1 - j, cb))
    prev = lambda cb: (lambda j: (jnp.maximum(nb - 2 - j, 0), cb))
    return pl.pallas_call(
        body, name=name, grid=(nb,),
        in_specs=[pl.BlockSpec((WINDOW, 512), cur(P_QA // 512)), pl.BlockSpec((WINDOW, 512), cur(P_ZA // 512)),
                  pl.BlockSpec((WINDOW, 128), cur(P_KA // 128)), pl.BlockSpec((WINDOW, 128), cur(P_VA // 128)),
                  pl.BlockSpec((WINDOW, 128), prev(P_KA // 128)), pl.BlockSpec((WINDOW, 128), prev(P_VA // 128)),
                  _const_spec((1, 128)), _const_spec((1, 128)), _const_spec((1, 128)),
                  pl.BlockSpec((WINDOW, 512), cur(0))],
        out_specs=[pl.BlockSpec((WINDOW, 1024), cur(0)), pl.BlockSpec((WINDOW, 256), cur(0)),
                   _const_spec((1, 128)), _const_spec((1, 128)), _const_spec((1, 128))],
        out_shape=[jax.ShapeDtypeStruct((t, 1024), BF16), jax.ShapeDtypeStruct((t, 256), BF16),
                   jax.ShapeDtypeStruct((1, 128), F32), jax.ShapeDtypeStruct((1, 128), F32),
                   jax.ShapeDtypeStruct((1, 128), F32)],
        scratch_shapes=[pltpu.VMEM((2, WINDOW, 128), F32)],
        compiler_params=_cparams(1),
    )(proj, proj, proj, proj, proj, proj, qg, kg, sinks, dya)


CONV_ROWS = 256


def _conv_taps(src_ref, w_ref, n_taps, base, t):
    for r0 in range(0, t, CONV_ROWS):
        acc = w_ref[0:1, :] * src_ref[pl.ds(r0 + base, CONV_ROWS), :]
        for k in range(1, n_taps):
            acc = acc + w_ref[k:k + 1, :] * src_ref[pl.ds(r0 + base + k, CONV_ROWS), :]
        yield r0, acc


def _conv_wgrad(dy_ref, src_ref, n_taps, base, t, dy_base=0):
    out = []
    for k in range(n_taps):
        acc = jnp.zeros((8, 128), F32)
        for r0 in range(0, t, CONV_ROWS):
            prod = dy_ref[pl.ds(r0 + dy_base, CONV_ROWS), :] * src_ref[pl.ds(r0 + base + k, CONV_ROWS), :]
            acc = acc + jnp.sum(prod.reshape(CONV_ROWS // 8, 8, 128), axis=0)
        out.append(jnp.sum(acc, axis=0, keepdims=True))
    return out


def glu_conv_fwd(name, proj, w32, bias):
    t = proj.shape[0]
    pad = 32

    def body(x_ref, w_ref, b_ref, o_ref, u_ref):
        u_ref[0:pad, :] = jnp.zeros((pad, 128), F32)
        u_ref[pad:pad + t, :] = x_ref[:, 0:128] * _sigmoid(x_ref[:, 128:256])
        for r0, acc in _conv_taps(u_ref, w_ref, CONV_K, pad - (CONV_K - 1), t):
            o_ref[pl.ds(r0, CONV_ROWS), :] = acc + b_ref[...]

    return pl.pallas_call(
        body, name=name, grid=(4,),
        in_specs=[pl.BlockSpec((t, 256), lambda cb: (0, P_GLU // 256 + cb)), pl.BlockSpec((32, 128), lambda cb: (0, cb)),
                  pl.BlockSpec((1, 128), lambda cb: (0, cb))],
        out_specs=pl.BlockSpec((t, 128), lambda cb: (0, cb)),
        out_shape=jax.ShapeDtypeStruct((t, 512), F32),
        scratch_shapes=[pltpu.VMEM((t + pad, 128), F32)],
        compiler_params=_cparams(1),
    )(proj, w32, bias)


def glu_conv_bwd(name, proj, w32, dub):
    t = proj.shape[0]
    pad = 32
    k1 = CONV_K - 1

    def body(x_ref, w_ref, dy_ref, dx_ref, dw_ref, db_ref, u_ref, dyp_ref, wrev_ref):
        val = x_ref[:, 0:128]
        sg = _sigmoid(x_ref[:, 128:256])
        u_ref[0:pad, :] = jnp.zeros((pad, 128), F32)
        u_ref[pad:pad + t, :] = val * sg
        dyp_ref[0:t, :] = dy_ref[...]
        dyp_ref[t:t + pad, :] = jnp.zeros((pad, 128), F32)
        for k in range(CONV_K):
            wrev_ref[k:k + 1, :] = w_ref[k1 - k:k1 - k + 1, :]
        wrev_ref[CONV_K:32, :] = jnp.zeros((32 - CONV_K, 128), F32)
        for r0, du in _conv_taps(dyp_ref, wrev_ref, CONV_K, 0, t):
            v = x_ref[pl.ds(r0, CONV_ROWS), 0:128]
            s = _sigmoid(x_ref[pl.ds(r0, CONV_ROWS), 128:256])
            dx_ref[pl.ds(r0, CONV_ROWS), 0:128] = (du * s).astype(dx_ref.dtype)
            dx_ref[pl.ds(r0, CONV_ROWS), 128:256] = (du * v * s * (1.0 - s)).astype(dx_ref.dtype)
        dws = _conv_wgrad(dyp_ref, u_ref, CONV_K, pad - k1, t)
        for k in range(CONV_K):
            dw_ref[k:k + 1, :] = dws[k]
        dw_ref[CONV_K:32, :] = jnp.zeros((32 - CONV_K, 128), F32)
        db_ref[...] = jnp.sum(dy_ref[...], axis=0, keepdims=True)

    return pl.pallas_call(
        body, name=name, grid=(4,),
        in_specs=[pl.BlockSpec((t, 256), lambda cb: (0, P_GLU // 256 + cb)), pl.BlockSpec((32, 128), lambda cb: (0, cb)),
                  pl.BlockSpec((t, 128), lambda cb: (0, cb))],
        out_specs=[pl.BlockSpec((t, 256), lambda cb: (0, cb)), pl.BlockSpec((32, 128), lambda cb: (0, cb)),
                   pl.BlockSpec((1, 128), lambda cb: (0, cb))],
        out_shape=[jax.ShapeDtypeStruct((t, 1024), BF16), jax.ShapeDtypeStruct((32, 512), F32),
                   jax.ShapeDtypeStruct((1, 512), F32)],
        scratch_shapes=[pltpu.VMEM((t + pad, 128), F32), pltpu.VMEM((t + pad, 128), F32), pltpu.VMEM((32, 128), F32)],
        compiler_params=_cparams(1),
    )(proj, w32, dub)


def sconv_fwd(name, proj, w8):
    t = proj.shape[0]
    pad = 8
    k1 = DN_CONV_K - 1

    def body(x_ref, w_ref, o_ref, xp_ref):
        xp_ref[0:pad, :] = jnp.zeros((pad, 128), F32)
        xp_ref[pad:pad + t, :] = x_ref[...]
        for r0, acc in _conv_taps(xp_ref, w_ref, DN_CONV_K, pad - k1, t):
            o_ref[pl.ds(r0, CONV_ROWS), :] = _silu(acc)

    return pl.pallas_call(
        body, name=name, grid=(12,),
        in_specs=[pl.BlockSpec((t, 128), lambda cb: (0, P_QKV // 128 + cb)), pl.BlockSpec((8, 128), lambda cb: (0, cb))],
        out_specs=pl.BlockSpec((t, 128), lambda cb: (0, cb)),
        out_shape=jax.ShapeDtypeStruct((t, 1536), F32),
        scratch_shapes=[pltpu.VMEM((t + pad, 128), F32)],
        compiler_params=_cparams(1),
    )(proj, w8)


def sconv_bwd(name, proj, w8, dqkv):
    t = proj.shape[0]
    pad = 8
    k1 = DN_CONV_K - 1

    def body(x_ref, w_ref, dy_ref, dx_ref, dw_ref, xp_ref, dpp_ref, wrev_ref):
        xp_ref[0:pad, :] = jnp.zeros((pad, 128), F32)
        xp_ref[pad:pad + t, :] = x_ref[...]
        for r0, pre in _conv_taps(xp_ref, w_ref, DN_CONV_K, pad - k1, t):
            s = _sigmoid(pre)
            dpp_ref[pl.ds(r0, CONV_ROWS), :] = dy_ref[pl.ds(r0, CONV_ROWS), :] * (s * (1.0 + pre * (1.0 - s)))
        dpp_ref[t:t + pad, :] = jnp.zeros((pad, 128), F32)
        for k in range(DN_CONV_K):
            wrev_ref[k:k + 1, :] = w_ref[k1 - k:k1 - k + 1, :]
        wrev_ref[DN_CONV_K:8, :] = jnp.zeros((8 - DN_CONV_K, 128), F32)
        for r0, dx in _conv_taps(dpp_ref, wrev_ref, DN_CONV_K, 0, t):
            dx_ref[pl.ds(r0, CONV_ROWS), :] = dx.astype(dx_ref.dtype)
        dws = _conv_wgrad(dpp_ref, xp_ref, DN_CONV_K, pad - k1, t)
        for k in range(DN_CONV_K):
            dw_ref[k:k + 1, :] = dws[k]
        dw_ref[DN_CONV_K:8, :] = jnp.zeros((8 - DN_CONV_K, 128), F32)

    return pl.pallas_call(
        body, name=name, grid=(12,),
        in_specs=[pl.BlockSpec((t, 128), lambda cb: (0, P_QKV // 128 + cb)), pl.BlockSpec((8, 128), lambda cb: (0, cb)),
                  pl.BlockSpec((t, 128), lambda cb: (0, cb))],
        out_specs=[pl.BlockSpec((t, 128), lambda cb: (0, cb)), pl.BlockSpec((8, 128), lambda cb: (0, cb))],
        out_shape=[jax.ShapeDtypeStruct((t, 1536), BF16), jax.ShapeDtypeStruct((8, 1536), F32)],
        scratch_shapes=[pltpu.VMEM((t + pad, 128), F32), pltpu.VMEM((t + pad, 128), F32), pltpu.VMEM((8, 128), F32)],
        compiler_params=_cparams(1),
    )(proj, w8, dqkv)


def _f_delta_step(qkv, ab, zc, s0, s1, s2, s3, a_log, dt_bias, dn_g, inverses=None, with_inverses=False):
    cs = DN_CHUNK
    n = 2 * cs
    states = (s0, s1, s2, s3)
    lane = lax.broadcasted_iota(jnp.int32, (1, 128), 1)
    ri = lax.broadcasted_iota(jnp.int32, (n, n), 0)
    ci = lax.broadcasted_iota(jnp.int32, (n, n), 1)
    same = (ri // cs) == (ci // cs)
    lower = same & (ri >= ci)
    strict = same & (ri > ci)
    sums = jnp.concatenate([jnp.where(lower, 1.0, 0.0), jnp.where(same, 1.0, 0.0), jnp.where(ci < cs, 1.0, 0.0),
                            jnp.where(ci >= cs, 1.0, 0.0)], axis=0)
    top = lax.broadcasted_iota(jnp.int32, (n, 1), 0) < cs

    def pick(row, idx):
        return jnp.sum(jnp.where(lane == idx, row, 0.0), axis=-1, keepdims=True)

    def l2n(x):
        return x * lax.rsqrt(jnp.sum(x * x, axis=-1, keepdims=True) + EPS)

    n_chunks = qkv.shape[0] // cs
    units = [(k, pair) for k in range(n_chunks) for pair in range(2)]

    pre = []
    for k, pair in units:
        hs = (2 * pair, 2 * pair + 1)
        rows = slice(k * cs, (k + 1) * cs)
        stack = lambda f: jnp.concatenate([f(hs[0]), f(hs[1])], axis=0)
        qd = l2n(stack(lambda h: qkv[rows, 128 * h:128 * h + 128])) * (128 ** -0.5)
        kd = l2n(stack(lambda h: qkv[rows, 512 + 128 * h:512 + 128 * h + 128]))
        vd = stack(lambda h: qkv[rows, 1024 + 128 * h:1024 + 128 * h + 128])
        beta = _sigmoid(stack(lambda h: pick(ab[rows], 4 + h)))
        g = stack(lambda h: -jnp.exp(pick(a_log, h)) * _softplus(pick(ab[rows], h) + pick(dt_bias, h)))
        g_sums = sel_mm(sums, g * jnp.ones((1, n), F32))
        gc_col = g_sums[0:n]
        gl_b = g_sums[n:2 * n]
        g_end = (g_sums[2 * n:3 * n], g_sums[3 * n:])
        decay = jnp.where(lower, jnp.exp(jnp.where(lower, gc_col - gc_col.T, 0.0)), 0.0)
        kb = kd * beta
        pre.append(dict(qd=qd, kd=kd, vb=vd * beta, kb=kb, gc_col=gc_col, gl_b=gl_b, g_end=g_end, decay=decay,
                        a=jnp.where(strict, mm_nt(kb, kd) * decay, 0.0)))
    if inverses is None:
        tmats = tri_inv(*[p["a"] for p in pre])
    else:
        tmats = [tri_inv_known(p["a"], t) for p, t in zip(pre, inverses)]

    mid = []
    for p, tmat in zip(pre, tmats):
        egc = jnp.exp(p["gc_col"])
        mid.append(dict(u=mm(tmat, p["vb"]), wm=mm(tmat, p["kb"] * egc), qe=p["qd"] * egc,
                        intra=jnp.where(lower, mm_nt(p["qd"], p["kd"]) * p["decay"], 0.0),
                        ke=p["kd"] * jnp.exp(p["gl_b"] - p["gc_col"]), g_end=p["g_end"]))

    ys = []
    for k in range(n_chunks):
        rows = slice(k * cs, (k + 1) * cs)
        new_states, y_heads = [], []
        for pair in range(2):
            m = mid[2 * k + pair]
            hs = (2 * pair, 2 * pair + 1)
            st = (states[hs[0]], states[hs[1]])
            v_new = m["u"] - jnp.concatenate([mm(m["wm"][:cs], st[0]), mm(m["wm"][cs:], st[1])], axis=0)
            o = jnp.concatenate([mm(m["qe"][:cs], st[0]), mm(m["qe"][cs:], st[1])], axis=0) + mm(m["intra"], v_new)
            new_states.append(st[0] * jnp.exp(m["g_end"][0]) + mm_tn(jnp.where(top, m["ke"], 0.0), v_new))
            new_states.append(st[1] * jnp.exp(m["g_end"][1]) + mm_tn(jnp.where(top, 0.0, m["ke"]), v_new))
            od = o * lax.rsqrt(jnp.mean(o * o, axis=-1, keepdims=True) + EPS) * dn_g
            y_heads += [od[:cs] * _silu(zc[rows, 128 * hs[0]:128 * hs[0] + 128]),
                        od[cs:] * _silu(zc[rows, 128 * hs[1]:128 * hs[1] + 128])]
        states = tuple(new_states)
        ys.append(jnp.concatenate(y_heads, axis=1))
    if with_inverses:
        return (jnp.concatenate(ys, axis=0), *states), tmats
    return (jnp.concatenate(ys, axis=0), *states)


DELTA_ROWS = 4 * DN_CHUNK
DELTA_UNITS = 2 * DELTA_ROWS // DN_CHUNK


def delta_fwd(name, qkv, proj, a_log, dt_bias, dn_g):
    t = qkv.shape[0]
    nc = t // DELTA_ROWS

    def body(qkv_ref, ab_ref, zc_ref, al_ref, dt_ref, g_ref, y_ref, ssave_ref, tsave_ref, s_ref):
        @pl.when(pl.program_id(0) == 0)
        def _():
            s_ref[...] = jnp.zeros_like(s_ref)

        ssave_ref[0] = s_ref[...]
        st = [s_ref[128 * h:128 * h + 128, :] for h in range(4)]
        (y, *ns), tmats = _f_delta_step(qkv_ref[...], ab_ref[...], zc_ref[...], *st, al_ref[...], dt_ref[...], g_ref[...],
                                        with_inverses=True)
        y_ref[...] = y
        for h in range(4):
            s_ref[128 * h:128 * h + 128, :] = ns[h]
        for u, tm in enumerate(tmats):
            tsave_ref[0, 128 * u:128 * u + 128, :] = tm

    return pl.pallas_call(
        body, name=name, grid=(nc,),
        in_specs=[pl.BlockSpec((DELTA_ROWS, 1536), lambda i: (i, 0)), pl.BlockSpec((DELTA_ROWS, 128), lambda i: (i, P_AB // 128)),
                  pl.BlockSpec((DELTA_ROWS, 512), lambda i: (i, P_ZC // 512)),
                  _const_spec((1, 128)), _const_spec((1, 128)), _const_spec((1, 128))],
        out_specs=[pl.BlockSpec((DELTA_ROWS, 512), lambda i: (i, 0)), pl.BlockSpec((1, 512, 128), lambda i: (i, 0, 0)),
                   pl.BlockSpec((1, DELTA_UNITS * 128, 128), lambda i: (i, 0, 0))],
        out_shape=[jax.ShapeDtypeStruct((t, 512), F32), jax.ShapeDtypeStruct((nc, 512, 128), F32),
                   jax.ShapeDtypeStruct((nc, DELTA_UNITS * 128, 128), F32)],
        scratch_shapes=[pltpu.VMEM((512, 128), F32)],
        compiler_params=_cparams(1),
    )(qkv, proj, proj, a_log, dt_bias, dn_g)


def delta_bwd(name, qkv, proj, ssave, tsave, a_log, dt_bias, dn_g, dyc, carry=None):
    t = qkv.shape[0]
    nc = t // DELTA_ROWS
    c_ins, c_in_specs, c_outs, c_out_specs, c_scratch = _host(carry)
    n_ci, n_co = len(c_ins), len(c_outs)

    def body(*refs):
        qkv_ref, ab_ref, zc_ref, ss_ref, ts_ref, al_ref, dt_ref, g_ref, dy_ref = refs[:9]
        dqkv_ref, dab_ref, dzc_ref, dal_ref, ddt_ref, dg_ref = refs[9 + n_ci:15 + n_ci]
        ds_ref = refs[15 + n_ci + n_co]
        carried = (refs[9:9 + n_ci], refs[15 + n_ci:15 + n_ci + n_co], *refs[16 + n_ci + n_co:])

        @pl.when(pl.program_id(0) == 0)
        def _():
            ds_ref[...] = jnp.zeros_like(ds_ref)
            dal_ref[...] = jnp.zeros_like(dal_ref)
            ddt_ref[...] = jnp.zeros_like(ddt_ref)
            dg_ref[...] = jnp.zeros_like(dg_ref)

        if carry is not None:
            carry.emit_start(pl.program_id(0) == 0, *carried)

        st = [ss_ref[0, 128 * h:128 * h + 128, :] for h in range(4)]
        known = [ts_ref[0, 128 * u:128 * u + 128, :] for u in range(DELTA_UNITS)]
        _, vjp = jax.vjp(functools.partial(_f_delta_step, inverses=known), qkv_ref[...], ab_ref[...], zc_ref[...], *st,
                         al_ref[...], dt_ref[...], g_ref[...])
        dst = tuple(ds_ref[128 * h:128 * h + 128, :] for h in range(4))
        dqkv, dab, dzc, d0, d1, d2, d3, dal, ddt, dg = vjp((dy_ref[...], *dst))
        dqkv_ref[...] = dqkv
        dab_ref[...] = dab.astype(dab_ref.dtype)
        dzc_ref[...] = dzc.astype(dzc_ref.dtype)
        for h, d in enumerate((d0, d1, d2, d3)):
            ds_ref[128 * h:128 * h + 128, :] = d
        dal_ref[...] += dal
        ddt_ref[...] += ddt
        dg_ref[...] += dg

        if carry is not None:
            carry.emit_finish(pl.program_id(0) == nc - 1, *carried)

    rev = lambda cb: (lambda j: (nc - 1 - j, cb))
    return pl.pallas_call(
        body, name=name, grid=(nc,),
        in_specs=[pl.BlockSpec((DELTA_ROWS, 1536), rev(0)), pl.BlockSpec((DELTA_ROWS, 128), rev(P_AB // 128)),
                  pl.BlockSpec((DELTA_ROWS, 512), rev(P_ZC // 512)), pl.BlockSpec((1, 512, 128), lambda j: (nc - 1 - j, 0, 0)),
                  pl.BlockSpec((1, DELTA_UNITS * 128, 128), lambda j: (nc - 1 - j, 0, 0)),
                  _const_spec((1, 128)), _const_spec((1, 128)), _const_spec((1, 128)),
                  pl.BlockSpec((DELTA_ROWS, 512), rev(0))] + c_in_specs,
        out_specs=[pl.BlockSpec((DELTA_ROWS, 1536), rev(0)), pl.BlockSpec((DELTA_ROWS, 128), rev(0)),
                   pl.BlockSpec((DELTA_ROWS, 512), rev(0)),
                   _const_spec((1, 128)), _const_spec((1, 128)), _const_spec((1, 128))] + c_out_specs,
        out_shape=[jax.ShapeDtypeStruct((t, 1536), F32), jax.ShapeDtypeStruct((t, 128), BF16),
                   jax.ShapeDtypeStruct((t, 512), BF16),
                   jax.ShapeDtypeStruct((1, 128), F32), jax.ShapeDtypeStruct((1, 128), F32), jax.ShapeDtypeStruct((1, 128), F32)]
        + c_outs,
        scratch_shapes=[pltpu.VMEM((512, 128), F32)] + c_scratch,
        compiler_params=_cparams(1),
    )(qkv, proj, proj, ssave, tsave, a_log, dt_bias, dn_g, dyc, *c_ins)


def loss_head(name, y, target, tm):
    t, d = y.shape

    def body(y_ref, t_ref, dy_ref, l_ref):
        err = y_ref[...] - t_ref[...]
        dy_ref[...] = err * (1.0 / d)
        part = 0.5 * jnp.sum(jnp.sum(err * err, axis=-1, keepdims=True) * (1.0 / d), axis=0, keepdims=True)

        @pl.when(pl.program_id(0) == 0)
        def _():
            l_ref[...] = part

        @pl.when(pl.program_id(0) > 0)
        def _():
            l_ref[...] += part

    return pl.pallas_call(
        body, name=name, grid=(t // tm,),
        in_specs=[_row_spec(tm, d, 0), _row_spec(tm, d, 0)],
        out_specs=[_row_spec(tm, d, 0), _const_spec((1, 1))],
        out_shape=[jax.ShapeDtypeStruct((t, d), F32), jax.ShapeDtypeStruct((1, 1), F32)],
        compiler_params=_cparams(1),
    )(y, target)


TM = 512
TM_MERGE = 256
TM_IN = 1024
TN_IN = 1152


def _lane_pad(v, n=128):
    return jnp.pad(v.astype(F32), (0, n - v.shape[0]))[None, :]


def f_norm_mod_res(x, g, scale, shift):
    return f_norm_mod(x, g, scale, shift), x


def prep_layer(w):
    p = dict(w)
    p["wp"] = _w_in_assemble(w["w_in"])
    p["wpa"] = _perm_heads_rows(w["w_proj_a"])
    p["dw32"] = jnp.pad(w["dw_w"], ((0, 32 - CONV_K), (0, 0)))
    p["sconv8"] = jnp.pad(w["sconv_w"], ((0, 8 - DN_CONV_K), (0, 0)))
    p["qg"] = jnp.tile(w["q_norm_g"], 2)[None, :]
    p["kg"] = jnp.tile(w["k_norm_g"], 2)[None, :]
    p["sinks128"] = _lane_pad(w["sinks"])
    p["al"] = _lane_pad(w["a_log"])
    p["dtb"] = _lane_pad(w["dt_bias"])
    p["dng"] = w["dn_norm_g"][None, :]
    return p


def layer_fwd(tag, x, mod, p, carry_inproj=None, carry_merge=None):
    d = D_MODEL
    shift, scale, gate = mod[:, :d], mod[:, d:2 * d], mod[:, 2 * d:]
    g = p["norm_g"][None, :]
    (h,) = rowwise_fwd(f"norm_fwd{tag}", f_norm_mod, [(x, d, 0)], [g, scale, shift], [(d, BF16)], TM)
    proj = matmul_nn(f"inproj_fwd{tag}", h, p["wp"], F32, TM_IN, TN_IN, d, carry=carry_inproj)
    proj, got_inproj = (proj, []) if carry_inproj is None else (proj[0], proj[1:])
    ya = attn_fwd(f"attn_fwd{tag}", proj, p["qg"], p["kg"], p["sinks128"])
    ub = glu_conv_fwd(f"glu_conv_fwd{tag}", proj, p["dw32"], p["dw_b"][None, :])
    conf_consts = [p["ln_g"][None, :], p["ln_b"][None, :], p["pw2_w"], p["pw2_b"][None, :]]
    (yb,) = rowwise_fwd(f"conf_fwd{tag}", f_conf_tail, [(ub, 512, 0), (proj, 512, P_ZB // 512)], conf_consts, [(512, F32)], TM)
    qkv = sconv_fwd(f"sconv_fwd{tag}", proj, p["sconv8"])
    yc, ssave, tsave = delta_fwd(f"delta_fwd{tag}", qkv, proj, p["al"], p["dtb"], p["dng"])
    merge_consts = [gate, p["wpa"], p["w_proj_b"], p["w_proj_c"], p["w_out"]]
    merge_rows = [(ya, 512, 0), (yb, 512, 0), (yc, 512, 0), (proj, 3 * d, P_MG // (3 * d)), (x, d, 0)]
    xn, *got_merge = rowwise_fwd(f"merge_fwd{tag}", f_merge, merge_rows, merge_consts, [(d, F32)], TM_MERGE, carry=carry_merge)
    saved = dict(x=x, h=h, proj=proj, ub=ub, qkv=qkv, ssave=ssave, tsave=tsave, norm_consts=[g, scale, shift],
                 conf_consts=conf_consts, merge_consts=merge_consts, merge_rows=merge_rows)
    return xn, saved, got_inproj, got_merge


def layer_bwd(tag, dxn, p, s, carry_merge=None, carry_delta=None, carry_dh=None):
    d = D_MODEL
    proj = s["proj"]
    dya, dyb, dyc, dmg, dgate, dwpa, dwpb, dwpc, dwout, *got_merge = rowwise_bwd(
        f"merge_bwd{tag}", f_merge, s["merge_rows"], s["merge_consts"], [(dxn, d, 0)], [F32, F32, F32, BF16, None], TM_MERGE,
        carry=carry_merge)
    carry_delta = None if carry_delta is None else carry_delta(got_merge)
    dqz, dkv, dqg, dkg, dsinks = attn_bwd(f"attn_bwd{tag}", proj, p["qg"], p["kg"], p["sinks128"], dya)
    dub, dzb, dln_g, dln_b, dpw2_w, dpw2_b = rowwise_bwd(
        f"conf_bwd{tag}", f_conf_tail, [(s["ub"], 512, 0), (proj, 512, P_ZB // 512)], s["conf_consts"], [(dyb, 512, 0)],
        [F32, BF16], TM)
    dglu, ddw32, ddw_b = glu_conv_bwd(f"glu_conv_bwd{tag}", proj, p["dw32"], dub)
    dqkv, dab, dzc, dal, ddtb, ddng, *got_delta = delta_bwd(f"delta_bwd{tag}", s["qkv"], proj, s["ssave"], s["tsave"], p["al"],
                                                            p["dtb"], p["dng"], dyc, carry_delta)
    dqkv_pre, dsconv8 = sconv_bwd(f"sconv_bwd{tag}", proj, p["sconv8"], dqkv)
    dproj = jnp.concatenate([dqz, dglu, dzb, dzc, dmg, dqkv_pre, dkv, dab], axis=1)
    dwp = matmul_nn(f"inproj_bwd_dw{tag}", s["h"].T, dproj, F32, d, TN_IN, 2048)
    reduced = dict(w_in=_w_in_grad_blocks(dwp), pw2_w=dpw2_w, w_proj_a=_unperm_heads_rows(dwpa), w_proj_b=dwpb, w_proj_c=dwpc,
                   w_out=dwout)
    carry_dh = None if carry_dh is None else carry_dh(reduced)
    dh = matmul_nn(f"inproj_bwd_dh{tag}", dproj, p["wp"], F32, TM_IN, d, P_TOTAL // 3, b_transposed=True, carry=carry_dh)
    dh, got_dh = (dh, []) if carry_dh is None else (dh[0], dh[1:])
    dx, dnorm_g, dscale, dshift = rowwise_bwd(
        f"norm_bwd{tag}", f_norm_mod_res, [(s["x"], d, 0)], s["norm_consts"], [(dh, d, 0), (dxn, d, 0)], [F32], TM)
    dmod = jnp.concatenate([dshift, dscale, dgate], axis=1)
    grads = dict(
        reduced, b_ada=dmod[0], norm_g=dnorm_g[0],
        q_norm_g=dqg[0, :64] + dqg[0, 64:], k_norm_g=dkg[0, :64] + dkg[0, 64:], sinks=dsinks[0, :ATT_HEADS],
        dw_w=ddw32[:CONV_K], dw_b=ddw_b[0], ln_g=dln_g[0], ln_b=dln_b[0], pw2_b=dpw2_b[0],
        sconv_w=dsconv8[:DN_CONV_K], a_log=dal[0, :DN_HEADS], dt_bias=ddtb[0, :DN_HEADS], dn_norm_g=ddng[0])
    return dx, grads, got_merge, got_delta, got_dh


SHARDED = {"w_ada": 2, "w_in": 2, "dw_w": 2, "pw2_w": 1, "sconv_w": 2, "w_proj_a": 2, "w_proj_b": 2, "w_proj_c": 2,
           "w_out": 1}
GATHERED = tuple(n for n in SHARDED if n != "w_ada")
GATHER_F32 = ("dw_w", "sconv_w")
REDUCE_BIG = tuple(n for n in GATHERED if n not in GATHER_F32)
SMALL = ("b_ada", "norm_g", "q_norm_g", "k_norm_g", "sinks", "dw_b", "ln_g", "ln_b", "pw2_b", "a_log", "dt_bias",
         "dn_norm_g")
SMALL_ROWS = 104
SMALL_GRAD_ROWS = 448
W_IN_SHARD = D_IN // N_CHIPS
SUM_TILE = 256


def _w_in_orig():
    orig = np.full(P_TOTAL, -1, np.int64)
    p = 0
    for s, n in _in_pieces():
        orig[p:p + n] = np.arange(s, s + n)
        p += n
    return orig


def _w_in_blocks(k):
    orig = _w_in_orig().reshape(-1, 128)
    lo, hi = k * W_IN_SHARD, (k + 1) * W_IN_SHARD
    return [b for b in range(orig.shape[0]) if np.any((orig[b] >= lo) & (orig[b] < hi))]


W_IN_BLOCKS = max(len(_w_in_blocks(k)) for k in range(N_CHIPS))


def _runs(idx):
    out, i = [], 0
    while i < len(idx):
        j = i + 1
        while j < len(idx) and ((idx[i] < 0 and idx[j] < 0) or (idx[i] >= 0 and idx[j] == idx[j - 1] + 1)):
            j += 1
        out.append((int(idx[i]) if idx[i] >= 0 else -1, j - i))
        i = j
    return out


def _take_cols(a, idx):
    parts = [jnp.zeros(a.shape[:-1] + (n,), a.dtype) if s < 0 else a[..., s:s + n] for s, n in _runs(idx)]
    return parts[0] if len(parts) == 1 else jnp.concatenate(parts, axis=-1)


def _w_in_send(k, shard):
    orig = _w_in_orig().reshape(-1, 128)
    lo, hi = k * W_IN_SHARD, (k + 1) * W_IN_SHARD
    idx = np.concatenate([np.where((orig[b] >= lo) & (orig[b] < hi), orig[b] - lo, -1) for b in _w_in_blocks(k)])
    idx = np.concatenate([idx, np.full((W_IN_BLOCKS - len(_w_in_blocks(k))) * 128, -1)])
    return _take_cols(shard, idx)


def _w_in_assemble(blocks):
    where = [{b: i for i, b in enumerate(_w_in_blocks(k))} for k in range(N_CHIPS)]
    n_blocks = P_TOTAL // 128
    owners = [[(k, where[k][b]) for k in range(N_CHIPS) if b in where[k]] for b in range(n_blocks)]
    parts, b = [], 0
    while b < n_blocks:
        if len(owners[b]) == 1:
            k, pos = owners[b][0]
            e = b + 1
            while e < n_blocks and owners[e] == [(k, pos + e - b)]:
                e += 1
            parts.append(blocks[k][:, pos * 128:(pos + e - b) * 128])
            b = e
        else:
            parts.append(functools.reduce(jnp.add, [blocks[k][:, pos * 128:(pos + 1) * 128] for k, pos in owners[b]]))
            b += 1
    return jnp.concatenate(parts, axis=1)


def _w_in_grad_blocks(wp):
    out = []
    for k in range(N_CHIPS):
        idx = np.concatenate([np.arange(128 * b, 128 * b + 128) for b in _w_in_blocks(k)])
        idx = np.concatenate([idx, np.full((W_IN_BLOCKS - len(_w_in_blocks(k))) * 128, -1)])
        out.append(_take_cols(wp, idx))
    return jnp.stack(out)


def _w_in_receive_grad(k, blocks):
    orig = _w_in_orig()
    inv = np.zeros(D_IN, np.int64)
    inv[orig[orig >= 0]] = np.nonzero(orig >= 0)[0]
    where = {b: i for i, b in enumerate(_w_in_blocks(k))}
    cols = inv[k * W_IN_SHARD:(k + 1) * W_IN_SHARD]
    return _take_cols(blocks, np.array([where[c // 128] * 128 + c % 128 for c in cols]))


def _join_layer(v, axis):
    if axis == 2:
        return jnp.transpose(v, (1, 0, 2)).reshape(v.shape[1], N_CHIPS * v.shape[2])
    return v.reshape(N_CHIPS * v.shape[1], v.shape[2])


def _split_layer(v, axis):
    a, b = v.shape
    if axis == 2:
        return jnp.transpose(v.reshape(a, N_CHIPS, b // N_CHIPS), (1, 0, 2))
    return v.reshape(N_CHIPS, a // N_CHIPS, b)


def pack_small(vals, names, rows):
    flat = jnp.concatenate([vals[n].astype(F32).reshape(-1) for n in names])
    return jnp.pad(flat, (0, rows * 128 - flat.shape[0])).reshape(rows, 128)


def unpack_small(packed, names, shapes):
    flat = packed.reshape(-1)
    out, off = {}, 0
    for n in names:
        k = int(np.prod(shapes[n]))
        out[n] = flat[off:off + k].reshape(shapes[n])
        off += k
    return out


ANY = pl.BlockSpec(memory_space=pl.ANY)


def _place():
    x, y, c = lax.axis_index("x"), lax.axis_index("y"), lax.axis_index("c")
    chips = [(1 - x, y), (x, 1 - y), (1 - x, 1 - y)]
    return x, y, c, chips


def _remote(src, dst, send_sem, recv_sem, to):
    return pltpu.make_async_remote_copy(src_ref=src, dst_ref=dst, send_sem=send_sem, recv_sem=recv_sem, device_id=to,
                                        device_id_type=MESH)


class Carry:
    def __init__(self, ins, out_shapes, sems, start, finish, in_place=False):
        self.ins, self.out_shapes, self.sems, self.start, self.finish, self.in_place = (
            list(ins), list(out_shapes), sems, start, finish, in_place)

    def scratch(self):
        return [pltpu.SemaphoreType.DMA(self.sems), pltpu.SemaphoreType.DMA(self.sems)]

    def aliases(self, first_in, first_out):
        return {first_in + i: first_out + i for i in range(len(self.ins))} if self.in_place else {}

    def emit_start(self, first, in_refs, out_refs, send_sems, recv_sems):
        @pl.when(first)
        def _():
            self.start(in_refs, out_refs, send_sems, recv_sems)

    def emit_finish(self, last, in_refs, out_refs, send_sems, recv_sems):
        @pl.when(last)
        def _():
            self.finish(in_refs, out_refs, send_sems, recv_sems)


def _host(carry):
    if carry is None:
        return [], [], [], [], []
    return carry.ins, [ANY] * len(carry.ins), carry.out_shapes, [ANY] * len(carry.out_shapes), carry.scratch()


def run_carry(name, carry):
    n_in, n_out = len(carry.ins), len(carry.out_shapes)

    def body(*refs):
        ins, outs, sems = refs[:n_in], refs[n_in:n_in + n_out], refs[n_in + n_out:]
        carry.start(ins, outs, *sems)
        carry.finish(ins, outs, *sems)

    return pl.pallas_call(
        body, name=name, out_shape=carry.out_shapes, in_specs=[ANY] * n_in, out_specs=[ANY] * n_out,
        input_output_aliases=carry.aliases(0, 0), scratch_shapes=carry.scratch(),
    )(*carry.ins)


def carry_allgather(layer, slots):
    n = len(slots)

    def copies(out, send_sems, recv_sems, only_ici_out=False):
        x, y, c, chips = _place()
        ici_out, ici_in, d2d_out, d2d_in = [], [], [], []
        for j, chip in enumerate(chips):
            for t in range(n):
                mine, land = out[t].at[2 * x + y], out[t].at[2 * chip[0] + chip[1]]
                ici_out.append(_remote(mine, mine, send_sems.at[t, j], recv_sems.at[t, j], (*chip, layer)))
                if only_ici_out:
                    continue
                ici_in.append(_remote(land, land, send_sems.at[t, j], recv_sems.at[t, j], (*chip, layer)))
                d2d_out.append(_remote(land, land, send_sems.at[t, 3 + j], recv_sems.at[t, 3 + j], (x, y, 1 - layer)))
                d2d_in.append(_remote(land, land, send_sems.at[t, 3 + j], recv_sems.at[t, 3 + j], (x, y, layer)))
        return c, ici_out, ici_in, d2d_out, d2d_in

    def start(ins, out, send_sems, recv_sems):
        c, ici_out, _, _, _ = copies(out, send_sems, recv_sems, only_ici_out=True)

        @pl.when(c == layer)
        def _():
            for cp in ici_out:
                cp.start()

    def finish(ins, out, send_sems, recv_sems):
        c, ici_out, ici_in, d2d_out, d2d_in = copies(out, send_sems, recv_sems)

        @pl.when(c == layer)
        def _():
            for arrived, onward in zip(ici_in, d2d_out):
                arrived.wait_recv()
                onward.start()
            for cp in ici_out + d2d_out:
                cp.wait_send()

        @pl.when(c != layer)
        def _():
            for cp in d2d_in:
                cp.wait_recv()

    return Carry(slots, [jax.ShapeDtypeStruct(s.shape, s.dtype) for s in slots], (n, 6), start, finish, in_place=True)


def carry_pair_send(layer, gs):
    def copies(g, recv, send_sems, recv_sems):
        x, y, c, _ = _place()
        return c, [_remote(g[t], recv[t], send_sems.at[t], recv_sems.at[t], (x, y, 1 - c)) for t in range(len(gs))]

    def start(g, recv, send_sems, recv_sems):
        c, cps = copies(g, recv, send_sems, recv_sems)

        @pl.when(c != layer)
        def _():
            for cp in cps:
                cp.start()

    def finish(g, recv, send_sems, recv_sems):
        c, cps = copies(g, recv, send_sems, recv_sems)

        @pl.when(c != layer)
        def _():
            for cp in cps:
                cp.wait_send()

        @pl.when(c == layer)
        def _():
            for cp in cps:
                cp.wait_recv()

    return Carry(gs, [jax.ShapeDtypeStruct(g.shape, g.dtype) for g in gs], (len(gs),), start, finish)


def grads_pair_sum(name, g, recv):
    _, a, b = recv.shape
    ta = min(a, SUM_TILE)

    def body(a_ref, b_ref, o_ref):
        o_ref[...] = (a_ref[...] + b_ref[...]).astype(o_ref.dtype)

    spec = pl.BlockSpec((None, ta, b), lambda s, i: (s, i, 0))
    return pl.pallas_call(
        body, name=name, grid=(N_CHIPS, a // ta), in_specs=[spec, spec], out_specs=spec,
        out_shape=jax.ShapeDtypeStruct(recv.shape, BF16), compiler_params=_cparams(2),
    )(g, recv)


def carry_chip_exchange(layer, ps):
    def copies(p, recv, send_sems, recv_sems):
        _, _, c, chips = _place()
        return c, [_remote(p[t].at[2 * chip[0] + chip[1]], recv[t].at[j], send_sems.at[t, j], recv_sems.at[t, j],
                           (*chip, layer)) for j, chip in enumerate(chips) for t in range(len(ps))]

    def start(p, recv, send_sems, recv_sems):
        c, cps = copies(p, recv, send_sems, recv_sems)

        @pl.when(c == layer)
        def _():
            for cp in cps:
                cp.start()

    def finish(p, recv, send_sems, recv_sems):
        c, cps = copies(p, recv, send_sems, recv_sems)

        @pl.when(c == layer)
        def _():
            for cp in cps:
                cp.wait()

    return Carry(ps, [jax.ShapeDtypeStruct((3,) + p.shape[1:], p.dtype) for p in ps], (len(ps), 3), start, finish)


def grads_chip_sum(name, layer, g, recv, recv2, into=None):
    _, a, b = recv.shape
    ta = min(a, SUM_TILE)
    my_slot = lambda: 2 * lax.axis_index("x") + lax.axis_index("y")

    def body(g_ref, r_ref, r2_ref, *rest):
        o_ref = rest[-1]
        own = g_ref[...] + r_ref[...]
        o_ref[...] = ((own + r2_ref[0].astype(F32)) + r2_ref[1].astype(F32)) + r2_ref[2].astype(F32)

    own_spec = pl.BlockSpec((None, ta, b), lambda i: (my_slot(), i, 0))
    return pl.pallas_call(
        body, name=name, grid=(a // ta,),
        in_specs=[own_spec, own_spec, pl.BlockSpec((3, ta, b), lambda i: (0, i, 0))] + ([] if into is None else [ANY]),
        out_specs=pl.BlockSpec((None, ta, b), lambda i: (layer, i, 0)),
        out_shape=jax.ShapeDtypeStruct((DEPTH, a, b), F32),
        input_output_aliases={} if into is None else {3: 0},
        compiler_params=_cparams(1),
    )(g, recv, recv2, *([] if into is None else [into]))


def grads_pair_gather(reds):
    n = len(reds)

    def body(*refs):
        buf = refs[n:2 * n]
        send_sems, recv_sems = refs[2 * n:]
        x, y, c, _ = _place()
        sibling = (x, y, 1 - c)
        cps = [_remote(buf[t].at[c], buf[t].at[c], send_sems.at[t], recv_sems.at[t], sibling) for t in range(n)]
        for cp in cps:
            cp.start()
        for t in range(n):
            _remote(buf[t].at[c], buf[t].at[1 - c], send_sems.at[t], recv_sems.at[t], sibling).wait_recv()
        for cp in cps:
            cp.wait_send()

    return pl.pallas_call(
        body, name="grads_pair_gather", out_shape=[jax.ShapeDtypeStruct(r.shape, r.dtype) for r in reds],
        in_specs=[ANY] * n, out_specs=[ANY] * n, input_output_aliases={t: t for t in range(n)},
        scratch_shapes=[pltpu.SemaphoreType.DMA((n,)), pltpu.SemaphoreType.DMA((n,))],
    )(*reds)


def small_allreduce(v):
    m, n = v.shape

    def body(x_ref, sum_ref, all_ref, send_sems, recv_sems, local_sem):
        x, y, c, chips = _place()
        me, sibling = (x, y, c), (x, y, 1 - c)

        def rows(px, py, pc):
            return all_ref.at[pl.ds((4 * px + 2 * py + pc) * m, m), :]

        def copy(k, block, to, src=None):
            return pltpu.make_async_remote_copy(src_ref=rows(*block) if src is None else src, dst_ref=rows(*block),
                                                send_sem=send_sems.at[k], recv_sem=recv_sems.at[k],
                                                device_id=to, device_id_type=MESH)

        mine = pltpu.make_async_copy(x_ref, rows(*me), local_sem)
        mine.start()
        first = [copy(0, me, sibling, src=x_ref)]
        first += [copy(1 + j, me, (*chip, c), src=x_ref) for j, chip in enumerate(chips)]
        for cp in first:
            cp.start()
        passed = [copy(4 + j, (*chip, c), sibling) for j, chip in enumerate(chips)]
        for j, chip in enumerate(chips):
            copy(1 + j, (*chip, c), me).wait_recv()
            passed[j].start()
        copy(0, sibling, me).wait_recv()
        for j, chip in enumerate(chips):
            copy(4 + j, (*chip, 1 - c), me).wait_recv()
        for cp in first + passed:
            cp.wait_send()
        mine.wait()
        acc = all_ref[0:m, :]
        for dev in range(1, 8):
            acc = acc + all_ref[dev * m:(dev + 1) * m, :]
        sum_ref[...] = acc

    vm = pl.BlockSpec(memory_space=pltpu.VMEM)
    return pl.pallas_call(
        body, name="small_allreduce",
        out_shape=[jax.ShapeDtypeStruct((m, n), F32), jax.ShapeDtypeStruct((8 * m, n), F32)],
        in_specs=[vm], out_specs=[vm, vm],
        scratch_shapes=[pltpu.SemaphoreType.DMA((7,)), pltpu.SemaphoreType.DMA((7,)), pltpu.SemaphoreType.DMA],
    )(v)


def grads_by_chip(layer_grads):
    return [layer_grads[n] if n == "w_in" else _split_layer(layer_grads[n], SHARDED[n]) for n in REDUCE_BIG]


def grads_pair_sums(layer, gs, recv):
    return [grads_pair_sum(f"grads_pair_sum{layer}_{n}", g, r) for n, g, r in zip(REDUCE_BIG, gs, recv)]


def adamw(name, w, g, m, v, block):
    grid = tuple(s // b for s, b in zip(w.shape, block))

    def body(w_ref, g_ref, m_ref, v_ref, d_ref, nm_ref, nv_ref):
        gv = g_ref[...]
        nm = ADAM_B1 * m_ref[...] + (1.0 - ADAM_B1) * gv
        nv = ADAM_B2 * v_ref[...] + (1.0 - ADAM_B2) * (gv * gv)
        m_hat = nm / (1.0 - ADAM_B1 ** ADAM_STEP)
        v_hat = nv / (1.0 - ADAM_B2 ** ADAM_STEP)
        d_ref[...] = -ADAM_LR * (m_hat / (jnp.sqrt(v_hat) + ADAM_EPS) + ADAM_WD * w_ref[...])
        nm_ref[...] = nm
        nv_ref[...] = nv

    spec = pl.BlockSpec(tuple(block), lambda *idx: idx)
    return pl.pallas_call(
        body, name=name, grid=grid, in_specs=[spec] * 4, out_specs=[spec] * 3,
        out_shape=[jax.ShapeDtypeStruct(w.shape, F32)] * 3, compiler_params=_cparams(len(grid)),
    )(w, g, m, v)


ADAM_ROWS = {"w_ada": 512, "dw_w": 62, "pw2_w": 256, "sconv_w": 8, "w_proj_a": 512, "w_proj_b": 512, "w_proj_c": 512,
             "w_out": 256}
ADAM_W_IN_COLS = 331

WEIGHT_NAMES = ("w_ada", "b_ada", "norm_g", "w_in", "q_norm_g", "k_norm_g", "sinks", "dw_w", "dw_b", "ln_g", "ln_b",
                "pw2_w", "pw2_b", "sconv_w", "a_log", "dt_bias", "dn_norm_g", "w_proj_a", "w_proj_b", "w_proj_c", "w_out")


def kernel(x, c, w_ada, b_ada, norm_g, w_in, q_norm_g, k_norm_g, sinks, dw_w, dw_b, ln_g, ln_b, pw2_w, pw2_b, sconv_w, a_log, dt_bias, dn_norm_g, w_proj_a, w_proj_b, w_proj_c, w_out, loss_target, m_w_ada, m_b_ada, m_norm_g, m_w_in, m_q_norm_g, m_k_norm_g, m_sinks, m_dw_w, m_dw_b, m_ln_g, m_ln_b, m_pw2_w, m_pw2_b, m_sconv_w, m_a_log, m_dt_bias, m_dn_norm_g, m_w_proj_a, m_w_proj_b, m_w_proj_c, m_w_out, v_w_ada, v_b_ada, v_norm_g, v_w_in, v_q_norm_g, v_k_norm_g, v_sinks, v_dw_w, v_dw_b, v_ln_g, v_ln_b, v_pw2_w, v_pw2_b, v_sconv_w, v_a_log, v_dt_bias, v_dn_norm_g, v_w_proj_a, v_w_proj_b, v_w_proj_c, v_w_out):
    args = dict(locals())
    w = {n: args[n] for n in WEIGHT_NAMES}
    mom = {n: args["m_" + n] for n in WEIGHT_NAMES}
    var = {n: args["v_" + n] for n in WEIGHT_NAMES}

    chip = 2 * lax.axis_index("x") + lax.axis_index("y")
    own = {n: w[n] if n in GATHER_F32 else w[n].astype(BF16) for n in GATHERED}
    own["w_in"] = lax.switch(chip, [functools.partial(_w_in_send, k) for k in range(N_CHIPS)], own["w_in"])
    slots = [[lax.dynamic_update_slice(lax.empty((N_CHIPS,) + own[n].shape[1:], own[n].dtype), own[n][l][None], (chip, 0, 0))
              for n in GATHERED] for l in range(DEPTH)]

    def layer_operands(l, gathered):
        lw = {n: w[n][l] for n in SMALL}
        lw.update({n: g if n == "w_in" else _join_layer(g, SHARDED[n]) for n, g in zip(GATHERED, gathered)})
        return prep_layer(lw)

    layers = [layer_operands(0, run_carry("weights_allgather0", carry_allgather(0, slots[0]))), None]

    mod, conds = ada_fwd(jnp.tile(c, (8, 1)), w["w_ada"], w["b_ada"])
    saved = [None] * DEPTH
    big = GATHERED.index("w_in")
    rest = [i for i in range(len(GATHERED)) if i != big]
    act, saved[0], got_big, got_rest = layer_fwd(
        "0", x[0], mod[0:1], layers[0], carry_inproj=carry_allgather(1, [slots[1][big]]),
        carry_merge=carry_allgather(1, [slots[1][i] for i in rest]))
    gathered1 = dict(zip(rest, got_rest))
    gathered1[big] = got_big[0]
    layers[1] = layer_operands(1, [gathered1[i] for i in range(len(GATHERED))])
    act, saved[1], _, _ = layer_fwd("1", act, mod[1:2], layers[1])
    dact, loss_part = loss_head("loss_head", act, loss_target[0], TM)
    loss = lax.psum(loss_part[0, 0], ("x", "y", "c"))
    layer_grads = [None] * DEPTH
    dact, layer_grads[1], _, _, _ = layer_bwd("1", dact, layers[1], saved[1])
    gs1 = grads_by_chip(layer_grads[1])
    gs0 = []

    def hand_over_layer0(reduced):
        gs0.extend(grads_by_chip(reduced))
        return carry_pair_send(0, gs0)

    dact, layer_grads[0], recv1, got1, recv0 = layer_bwd(
        "0", dact, layers[0], saved[0], carry_merge=carry_pair_send(1, gs1),
        carry_delta=lambda recv: carry_chip_exchange(1, grads_pair_sums(1, gs1, recv)), carry_dh=hand_over_layer0)

    got0 = run_carry("grads_chip_exchange0", carry_chip_exchange(0, grads_pair_sums(0, gs0, recv0)))
    reds = [grads_chip_sum(f"grads_chip_sum1_{n}", 1, g, r, r2) for n, g, r, r2 in zip(REDUCE_BIG, gs1, recv1, got1)]
    reds = [grads_chip_sum(f"grads_chip_sum0_{n}", 0, g, r, r2, into=red)
            for n, g, r, r2, red in zip(REDUCE_BIG, gs0, recv0, got0, reds)]
    final_grads = dict(zip(REDUCE_BIG, grads_pair_gather(reds)))
    final_grads["w_in"] = lax.switch(chip, [functools.partial(_w_in_receive_grad, k) for k in range(N_CHIPS)],
                                     final_grads["w_in"])
    small_names = SMALL + GATHER_F32
    small_shapes = {n: (DEPTH,) + layer_grads[0][n].shape for n in small_names}
    small_full = {n: jnp.stack([layer_grads[l][n] for l in range(DEPTH)]) for n in small_names}
    small_sum, small_all = small_allreduce(pack_small(small_full, small_names, SMALL_GRAD_ROWS))
    small_sum = unpack_small(small_sum, small_names, small_shapes)
    for n in GATHER_F32:
        width = w[n].shape[2]
        final_grads[n] = lax.dynamic_slice_in_dim(small_sum[n], chip * width, width, axis=2)
    n_mod = DEPTH * 3 * D_MODEL
    dmod = small_all.reshape(8, -1)[:, :n_mod].reshape(8, DEPTH, 3 * D_MODEL)
    width = w["w_ada"].shape[2]
    dmod = jnp.transpose(lax.dynamic_slice_in_dim(dmod, chip * width, width, axis=2), (1, 0, 2))
    final_grads["w_ada"] = ada_bwd(conds, dmod)
    final_grads.update({n: small_sum[n] for n in SMALL})
    small_grads = pack_small(final_grads, SMALL, SMALL_ROWS)

    delta, new_m, new_v = {}, {}, {}
    for n in SHARDED:
        shp = w[n].shape
        if n == "w_in":
            view = lambda a: jnp.transpose(a, (2, 0, 1))
            back = lambda a: jnp.transpose(a, (1, 2, 0))
            g3 = view(final_grads[n])
            final_grads[n] = back(g3)
            d, nm, nv = adamw("adamw_" + n, view(w[n]), g3, view(mom[n]), view(var[n]), (ADAM_W_IN_COLS, shp[0], shp[1]))
        else:
            view = lambda a, shp=shp: a.reshape(shp[0] * shp[1], shp[2])
            back = lambda a, shp=shp: a.reshape(shp)
            d, nm, nv = adamw("adamw_" + n, view(w[n]), view(final_grads[n]), view(mom[n]), view(var[n]),
                              (ADAM_ROWS[n], shp[2]))
        delta[n], new_m[n], new_v[n] = back(d), back(nm), back(nv)
    d, nm, nv = adamw("adamw_small", pack_small(w, SMALL, SMALL_ROWS), small_grads, pack_small(mom, SMALL, SMALL_ROWS),
                      pack_small(var, SMALL, SMALL_ROWS), (SMALL_ROWS, 128))
    delta.update(unpack_small(d, SMALL, small_shapes))
    new_m.update(unpack_small(nm, SMALL, small_shapes))
    new_v.update(unpack_small(nv, SMALL, small_shapes))

    return (loss, dact[None], *[final_grads[n] for n in WEIGHT_NAMES], *[delta[n] for n in WEIGHT_NAMES],
            *[new_m[n] for n in WEIGHT_NAMES], *[new_v[n] for n in WEIGHT_NAMES])
```

```python
import functools

import numpy as np
import jax
import jax.numpy as jnp
from jax import lax
from jax.experimental import pallas as pl
from jax.experimental.pallas import tpu as pltpu

F32 = jnp.float32
BF16 = jnp.bfloat16
MESH = pl.DeviceIdType.MESH

D_MODEL = 1024
DEPTH = 2
ATT_HEADS = 8
ATT_HEAD_DIM = 64
WINDOW = 128
CONV_K = 31
DN_HEADS = 4
DN_CONV_K = 4
DN_CHUNK = 64
EPS = 1e-6
NEG_INF = -1e30
N_CHIPS = 4
D_IN = 7944

ADAM_LR = 0.001
ADAM_B1 = 0.9
ADAM_B2 = 0.999
ADAM_EPS = 1e-08
ADAM_WD = 0.01
ADAM_STEP = 10

VMEM_LIMIT = 56 * 1024 * 1024

P_QA, P_ZA, P_GLU, P_ZB, P_ZC, P_MG, P_QKV, P_KA, P_VA, P_AB, P_TOTAL = (
    0, 512, 1024, 2048, 2560, 3072, 6144, 7680, 7808, 7936, 8064)
HEAD_ORDER = (0, 4, 1, 5, 2, 6, 3, 7)


def _in_pieces():
    p = [(0 + 64 * h, 64) for h in HEAD_ORDER]
    p += [(768 + 64 * h, 64) for h in HEAD_ORDER]
    for g in range(4):
        p += [(1280 + 128 * g, 128), (1792 + 128 * g, 128)]
    p += [(2304, 512), (4360, 512), (4872, 3072), (2816, 1536), (512, 128), (640, 128), (4352, 8)]
    return p


def _perm_heads_rows(w):
    return jnp.concatenate([w[64 * h:64 * h + 64] for h in HEAD_ORDER], axis=0)


def _unperm_heads_rows(w):
    inv = [HEAD_ORDER.index(h) for h in range(8)]
    return jnp.concatenate([w[64 * s:64 * s + 64] for s in inv], axis=0)


def _split_bf16(a, terms):
    out, rest = [], a.astype(F32)
    for _ in range(terms - 1):
        out.append(rest.astype(BF16))
        rest = rest - out[-1].astype(F32)
    return out + [rest.astype(BF16)]


def _dot(a, b, dims, exact):
    d = lambda p, q: lax.dot_general(p, q, (dims, ((), ())), preferred_element_type=F32)
    if exact:
        (ah, al), (bh, bl) = _split_bf16(a, 2), _split_bf16(b, 2)
        return d(ah, bh) + (d(ah, bl) + d(al, bh))
    return d(a.astype(BF16), b.astype(BF16))


def _make_mm(exact):
    @jax.custom_vjp
    def nn(a, b):
        return _dot(a, b, ((1,), (0,)), exact)

    @jax.custom_vjp
    def nt(a, b):
        return _dot(a, b, ((1,), (1,)), exact)

    @jax.custom_vjp
    def tn(a, b):
        return _dot(a, b, ((0,), (0,)), exact)

    nn.defvjp(lambda a, b: (nn(a, b), (a, b)),
              lambda r, g: (nt(g, r[1]).astype(r[0].dtype), tn(r[0], g).astype(r[1].dtype)))
    nt.defvjp(lambda a, b: (nt(a, b), (a, b)),
              lambda r, g: (nn(g, r[1]).astype(r[0].dtype), tn(g, r[0]).astype(r[1].dtype)))
    tn.defvjp(lambda a, b: (tn(a, b), (a, b)),
              lambda r, g: (nt(r[1], g).astype(r[0].dtype), nn(r[0], g).astype(r[1].dtype)))
    return nn, nt, tn


mm, mm_nt, mm_tn = _make_mm(False)
xmm, xmm_nt, xmm_tn = _make_mm(True)


@jax.custom_vjp
def sel_mm(m, g):
    mb = m.astype(BF16)
    parts = [jnp.dot(mb, p, preferred_element_type=F32) for p in _split_bf16(g, 3)]
    return parts[0] + (parts[1] + parts[2])


def _sel_mm_bwd(m, dy):
    mb = m.astype(BF16)
    parts = [lax.dot_general(mb, p, (((0,), (0,)), ((), ())), preferred_element_type=F32) for p in _split_bf16(dy, 3)]
    return jnp.zeros_like(m), parts[0] + (parts[1] + parts[2])


sel_mm.defvjp(lambda m, g: (sel_mm(m, g), m), _sel_mm_bwd)


@jax.custom_vjp
def tri_inv(*mats):
    n = mats[0].shape[0]
    eye = jnp.where(lax.broadcasted_iota(jnp.int32, (n, n), 0) == lax.broadcasted_iota(jnp.int32, (n, n), 1), 1.0, 0.0)
    ts = [eye - a for a in mats]
    pws = list(mats)
    for _ in range(5):
        pws = [xmm(pw, pw) for pw in pws]
        ts = [t + xmm(t, pw) for t, pw in zip(ts, pws)]
    return tuple(ts)


def _tri_inv_bwd(ts, dts):
    inner = [xmm_nt(dt, t) for t, dt in zip(ts, dts)]
    return tuple(-xmm_tn(t, m) for t, m in zip(ts, inner))


tri_inv.defvjp(lambda *mats: (tri_inv(*mats),) * 2, _tri_inv_bwd)


@jax.custom_vjp
def tri_inv_known(a, t):
    return t


tri_inv_known.defvjp(lambda a, t: (t, t), lambda t, dt: (_tri_inv_bwd((t,), (dt,))[0], jnp.zeros_like(t)))


def _sigmoid(x):
    return 1.0 / (1.0 + jnp.exp(-x))


def _silu(x):
    return x * _sigmoid(x)


def _softplus(x):
    return jnp.maximum(x, 0.0) + jnp.log(1.0 + jnp.exp(-jnp.abs(x)))


def _cparams(n_grid):
    return pltpu.CompilerParams(dimension_semantics=("arbitrary",) * n_grid, vmem_limit_bytes=VMEM_LIMIT)


def _row_spec(tm, width, colblk):
    return pl.BlockSpec((tm, width), lambda i, cb=colblk: (i, cb))


def _const_spec(shape):
    nd = len(shape)
    return pl.BlockSpec(tuple(shape), lambda i, nd=nd: (0,) * nd)


def rowwise_fwd(name, f, rows, consts, outs, tm, carry=None):
    n_r, n_c = len(rows), len(consts)
    t = rows[0][0].shape[0]
    c_ins, c_in_specs, c_outs, c_out_specs, c_scratch = _host(carry)
    n_in, n_ci, n_co = n_r + n_c, len(c_ins), len(c_outs)

    def body(*refs):
        carried = (refs[n_in:n_in + n_ci], refs[n_in + n_ci + len(outs):n_in + n_ci + len(outs) + n_co],
                   *refs[n_in + n_ci + len(outs) + n_co:])
        if carry is not None:
            carry.emit_start(pl.program_id(0) == 0, *carried)
        vals = [r[...] for r in refs[:n_in]]
        res = f(*vals)
        if not isinstance(res, (tuple, list)):
            res = (res,)
        for o_ref, v, out in zip(refs[n_in + n_ci:n_in + n_ci + len(outs)], res, outs):
            o_ref[...] = (v.T if len(out) == 3 else v).astype(o_ref.dtype)
        if carry is not None:
            carry.emit_finish(pl.program_id(0) == t // tm - 1, *carried)

    return pl.pallas_call(
        body, name=name, grid=(t // tm,),
        in_specs=[_row_spec(tm, w, cb) for _, w, cb in rows] + [_const_spec(c.shape) for c in consts] + c_in_specs,
        out_specs=[_row_spec(tm, o[0], 0) if len(o) == 2 else pl.BlockSpec((o[0], tm), lambda i: (0, i)) for o in outs]
        + c_out_specs,
        out_shape=[jax.ShapeDtypeStruct((t, o[0]) if len(o) == 2 else (o[0], t), o[1]) for o in outs] + c_outs,
        input_output_aliases={} if carry is None else carry.aliases(n_in, len(outs)),
        scratch_shapes=c_scratch,
        compiler_params=_cparams(1),
    )(*[a for a, _, _ in rows], *consts, *c_ins)


def rowwise_bwd(name, f, rows, consts, cts, row_grad_dtypes, tm, carry=None):
    n_r, n_c, n_ct = len(rows), len(consts), len(cts)
    t = rows[0][0].shape[0]
    keep = [k for k, dt in enumerate(row_grad_dtypes) if dt is not None]
    c_ins, c_in_specs, c_outs, c_out_specs, c_scratch = _host(carry)
    n_in, n_out = n_r + n_c + n_ct, len(keep) + n_c

    def body(*refs):
        ins = [r[...].astype(F32) for r in refs[:n_r + n_c]]
        g_out = [r[...].astype(F32) for r in refs[n_r + n_c:n_in]]
        out_refs = refs[n_in + len(c_ins):n_in + len(c_ins) + n_out]
        carried = (refs[n_in:n_in + len(c_ins)], refs[n_in + len(c_ins) + n_out:n_in + len(c_ins) + n_out + len(c_outs)],
                   *refs[n_in + len(c_ins) + n_out + len(c_outs):])
        if carry is not None:
            carry.emit_start(pl.program_id(0) == 0, *carried)

        def fw(*a):
            res = f(*a)
            return tuple(res) if isinstance(res, (tuple, list)) else (res,)

        _, vjp = jax.vjp(fw, *ins)
        grads = vjp(tuple(g_out))
        for o_ref, k in zip(out_refs[:len(keep)], keep):
            o_ref[...] = grads[k].astype(o_ref.dtype)
        first = pl.program_id(0) == 0
        for o_ref, g in zip(out_refs[len(keep):], grads[n_r:]):
            @pl.when(first)
            def _(o_ref=o_ref, g=g):
                o_ref[...] = g

            @pl.when(jnp.logical_not(first))
            def _(o_ref=o_ref, g=g):
                o_ref[...] += g
        if carry is not None:
            carry.emit_finish(pl.program_id(0) == t // tm - 1, *carried)

    return pl.pallas_call(
        body, name=name, grid=(t // tm,),
        in_specs=[_row_spec(tm, w, cb) for _, w, cb in rows] + [_const_spec(c.shape) for c in consts]
        + [_row_spec(tm, w, cb) for _, w, cb in cts] + c_in_specs,
        out_specs=[_row_spec(tm, rows[k][1], 0) for k in keep] + [_const_spec(c.shape) for c in consts] + c_out_specs,
        out_shape=[jax.ShapeDtypeStruct((t, rows[k][1]), row_grad_dtypes[k]) for k in keep]
        + [jax.ShapeDtypeStruct(c.shape, F32) for c in consts] + c_outs,
        scratch_shapes=c_scratch,
        compiler_params=_cparams(1),
    )(*[a for a, _, _ in rows], *consts, *[a for a, _, _ in cts], *c_ins)


def f_norm_mod(x, g, scale, shift):
    y = x * lax.rsqrt(jnp.mean(x * x, axis=-1, keepdims=True) + EPS) * g
    return y * (1.0 + scale) + shift


def f_conf_tail(u, zb, ln_g, ln_b, pw2_w, pw2_b):
    mu = jnp.mean(u, axis=-1, keepdims=True)
    xc = u - mu
    var = jnp.mean(xc * xc, axis=-1, keepdims=True)
    y = _silu(xc * lax.rsqrt(var + EPS) * ln_g + ln_b)
    return (mm(y, pw2_w) + pw2_b) * _silu(zb)


def f_merge(ya, yb, yc, mg, x, gate, wpa, wpb, wpc, wout):
    d = D_MODEL
    merged = (_sigmoid(mg[:, :d]) * mm(ya, wpa) + _sigmoid(mg[:, d:2 * d]) * mm(yb, wpb)
              + _sigmoid(mg[:, 2 * d:]) * mm(yc, wpc))
    return x + gate * mm(merged, wout)


def matmul_nn(name, a, b, out_dtype, tm, tn, tk, b_transposed=False, carry=None):
    m, k = a.shape
    n = b.shape[0] if b_transposed else b.shape[1]
    nk = k // tk
    grid = (m // tm, n // tn, nk)
    b_spec = (pl.BlockSpec((tn, tk), lambda i, j, kk: (j, kk)) if b_transposed
              else pl.BlockSpec((tk, tn), lambda i, j, kk: (kk, j)))
    c_ins, c_in_specs, c_outs, c_out_specs, c_scratch = _host(carry)
    n_ci, n_co = len(c_ins), len(c_outs)

    def body(*refs):
        a_ref, b_ref, o_ref = refs[0], refs[1], refs[2 + n_ci]
        carried = (refs[2:2 + n_ci], refs[3 + n_ci:3 + n_ci + n_co], *refs[3 + n_ci + n_co:3 + n_ci + n_co + len(c_scratch)])
        at = lambda step: functools.reduce(jnp.logical_and, [pl.program_id(d) == s for d, s in enumerate(step)])
        if carry is not None:
            carry.emit_start(at((0, 0, 0)), *carried)
        part = lax.dot_general(a_ref[...].astype(BF16), b_ref[...].astype(BF16),
                               (((1,), (1 if b_transposed else 0,)), ((), ())), preferred_element_type=F32)
        if nk == 1:
            o_ref[...] = part.astype(o_ref.dtype)
        else:
            kk = pl.program_id(2)
            acc_ref = refs[-1]

            @pl.when(kk == 0)
            def _():
                acc_ref[...] = part

            @pl.when(kk > 0)
            def _():
                acc_ref[...] += part

            @pl.when(kk == nk - 1)
            def _():
                o_ref[...] = acc_ref[...].astype(o_ref.dtype)
        if carry is not None:
            carry.emit_finish(at(tuple(g - 1 for g in grid)), *carried)

    res = pl.pallas_call(
        body, name=name, grid=grid,
        in_specs=[pl.BlockSpec((tm, tk), lambda i, j, kk: (i, kk)), b_spec] + c_in_specs,
        out_specs=[pl.BlockSpec((tm, tn), lambda i, j, kk: (i, j))] + c_out_specs,
        out_shape=[jax.ShapeDtypeStruct((m, n), out_dtype)] + c_outs,
        input_output_aliases={} if carry is None else carry.aliases(2, 1),
        scratch_shapes=c_scratch + ([] if nk == 1 else [pltpu.VMEM((tm, tn), F32)]),
        compiler_params=_cparams(3),
    )(a, b, *c_ins)
    return res[0] if carry is None else res


def ada_fwd(c8, w_shard, b_ada):
    n_cols = w_shard.shape[2]
    masks = [(m >> 2 & 1, m >> 1 & 1, m & 1) for m in range(1, 8)]

    def body(c_ref, w_ref, b_ref, mod_ref, conds_ref, cbuf, sendbuf, recvbuf, send_sems, recv_sems):
        x, y, c, chips = _place()
        flip = lambda v, bit: 1 - v if bit else v
        peers = [(flip(x, mx), flip(y, my), flip(c, mc)) for mx, my, mc in masks]
        dev = lambda p: 4 * p[0] + 2 * p[1] + p[2]
        cbuf[dev((x, y, c))] = c_ref[...]
        first = [_remote(c_ref, cbuf.at[dev((x, y, c))], send_sems.at[i], recv_sems.at[i], p) for i, p in enumerate(peers)]
        for cp in first:
            cp.start()
        for i, p in enumerate(peers):
            _remote(c_ref, cbuf.at[dev(p)], send_sems.at[i], recv_sems.at[i], p).wait_recv()
        conds = jnp.concatenate([cbuf[d, 0:1, :] for d in range(8)], axis=0)
        conds_ref[...] = conds
        act = _silu(conds)
        parts = [mm(act, w_ref[l]) for l in range(DEPTH)]
        row8 = lax.broadcasted_iota(jnp.int32, (8, 1), 0)

        def tile_for(chip):
            r = 2 * (2 * chip[0] + chip[1]) + c
            rows = [jnp.sum(jnp.where(row8 == r, parts[l], 0.0), axis=0, keepdims=True) for l in range(DEPTH)]
            return jnp.where(row8 == 0, rows[0], jnp.where(row8 == 1, rows[1], 0.0))

        my_slot = 2 * x + y
        recvbuf[my_slot] = tile_for((x, y))
        second = []
        for j, chip in enumerate(chips):
            sendbuf[j] = tile_for(chip)
            second.append(_remote(sendbuf.at[j], recvbuf.at[my_slot], send_sems.at[7 + j], recv_sems.at[7 + j], (*chip, c)))
            second[-1].start()
        for j, chip in enumerate(chips):
            _remote(sendbuf.at[j], recvbuf.at[2 * chip[0] + chip[1]], send_sems.at[7 + j], recv_sems.at[7 + j],
                    (*chip, c)).wait_recv()
        rows = [jnp.concatenate([recvbuf[k, l:l + 1, :] for k in range(N_CHIPS)], axis=1) + b_ref[l:l + 1, :]
                for l in range(DEPTH)]
        mod_ref[...] = jnp.concatenate(rows + [jnp.zeros((8 - DEPTH, N_CHIPS * n_cols), F32)], axis=0)
        for cp in first + second:
            cp.wait_send()

    vm = pl.BlockSpec(memory_space=pltpu.VMEM)
    return pl.pallas_call(
        body, name="ada_fwd",
        out_shape=[jax.ShapeDtypeStruct((8, N_CHIPS * n_cols), F32), jax.ShapeDtypeStruct((8, D_MODEL), F32)],
        in_specs=[vm, vm, vm], out_specs=[vm, vm],
        scratch_shapes=[pltpu.VMEM((8, 8, D_MODEL), F32), pltpu.VMEM((3, 8, n_cols), F32),
                        pltpu.VMEM((N_CHIPS, 8, n_cols), F32), pltpu.SemaphoreType.DMA((10,)), pltpu.SemaphoreType.DMA((10,))],
        compiler_params=pltpu.CompilerParams(vmem_limit_bytes=VMEM_LIMIT),
    )(c8, w_shard, b_ada)


def ada_bwd(conds, dmod):
    def body(c_ref, d_ref, o_ref):
        act = _silu(c_ref[...])
        for l in range(DEPTH):
            o_ref[l] = mm_tn(act, d_ref[l])

    return pl.pallas_call(
        body, name="ada_bwd", out_shape=jax.ShapeDtypeStruct((DEPTH, D_MODEL, dmod.shape[2]), F32),
        compiler_params=pltpu.CompilerParams(vmem_limit_bytes=VMEM_LIMIT),
    )(conds, dmod)


def _f_attn(first_block, q, za, kc, vc, kp, vp, qg, kg, sinks):
    w = WINDOW
    lane = lax.broadcasted_iota(jnp.int32, (1, 128), 1)
    halves = [lane < 64, lane >= 64]

    def rms_halves(x, g):
        x2 = x * x
        s0 = jnp.sum(jnp.where(halves[0], x2, 0.0), axis=-1, keepdims=True)
        s1 = jnp.sum(jnp.where(halves[1], x2, 0.0), axis=-1, keepdims=True)
        r = jnp.where(halves[0], lax.rsqrt(s0 / 64.0 + EPS), lax.rsqrt(s1 / 64.0 + EPS))
        return x * r * g

    kcat = rms_halves(jnp.concatenate([kp, kc], axis=0), kg)
    vcat = jnp.concatenate([vp, vc], axis=0)
    qi = lax.broadcasted_iota(jnp.int32, (w, 2 * w), 0)
    kj = lax.broadcasted_iota(jnp.int32, (w, 2 * w), 1)
    dist = qi + w - kj
    valid = (dist >= 0) & (dist < w) & (jnp.logical_not(first_block) | (kj >= w))
    distf = dist.astype(F32)
    units = [(grp, half) for grp in range(4) for half in range(2)]
    qns = [rms_halves(q[:, 128 * grp:128 * grp + 128], qg) * (ATT_HEAD_DIM ** -0.5) for grp in range(4)]
    vhalf = [jnp.where(halves[half], vcat, 0.0) for half in range(2)]
    scores, sinks_h = [], []
    for grp, half in units:
        head = HEAD_ORDER[2 * grp + half]
        slope = 2.0 ** (-8.0 * (head + 1) / ATT_HEADS)
        sinks_h.append(jnp.sum(jnp.where(lane == head, sinks, 0.0), axis=-1, keepdims=True))
        s = mm_nt(jnp.where(halves[half], qns[grp], 0.0), kcat) - slope * distf
        scores.append(jnp.where(valid, s, NEG_INF))
    probs = []
    for s, sink in zip(scores, sinks_h):
        m = lax.stop_gradient(jnp.maximum(jnp.max(s, axis=-1, keepdims=True), sink))
        p = jnp.exp(s - m)
        probs.append(p / (jnp.sum(p, axis=-1, keepdims=True) + jnp.exp(sink - m)))
    outs = [mm(p, vhalf[half]) for p, (grp, half) in zip(probs, units)]
    return jnp.concatenate([outs[2 * grp] + outs[2 * grp + 1] for grp in range(4)], axis=1) * _silu(za)


def attn_fwd(name, proj, qg, kg, sinks):
    t = proj.shape[0]
    nb = t // WINDOW

    def body(q_ref, za_ref, kc_ref, vc_ref, kp_ref, vp_ref, qg_ref, kg_ref, s_ref, o_ref):
        first = pl.program_id(0) == 0
        o_ref[...] = _f_attn(first, q_ref[...], za_ref[...], kc_ref[...], vc_ref[...], kp_ref[...], vp_ref[...],
                             qg_ref[...], kg_ref[...], s_ref[...])

    cur = lambda cb: (lambda i: (i, cb))
    prev = lambda cb: (lambda i: (jnp.maximum(i - 1, 0), cb))
    return pl.pallas_call(
        body, name=name, grid=(nb,),
        in_specs=[pl.BlockSpec((WINDOW, 512), cur(P_QA // 512)), pl.BlockSpec((WINDOW, 512), cur(P_ZA // 512)),
                  pl.BlockSpec((WINDOW, 128), cur(P_KA // 128)), pl.BlockSpec((WINDOW, 128), cur(P_VA // 128)),
                  pl.BlockSpec((WINDOW, 128), prev(P_KA // 128)), pl.BlockSpec((WINDOW, 128), prev(P_VA // 128)),
                  _const_spec((1, 128)), _const_spec((1, 128)), _const_spec((1, 128))],
        out_specs=pl.BlockSpec((WINDOW, 512), lambda i: (i, 0)),
        out_shape=jax.ShapeDtypeStruct((t, 512), F32),
        compiler_params=_cparams(1),
    )(proj, proj, proj, proj, proj, proj, qg, kg, sinks)


def attn_bwd(name, proj, qg, kg, sinks, dya):
    t = proj.shape[0]
    nb = t // WINDOW

    def body(q_ref, za_ref, kc_ref, vc_ref, kp_ref, vp_ref, qg_ref, kg_ref, s_ref, dy_ref,
             dqz_ref, dkv_ref, dqg_ref, dkg_ref, ds_ref, carry_ref):
        j = pl.program_id(0)
        first = j == nb - 1

        @pl.when(j == 0)
        def _():
            carry_ref[...] = jnp.zeros_like(carry_ref)
            dqg_ref[...] = jnp.zeros_like(dqg_ref)
            dkg_ref[...] = jnp.zeros_like(dkg_ref)
            ds_ref[...] = jnp.zeros_like(ds_ref)

        ins = [r[...] for r in (q_ref, za_ref, kc_ref, vc_ref, kp_ref, vp_ref, qg_ref, kg_ref, s_ref)]
        _, vjp = jax.vjp(functools.partial(_f_attn, first), *ins)
        dq, dza, dkc, dvc, dkp, dvp, dqg, dkg, dsk = vjp(dy_ref[...])
        dqz_ref[:, 0:512] = dq.astype(dqz_ref.dtype)
        dqz_ref[:, 512:1024] = dza.astype(dqz_ref.dtype)
        dkv_ref[:, 0:128] = (dkc + carry_ref[0]).astype(dkv_ref.dtype)
        dkv_ref[:, 128:256] = (dvc + carry_ref[1]).astype(dkv_ref.dtype)
        carry_ref[0] = dkp
        carry_ref[1] = dvp
        dqg_ref[...] += dqg
        dkg_ref[...] += dkg
        ds_ref[...] += dsk

    cur = lambda cb: (lambda j: (nb - 1 - j, cb))
    prev = lambda cb: (lambda j: (jnp.maximum(nb - 2 - j, 0), cb))
    return pl.pallas_call(
        body, name=name, grid=(nb,),
        in_specs=[pl.BlockSpec((WINDOW, 512), cur(P_QA // 512)), pl.BlockSpec((WINDOW, 512), cur(P_ZA // 512)),
                  pl.BlockSpec((WINDOW, 128), cur(P_KA // 128)), pl.BlockSpec((WINDOW, 128), cur(P_VA // 128)),
                  pl.BlockSpec((WINDOW, 128), prev(P_KA // 128)), pl.BlockSpec((WINDOW, 128), prev(P_VA // 128)),
                  _const_spec((1, 128)), _const_spec((1, 128)), _const_spec((1, 128)),
                  pl.BlockSpec((WINDOW, 512), cur(0))],
        out_specs=[pl.BlockSpec((WINDOW, 1024), cur(0)), pl.BlockSpec((WINDOW, 256), cur(0)),
                   _const_spec((1, 128)), _const_spec((1, 128)), _const_spec((1, 128))],
        out_shape=[jax.ShapeDtypeStruct((t, 1024), BF16), jax.ShapeDtypeStruct((t, 256), BF16),
                   jax.ShapeDtypeStruct((1, 128), F32), jax.ShapeDtypeStruct((1, 128), F32),
                   jax.ShapeDtypeStruct((1, 128), F32)],
        scratch_shapes=[pltpu.VMEM((2, WINDOW, 128), F32)],
        compiler_params=_cparams(1),
    )(proj, proj, proj, proj, proj, proj, qg, kg, sinks, dya)


CONV_ROWS = 256


def _conv_taps(src_ref, w_ref, n_taps, base, t):
    for r0 in range(0, t, CONV_ROWS):
        acc = w_ref[0:1, :] * src_ref[pl.ds(r0 + base, CONV_ROWS), :]
        for k in range(1, n_taps):
            acc = acc + w_ref[k:k + 1, :] * src_ref[pl.ds(r0 + base + k, CONV_ROWS), :]
        yield r0, acc


def _conv_wgrad(dy_ref, src_ref, n_taps, base, t, dy_base=0):
    out = []
    for k in range(n_taps):
        acc = jnp.zeros((8, 128), F32)
        for r0 in range(0, t, CONV_ROWS):
            prod = dy_ref[pl.ds(r0 + dy_base, CONV_ROWS), :] * src_ref[pl.ds(r0 + base + k, CONV_ROWS), :]
            acc = acc + jnp.sum(prod.reshape(CONV_ROWS // 8, 8, 128), axis=0)
        out.append(jnp.sum(acc, axis=0, keepdims=True))
    return out


def glu_conv_fwd(name, proj, w32, bias):
    t = proj.shape[0]
    pad = 32

    def body(x_ref, w_ref, b_ref, o_ref, u_ref):
        u_ref[0:pad, :] = jnp.zeros((pad, 128), F32)
        u_ref[pad:pad + t, :] = x_ref[:, 0:128] * _sigmoid(x_ref[:, 128:256])
        for r0, acc in _conv_taps(u_ref, w_ref, CONV_K, pad - (CONV_K - 1), t):
            o_ref[pl.ds(r0, CONV_ROWS), :] = acc + b_ref[...]

    return pl.pallas_call(
        body, name=name, grid=(4,),
        in_specs=[pl.BlockSpec((t, 256), lambda cb: (0, P_GLU // 256 + cb)), pl.BlockSpec((32, 128), lambda cb: (0, cb)),
                  pl.BlockSpec((1, 128), lambda cb: (0, cb))],
        out_specs=pl.BlockSpec((t, 128), lambda cb: (0, cb)),
        out_shape=jax.ShapeDtypeStruct((t, 512), F32),
        scratch_shapes=[pltpu.VMEM((t + pad, 128), F32)],
        compiler_params=_cparams(1),
    )(proj, w32, bias)


def glu_conv_bwd(name, proj, w32, dub):
    t = proj.shape[0]
    pad = 32
    k1 = CONV_K - 1

    def body(x_ref, w_ref, dy_ref, dx_ref, dw_ref, db_ref, u_ref, dyp_ref, wrev_ref):
        val = x_ref[:, 0:128]
        sg = _sigmoid(x_ref[:, 128:256])
        u_ref[0:pad, :] = jnp.zeros((pad, 128), F32)
        u_ref[pad:pad + t, :] = val * sg
        dyp_ref[0:t, :] = dy_ref[...]
        dyp_ref[t:t + pad, :] = jnp.zeros((pad, 128), F32)
        for k in range(CONV_K):
            wrev_ref[k:k + 1, :] = w_ref[k1 - k:k1 - k + 1, :]
        wrev_ref[CONV_K:32, :] = jnp.zeros((32 - CONV_K, 128), F32)
        for r0, du in _conv_taps(dyp_ref, wrev_ref, CONV_K, 0, t):
            v = x_ref[pl.ds(r0, CONV_ROWS), 0:128]
            s = _sigmoid(x_ref[pl.ds(r0, CONV_ROWS), 128:256])
            dx_ref[pl.ds(r0, CONV_ROWS), 0:128] = (du * s).astype(dx_ref.dtype)
            dx_ref[pl.ds(r0, CONV_ROWS), 128:256] = (du * v * s * (1.0 - s)).astype(dx_ref.dtype)
        dws = _conv_wgrad(dyp_ref, u_ref, CONV_K, pad - k1, t)
        for k in range(CONV_K):
            dw_ref[k:k + 1, :] = dws[k]
        dw_ref[CONV_K:32, :] = jnp.zeros((32 - CONV_K, 128), F32)
        db_ref[...] = jnp.sum(dy_ref[...], axis=0, keepdims=True)

    return pl.pallas_call(
        body, name=name, grid=(4,),
        in_specs=[pl.BlockSpec((t, 256), lambda cb: (0, P_GLU // 256 + cb)), pl.BlockSpec((32, 128), lambda cb: (0, cb)),
                  pl.BlockSpec((t, 128), lambda cb: (0, cb))],
        out_specs=[pl.BlockSpec((t, 256), lambda cb: (0, cb)), pl.BlockSpec((32, 128), lambda cb: (0, cb)),
                   pl.BlockSpec((1, 128), lambda cb: (0, cb))],
        out_shape=[jax.ShapeDtypeStruct((t, 1024), BF16), jax.ShapeDtypeStruct((32, 512), F32),
                   jax.ShapeDtypeStruct((1, 512), F32)],
        scratch_shapes=[pltpu.VMEM((t + pad, 128), F32), pltpu.VMEM((t + pad, 128), F32), pltpu.VMEM((32, 128), F32)],
        compiler_params=_cparams(1),
    )(proj, w32, dub)


def sconv_fwd(name, proj, w8):
    t = proj.shape[0]
    pad = 8
    k1 = DN_CONV_K - 1

    def body(x_ref, w_ref, o_ref, xp_ref):
        xp_ref[0:pad, :] = jnp.zeros((pad, 128), F32)
        xp_ref[pad:pad + t, :] = x_ref[...]
        for r0, acc in _conv_taps(xp_ref, w_ref, DN_CONV_K, pad - k1, t):
            o_ref[pl.ds(r0, CONV_ROWS), :] = _silu(acc)

    return pl.pallas_call(
        body, name=name, grid=(12,),
        in_specs=[pl.BlockSpec((t, 128), lambda cb: (0, P_QKV // 128 + cb)), pl.BlockSpec((8, 128), lambda cb: (0, cb))],
        out_specs=pl.BlockSpec((t, 128), lambda cb: (0, cb)),
        out_shape=jax.ShapeDtypeStruct((t, 1536), F32),
        scratch_shapes=[pltpu.VMEM((t + pad, 128), F32)],
        compiler_params=_cparams(1),
    )(proj, w8)


def sconv_bwd(name, proj, w8, dqkv):
    t = proj.shape[0]
    pad = 8
    k1 = DN_CONV_K - 1

    def body(x_ref, w_ref, dy_ref, dx_ref, dw_ref, xp_ref, dpp_ref, wrev_ref):
        xp_ref[0:pad, :] = jnp.zeros((pad, 128), F32)
        xp_ref[pad:pad + t, :] = x_ref[...]
        for r0, pre in _conv_taps(xp_ref, w_ref, DN_CONV_K, pad - k1, t):
            s = _sigmoid(pre)
            dpp_ref[pl.ds(r0, CONV_ROWS), :] = dy_ref[pl.ds(r0, CONV_ROWS), :] * (s * (1.0 + pre * (1.0 - s)))
        dpp_ref[t:t + pad, :] = jnp.zeros((pad, 128), F32)
        for k in range(DN_CONV_K):
            wrev_ref[k:k + 1, :] = w_ref[k1 - k:k1 - k + 1, :]
        wrev_ref[DN_CONV_K:8, :] = jnp.zeros((8 - DN_CONV_K, 128), F32)
        for r0, dx in _conv_taps(dpp_ref, wrev_ref, DN_CONV_K, 0, t):
            dx_ref[pl.ds(r0, CONV_ROWS), :] = dx.astype(dx_ref.dtype)
        dws = _conv_wgrad(dpp_ref, xp_ref, DN_CONV_K, pad - k1, t)
        for k in range(DN_CONV_K):
            dw_ref[k:k + 1, :] = dws[k]
        dw_ref[DN_CONV_K:8, :] = jnp.zeros((8 - DN_CONV_K, 128), F32)

    return pl.pallas_call(
        body, name=name, grid=(12,),
        in_specs=[pl.BlockSpec((t, 128), lambda cb: (0, P_QKV // 128 + cb)), pl.BlockSpec((8, 128), lambda cb: (0, cb)),
                  pl.BlockSpec((t, 128), lambda cb: (0, cb))],
        out_specs=[pl.BlockSpec((t, 128), lambda cb: (0, cb)), pl.BlockSpec((8, 128), lambda cb: (0, cb))],
        out_shape=[jax.ShapeDtypeStruct((t, 1536), BF16), jax.ShapeDtypeStruct((8, 1536), F32)],
        scratch_shapes=[pltpu.VMEM((t + pad, 128), F32), pltpu.VMEM((t + pad, 128), F32), pltpu.VMEM((8, 128), F32)],
        compiler_params=_cparams(1),
    )(proj, w8, dqkv)


def _f_delta_step(qkv, ab, zc, s0, s1, s2, s3, a_log, dt_bias, dn_g, inverses=None, with_inverses=False):
    cs = DN_CHUNK
    n = 2 * cs
    states = (s0, s1, s2, s3)
    lane = lax.broadcasted_iota(jnp.int32, (1, 128), 1)
    ri = lax.broadcasted_iota(jnp.int32, (n, n), 0)
    ci = lax.broadcasted_iota(jnp.int32, (n, n), 1)
    same = (ri // cs) == (ci // cs)
    lower = same & (ri >= ci)
    strict = same & (ri > ci)
    sums = jnp.concatenate([jnp.where(lower, 1.0, 0.0), jnp.where(same, 1.0, 0.0), jnp.where(ci < cs, 1.0, 0.0),
                            jnp.where(ci >= cs, 1.0, 0.0)], axis=0)
    top = lax.broadcasted_iota(jnp.int32, (n, 1), 0) < cs

    def pick(row, idx):
        return jnp.sum(jnp.where(lane == idx, row, 0.0), axis=-1, keepdims=True)

    def l2n(x):
        return x * lax.rsqrt(jnp.sum(x * x, axis=-1, keepdims=True) + EPS)

    n_chunks = qkv.shape[0] // cs
    units = [(k, pair) for k in range(n_chunks) for pair in range(2)]

    pre = []
    for k, pair in units:
        hs = (2 * pair, 2 * pair + 1)
        rows = slice(k * cs, (k + 1) * cs)
        stack = lambda f: jnp.concatenate([f(hs[0]), f(hs[1])], axis=0)
        qd = l2n(stack(lambda h: qkv[rows, 128 * h:128 * h + 128])) * (128 ** -0.5)
        kd = l2n(stack(lambda h: qkv[rows, 512 + 128 * h:512 + 128 * h + 128]))
        vd = stack(lambda h: qkv[rows, 1024 + 128 * h:1024 + 128 * h + 128])
        beta = _sigmoid(stack(lambda h: pick(ab[rows], 4 + h)))
        g = stack(lambda h: -jnp.exp(pick(a_log, h)) * _softplus(pick(ab[rows], h) + pick(dt_bias, h)))
        g_sums = sel_mm(sums, g * jnp.ones((1, n), F32))
        gc_col = g_sums[0:n]
        gl_b = g_sums[n:2 * n]
        g_end = (g_sums[2 * n:3 * n], g_sums[3 * n:])
        decay = jnp.where(lower, jnp.exp(jnp.where(lower, gc_col - gc_col.T, 0.0)), 0.0)
        kb = kd * beta
        pre.append(dict(qd=qd, kd=kd, vb=vd * beta, kb=kb, gc_col=gc_col, gl_b=gl_b, g_end=g_end, decay=decay,
                        a=jnp.where(strict, mm_nt(kb, kd) * decay, 0.0)))
    if inverses is None:
        tmats = tri_inv(*[p["a"] for p in pre])
    else:
        tmats = [tri_inv_known(p["a"], t) for p, t in zip(pre, inverses)]

    mid = []
    for p, tmat in zip(pre, tmats):
        egc = jnp.exp(p["gc_col"])
        mid.append(dict(u=mm(tmat, p["vb"]), wm=mm(tmat, p["kb"] * egc), qe=p["qd"] * egc,
                        intra=jnp.where(lower, mm_nt(p["qd"], p["kd"]) * p["decay"], 0.0),
                        ke=p["kd"] * jnp.exp(p["gl_b"] - p["gc_col"]), g_end=p["g_end"]))

    ys = []
    for k in range(n_chunks):
        rows = slice(k * cs, (k + 1) * cs)
        new_states, y_heads = [], []
        for pair in range(2):
            m = mid[2 * k + pair]
            hs = (2 * pair, 2 * pair + 1)
            st = (states[hs[0]], states[hs[1]])
            v_new = m["u"] - jnp.concatenate([mm(m["wm"][:cs], st[0]), mm(m["wm"][cs:], st[1])], axis=0)
            o = jnp.concatenate([mm(m["qe"][:cs], st[0]), mm(m["qe"][cs:], st[1])], axis=0) + mm(m["intra"], v_new)
            new_states.append(st[0] * jnp.exp(m["g_end"][0]) + mm_tn(jnp.where(top, m["ke"], 0.0), v_new))
            new_states.append(st[1] * jnp.exp(m["g_end"][1]) + mm_tn(jnp.where(top, 0.0, m["ke"]), v_new))
            od = o * lax.rsqrt(jnp.mean(o * o, axis=-1, keepdims=True) + EPS) * dn_g
            y_heads += [od[:cs] * _silu(zc[rows, 128 * hs[0]:128 * hs[0] + 128]),
                        od[cs:] * _silu(zc[rows, 128 * hs[1]:128 * hs[1] + 128])]
        states = tuple(new_states)
        ys.append(jnp.concatenate(y_heads, axis=1))
    if with_inverses:
        return (jnp.concatenate(ys, axis=0), *states), tmats
    return (jnp.concatenate(ys, axis=0), *states)


DELTA_ROWS = 4 * DN_CHUNK
DELTA_UNITS = 2 * DELTA_ROWS // DN_CHUNK


def delta_fwd(name, qkv, proj, a_log, dt_bias, dn_g):
    t = qkv.shape[0]
    nc = t // DELTA_ROWS

    def body(qkv_ref, ab_ref, zc_ref, al_ref, dt_ref, g_ref, y_ref, ssave_ref, tsave_ref, s_ref):
        @pl.when(pl.program_id(0) == 0)
        def _():
            s_ref[...] = jnp.zeros_like(s_ref)

        ssave_ref[0] = s_ref[...]
        st = [s_ref[128 * h:128 * h + 128, :] for h in range(4)]
        (y, *ns), tmats = _f_delta_step(qkv_ref[...], ab_ref[...], zc_ref[...], *st, al_ref[...], dt_ref[...], g_ref[...],
                                        with_inverses=True)
        y_ref[...] = y
        for h in range(4):
            s_ref[128 * h:128 * h + 128, :] = ns[h]
        for u, tm in enumerate(tmats):
            tsave_ref[0, 128 * u:128 * u + 128, :] = tm

    return pl.pallas_call(
        body, name=name, grid=(nc,),
        in_specs=[pl.BlockSpec((DELTA_ROWS, 1536), lambda i: (i, 0)), pl.BlockSpec((DELTA_ROWS, 128), lambda i: (i, P_AB // 128)),
                  pl.BlockSpec((DELTA_ROWS, 512), lambda i: (i, P_ZC // 512)),
                  _const_spec((1, 128)), _const_spec((1, 128)), _const_spec((1, 128))],
        out_specs=[pl.BlockSpec((DELTA_ROWS, 512), lambda i: (i, 0)), pl.BlockSpec((1, 512, 128), lambda i: (i, 0, 0)),
                   pl.BlockSpec((1, DELTA_UNITS * 128, 128), lambda i: (i, 0, 0))],
        out_shape=[jax.ShapeDtypeStruct((t, 512), F32), jax.ShapeDtypeStruct((nc, 512, 128), F32),
                   jax.ShapeDtypeStruct((nc, DELTA_UNITS * 128, 128), F32)],
        scratch_shapes=[pltpu.VMEM((512, 128), F32)],
        compiler_params=_cparams(1),
    )(qkv, proj, proj, a_log, dt_bias, dn_g)


def delta_bwd(name, qkv, proj, ssave, tsave, a_log, dt_bias, dn_g, dyc, carry=None):
    t = qkv.shape[0]
    nc = t // DELTA_ROWS
    c_ins, c_in_specs, c_outs, c_out_specs, c_scratch = _host(carry)
    n_ci, n_co = len(c_ins), len(c_outs)

    def body(*refs):
        qkv_ref, ab_ref, zc_ref, ss_ref, ts_ref, al_ref, dt_ref, g_ref, dy_ref = refs[:9]
        dqkv_ref, dab_ref, dzc_ref, dal_ref, ddt_ref, dg_ref = refs[9 + n_ci:15 + n_ci]
        ds_ref = refs[15 + n_ci + n_co]
        carried = (refs[9:9 + n_ci], refs[15 + n_ci:15 + n_ci + n_co], *refs[16 + n_ci + n_co:])

        @pl.when(pl.program_id(0) == 0)
        def _():
            ds_ref[...] = jnp.zeros_like(ds_ref)
            dal_ref[...] = jnp.zeros_like(dal_ref)
            ddt_ref[...] = jnp.zeros_like(ddt_ref)
            dg_ref[...] = jnp.zeros_like(dg_ref)

        if carry is not None:
            carry.emit_start(pl.program_id(0) == 0, *carried)

        st = [ss_ref[0, 128 * h:128 * h + 128, :] for h in range(4)]
        known = [ts_ref[0, 128 * u:128 * u + 128, :] for u in range(DELTA_UNITS)]
        _, vjp = jax.vjp(functools.partial(_f_delta_step, inverses=known), qkv_ref[...], ab_ref[...], zc_ref[...], *st,
                         al_ref[...], dt_ref[...], g_ref[...])
        dst = tuple(ds_ref[128 * h:128 * h + 128, :] for h in range(4))
        dqkv, dab, dzc, d0, d1, d2, d3, dal, ddt, dg = vjp((dy_ref[...], *dst))
        dqkv_ref[...] = dqkv
        dab_ref[...] = dab.astype(dab_ref.dtype)
        dzc_ref[...] = dzc.astype(dzc_ref.dtype)
        for h, d in enumerate((d0, d1, d2, d3)):
            ds_ref[128 * h:128 * h + 128, :] = d
        dal_ref[...] += dal
        ddt_ref[...] += ddt
        dg_ref[...] += dg

        if carry is not None:
            carry.emit_finish(pl.program_id(0) == nc - 1, *carried)

    rev = lambda cb: (lambda j: (nc - 1 - j, cb))
    return pl.pallas_call(
        body, name=name, grid=(nc,),
        in_specs=[pl.BlockSpec((DELTA_ROWS, 1536), rev(0)), pl.BlockSpec((DELTA_ROWS, 128), rev(P_AB // 128)),
                  pl.BlockSpec((DELTA_ROWS, 512), rev(P_ZC // 512)), pl.BlockSpec((1, 512, 128), lambda j: (nc - 1 - j, 0, 0)),
                  pl.BlockSpec((1, DELTA_UNITS * 128, 128), lambda j: (nc - 1 - j, 0, 0)),
                  _const_spec((1, 128)), _const_spec((1, 128)), _const_spec((1, 128)),
                  pl.BlockSpec((DELTA_ROWS, 512), rev(0))] + c_in_specs,
        out_specs=[pl.BlockSpec((DELTA_ROWS, 1536), rev(0)), pl.BlockSpec((DELTA_ROWS, 128), rev(0)),
                   pl.BlockSpec((DELTA_ROWS, 512), rev(0)),
                   _const_spec((1, 128)), _const_spec((1, 128)), _const_spec((1, 128))] + c_out_specs,
        out_shape=[jax.ShapeDtypeStruct((t, 1536), F32), jax.ShapeDtypeStruct((t, 128), BF16),
                   jax.ShapeDtypeStruct((t, 512), BF16),
                   jax.ShapeDtypeStruct((1, 128), F32), jax.ShapeDtypeStruct((1, 128), F32), jax.ShapeDtypeStruct((1, 128), F32)]
        + c_outs,
        scratch_shapes=[pltpu.VMEM((512, 128), F32)] + c_scratch,
        compiler_params=_cparams(1),
    )(qkv, proj, proj, ssave, tsave, a_log, dt_bias, dn_g, dyc, *c_ins)


def loss_head(name, y, target, tm):
    t, d = y.shape

    def body(y_ref, t_ref, dy_ref, l_ref):
        err = y_ref[...] - t_ref[...]
        dy_ref[...] = err * (1.0 / d)
        part = 0.5 * jnp.sum(jnp.sum(err * err, axis=-1, keepdims=True) * (1.0 / d), axis=0, keepdims=True)

        @pl.when(pl.program_id(0) == 0)
        def _():
            l_ref[...] = part

        @pl.when(pl.program_id(0) > 0)
        def _():
            l_ref[...] += part

    return pl.pallas_call(
        body, name=name, grid=(t // tm,),
        in_specs=[_row_spec(tm, d, 0), _row_spec(tm, d, 0)],
        out_specs=[_row_spec(tm, d, 0), _const_spec((1, 1))],
        out_shape=[jax.ShapeDtypeStruct((t, d), F32), jax.ShapeDtypeStruct((1, 1), F32)],
        compiler_params=_cparams(1),
    )(y, target)


TM = 512
TM_MERGE = 256
TM_IN = 1024
TN_IN = 1152


def _lane_pad(v, n=128):
    return jnp.pad(v.astype(F32), (0, n - v.shape[0]))[None, :]


def f_norm_mod_res(x, g, scale, shift):
    return f_norm_mod(x, g, scale, shift), x


def prep_layer(w):
    p = dict(w)
    p["wpt"] = _w_in_assemble(w["w_in"])
    p["wpa"] = _perm_heads_rows(w["w_proj_a"])
    p["dw32"] = jnp.pad(w["dw_w"], ((0, 32 - CONV_K), (0, 0)))
    p["sconv8"] = jnp.pad(w["sconv_w"], ((0, 8 - DN_CONV_K), (0, 0)))
    p["qg"] = jnp.tile(w["q_norm_g"], 2)[None, :]
    p["kg"] = jnp.tile(w["k_norm_g"], 2)[None, :]
    p["sinks128"] = _lane_pad(w["sinks"])
    p["al"] = _lane_pad(w["a_log"])
    p["dtb"] = _lane_pad(w["dt_bias"])
    p["dng"] = w["dn_norm_g"][None, :]
    return p


def layer_fwd(tag, x, mod, p, carry_inproj=None, carry_merge=None):
    d = D_MODEL
    shift, scale, gate = mod[:, :d], mod[:, d:2 * d], mod[:, 2 * d:]
    g = p["norm_g"][None, :]
    h, h_t = rowwise_fwd(f"norm_fwd{tag}", lambda *a: (f_norm_mod(*a),) * 2, [(x, d, 0)], [g, scale, shift],
                         [(d, BF16), (d, BF16, "transposed")], TM)
    proj = matmul_nn(f"inproj_fwd{tag}", h, p["wpt"], F32, TM_IN, TN_IN, d, b_transposed=True, carry=carry_inproj)
    proj, got_inproj = (proj, []) if carry_inproj is None else (proj[0], proj[1:])
    ya = attn_fwd(f"attn_fwd{tag}", proj, p["qg"], p["kg"], p["sinks128"])
    ub = glu_conv_fwd(f"glu_conv_fwd{tag}", proj, p["dw32"], p["dw_b"][None, :])
    conf_consts = [p["ln_g"][None, :], p["ln_b"][None, :], p["pw2_w"], p["pw2_b"][None, :]]
    (yb,) = rowwise_fwd(f"conf_fwd{tag}", f_conf_tail, [(ub, 512, 0), (proj, 512, P_ZB // 512)], conf_consts, [(512, F32)], TM)
    qkv = sconv_fwd(f"sconv_fwd{tag}", proj, p["sconv8"])
    yc, ssave, tsave = delta_fwd(f"delta_fwd{tag}", qkv, proj, p["al"], p["dtb"], p["dng"])
    merge_consts = [gate, p["wpa"], p["w_proj_b"], p["w_proj_c"], p["w_out"]]
    merge_rows = [(ya, 512, 0), (yb, 512, 0), (yc, 512, 0), (proj, 3 * d, P_MG // (3 * d)), (x, d, 0)]
    xn, *got_merge = rowwise_fwd(f"merge_fwd{tag}", f_merge, merge_rows, merge_consts, [(d, F32)], TM_MERGE, carry=carry_merge)
    saved = dict(x=x, h_t=h_t, proj=proj, ub=ub, qkv=qkv, ssave=ssave, tsave=tsave, norm_consts=[g, scale, shift],
                 conf_consts=conf_consts, merge_consts=merge_consts, merge_rows=merge_rows)
    return xn, saved, got_inproj, got_merge


def layer_bwd(tag, dxn, p, s, carry_merge=None, carry_delta=None, carry_dh=None):
    d = D_MODEL
    proj = s["proj"]
    dya, dyb, dyc, dmg, dgate, dwpa, dwpb, dwpc, dwout, *got_merge = rowwise_bwd(
        f"merge_bwd{tag}", f_merge, s["merge_rows"], s["merge_consts"], [(dxn, d, 0)], [F32, F32, F32, BF16, None], TM_MERGE,
        carry=carry_merge)
    carry_delta = None if carry_delta is None else carry_delta(got_merge)
    dqz, dkv, dqg, dkg, dsinks = attn_bwd(f"attn_bwd{tag}", proj, p["qg"], p["kg"], p["sinks128"], dya)
    dub, dzb, dln_g, dln_b, dpw2_w, dpw2_b = rowwise_bwd(
        f"conf_bwd{tag}", f_conf_tail, [(s["ub"], 512, 0), (proj, 512, P_ZB // 512)], s["conf_consts"], [(dyb, 512, 0)],
        [F32, BF16], TM)
    dglu, ddw32, ddw_b = glu_conv_bwd(f"glu_conv_bwd{tag}", proj, p["dw32"], dub)
    dqkv, dab, dzc, dal, ddtb, ddng, *got_delta = delta_bwd(f"delta_bwd{tag}", s["qkv"], proj, s["ssave"], s["tsave"], p["al"],
                                                            p["dtb"], p["dng"], dyc, carry_delta)
    dqkv_pre, dsconv8 = sconv_bwd(f"sconv_bwd{tag}", proj, p["sconv8"], dqkv)
    dproj = jnp.concatenate([dqz, dglu, dzb, dzc, dmg, dqkv_pre, dkv, dab], axis=1)
    dwp = matmul_nn(f"inproj_bwd_dw{tag}", s["h_t"], dproj, F32, d, TN_IN, 2048)
    reduced = dict(w_in=_w_in_grad_blocks(dwp), pw2_w=dpw2_w, w_proj_a=_unperm_heads_rows(dwpa), w_proj_b=dwpb, w_proj_c=dwpc,
                   w_out=dwout)
    carry_dh = None if carry_dh is None else carry_dh(reduced)
    dh = matmul_nn(f"inproj_bwd_dh{tag}", dproj, p["wpt"], F32, TM_IN, d, P_TOTAL // 3, carry=carry_dh)
    dh, got_dh = (dh, []) if carry_dh is None else (dh[0], dh[1:])
    dx, dnorm_g, dscale, dshift = rowwise_bwd(
        f"norm_bwd{tag}", f_norm_mod_res, [(s["x"], d, 0)], s["norm_consts"], [(dh, d, 0), (dxn, d, 0)], [F32], TM)
    dmod = jnp.concatenate([dshift, dscale, dgate], axis=1)
    grads = dict(
        reduced, b_ada=dmod[0], norm_g=dnorm_g[0],
        q_norm_g=dqg[0, :64] + dqg[0, 64:], k_norm_g=dkg[0, :64] + dkg[0, 64:], sinks=dsinks[0, :ATT_HEADS],
        dw_w=ddw32[:CONV_K], dw_b=ddw_b[0], ln_g=dln_g[0], ln_b=dln_b[0], pw2_b=dpw2_b[0],
        sconv_w=dsconv8[:DN_CONV_K], a_log=dal[0, :DN_HEADS], dt_bias=ddtb[0, :DN_HEADS], dn_norm_g=ddng[0])
    return dx, grads, got_merge, got_delta, got_dh


SHARDED = {"w_ada": 2, "w_in": 2, "dw_w": 2, "pw2_w": 1, "sconv_w": 2, "w_proj_a": 2, "w_proj_b": 2, "w_proj_c": 2,
           "w_out": 1}
GATHERED = tuple(n for n in SHARDED if n != "w_ada")
GATHER_F32 = ("dw_w", "sconv_w")
REDUCE_BIG = tuple(n for n in GATHERED if n not in GATHER_F32)
SMALL = ("b_ada", "norm_g", "q_norm_g", "k_norm_g", "sinks", "dw_b", "ln_g", "ln_b", "pw2_b", "a_log", "dt_bias",
         "dn_norm_g")
SMALL_ROWS = 104
SMALL_GRAD_ROWS = 448
W_IN_SHARD = D_IN // N_CHIPS
SUM_TILE = 256


def _w_in_orig():
    orig = np.full(P_TOTAL, -1, np.int64)
    p = 0
    for s, n in _in_pieces():
        orig[p:p + n] = np.arange(s, s + n)
        p += n
    return orig


def _w_in_blocks(k):
    orig = _w_in_orig().reshape(-1, 128)
    lo, hi = k * W_IN_SHARD, (k + 1) * W_IN_SHARD
    return [b for b in range(orig.shape[0]) if np.any((orig[b] >= lo) & (orig[b] < hi))]


W_IN_BLOCKS = max(len(_w_in_blocks(k)) for k in range(N_CHIPS))


def _runs(idx):
    out, i = [], 0
    while i < len(idx):
        j = i + 1
        while j < len(idx) and ((idx[i] < 0 and idx[j] < 0) or (idx[i] >= 0 and idx[j] == idx[j - 1] + 1)):
            j += 1
        out.append((int(idx[i]) if idx[i] >= 0 else -1, j - i))
        i = j
    return out


def _take(a, idx, axis=-1):
    axis = axis % a.ndim
    parts = []
    for s, n in _runs(idx):
        if s < 0:
            parts.append(jnp.zeros(a.shape[:axis] + (n,) + a.shape[axis + 1:], a.dtype))
        else:
            parts.append(lax.slice_in_dim(a, s, s + n, axis=axis))
    return parts[0] if len(parts) == 1 else jnp.concatenate(parts, axis=axis)


def _w_in_send(k, shard_t):
    orig = _w_in_orig().reshape(-1, 128)
    lo, hi = k * W_IN_SHARD, (k + 1) * W_IN_SHARD
    idx = np.concatenate([np.where((orig[b] >= lo) & (orig[b] < hi), orig[b] - lo, -1) for b in _w_in_blocks(k)])
    idx = np.concatenate([idx, np.full((W_IN_BLOCKS - len(_w_in_blocks(k))) * 128, -1)])
    return _take(shard_t, idx, axis=-2)


def _w_in_assemble(blocks_t):
    where = [{b: i for i, b in enumerate(_w_in_blocks(k))} for k in range(N_CHIPS)]
    n_blocks = P_TOTAL // 128
    owners = [[(k, where[k][b]) for k in range(N_CHIPS) if b in where[k]] for b in range(n_blocks)]
    parts, b = [], 0
    while b < n_blocks:
        if len(owners[b]) == 1:
            k, pos = owners[b][0]
            e = b + 1
            while e < n_blocks and owners[e] == [(k, pos + e - b)]:
                e += 1
            parts.append(blocks_t[k][pos * 128:(pos + e - b) * 128])
            b = e
        else:
            parts.append(functools.reduce(jnp.add, [blocks_t[k][pos * 128:(pos + 1) * 128] for k, pos in owners[b]]))
            b += 1
    return jnp.concatenate(parts, axis=0)


def _w_in_grad_blocks(wp):
    out = []
    for k in range(N_CHIPS):
        idx = np.concatenate([np.arange(128 * b, 128 * b + 128) for b in _w_in_blocks(k)])
        idx = np.concatenate([idx, np.full((W_IN_BLOCKS - len(_w_in_blocks(k))) * 128, -1)])
        out.append(_take(wp, idx))
    return jnp.stack(out)


def _w_in_receive_grad(k, blocks):
    orig = _w_in_orig()
    inv = np.zeros(D_IN, np.int64)
    inv[orig[orig >= 0]] = np.nonzero(orig >= 0)[0]
    where = {b: i for i, b in enumerate(_w_in_blocks(k))}
    cols = inv[k * W_IN_SHARD:(k + 1) * W_IN_SHARD]
    return _take(blocks, np.array([where[c // 128] * 128 + c % 128 for c in cols]))


def _join_layer(v, axis):
    if axis == 2:
        return jnp.transpose(v, (1, 0, 2)).reshape(v.shape[1], N_CHIPS * v.shape[2])
    return v.reshape(N_CHIPS * v.shape[1], v.shape[2])


def _split_layer(v, axis):
    a, b = v.shape
    if axis == 2:
        return jnp.transpose(v.reshape(a, N_CHIPS, b // N_CHIPS), (1, 0, 2))
    return v.reshape(N_CHIPS, a // N_CHIPS, b)


def pack_small(vals, names, rows):
    flat = jnp.concatenate([vals[n].astype(F32).reshape(-1) for n in names])
    return jnp.pad(flat, (0, rows * 128 - flat.shape[0])).reshape(rows, 128)


def unpack_small(packed, names, shapes):
    flat = packed.reshape(-1)
    out, off = {}, 0
    for n in names:
        k = int(np.prod(shapes[n]))
        out[n] = flat[off:off + k].reshape(shapes[n])
        off += k
    return out


ANY = pl.BlockSpec(memory_space=pl.ANY)


def _place():
    x, y, c = lax.axis_index("x"), lax.axis_index("y"), lax.axis_index("c")
    chips = [(1 - x, y), (x, 1 - y), (1 - x, 1 - y)]
    return x, y, c, chips


def _remote(src, dst, send_sem, recv_sem, to):
    return pltpu.make_async_remote_copy(src_ref=src, dst_ref=dst, send_sem=send_sem, recv_sem=recv_sem, device_id=to,
                                        device_id_type=MESH)


class Carry:
    def __init__(self, ins, out_shapes, sems, start, finish, in_place=False):
        self.ins, self.out_shapes, self.sems, self.start, self.finish, self.in_place = (
            list(ins), list(out_shapes), sems, start, finish, in_place)

    def scratch(self):
        return [pltpu.SemaphoreType.DMA(self.sems), pltpu.SemaphoreType.DMA(self.sems)]

    def aliases(self, first_in, first_out):
        return {first_in + i: first_out + i for i in range(len(self.ins))} if self.in_place else {}

    def emit_start(self, first, in_refs, out_refs, send_sems, recv_sems):
        @pl.when(first)
        def _():
            self.start(in_refs, out_refs, send_sems, recv_sems)

    def emit_finish(self, last, in_refs, out_refs, send_sems, recv_sems):
        @pl.when(last)
        def _():
            self.finish(in_refs, out_refs, send_sems, recv_sems)


def _host(carry):
    if carry is None:
        return [], [], [], [], []
    return carry.ins, [ANY] * len(carry.ins), carry.out_shapes, [ANY] * len(carry.out_shapes), carry.scratch()


def run_carry(name, carry):
    n_in, n_out = len(carry.ins), len(carry.out_shapes)

    def body(*refs):
        ins, outs, sems = refs[:n_in], refs[n_in:n_in + n_out], refs[n_in + n_out:]
        carry.start(ins, outs, *sems)
        carry.finish(ins, outs, *sems)

    return pl.pallas_call(
        body, name=name, out_shape=carry.out_shapes, in_specs=[ANY] * n_in, out_specs=[ANY] * n_out,
        input_output_aliases=carry.aliases(0, 0), scratch_shapes=carry.scratch(),
    )(*carry.ins)


def carry_allgather(layer, slots):
    n = len(slots)

    def copies(out, send_sems, recv_sems, only_ici_out=False):
        x, y, c, chips = _place()
        ici_out, ici_in, d2d_out, d2d_in = [], [], [], []
        for j, chip in enumerate(chips):
            for t in range(n):
                mine, land = out[t].at[2 * x + y], out[t].at[2 * chip[0] + chip[1]]
                ici_out.append(_remote(mine, mine, send_sems.at[t, j], recv_sems.at[t, j], (*chip, layer)))
                if only_ici_out:
                    continue
                ici_in.append(_remote(land, land, send_sems.at[t, j], recv_sems.at[t, j], (*chip, layer)))
                d2d_out.append(_remote(land, land, send_sems.at[t, 3 + j], recv_sems.at[t, 3 + j], (x, y, 1 - layer)))
                d2d_in.append(_remote(land, land, send_sems.at[t, 3 + j], recv_sems.at[t, 3 + j], (x, y, layer)))
        return c, ici_out, ici_in, d2d_out, d2d_in

    def start(ins, out, send_sems, recv_sems):
        c, ici_out, _, _, _ = copies(out, send_sems, recv_sems, only_ici_out=True)

        @pl.when(c == layer)
        def _():
            for cp in ici_out:
                cp.start()

    def finish(ins, out, send_sems, recv_sems):
        c, ici_out, ici_in, d2d_out, d2d_in = copies(out, send_sems, recv_sems)

        @pl.when(c == layer)
        def _():
            for arrived, onward in zip(ici_in, d2d_out):
                arrived.wait_recv()
                onward.start()
            for cp in ici_out + d2d_out:
                cp.wait_send()

        @pl.when(c != layer)
        def _():
            for cp in d2d_in:
                cp.wait_recv()

    return Carry(slots, [jax.ShapeDtypeStruct(s.shape, s.dtype) for s in slots], (n, 6), start, finish, in_place=True)


def carry_pair_send(layer, gs):
    def copies(g, recv, send_sems, recv_sems):
        x, y, c, _ = _place()
        return c, [_remote(g[t], recv[t], send_sems.at[t], recv_sems.at[t], (x, y, 1 - c)) for t in range(len(gs))]

    def start(g, recv, send_sems, recv_sems):
        c, cps = copies(g, recv, send_sems, recv_sems)

        @pl.when(c != layer)
        def _():
            for cp in cps:
                cp.start()

    def finish(g, recv, send_sems, recv_sems):
        c, cps = copies(g, recv, send_sems, recv_sems)

        @pl.when(c != layer)
        def _():
            for cp in cps:
                cp.wait_send()

        @pl.when(c == layer)
        def _():
            for cp in cps:
                cp.wait_recv()

    return Carry(gs, [jax.ShapeDtypeStruct(g.shape, g.dtype) for g in gs], (len(gs),), start, finish)


def grads_pair_sum(name, g, recv):
    _, a, b = recv.shape
    ta = min(a, SUM_TILE)

    def body(a_ref, b_ref, o_ref):
        o_ref[...] = (a_ref[...] + b_ref[...]).astype(o_ref.dtype)

    spec = pl.BlockSpec((None, ta, b), lambda s, i: (s, i, 0))
    return pl.pallas_call(
        body, name=name, grid=(N_CHIPS, a // ta), in_specs=[spec, spec], out_specs=spec,
        out_shape=jax.ShapeDtypeStruct(recv.shape, BF16), compiler_params=_cparams(2),
    )(g, recv)


def carry_chip_exchange(layer, ps):
    def copies(p, recv, send_sems, recv_sems):
        _, _, c, chips = _place()
        return c, [_remote(p[t].at[2 * chip[0] + chip[1]], recv[t].at[j], send_sems.at[t, j], recv_sems.at[t, j],
                           (*chip, layer)) for j, chip in enumerate(chips) for t in range(len(ps))]

    def start(p, recv, send_sems, recv_sems):
        c, cps = copies(p, recv, send_sems, recv_sems)

        @pl.when(c == layer)
        def _():
            for cp in cps:
                cp.start()

    def finish(p, recv, send_sems, recv_sems):
        c, cps = copies(p, recv, send_sems, recv_sems)

        @pl.when(c == layer)
        def _():
            for cp in cps:
                cp.wait()

    return Carry(ps, [jax.ShapeDtypeStruct((3,) + p.shape[1:], p.dtype) for p in ps], (len(ps), 3), start, finish)


def grads_chip_sum(name, layer, g, recv, recv2, into=None):
    _, a, b = recv.shape
    ta = min(a, SUM_TILE)
    my_slot = lambda: 2 * lax.axis_index("x") + lax.axis_index("y")

    def body(g_ref, r_ref, r2_ref, *rest):
        o_ref = rest[-1]
        own = g_ref[...] + r_ref[...]
        o_ref[...] = ((own + r2_ref[0].astype(F32)) + r2_ref[1].astype(F32)) + r2_ref[2].astype(F32)

    own_spec = pl.BlockSpec((None, ta, b), lambda i: (my_slot(), i, 0))
    return pl.pallas_call(
        body, name=name, grid=(a // ta,),
        in_specs=[own_spec, own_spec, pl.BlockSpec((3, ta, b), lambda i: (0, i, 0))] + ([] if into is None else [ANY]),
        out_specs=pl.BlockSpec((None, ta, b), lambda i: (layer, i, 0)),
        out_shape=jax.ShapeDtypeStruct((DEPTH, a, b), F32),
        input_output_aliases={} if into is None else {3: 0},
        compiler_params=_cparams(1),
    )(g, recv, recv2, *([] if into is None else [into]))


def grads_pair_gather(reds):
    n = len(reds)

    def body(*refs):
        buf = refs[n:2 * n]
        send_sems, recv_sems = refs[2 * n:]
        x, y, c, _ = _place()
        sibling = (x, y, 1 - c)
        cps = [_remote(buf[t].at[c], buf[t].at[c], send_sems.at[t], recv_sems.at[t], sibling) for t in range(n)]
        for cp in cps:
            cp.start()
        for t in range(n):
            _remote(buf[t].at[c], buf[t].at[1 - c], send_sems.at[t], recv_sems.at[t], sibling).wait_recv()
        for cp in cps:
            cp.wait_send()

    return pl.pallas_call(
        body, name="grads_pair_gather", out_shape=[jax.ShapeDtypeStruct(r.shape, r.dtype) for r in reds],
        in_specs=[ANY] * n, out_specs=[ANY] * n, input_output_aliases={t: t for t in range(n)},
        scratch_shapes=[pltpu.SemaphoreType.DMA((n,)), pltpu.SemaphoreType.DMA((n,))],
    )(*reds)


def small_allreduce(v):
    m, n = v.shape

    def body(x_ref, sum_ref, all_ref, send_sems, recv_sems, local_sem):
        x, y, c, chips = _place()
        me, sibling = (x, y, c), (x, y, 1 - c)

        def rows(px, py, pc):
            return all_ref.at[pl.ds((4 * px + 2 * py + pc) * m, m), :]

        def copy(k, block, to, src=None):
            return pltpu.make_async_remote_copy(src_ref=rows(*block) if src is None else src, dst_ref=rows(*block),
                                                send_sem=send_sems.at[k], recv_sem=recv_sems.at[k],
                                                device_id=to, device_id_type=MESH)

        mine = pltpu.make_async_copy(x_ref, rows(*me), local_sem)
        mine.start()
        first = [copy(0, me, sibling, src=x_ref)]
        first += [copy(1 + j, me, (*chip, c), src=x_ref) for j, chip in enumerate(chips)]
        for cp in first:
            cp.start()
        passed = [copy(4 + j, (*chip, c), sibling) for j, chip in enumerate(chips)]
        for j, chip in enumerate(chips):
            copy(1 + j, (*chip, c), me).wait_recv()
            passed[j].start()
        copy(0, sibling, me).wait_recv()
        for j, chip in enumerate(chips):
            copy(4 + j, (*chip, 1 - c), me).wait_recv()
        for cp in first + passed:
            cp.wait_send()
        mine.wait()
        acc = all_ref[0:m, :]
        for dev in range(1, 8):
            acc = acc + all_ref[dev * m:(dev + 1) * m, :]
        sum_ref[...] = acc

    vm = pl.BlockSpec(memory_space=pltpu.VMEM)
    return pl.pallas_call(
        body, name="small_allreduce",
        out_shape=[jax.ShapeDtypeStruct((m, n), F32), jax.ShapeDtypeStruct((8 * m, n), F32)],
        in_specs=[vm], out_specs=[vm, vm],
        scratch_shapes=[pltpu.SemaphoreType.DMA((7,)), pltpu.SemaphoreType.DMA((7,)), pltpu.SemaphoreType.DMA],
    )(v)


def grads_by_chip(layer_grads):
    return [layer_grads[n] if n == "w_in" else _split_layer(layer_grads[n], SHARDED[n]) for n in REDUCE_BIG]


def grads_pair_sums(layer, gs, recv):
    return [grads_pair_sum(f"grads_pair_sum{layer}_{n}", g, r) for n, g, r in zip(REDUCE_BIG, gs, recv)]


def adamw(name, w, g, m, v, block):
    grid = tuple(s // b for s, b in zip(w.shape, block))

    def body(w_ref, g_ref, m_ref, v_ref, d_ref, nm_ref, nv_ref):
        gv = g_ref[...]
        nm = ADAM_B1 * m_ref[...] + (1.0 - ADAM_B1) * gv
        nv = ADAM_B2 * v_ref[...] + (1.0 - ADAM_B2) * (gv * gv)
        m_hat = nm / (1.0 - ADAM_B1 ** ADAM_STEP)
        v_hat = nv / (1.0 - ADAM_B2 ** ADAM_STEP)
        d_ref[...] = -ADAM_LR * (m_hat / (jnp.sqrt(v_hat) + ADAM_EPS) + ADAM_WD * w_ref[...])
        nm_ref[...] = nm
        nv_ref[...] = nv

    spec = pl.BlockSpec(tuple(block), lambda *idx: idx)
    return pl.pallas_call(
        body, name=name, grid=grid, in_specs=[spec] * 4, out_specs=[spec] * 3,
        out_shape=[jax.ShapeDtypeStruct(w.shape, F32)] * 3, compiler_params=_cparams(len(grid)),
    )(w, g, m, v)


ADAM_ROWS = {"w_ada": 512, "dw_w": 62, "pw2_w": 256, "sconv_w": 8, "w_proj_a": 512, "w_proj_b": 512, "w_proj_c": 512,
             "w_out": 256}
ADAM_W_IN_COLS = 331

WEIGHT_NAMES = ("w_ada", "b_ada", "norm_g", "w_in", "q_norm_g", "k_norm_g", "sinks", "dw_w", "dw_b", "ln_g", "ln_b",
                "pw2_w", "pw2_b", "sconv_w", "a_log", "dt_bias", "dn_norm_g", "w_proj_a", "w_proj_b", "w_proj_c", "w_out")


def kernel(x, c, w_ada, b_ada, norm_g, w_in, q_norm_g, k_norm_g, sinks, dw_w, dw_b, ln_g, ln_b, pw2_w, pw2_b, sconv_w, a_log, dt_bias, dn_norm_g, w_proj_a, w_proj_b, w_proj_c, w_out, loss_target, m_w_ada, m_b_ada, m_norm_g, m_w_in, m_q_norm_g, m_k_norm_g, m_sinks, m_dw_w, m_dw_b, m_ln_g, m_ln_b, m_pw2_w, m_pw2_b, m_sconv_w, m_a_log, m_dt_bias, m_dn_norm_g, m_w_proj_a, m_w_proj_b, m_w_proj_c, m_w_out, v_w_ada, v_b_ada, v_norm_g, v_w_in, v_q_norm_g, v_k_norm_g, v_sinks, v_dw_w, v_dw_b, v_ln_g, v_ln_b, v_pw2_w, v_pw2_b, v_sconv_w, v_a_log, v_dt_bias, v_dn_norm_g, v_w_proj_a, v_w_proj_b, v_w_proj_c, v_w_out):
    args = dict(locals())
    w = {n: args[n] for n in WEIGHT_NAMES}
    mom = {n: args["m_" + n] for n in WEIGHT_NAMES}
    var = {n: args["v_" + n] for n in WEIGHT_NAMES}

    chip = 2 * lax.axis_index("x") + lax.axis_index("y")
    own = {n: w[n] if n in GATHER_F32 else w[n].astype(BF16) for n in GATHERED}
    own["w_in"] = lax.switch(chip, [functools.partial(_w_in_send, k) for k in range(N_CHIPS)],
                             jnp.transpose(own["w_in"], (0, 2, 1)))
    slots = [[lax.dynamic_update_slice(lax.empty((N_CHIPS,) + own[n].shape[1:], own[n].dtype), own[n][l][None], (chip, 0, 0))
              for n in GATHERED] for l in range(DEPTH)]

    def layer_operands(l, gathered):
        lw = {n: w[n][l] for n in SMALL}
        lw.update({n: g if n == "w_in" else _join_layer(g, SHARDED[n]) for n, g in zip(GATHERED, gathered)})
        return prep_layer(lw)

    layers = [layer_operands(0, run_carry("weights_allgather0", carry_allgather(0, slots[0]))), None]

    mod, conds = ada_fwd(jnp.tile(c, (8, 1)), w["w_ada"], w["b_ada"])
    saved = [None] * DEPTH
    big = GATHERED.index("w_in")
    rest = [i for i in range(len(GATHERED)) if i != big]
    act, saved[0], got_big, got_rest = layer_fwd(
        "0", x[0], mod[0:1], layers[0], carry_inproj=carry_allgather(1, [slots[1][big]]),
        carry_merge=carry_allgather(1, [slots[1][i] for i in rest]))
    gathered1 = dict(zip(rest, got_rest))
    gathered1[big] = got_big[0]
    layers[1] = layer_operands(1, [gathered1[i] for i in range(len(GATHERED))])
    act, saved[1], _, _ = layer_fwd("1", act, mod[1:2], layers[1])
    dact, loss_part = loss_head("loss_head", act, loss_target[0], TM)
    loss = lax.psum(loss_part[0, 0], ("x", "y", "c"))
    layer_grads = [None] * DEPTH
    dact, layer_grads[1], _, _, _ = layer_bwd("1", dact, layers[1], saved[1])
    gs1 = grads_by_chip(layer_grads[1])
    gs0 = []

    def hand_over_layer0(reduced):
        gs0.extend(grads_by_chip(reduced))
        return carry_pair_send(0, gs0)

    dact, layer_grads[0], recv1, got1, recv0 = layer_bwd(
        "0", dact, layers[0], saved[0], carry_merge=carry_pair_send(1, gs1),
        carry_delta=lambda recv: carry_chip_exchange(1, grads_pair_sums(1, gs1, recv)), carry_dh=hand_over_layer0)

    got0 = run_carry("grads_chip_exchange0", carry_chip_exchange(0, grads_pair_sums(0, gs0, recv0)))
    reds = [grads_chip_sum(f"grads_chip_sum1_{n}", 1, g, r, r2) for n, g, r, r2 in zip(REDUCE_BIG, gs1, recv1, got1)]
    reds = [grads_chip_sum(f"grads_chip_sum0_{n}", 0, g, r, r2, into=red)
            for n, g, r, r2, red in zip(REDUCE_BIG, gs0, recv0, got0, reds)]
    final_grads = dict(zip(REDUCE_BIG, grads_pair_gather(reds)))
    final_grads["w_in"] = lax.switch(chip, [functools.partial(_w_in_receive_grad, k) for k in range(N_CHIPS)],
                                     final_grads["w_in"])
    small_names = SMALL + GATHER_F32
    small_shapes = {n: (DEPTH,) + layer_grads[0][n].shape for n in small_names}
    small_full = {n: jnp.stack([layer_grads[l][n] for l in range(DEPTH)]) for n in small_names}
    small_sum, small_all = small_allreduce(pack_small(small_full, small_names, SMALL_GRAD_ROWS))
    small_sum = unpack_small(small_sum, small_names, small_shapes)
    for n in GATHER_F32:
        width = w[n].shape[2]
        final_grads[n] = lax.dynamic_slice_in_dim(small_sum[n], chip * width, width, axis=2)
    n_mod = DEPTH * 3 * D_MODEL
    dmod = small_all.reshape(8, -1)[:, :n_mod].reshape(8, DEPTH, 3 * D_MODEL)
    width = w["w_ada"].shape[2]
    dmod = jnp.transpose(lax.dynamic_slice_in_dim(dmod, chip * width, width, axis=2), (1, 0, 2))
    final_grads["w_ada"] = ada_bwd(conds, dmod)
    final_grads.update({n: small_sum[n] for n in SMALL})
    small_grads = pack_small(final_grads, SMALL, SMALL_ROWS)

    delta, new_m, new_v = {}, {}, {}
    for n in SHARDED:
        shp = w[n].shape
        if n == "w_in":
            view = lambda a: jnp.transpose(a, (2, 0, 1))
            back = lambda a: jnp.transpose(a, (1, 2, 0))
            g3 = view(final_grads[n])
            final_grads[n] = back(g3)
            d, nm, nv = adamw("adamw_" + n, view(w[n]), g3, view(mom[n]), view(var[n]), (ADAM_W_IN_COLS, shp[0], shp[1]))
        else:
            view = lambda a, shp=shp: a.reshape(shp[0] * shp[1], shp[2])
            back = lambda a, shp=shp: a.reshape(shp)
            d, nm, nv = adamw("adamw_" + n, view(w[n]), view(final_grads[n]), view(mom[n]), view(var[n]),
                              (ADAM_ROWS[n], shp[2]))
        delta[n], new_m[n], new_v[n] = back(d), back(nm), back(nv)
    d, nm, nv = adamw("adamw_small", pack_small(w, SMALL, SMALL_ROWS), small_grads, pack_small(mom, SMALL, SMALL_ROWS),
                      pack_small(var, SMALL, SMALL_ROWS), (SMALL_ROWS, 128))
    delta.update(unpack_small(d, SMALL, small_shapes))
    new_m.update(unpack_small(nm, SMALL, small_shapes))
    new_v.update(unpack_small(nv, SMALL, small_shapes))

    return (loss, dact[None], *[final_grads[n] for n in WEIGHT_NAMES], *[delta[n] for n in WEIGHT_NAMES],
            *[new_m[n] for n in WEIGHT_NAMES], *[new_v[n] for n in WEIGHT_NAMES])
```

```python
import functools

import numpy as np
import jax
import jax.numpy as jnp
from jax import lax
from jax.experimental import pallas as pl
from jax.experimental.pallas import tpu as pltpu

F32 = jnp.float32
BF16 = jnp.bfloat16
MESH = pl.DeviceIdType.MESH

D_MODEL = 1024
DEPTH = 2
ATT_HEADS = 8
ATT_HEAD_DIM = 64
WINDOW = 128
CONV_K = 31
DN_HEADS = 4
DN_CONV_K = 4
DN_CHUNK = 64
EPS = 1e-6
NEG_INF = -1e30
N_CHIPS = 4
D_IN = 7944

ADAM_LR = 0.001
ADAM_B1 = 0.9
ADAM_B2 = 0.999
ADAM_EPS = 1e-08
ADAM_WD = 0.01
ADAM_STEP = 10

VMEM_LIMIT = 56 * 1024 * 1024

P_QA, P_ZA, P_GLU, P_ZB, P_ZC, P_MG, P_QKV, P_KA, P_VA, P_AB, P_TOTAL = (
    0, 512, 1024, 2048, 2560, 3072, 6144, 7680, 7808, 7936, 8064)
HEAD_ORDER = (0, 4, 1, 5, 2, 6, 3, 7)


def _in_pieces():
    p = [(0 + 64 * h, 64) for h in HEAD_ORDER]
    p += [(768 + 64 * h, 64) for h in HEAD_ORDER]
    for g in range(4):
        p += [(1280 + 128 * g, 128), (1792 + 128 * g, 128)]
    p += [(2304, 512), (4360, 512), (4872, 3072), (2816, 1536), (512, 128), (640, 128), (4352, 8)]
    return p


def _perm_heads_rows(w):
    return jnp.concatenate([w[64 * h:64 * h + 64] for h in HEAD_ORDER], axis=0)


def _unperm_heads_rows(w):
    inv = [HEAD_ORDER.index(h) for h in range(8)]
    return jnp.concatenate([w[64 * s:64 * s + 64] for s in inv], axis=0)


def _split_bf16(a, terms):
    out, rest = [], a.astype(F32)
    for _ in range(terms - 1):
        out.append(rest.astype(BF16))
        rest = rest - out[-1].astype(F32)
    return out + [rest.astype(BF16)]


def _dot(a, b, dims, exact):
    d = lambda p, q: lax.dot_general(p, q, (dims, ((), ())), preferred_element_type=F32)
    if exact:
        (ah, al), (bh, bl) = _split_bf16(a, 2), _split_bf16(b, 2)
        return d(ah, bh) + (d(ah, bl) + d(al, bh))
    return d(a.astype(BF16), b.astype(BF16))


def _make_mm(exact):
    @jax.custom_vjp
    def nn(a, b):
        return _dot(a, b, ((1,), (0,)), exact)

    @jax.custom_vjp
    def nt(a, b):
        return _dot(a, b, ((1,), (1,)), exact)

    @jax.custom_vjp
    def tn(a, b):
        return _dot(a, b, ((0,), (0,)), exact)

    nn.defvjp(lambda a, b: (nn(a, b), (a, b)),
              lambda r, g: (nt(g, r[1]).astype(r[0].dtype), tn(r[0], g).astype(r[1].dtype)))
    nt.defvjp(lambda a, b: (nt(a, b), (a, b)),
              lambda r, g: (nn(g, r[1]).astype(r[0].dtype), tn(g, r[0]).astype(r[1].dtype)))
    tn.defvjp(lambda a, b: (tn(a, b), (a, b)),
              lambda r, g: (nt(r[1], g).astype(r[0].dtype), nn(r[0], g).astype(r[1].dtype)))
    return nn, nt, tn


mm, mm_nt, mm_tn = _make_mm(False)
xmm, xmm_nt, xmm_tn = _make_mm(True)


@jax.custom_vjp
def sel_mm(m, g):
    mb = m.astype(BF16)
    parts = [jnp.dot(mb, p, preferred_element_type=F32) for p in _split_bf16(g, 3)]
    return parts[0] + (parts[1] + parts[2])


def _sel_mm_bwd(m, dy):
    mb = m.astype(BF16)
    parts = [lax.dot_general(mb, p, (((0,), (0,)), ((), ())), preferred_element_type=F32) for p in _split_bf16(dy, 3)]
    return jnp.zeros_like(m), parts[0] + (parts[1] + parts[2])


sel_mm.defvjp(lambda m, g: (sel_mm(m, g), m), _sel_mm_bwd)


@jax.custom_vjp
def tri_inv(*mats):
    n = mats[0].shape[0]
    eye = jnp.where(lax.broadcasted_iota(jnp.int32, (n, n), 0) == lax.broadcasted_iota(jnp.int32, (n, n), 1), 1.0, 0.0)
    ts = [eye - a for a in mats]
    pws = list(mats)
    for _ in range(5):
        pws = [xmm(pw, pw) for pw in pws]
        ts = [t + xmm(t, pw) for t, pw in zip(ts, pws)]
    return tuple(ts)


def _tri_inv_bwd(ts, dts):
    inner = [xmm_nt(dt, t) for t, dt in zip(ts, dts)]
    return tuple(-xmm_tn(t, m) for t, m in zip(ts, inner))


tri_inv.defvjp(lambda *mats: (tri_inv(*mats),) * 2, _tri_inv_bwd)


@jax.custom_vjp
def tri_inv_known(a, t):
    return t


tri_inv_known.defvjp(lambda a, t: (t, t), lambda t, dt: (_tri_inv_bwd((t,), (dt,))[0], jnp.zeros_like(t)))


def _sigmoid(x):
    return 1.0 / (1.0 + jnp.exp(-x))


def _silu(x):
    return x * _sigmoid(x)


def _softplus(x):
    return jnp.maximum(x, 0.0) + jnp.log(1.0 + jnp.exp(-jnp.abs(x)))


def _cparams(n_grid):
    return pltpu.CompilerParams(dimension_semantics=("arbitrary",) * n_grid, vmem_limit_bytes=VMEM_LIMIT)


def _row_spec(tm, width, colblk):
    return pl.BlockSpec((tm, width), lambda i, cb=colblk: (i, cb))


def _const_spec(shape):
    nd = len(shape)
    return pl.BlockSpec(tuple(shape), lambda i, nd=nd: (0,) * nd)


def rowwise_fwd(name, f, rows, consts, outs, tm, carry=None):
    n_r, n_c = len(rows), len(consts)
    t = rows[0][0].shape[0]
    c_ins, c_in_specs, c_outs, c_out_specs, c_scratch = _host(carry)
    n_in, n_ci, n_co = n_r + n_c, len(c_ins), len(c_outs)

    def body(*refs):
        carried = (refs[n_in:n_in + n_ci], refs[n_in + n_ci + len(outs):n_in + n_ci + len(outs) + n_co],
                   *refs[n_in + n_ci + len(outs) + n_co:])
        if carry is not None:
            carry.emit_start(pl.program_id(0) == 0, *carried)
        vals = [r[...] for r in refs[:n_in]]
        res = f(*vals)
        if not isinstance(res, (tuple, list)):
            res = (res,)
        for o_ref, v, out in zip(refs[n_in + n_ci:n_in + n_ci + len(outs)], res, outs):
            o_ref[...] = (v.T if len(out) == 3 else v).astype(o_ref.dtype)
        if carry is not None:
            carry.emit_finish(pl.program_id(0) == t // tm - 1, *carried)

    return pl.pallas_call(
        body, name=name, grid=(t // tm,),
        in_specs=[_row_spec(tm, w, cb) for _, w, cb in rows] + [_const_spec(c.shape) for c in consts] + c_in_specs,
        out_specs=[_row_spec(tm, o[0], 0) if len(o) == 2 else pl.BlockSpec((o[0], tm), lambda i: (0, i)) for o in outs]
        + c_out_specs,
        out_shape=[jax.ShapeDtypeStruct((t, o[0]) if len(o) == 2 else (o[0], t), o[1]) for o in outs] + c_outs,
        input_output_aliases={} if carry is None else carry.aliases(n_in, len(outs)),
        scratch_shapes=c_scratch,
        compiler_params=_cparams(1),
    )(*[a for a, _, _ in rows], *consts, *c_ins)


def rowwise_bwd(name, f, rows, consts, cts, row_grad_dtypes, tm, carry=None):
    n_r, n_c, n_ct = len(rows), len(consts), len(cts)
    t = rows[0][0].shape[0]
    keep = [k for k, dt in enumerate(row_grad_dtypes) if dt is not None]
    c_ins, c_in_specs, c_outs, c_out_specs, c_scratch = _host(carry)
    n_in, n_out = n_r + n_c + n_ct, len(keep) + n_c

    def body(*refs):
        ins = [r[...].astype(F32) for r in refs[:n_r + n_c]]
        g_out = [r[...].astype(F32) for r in refs[n_r + n_c:n_in]]
        out_refs = refs[n_in + len(c_ins):n_in + len(c_ins) + n_out]
        carried = (refs[n_in:n_in + len(c_ins)], refs[n_in + len(c_ins) + n_out:n_in + len(c_ins) + n_out + len(c_outs)],
                   *refs[n_in + len(c_ins) + n_out + len(c_outs):])
        if carry is not None:
            carry.emit_start(pl.program_id(0) == 0, *carried)

        def fw(*a):
            res = f(*a)
            return tuple(res) if isinstance(res, (tuple, list)) else (res,)

        _, vjp = jax.vjp(fw, *ins)
        grads = vjp(tuple(g_out))
        for o_ref, k in zip(out_refs[:len(keep)], keep):
            o_ref[...] = grads[k].astype(o_ref.dtype)
        first = pl.program_id(0) == 0
        for o_ref, g in zip(out_refs[len(keep):], grads[n_r:]):
            @pl.when(first)
            def _(o_ref=o_ref, g=g):
                o_ref[...] = g

            @pl.when(jnp.logical_not(first))
            def _(o_ref=o_ref, g=g):
                o_ref[...] += g
        if carry is not None:
            carry.emit_finish(pl.program_id(0) == t // tm - 1, *carried)

    return pl.pallas_call(
        body, name=name, grid=(t // tm,),
        in_specs=[_row_spec(tm, w, cb) for _, w, cb in rows] + [_const_spec(c.shape) for c in consts]
        + [_row_spec(tm, w, cb) for _, w, cb in cts] + c_in_specs,
        out_specs=[_row_spec(tm, rows[k][1], 0) for k in keep] + [_const_spec(c.shape) for c in consts] + c_out_specs,
        out_shape=[jax.ShapeDtypeStruct((t, rows[k][1]), row_grad_dtypes[k]) for k in keep]
        + [jax.ShapeDtypeStruct(c.shape, F32) for c in consts] + c_outs,
        scratch_shapes=c_scratch,
        compiler_params=_cparams(1),
    )(*[a for a, _, _ in rows], *consts, *[a for a, _, _ in cts], *c_ins)


def f_norm_mod(x, g, scale, shift):
    y = x * lax.rsqrt(jnp.mean(x * x, axis=-1, keepdims=True) + EPS) * g
    return y * (1.0 + scale) + shift


def f_conf_tail(u, zb, ln_g, ln_b, pw2_w, pw2_b):
    mu = jnp.mean(u, axis=-1, keepdims=True)
    xc = u - mu
    var = jnp.mean(xc * xc, axis=-1, keepdims=True)
    y = _silu(xc * lax.rsqrt(var + EPS) * ln_g + ln_b)
    return (mm(y, pw2_w) + pw2_b) * _silu(zb)


def f_merge(ya, yb, yc, mg, x, gate, wpa, wpb, wpc, wout):
    d = D_MODEL
    merged = (_sigmoid(mg[:, :d]) * mm(ya, wpa) + _sigmoid(mg[:, d:2 * d]) * mm(yb, wpb)
              + _sigmoid(mg[:, 2 * d:]) * mm(yc, wpc))
    return x + gate * mm(merged, wout)


def matmul_nn(name, a, b, out_dtype, tm, tn, tk, b_transposed=False, carry=None):
    m, k = a.shape
    n = b.shape[0] if b_transposed else b.shape[1]
    nk = k // tk
    grid = (m // tm, n // tn, nk)
    b_spec = (pl.BlockSpec((tn, tk), lambda i, j, kk: (j, kk)) if b_transposed
              else pl.BlockSpec((tk, tn), lambda i, j, kk: (kk, j)))
    c_ins, c_in_specs, c_outs, c_out_specs, c_scratch = _host(carry)
    n_ci, n_co = len(c_ins), len(c_outs)

    def body(*refs):
        a_ref, b_ref, o_ref = refs[0], refs[1], refs[2 + n_ci]
        carried = (refs[2:2 + n_ci], refs[3 + n_ci:3 + n_ci + n_co], *refs[3 + n_ci + n_co:3 + n_ci + n_co + len(c_scratch)])
        at = lambda step: functools.reduce(jnp.logical_and, [pl.program_id(d) == s for d, s in enumerate(step)])
        if carry is not None:
            carry.emit_start(at((0, 0, 0)), *carried)
        part = lax.dot_general(a_ref[...].astype(BF16), b_ref[...].astype(BF16),
                               (((1,), (1 if b_transposed else 0,)), ((), ())), preferred_element_type=F32)
        if nk == 1:
            o_ref[...] = part.astype(o_ref.dtype)
        else:
            kk = pl.program_id(2)
            acc_ref = refs[-1]

            @pl.when(kk == 0)
            def _():
                acc_ref[...] = part

            @pl.when(kk > 0)
            def _():
                acc_ref[...] += part

            @pl.when(kk == nk - 1)
            def _():
                o_ref[...] = acc_ref[...].astype(o_ref.dtype)
        if carry is not None:
            carry.emit_finish(at(tuple(g - 1 for g in grid)), *carried)

    res = pl.pallas_call(
        body, name=name, grid=grid,
        in_specs=[pl.BlockSpec((tm, tk), lambda i, j, kk: (i, kk)), b_spec] + c_in_specs,
        out_specs=[pl.BlockSpec((tm, tn), lambda i, j, kk: (i, j))] + c_out_specs,
        out_shape=[jax.ShapeDtypeStruct((m, n), out_dtype)] + c_outs,
        input_output_aliases={} if carry is None else carry.aliases(2, 1),
        scratch_shapes=c_scratch + ([] if nk == 1 else [pltpu.VMEM((tm, tn), F32)]),
        compiler_params=_cparams(3),
    )(a, b, *c_ins)
    return res[0] if carry is None else res


def ada_fwd(c8, w_shard, b_ada):
    n_cols = w_shard.shape[2]
    masks = [(m >> 2 & 1, m >> 1 & 1, m & 1) for m in range(1, 8)]

    def body(c_ref, w_ref, b_ref, mod_ref, conds_ref, cbuf, sendbuf, recvbuf, send_sems, recv_sems):
        x, y, c, chips = _place()
        flip = lambda v, bit: 1 - v if bit else v
        peers = [(flip(x, mx), flip(y, my), flip(c, mc)) for mx, my, mc in masks]
        dev = lambda p: 4 * p[0] + 2 * p[1] + p[2]
        cbuf[dev((x, y, c))] = c_ref[...]
        first = [_remote(c_ref, cbuf.at[dev((x, y, c))], send_sems.at[i], recv_sems.at[i], p) for i, p in enumerate(peers)]
        for cp in first:
            cp.start()
        for i, p in enumerate(peers):
            _remote(c_ref, cbuf.at[dev(p)], send_sems.at[i], recv_sems.at[i], p).wait_recv()
        conds = jnp.concatenate([cbuf[d, 0:1, :] for d in range(8)], axis=0)
        conds_ref[...] = conds
        act = _silu(conds)
        parts = [mm(act, w_ref[l]) for l in range(DEPTH)]
        row8 = lax.broadcasted_iota(jnp.int32, (8, 1), 0)

        def tile_for(chip):
            r = 2 * (2 * chip[0] + chip[1]) + c
            rows = [jnp.sum(jnp.where(row8 == r, parts[l], 0.0), axis=0, keepdims=True) for l in range(DEPTH)]
            return jnp.where(row8 == 0, rows[0], jnp.where(row8 == 1, rows[1], 0.0))

        my_slot = 2 * x + y
        recvbuf[my_slot] = tile_for((x, y))
        second = []
        for j, chip in enumerate(chips):
            sendbuf[j] = tile_for(chip)
            second.append(_remote(sendbuf.at[j], recvbuf.at[my_slot], send_sems.at[7 + j], recv_sems.at[7 + j], (*chip, c)))
            second[-1].start()
        for j, chip in enumerate(chips):
            _remote(sendbuf.at[j], recvbuf.at[2 * chip[0] + chip[1]], send_sems.at[7 + j], recv_sems.at[7 + j],
                    (*chip, c)).wait_recv()
        rows = [jnp.concatenate([recvbuf[k, l:l + 1, :] for k in range(N_CHIPS)], axis=1) + b_ref[l:l + 1, :]
                for l in range(DEPTH)]
        mod_ref[...] = jnp.concatenate(rows + [jnp.zeros((8 - DEPTH, N_CHIPS * n_cols), F32)], axis=0)
        for cp in first + second:
            cp.wait_send()

    vm = pl.BlockSpec(memory_space=pltpu.VMEM)
    return pl.pallas_call(
        body, name="ada_fwd",
        out_shape=[jax.ShapeDtypeStruct((8, N_CHIPS * n_cols), F32), jax.ShapeDtypeStruct((8, D_MODEL), F32)],
        in_specs=[vm, vm, vm], out_specs=[vm, vm],
        scratch_shapes=[pltpu.VMEM((8, 8, D_MODEL), F32), pltpu.VMEM((3, 8, n_cols), F32),
                        pltpu.VMEM((N_CHIPS, 8, n_cols), F32), pltpu.SemaphoreType.DMA((10,)), pltpu.SemaphoreType.DMA((10,))],
        compiler_params=pltpu.CompilerParams(vmem_limit_bytes=VMEM_LIMIT),
    )(c8, w_shard, b_ada)


def ada_bwd(conds, dmod):
    def body(c_ref, d_ref, o_ref):
        act = _silu(c_ref[...])
        for l in range(DEPTH):
            o_ref[l] = mm_tn(act, d_ref[l])

    return pl.pallas_call(
        body, name="ada_bwd", out_shape=jax.ShapeDtypeStruct((DEPTH, D_MODEL, dmod.shape[2]), F32),
        compiler_params=pltpu.CompilerParams(vmem_limit_bytes=VMEM_LIMIT),
    )(conds, dmod)


def _f_attn(first_block, q, za, kc, vc, kp, vp, qg, kg, sinks):
    w = WINDOW
    lane = lax.broadcasted_iota(jnp.int32, (1, 128), 1)
    halves = [lane < 64, lane >= 64]

    def rms_halves(x, g):
        x2 = x * x
        s0 = jnp.sum(jnp.where(halves[0], x2, 0.0), axis=-1, keepdims=True)
        s1 = jnp.sum(jnp.where(halves[1], x2, 0.0), axis=-1, keepdims=True)
        r = jnp.where(halves[0], lax.rsqrt(s0 / 64.0 + EPS), lax.rsqrt(s1 / 64.0 + EPS))
        return x * r * g

    kcat = rms_halves(jnp.concatenate([kp, kc], axis=0), kg)
    vcat = jnp.concatenate([vp, vc], axis=0)
    qi = lax.broadcasted_iota(jnp.int32, (w, 2 * w), 0)
    kj = lax.broadcasted_iota(jnp.int32, (w, 2 * w), 1)
    dist = qi + w - kj
    valid = (dist >= 0) & (dist < w) & (jnp.logical_not(first_block) | (kj >= w))
    distf = dist.astype(F32)
    units = [(grp, half) for grp in range(4) for half in range(2)]
    qns = [rms_halves(q[:, 128 * grp:128 * grp + 128], qg) * (ATT_HEAD_DIM ** -0.5) for grp in range(4)]
    vhalf = [jnp.where(halves[half], vcat, 0.0) for half in range(2)]
    scores, sinks_h = [], []
    for grp, half in units:
        head = HEAD_ORDER[2 * grp + half]
        slope = 2.0 ** (-8.0 * (head + 1) / ATT_HEADS)
        sinks_h.append(jnp.sum(jnp.where(lane == head, sinks, 0.0), axis=-1, keepdims=True))
        s = mm_nt(jnp.where(halves[half], qns[grp], 0.0), kcat) - slope * distf
        scores.append(jnp.where(valid, s, NEG_INF))
    probs = []
    for s, sink in zip(scores, sinks_h):
        m = lax.stop_gradient(jnp.maximum(jnp.max(s, axis=-1, keepdims=True), sink))
        p = jnp.exp(s - m)
        probs.append(p / (jnp.sum(p, axis=-1, keepdims=True) + jnp.exp(sink - m)))
    outs = [mm(p, vhalf[half]) for p, (grp, half) in zip(probs, units)]
    return jnp.concatenate([outs[2 * grp] + outs[2 * grp + 1] for grp in range(4)], axis=1) * _silu(za)


def attn_fwd(name, proj, qg, kg, sinks):
    t = proj.shape[0]
    nb = t // WINDOW

    def body(q_ref, za_ref, kc_ref, vc_ref, kp_ref, vp_ref, qg_ref, kg_ref, s_ref, o_ref):
        first = pl.program_id(0) == 0
        o_ref[...] = _f_attn(first, q_ref[...], za_ref[...], kc_ref[...], vc_ref[...], kp_ref[...], vp_ref[...],
                             qg_ref[...], kg_ref[...], s_ref[...])

    cur = lambda cb: (lambda i: (i, cb))
    prev = lambda cb: (lambda i: (jnp.maximum(i - 1, 0), cb))
    return pl.pallas_call(
        body, name=name, grid=(nb,),
        in_specs=[pl.BlockSpec((WINDOW, 512), cur(P_QA // 512)), pl.BlockSpec((WINDOW, 512), cur(P_ZA // 512)),
                  pl.BlockSpec((WINDOW, 128), cur(P_KA // 128)), pl.BlockSpec((WINDOW, 128), cur(P_VA // 128)),
                  pl.BlockSpec((WINDOW, 128), prev(P_KA // 128)), pl.BlockSpec((WINDOW, 128), prev(P_VA // 128)),
                  _const_spec((1, 128)), _const_spec((1, 128)), _const_spec((1, 128))],
        out_specs=pl.BlockSpec((WINDOW, 512), lambda i: (i, 0)),
        out_shape=jax.ShapeDtypeStruct((t, 512), F32),
        compiler_params=_cparams(1),
    )(proj, proj, proj, proj, proj, proj, qg, kg, sinks)


def attn_bwd(name, proj, qg, kg, sinks, dya):
    t = proj.shape[0]
    nb = t // WINDOW

    def body(q_ref, za_ref, kc_ref, vc_ref, kp_ref, vp_ref, qg_ref, kg_ref, s_ref, dy_ref,
             dqz_ref, dkv_ref, dqg_ref, dkg_ref, ds_ref, carry_ref):
        j = pl.program_id(0)
        first = j == nb - 1

        @pl.when(j == 0)
        def _():
            carry_ref[...] = jnp.zeros_like(carry_ref)
            dqg_ref[...] = jnp.zeros_like(dqg_ref)
            dkg_ref[...] = jnp.zeros_like(dkg_ref)
            ds_ref[...] = jnp.zeros_like(ds_ref)

        ins = [r[...] for r in (q_ref, za_ref, kc_ref, vc_ref, kp_ref, vp_ref, qg_ref, kg_ref, s_ref)]
        _, vjp = jax.vjp(functools.partial(_f_attn, first), *ins)
        dq, dza, dkc, dvc, dkp, dvp, dqg, dkg, dsk = vjp(dy_ref[...])
        dqz_ref[:, 0:512] = dq.astype(dqz_ref.dtype)
        dqz_ref[:, 512:1024] = dza.astype(dqz_ref.dtype)
        dkv_ref[:, 0:128] = (dkc + carry_ref[0]).astype(dkv_ref.dtype)
        dkv_ref[:, 128:256] = (dvc + carry_ref[1]).astype(dkv_ref.dtype)
        carry_ref[0] = dkp
        carry_ref[1] = dvp
        dqg_ref[...] += dqg
        dkg_ref[...] += dkg
        ds_ref[...] += dsk

    cur = lambda cb: (lambda j: (nb - 1 - j, cb))
    prev = lambda cb: (lambda j: (jnp.maximum(nb - 2 - j, 0), cb))
    return pl.pallas_call(
        body, name=name, grid=(nb,),
        in_specs=[pl.BlockSpec((WINDOW, 512), cur(P_QA // 512)), pl.BlockSpec((WINDOW, 512), cur(P_ZA // 512)),
                  pl.BlockSpec((WINDOW, 128), cur(P_KA // 128)), pl.BlockSpec((WINDOW, 128), cur(P_VA // 128)),
                  pl.BlockSpec((WINDOW, 128), prev(P_KA // 128)), pl.BlockSpec((WINDOW, 128), prev(P_VA // 128)),
                  _const_spec((1, 128)), _const_spec((1, 128)), _const_spec((1, 128)),
                  pl.BlockSpec((WINDOW, 512), cur(0))],
        out_specs=[pl.BlockSpec((WINDOW, 1024), cur(0)), pl.BlockSpec((WINDOW, 256), cur(0)),
                   _const_spec((1, 128)), _const_spec((1, 128)), _const_spec((1, 128))],
        out_shape=[jax.ShapeDtypeStruct((t, 1024), BF16), jax.ShapeDtypeStruct((t, 256), BF16),
                   jax.ShapeDtypeStruct((1, 128), F32), jax.ShapeDtypeStruct((1, 128), F32),
                   jax.ShapeDtypeStruct((1, 128), F32)],
        scratch_shapes=[pltpu.VMEM((2, WINDOW, 128), F32)],
        compiler_params=_cparams(1),
    )(proj, proj, proj, proj, proj, proj, qg, kg, sinks, dya)


CONV_ROWS = 256


def _conv_taps(src_ref, w_ref, n_taps, base, t):
    for r0 in range(0, t, CONV_ROWS):
        acc = w_ref[0:1, :] * src_ref[pl.ds(r0 + base, CONV_ROWS), :]
        for k in range(1, n_taps):
            acc = acc + w_ref[k:k + 1, :] * src_ref[pl.ds(r0 + base + k, CONV_ROWS), :]
        yield r0, acc


def _conv_wgrad(dy_ref, src_ref, n_taps, base, t, dy_base=0):
    out = []
    for k in range(n_taps):
        acc = jnp.zeros((8, 128), F32)
        for r0 in range(0, t, CONV_ROWS):
            prod = dy_ref[pl.ds(r0 + dy_base, CONV_ROWS), :] * src_ref[pl.ds(r0 + base + k, CONV_ROWS), :]
            acc = acc + jnp.sum(prod.reshape(CONV_ROWS // 8, 8, 128), axis=0)
        out.append(jnp.sum(acc, axis=0, keepdims=True))
    return out


def glu_conv_fwd(name, proj, w32, bias):
    t = proj.shape[0]
    pad = 32

    def body(x_ref, w_ref, b_ref, o_ref, u_ref):
        u_ref[0:pad, :] = jnp.zeros((pad, 128), F32)
        u_ref[pad:pad + t, :] = x_ref[:, 0:128] * _sigmoid(x_ref[:, 128:256])
        for r0, acc in _conv_taps(u_ref, w_ref, CONV_K, pad - (CONV_K - 1), t):
            o_ref[pl.ds(r0, CONV_ROWS), :] = acc + b_ref[...]

    return pl.pallas_call(
        body, name=name, grid=(4,),
        in_specs=[pl.BlockSpec((t, 256), lambda cb: (0, P_GLU // 256 + cb)), pl.BlockSpec((32, 128), lambda cb: (0, cb)),
                  pl.BlockSpec((1, 128), lambda cb: (0, cb))],
        out_specs=pl.BlockSpec((t, 128), lambda cb: (0, cb)),
        out_shape=jax.ShapeDtypeStruct((t, 512), F32),
        scratch_shapes=[pltpu.VMEM((t + pad, 128), F32)],
        compiler_params=_cparams(1),
    )(proj, w32, bias)


def glu_conv_bwd(name, proj, w32, dub):
    t = proj.shape[0]
    pad = 32
    k1 = CONV_K - 1

    def body(x_ref, w_ref, dy_ref, dx_ref, dw_ref, db_ref, u_ref, dyp_ref, wrev_ref):
        val = x_ref[:, 0:128]
        sg = _sigmoid(x_ref[:, 128:256])
        u_ref[0:pad, :] = jnp.zeros((pad, 128), F32)
        u_ref[pad:pad + t, :] = val * sg
        dyp_ref[0:t, :] = dy_ref[...]
        dyp_ref[t:t + pad, :] = jnp.zeros((pad, 128), F32)
        for k in range(CONV_K):
            wrev_ref[k:k + 1, :] = w_ref[k1 - k:k1 - k + 1, :]
        wrev_ref[CONV_K:32, :] = jnp.zeros((32 - CONV_K, 128), F32)
        for r0, du in _conv_taps(dyp_ref, wrev_ref, CONV_K, 0, t):
            v = x_ref[pl.ds(r0, CONV_ROWS), 0:128]
            s = _sigmoid(x_ref[pl.ds(r0, CONV_ROWS), 128:256])
            dx_ref[pl.ds(r0, CONV_ROWS), 0:128] = (du * s).astype(dx_ref.dtype)
            dx_ref[pl.ds(r0, CONV_ROWS), 128:256] = (du * v * s * (1.0 - s)).astype(dx_ref.dtype)
        dws = _conv_wgrad(dyp_ref, u_ref, CONV_K, pad - k1, t)
        for k in range(CONV_K):
            dw_ref[k:k + 1, :] = dws[k]
        dw_ref[CONV_K:32, :] = jnp.zeros((32 - CONV_K, 128), F32)
        db_ref[...] = jnp.sum(dy_ref[...], axis=0, keepdims=True)

    return pl.pallas_call(
        body, name=name, grid=(4,),
        in_specs=[pl.BlockSpec((t, 256), lambda cb: (0, P_GLU // 256 + cb)), pl.BlockSpec((32, 128), lambda cb: (0, cb)),
                  pl.BlockSpec((t, 128), lambda cb: (0, cb))],
        out_specs=[pl.BlockSpec((t, 256), lambda cb: (0, cb)), pl.BlockSpec((32, 128), lambda cb: (0, cb)),
                   pl.BlockSpec((1, 128), lambda cb: (0, cb))],
        out_shape=[jax.ShapeDtypeStruct((t, 1024), BF16), jax.ShapeDtypeStruct((32, 512), F32),
                   jax.ShapeDtypeStruct((1, 512), F32)],
        scratch_shapes=[pltpu.VMEM((t + pad, 128), F32), pltpu.VMEM((t + pad, 128), F32), pltpu.VMEM((32, 128), F32)],
        compiler_params=_cparams(1),
    )(proj, w32, dub)


def sconv_fwd(name, proj, w8):
    t = proj.shape[0]
    pad = 8
    k1 = DN_CONV_K - 1

    def body(x_ref, w_ref, o_ref, xp_ref):
        xp_ref[0:pad, :] = jnp.zeros((pad, 128), F32)
        xp_ref[pad:pad + t, :] = x_ref[...]
        for r0, acc in _conv_taps(xp_ref, w_ref, DN_CONV_K, pad - k1, t):
            o_ref[pl.ds(r0, CONV_ROWS), :] = _silu(acc)

    return pl.pallas_call(
        body, name=name, grid=(12,),
        in_specs=[pl.BlockSpec((t, 128), lambda cb: (0, P_QKV // 128 + cb)), pl.BlockSpec((8, 128), lambda cb: (0, cb))],
        out_specs=pl.BlockSpec((t, 128), lambda cb: (0, cb)),
        out_shape=jax.ShapeDtypeStruct((t, 1536), F32),
        scratch_shapes=[pltpu.VMEM((t + pad, 128), F32)],
        compiler_params=_cparams(1),
    )(proj, w8)


def sconv_bwd(name, proj, w8, dqkv):
    t = proj.shape[0]
    pad = 8
    k1 = DN_CONV_K - 1

    def body(x_ref, w_ref, dy_ref, dx_ref, dw_ref, xp_ref, dpp_ref, wrev_ref):
        xp_ref[0:pad, :] = jnp.zeros((pad, 128), F32)
        xp_ref[pad:pad + t, :] = x_ref[...]
        for r0, pre in _conv_taps(xp_ref, w_ref, DN_CONV_K, pad - k1, t):
            s = _sigmoid(pre)
            dpp_ref[pl.ds(r0, CONV_ROWS), :] = dy_ref[pl.ds(r0, CONV_ROWS), :] * (s * (1.0 + pre * (1.0 - s)))
        dpp_ref[t:t + pad, :] = jnp.zeros((pad, 128), F32)
        for k in range(DN_CONV_K):
            wrev_ref[k:k + 1, :] = w_ref[k1 - k:k1 - k + 1, :]
        wrev_ref[DN_CONV_K:8, :] = jnp.zeros((8 - DN_CONV_K, 128), F32)
        for r0, dx in _conv_taps(dpp_ref, wrev_ref, DN_CONV_K, 0, t):
            dx_ref[pl.ds(r0, CONV_ROWS), :] = dx.astype(dx_ref.dtype)
        dws = _conv_wgrad(dpp_ref, xp_ref, DN_CONV_K, pad - k1, t)
        for k in range(DN_CONV_K):
            dw_ref[k:k + 1, :] = dws[k]
        dw_ref[DN_CONV_K:8, :] = jnp.zeros((8 - DN_CONV_K, 128), F32)

    return pl.pallas_call(
        body, name=name, grid=(12,),
        in_specs=[pl.BlockSpec((t, 128), lambda cb: (0, P_QKV // 128 + cb)), pl.BlockSpec((8, 128), lambda cb: (0, cb)),
                  pl.BlockSpec((t, 128), lambda cb: (0, cb))],
        out_specs=[pl.BlockSpec((t, 128), lambda cb: (0, cb)), pl.BlockSpec((8, 128), lambda cb: (0, cb))],
        out_shape=[jax.ShapeDtypeStruct((t, 1536), BF16), jax.ShapeDtypeStruct((8, 1536), F32)],
        scratch_shapes=[pltpu.VMEM((t + pad, 128), F32), pltpu.VMEM((t + pad, 128), F32), pltpu.VMEM((8, 128), F32)],
        compiler_params=_cparams(1),
    )(proj, w8, dqkv)


def _f_delta_step(qkv, ab, zc, s0, s1, s2, s3, a_log, dt_bias, dn_g, inverses=None, with_inverses=False):
    cs = DN_CHUNK
    n = 2 * cs
    states = (s0, s1, s2, s3)
    lane = lax.broadcasted_iota(jnp.int32, (1, 128), 1)
    ri = lax.broadcasted_iota(jnp.int32, (n, n), 0)
    ci = lax.broadcasted_iota(jnp.int32, (n, n), 1)
    same = (ri // cs) == (ci // cs)
    lower = same & (ri >= ci)
    strict = same & (ri > ci)
    sums = jnp.concatenate([jnp.where(lower, 1.0, 0.0), jnp.where(same, 1.0, 0.0), jnp.where(ci < cs, 1.0, 0.0),
                            jnp.where(ci >= cs, 1.0, 0.0)], axis=0)
    top = lax.broadcasted_iota(jnp.int32, (n, 1), 0) < cs

    def pick(row, idx):
        return jnp.sum(jnp.where(lane == idx, row, 0.0), axis=-1, keepdims=True)

    def l2n(x):
        return x * lax.rsqrt(jnp.sum(x * x, axis=-1, keepdims=True) + EPS)

    n_chunks = qkv.shape[0] // cs
    units = [(k, pair) for k in range(n_chunks) for pair in range(2)]

    pre = []
    for k, pair in units:
        hs = (2 * pair, 2 * pair + 1)
        rows = slice(k * cs, (k + 1) * cs)
        stack = lambda f: jnp.concatenate([f(hs[0]), f(hs[1])], axis=0)
        qd = l2n(stack(lambda h: qkv[rows, 128 * h:128 * h + 128])) * (128 ** -0.5)
        kd = l2n(stack(lambda h: qkv[rows, 512 + 128 * h:512 + 128 * h + 128]))
        vd = stack(lambda h: qkv[rows, 1024 + 128 * h:1024 + 128 * h + 128])
        beta = _sigmoid(stack(lambda h: pick(ab[rows], 4 + h)))
        g = stack(lambda h: -jnp.exp(pick(a_log, h)) * _softplus(pick(ab[rows], h) + pick(dt_bias, h)))
        g_sums = sel_mm(sums, g * jnp.ones((1, n), F32))
        gc_col = g_sums[0:n]
        gl_b = g_sums[n:2 * n]
        g_end = (g_sums[2 * n:3 * n], g_sums[3 * n:])
        decay = jnp.where(lower, jnp.exp(jnp.where(lower, gc_col - gc_col.T, 0.0)), 0.0)
        kb = kd * beta
        pre.append(dict(qd=qd, kd=kd, vb=vd * beta, kb=kb, gc_col=gc_col, gl_b=gl_b, g_end=g_end, decay=decay,
                        a=jnp.where(strict, mm_nt(kb, kd) * decay, 0.0)))
    if inverses is None:
        tmats = tri_inv(*[p["a"] for p in pre])
    else:
        tmats = [tri_inv_known(p["a"], t) for p, t in zip(pre, inverses)]

    mid = []
    for p, tmat in zip(pre, tmats):
        egc = jnp.exp(p["gc_col"])
        mid.append(dict(u=mm(tmat, p["vb"]), wm=mm(tmat, p["kb"] * egc), qe=p["qd"] * egc,
                        intra=jnp.where(lower, mm_nt(p["qd"], p["kd"]) * p["decay"], 0.0),
                        ke=p["kd"] * jnp.exp(p["gl_b"] - p["gc_col"]), g_end=p["g_end"]))

    ys = []
    for k in range(n_chunks):
        rows = slice(k * cs, (k + 1) * cs)
        new_states, y_heads = [], []
        for pair in range(2):
            m = mid[2 * k + pair]
            hs = (2 * pair, 2 * pair + 1)
            st = (states[hs[0]], states[hs[1]])
            v_new = m["u"] - jnp.concatenate([mm(m["wm"][:cs], st[0]), mm(m["wm"][cs:], st[1])], axis=0)
            o = jnp.concatenate([mm(m["qe"][:cs], st[0]), mm(m["qe"][cs:], st[1])], axis=0) + mm(m["intra"], v_new)
            new_states.append(st[0] * jnp.exp(m["g_end"][0]) + mm_tn(jnp.where(top, m["ke"], 0.0), v_new))
            new_states.append(st[1] * jnp.exp(m["g_end"][1]) + mm_tn(jnp.where(top, 0.0, m["ke"]), v_new))
            od = o * lax.rsqrt(jnp.mean(o * o, axis=-1, keepdims=True) + EPS) * dn_g
            y_heads += [od[:cs] * _silu(zc[rows, 128 * hs[0]:128 * hs[0] + 128]),
                        od[cs:] * _silu(zc[rows, 128 * hs[1]:128 * hs[1] + 128])]
        states = tuple(new_states)
        ys.append(jnp.concatenate(y_heads, axis=1))
    if with_inverses:
        return (jnp.concatenate(ys, axis=0), *states), tmats
    return (jnp.concatenate(ys, axis=0), *states)


DELTA_ROWS = 4 * DN_CHUNK
DELTA_UNITS = 2 * DELTA_ROWS // DN_CHUNK


def delta_fwd(name, qkv, proj, a_log, dt_bias, dn_g):
    t = qkv.shape[0]
    nc = t // DELTA_ROWS

    def body(qkv_ref, ab_ref, zc_ref, al_ref, dt_ref, g_ref, y_ref, ssave_ref, tsave_ref, s_ref):
        @pl.when(pl.program_id(0) == 0)
        def _():
            s_ref[...] = jnp.zeros_like(s_ref)

        ssave_ref[0] = s_ref[...]
        st = [s_ref[128 * h:128 * h + 128, :] for h in range(4)]
        (y, *ns), tmats = _f_delta_step(qkv_ref[...], ab_ref[...], zc_ref[...], *st, al_ref[...], dt_ref[...], g_ref[...],
                                        with_inverses=True)
        y_ref[...] = y
        for h in range(4):
            s_ref[128 * h:128 * h + 128, :] = ns[h]
        for u, tm in enumerate(tmats):
            tsave_ref[0, 128 * u:128 * u + 128, :] = tm

    return pl.pallas_call(
        body, name=name, grid=(nc,),
        in_specs=[pl.BlockSpec((DELTA_ROWS, 1536), lambda i: (i, 0)), pl.BlockSpec((DELTA_ROWS, 128), lambda i: (i, P_AB // 128)),
                  pl.BlockSpec((DELTA_ROWS, 512), lambda i: (i, P_ZC // 512)),
                  _const_spec((1, 128)), _const_spec((1, 128)), _const_spec((1, 128))],
        out_specs=[pl.BlockSpec((DELTA_ROWS, 512), lambda i: (i, 0)), pl.BlockSpec((1, 512, 128), lambda i: (i, 0, 0)),
                   pl.BlockSpec((1, DELTA_UNITS * 128, 128), lambda i: (i, 0, 0))],
        out_shape=[jax.ShapeDtypeStruct((t, 512), F32), jax.ShapeDtypeStruct((nc, 512, 128), F32),
                   jax.ShapeDtypeStruct((nc, DELTA_UNITS * 128, 128), F32)],
        scratch_shapes=[pltpu.VMEM((512, 128), F32)],
        compiler_params=_cparams(1),
    )(qkv, proj, proj, a_log, dt_bias, dn_g)


def delta_bwd(name, qkv, proj, ssave, tsave, a_log, dt_bias, dn_g, dyc, carry=None):
    t = qkv.shape[0]
    nc = t // DELTA_ROWS
    c_ins, c_in_specs, c_outs, c_out_specs, c_scratch = _host(carry)
    n_ci, n_co = len(c_ins), len(c_outs)

    def body(*refs):
        qkv_ref, ab_ref, zc_ref, ss_ref, ts_ref, al_ref, dt_ref, g_ref, dy_ref = refs[:9]
        dqkv_ref, dab_ref, dzc_ref, dal_ref, ddt_ref, dg_ref = refs[9 + n_ci:15 + n_ci]
        ds_ref = refs[15 + n_ci + n_co]
        carried = (refs[9:9 + n_ci], refs[15 + n_ci:15 + n_ci + n_co], *refs[16 + n_ci + n_co:])

        @pl.when(pl.program_id(0) == 0)
        def _():
            ds_ref[...] = jnp.zeros_like(ds_ref)
            dal_ref[...] = jnp.zeros_like(dal_ref)
            ddt_ref[...] = jnp.zeros_like(ddt_ref)
            dg_ref[...] = jnp.zeros_like(dg_ref)

        if carry is not None:
            carry.emit_start(pl.program_id(0) == 0, *carried)

        st = [ss_ref[0, 128 * h:128 * h + 128, :] for h in range(4)]
        known = [ts_ref[0, 128 * u:128 * u + 128, :] for u in range(DELTA_UNITS)]
        _, vjp = jax.vjp(functools.partial(_f_delta_step, inverses=known), qkv_ref[...], ab_ref[...], zc_ref[...], *st,
                         al_ref[...], dt_ref[...], g_ref[...])
        dst = tuple(ds_ref[128 * h:128 * h + 128, :] for h in range(4))
        dqkv, dab, dzc, d0, d1, d2, d3, dal, ddt, dg = vjp((dy_ref[...], *dst))
        dqkv_ref[...] = dqkv
        dab_ref[...] = dab.astype(dab_ref.dtype)
        dzc_ref[...] = dzc.astype(dzc_ref.dtype)
        for h, d in enumerate((d0, d1, d2, d3)):
            ds_ref[128 * h:128 * h + 128, :] = d
        dal_ref[...] += dal
        ddt_ref[...] += ddt
        dg_ref[...] += dg

        if carry is not None:
            carry.emit_finish(pl.program_id(0) == nc - 1, *carried)

    rev = lambda cb: (lambda j: (nc - 1 - j, cb))
    return pl.pallas_call(
        body, name=name, grid=(nc,),
        in_specs=[pl.BlockSpec((DELTA_ROWS, 1536), rev(0)), pl.BlockSpec((DELTA_ROWS, 128), rev(P_AB // 128)),
                  pl.BlockSpec((DELTA_ROWS, 512), rev(P_ZC // 512)), pl.BlockSpec((1, 512, 128), lambda j: (nc - 1 - j, 0, 0)),
                  pl.BlockSpec((1, DELTA_UNITS * 128, 128), lambda j: (nc - 1 - j, 0, 0)),
                  _const_spec((1, 128)), _const_spec((1, 128)), _const_spec((1, 128)),
                  pl.BlockSpec((DELTA_ROWS, 512), rev(0))] + c_in_specs,
        out_specs=[pl.BlockSpec((DELTA_ROWS, 1536), rev(0)), pl.BlockSpec((DELTA_ROWS, 128), rev(0)),
                   pl.BlockSpec((DELTA_ROWS, 512), rev(0)),
                   _const_spec((1, 128)), _const_spec((1, 128)), _const_spec((1, 128))] + c_out_specs,
        out_shape=[jax.ShapeDtypeStruct((t, 1536), F32), jax.ShapeDtypeStruct((t, 128), BF16),
                   jax.ShapeDtypeStruct((t, 512), BF16),
                   jax.ShapeDtypeStruct((1, 128), F32), jax.ShapeDtypeStruct((1, 128), F32), jax.ShapeDtypeStruct((1, 128), F32)]
        + c_outs,
        scratch_shapes=[pltpu.VMEM((512, 128), F32)] + c_scratch,
        compiler_params=_cparams(1),
    )(qkv, proj, proj, ssave, tsave, a_log, dt_bias, dn_g, dyc, *c_ins)


def loss_head(name, y, target, tm):
    t, d = y.shape

    def body(y_ref, t_ref, dy_ref, l_ref):
        err = y_ref[...] - t_ref[...]
        dy_ref[...] = err * (1.0 / d)
        part = 0.5 * jnp.sum(jnp.sum(err * err, axis=-1, keepdims=True) * (1.0 / d), axis=0, keepdims=True)

        @pl.when(pl.program_id(0) == 0)
        def _():
            l_ref[...] = part

        @pl.when(pl.program_id(0) > 0)
        def _():
            l_ref[...] += part

    return pl.pallas_call(
        body, name=name, grid=(t // tm,),
        in_specs=[_row_spec(tm, d, 0), _row_spec(tm, d, 0)],
        out_specs=[_row_spec(tm, d, 0), _const_spec((1, 1))],
        out_shape=[jax.ShapeDtypeStruct((t, d), F32), jax.ShapeDtypeStruct((1, 1), F32)],
        compiler_params=_cparams(1),
    )(y, target)


TM = 512
TM_MERGE = 256
TM_IN = 1024
TN_IN = 1152


def _lane_pad(v, n=128):
    return jnp.pad(v.astype(F32), (0, n - v.shape[0]))[None, :]


def f_norm_mod_res(x, g, scale, shift):
    return f_norm_mod(x, g, scale, shift), x


def prep_layer(w):
    p = dict(w)
    p["wp"] = _w_in_assemble(w["w_in"])
    p["wpa"] = _perm_heads_rows(w["w_proj_a"])
    p["dw32"] = jnp.pad(w["dw_w"], ((0, 32 - CONV_K), (0, 0)))
    p["sconv8"] = jnp.pad(w["sconv_w"], ((0, 8 - DN_CONV_K), (0, 0)))
    p["qg"] = jnp.tile(w["q_norm_g"], 2)[None, :]
    p["kg"] = jnp.tile(w["k_norm_g"], 2)[None, :]
    p["sinks128"] = _lane_pad(w["sinks"])
    p["al"] = _lane_pad(w["a_log"])
    p["dtb"] = _lane_pad(w["dt_bias"])
    p["dng"] = w["dn_norm_g"][None, :]
    return p


def layer_fwd(tag, x, mod, p, carry_inproj=None, carry_merge=None):
    d = D_MODEL
    shift, scale, gate = mod[:, :d], mod[:, d:2 * d], mod[:, 2 * d:]
    g = p["norm_g"][None, :]
    h, h_t = rowwise_fwd(f"norm_fwd{tag}", lambda *a: (f_norm_mod(*a),) * 2, [(x, d, 0)], [g, scale, shift],
                         [(d, BF16), (d, BF16, "transposed")], TM)
    proj = matmul_nn(f"inproj_fwd{tag}", h, p["wp"], F32, TM_IN, TN_IN, d, carry=carry_inproj)
    proj, got_inproj = (proj, []) if carry_inproj is None else (proj[0], proj[1:])
    ya = attn_fwd(f"attn_fwd{tag}", proj, p["qg"], p["kg"], p["sinks128"])
    ub = glu_conv_fwd(f"glu_conv_fwd{tag}", proj, p["dw32"], p["dw_b"][None, :])
    conf_consts = [p["ln_g"][None, :], p["ln_b"][None, :], p["pw2_w"], p["pw2_b"][None, :]]
    (yb,) = rowwise_fwd(f"conf_fwd{tag}", f_conf_tail, [(ub, 512, 0), (proj, 512, P_ZB // 512)], conf_consts, [(512, F32)], TM)
    qkv = sconv_fwd(f"sconv_fwd{tag}", proj, p["sconv8"])
    yc, ssave, tsave = delta_fwd(f"delta_fwd{tag}", qkv, proj, p["al"], p["dtb"], p["dng"])
    merge_consts = [gate, p["wpa"], p["w_proj_b"], p["w_proj_c"], p["w_out"]]
    merge_rows = [(ya, 512, 0), (yb, 512, 0), (yc, 512, 0), (proj, 3 * d, P_MG // (3 * d)), (x, d, 0)]
    xn, *got_merge = rowwise_fwd(f"merge_fwd{tag}", f_merge, merge_rows, merge_consts, [(d, F32)], TM_MERGE, carry=carry_merge)
    saved = dict(x=x, h_t=h_t, proj=proj, ub=ub, qkv=qkv, ssave=ssave, tsave=tsave, norm_consts=[g, scale, shift],
                 conf_consts=conf_consts, merge_consts=merge_consts, merge_rows=merge_rows)
    return xn, saved, got_inproj, got_merge


def layer_bwd(tag, dxn, p, s, carry_merge=None, carry_delta=None, carry_dh=None):
    d = D_MODEL
    proj = s["proj"]
    dya, dyb, dyc, dmg, dgate, dwpa, dwpb, dwpc, dwout, *got_merge = rowwise_bwd(
        f"merge_bwd{tag}", f_merge, s["merge_rows"], s["merge_consts"], [(dxn, d, 0)], [F32, F32, F32, BF16, None], TM_MERGE,
        carry=carry_merge)
    carry_delta = None if carry_delta is None else carry_delta(got_merge)
    dqz, dkv, dqg, dkg, dsinks = attn_bwd(f"attn_bwd{tag}", proj, p["qg"], p["kg"], p["sinks128"], dya)
    dub, dzb, dln_g, dln_b, dpw2_w, dpw2_b = rowwise_bwd(
        f"conf_bwd{tag}", f_conf_tail, [(s["ub"], 512, 0), (proj, 512, P_ZB // 512)], s["conf_consts"], [(dyb, 512, 0)],
        [F32, BF16], TM)
    dglu, ddw32, ddw_b = glu_conv_bwd(f"glu_conv_bwd{tag}", proj, p["dw32"], dub)
    dqkv, dab, dzc, dal, ddtb, ddng, *got_delta = delta_bwd(f"delta_bwd{tag}", s["qkv"], proj, s["ssave"], s["tsave"], p["al"],
                                                            p["dtb"], p["dng"], dyc, carry_delta)
    dqkv_pre, dsconv8 = sconv_bwd(f"sconv_bwd{tag}", proj, p["sconv8"], dqkv)
    dproj = jnp.concatenate([dqz, dglu, dzb, dzc, dmg, dqkv_pre, dkv, dab], axis=1)
    dwp = matmul_nn(f"inproj_bwd_dw{tag}", s["h_t"], dproj, F32, d, TN_IN, 2048)
    reduced = dict(w_in=_w_in_grad_blocks(dwp), pw2_w=dpw2_w, w_proj_a=_unperm_heads_rows(dwpa), w_proj_b=dwpb, w_proj_c=dwpc,
                   w_out=dwout)
    carry_dh = None if carry_dh is None else carry_dh(reduced)
    dh = matmul_nn(f"inproj_bwd_dh{tag}", dproj, p["wp"], F32, TM_IN, d, P_TOTAL // 3, b_transposed=True, carry=carry_dh)
    dh, got_dh = (dh, []) if carry_dh is None else (dh[0], dh[1:])
    dx, dnorm_g, dscale, dshift = rowwise_bwd(
        f"norm_bwd{tag}", f_norm_mod_res, [(s["x"], d, 0)], s["norm_consts"], [(dh, d, 0), (dxn, d, 0)], [F32], TM)
    dmod = jnp.concatenate([dshift, dscale, dgate], axis=1)
    grads = dict(
        reduced, b_ada=dmod[0], norm_g=dnorm_g[0],
        q_norm_g=dqg[0, :64] + dqg[0, 64:], k_norm_g=dkg[0, :64] + dkg[0, 64:], sinks=dsinks[0, :ATT_HEADS],
        dw_w=ddw32[:CONV_K], dw_b=ddw_b[0], ln_g=dln_g[0], ln_b=dln_b[0], pw2_b=dpw2_b[0],
        sconv_w=dsconv8[:DN_CONV_K], a_log=dal[0, :DN_HEADS], dt_bias=ddtb[0, :DN_HEADS], dn_norm_g=ddng[0])
    return dx, grads, got_merge, got_delta, got_dh


SHARDED = {"w_ada": 2, "w_in": 2, "dw_w": 2, "pw2_w": 1, "sconv_w": 2, "w_proj_a": 2, "w_proj_b": 2, "w_proj_c": 2,
           "w_out": 1}
GATHERED = tuple(n for n in SHARDED if n != "w_ada")
GATHER_F32 = ("dw_w", "sconv_w")
REDUCE_BIG = tuple(n for n in GATHERED if n not in GATHER_F32)
SMALL = ("b_ada", "norm_g", "q_norm_g", "k_norm_g", "sinks", "dw_b", "ln_g", "ln_b", "pw2_b", "a_log", "dt_bias",
         "dn_norm_g")
SMALL_ROWS = 104
SMALL_GRAD_ROWS = 448
W_IN_SHARD = D_IN // N_CHIPS
SUM_PARTS = 4


def _w_in_orig():
    orig = np.full(P_TOTAL, -1, np.int64)
    p = 0
    for s, n in _in_pieces():
        orig[p:p + n] = np.arange(s, s + n)
        p += n
    return orig


def _w_in_blocks(k):
    orig = _w_in_orig().reshape(-1, 128)
    lo, hi = k * W_IN_SHARD, (k + 1) * W_IN_SHARD
    return [b for b in range(orig.shape[0]) if np.any((orig[b] >= lo) & (orig[b] < hi))]


W_IN_BLOCKS = max(len(_w_in_blocks(k)) for k in range(N_CHIPS))


def _runs(idx):
    out, i = [], 0
    while i < len(idx):
        j = i + 1
        while j < len(idx) and ((idx[i] < 0 and idx[j] < 0) or (idx[i] >= 0 and idx[j] == idx[j - 1] + 1)):
            j += 1
        out.append((int(idx[i]) if idx[i] >= 0 else -1, j - i))
        i = j
    return out


def _take(a, idx):
    parts = [jnp.zeros(a.shape[:-1] + (n,), a.dtype) if s < 0 else a[..., s:s + n] for s, n in _runs(idx)]
    return parts[0] if len(parts) == 1 else jnp.concatenate(parts, axis=-1)


def _w_in_send(k, shard):
    orig = _w_in_orig().reshape(-1, 128)
    lo, hi = k * W_IN_SHARD, (k + 1) * W_IN_SHARD
    idx = np.concatenate([np.where((orig[b] >= lo) & (orig[b] < hi), orig[b] - lo, -1) for b in _w_in_blocks(k)])
    idx = np.concatenate([idx, np.full((W_IN_BLOCKS - len(_w_in_blocks(k))) * 128, -1)])
    return _take(shard, idx)


def _w_in_assemble(blocks):
    where = [{b: i for i, b in enumerate(_w_in_blocks(k))} for k in range(N_CHIPS)]
    n_blocks = P_TOTAL // 128
    owners = [[(k, where[k][b]) for k in range(N_CHIPS) if b in where[k]] for b in range(n_blocks)]
    parts, b = [], 0
    while b < n_blocks:
        if len(owners[b]) == 1:
            k, pos = owners[b][0]
            e = b + 1
            while e < n_blocks and owners[e] == [(k, pos + e - b)]:
                e += 1
            parts.append(blocks[k][:, pos * 128:(pos + e - b) * 128])
            b = e
        else:
            parts.append(functools.reduce(jnp.add, [blocks[k][:, pos * 128:(pos + 1) * 128] for k, pos in owners[b]]))
            b += 1
    return jnp.concatenate(parts, axis=1)


def _w_in_grad_blocks(wp):
    out = []
    for k in range(N_CHIPS):
        idx = np.concatenate([np.arange(128 * b, 128 * b + 128) for b in _w_in_blocks(k)])
        idx = np.concatenate([idx, np.full((W_IN_BLOCKS - len(_w_in_blocks(k))) * 128, -1)])
        out.append(_take(wp, idx))
    return jnp.stack(out)


def _w_in_receive_grad(k, blocks):
    orig = _w_in_orig()
    inv = np.zeros(D_IN, np.int64)
    inv[orig[orig >= 0]] = np.nonzero(orig >= 0)[0]
    where = {b: i for i, b in enumerate(_w_in_blocks(k))}
    cols = inv[k * W_IN_SHARD:(k + 1) * W_IN_SHARD]
    return _take(blocks, np.array([where[c // 128] * 128 + c % 128 for c in cols]))


def _join_layer(v, axis):
    if axis == 2:
        return jnp.transpose(v, (1, 0, 2)).reshape(v.shape[1], N_CHIPS * v.shape[2])
    return v.reshape(N_CHIPS * v.shape[1], v.shape[2])


def _split_layer(v, axis):
    a, b = v.shape
    if axis == 2:
        return jnp.transpose(v.reshape(a, N_CHIPS, b // N_CHIPS), (1, 0, 2))
    return v.reshape(N_CHIPS, a // N_CHIPS, b)


def pack_small(vals, names, rows):
    flat = jnp.concatenate([vals[n].astype(F32).reshape(-1) for n in names])
    return jnp.pad(flat, (0, rows * 128 - flat.shape[0])).reshape(rows, 128)


def unpack_small(packed, names, shapes):
    flat = packed.reshape(-1)
    out, off = {}, 0
    for n in names:
        k = int(np.prod(shapes[n]))
        out[n] = flat[off:off + k].reshape(shapes[n])
        off += k
    return out


ANY = pl.BlockSpec(memory_space=pl.ANY)


def _place():
    x, y, c = lax.axis_index("x"), lax.axis_index("y"), lax.axis_index("c")
    chips = [(1 - x, y), (x, 1 - y), (1 - x, 1 - y)]
    return x, y, c, chips


def _remote(src, dst, send_sem, recv_sem, to):
    return pltpu.make_async_remote_copy(src_ref=src, dst_ref=dst, send_sem=send_sem, recv_sem=recv_sem, device_id=to,
                                        device_id_type=MESH)


class Carry:
    def __init__(self, ins, out_shapes, sems, start, finish, in_place=False):
        self.ins, self.out_shapes, self.sems, self.start, self.finish, self.in_place = (
            list(ins), list(out_shapes), sems, start, finish, in_place)

    def scratch(self):
        return [pltpu.SemaphoreType.DMA(self.sems), pltpu.SemaphoreType.DMA(self.sems)]

    def aliases(self, first_in, first_out):
        return {first_in + i: first_out + i for i in range(len(self.ins))} if self.in_place else {}

    def emit_start(self, first, in_refs, out_refs, send_sems, recv_sems):
        @pl.when(first)
        def _():
            self.start(in_refs, out_refs, send_sems, recv_sems)

    def emit_finish(self, last, in_refs, out_refs, send_sems, recv_sems):
        @pl.when(last)
        def _():
            self.finish(in_refs, out_refs, send_sems, recv_sems)


def _host(carry):
    if carry is None:
        return [], [], [], [], []
    return carry.ins, [ANY] * len(carry.ins), carry.out_shapes, [ANY] * len(carry.out_shapes), carry.scratch()


def run_carry(name, carry):
    n_in, n_out = len(carry.ins), len(carry.out_shapes)

    def body(*refs):
        ins, outs, sems = refs[:n_in], refs[n_in:n_in + n_out], refs[n_in + n_out:]
        carry.start(ins, outs, *sems)
        carry.finish(ins, outs, *sems)

    return pl.pallas_call(
        body, name=name, out_shape=carry.out_shapes, in_specs=[ANY] * n_in, out_specs=[ANY] * n_out,
        input_output_aliases=carry.aliases(0, 0), scratch_shapes=carry.scratch(),
    )(*carry.ins)


def carry_allgather(layer, slots):
    n = len(slots)

    def copies(out, send_sems, recv_sems, only_ici_out=False):
        x, y, c, chips = _place()
        ici_out, ici_in, d2d_out, d2d_in = [], [], [], []
        for j, chip in enumerate(chips):
            for t in range(n):
                mine, land = out[t].at[2 * x + y], out[t].at[2 * chip[0] + chip[1]]
                ici_out.append(_remote(mine, mine, send_sems.at[t, j], recv_sems.at[t, j], (*chip, layer)))
                if only_ici_out:
                    continue
                ici_in.append(_remote(land, land, send_sems.at[t, j], recv_sems.at[t, j], (*chip, layer)))
                d2d_out.append(_remote(land, land, send_sems.at[t, 3 + j], recv_sems.at[t, 3 + j], (x, y, 1 - layer)))
                d2d_in.append(_remote(land, land, send_sems.at[t, 3 + j], recv_sems.at[t, 3 + j], (x, y, layer)))
        return c, ici_out, ici_in, d2d_out, d2d_in

    def start(ins, out, send_sems, recv_sems):
        c, ici_out, _, _, _ = copies(out, send_sems, recv_sems, only_ici_out=True)

        @pl.when(c == layer)
        def _():
            for cp in ici_out:
                cp.start()

    def finish(ins, out, send_sems, recv_sems):
        c, ici_out, ici_in, d2d_out, d2d_in = copies(out, send_sems, recv_sems)

        @pl.when(c == layer)
        def _():
            for arrived, onward in zip(ici_in, d2d_out):
                arrived.wait_recv()
                onward.start()
            for cp in ici_out + d2d_out:
                cp.wait_send()

        @pl.when(c != layer)
        def _():
            for cp in d2d_in:
                cp.wait_recv()

    return Carry(slots, [jax.ShapeDtypeStruct(s.shape, s.dtype) for s in slots], (n, 6), start, finish, in_place=True)


def carry_pair_send(layer, gs):
    def copies(g, recv, send_sems, recv_sems):
        x, y, c, _ = _place()
        return c, [_remote(g[t], recv[t], send_sems.at[t], recv_sems.at[t], (x, y, 1 - c)) for t in range(len(gs))]

    def start(g, recv, send_sems, recv_sems):
        c, cps = copies(g, recv, send_sems, recv_sems)

        @pl.when(c != layer)
        def _():
            for cp in cps:
                cp.start()

    def finish(g, recv, send_sems, recv_sems):
        c, cps = copies(g, recv, send_sems, recv_sems)

        @pl.when(c != layer)
        def _():
            for cp in cps:
                cp.wait_send()

        @pl.when(c == layer)
        def _():
            for cp in cps:
                cp.wait_recv()

    return Carry(gs, [jax.ShapeDtypeStruct(g.shape, g.dtype) for g in gs], (len(gs),), start, finish)


def grads_pair_sums(layer, gs, recv):
    n = len(gs)

    def body(*refs):
        for t in range(n):
            refs[2 * n + t][...] = (refs[t][...] + refs[n + t][...]).astype(BF16)

    specs = [pl.BlockSpec((None, g.shape[1] // SUM_PARTS, g.shape[2]), lambda s, i: (s, i, 0)) for g in gs]
    return pl.pallas_call(
        body, name=f"grads_pair_sums{layer}", grid=(N_CHIPS, SUM_PARTS), in_specs=specs + specs, out_specs=specs,
        out_shape=[jax.ShapeDtypeStruct(g.shape, BF16) for g in gs], compiler_params=_cparams(2),
    )(*gs, *recv)


def carry_chip_exchange(layer, ps):
    def copies(p, recv, send_sems, recv_sems):
        _, _, c, chips = _place()
        return c, [_remote(p[t].at[2 * chip[0] + chip[1]], recv[t].at[j], send_sems.at[t, j], recv_sems.at[t, j],
                           (*chip, layer)) for j, chip in enumerate(chips) for t in range(len(ps))]

    def start(p, recv, send_sems, recv_sems):
        c, cps = copies(p, recv, send_sems, recv_sems)

        @pl.when(c == layer)
        def _():
            for cp in cps:
                cp.start()

    def finish(p, recv, send_sems, recv_sems):
        c, cps = copies(p, recv, send_sems, recv_sems)

        @pl.when(c == layer)
        def _():
            for cp in cps:
                cp.wait()

    return Carry(ps, [jax.ShapeDtypeStruct((3,) + p.shape[1:], p.dtype) for p in ps], (len(ps), 3), start, finish)


def grads_chip_sums(layer, gs, recv, recv2, into=None):
    n = len(gs)
    my_slot = lambda: 2 * lax.axis_index("x") + lax.axis_index("y")

    def body(*refs):
        outs = refs[-n:]
        for t in range(n):
            r2 = refs[2 * n + t]
            own = refs[t][...] + refs[n + t][...]
            outs[t][...] = ((own + r2[0].astype(F32)) + r2[1].astype(F32)) + r2[2].astype(F32)

    part = lambda g: g.shape[1] // SUM_PARTS
    own_specs = [pl.BlockSpec((None, part(g), g.shape[2]), lambda i: (my_slot(), i, 0)) for g in gs]
    return pl.pallas_call(
        body, name=f"grads_chip_sums{layer}", grid=(SUM_PARTS,),
        in_specs=own_specs + own_specs + [pl.BlockSpec((3, part(g), g.shape[2]), lambda i: (0, i, 0)) for g in gs]
        + ([] if into is None else [ANY] * n),
        out_specs=[pl.BlockSpec((None, part(g), g.shape[2]), lambda i: (layer, i, 0)) for g in gs],
        out_shape=[jax.ShapeDtypeStruct((DEPTH,) + g.shape[1:], F32) for g in gs],
        input_output_aliases={} if into is None else {3 * n + t: t for t in range(n)},
        compiler_params=_cparams(1),
    )(*gs, *recv, *recv2, *([] if into is None else into))


def grads_pair_gather(reds):
    n = len(reds)

    def body(*refs):
        buf = refs[n:2 * n]
        send_sems, recv_sems = refs[2 * n:]
        x, y, c, _ = _place()
        sibling = (x, y, 1 - c)
        cps = [_remote(buf[t].at[c], buf[t].at[c], send_sems.at[t], recv_sems.at[t], sibling) for t in range(n)]
        for cp in cps:
            cp.start()
        for t in range(n):
            _remote(buf[t].at[c], buf[t].at[1 - c], send_sems.at[t], recv_sems.at[t], sibling).wait_recv()
        for cp in cps:
            cp.wait_send()

    return pl.pallas_call(
        body, name="grads_pair_gather", out_shape=[jax.ShapeDtypeStruct(r.shape, r.dtype) for r in reds],
        in_specs=[ANY] * n, out_specs=[ANY] * n, input_output_aliases={t: t for t in range(n)},
        scratch_shapes=[pltpu.SemaphoreType.DMA((n,)), pltpu.SemaphoreType.DMA((n,))],
    )(*reds)


def small_allreduce(v):
    m, n = v.shape

    def body(x_ref, sum_ref, all_ref, send_sems, recv_sems, local_sem):
        x, y, c, chips = _place()
        me, sibling = (x, y, c), (x, y, 1 - c)

        def rows(px, py, pc):
            return all_ref.at[pl.ds((4 * px + 2 * py + pc) * m, m), :]

        def copy(k, block, to, src=None):
            return pltpu.make_async_remote_copy(src_ref=rows(*block) if src is None else src, dst_ref=rows(*block),
                                                send_sem=send_sems.at[k], recv_sem=recv_sems.at[k],
                                                device_id=to, device_id_type=MESH)

        mine = pltpu.make_async_copy(x_ref, rows(*me), local_sem)
        mine.start()
        first = [copy(0, me, sibling, src=x_ref)]
        first += [copy(1 + j, me, (*chip, c), src=x_ref) for j, chip in enumerate(chips)]
        for cp in first:
            cp.start()
        passed = [copy(4 + j, (*chip, c), sibling) for j, chip in enumerate(chips)]
        for j, chip in enumerate(chips):
            copy(1 + j, (*chip, c), me).wait_recv()
            passed[j].start()
        copy(0, sibling, me).wait_recv()
        for j, chip in enumerate(chips):
            copy(4 + j, (*chip, 1 - c), me).wait_recv()
        for cp in first + passed:
            cp.wait_send()
        mine.wait()
        acc = all_ref[0:m, :]
        for dev in range(1, 8):
            acc = acc + all_ref[dev * m:(dev + 1) * m, :]
        sum_ref[...] = acc

    vm = pl.BlockSpec(memory_space=pltpu.VMEM)
    return pl.pallas_call(
        body, name="small_allreduce",
        out_shape=[jax.ShapeDtypeStruct((m, n), F32), jax.ShapeDtypeStruct((8 * m, n), F32)],
        in_specs=[vm], out_specs=[vm, vm],
        scratch_shapes=[pltpu.SemaphoreType.DMA((7,)), pltpu.SemaphoreType.DMA((7,)), pltpu.SemaphoreType.DMA],
    )(v)


def grads_by_chip(layer_grads):
    return [layer_grads[n] if n == "w_in" else _split_layer(layer_grads[n], SHARDED[n]) for n in REDUCE_BIG]


def adamw(name, w, g, m, v, block):
    grid = tuple(s // b for s, b in zip(w.shape, block))

    def body(w_ref, g_ref, m_ref, v_ref, d_ref, nm_ref, nv_ref):
        gv = g_ref[...]
        nm = ADAM_B1 * m_ref[...] + (1.0 - ADAM_B1) * gv
        nv = ADAM_B2 * v_ref[...] + (1.0 - ADAM_B2) * (gv * gv)
        m_hat = nm / (1.0 - ADAM_B1 ** ADAM_STEP)
        v_hat = nv / (1.0 - ADAM_B2 ** ADAM_STEP)
        d_ref[...] = -ADAM_LR * (m_hat / (jnp.sqrt(v_hat) + ADAM_EPS) + ADAM_WD * w_ref[...])
        nm_ref[...] = nm
        nv_ref[...] = nv

    spec = pl.BlockSpec(tuple(block), lambda *idx: idx)
    return pl.pallas_call(
        body, name=name, grid=grid, in_specs=[spec] * 4, out_specs=[spec] * 3,
        out_shape=[jax.ShapeDtypeStruct(w.shape, F32)] * 3, compiler_params=_cparams(len(grid)),
    )(w, g, m, v)


ADAM_ROWS = {"w_ada": 512, "dw_w": 62, "pw2_w": 256, "sconv_w": 8, "w_proj_a": 512, "w_proj_b": 512, "w_proj_c": 512,
             "w_out": 256}
ADAM_W_IN_COLS = 331

WEIGHT_NAMES = ("w_ada", "b_ada", "norm_g", "w_in", "q_norm_g", "k_norm_g", "sinks", "dw_w", "dw_b", "ln_g", "ln_b",
                "pw2_w", "pw2_b", "sconv_w", "a_log", "dt_bias", "dn_norm_g", "w_proj_a", "w_proj_b", "w_proj_c", "w_out")


def kernel(x, c, w_ada, b_ada, norm_g, w_in, q_norm_g, k_norm_g, sinks, dw_w, dw_b, ln_g, ln_b, pw2_w, pw2_b, sconv_w, a_log, dt_bias, dn_norm_g, w_proj_a, w_proj_b, w_proj_c, w_out, loss_target, m_w_ada, m_b_ada, m_norm_g, m_w_in, m_q_norm_g, m_k_norm_g, m_sinks, m_dw_w, m_dw_b, m_ln_g, m_ln_b, m_pw2_w, m_pw2_b, m_sconv_w, m_a_log, m_dt_bias, m_dn_norm_g, m_w_proj_a, m_w_proj_b, m_w_proj_c, m_w_out, v_w_ada, v_b_ada, v_norm_g, v_w_in, v_q_norm_g, v_k_norm_g, v_sinks, v_dw_w, v_dw_b, v_ln_g, v_ln_b, v_pw2_w, v_pw2_b, v_sconv_w, v_a_log, v_dt_bias, v_dn_norm_g, v_w_proj_a, v_w_proj_b, v_w_proj_c, v_w_out):
    args = dict(locals())
    w = {n: args[n] for n in WEIGHT_NAMES}
    mom = {n: args["m_" + n] for n in WEIGHT_NAMES}
    var = {n: args["v_" + n] for n in WEIGHT_NAMES}

    chip = 2 * lax.axis_index("x") + lax.axis_index("y")
    own = {n: w[n] if n in GATHER_F32 else w[n].astype(BF16) for n in GATHERED}
    own["w_in"] = lax.switch(chip, [functools.partial(_w_in_send, k) for k in range(N_CHIPS)], own["w_in"])
    slots = [[lax.dynamic_update_slice(lax.empty((N_CHIPS,) + own[n].shape[1:], own[n].dtype), own[n][l][None], (chip, 0, 0))
              for n in GATHERED] for l in range(DEPTH)]

    def layer_operands(l, gathered):
        lw = {n: w[n][l] for n in SMALL}
        lw.update({n: g if n == "w_in" else _join_layer(g, SHARDED[n]) for n, g in zip(GATHERED, gathered)})
        return prep_layer(lw)

    layers = [layer_operands(0, run_carry("weights_allgather0", carry_allgather(0, slots[0]))), None]

    mod, conds = ada_fwd(jnp.tile(c, (8, 1)), w["w_ada"], w["b_ada"])
    saved = [None] * DEPTH
    big = GATHERED.index("w_in")
    rest = [i for i in range(len(GATHERED)) if i != big]
    act, saved[0], got_big, got_rest = layer_fwd(
        "0", x[0], mod[0:1], layers[0], carry_inproj=carry_allgather(1, [slots[1][big]]),
        carry_merge=carry_allgather(1, [slots[1][i] for i in rest]))
    gathered1 = dict(zip(rest, got_rest))
    gathered1[big] = got_big[0]
    layers[1] = layer_operands(1, [gathered1[i] for i in range(len(GATHERED))])
    act, saved[1], _, _ = layer_fwd("1", act, mod[1:2], layers[1])
    dact, loss_part = loss_head("loss_head", act, loss_target[0], TM)
    loss = lax.psum(loss_part[0, 0], ("x", "y", "c"))
    layer_grads = [None] * DEPTH
    dact, layer_grads[1], _, _, _ = layer_bwd("1", dact, layers[1], saved[1])
    gs1 = grads_by_chip(layer_grads[1])
    gs0 = []

    def hand_over_layer0(reduced):
        gs0.extend(grads_by_chip(reduced))
        return carry_pair_send(0, gs0)

    dact, layer_grads[0], recv1, got1, recv0 = layer_bwd(
        "0", dact, layers[0], saved[0], carry_merge=carry_pair_send(1, gs1),
        carry_delta=lambda recv: carry_chip_exchange(1, grads_pair_sums(1, gs1, recv)), carry_dh=hand_over_layer0)

    got0 = run_carry("grads_chip_exchange0", carry_chip_exchange(0, grads_pair_sums(0, gs0, recv0)))
    reds = grads_chip_sums(0, gs0, recv0, got0, into=grads_chip_sums(1, gs1, recv1, got1))
    final_grads = dict(zip(REDUCE_BIG, grads_pair_gather(reds)))
    final_grads["w_in"] = lax.switch(chip, [functools.partial(_w_in_receive_grad, k) for k in range(N_CHIPS)],
                                     final_grads["w_in"])
    small_names = SMALL + GATHER_F32
    small_shapes = {n: (DEPTH,) + layer_grads[0][n].shape for n in small_names}
    small_full = {n: jnp.stack([layer_grads[l][n] for l in range(DEPTH)]) for n in small_names}
    small_sum, small_all = small_allreduce(pack_small(small_full, small_names, SMALL_GRAD_ROWS))
    small_sum = unpack_small(small_sum, small_names, small_shapes)
    for n in GATHER_F32:
        width = w[n].shape[2]
        final_grads[n] = lax.dynamic_slice_in_dim(small_sum[n], chip * width, width, axis=2)
    n_mod = DEPTH * 3 * D_MODEL
    dmod = small_all.reshape(8, -1)[:, :n_mod].reshape(8, DEPTH, 3 * D_MODEL)
    width = w["w_ada"].shape[2]
    dmod = jnp.transpose(lax.dynamic_slice_in_dim(dmod, chip * width, width, axis=2), (1, 0, 2))
    final_grads["w_ada"] = ada_bwd(conds, dmod)
    final_grads.update({n: small_sum[n] for n in SMALL})
    small_grads = pack_small(final_grads, SMALL, SMALL_ROWS)

    delta, new_m, new_v = {}, {}, {}
    for n in SHARDED:
        shp = w[n].shape
        if n == "w_in":
            view = lambda a: jnp.transpose(a, (2, 0, 1))
            back = lambda a: jnp.transpose(a, (1, 2, 0))
            g3 = view(final_grads[n])
            final_grads[n] = back(g3)
            d, nm, nv = adamw("adamw_" + n, view(w[n]), g3, view(mom[n]), view(var[n]), (ADAM_W_IN_COLS, shp[0], shp[1]))
        else:
            view = lambda a, shp=shp: a.reshape(shp[0] * shp[1], shp[2])
            back = lambda a, shp=shp: a.reshape(shp)
            d, nm, nv = adamw("adamw_" + n, view(w[n]), view(final_grads[n]), view(mom[n]), view(var[n]),
                              (ADAM_ROWS[n], shp[2]))
        delta[n], new_m[n], new_v[n] = back(d), back(nm), back(nv)
    d, nm, nv = adamw("adamw_small", pack_small(w, SMALL, SMALL_ROWS), small_grads, pack_small(mom, SMALL, SMALL_ROWS),
                      pack_small(var, SMALL, SMALL_ROWS), (SMALL_ROWS, 128))
    delta.update(unpack_small(d, SMALL, small_shapes))
    new_m.update(unpack_small(nm, SMALL, small_shapes))
    new_v.update(unpack_small(nv, SMALL, small_shapes))

    return (loss, dact[None], *[final_grads[n] for n in WEIGHT_NAMES], *[delta[n] for n in WEIGHT_NAMES],
            *[new_m[n] for n in WEIGHT_NAMES], *[new_v[n] for n in WEIGHT_NAMES])
```

```python
import functools

import numpy as np
import jax
import jax.numpy as jnp
from jax import lax
from jax.experimental import pallas as pl
from jax.experimental.pallas import tpu as pltpu

F32 = jnp.float32
BF16 = jnp.bfloat16
MESH = pl.DeviceIdType.MESH

D_MODEL = 1024
DEPTH = 2
ATT_HEADS = 8
ATT_HEAD_DIM = 64
WINDOW = 128
CONV_K = 31
DN_HEADS = 4
DN_CONV_K = 4
DN_CHUNK = 64
EPS = 1e-6
NEG_INF = -1e30
N_CHIPS = 4
D_IN = 7944

ADAM_LR = 0.001
ADAM_B1 = 0.9
ADAM_B2 = 0.999
ADAM_EPS = 1e-08
ADAM_WD = 0.01
ADAM_STEP = 10

VMEM_LIMIT = 56 * 1024 * 1024

P_QA, P_ZA, P_GLU, P_ZB, P_ZC, P_MG, P_QKV, P_KA, P_VA, P_AB, P_TOTAL = (
    0, 512, 1024, 2048, 2560, 3072, 6144, 7680, 7808, 7936, 8064)
HEAD_ORDER = (0, 4, 1, 5, 2, 6, 3, 7)


def _in_pieces():
    p = [(0 + 64 * h, 64) for h in HEAD_ORDER]
    p += [(768 + 64 * h, 64) for h in HEAD_ORDER]
    for g in range(4):
        p += [(1280 + 128 * g, 128), (1792 + 128 * g, 128)]
    p += [(2304, 512), (4360, 512), (4872, 3072), (2816, 1536), (512, 128), (640, 128), (4352, 8)]
    return p


def _perm_heads_rows(w):
    return jnp.concatenate([w[64 * h:64 * h + 64] for h in HEAD_ORDER], axis=0)


def _unperm_heads_rows(w):
    inv = [HEAD_ORDER.index(h) for h in range(8)]
    return jnp.concatenate([w[64 * s:64 * s + 64] for s in inv], axis=0)


def _split_bf16(a, terms):
    out, rest = [], a.astype(F32)
    for _ in range(terms - 1):
        out.append(rest.astype(BF16))
        rest = rest - out[-1].astype(F32)
    return out + [rest.astype(BF16)]


def _dot(a, b, dims, exact):
    d = lambda p, q: lax.dot_general(p, q, (dims, ((), ())), preferred_element_type=F32)
    if exact:
        (ah, al), (bh, bl) = _split_bf16(a, 2), _split_bf16(b, 2)
        return d(ah, bh) + (d(ah, bl) + d(al, bh))
    return d(a.astype(BF16), b.astype(BF16))


def _make_mm(exact):
    @jax.custom_vjp
    def nn(a, b):
        return _dot(a, b, ((1,), (0,)), exact)

    @jax.custom_vjp
    def nt(a, b):
        return _dot(a, b, ((1,), (1,)), exact)

    @jax.custom_vjp
    def tn(a, b):
        return _dot(a, b, ((0,), (0,)), exact)

    nn.defvjp(lambda a, b: (nn(a, b), (a, b)),
              lambda r, g: (nt(g, r[1]).astype(r[0].dtype), tn(r[0], g).astype(r[1].dtype)))
    nt.defvjp(lambda a, b: (nt(a, b), (a, b)),
              lambda r, g: (nn(g, r[1]).astype(r[0].dtype), tn(g, r[0]).astype(r[1].dtype)))
    tn.defvjp(lambda a, b: (tn(a, b), (a, b)),
              lambda r, g: (nt(r[1], g).astype(r[0].dtype), nn(r[0], g).astype(r[1].dtype)))
    return nn, nt, tn


mm, mm_nt, mm_tn = _make_mm(False)
xmm, xmm_nt, xmm_tn = _make_mm(True)


@jax.custom_vjp
def sel_mm(m, g):
    mb = m.astype(BF16)
    parts = [jnp.dot(mb, p, preferred_element_type=F32) for p in _split_bf16(g, 3)]
    return parts[0] + (parts[1] + parts[2])


def _sel_mm_bwd(m, dy):
    mb = m.astype(BF16)
    parts = [lax.dot_general(mb, p, (((0,), (0,)), ((), ())), preferred_element_type=F32) for p in _split_bf16(dy, 3)]
    return jnp.zeros_like(m), parts[0] + (parts[1] + parts[2])


sel_mm.defvjp(lambda m, g: (sel_mm(m, g), m), _sel_mm_bwd)


@jax.custom_vjp
def tri_inv(*mats):
    n = mats[0].shape[0]
    eye = jnp.where(lax.broadcasted_iota(jnp.int32, (n, n), 0) == lax.broadcasted_iota(jnp.int32, (n, n), 1), 1.0, 0.0)
    ts = [eye - a for a in mats]
    pws = list(mats)
    for _ in range(5):
        pws = [xmm(pw, pw) for pw in pws]
        ts = [t + xmm(t, pw) for t, pw in zip(ts, pws)]
    return tuple(ts)


def _tri_inv_bwd(ts, dts):
    inner = [xmm_nt(dt, t) for t, dt in zip(ts, dts)]
    return tuple(-xmm_tn(t, m) for t, m in zip(ts, inner))


tri_inv.defvjp(lambda *mats: (tri_inv(*mats),) * 2, _tri_inv_bwd)


@jax.custom_vjp
def tri_inv_known(a, t):
    return t


tri_inv_known.defvjp(lambda a, t: (t, t), lambda t, dt: (_tri_inv_bwd((t,), (dt,))[0], jnp.zeros_like(t)))


def _sigmoid(x):
    return 1.0 / (1.0 + jnp.exp(-x))


def _silu(x):
    return x * _sigmoid(x)


def _softplus(x):
    return jnp.maximum(x, 0.0) + jnp.log(1.0 + jnp.exp(-jnp.abs(x)))


def _cparams(n_grid):
    return pltpu.CompilerParams(dimension_semantics=("arbitrary",) * n_grid, vmem_limit_bytes=VMEM_LIMIT)


def _row_spec(tm, width, colblk):
    return pl.BlockSpec((tm, width), lambda i, cb=colblk: (i, cb))


def _const_spec(shape):
    nd = len(shape)
    return pl.BlockSpec(tuple(shape), lambda i, nd=nd: (0,) * nd)


def rowwise_fwd(name, f, rows, consts, outs, tm, carry=None):
    n_r, n_c = len(rows), len(consts)
    t = rows[0][0].shape[0]
    c_ins, c_in_specs, c_outs, c_out_specs, c_scratch = _host(carry)
    n_in, n_ci, n_co = n_r + n_c, len(c_ins), len(c_outs)

    def body(*refs):
        carried = (refs[n_in:n_in + n_ci], refs[n_in + n_ci + len(outs):n_in + n_ci + len(outs) + n_co],
                   *refs[n_in + n_ci + len(outs) + n_co:])
        if carry is not None:
            carry.emit_start(pl.program_id(0) == 0, *carried)
        vals = [r[...] for r in refs[:n_in]]
        res = f(*vals)
        if not isinstance(res, (tuple, list)):
            res = (res,)
        for o_ref, v, out in zip(refs[n_in + n_ci:n_in + n_ci + len(outs)], res, outs):
            o_ref[...] = (v.T if len(out) == 3 else v).astype(o_ref.dtype)
        if carry is not None:
            carry.emit_finish(pl.program_id(0) == t // tm - 1, *carried)

    return pl.pallas_call(
        body, name=name, grid=(t // tm,),
        in_specs=[_row_spec(tm, w, cb) for _, w, cb in rows] + [_const_spec(c.shape) for c in consts] + c_in_specs,
        out_specs=[_row_spec(tm, o[0], 0) if len(o) == 2 else pl.BlockSpec((o[0], tm), lambda i: (0, i)) for o in outs]
        + c_out_specs,
        out_shape=[jax.ShapeDtypeStruct((t, o[0]) if len(o) == 2 else (o[0], t), o[1]) for o in outs] + c_outs,
        input_output_aliases={} if carry is None else carry.aliases(n_in, len(outs)),
        scratch_shapes=c_scratch,
        compiler_params=_cparams(1),
    )(*[a for a, _, _ in rows], *consts, *c_ins)


def rowwise_bwd(name, f, rows, consts, cts, row_grad_dtypes, tm, carry=None):
    n_r, n_c, n_ct = len(rows), len(consts), len(cts)
    t = rows[0][0].shape[0]
    keep = [k for k, dt in enumerate(row_grad_dtypes) if dt is not None]
    c_ins, c_in_specs, c_outs, c_out_specs, c_scratch = _host(carry)
    n_in, n_out = n_r + n_c + n_ct, len(keep) + n_c

    def body(*refs):
        ins = [r[...].astype(F32) for r in refs[:n_r + n_c]]
        g_out = [r[...].astype(F32) for r in refs[n_r + n_c:n_in]]
        out_refs = refs[n_in + len(c_ins):n_in + len(c_ins) + n_out]
        carried = (refs[n_in:n_in + len(c_ins)], refs[n_in + len(c_ins) + n_out:n_in + len(c_ins) + n_out + len(c_outs)],
                   *refs[n_in + len(c_ins) + n_out + len(c_outs):])
        if carry is not None:
            carry.emit_start(pl.program_id(0) == 0, *carried)

        def fw(*a):
            res = f(*a)
            return tuple(res) if isinstance(res, (tuple, list)) else (res,)

        _, vjp = jax.vjp(fw, *ins)
        grads = vjp(tuple(g_out))
        for o_ref, k in zip(out_refs[:len(keep)], keep):
            o_ref[...] = grads[k].astype(o_ref.dtype)
        first = pl.program_id(0) == 0
        for o_ref, g in zip(out_refs[len(keep):], grads[n_r:]):
            @pl.when(first)
            def _(o_ref=o_ref, g=g):
                o_ref[...] = g

            @pl.when(jnp.logical_not(first))
            def _(o_ref=o_ref, g=g):
                o_ref[...] += g
        if carry is not None:
            carry.emit_finish(pl.program_id(0) == t // tm - 1, *carried)

    return pl.pallas_call(
        body, name=name, grid=(t // tm,),
        in_specs=[_row_spec(tm, w, cb) for _, w, cb in rows] + [_const_spec(c.shape) for c in consts]
        + [_row_spec(tm, w, cb) for _, w, cb in cts] + c_in_specs,
        out_specs=[_row_spec(tm, rows[k][1], 0) for k in keep] + [_const_spec(c.shape) for c in consts] + c_out_specs,
        out_shape=[jax.ShapeDtypeStruct((t, rows[k][1]), row_grad_dtypes[k]) for k in keep]
        + [jax.ShapeDtypeStruct(c.shape, F32) for c in consts] + c_outs,
        scratch_shapes=c_scratch,
        compiler_params=_cparams(1),
    )(*[a for a, _, _ in rows], *consts, *[a for a, _, _ in cts], *c_ins)


def f_norm_mod(x, g, scale, shift):
    y = x * lax.rsqrt(jnp.mean(x * x, axis=-1, keepdims=True) + EPS) * g
    return y * (1.0 + scale) + shift


def f_conf_tail(u, zb, ln_g, ln_b, pw2_w, pw2_b):
    mu = jnp.mean(u, axis=-1, keepdims=True)
    xc = u - mu
    var = jnp.mean(xc * xc, axis=-1, keepdims=True)
    y = _silu(xc * lax.rsqrt(var + EPS) * ln_g + ln_b)
    return (mm(y, pw2_w) + pw2_b) * _silu(zb)


def f_merge(ya, yb, yc, mg, x, gate, wpa, wpb, wpc, wout):
    d = D_MODEL
    merged = (_sigmoid(mg[:, :d]) * mm(ya, wpa) + _sigmoid(mg[:, d:2 * d]) * mm(yb, wpb)
              + _sigmoid(mg[:, 2 * d:]) * mm(yc, wpc))
    return x + gate * mm(merged, wout)


def matmul_nn(name, a, b, out_dtype, tm, tn, tk, b_transposed=False, carry=None):
    m, k = a.shape
    n = b.shape[0] if b_transposed else b.shape[1]
    nk = k // tk
    grid = (m // tm, n // tn, nk)
    b_spec = (pl.BlockSpec((tn, tk), lambda i, j, kk: (j, kk)) if b_transposed
              else pl.BlockSpec((tk, tn), lambda i, j, kk: (kk, j)))
    c_ins, c_in_specs, c_outs, c_out_specs, c_scratch = _host(carry)
    n_ci, n_co = len(c_ins), len(c_outs)

    def body(*refs):
        a_ref, b_ref, o_ref = refs[0], refs[1], refs[2 + n_ci]
        carried = (refs[2:2 + n_ci], refs[3 + n_ci:3 + n_ci + n_co], *refs[3 + n_ci + n_co:3 + n_ci + n_co + len(c_scratch)])
        at = lambda step: functools.reduce(jnp.logical_and, [pl.program_id(d) == s for d, s in enumerate(step)])
        if carry is not None:
            carry.emit_start(at((0, 0, 0)), *carried)
        part = lax.dot_general(a_ref[...].astype(BF16), b_ref[...].astype(BF16),
                               (((1,), (1 if b_transposed else 0,)), ((), ())), preferred_element_type=F32)
        if nk == 1:
            o_ref[...] = part.astype(o_ref.dtype)
        else:
            kk = pl.program_id(2)
            acc_ref = refs[-1]

            @pl.when(kk == 0)
            def _():
                acc_ref[...] = part

            @pl.when(kk > 0)
            def _():
                acc_ref[...] += part

            @pl.when(kk == nk - 1)
            def _():
                o_ref[...] = acc_ref[...].astype(o_ref.dtype)
        if carry is not None:
            carry.emit_finish(at(tuple(g - 1 for g in grid)), *carried)

    res = pl.pallas_call(
        body, name=name, grid=grid,
        in_specs=[pl.BlockSpec((tm, tk), lambda i, j, kk: (i, kk)), b_spec] + c_in_specs,
        out_specs=[pl.BlockSpec((tm, tn), lambda i, j, kk: (i, j))] + c_out_specs,
        out_shape=[jax.ShapeDtypeStruct((m, n), out_dtype)] + c_outs,
        input_output_aliases={} if carry is None else carry.aliases(2, 1),
        scratch_shapes=c_scratch + ([] if nk == 1 else [pltpu.VMEM((tm, tn), F32)]),
        compiler_params=_cparams(3),
    )(a, b, *c_ins)
    return res[0] if carry is None else res


def ada_fwd(c8, w_shard, b_ada):
    n_cols = w_shard.shape[2]
    masks = [(m >> 2 & 1, m >> 1 & 1, m & 1) for m in range(1, 8)]

    def body(c_ref, w_ref, b_ref, mod_ref, conds_ref, cbuf, sendbuf, recvbuf, send_sems, recv_sems):
        x, y, c, chips = _place()
        flip = lambda v, bit: 1 - v if bit else v
        peers = [(flip(x, mx), flip(y, my), flip(c, mc)) for mx, my, mc in masks]
        dev = lambda p: 4 * p[0] + 2 * p[1] + p[2]
        cbuf[dev((x, y, c))] = c_ref[...]
        first = [_remote(c_ref, cbuf.at[dev((x, y, c))], send_sems.at[i], recv_sems.at[i], p) for i, p in enumerate(peers)]
        for cp in first:
            cp.start()
        for i, p in enumerate(peers):
            _remote(c_ref, cbuf.at[dev(p)], send_sems.at[i], recv_sems.at[i], p).wait_recv()
        conds = jnp.concatenate([cbuf[d, 0:1, :] for d in range(8)], axis=0)
        conds_ref[...] = conds
        act = _silu(conds)
        parts = [mm(act, w_ref[l]) for l in range(DEPTH)]
        row8 = lax.broadcasted_iota(jnp.int32, (8, 1), 0)

        def tile_for(chip):
            r = 2 * (2 * chip[0] + chip[1]) + c
            rows = [jnp.sum(jnp.where(row8 == r, parts[l], 0.0), axis=0, keepdims=True) for l in range(DEPTH)]
            return jnp.where(row8 == 0, rows[0], jnp.where(row8 == 1, rows[1], 0.0))

        my_slot = 2 * x + y
        recvbuf[my_slot] = tile_for((x, y))
        second = []
        for j, chip in enumerate(chips):
            sendbuf[j] = tile_for(chip)
            second.append(_remote(sendbuf.at[j], recvbuf.at[my_slot], send_sems.at[7 + j], recv_sems.at[7 + j], (*chip, c)))
            second[-1].start()
        for j, chip in enumerate(chips):
            _remote(sendbuf.at[j], recvbuf.at[2 * chip[0] + chip[1]], send_sems.at[7 + j], recv_sems.at[7 + j],
                    (*chip, c)).wait_recv()
        rows = [jnp.concatenate([recvbuf[k, l:l + 1, :] for k in range(N_CHIPS)], axis=1) + b_ref[l:l + 1, :]
                for l in range(DEPTH)]
        mod_ref[...] = jnp.concatenate(rows + [jnp.zeros((8 - DEPTH, N_CHIPS * n_cols), F32)], axis=0)
        for cp in first + second:
            cp.wait_send()

    vm = pl.BlockSpec(memory_space=pltpu.VMEM)
    return pl.pallas_call(
        body, name="ada_fwd",
        out_shape=[jax.ShapeDtypeStruct((8, N_CHIPS * n_cols), F32), jax.ShapeDtypeStruct((8, D_MODEL), F32)],
        in_specs=[vm, vm, vm], out_specs=[vm, vm],
        scratch_shapes=[pltpu.VMEM((8, 8, D_MODEL), F32), pltpu.VMEM((3, 8, n_cols), F32),
                        pltpu.VMEM((N_CHIPS, 8, n_cols), F32), pltpu.SemaphoreType.DMA((10,)), pltpu.SemaphoreType.DMA((10,))],
        compiler_params=pltpu.CompilerParams(vmem_limit_bytes=VMEM_LIMIT),
    )(c8, w_shard, b_ada)


def ada_bwd(conds, dmod):
    def body(c_ref, d_ref, o_ref):
        act = _silu(c_ref[...])
        for l in range(DEPTH):
            o_ref[l] = mm_tn(act, d_ref[l])

    return pl.pallas_call(
        body, name="ada_bwd", out_shape=jax.ShapeDtypeStruct((DEPTH, D_MODEL, dmod.shape[2]), F32),
        compiler_params=pltpu.CompilerParams(vmem_limit_bytes=VMEM_LIMIT),
    )(conds, dmod)


def _f_attn(first_block, q, za, kc, vc, kp, vp, qg, kg, sinks):
    w = WINDOW
    lane = lax.broadcasted_iota(jnp.int32, (1, 128), 1)
    halves = [lane < 64, lane >= 64]

    def rms_halves(x, g):
        x2 = x * x
        s0 = jnp.sum(jnp.where(halves[0], x2, 0.0), axis=-1, keepdims=True)
        s1 = jnp.sum(jnp.where(halves[1], x2, 0.0), axis=-1, keepdims=True)
        r = jnp.where(halves[0], lax.rsqrt(s0 / 64.0 + EPS), lax.rsqrt(s1 / 64.0 + EPS))
        return x * r * g

    kcat = rms_halves(jnp.concatenate([kp, kc], axis=0), kg)
    vcat = jnp.concatenate([vp, vc], axis=0)
    qi = lax.broadcasted_iota(jnp.int32, (w, 2 * w), 0)
    kj = lax.broadcasted_iota(jnp.int32, (w, 2 * w), 1)
    dist = qi + w - kj
    valid = (dist >= 0) & (dist < w) & (jnp.logical_not(first_block) | (kj >= w))
    distf = dist.astype(F32)
    units = [(grp, half) for grp in range(4) for half in range(2)]
    qns = [rms_halves(q[:, 128 * grp:128 * grp + 128], qg) * (ATT_HEAD_DIM ** -0.5) for grp in range(4)]
    vhalf = [jnp.where(halves[half], vcat, 0.0) for half in range(2)]
    scores, sinks_h = [], []
    for grp, half in units:
        head = HEAD_ORDER[2 * grp + half]
        slope = 2.0 ** (-8.0 * (head + 1) / ATT_HEADS)
        sinks_h.append(jnp.sum(jnp.where(lane == head, sinks, 0.0), axis=-1, keepdims=True))
        s = mm_nt(jnp.where(halves[half], qns[grp], 0.0), kcat) - slope * distf
        scores.append(jnp.where(valid, s, NEG_INF))
    probs = []
    for s, sink in zip(scores, sinks_h):
        m = lax.stop_gradient(jnp.maximum(jnp.max(s, axis=-1, keepdims=True), sink))
        p = jnp.exp(s - m)
        probs.append(p / (jnp.sum(p, axis=-1, keepdims=True) + jnp.exp(sink - m)))
    outs = [mm(p, vhalf[half]) for p, (grp, half) in zip(probs, units)]
    return jnp.concatenate([outs[2 * grp] + outs[2 * grp + 1] for grp in range(4)], axis=1) * _silu(za)


def attn_fwd(name, proj, qg, kg, sinks):
    t = proj.shape[0]
    nb = t // WINDOW

    def body(q_ref, za_ref, kc_ref, vc_ref, kp_ref, vp_ref, qg_ref, kg_ref, s_ref, o_ref):
        first = pl.program_id(0) == 0
        o_ref[...] = _f_attn(first, q_ref[...], za_ref[...], kc_ref[...], vc_ref[...], kp_ref[...], vp_ref[...],
                             qg_ref[...], kg_ref[...], s_ref[...])

    cur = lambda cb: (lambda i: (i, cb))
    prev = lambda cb: (lambda i: (jnp.maximum(i - 1, 0), cb))
    return pl.pallas_call(
        body, name=name, grid=(nb,),
        in_specs=[pl.BlockSpec((WINDOW, 512), cur(P_QA // 512)), pl.BlockSpec((WINDOW, 512), cur(P_ZA // 512)),
                  pl.BlockSpec((WINDOW, 128), cur(P_KA // 128)), pl.BlockSpec((WINDOW, 128), cur(P_VA // 128)),
                  pl.BlockSpec((WINDOW, 128), prev(P_KA // 128)), pl.BlockSpec((WINDOW, 128), prev(P_VA // 128)),
                  _const_spec((1, 128)), _const_spec((1, 128)), _const_spec((1, 128))],
        out_specs=pl.BlockSpec((WINDOW, 512), lambda i: (i, 0)),
        out_shape=jax.ShapeDtypeStruct((t, 512), F32),
        compiler_params=_cparams(1),
    )(proj, proj, proj, proj, proj, proj, qg, kg, sinks)


def attn_bwd(name, proj, qg, kg, sinks, dya):
    t = proj.shape[0]
    nb = t // WINDOW

    def body(q_ref, za_ref, kc_ref, vc_ref, kp_ref, vp_ref, qg_ref, kg_ref, s_ref, dy_ref,
             dqz_ref, dkv_ref, dqg_ref, dkg_ref, ds_ref, carry_ref):
        j = pl.program_id(0)
        first = j == nb - 1

        @pl.when(j == 0)
        def _():
            carry_ref[...] = jnp.zeros_like(carry_ref)
            dqg_ref[...] = jnp.zeros_like(dqg_ref)
            dkg_ref[...] = jnp.zeros_like(dkg_ref)
            ds_ref[...] = jnp.zeros_like(ds_ref)

        ins = [r[...] for r in (q_ref, za_ref, kc_ref, vc_ref, kp_ref, vp_ref, qg_ref, kg_ref, s_ref)]
        _, vjp = jax.vjp(functools.partial(_f_attn, first), *ins)
        dq, dza, dkc, dvc, dkp, dvp, dqg, dkg, dsk = vjp(dy_ref[...])
        dqz_ref[:, 0:512] = dq.astype(dqz_ref.dtype)
        dqz_ref[:, 512:1024] = dza.astype(dqz_ref.dtype)
        dkv_ref[:, 0:128] = (dkc + carry_ref[0]).astype(dkv_ref.dtype)
        dkv_ref[:, 128:256] = (dvc + carry_ref[1]).astype(dkv_ref.dtype)
        carry_ref[0] = dkp
        carry_ref[1] = dvp
        dqg_ref[...] += dqg
        dkg_ref[...] += dkg
        ds_ref[...] += dsk

    cur = lambda cb: (lambda j: (nb - 1 - j, cb))
    prev = lambda cb: (lambda j: (jnp.maximum(nb - 2 - j, 0), cb))
    return pl.pallas_call(
        body, name=name, grid=(nb,),
        in_specs=[pl.BlockSpec((WINDOW, 512), cur(P_QA // 512)), pl.BlockSpec((WINDOW, 512), cur(P_ZA // 512)),
                  pl.BlockSpec((WINDOW, 128), cur(P_KA // 128)), pl.BlockSpec((WINDOW, 128), cur(P_VA // 128)),
                  pl.BlockSpec((WINDOW, 128), prev(P_KA // 128)), pl.BlockSpec((WINDOW, 128), prev(P_VA // 128)),
                  _const_spec((1, 128)), _const_spec((1, 128)), _const_spec((1, 128)),
                  pl.BlockSpec((WINDOW, 512), cur(0))],
        out_specs=[pl.BlockSpec((WINDOW, 1024), cur(0)), pl.BlockSpec((WINDOW, 256), cur(0)),
                   _const_spec((1, 128)), _const_spec((1, 128)), _const_spec((1, 128))],
        out_shape=[jax.ShapeDtypeStruct((t, 1024), BF16), jax.ShapeDtypeStruct((t, 256), BF16),
                   jax.ShapeDtypeStruct((1, 128), F32), jax.ShapeDtypeStruct((1, 128), F32),
                   jax.ShapeDtypeStruct((1, 128), F32)],
        scratch_shapes=[pltpu.VMEM((2, WINDOW, 128), F32)],
        compiler_params=_cparams(1),
    )(proj, proj, proj, proj, proj, proj, qg, kg, sinks, dya)


CONV_ROWS = 256


def _conv_taps(src_ref, w_ref, n_taps, base, t):
    for r0 in range(0, t, CONV_ROWS):
        acc = w_ref[0:1, :] * src_ref[pl.ds(r0 + base, CONV_ROWS), :]
        for k in range(1, n_taps):
            acc = acc + w_ref[k:k + 1, :] * src_ref[pl.ds(r0 + base + k, CONV_ROWS), :]
        yield r0, acc


def _conv_wgrad(dy_ref, src_ref, n_taps, base, t, dy_base=0):
    out = []
    for k in range(n_taps):
        acc = jnp.zeros((8, 128), F32)
        for r0 in range(0, t, CONV_ROWS):
            prod = dy_ref[pl.ds(r0 + dy_base, CONV_ROWS), :] * src_ref[pl.ds(r0 + base + k, CONV_ROWS), :]
            acc = acc + jnp.sum(prod.reshape(CONV_ROWS // 8, 8, 128), axis=0)
        out.append(jnp.sum(acc, axis=0, keepdims=True))
    return out


def glu_conv_fwd(name, proj, w32, bias):
    t = proj.shape[0]
    pad = 32

    def body(x_ref, w_ref, b_ref, o_ref, u_ref):
        u_ref[0:pad, :] = jnp.zeros((pad, 128), F32)
        u_ref[pad:pad + t, :] = x_ref[:, 0:128] * _sigmoid(x_ref[:, 128:256])
        for r0, acc in _conv_taps(u_ref, w_ref, CONV_K, pad - (CONV_K - 1), t):
            o_ref[pl.ds(r0, CONV_ROWS), :] = acc + b_ref[...]

    return pl.pallas_call(
        body, name=name, grid=(4,),
        in_specs=[pl.BlockSpec((t, 256), lambda cb: (0, P_GLU // 256 + cb)), pl.BlockSpec((32, 128), lambda cb: (0, cb)),
                  pl.BlockSpec((1, 128), lambda cb: (0, cb))],
        out_specs=pl.BlockSpec((t, 128), lambda cb: (0, cb)),
        out_shape=jax.ShapeDtypeStruct((t, 512), F32),
        scratch_shapes=[pltpu.VMEM((t + pad, 128), F32)],
        compiler_params=_cparams(1),
    )(proj, w32, bias)


def glu_conv_bwd(name, proj, w32, dub):
    t = proj.shape[0]
    pad = 32
    k1 = CONV_K - 1

    def body(x_ref, w_ref, dy_ref, dx_ref, dw_ref, db_ref, u_ref, dyp_ref, wrev_ref):
        val = x_ref[:, 0:128]
        sg = _sigmoid(x_ref[:, 128:256])
        u_ref[0:pad, :] = jnp.zeros((pad, 128), F32)
        u_ref[pad:pad + t, :] = val * sg
        dyp_ref[0:t, :] = dy_ref[...]
        dyp_ref[t:t + pad, :] = jnp.zeros((pad, 128), F32)
        for k in range(CONV_K):
            wrev_ref[k:k + 1, :] = w_ref[k1 - k:k1 - k + 1, :]
        wrev_ref[CONV_K:32, :] = jnp.zeros((32 - CONV_K, 128), F32)
        for r0, du in _conv_taps(dyp_ref, wrev_ref, CONV_K, 0, t):
            v = x_ref[pl.ds(r0, CONV_ROWS), 0:128]
            s = _sigmoid(x_ref[pl.ds(r0, CONV_ROWS), 128:256])
            dx_ref[pl.ds(r0, CONV_ROWS), 0:128] = (du * s).astype(dx_ref.dtype)
            dx_ref[pl.ds(r0, CONV_ROWS), 128:256] = (du * v * s * (1.0 - s)).astype(dx_ref.dtype)
        dws = _conv_wgrad(dyp_ref, u_ref, CONV_K, pad - k1, t)
        for k in range(CONV_K):
            dw_ref[k:k + 1, :] = dws[k]
        dw_ref[CONV_K:32, :] = jnp.zeros((32 - CONV_K, 128), F32)
        db_ref[...] = jnp.sum(dy_ref[...], axis=0, keepdims=True)

    return pl.pallas_call(
        body, name=name, grid=(4,),
        in_specs=[pl.BlockSpec((t, 256), lambda cb: (0, P_GLU // 256 + cb)), pl.BlockSpec((32, 128), lambda cb: (0, cb)),
                  pl.BlockSpec((t, 128), lambda cb: (0, cb))],
        out_specs=[pl.BlockSpec((t, 256), lambda cb: (0, cb)), pl.BlockSpec((32, 128), lambda cb: (0, cb)),
                   pl.BlockSpec((1, 128), lambda cb: (0, cb))],
        out_shape=[jax.ShapeDtypeStruct((t, 1024), BF16), jax.ShapeDtypeStruct((32, 512), F32),
                   jax.ShapeDtypeStruct((1, 512), F32)],
        scratch_shapes=[pltpu.VMEM((t + pad, 128), F32), pltpu.VMEM((t + pad, 128), F32), pltpu.VMEM((32, 128), F32)],
        compiler_params=_cparams(1),
    )(proj, w32, dub)


def sconv_fwd(name, proj, w8):
    t = proj.shape[0]
    pad = 8
    k1 = DN_CONV_K - 1

    def body(x_ref, w_ref, o_ref, xp_ref):
        xp_ref[0:pad, :] = jnp.zeros((pad, 128), F32)
        xp_ref[pad:pad + t, :] = x_ref[...]
        for r0, acc in _conv_taps(xp_ref, w_ref, DN_CONV_K, pad - k1, t):
            o_ref[pl.ds(r0, CONV_ROWS), :] = _silu(acc)

    return pl.pallas_call(
        body, name=name, grid=(12,),
        in_specs=[pl.BlockSpec((t, 128), lambda cb: (0, P_QKV // 128 + cb)), pl.BlockSpec((8, 128), lambda cb: (0, cb))],
        out_specs=pl.BlockSpec((t, 128), lambda cb: (0, cb)),
        out_shape=jax.ShapeDtypeStruct((t, 1536), F32),
        scratch_shapes=[pltpu.VMEM((t + pad, 128), F32)],
        compiler_params=_cparams(1),
    )(proj, w8)


def sconv_bwd(name, proj, w8, dqkv):
    t = proj.shape[0]
    pad = 8
    k1 = DN_CONV_K - 1

    def body(x_ref, w_ref, dy_ref, dx_ref, dw_ref, xp_ref, dpp_ref, wrev_ref):
        xp_ref[0:pad, :] = jnp.zeros((pad, 128), F32)
        xp_ref[pad:pad + t, :] = x_ref[...]
        for r0, pre in _conv_taps(xp_ref, w_ref, DN_CONV_K, pad - k1, t):
            s = _sigmoid(pre)
            dpp_ref[pl.ds(r0, CONV_ROWS), :] = dy_ref[pl.ds(r0, CONV_ROWS), :] * (s * (1.0 + pre * (1.0 - s)))
        dpp_ref[t:t + pad, :] = jnp.zeros((pad, 128), F32)
        for k in range(DN_CONV_K):
            wrev_ref[k:k + 1, :] = w_ref[k1 - k:k1 - k + 1, :]
        wrev_ref[DN_CONV_K:8, :] = jnp.zeros((8 - DN_CONV_K, 128), F32)
        for r0, dx in _conv_taps(dpp_ref, wrev_ref, DN_CONV_K, 0, t):
            dx_ref[pl.ds(r0, CONV_ROWS), :] = dx.astype(dx_ref.dtype)
        dws = _conv_wgrad(dpp_ref, xp_ref, DN_CONV_K, pad - k1, t)
        for k in range(DN_CONV_K):
            dw_ref[k:k + 1, :] = dws[k]
        dw_ref[DN_CONV_K:8, :] = jnp.zeros((8 - DN_CONV_K, 128), F32)

    return pl.pallas_call(
        body, name=name, grid=(12,),
        in_specs=[pl.BlockSpec((t, 128), lambda cb: (0, P_QKV // 128 + cb)), pl.BlockSpec((8, 128), lambda cb: (0, cb)),
                  pl.BlockSpec((t, 128), lambda cb: (0, cb))],
        out_specs=[pl.BlockSpec((t, 128), lambda cb: (0, cb)), pl.BlockSpec((8, 128), lambda cb: (0, cb))],
        out_shape=[jax.ShapeDtypeStruct((t, 1536), BF16), jax.ShapeDtypeStruct((8, 1536), F32)],
        scratch_shapes=[pltpu.VMEM((t + pad, 128), F32), pltpu.VMEM((t + pad, 128), F32), pltpu.VMEM((8, 128), F32)],
        compiler_params=_cparams(1),
    )(proj, w8, dqkv)


def _f_delta_step(qkv, ab, zc, s0, s1, s2, s3, a_log, dt_bias, dn_g, inverses=None, with_inverses=False):
    cs = DN_CHUNK
    n = 2 * cs
    states = (s0, s1, s2, s3)
    lane = lax.broadcasted_iota(jnp.int32, (1, 128), 1)
    ri = lax.broadcasted_iota(jnp.int32, (n, n), 0)
    ci = lax.broadcasted_iota(jnp.int32, (n, n), 1)
    same = (ri // cs) == (ci // cs)
    lower = same & (ri >= ci)
    strict = same & (ri > ci)
    sums = jnp.concatenate([jnp.where(lower, 1.0, 0.0), jnp.where(same, 1.0, 0.0), jnp.where(ci < cs, 1.0, 0.0),
                            jnp.where(ci >= cs, 1.0, 0.0)], axis=0)
    top = lax.broadcasted_iota(jnp.int32, (n, 1), 0) < cs

    def pick(row, idx):
        return jnp.sum(jnp.where(lane == idx, row, 0.0), axis=-1, keepdims=True)

    def l2n(x):
        return x * lax.rsqrt(jnp.sum(x * x, axis=-1, keepdims=True) + EPS)

    n_chunks = qkv.shape[0] // cs
    units = [(k, pair) for k in range(n_chunks) for pair in range(2)]

    pre = []
    for k, pair in units:
        hs = (2 * pair, 2 * pair + 1)
        rows = slice(k * cs, (k + 1) * cs)
        stack = lambda f: jnp.concatenate([f(hs[0]), f(hs[1])], axis=0)
        qd = l2n(stack(lambda h: qkv[rows, 128 * h:128 * h + 128])) * (128 ** -0.5)
        kd = l2n(stack(lambda h: qkv[rows, 512 + 128 * h:512 + 128 * h + 128]))
        vd = stack(lambda h: qkv[rows, 1024 + 128 * h:1024 + 128 * h + 128])
        beta = _sigmoid(stack(lambda h: pick(ab[rows], 4 + h)))
        g = stack(lambda h: -jnp.exp(pick(a_log, h)) * _softplus(pick(ab[rows], h) + pick(dt_bias, h)))
        g_sums = sel_mm(sums, g * jnp.ones((1, n), F32))
        gc_col = g_sums[0:n]
        gl_b = g_sums[n:2 * n]
        g_end = (g_sums[2 * n:3 * n], g_sums[3 * n:])
        decay = jnp.where(lower, jnp.exp(jnp.where(lower, gc_col - gc_col.T, 0.0)), 0.0)
        kb = kd * beta
        pre.append(dict(qd=qd, kd=kd, vb=vd * beta, kb=kb, gc_col=gc_col, gl_b=gl_b, g_end=g_end, decay=decay,
                        a=jnp.where(strict, mm_nt(kb, kd) * decay, 0.0)))
    if inverses is None:
        tmats = tri_inv(*[p["a"] for p in pre])
    else:
        tmats = [tri_inv_known(p["a"], t) for p, t in zip(pre, inverses)]

    mid = []
    for p, tmat in zip(pre, tmats):
        egc = jnp.exp(p["gc_col"])
        mid.append(dict(u=mm(tmat, p["vb"]), wm=mm(tmat, p["kb"] * egc), qe=p["qd"] * egc,
                        intra=jnp.where(lower, mm_nt(p["qd"], p["kd"]) * p["decay"], 0.0),
                        ke=p["kd"] * jnp.exp(p["gl_b"] - p["gc_col"]), g_end=p["g_end"]))

    ys = []
    for k in range(n_chunks):
        rows = slice(k * cs, (k + 1) * cs)
        new_states, y_heads = [], []
        for pair in range(2):
            m = mid[2 * k + pair]
            hs = (2 * pair, 2 * pair + 1)
            st = (states[hs[0]], states[hs[1]])
            v_new = m["u"] - jnp.concatenate([mm(m["wm"][:cs], st[0]), mm(m["wm"][cs:], st[1])], axis=0)
            o = jnp.concatenate([mm(m["qe"][:cs], st[0]), mm(m["qe"][cs:], st[1])], axis=0) + mm(m["intra"], v_new)
            new_states.append(st[0] * jnp.exp(m["g_end"][0]) + mm_tn(jnp.where(top, m["ke"], 0.0), v_new))
            new_states.append(st[1] * jnp.exp(m["g_end"][1]) + mm_tn(jnp.where(top, 0.0, m["ke"]), v_new))
            od = o * lax.rsqrt(jnp.mean(o * o, axis=-1, keepdims=True) + EPS) * dn_g
            y_heads += [od[:cs] * _silu(zc[rows, 128 * hs[0]:128 * hs[0] + 128]),
                        od[cs:] * _silu(zc[rows, 128 * hs[1]:128 * hs[1] + 128])]
        states = tuple(new_states)
        ys.append(jnp.concatenate(y_heads, axis=1))
    if with_inverses:
        return (jnp.concatenate(ys, axis=0), *states), tmats
    return (jnp.concatenate(ys, axis=0), *states)


DELTA_ROWS = 4 * DN_CHUNK
DELTA_UNITS = 2 * DELTA_ROWS // DN_CHUNK


def delta_fwd(name, qkv, proj, a_log, dt_bias, dn_g):
    t = qkv.shape[0]
    nc = t // DELTA_ROWS

    def body(qkv_ref, ab_ref, zc_ref, al_ref, dt_ref, g_ref, y_ref, ssave_ref, tsave_ref, s_ref):
        @pl.when(pl.program_id(0) == 0)
        def _():
            s_ref[...] = jnp.zeros_like(s_ref)

        ssave_ref[0] = s_ref[...]
        st = [s_ref[128 * h:128 * h + 128, :] for h in range(4)]
        (y, *ns), tmats = _f_delta_step(qkv_ref[...], ab_ref[...], zc_ref[...], *st, al_ref[...], dt_ref[...], g_ref[...],
                                        with_inverses=True)
        y_ref[...] = y
        for h in range(4):
            s_ref[128 * h:128 * h + 128, :] = ns[h]
        for u, tm in enumerate(tmats):
            tsave_ref[0, 128 * u:128 * u + 128, :] = tm

    return pl.pallas_call(
        body, name=name, grid=(nc,),
        in_specs=[pl.BlockSpec((DELTA_ROWS, 1536), lambda i: (i, 0)), pl.BlockSpec((DELTA_ROWS, 128), lambda i: (i, P_AB // 128)),
                  pl.BlockSpec((DELTA_ROWS, 512), lambda i: (i, P_ZC // 512)),
                  _const_spec((1, 128)), _const_spec((1, 128)), _const_spec((1, 128))],
        out_specs=[pl.BlockSpec((DELTA_ROWS, 512), lambda i: (i, 0)), pl.BlockSpec((1, 512, 128), lambda i: (i, 0, 0)),
                   pl.BlockSpec((1, DELTA_UNITS * 128, 128), lambda i: (i, 0, 0))],
        out_shape=[jax.ShapeDtypeStruct((t, 512), F32), jax.ShapeDtypeStruct((nc, 512, 128), F32),
                   jax.ShapeDtypeStruct((nc, DELTA_UNITS * 128, 128), F32)],
        scratch_shapes=[pltpu.VMEM((512, 128), F32)],
        compiler_params=_cparams(1),
    )(qkv, proj, proj, a_log, dt_bias, dn_g)


def delta_bwd(name, qkv, proj, ssave, tsave, a_log, dt_bias, dn_g, dyc, carry=None):
    t = qkv.shape[0]
    nc = t // DELTA_ROWS
    c_ins, c_in_specs, c_outs, c_out_specs, c_scratch = _host(carry)
    n_ci, n_co = len(c_ins), len(c_outs)

    def body(*refs):
        qkv_ref, ab_ref, zc_ref, ss_ref, ts_ref, al_ref, dt_ref, g_ref, dy_ref = refs[:9]
        dqkv_ref, dab_ref, dzc_ref, dal_ref, ddt_ref, dg_ref = refs[9 + n_ci:15 + n_ci]
        ds_ref = refs[15 + n_ci + n_co]
        carried = (refs[9:9 + n_ci], refs[15 + n_ci:15 + n_ci + n_co], *refs[16 + n_ci + n_co:])

        @pl.when(pl.program_id(0) == 0)
        def _():
            ds_ref[...] = jnp.zeros_like(ds_ref)
            dal_ref[...] = jnp.zeros_like(dal_ref)
            ddt_ref[...] = jnp.zeros_like(ddt_ref)
            dg_ref[...] = jnp.zeros_like(dg_ref)

        if carry is not None:
            carry.emit_start(pl.program_id(0) == 0, *carried)

        st = [ss_ref[0, 128 * h:128 * h + 128, :] for h in range(4)]
        known = [ts_ref[0, 128 * u:128 * u + 128, :] for u in range(DELTA_UNITS)]
        _, vjp = jax.vjp(functools.partial(_f_delta_step, inverses=known), qkv_ref[...], ab_ref[...], zc_ref[...], *st,
                         al_ref[...], dt_ref[...], g_ref[...])
        dst = tuple(ds_ref[128 * h:128 * h + 128, :] for h in range(4))
        dqkv, dab, dzc, d0, d1, d2, d3, dal, ddt, dg = vjp((dy_ref[...], *dst))
        dqkv_ref[...] = dqkv
        dab_ref[...] = dab.astype(dab_ref.dtype)
        dzc_ref[...] = dzc.astype(dzc_ref.dtype)
        for h, d in enumerate((d0, d1, d2, d3)):
            ds_ref[128 * h:128 * h + 128, :] = d
        dal_ref[...] += dal
        ddt_ref[...] += ddt
        dg_ref[...] += dg

        if carry is not None:
            carry.emit_finish(pl.program_id(0) == nc - 1, *carried)

    rev = lambda cb: (lambda j: (nc - 1 - j, cb))
    return pl.pallas_call(
        body, name=name, grid=(nc,),
        in_specs=[pl.BlockSpec((DELTA_ROWS, 1536), rev(0)), pl.BlockSpec((DELTA_ROWS, 128), rev(P_AB // 128)),
                  pl.BlockSpec((DELTA_ROWS, 512), rev(P_ZC // 512)), pl.BlockSpec((1, 512, 128), lambda j: (nc - 1 - j, 0, 0)),
                  pl.BlockSpec((1, DELTA_UNITS * 128, 128), lambda j: (nc - 1 - j, 0, 0)),
                  _const_spec((1, 128)), _const_spec((1, 128)), _const_spec((1, 128)),
                  pl.BlockSpec((DELTA_ROWS, 512), rev(0))] + c_in_specs,
        out_specs=[pl.BlockSpec((DELTA_ROWS, 1536), rev(0)), pl.BlockSpec((DELTA_ROWS, 128), rev(0)),
                   pl.BlockSpec((DELTA_ROWS, 512), rev(0)),
                   _const_spec((1, 128)), _const_spec((1, 128)), _const_spec((1, 128))] + c_out_specs,
        out_shape=[jax.ShapeDtypeStruct((t, 1536), F32), jax.ShapeDtypeStruct((t, 128), BF16),
                   jax.ShapeDtypeStruct((t, 512), BF16),
                   jax.ShapeDtypeStruct((1, 128), F32), jax.ShapeDtypeStruct((1, 128), F32), jax.ShapeDtypeStruct((1, 128), F32)]
        + c_outs,
        scratch_shapes=[pltpu.VMEM((512, 128), F32)] + c_scratch,
        compiler_params=_cparams(1),
    )(qkv, proj, proj, ssave, tsave, a_log, dt_bias, dn_g, dyc, *c_ins)


def loss_head(name, y, target, tm):
    t, d = y.shape

    def body(y_ref, t_ref, dy_ref, l_ref):
        err = y_ref[...] - t_ref[...]
        dy_ref[...] = err * (1.0 / d)
        part = 0.5 * jnp.sum(jnp.sum(err * err, axis=-1, keepdims=True) * (1.0 / d), axis=0, keepdims=True)

        @pl.when(pl.program_id(0) == 0)
        def _():
            l_ref[...] = part

        @pl.when(pl.program_id(0) > 0)
        def _():
            l_ref[...] += part

    return pl.pallas_call(
        body, name=name, grid=(t // tm,),
        in_specs=[_row_spec(tm, d, 0), _row_spec(tm, d, 0)],
        out_specs=[_row_spec(tm, d, 0), _const_spec((1, 1))],
        out_shape=[jax.ShapeDtypeStruct((t, d), F32), jax.ShapeDtypeStruct((1, 1), F32)],
        compiler_params=_cparams(1),
    )(y, target)


TM = 512
TM_MERGE = 256
TM_IN = 1024
TN_IN = 1152


def _lane_pad(v, n=128):
    return jnp.pad(v.astype(F32), (0, n - v.shape[0]))[None, :]


def f_norm_mod_res(x, g, scale, shift):
    return f_norm_mod(x, g, scale, shift), x


def prep_layer(w):
    p = dict(w)
    p["wp"] = _w_in_assemble(w["w_in"])
    p["wpa"] = _perm_heads_rows(w["w_proj_a"])
    p["dw32"] = jnp.pad(w["dw_w"], ((0, 32 - CONV_K), (0, 0)))
    p["sconv8"] = jnp.pad(w["sconv_w"], ((0, 8 - DN_CONV_K), (0, 0)))
    p["qg"] = jnp.tile(w["q_norm_g"], 2)[None, :]
    p["kg"] = jnp.tile(w["k_norm_g"], 2)[None, :]
    p["sinks128"] = _lane_pad(w["sinks"])
    p["al"] = _lane_pad(w["a_log"])
    p["dtb"] = _lane_pad(w["dt_bias"])
    p["dng"] = w["dn_norm_g"][None, :]
    return p


def layer_fwd(tag, x, mod, p, carry_inproj=None, carry_merge=None):
    d = D_MODEL
    shift, scale, gate = mod[:, :d], mod[:, d:2 * d], mod[:, 2 * d:]
    g = p["norm_g"][None, :]
    h, h_t = rowwise_fwd(f"norm_fwd{tag}", lambda *a: (f_norm_mod(*a),) * 2, [(x, d, 0)], [g, scale, shift],
                         [(d, BF16), (d, BF16, "transposed")], TM)
    proj = matmul_nn(f"inproj_fwd{tag}", h, p["wp"], F32, TM_IN, TN_IN, d, carry=carry_inproj)
    proj, got_inproj = (proj, []) if carry_inproj is None else (proj[0], proj[1:])
    ya = attn_fwd(f"attn_fwd{tag}", proj, p["qg"], p["kg"], p["sinks128"])
    ub = glu_conv_fwd(f"glu_conv_fwd{tag}", proj, p["dw32"], p["dw_b"][None, :])
    conf_consts = [p["ln_g"][None, :], p["ln_b"][None, :], p["pw2_w"], p["pw2_b"][None, :]]
    (yb,) = rowwise_fwd(f"conf_fwd{tag}", f_conf_tail, [(ub, 512, 0), (proj, 512, P_ZB // 512)], conf_consts, [(512, F32)], TM)
    qkv = sconv_fwd(f"sconv_fwd{tag}", proj, p["sconv8"])
    yc, ssave, tsave = delta_fwd(f"delta_fwd{tag}", qkv, proj, p["al"], p["dtb"], p["dng"])
    merge_consts = [gate, p["wpa"], p["w_proj_b"], p["w_proj_c"], p["w_out"]]
    merge_rows = [(ya, 512, 0), (yb, 512, 0), (yc, 512, 0), (proj, 3 * d, P_MG // (3 * d)), (x, d, 0)]
    xn, *got_merge = rowwise_fwd(f"merge_fwd{tag}", f_merge, merge_rows, merge_consts, [(d, F32)], TM_MERGE, carry=carry_merge)
    saved = dict(x=x, h_t=h_t, proj=proj, ub=ub, qkv=qkv, ssave=ssave, tsave=tsave, norm_consts=[g, scale, shift],
                 conf_consts=conf_consts, merge_consts=merge_consts, merge_rows=merge_rows)
    return xn, saved, got_inproj, got_merge


def layer_bwd(tag, dxn, p, s, carry_merge=None, carry_delta=None, carry_dh=None, carry_norm=None):
    d = D_MODEL
    proj = s["proj"]
    dya, dyb, dyc, dmg, dgate, dwpa, dwpb, dwpc, dwout, *got_merge = rowwise_bwd(
        f"merge_bwd{tag}", f_merge, s["merge_rows"], s["merge_consts"], [(dxn, d, 0)], [F32, F32, F32, BF16, None], TM_MERGE,
        carry=carry_merge)
    carry_delta = None if carry_delta is None else carry_delta(got_merge)
    dqz, dkv, dqg, dkg, dsinks = attn_bwd(f"attn_bwd{tag}", proj, p["qg"], p["kg"], p["sinks128"], dya)
    dub, dzb, dln_g, dln_b, dpw2_w, dpw2_b = rowwise_bwd(
        f"conf_bwd{tag}", f_conf_tail, [(s["ub"], 512, 0), (proj, 512, P_ZB // 512)], s["conf_consts"], [(dyb, 512, 0)],
        [F32, BF16], TM)
    dglu, ddw32, ddw_b = glu_conv_bwd(f"glu_conv_bwd{tag}", proj, p["dw32"], dub)
    dqkv, dab, dzc, dal, ddtb, ddng, *got_delta = delta_bwd(f"delta_bwd{tag}", s["qkv"], proj, s["ssave"], s["tsave"], p["al"],
                                                            p["dtb"], p["dng"], dyc, carry_delta)
    dqkv_pre, dsconv8 = sconv_bwd(f"sconv_bwd{tag}", proj, p["sconv8"], dqkv)
    dproj = jnp.concatenate([dqz, dglu, dzb, dzc, dmg, dqkv_pre, dkv, dab], axis=1)
    dwp = matmul_nn(f"inproj_bwd_dw{tag}", s["h_t"], dproj, F32, d, TN_IN, 2048)
    reduced = dict(w_in=_w_in_grad_blocks(dwp), pw2_w=dpw2_w, w_proj_a=_unperm_heads_rows(dwpa), w_proj_b=dwpb, w_proj_c=dwpc,
                   w_out=dwout)
    carry_dh = None if carry_dh is None else carry_dh(reduced)
    dh = matmul_nn(f"inproj_bwd_dh{tag}", dproj, p["wp"], F32, TM_IN, d, P_TOTAL // 3, b_transposed=True, carry=carry_dh)
    dh, got_dh = (dh, []) if carry_dh is None else (dh[0], dh[1:])
    carry_norm = None if carry_norm is None else carry_norm(got_dh)
    dx, dnorm_g, dscale, dshift, *got_norm = rowwise_bwd(
        f"norm_bwd{tag}", f_norm_mod_res, [(s["x"], d, 0)], s["norm_consts"], [(dh, d, 0), (dxn, d, 0)], [F32], TM,
        carry=carry_norm)
    dmod = jnp.concatenate([dshift, dscale, dgate], axis=1)
    grads = dict(
        reduced, b_ada=dmod[0], norm_g=dnorm_g[0],
        q_norm_g=dqg[0, :64] + dqg[0, 64:], k_norm_g=dkg[0, :64] + dkg[0, 64:], sinks=dsinks[0, :ATT_HEADS],
        dw_w=ddw32[:CONV_K], dw_b=ddw_b[0], ln_g=dln_g[0], ln_b=dln_b[0], pw2_b=dpw2_b[0],
        sconv_w=dsconv8[:DN_CONV_K], a_log=dal[0, :DN_HEADS], dt_bias=ddtb[0, :DN_HEADS], dn_norm_g=ddng[0])
    return dx, grads, got_merge, got_delta, got_dh, got_norm


SHARDED = {"w_ada": 2, "w_in": 2, "dw_w": 2, "pw2_w": 1, "sconv_w": 2, "w_proj_a": 2, "w_proj_b": 2, "w_proj_c": 2,
           "w_out": 1}
GATHERED = tuple(n for n in SHARDED if n != "w_ada")
GATHER_F32 = ("dw_w", "sconv_w")
REDUCE_BIG = tuple(n for n in GATHERED if n not in GATHER_F32)
SMALL = ("b_ada", "norm_g", "q_norm_g", "k_norm_g", "sinks", "dw_b", "ln_g", "ln_b", "pw2_b", "a_log", "dt_bias",
         "dn_norm_g")
SMALL_ROWS = 104
SMALL_GRAD_ROWS = 448
W_IN_SHARD = D_IN // N_CHIPS
SUM_PARTS = 4


def _w_in_orig():
    orig = np.full(P_TOTAL, -1, np.int64)
    p = 0
    for s, n in _in_pieces():
        orig[p:p + n] = np.arange(s, s + n)
        p += n
    return orig


def _w_in_blocks(k):
    orig = _w_in_orig().reshape(-1, 128)
    lo, hi = k * W_IN_SHARD, (k + 1) * W_IN_SHARD
    return [b for b in range(orig.shape[0]) if np.any((orig[b] >= lo) & (orig[b] < hi))]


W_IN_BLOCKS = max(len(_w_in_blocks(k)) for k in range(N_CHIPS))


def _runs(idx):
    out, i = [], 0
    while i < len(idx):
        j = i + 1
        while j < len(idx) and ((idx[i] < 0 and idx[j] < 0) or (idx[i] >= 0 and idx[j] == idx[j - 1] + 1)):
            j += 1
        out.append((int(idx[i]) if idx[i] >= 0 else -1, j - i))
        i = j
    return out


def _take(a, idx):
    parts = [jnp.zeros(a.shape[:-1] + (n,), a.dtype) if s < 0 else a[..., s:s + n] for s, n in _runs(idx)]
    return parts[0] if len(parts) == 1 else jnp.concatenate(parts, axis=-1)


def _w_in_send(k, shard):
    orig = _w_in_orig().reshape(-1, 128)
    lo, hi = k * W_IN_SHARD, (k + 1) * W_IN_SHARD
    idx = np.concatenate([np.where((orig[b] >= lo) & (orig[b] < hi), orig[b] - lo, -1) for b in _w_in_blocks(k)])
    idx = np.concatenate([idx, np.full((W_IN_BLOCKS - len(_w_in_blocks(k))) * 128, -1)])
    return _take(shard, idx)


def _w_in_assemble(blocks):
    where = [{b: i for i, b in enumerate(_w_in_blocks(k))} for k in range(N_CHIPS)]
    n_blocks = P_TOTAL // 128
    owners = [[(k, where[k][b]) for k in range(N_CHIPS) if b in where[k]] for b in range(n_blocks)]
    parts, b = [], 0
    while b < n_blocks:
        if len(owners[b]) == 1:
            k, pos = owners[b][0]
            e = b + 1
            while e < n_blocks and owners[e] == [(k, pos + e - b)]:
                e += 1
            parts.append(blocks[k][:, pos * 128:(pos + e - b) * 128])
            b = e
        else:
            parts.append(functools.reduce(jnp.add, [blocks[k][:, pos * 128:(pos + 1) * 128] for k, pos in owners[b]]))
            b += 1
    return jnp.concatenate(parts, axis=1)


def _w_in_grad_blocks(wp):
    out = []
    for k in range(N_CHIPS):
        idx = np.concatenate([np.arange(128 * b, 128 * b + 128) for b in _w_in_blocks(k)])
        idx = np.concatenate([idx, np.full((W_IN_BLOCKS - len(_w_in_blocks(k))) * 128, -1)])
        out.append(_take(wp, idx))
    return jnp.stack(out)


def _w_in_receive_grad(k, blocks):
    orig = _w_in_orig()
    inv = np.zeros(D_IN, np.int64)
    inv[orig[orig >= 0]] = np.nonzero(orig >= 0)[0]
    where = {b: i for i, b in enumerate(_w_in_blocks(k))}
    cols = inv[k * W_IN_SHARD:(k + 1) * W_IN_SHARD]
    return _take(blocks, np.array([where[c // 128] * 128 + c % 128 for c in cols]))


def _join_layer(v, axis):
    if axis == 2:
        return jnp.transpose(v, (1, 0, 2)).reshape(v.shape[1], N_CHIPS * v.shape[2])
    return v.reshape(N_CHIPS * v.shape[1], v.shape[2])


def _split_layer(v, axis):
    a, b = v.shape
    if axis == 2:
        return jnp.transpose(v.reshape(a, N_CHIPS, b // N_CHIPS), (1, 0, 2))
    return v.reshape(N_CHIPS, a // N_CHIPS, b)


def pack_small(vals, names, rows):
    flat = jnp.concatenate([vals[n].astype(F32).reshape(-1) for n in names])
    return jnp.pad(flat, (0, rows * 128 - flat.shape[0])).reshape(rows, 128)


def unpack_small(packed, names, shapes):
    flat = packed.reshape(-1)
    out, off = {}, 0
    for n in names:
        k = int(np.prod(shapes[n]))
        out[n] = flat[off:off + k].reshape(shapes[n])
        off += k
    return out


ANY = pl.BlockSpec(memory_space=pl.ANY)


def _place():
    x, y, c = lax.axis_index("x"), lax.axis_index("y"), lax.axis_index("c")
    chips = [(1 - x, y), (x, 1 - y), (1 - x, 1 - y)]
    return x, y, c, chips


def _remote(src, dst, send_sem, recv_sem, to):
    return pltpu.make_async_remote_copy(src_ref=src, dst_ref=dst, send_sem=send_sem, recv_sem=recv_sem, device_id=to,
                                        device_id_type=MESH)


class Carry:
    def __init__(self, ins, out_shapes, sems, start, finish, in_place=False):
        self.ins, self.out_shapes, self.sems, self.start, self.finish, self.in_place = (
            list(ins), list(out_shapes), sems, start, finish, in_place)

    def scratch(self):
        return [pltpu.SemaphoreType.DMA(self.sems), pltpu.SemaphoreType.DMA(self.sems)]

    def aliases(self, first_in, first_out):
        return {first_in + i: first_out + i for i in range(len(self.ins))} if self.in_place else {}

    def emit_start(self, first, in_refs, out_refs, send_sems, recv_sems):
        @pl.when(first)
        def _():
            self.start(in_refs, out_refs, send_sems, recv_sems)

    def emit_finish(self, last, in_refs, out_refs, send_sems, recv_sems):
        @pl.when(last)
        def _():
            self.finish(in_refs, out_refs, send_sems, recv_sems)


def _host(carry):
    if carry is None:
        return [], [], [], [], []
    return carry.ins, [ANY] * len(carry.ins), carry.out_shapes, [ANY] * len(carry.out_shapes), carry.scratch()


def run_carry(name, carry):
    n_in, n_out = len(carry.ins), len(carry.out_shapes)

    def body(*refs):
        ins, outs, sems = refs[:n_in], refs[n_in:n_in + n_out], refs[n_in + n_out:]
        carry.start(ins, outs, *sems)
        carry.finish(ins, outs, *sems)

    return pl.pallas_call(
        body, name=name, out_shape=carry.out_shapes, in_specs=[ANY] * n_in, out_specs=[ANY] * n_out,
        input_output_aliases=carry.aliases(0, 0), scratch_shapes=carry.scratch(),
    )(*carry.ins)


def carry_allgather(layer, slots):
    n = len(slots)

    def copies(out, send_sems, recv_sems, only_ici_out=False):
        x, y, c, chips = _place()
        ici_out, ici_in, d2d_out, d2d_in = [], [], [], []
        for j, chip in enumerate(chips):
            for t in range(n):
                mine, land = out[t].at[2 * x + y], out[t].at[2 * chip[0] + chip[1]]
                ici_out.append(_remote(mine, mine, send_sems.at[t, j], recv_sems.at[t, j], (*chip, layer)))
                if only_ici_out:
                    continue
                ici_in.append(_remote(land, land, send_sems.at[t, j], recv_sems.at[t, j], (*chip, layer)))
                d2d_out.append(_remote(land, land, send_sems.at[t, 3 + j], recv_sems.at[t, 3 + j], (x, y, 1 - layer)))
                d2d_in.append(_remote(land, land, send_sems.at[t, 3 + j], recv_sems.at[t, 3 + j], (x, y, layer)))
        return c, ici_out, ici_in, d2d_out, d2d_in

    def start(ins, out, send_sems, recv_sems):
        c, ici_out, _, _, _ = copies(out, send_sems, recv_sems, only_ici_out=True)

        @pl.when(c == layer)
        def _():
            for cp in ici_out:
                cp.start()

    def finish(ins, out, send_sems, recv_sems):
        c, ici_out, ici_in, d2d_out, d2d_in = copies(out, send_sems, recv_sems)

        @pl.when(c == layer)
        def _():
            for arrived, onward in zip(ici_in, d2d_out):
                arrived.wait_recv()
                onward.start()
            for cp in ici_out + d2d_out:
                cp.wait_send()

        @pl.when(c != layer)
        def _():
            for cp in d2d_in:
                cp.wait_recv()

    return Carry(slots, [jax.ShapeDtypeStruct(s.shape, s.dtype) for s in slots], (n, 6), start, finish, in_place=True)


def carry_pair_send(layer, gs):
    def copies(g, recv, send_sems, recv_sems):
        x, y, c, _ = _place()
        return c, [_remote(g[t], recv[t], send_sems.at[t], recv_sems.at[t], (x, y, 1 - c)) for t in range(len(gs))]

    def start(g, recv, send_sems, recv_sems):
        c, cps = copies(g, recv, send_sems, recv_sems)

        @pl.when(c != layer)
        def _():
            for cp in cps:
                cp.start()

    def finish(g, recv, send_sems, recv_sems):
        c, cps = copies(g, recv, send_sems, recv_sems)

        @pl.when(c != layer)
        def _():
            for cp in cps:
                cp.wait_send()

        @pl.when(c == layer)
        def _():
            for cp in cps:
                cp.wait_recv()

    return Carry(gs, [jax.ShapeDtypeStruct(g.shape, g.dtype) for g in gs], (len(gs),), start, finish)


def grads_pair_sums(layer, gs, recv):
    n = len(gs)

    def body(*refs):
        for t in range(n):
            refs[2 * n + t][...] = (refs[t][...] + refs[n + t][...]).astype(BF16)

    specs = [pl.BlockSpec((None, g.shape[1] // SUM_PARTS, g.shape[2]), lambda s, i: (s, i, 0)) for g in gs]
    return pl.pallas_call(
        body, name=f"grads_pair_sums{layer}", grid=(N_CHIPS, SUM_PARTS), in_specs=specs + specs, out_specs=specs,
        out_shape=[jax.ShapeDtypeStruct(g.shape, BF16) for g in gs], compiler_params=_cparams(2),
    )(*gs, *recv)


def carry_chip_exchange(layer, ps):
    def copies(p, recv, send_sems, recv_sems):
        _, _, c, chips = _place()
        return c, [_remote(p[t].at[2 * chip[0] + chip[1]], recv[t].at[j], send_sems.at[t, j], recv_sems.at[t, j],
                           (*chip, layer)) for j, chip in enumerate(chips) for t in range(len(ps))]

    def start(p, recv, send_sems, recv_sems):
        c, cps = copies(p, recv, send_sems, recv_sems)

        @pl.when(c == layer)
        def _():
            for cp in cps:
                cp.start()

    def finish(p, recv, send_sems, recv_sems):
        c, cps = copies(p, recv, send_sems, recv_sems)

        @pl.when(c == layer)
        def _():
            for cp in cps:
                cp.wait()

    return Carry(ps, [jax.ShapeDtypeStruct((3,) + p.shape[1:], p.dtype) for p in ps], (len(ps), 3), start, finish)


def grads_chip_sums(layer, gs, recv, recv2, into=None):
    n = len(gs)
    my_slot = lambda: 2 * lax.axis_index("x") + lax.axis_index("y")

    def body(*refs):
        outs = refs[-n:]
        for t in range(n):
            r2 = refs[2 * n + t]
            own = refs[t][...] + refs[n + t][...]
            outs[t][...] = ((own + r2[0].astype(F32)) + r2[1].astype(F32)) + r2[2].astype(F32)

    part = lambda g: g.shape[1] // SUM_PARTS
    own_specs = [pl.BlockSpec((None, part(g), g.shape[2]), lambda i: (my_slot(), i, 0)) for g in gs]
    return pl.pallas_call(
        body, name=f"grads_chip_sums{layer}", grid=(SUM_PARTS,),
        in_specs=own_specs + own_specs + [pl.BlockSpec((3, part(g), g.shape[2]), lambda i: (0, i, 0)) for g in gs]
        + ([] if into is None else [ANY] * n),
        out_specs=[pl.BlockSpec((None, part(g), g.shape[2]), lambda i: (layer, i, 0)) for g in gs],
        out_shape=[jax.ShapeDtypeStruct((DEPTH,) + g.shape[1:], F32) for g in gs],
        input_output_aliases={} if into is None else {3 * n + t: t for t in range(n)},
        compiler_params=_cparams(1),
    )(*gs, *recv, *recv2, *([] if into is None else into))


def grads_pair_gather(reds):
    n = len(reds)

    def body(*refs):
        buf = refs[n:2 * n]
        send_sems, recv_sems = refs[2 * n:]
        x, y, c, _ = _place()
        sibling = (x, y, 1 - c)
        cps = [_remote(buf[t].at[c], buf[t].at[c], send_sems.at[t], recv_sems.at[t], sibling) for t in range(n)]
        for cp in cps:
            cp.start()
        for t in range(n):
            _remote(buf[t].at[c], buf[t].at[1 - c], send_sems.at[t], recv_sems.at[t], sibling).wait_recv()
        for cp in cps:
            cp.wait_send()

    return pl.pallas_call(
        body, name="grads_pair_gather", out_shape=[jax.ShapeDtypeStruct(r.shape, r.dtype) for r in reds],
        in_specs=[ANY] * n, out_specs=[ANY] * n, input_output_aliases={t: t for t in range(n)},
        scratch_shapes=[pltpu.SemaphoreType.DMA((n,)), pltpu.SemaphoreType.DMA((n,))],
    )(*reds)


def small_allreduce(v):
    m, n = v.shape

    def body(x_ref, sum_ref, all_ref, send_sems, recv_sems, local_sem):
        x, y, c, chips = _place()
        me, sibling = (x, y, c), (x, y, 1 - c)

        def rows(px, py, pc):
            return all_ref.at[pl.ds((4 * px + 2 * py + pc) * m, m), :]

        def copy(k, block, to, src=None):
            return pltpu.make_async_remote_copy(src_ref=rows(*block) if src is None else src, dst_ref=rows(*block),
                                                send_sem=send_sems.at[k], recv_sem=recv_sems.at[k],
                                                device_id=to, device_id_type=MESH)

        mine = pltpu.make_async_copy(x_ref, rows(*me), local_sem)
        mine.start()
        first = [copy(0, me, sibling, src=x_ref)]
        first += [copy(1 + j, me, (*chip, c), src=x_ref) for j, chip in enumerate(chips)]
        for cp in first:
            cp.start()
        passed = [copy(4 + j, (*chip, c), sibling) for j, chip in enumerate(chips)]
        for j, chip in enumerate(chips):
            copy(1 + j, (*chip, c), me).wait_recv()
            passed[j].start()
        copy(0, sibling, me).wait_recv()
        for j, chip in enumerate(chips):
            copy(4 + j, (*chip, 1 - c), me).wait_recv()
        for cp in first + passed:
            cp.wait_send()
        mine.wait()
        acc = all_ref[0:m, :]
        for dev in range(1, 8):
            acc = acc + all_ref[dev * m:(dev + 1) * m, :]
        sum_ref[...] = acc

    vm = pl.BlockSpec(memory_space=pltpu.VMEM)
    return pl.pallas_call(
        body, name="small_allreduce",
        out_shape=[jax.ShapeDtypeStruct((m, n), F32), jax.ShapeDtypeStruct((8 * m, n), F32)],
        in_specs=[vm], out_specs=[vm, vm],
        scratch_shapes=[pltpu.SemaphoreType.DMA((7,)), pltpu.SemaphoreType.DMA((7,)), pltpu.SemaphoreType.DMA],
    )(v)


def grads_by_chip(layer_grads):
    return [layer_grads[n] if n == "w_in" else _split_layer(layer_grads[n], SHARDED[n]) for n in REDUCE_BIG]


def adamw(name, w, g, m, v, block):
    grid = tuple(s // b for s, b in zip(w.shape, block))

    def body(w_ref, g_ref, m_ref, v_ref, d_ref, nm_ref, nv_ref):
        gv = g_ref[...]
        nm = ADAM_B1 * m_ref[...] + (1.0 - ADAM_B1) * gv
        nv = ADAM_B2 * v_ref[...] + (1.0 - ADAM_B2) * (gv * gv)
        m_hat = nm / (1.0 - ADAM_B1 ** ADAM_STEP)
        v_hat = nv / (1.0 - ADAM_B2 ** ADAM_STEP)
        d_ref[...] = -ADAM_LR * (m_hat / (jnp.sqrt(v_hat) + ADAM_EPS) + ADAM_WD * w_ref[...])
        nm_ref[...] = nm
        nv_ref[...] = nv

    spec = pl.BlockSpec(tuple(block), lambda *idx: idx)
    return pl.pallas_call(
        body, name=name, grid=grid, in_specs=[spec] * 4, out_specs=[spec] * 3,
        out_shape=[jax.ShapeDtypeStruct(w.shape, F32)] * 3, compiler_params=_cparams(len(grid)),
    )(w, g, m, v)


ADAM_ROWS = {"w_ada": 512, "dw_w": 62, "pw2_w": 256, "sconv_w": 8, "w_proj_a": 512, "w_proj_b": 512, "w_proj_c": 512,
             "w_out": 256}
ADAM_W_IN_COLS = 331

WEIGHT_NAMES = ("w_ada", "b_ada", "norm_g", "w_in", "q_norm_g", "k_norm_g", "sinks", "dw_w", "dw_b", "ln_g", "ln_b",
                "pw2_w", "pw2_b", "sconv_w", "a_log", "dt_bias", "dn_norm_g", "w_proj_a", "w_proj_b", "w_proj_c", "w_out")


def kernel(x, c, w_ada, b_ada, norm_g, w_in, q_norm_g, k_norm_g, sinks, dw_w, dw_b, ln_g, ln_b, pw2_w, pw2_b, sconv_w, a_log, dt_bias, dn_norm_g, w_proj_a, w_proj_b, w_proj_c, w_out, loss_target, m_w_ada, m_b_ada, m_norm_g, m_w_in, m_q_norm_g, m_k_norm_g, m_sinks, m_dw_w, m_dw_b, m_ln_g, m_ln_b, m_pw2_w, m_pw2_b, m_sconv_w, m_a_log, m_dt_bias, m_dn_norm_g, m_w_proj_a, m_w_proj_b, m_w_proj_c, m_w_out, v_w_ada, v_b_ada, v_norm_g, v_w_in, v_q_norm_g, v_k_norm_g, v_sinks, v_dw_w, v_dw_b, v_ln_g, v_ln_b, v_pw2_w, v_pw2_b, v_sconv_w, v_a_log, v_dt_bias, v_dn_norm_g, v_w_proj_a, v_w_proj_b, v_w_proj_c, v_w_out):
    args = dict(locals())
    w = {n: args[n] for n in WEIGHT_NAMES}
    mom = {n: args["m_" + n] for n in WEIGHT_NAMES}
    var = {n: args["v_" + n] for n in WEIGHT_NAMES}

    chip = 2 * lax.axis_index("x") + lax.axis_index("y")
    own = {n: w[n] if n in GATHER_F32 else w[n].astype(BF16) for n in GATHERED}
    own["w_in"] = lax.switch(chip, [functools.partial(_w_in_send, k) for k in range(N_CHIPS)], own["w_in"])
    slots = [[lax.dynamic_update_slice(lax.empty((N_CHIPS,) + own[n].shape[1:], own[n].dtype), own[n][l][None], (chip, 0, 0))
              for n in GATHERED] for l in range(DEPTH)]

    def layer_operands(l, gathered):
        lw = {n: w[n][l] for n in SMALL}
        lw.update({n: g if n == "w_in" else _join_layer(g, SHARDED[n]) for n, g in zip(GATHERED, gathered)})
        return prep_layer(lw)

    layers = [layer_operands(0, run_carry("weights_allgather0", carry_allgather(0, slots[0]))), None]

    mod, conds = ada_fwd(jnp.tile(c, (8, 1)), w["w_ada"], w["b_ada"])
    saved = [None] * DEPTH
    big = GATHERED.index("w_in")
    rest = [i for i in range(len(GATHERED)) if i != big]
    act, saved[0], got_big, got_rest = layer_fwd(
        "0", x[0], mod[0:1], layers[0], carry_inproj=carry_allgather(1, [slots[1][big]]),
        carry_merge=carry_allgather(1, [slots[1][i] for i in rest]))
    gathered1 = dict(zip(rest, got_rest))
    gathered1[big] = got_big[0]
    layers[1] = layer_operands(1, [gathered1[i] for i in range(len(GATHERED))])
    act, saved[1], _, _ = layer_fwd("1", act, mod[1:2], layers[1])
    dact, loss_part = loss_head("loss_head", act, loss_target[0], TM)
    loss = lax.psum(loss_part[0, 0], ("x", "y", "c"))
    layer_grads = [None] * DEPTH
    dact, layer_grads[1], *_ = layer_bwd("1", dact, layers[1], saved[1])
    gs1 = grads_by_chip(layer_grads[1])
    gs0 = []

    def hand_over_layer0(reduced):
        gs0.extend(grads_by_chip(reduced))
        return carry_pair_send(0, gs0)

    dact, layer_grads[0], recv1, got1, recv0, got0 = layer_bwd(
        "0", dact, layers[0], saved[0], carry_merge=carry_pair_send(1, gs1),
        carry_delta=lambda recv: carry_chip_exchange(1, grads_pair_sums(1, gs1, recv)), carry_dh=hand_over_layer0,
        carry_norm=lambda recv: carry_chip_exchange(0, grads_pair_sums(0, gs0, recv)))

    reds = grads_chip_sums(0, gs0, recv0, got0, into=grads_chip_sums(1, gs1, recv1, got1))
    final_grads = dict(zip(REDUCE_BIG, grads_pair_gather(reds)))
    final_grads["w_in"] = lax.switch(chip, [functools.partial(_w_in_receive_grad, k) for k in range(N_CHIPS)],
                                     final_grads["w_in"])
    small_names = SMALL + GATHER_F32
    small_shapes = {n: (DEPTH,) + layer_grads[0][n].shape for n in small_names}
    small_full = {n: jnp.stack([layer_grads[l][n] for l in range(DEPTH)]) for n in small_names}
    small_sum, small_all = small_allreduce(pack_small(small_full, small_names, SMALL_GRAD_ROWS))
    small_sum = unpack_small(small_sum, small_names, small_shapes)
    for n in GATHER_F32:
        width = w[n].shape[2]
        final_grads[n] = lax.dynamic_slice_in_dim(small_sum[n], chip * width, width, axis=2)
    n_mod = DEPTH * 3 * D_MODEL
    dmod = small_all.reshape(8, -1)[:, :n_mod].reshape(8, DEPTH, 3 * D_MODEL)
    width = w["w_ada"].shape[2]
    dmod = jnp.transpose(lax.dynamic_slice_in_dim(dmod, chip * width, width, axis=2), (1, 0, 2))
    final_grads["w_ada"] = ada_bwd(conds, dmod)
    final_grads.update({n: small_sum[n] for n in SMALL})
    small_grads = pack_small(final_grads, SMALL, SMALL_ROWS)

    delta, new_m, new_v = {}, {}, {}
    for n in SHARDED:
        shp = w[n].shape
        if n == "w_in":
            view = lambda a: jnp.transpose(a, (2, 0, 1))
            back = lambda a: jnp.transpose(a, (1, 2, 0))
            g3 = view(final_grads[n])
            final_grads[n] = back(g3)
            d, nm, nv = adamw("adamw_" + n, view(w[n]), g3, view(mom[n]), view(var[n]), (ADAM_W_IN_COLS, shp[0], shp[1]))
        else:
            view = lambda a, shp=shp: a.reshape(shp[0] * shp[1], shp[2])
            back = lambda a, shp=shp: a.reshape(shp)
            d, nm, nv = adamw("adamw_" + n, view(w[n]), view(final_grads[n]), view(mom[n]), view(var[n]),
                              (ADAM_ROWS[n], shp[2]))
        delta[n], new_m[n], new_v[n] = back(d), back(nm), back(nv)
    d, nm, nv = adamw("adamw_small", pack_small(w, SMALL, SMALL_ROWS), small_grads, pack_small(mom, SMALL, SMALL_ROWS),
                      pack_small(var, SMALL, SMALL_ROWS), (SMALL_ROWS, 128))
    delta.update(unpack_small(d, SMALL, small_shapes))
    new_m.update(unpack_small(nm, SMALL, small_shapes))
    new_v.update(unpack_small(nv, SMALL, small_shapes))

    return (loss, dact[None], *[final_grads[n] for n in WEIGHT_NAMES], *[delta[n] for n in WEIGHT_NAMES],
            *[new_m[n] for n in WEIGHT_NAMES], *[new_v[n] for n in WEIGHT_NAMES])
```

```python
import functools

import numpy as np
import jax
import jax.numpy as jnp
from jax import lax
from jax.experimental import pallas as pl
from jax.experimental.pallas import tpu as pltpu

F32 = jnp.float32
BF16 = jnp.bfloat16
MESH = pl.DeviceIdType.MESH

D_MODEL = 1024
DEPTH = 2
ATT_HEADS = 8
ATT_HEAD_DIM = 64
WINDOW = 128
CONV_K = 31
DN_HEADS = 4
DN_CONV_K = 4
DN_CHUNK = 64
EPS = 1e-6
NEG_INF = -1e30
N_CHIPS = 4
D_IN = 7944

ADAM_LR = 0.001
ADAM_B1 = 0.9
ADAM_B2 = 0.999
ADAM_EPS = 1e-08
ADAM_WD = 0.01
ADAM_STEP = 10

VMEM_LIMIT = 56 * 1024 * 1024

P_QA, P_ZA, P_GLU, P_ZB, P_ZC, P_MG, P_QKV, P_KA, P_VA, P_AB, P_TOTAL = (
    0, 512, 1024, 2048, 2560, 3072, 6144, 7680, 7808, 7936, 8064)
HEAD_ORDER = (0, 4, 1, 5, 2, 6, 3, 7)


def _in_pieces():
    p = [(0 + 64 * h, 64) for h in HEAD_ORDER]
    p += [(768 + 64 * h, 64) for h in HEAD_ORDER]
    for g in range(4):
        p += [(1280 + 128 * g, 128), (1792 + 128 * g, 128)]
    p += [(2304, 512), (4360, 512), (4872, 3072), (2816, 1536), (512, 128), (640, 128), (4352, 8)]
    return p


def _perm_heads_rows(w):
    return jnp.concatenate([w[64 * h:64 * h + 64] for h in HEAD_ORDER], axis=0)


def _unperm_heads_rows(w):
    inv = [HEAD_ORDER.index(h) for h in range(8)]
    return jnp.concatenate([w[64 * s:64 * s + 64] for s in inv], axis=0)


def _split_bf16(a, terms):
    out, rest = [], a.astype(F32)
    for _ in range(terms - 1):
        out.append(rest.astype(BF16))
        rest = rest - out[-1].astype(F32)
    return out + [rest.astype(BF16)]


def _dot(a, b, dims, exact):
    d = lambda p, q: lax.dot_general(p, q, (dims, ((), ())), preferred_element_type=F32)
    if exact:
        (ah, al), (bh, bl) = _split_bf16(a, 2), _split_bf16(b, 2)
        return d(ah, bh) + (d(ah, bl) + d(al, bh))
    return d(a.astype(BF16), b.astype(BF16))


def _make_mm(exact):
    @jax.custom_vjp
    def nn(a, b):
        return _dot(a, b, ((1,), (0,)), exact)

    @jax.custom_vjp
    def nt(a, b):
        return _dot(a, b, ((1,), (1,)), exact)

    @jax.custom_vjp
    def tn(a, b):
        return _dot(a, b, ((0,), (0,)), exact)

    nn.defvjp(lambda a, b: (nn(a, b), (a, b)),
              lambda r, g: (nt(g, r[1]).astype(r[0].dtype), tn(r[0], g).astype(r[1].dtype)))
    nt.defvjp(lambda a, b: (nt(a, b), (a, b)),
              lambda r, g: (nn(g, r[1]).astype(r[0].dtype), tn(g, r[0]).astype(r[1].dtype)))
    tn.defvjp(lambda a, b: (tn(a, b), (a, b)),
              lambda r, g: (nt(r[1], g).astype(r[0].dtype), nn(r[0], g).astype(r[1].dtype)))
    return nn, nt, tn


mm, mm_nt, mm_tn = _make_mm(False)
xmm, xmm_nt, xmm_tn = _make_mm(True)


@jax.custom_vjp
def sel_mm(m, g):
    mb = m.astype(BF16)
    parts = [jnp.dot(mb, p, preferred_element_type=F32) for p in _split_bf16(g, 3)]
    return parts[0] + (parts[1] + parts[2])


def _sel_mm_bwd(m, dy):
    mb = m.astype(BF16)
    parts = [lax.dot_general(mb, p, (((0,), (0,)), ((), ())), preferred_element_type=F32) for p in _split_bf16(dy, 3)]
    return jnp.zeros_like(m), parts[0] + (parts[1] + parts[2])


sel_mm.defvjp(lambda m, g: (sel_mm(m, g), m), _sel_mm_bwd)


@jax.custom_vjp
def tri_inv(*mats):
    n = mats[0].shape[0]
    eye = jnp.where(lax.broadcasted_iota(jnp.int32, (n, n), 0) == lax.broadcasted_iota(jnp.int32, (n, n), 1), 1.0, 0.0)
    ts = [eye - a for a in mats]
    pws = list(mats)
    for _ in range(5):
        pws = [xmm(pw, pw) for pw in pws]
        ts = [t + xmm(t, pw) for t, pw in zip(ts, pws)]
    return tuple(ts)


def _tri_inv_bwd(ts, dts):
    inner = [xmm_nt(dt, t) for t, dt in zip(ts, dts)]
    return tuple(-xmm_tn(t, m) for t, m in zip(ts, inner))


tri_inv.defvjp(lambda *mats: (tri_inv(*mats),) * 2, _tri_inv_bwd)


@jax.custom_vjp
def tri_inv_known(a, t):
    return t


tri_inv_known.defvjp(lambda a, t: (t, t), lambda t, dt: (_tri_inv_bwd((t,), (dt,))[0], jnp.zeros_like(t)))


def _sigmoid(x):
    return 1.0 / (1.0 + jnp.exp(-x))


def _silu(x):
    return x * _sigmoid(x)


def _softplus(x):
    return jnp.maximum(x, 0.0) + jnp.log(1.0 + jnp.exp(-jnp.abs(x)))


def _cparams(n_grid):
    return pltpu.CompilerParams(dimension_semantics=("arbitrary",) * n_grid, vmem_limit_bytes=VMEM_LIMIT)


def _row_spec(tm, width, colblk):
    return pl.BlockSpec((tm, width), lambda i, cb=colblk: (i, cb))


def _const_spec(shape):
    nd = len(shape)
    return pl.BlockSpec(tuple(shape), lambda i, nd=nd: (0,) * nd)


def rowwise_fwd(name, f, rows, consts, outs, tm, carry=None):
    n_r, n_c = len(rows), len(consts)
    t = rows[0][0].shape[0]
    c_ins, c_in_specs, c_outs, c_out_specs, c_scratch = _host(carry)
    n_in, n_ci, n_co = n_r + n_c, len(c_ins), len(c_outs)

    def body(*refs):
        carried = (refs[n_in:n_in + n_ci], refs[n_in + n_ci + len(outs):n_in + n_ci + len(outs) + n_co],
                   *refs[n_in + n_ci + len(outs) + n_co:])
        if carry is not None:
            carry.emit_start(pl.program_id(0) == 0, *carried)
        vals = [r[...] for r in refs[:n_in]]
        res = f(*vals)
        if not isinstance(res, (tuple, list)):
            res = (res,)
        for o_ref, v, out in zip(refs[n_in + n_ci:n_in + n_ci + len(outs)], res, outs):
            o_ref[...] = (v.T if len(out) == 3 else v).astype(o_ref.dtype)
        if carry is not None:
            carry.emit_finish(pl.program_id(0) == t // tm - 1, *carried)

    return pl.pallas_call(
        body, name=name, grid=(t // tm,),
        in_specs=[_row_spec(tm, w, cb) for _, w, cb in rows] + [_const_spec(c.shape) for c in consts] + c_in_specs,
        out_specs=[_row_spec(tm, o[0], 0) if len(o) == 2 else pl.BlockSpec((o[0], tm), lambda i: (0, i)) for o in outs]
        + c_out_specs,
        out_shape=[jax.ShapeDtypeStruct((t, o[0]) if len(o) == 2 else (o[0], t), o[1]) for o in outs] + c_outs,
        input_output_aliases={} if carry is None else carry.aliases(n_in, len(outs)),
        scratch_shapes=c_scratch,
        compiler_params=_cparams(1),
    )(*[a for a, _, _ in rows], *consts, *c_ins)


def rowwise_bwd(name, f, rows, consts, cts, row_grad_dtypes, tm, carry=None):
    n_r, n_c, n_ct = len(rows), len(consts), len(cts)
    t = rows[0][0].shape[0]
    keep = [k for k, dt in enumerate(row_grad_dtypes) if dt is not None]
    c_ins, c_in_specs, c_outs, c_out_specs, c_scratch = _host(carry)
    n_in, n_out = n_r + n_c + n_ct, len(keep) + n_c

    def body(*refs):
        ins = [r[...].astype(F32) for r in refs[:n_r + n_c]]
        g_out = [r[...].astype(F32) for r in refs[n_r + n_c:n_in]]
        out_refs = refs[n_in + len(c_ins):n_in + len(c_ins) + n_out]
        carried = (refs[n_in:n_in + len(c_ins)], refs[n_in + len(c_ins) + n_out:n_in + len(c_ins) + n_out + len(c_outs)],
                   *refs[n_in + len(c_ins) + n_out + len(c_outs):])
        if carry is not None:
            carry.emit_start(pl.program_id(0) == 0, *carried)

        def fw(*a):
            res = f(*a)
            return tuple(res) if isinstance(res, (tuple, list)) else (res,)

        _, vjp = jax.vjp(fw, *ins)
        grads = vjp(tuple(g_out))
        for o_ref, k in zip(out_refs[:len(keep)], keep):
            o_ref[...] = grads[k].astype(o_ref.dtype)
        first = pl.program_id(0) == 0
        for o_ref, g in zip(out_refs[len(keep):], grads[n_r:]):
            @pl.when(first)
            def _(o_ref=o_ref, g=g):
                o_ref[...] = g

            @pl.when(jnp.logical_not(first))
            def _(o_ref=o_ref, g=g):
                o_ref[...] += g
        if carry is not None:
            carry.emit_finish(pl.program_id(0) == t // tm - 1, *carried)

    return pl.pallas_call(
        body, name=name, grid=(t // tm,),
        in_specs=[_row_spec(tm, w, cb) for _, w, cb in rows] + [_const_spec(c.shape) for c in consts]
        + [_row_spec(tm, w, cb) for _, w, cb in cts] + c_in_specs,
        out_specs=[_row_spec(tm, rows[k][1], 0) for k in keep] + [_const_spec(c.shape) for c in consts] + c_out_specs,
        out_shape=[jax.ShapeDtypeStruct((t, rows[k][1]), row_grad_dtypes[k]) for k in keep]
        + [jax.ShapeDtypeStruct(c.shape, F32) for c in consts] + c_outs,
        scratch_shapes=c_scratch,
        compiler_params=_cparams(1),
    )(*[a for a, _, _ in rows], *consts, *[a for a, _, _ in cts], *c_ins)


def f_norm_mod(x, g, scale, shift):
    y = x * lax.rsqrt(jnp.mean(x * x, axis=-1, keepdims=True) + EPS) * g
    return y * (1.0 + scale) + shift


def f_conf_tail(u, zb, ln_g, ln_b, pw2_w, pw2_b):
    mu = jnp.mean(u, axis=-1, keepdims=True)
    xc = u - mu
    var = jnp.mean(xc * xc, axis=-1, keepdims=True)
    y = _silu(xc * lax.rsqrt(var + EPS) * ln_g + ln_b)
    return (mm(y, pw2_w) + pw2_b) * _silu(zb)


def f_merge(ya, yb, yc, mg, x, gate, wpa, wpb, wpc, wout):
    d = D_MODEL
    merged = (_sigmoid(mg[:, :d]) * mm(ya, wpa) + _sigmoid(mg[:, d:2 * d]) * mm(yb, wpb)
              + _sigmoid(mg[:, 2 * d:]) * mm(yc, wpc))
    return x + gate * mm(merged, wout)


def matmul_nn(name, a, b, out_dtype, tm, tn, tk, b_transposed=False, carry=None):
    m, k = a.shape
    n = b.shape[0] if b_transposed else b.shape[1]
    nk = k // tk
    grid = (m // tm, n // tn, nk)
    b_spec = (pl.BlockSpec((tn, tk), lambda i, j, kk: (j, kk)) if b_transposed
              else pl.BlockSpec((tk, tn), lambda i, j, kk: (kk, j)))
    c_ins, c_in_specs, c_outs, c_out_specs, c_scratch = _host(carry)
    n_ci, n_co = len(c_ins), len(c_outs)

    def body(*refs):
        a_ref, b_ref, o_ref = refs[0], refs[1], refs[2 + n_ci]
        carried = (refs[2:2 + n_ci], refs[3 + n_ci:3 + n_ci + n_co], *refs[3 + n_ci + n_co:3 + n_ci + n_co + len(c_scratch)])
        at = lambda step: functools.reduce(jnp.logical_and, [pl.program_id(d) == s for d, s in enumerate(step)])
        if carry is not None:
            carry.emit_start(at((0, 0, 0)), *carried)
        part = lax.dot_general(a_ref[...].astype(BF16), b_ref[...].astype(BF16),
                               (((1,), (1 if b_transposed else 0,)), ((), ())), preferred_element_type=F32)
        if nk == 1:
            o_ref[...] = part.astype(o_ref.dtype)
        else:
            kk = pl.program_id(2)
            acc_ref = refs[-1]

            @pl.when(kk == 0)
            def _():
                acc_ref[...] = part

            @pl.when(kk > 0)
            def _():
                acc_ref[...] += part

            @pl.when(kk == nk - 1)
            def _():
                o_ref[...] = acc_ref[...].astype(o_ref.dtype)
        if carry is not None:
            carry.emit_finish(at(tuple(g - 1 for g in grid)), *carried)

    res = pl.pallas_call(
        body, name=name, grid=grid,
        in_specs=[pl.BlockSpec((tm, tk), lambda i, j, kk: (i, kk)), b_spec] + c_in_specs,
        out_specs=[pl.BlockSpec((tm, tn), lambda i, j, kk: (i, j))] + c_out_specs,
        out_shape=[jax.ShapeDtypeStruct((m, n), out_dtype)] + c_outs,
        input_output_aliases={} if carry is None else carry.aliases(2, 1),
        scratch_shapes=c_scratch + ([] if nk == 1 else [pltpu.VMEM((tm, tn), F32)]),
        compiler_params=_cparams(3),
    )(a, b, *c_ins)
    return res[0] if carry is None else res


def ada_fwd(c8, w_shard, b_ada):
    n_cols = w_shard.shape[2]
    masks = [(m >> 2 & 1, m >> 1 & 1, m & 1) for m in range(1, 8)]

    def body(c_ref, w_ref, b_ref, mod_ref, conds_ref, cbuf, sendbuf, recvbuf, send_sems, recv_sems):
        x, y, c, chips = _place()
        flip = lambda v, bit: 1 - v if bit else v
        peers = [(flip(x, mx), flip(y, my), flip(c, mc)) for mx, my, mc in masks]
        dev = lambda p: 4 * p[0] + 2 * p[1] + p[2]
        cbuf[dev((x, y, c))] = c_ref[...]
        first = [_remote(c_ref, cbuf.at[dev((x, y, c))], send_sems.at[i], recv_sems.at[i], p) for i, p in enumerate(peers)]
        for cp in first:
            cp.start()
        for i, p in enumerate(peers):
            _remote(c_ref, cbuf.at[dev(p)], send_sems.at[i], recv_sems.at[i], p).wait_recv()
        conds = jnp.concatenate([cbuf[d, 0:1, :] for d in range(8)], axis=0)
        conds_ref[...] = conds
        act = _silu(conds)
        parts = [mm(act, w_ref[l]) for l in range(DEPTH)]
        row8 = lax.broadcasted_iota(jnp.int32, (8, 1), 0)

        def tile_for(chip):
            r = 2 * (2 * chip[0] + chip[1]) + c
            rows = [jnp.sum(jnp.where(row8 == r, parts[l], 0.0), axis=0, keepdims=True) for l in range(DEPTH)]
            return jnp.where(row8 == 0, rows[0], jnp.where(row8 == 1, rows[1], 0.0))

        my_slot = 2 * x + y
        recvbuf[my_slot] = tile_for((x, y))
        second = []
        for j, chip in enumerate(chips):
            sendbuf[j] = tile_for(chip)
            second.append(_remote(sendbuf.at[j], recvbuf.at[my_slot], send_sems.at[7 + j], recv_sems.at[7 + j], (*chip, c)))
            second[-1].start()
        for j, chip in enumerate(chips):
            _remote(sendbuf.at[j], recvbuf.at[2 * chip[0] + chip[1]], send_sems.at[7 + j], recv_sems.at[7 + j],
                    (*chip, c)).wait_recv()
        rows = [jnp.concatenate([recvbuf[k, l:l + 1, :] for k in range(N_CHIPS)], axis=1) + b_ref[l:l + 1, :]
                for l in range(DEPTH)]
        mod_ref[...] = jnp.concatenate(rows + [jnp.zeros((8 - DEPTH, N_CHIPS * n_cols), F32)], axis=0)
        for cp in first + second:
            cp.wait_send()

    vm = pl.BlockSpec(memory_space=pltpu.VMEM)
    return pl.pallas_call(
        body, name="ada_fwd",
        out_shape=[jax.ShapeDtypeStruct((8, N_CHIPS * n_cols), F32), jax.ShapeDtypeStruct((8, D_MODEL), F32)],
        in_specs=[vm, vm, vm], out_specs=[vm, vm],
        scratch_shapes=[pltpu.VMEM((8, 8, D_MODEL), F32), pltpu.VMEM((3, 8, n_cols), F32),
                        pltpu.VMEM((N_CHIPS, 8, n_cols), F32), pltpu.SemaphoreType.DMA((10,)), pltpu.SemaphoreType.DMA((10,))],
        compiler_params=pltpu.CompilerParams(vmem_limit_bytes=VMEM_LIMIT),
    )(c8, w_shard, b_ada)


def ada_bwd(conds, dmod):
    def body(c_ref, d_ref, o_ref):
        act = _silu(c_ref[...])
        for l in range(DEPTH):
            o_ref[l] = mm_tn(act, d_ref[l])

    return pl.pallas_call(
        body, name="ada_bwd", out_shape=jax.ShapeDtypeStruct((DEPTH, D_MODEL, dmod.shape[2]), F32),
        compiler_params=pltpu.CompilerParams(vmem_limit_bytes=VMEM_LIMIT),
    )(conds, dmod)


def _f_attn(first_block, q, za, kc, vc, kp, vp, qg, kg, sinks):
    w = WINDOW
    lane = lax.broadcasted_iota(jnp.int32, (1, 128), 1)
    halves = [lane < 64, lane >= 64]

    def rms_halves(x, g):
        x2 = x * x
        s0 = jnp.sum(jnp.where(halves[0], x2, 0.0), axis=-1, keepdims=True)
        s1 = jnp.sum(jnp.where(halves[1], x2, 0.0), axis=-1, keepdims=True)
        r = jnp.where(halves[0], lax.rsqrt(s0 / 64.0 + EPS), lax.rsqrt(s1 / 64.0 + EPS))
        return x * r * g

    kcat = rms_halves(jnp.concatenate([kp, kc], axis=0), kg)
    vcat = jnp.concatenate([vp, vc], axis=0)
    qi = lax.broadcasted_iota(jnp.int32, (w, 2 * w), 0)
    kj = lax.broadcasted_iota(jnp.int32, (w, 2 * w), 1)
    dist = qi + w - kj
    valid = (dist >= 0) & (dist < w) & (jnp.logical_not(first_block) | (kj >= w))
    distf = dist.astype(F32)
    units = [(grp, half) for grp in range(4) for half in range(2)]
    qns = [rms_halves(q[:, 128 * grp:128 * grp + 128], qg) * (ATT_HEAD_DIM ** -0.5) for grp in range(4)]
    vhalf = [jnp.where(halves[half], vcat, 0.0) for half in range(2)]
    scores, sinks_h = [], []
    for grp, half in units:
        head = HEAD_ORDER[2 * grp + half]
        slope = 2.0 ** (-8.0 * (head + 1) / ATT_HEADS)
        sinks_h.append(jnp.sum(jnp.where(lane == head, sinks, 0.0), axis=-1, keepdims=True))
        s = mm_nt(jnp.where(halves[half], qns[grp], 0.0), kcat) - slope * distf
        scores.append(jnp.where(valid, s, NEG_INF))
    probs = []
    for s, sink in zip(scores, sinks_h):
        m = lax.stop_gradient(jnp.maximum(jnp.max(s, axis=-1, keepdims=True), sink))
        p = jnp.exp(s - m)
        probs.append(p / (jnp.sum(p, axis=-1, keepdims=True) + jnp.exp(sink - m)))
    outs = [mm(p, vhalf[half]) for p, (grp, half) in zip(probs, units)]
    return jnp.concatenate([outs[2 * grp] + outs[2 * grp + 1] for grp in range(4)], axis=1) * _silu(za)


def attn_fwd(name, proj, qg, kg, sinks):
    t = proj.shape[0]
    nb = t // WINDOW

    def body(q_ref, za_ref, kc_ref, vc_ref, kp_ref, vp_ref, qg_ref, kg_ref, s_ref, o_ref):
        first = pl.program_id(0) == 0
        o_ref[...] = _f_attn(first, q_ref[...], za_ref[...], kc_ref[...], vc_ref[...], kp_ref[...], vp_ref[...],
                             qg_ref[...], kg_ref[...], s_ref[...])

    cur = lambda cb: (lambda i: (i, cb))
    prev = lambda cb: (lambda i: (jnp.maximum(i - 1, 0), cb))
    return pl.pallas_call(
        body, name=name, grid=(nb,),
        in_specs=[pl.BlockSpec((WINDOW, 512), cur(P_QA // 512)), pl.BlockSpec((WINDOW, 512), cur(P_ZA // 512)),
                  pl.BlockSpec((WINDOW, 128), cur(P_KA // 128)), pl.BlockSpec((WINDOW, 128), cur(P_VA // 128)),
                  pl.BlockSpec((WINDOW, 128), prev(P_KA // 128)), pl.BlockSpec((WINDOW, 128), prev(P_VA // 128)),
                  _const_spec((1, 128)), _const_spec((1, 128)), _const_spec((1, 128))],
        out_specs=pl.BlockSpec((WINDOW, 512), lambda i: (i, 0)),
        out_shape=jax.ShapeDtypeStruct((t, 512), F32),
        compiler_params=_cparams(1),
    )(proj, proj, proj, proj, proj, proj, qg, kg, sinks)


def attn_bwd(name, proj, qg, kg, sinks, dya):
    t = proj.shape[0]
    nb = t // WINDOW

    def body(q_ref, za_ref, kc_ref, vc_ref, kp_ref, vp_ref, qg_ref, kg_ref, s_ref, dy_ref,
             dqz_ref, dkv_ref, dqg_ref, dkg_ref, ds_ref, carry_ref):
        j = pl.program_id(0)
        first = j == nb - 1

        @pl.when(j == 0)
        def _():
            carry_ref[...] = jnp.zeros_like(carry_ref)
            dqg_ref[...] = jnp.zeros_like(dqg_ref)
            dkg_ref[...] = jnp.zeros_like(dkg_ref)
            ds_ref[...] = jnp.zeros_like(ds_ref)

        ins = [r[...] for r in (q_ref, za_ref, kc_ref, vc_ref, kp_ref, vp_ref, qg_ref, kg_ref, s_ref)]
        _, vjp = jax.vjp(functools.partial(_f_attn, first), *ins)
        dq, dza, dkc, dvc, dkp, dvp, dqg, dkg, dsk = vjp(dy_ref[...])
        dqz_ref[:, 0:512] = dq.astype(dqz_ref.dtype)
        dqz_ref[:, 512:1024] = dza.astype(dqz_ref.dtype)
        dkv_ref[:, 0:128] = (dkc + carry_ref[0]).astype(dkv_ref.dtype)
        dkv_ref[:, 128:256] = (dvc + carry_ref[1]).astype(dkv_ref.dtype)
        carry_ref[0] = dkp
        carry_ref[1] = dvp
        dqg_ref[...] += dqg
        dkg_ref[...] += dkg
        ds_ref[...] += dsk

    cur = lambda cb: (lambda j: (nb - 1 - j, cb))
    prev = lambda cb: (lambda j: (jnp.maximum(nb - 2 - j, 0), cb))
    return pl.pallas_call(
        body, name=name, grid=(nb,),
        in_specs=[pl.BlockSpec((WINDOW, 512), cur(P_QA // 512)), pl.BlockSpec((WINDOW, 512), cur(P_ZA // 512)),
                  pl.BlockSpec((WINDOW, 128), cur(P_KA // 128)), pl.BlockSpec((WINDOW, 128), cur(P_VA // 128)),
                  pl.BlockSpec((WINDOW, 128), prev(P_KA // 128)), pl.BlockSpec((WINDOW, 128), prev(P_VA // 128)),
                  _const_spec((1, 128)), _const_spec((1, 128)), _const_spec((1, 128)),
                  pl.BlockSpec((WINDOW, 512), cur(0))],
        out_specs=[pl.BlockSpec((WINDOW, 1024), cur(0)), pl.BlockSpec((WINDOW, 256), cur(0)),
                   _const_spec((1, 128)), _const_spec((1, 128)), _const_spec((1, 128))],
        out_shape=[jax.ShapeDtypeStruct((t, 1024), BF16), jax.ShapeDtypeStruct((t, 256), BF16),
                   jax.ShapeDtypeStruct((1, 128), F32), jax.ShapeDtypeStruct((1, 128), F32),
                   jax.ShapeDtypeStruct((1, 128), F32)],
        scratch_shapes=[pltpu.VMEM((2, WINDOW, 128), F32)],
        compiler_params=_cparams(1),
    )(proj, proj, proj, proj, proj, proj, qg, kg, sinks, dya)


CONV_ROWS = 256


def _conv_taps(src_ref, w_ref, n_taps, base, t):
    for r0 in range(0, t, CONV_ROWS):
        acc = w_ref[0:1, :] * src_ref[pl.ds(r0 + base, CONV_ROWS), :]
        for k in range(1, n_taps):
            acc = acc + w_ref[k:k + 1, :] * src_ref[pl.ds(r0 + base + k, CONV_ROWS), :]
        yield r0, acc


def _conv_wgrad(dy_ref, src_ref, n_taps, base, t, dy_base=0):
    out = []
    for k in range(n_taps):
        acc = jnp.zeros((8, 128), F32)
        for r0 in range(0, t, CONV_ROWS):
            prod = dy_ref[pl.ds(r0 + dy_base, CONV_ROWS), :] * src_ref[pl.ds(r0 + base + k, CONV_ROWS), :]
            acc = acc + jnp.sum(prod.reshape(CONV_ROWS // 8, 8, 128), axis=0)
        out.append(jnp.sum(acc, axis=0, keepdims=True))
    return out


def glu_conv_fwd(name, proj, w32, bias):
    t = proj.shape[0]
    pad = 32

    def body(x_ref, w_ref, b_ref, o_ref, u_ref):
        u_ref[0:pad, :] = jnp.zeros((pad, 128), F32)
        u_ref[pad:pad + t, :] = x_ref[:, 0:128] * _sigmoid(x_ref[:, 128:256])
        for r0, acc in _conv_taps(u_ref, w_ref, CONV_K, pad - (CONV_K - 1), t):
            o_ref[pl.ds(r0, CONV_ROWS), :] = acc + b_ref[...]

    return pl.pallas_call(
        body, name=name, grid=(4,),
        in_specs=[pl.BlockSpec((t, 256), lambda cb: (0, P_GLU // 256 + cb)), pl.BlockSpec((32, 128), lambda cb: (0, cb)),
                  pl.BlockSpec((1, 128), lambda cb: (0, cb))],
        out_specs=pl.BlockSpec((t, 128), lambda cb: (0, cb)),
        out_shape=jax.ShapeDtypeStruct((t, 512), F32),
        scratch_shapes=[pltpu.VMEM((t + pad, 128), F32)],
        compiler_params=_cparams(1),
    )(proj, w32, bias)


def glu_conv_bwd(name, proj, w32, dub):
    t = proj.shape[0]
    pad = 32
    k1 = CONV_K - 1

    def body(x_ref, w_ref, dy_ref, dx_ref, dw_ref, db_ref, u_ref, dyp_ref, wrev_ref):
        val = x_ref[:, 0:128]
        sg = _sigmoid(x_ref[:, 128:256])
        u_ref[0:pad, :] = jnp.zeros((pad, 128), F32)
        u_ref[pad:pad + t, :] = val * sg
        dyp_ref[0:t, :] = dy_ref[...]
        dyp_ref[t:t + pad, :] = jnp.zeros((pad, 128), F32)
        for k in range(CONV_K):
            wrev_ref[k:k + 1, :] = w_ref[k1 - k:k1 - k + 1, :]
        wrev_ref[CONV_K:32, :] = jnp.zeros((32 - CONV_K, 128), F32)
        for r0, du in _conv_taps(dyp_ref, wrev_ref, CONV_K, 0, t):
            v = x_ref[pl.ds(r0, CONV_ROWS), 0:128]
            s = _sigmoid(x_ref[pl.ds(r0, CONV_ROWS), 128:256])
            dx_ref[pl.ds(r0, CONV_ROWS), 0:128] = (du * s).astype(dx_ref.dtype)
            dx_ref[pl.ds(r0, CONV_ROWS), 128:256] = (du * v * s * (1.0 - s)).astype(dx_ref.dtype)
        dws = _conv_wgrad(dyp_ref, u_ref, CONV_K, pad - k1, t)
        for k in range(CONV_K):
            dw_ref[k:k + 1, :] = dws[k]
        dw_ref[CONV_K:32, :] = jnp.zeros((32 - CONV_K, 128), F32)
        db_ref[...] = jnp.sum(dy_ref[...], axis=0, keepdims=True)

    return pl.pallas_call(
        body, name=name, grid=(4,),
        in_specs=[pl.BlockSpec((t, 256), lambda cb: (0, P_GLU // 256 + cb)), pl.BlockSpec((32, 128), lambda cb: (0, cb)),
                  pl.BlockSpec((t, 128), lambda cb: (0, cb))],
        out_specs=[pl.BlockSpec((t, 256), lambda cb: (0, cb)), pl.BlockSpec((32, 128), lambda cb: (0, cb)),
                   pl.BlockSpec((1, 128), lambda cb: (0, cb))],
        out_shape=[jax.ShapeDtypeStruct((t, 1024), BF16), jax.ShapeDtypeStruct((32, 512), F32),
                   jax.ShapeDtypeStruct((1, 512), F32)],
        scratch_shapes=[pltpu.VMEM((t + pad, 128), F32), pltpu.VMEM((t + pad, 128), F32), pltpu.VMEM((32, 128), F32)],
        compiler_params=_cparams(1),
    )(proj, w32, dub)


def sconv_fwd(name, proj, w8):
    t = proj.shape[0]
    pad = 8
    k1 = DN_CONV_K - 1

    def body(x_ref, w_ref, o_ref, xp_ref):
        xp_ref[0:pad, :] = jnp.zeros((pad, 128), F32)
        xp_ref[pad:pad + t, :] = x_ref[...]
        for r0, acc in _conv_taps(xp_ref, w_ref, DN_CONV_K, pad - k1, t):
            o_ref[pl.ds(r0, CONV_ROWS), :] = _silu(acc)

    return pl.pallas_call(
        body, name=name, grid=(12,),
        in_specs=[pl.BlockSpec((t, 128), lambda cb: (0, P_QKV // 128 + cb)), pl.BlockSpec((8, 128), lambda cb: (0, cb))],
        out_specs=pl.BlockSpec((t, 128), lambda cb: (0, cb)),
        out_shape=jax.ShapeDtypeStruct((t, 1536), F32),
        scratch_shapes=[pltpu.VMEM((t + pad, 128), F32)],
        compiler_params=_cparams(1),
    )(proj, w8)


def sconv_bwd(name, proj, w8, dqkv):
    t = proj.shape[0]
    pad = 8
    k1 = DN_CONV_K - 1

    def body(x_ref, w_ref, dy_ref, dx_ref, dw_ref, xp_ref, dpp_ref, wrev_ref):
        xp_ref[0:pad, :] = jnp.zeros((pad, 128), F32)
        xp_ref[pad:pad + t, :] = x_ref[...]
        for r0, pre in _conv_taps(xp_ref, w_ref, DN_CONV_K, pad - k1, t):
            s = _sigmoid(pre)
            dpp_ref[pl.ds(r0, CONV_ROWS), :] = dy_ref[pl.ds(r0, CONV_ROWS), :] * (s * (1.0 + pre * (1.0 - s)))
        dpp_ref[t:t + pad, :] = jnp.zeros((pad, 128), F32)
        for k in range(DN_CONV_K):
            wrev_ref[k:k + 1, :] = w_ref[k1 - k:k1 - k + 1, :]
        wrev_ref[DN_CONV_K:8, :] = jnp.zeros((8 - DN_CONV_K, 128), F32)
        for r0, dx in _conv_taps(dpp_ref, wrev_ref, DN_CONV_K, 0, t):
            dx_ref[pl.ds(r0, CONV_ROWS), :] = dx.astype(dx_ref.dtype)
        dws = _conv_wgrad(dpp_ref, xp_ref, DN_CONV_K, pad - k1, t)
        for k in range(DN_CONV_K):
            dw_ref[k:k + 1, :] = dws[k]
        dw_ref[DN_CONV_K:8, :] = jnp.zeros((8 - DN_CONV_K, 128), F32)

    return pl.pallas_call(
        body, name=name, grid=(12,),
        in_specs=[pl.BlockSpec((t, 128), lambda cb: (0, P_QKV // 128 + cb)), pl.BlockSpec((8, 128), lambda cb: (0, cb)),
                  pl.BlockSpec((t, 128), lambda cb: (0, cb))],
        out_specs=[pl.BlockSpec((t, 128), lambda cb: (0, cb)), pl.BlockSpec((8, 128), lambda cb: (0, cb))],
        out_shape=[jax.ShapeDtypeStruct((t, 1536), BF16), jax.ShapeDtypeStruct((8, 1536), F32)],
        scratch_shapes=[pltpu.VMEM((t + pad, 128), F32), pltpu.VMEM((t + pad, 128), F32), pltpu.VMEM((8, 128), F32)],
        compiler_params=_cparams(1),
    )(proj, w8, dqkv)


def _f_delta_step(qkv, ab, zc, s0, s1, s2, s3, a_log, dt_bias, dn_g, inverses=None, with_inverses=False):
    cs = DN_CHUNK
    n = 2 * cs
    states = (s0, s1, s2, s3)
    lane = lax.broadcasted_iota(jnp.int32, (1, 128), 1)
    ri = lax.broadcasted_iota(jnp.int32, (n, n), 0)
    ci = lax.broadcasted_iota(jnp.int32, (n, n), 1)
    same = (ri // cs) == (ci // cs)
    lower = same & (ri >= ci)
    strict = same & (ri > ci)
    sums = jnp.concatenate([jnp.where(lower, 1.0, 0.0), jnp.where(same, 1.0, 0.0), jnp.where(ci < cs, 1.0, 0.0),
                            jnp.where(ci >= cs, 1.0, 0.0)], axis=0)
    top = lax.broadcasted_iota(jnp.int32, (n, 1), 0) < cs

    def pick(row, idx):
        return jnp.sum(jnp.where(lane == idx, row, 0.0), axis=-1, keepdims=True)

    def l2n(x):
        return x * lax.rsqrt(jnp.sum(x * x, axis=-1, keepdims=True) + EPS)

    n_chunks = qkv.shape[0] // cs
    units = [(k, pair) for k in range(n_chunks) for pair in range(2)]

    pre = []
    for k, pair in units:
        hs = (2 * pair, 2 * pair + 1)
        rows = slice(k * cs, (k + 1) * cs)
        stack = lambda f: jnp.concatenate([f(hs[0]), f(hs[1])], axis=0)
        qd = l2n(stack(lambda h: qkv[rows, 128 * h:128 * h + 128])) * (128 ** -0.5)
        kd = l2n(stack(lambda h: qkv[rows, 512 + 128 * h:512 + 128 * h + 128]))
        vd = stack(lambda h: qkv[rows, 1024 + 128 * h:1024 + 128 * h + 128])
        beta = _sigmoid(stack(lambda h: pick(ab[rows], 4 + h)))
        g = stack(lambda h: -jnp.exp(pick(a_log, h)) * _softplus(pick(ab[rows], h) + pick(dt_bias, h)))
        g_sums = sel_mm(sums, g * jnp.ones((1, n), F32))
        gc_col = g_sums[0:n]
        gl_b = g_sums[n:2 * n]
        g_end = (g_sums[2 * n:3 * n], g_sums[3 * n:])
        decay = jnp.where(lower, jnp.exp(jnp.where(lower, gc_col - gc_col.T, 0.0)), 0.0)
        kb = kd * beta
        pre.append(dict(qd=qd, kd=kd, vb=vd * beta, kb=kb, gc_col=gc_col, gl_b=gl_b, g_end=g_end, decay=decay,
                        a=jnp.where(strict, mm_nt(kb, kd) * decay, 0.0)))
    if inverses is None:
        tmats = tri_inv(*[p["a"] for p in pre])
    else:
        tmats = [tri_inv_known(p["a"], t) for p, t in zip(pre, inverses)]

    mid = []
    for p, tmat in zip(pre, tmats):
        egc = jnp.exp(p["gc_col"])
        mid.append(dict(u=mm(tmat, p["vb"]), wm=mm(tmat, p["kb"] * egc), qe=p["qd"] * egc,
                        intra=jnp.where(lower, mm_nt(p["qd"], p["kd"]) * p["decay"], 0.0),
                        ke=p["kd"] * jnp.exp(p["gl_b"] - p["gc_col"]), g_end=p["g_end"]))

    ys = []
    for k in range(n_chunks):
        rows = slice(k * cs, (k + 1) * cs)
        new_states, y_heads = [], []
        for pair in range(2):
            m = mid[2 * k + pair]
            hs = (2 * pair, 2 * pair + 1)
            st = (states[hs[0]], states[hs[1]])
            v_new = m["u"] - jnp.concatenate([mm(m["wm"][:cs], st[0]), mm(m["wm"][cs:], st[1])], axis=0)
            o = jnp.concatenate([mm(m["qe"][:cs], st[0]), mm(m["qe"][cs:], st[1])], axis=0) + mm(m["intra"], v_new)
            new_states.append(st[0] * jnp.exp(m["g_end"][0]) + mm_tn(jnp.where(top, m["ke"], 0.0), v_new))
            new_states.append(st[1] * jnp.exp(m["g_end"][1]) + mm_tn(jnp.where(top, 0.0, m["ke"]), v_new))
            od = o * lax.rsqrt(jnp.mean(o * o, axis=-1, keepdims=True) + EPS) * dn_g
            y_heads += [od[:cs] * _silu(zc[rows, 128 * hs[0]:128 * hs[0] + 128]),
                        od[cs:] * _silu(zc[rows, 128 * hs[1]:128 * hs[1] + 128])]
        states = tuple(new_states)
        ys.append(jnp.concatenate(y_heads, axis=1))
    if with_inverses:
        return (jnp.concatenate(ys, axis=0), *states), tmats
    return (jnp.concatenate(ys, axis=0), *states)


DELTA_ROWS = 4 * DN_CHUNK
DELTA_UNITS = 2 * DELTA_ROWS // DN_CHUNK


def delta_fwd(name, qkv, proj, a_log, dt_bias, dn_g):
    t = qkv.shape[0]
    nc = t // DELTA_ROWS

    def body(qkv_ref, ab_ref, zc_ref, al_ref, dt_ref, g_ref, y_ref, ssave_ref, tsave_ref, s_ref):
        @pl.when(pl.program_id(0) == 0)
        def _():
            s_ref[...] = jnp.zeros_like(s_ref)

        ssave_ref[0] = s_ref[...]
        st = [s_ref[128 * h:128 * h + 128, :] for h in range(4)]
        (y, *ns), tmats = _f_delta_step(qkv_ref[...], ab_ref[...], zc_ref[...], *st, al_ref[...], dt_ref[...], g_ref[...],
                                        with_inverses=True)
        y_ref[...] = y
        for h in range(4):
            s_ref[128 * h:128 * h + 128, :] = ns[h]
        for u, tm in enumerate(tmats):
            tsave_ref[0, 128 * u:128 * u + 128, :] = tm

    return pl.pallas_call(
        body, name=name, grid=(nc,),
        in_specs=[pl.BlockSpec((DELTA_ROWS, 1536), lambda i: (i, 0)), pl.BlockSpec((DELTA_ROWS, 128), lambda i: (i, P_AB // 128)),
                  pl.BlockSpec((DELTA_ROWS, 512), lambda i: (i, P_ZC // 512)),
                  _const_spec((1, 128)), _const_spec((1, 128)), _const_spec((1, 128))],
        out_specs=[pl.BlockSpec((DELTA_ROWS, 512), lambda i: (i, 0)), pl.BlockSpec((1, 512, 128), lambda i: (i, 0, 0)),
                   pl.BlockSpec((1, DELTA_UNITS * 128, 128), lambda i: (i, 0, 0))],
        out_shape=[jax.ShapeDtypeStruct((t, 512), F32), jax.ShapeDtypeStruct((nc, 512, 128), F32),
                   jax.ShapeDtypeStruct((nc, DELTA_UNITS * 128, 128), F32)],
        scratch_shapes=[pltpu.VMEM((512, 128), F32)],
        compiler_params=_cparams(1),
    )(qkv, proj, proj, a_log, dt_bias, dn_g)


def delta_bwd(name, qkv, proj, ssave, tsave, a_log, dt_bias, dn_g, dyc, carry=None):
    t = qkv.shape[0]
    nc = t // DELTA_ROWS
    c_ins, c_in_specs, c_outs, c_out_specs, c_scratch = _host(carry)
    n_ci, n_co = len(c_ins), len(c_outs)

    def body(*refs):
        qkv_ref, ab_ref, zc_ref, ss_ref, ts_ref, al_ref, dt_ref, g_ref, dy_ref = refs[:9]
        dqkv_ref, dab_ref, dzc_ref, dal_ref, ddt_ref, dg_ref = refs[9 + n_ci:15 + n_ci]
        ds_ref = refs[15 + n_ci + n_co]
        carried = (refs[9:9 + n_ci], refs[15 + n_ci:15 + n_ci + n_co], *refs[16 + n_ci + n_co:])

        @pl.when(pl.program_id(0) == 0)
        def _():
            ds_ref[...] = jnp.zeros_like(ds_ref)
            dal_ref[...] = jnp.zeros_like(dal_ref)
            ddt_ref[...] = jnp.zeros_like(ddt_ref)
            dg_ref[...] = jnp.zeros_like(dg_ref)

        if carry is not None:
            carry.emit_start(pl.program_id(0) == 0, *carried)

        st = [ss_ref[0, 128 * h:128 * h + 128, :] for h in range(4)]
        known = [ts_ref[0, 128 * u:128 * u + 128, :] for u in range(DELTA_UNITS)]
        _, vjp = jax.vjp(functools.partial(_f_delta_step, inverses=known), qkv_ref[...], ab_ref[...], zc_ref[...], *st,
                         al_ref[...], dt_ref[...], g_ref[...])
        dst = tuple(ds_ref[128 * h:128 * h + 128, :] for h in range(4))
        dqkv, dab, dzc, d0, d1, d2, d3, dal, ddt, dg = vjp((dy_ref[...], *dst))
        dqkv_ref[...] = dqkv
        dab_ref[...] = dab.astype(dab_ref.dtype)
        dzc_ref[...] = dzc.astype(dzc_ref.dtype)
        for h, d in enumerate((d0, d1, d2, d3)):
            ds_ref[128 * h:128 * h + 128, :] = d
        dal_ref[...] += dal
        ddt_ref[...] += ddt
        dg_ref[...] += dg

        if carry is not None:
            carry.emit_finish(pl.program_id(0) == nc - 1, *carried)

    rev = lambda cb: (lambda j: (nc - 1 - j, cb))
    return pl.pallas_call(
        body, name=name, grid=(nc,),
        in_specs=[pl.BlockSpec((DELTA_ROWS, 1536), rev(0)), pl.BlockSpec((DELTA_ROWS, 128), rev(P_AB // 128)),
                  pl.BlockSpec((DELTA_ROWS, 512), rev(P_ZC // 512)), pl.BlockSpec((1, 512, 128), lambda j: (nc - 1 - j, 0, 0)),
                  pl.BlockSpec((1, DELTA_UNITS * 128, 128), lambda j: (nc - 1 - j, 0, 0)),
                  _const_spec((1, 128)), _const_spec((1, 128)), _const_spec((1, 128)),
                  pl.BlockSpec((DELTA_ROWS, 512), rev(0))] + c_in_specs,
        out_specs=[pl.BlockSpec((DELTA_ROWS, 1536), rev(0)), pl.BlockSpec((DELTA_ROWS, 128), rev(0)),
                   pl.BlockSpec((DELTA_ROWS, 512), rev(0)),
                   _const_spec((1, 128)), _const_spec((1, 128)), _const_spec((1, 128))] + c_out_specs,
        out_shape=[jax.ShapeDtypeStruct((t, 1536), F32), jax.ShapeDtypeStruct((t, 128), BF16),
                   jax.ShapeDtypeStruct((t, 512), BF16),
                   jax.ShapeDtypeStruct((1, 128), F32), jax.ShapeDtypeStruct((1, 128), F32), jax.ShapeDtypeStruct((1, 128), F32)]
        + c_outs,
        scratch_shapes=[pltpu.VMEM((512, 128), F32)] + c_scratch,
        compiler_params=_cparams(1),
    )(qkv, proj, proj, ssave, tsave, a_log, dt_bias, dn_g, dyc, *c_ins)


def loss_head(name, y, target, tm):
    t, d = y.shape

    def body(y_ref, t_ref, dy_ref, l_ref):
        err = y_ref[...] - t_ref[...]
        dy_ref[...] = err * (1.0 / d)
        part = 0.5 * jnp.sum(jnp.sum(err * err, axis=-1, keepdims=True) * (1.0 / d), axis=0, keepdims=True)

        @pl.when(pl.program_id(0) == 0)
        def _():
            l_ref[...] = part

        @pl.when(pl.program_id(0) > 0)
        def _():
            l_ref[...] += part

    return pl.pallas_call(
        body, name=name, grid=(t // tm,),
        in_specs=[_row_spec(tm, d, 0), _row_spec(tm, d, 0)],
        out_specs=[_row_spec(tm, d, 0), _const_spec((1, 1))],
        out_shape=[jax.ShapeDtypeStruct((t, d), F32), jax.ShapeDtypeStruct((1, 1), F32)],
        compiler_params=_cparams(1),
    )(y, target)


TM = 512
TM_MERGE = 256
TM_IN = 1024
TN_IN = 1152


def _lane_pad(v, n=128):
    return jnp.pad(v.astype(F32), (0, n - v.shape[0]))[None, :]


def f_norm_mod_res(x, g, scale, shift):
    return f_norm_mod(x, g, scale, shift), x


def prep_layer(w):
    p = dict(w)
    p["wp"] = _w_in_assemble(w["w_in"])
    p["wpa"] = _perm_heads_rows(w["w_proj_a"])
    p["dw32"] = jnp.pad(w["dw_w"], ((0, 32 - CONV_K), (0, 0)))
    p["sconv8"] = jnp.pad(w["sconv_w"], ((0, 8 - DN_CONV_K), (0, 0)))
    p["qg"] = jnp.tile(w["q_norm_g"], 2)[None, :]
    p["kg"] = jnp.tile(w["k_norm_g"], 2)[None, :]
    p["sinks128"] = _lane_pad(w["sinks"])
    p["al"] = _lane_pad(w["a_log"])
    p["dtb"] = _lane_pad(w["dt_bias"])
    p["dng"] = w["dn_norm_g"][None, :]
    return p


def layer_fwd(tag, x, mod, p, carry_inproj=None, carry_merge=None):
    d = D_MODEL
    shift, scale, gate = mod[:, :d], mod[:, d:2 * d], mod[:, 2 * d:]
    g = p["norm_g"][None, :]
    h, h_t = rowwise_fwd(f"norm_fwd{tag}", lambda *a: (f_norm_mod(*a),) * 2, [(x, d, 0)], [g, scale, shift],
                         [(d, BF16), (d, BF16, "transposed")], TM)
    proj = matmul_nn(f"inproj_fwd{tag}", h, p["wp"], F32, TM_IN, TN_IN, d, carry=carry_inproj)
    proj, got_inproj = (proj, []) if carry_inproj is None else (proj[0], proj[1:])
    ya = attn_fwd(f"attn_fwd{tag}", proj, p["qg"], p["kg"], p["sinks128"])
    ub = glu_conv_fwd(f"glu_conv_fwd{tag}", proj, p["dw32"], p["dw_b"][None, :])
    conf_consts = [p["ln_g"][None, :], p["ln_b"][None, :], p["pw2_w"], p["pw2_b"][None, :]]
    (yb,) = rowwise_fwd(f"conf_fwd{tag}", f_conf_tail, [(ub, 512, 0), (proj, 512, P_ZB // 512)], conf_consts, [(512, F32)], TM)
    qkv = sconv_fwd(f"sconv_fwd{tag}", proj, p["sconv8"])
    yc, ssave, tsave = delta_fwd(f"delta_fwd{tag}", qkv, proj, p["al"], p["dtb"], p["dng"])
    merge_consts = [gate, p["wpa"], p["w_proj_b"], p["w_proj_c"], p["w_out"]]
    merge_rows = [(ya, 512, 0), (yb, 512, 0), (yc, 512, 0), (proj, 3 * d, P_MG // (3 * d)), (x, d, 0)]
    xn, *got_merge = rowwise_fwd(f"merge_fwd{tag}", f_merge, merge_rows, merge_consts, [(d, F32)], TM_MERGE, carry=carry_merge)
    saved = dict(x=x, h_t=h_t, proj=proj, ub=ub, qkv=qkv, ssave=ssave, tsave=tsave, norm_consts=[g, scale, shift],
                 conf_consts=conf_consts, merge_consts=merge_consts, merge_rows=merge_rows)
    return xn, saved, got_inproj, got_merge


def layer_bwd(tag, dxn, p, s, carry_merge=None, carry_delta=None, carry_dh=None):
    d = D_MODEL
    proj = s["proj"]
    dya, dyb, dyc, dmg, dgate, dwpa, dwpb, dwpc, dwout, *got_merge = rowwise_bwd(
        f"merge_bwd{tag}", f_merge, s["merge_rows"], s["merge_consts"], [(dxn, d, 0)], [F32, F32, F32, BF16, None], TM_MERGE,
        carry=carry_merge)
    carry_delta = None if carry_delta is None else carry_delta(got_merge)
    dqz, dkv, dqg, dkg, dsinks = attn_bwd(f"attn_bwd{tag}", proj, p["qg"], p["kg"], p["sinks128"], dya)
    dub, dzb, dln_g, dln_b, dpw2_w, dpw2_b = rowwise_bwd(
        f"conf_bwd{tag}", f_conf_tail, [(s["ub"], 512, 0), (proj, 512, P_ZB // 512)], s["conf_consts"], [(dyb, 512, 0)],
        [F32, BF16], TM)
    dglu, ddw32, ddw_b = glu_conv_bwd(f"glu_conv_bwd{tag}", proj, p["dw32"], dub)
    dqkv, dab, dzc, dal, ddtb, ddng, *got_delta = delta_bwd(f"delta_bwd{tag}", s["qkv"], proj, s["ssave"], s["tsave"], p["al"],
                                                            p["dtb"], p["dng"], dyc, carry_delta)
    dqkv_pre, dsconv8 = sconv_bwd(f"sconv_bwd{tag}", proj, p["sconv8"], dqkv)
    dproj = jnp.concatenate([dqz, dglu, dzb, dzc, dmg, dqkv_pre, dkv, dab], axis=1)
    dwp = matmul_nn(f"inproj_bwd_dw{tag}", s["h_t"], dproj, F32, d, TN_IN, 2048)
    reduced = dict(w_in=_w_in_grad_blocks(dwp), pw2_w=dpw2_w, w_proj_a=_unperm_heads_rows(dwpa), w_proj_b=dwpb, w_proj_c=dwpc,
                   w_out=dwout)
    carry_dh = None if carry_dh is None else carry_dh(reduced)
    dh = matmul_nn(f"inproj_bwd_dh{tag}", dproj, p["wp"], F32, TM_IN, d, P_TOTAL // 3, b_transposed=True, carry=carry_dh)
    dh, got_dh = (dh, []) if carry_dh is None else (dh[0], dh[1:])
    dx, dnorm_g, dscale, dshift = rowwise_bwd(
        f"norm_bwd{tag}", f_norm_mod_res, [(s["x"], d, 0)], s["norm_consts"], [(dh, d, 0), (dxn, d, 0)], [F32], TM)
    dmod = jnp.concatenate([dshift, dscale, dgate], axis=1)
    grads = dict(
        reduced, b_ada=dmod[0], norm_g=dnorm_g[0],
        q_norm_g=dqg[0, :64] + dqg[0, 64:], k_norm_g=dkg[0, :64] + dkg[0, 64:], sinks=dsinks[0, :ATT_HEADS],
        dw_w=ddw32[:CONV_K], dw_b=ddw_b[0], ln_g=dln_g[0], ln_b=dln_b[0], pw2_b=dpw2_b[0],
        sconv_w=dsconv8[:DN_CONV_K], a_log=dal[0, :DN_HEADS], dt_bias=ddtb[0, :DN_HEADS], dn_norm_g=ddng[0])
    return dx, grads, got_merge, got_delta, got_dh


SHARDED = {"w_ada": 2, "w_in": 2, "dw_w": 2, "pw2_w": 1, "sconv_w": 2, "w_proj_a": 2, "w_proj_b": 2, "w_proj_c": 2,
           "w_out": 1}
GATHERED = tuple(n for n in SHARDED if n != "w_ada")
GATHER_F32 = ("dw_w", "sconv_w")
REDUCE_BIG = tuple(n for n in GATHERED if n not in GATHER_F32)
SMALL = ("b_ada", "norm_g", "q_norm_g", "k_norm_g", "sinks", "dw_b", "ln_g", "ln_b", "pw2_b", "a_log", "dt_bias",
         "dn_norm_g")
SMALL_ROWS = 104
SMALL_GRAD_ROWS = 448
W_IN_SHARD = D_IN // N_CHIPS
SUM_PARTS = 4


def _w_in_orig():
    orig = np.full(P_TOTAL, -1, np.int64)
    p = 0
    for s, n in _in_pieces():
        orig[p:p + n] = np.arange(s, s + n)
        p += n
    return orig


def _w_in_blocks(k):
    orig = _w_in_orig().reshape(-1, 128)
    lo, hi = k * W_IN_SHARD, (k + 1) * W_IN_SHARD
    return [b for b in range(orig.shape[0]) if np.any((orig[b] >= lo) & (orig[b] < hi))]


W_IN_BLOCKS = max(len(_w_in_blocks(k)) for k in range(N_CHIPS))


def _runs(idx):
    out, i = [], 0
    while i < len(idx):
        j = i + 1
        while j < len(idx) and ((idx[i] < 0 and idx[j] < 0) or (idx[i] >= 0 and idx[j] == idx[j - 1] + 1)):
            j += 1
        out.append((int(idx[i]) if idx[i] >= 0 else -1, j - i))
        i = j
    return out


def _take(a, idx):
    parts = [jnp.zeros(a.shape[:-1] + (n,), a.dtype) if s < 0 else a[..., s:s + n] for s, n in _runs(idx)]
    return parts[0] if len(parts) == 1 else jnp.concatenate(parts, axis=-1)


def _w_in_send(k, shard):
    orig = _w_in_orig().reshape(-1, 128)
    lo, hi = k * W_IN_SHARD, (k + 1) * W_IN_SHARD
    idx = np.concatenate([np.where((orig[b] >= lo) & (orig[b] < hi), orig[b] - lo, -1) for b in _w_in_blocks(k)])
    idx = np.concatenate([idx, np.full((W_IN_BLOCKS - len(_w_in_blocks(k))) * 128, -1)])
    return _take(shard, idx)


def _w_in_assemble(blocks):
    where = [{b: i for i, b in enumerate(_w_in_blocks(k))} for k in range(N_CHIPS)]
    n_blocks = P_TOTAL // 128
    owners = [[(k, where[k][b]) for k in range(N_CHIPS) if b in where[k]] for b in range(n_blocks)]
    parts, b = [], 0
    while b < n_blocks:
        if len(owners[b]) == 1:
            k, pos = owners[b][0]
            e = b + 1
            while e < n_blocks and owners[e] == [(k, pos + e - b)]:
                e += 1
            parts.append(blocks[k][:, pos * 128:(pos + e - b) * 128])
            b = e
        else:
            parts.append(functools.reduce(jnp.add, [blocks[k][:, pos * 128:(pos + 1) * 128] for k, pos in owners[b]]))
            b += 1
    return jnp.concatenate(parts, axis=1)


def _w_in_grad_blocks(wp):
    out = []
    for k in range(N_CHIPS):
        idx = np.concatenate([np.arange(128 * b, 128 * b + 128) for b in _w_in_blocks(k)])
        idx = np.concatenate([idx, np.full((W_IN_BLOCKS - len(_w_in_blocks(k))) * 128, -1)])
        out.append(_take(wp, idx))
    return jnp.stack(out)


def _w_in_receive_grad(k, blocks):
    orig = _w_in_orig()
    inv = np.zeros(D_IN, np.int64)
    inv[orig[orig >= 0]] = np.nonzero(orig >= 0)[0]
    where = {b: i for i, b in enumerate(_w_in_blocks(k))}
    cols = inv[k * W_IN_SHARD:(k + 1) * W_IN_SHARD]
    return _take(blocks, np.array([where[c // 128] * 128 + c % 128 for c in cols]))


def _join_layer(v, axis):
    if axis == 2:
        return jnp.transpose(v, (1, 0, 2)).reshape(v.shape[1], N_CHIPS * v.shape[2])
    return v.reshape(N_CHIPS * v.shape[1], v.shape[2])


def _split_layer(v, axis):
    a, b = v.shape
    if axis == 2:
        return jnp.transpose(v.reshape(a, N_CHIPS, b // N_CHIPS), (1, 0, 2))
    return v.reshape(N_CHIPS, a // N_CHIPS, b)


def pack_small(vals, names, rows):
    flat = jnp.concatenate([vals[n].astype(F32).reshape(-1) for n in names])
    return jnp.pad(flat, (0, rows * 128 - flat.shape[0])).reshape(rows, 128)


def unpack_small(packed, names, shapes):
    flat = packed.reshape(-1)
    out, off = {}, 0
    for n in names:
        k = int(np.prod(shapes[n]))
        out[n] = flat[off:off + k].reshape(shapes[n])
        off += k
    return out


ANY = pl.BlockSpec(memory_space=pl.ANY)


def _place():
    x, y, c = lax.axis_index("x"), lax.axis_index("y"), lax.axis_index("c")
    chips = [(1 - x, y), (x, 1 - y), (1 - x, 1 - y)]
    return x, y, c, chips


def _remote(src, dst, send_sem, recv_sem, to):
    return pltpu.make_async_remote_copy(src_ref=src, dst_ref=dst, send_sem=send_sem, recv_sem=recv_sem, device_id=to,
                                        device_id_type=MESH)


class Carry:
    def __init__(self, ins, out_shapes, sems, start, finish, in_place=False):
        self.ins, self.out_shapes, self.sems, self.start, self.finish, self.in_place = (
            list(ins), list(out_shapes), sems, start, finish, in_place)

    def scratch(self):
        return [pltpu.SemaphoreType.DMA(self.sems), pltpu.SemaphoreType.DMA(self.sems)]

    def aliases(self, first_in, first_out):
        return {first_in + i: first_out + i for i in range(len(self.ins))} if self.in_place else {}

    def emit_start(self, first, in_refs, out_refs, send_sems, recv_sems):
        @pl.when(first)
        def _():
            self.start(in_refs, out_refs, send_sems, recv_sems)

    def emit_finish(self, last, in_refs, out_refs, send_sems, recv_sems):
        @pl.when(last)
        def _():
            self.finish(in_refs, out_refs, send_sems, recv_sems)


def _host(carry):
    if carry is None:
        return [], [], [], [], []
    return carry.ins, [ANY] * len(carry.ins), carry.out_shapes, [ANY] * len(carry.out_shapes), carry.scratch()


def run_carry(name, carry):
    n_in, n_out = len(carry.ins), len(carry.out_shapes)

    def body(*refs):
        ins, outs, sems = refs[:n_in], refs[n_in:n_in + n_out], refs[n_in + n_out:]
        carry.start(ins, outs, *sems)
        carry.finish(ins, outs, *sems)

    return pl.pallas_call(
        body, name=name, out_shape=carry.out_shapes, in_specs=[ANY] * n_in, out_specs=[ANY] * n_out,
        input_output_aliases=carry.aliases(0, 0), scratch_shapes=carry.scratch(),
    )(*carry.ins)


def carry_allgather(layer, slots):
    n = len(slots)

    def copies(out, send_sems, recv_sems, only_ici_out=False):
        x, y, c, chips = _place()
        ici_out, ici_in, d2d_out, d2d_in = [], [], [], []
        for j, chip in enumerate(chips):
            for t in range(n):
                mine, land = out[t].at[2 * x + y], out[t].at[2 * chip[0] + chip[1]]
                ici_out.append(_remote(mine, mine, send_sems.at[t, j], recv_sems.at[t, j], (*chip, layer)))
                if only_ici_out:
                    continue
                ici_in.append(_remote(land, land, send_sems.at[t, j], recv_sems.at[t, j], (*chip, layer)))
                d2d_out.append(_remote(land, land, send_sems.at[t, 3 + j], recv_sems.at[t, 3 + j], (x, y, 1 - layer)))
                d2d_in.append(_remote(land, land, send_sems.at[t, 3 + j], recv_sems.at[t, 3 + j], (x, y, layer)))
        return c, ici_out, ici_in, d2d_out, d2d_in

    def start(ins, out, send_sems, recv_sems):
        c, ici_out, _, _, _ = copies(out, send_sems, recv_sems, only_ici_out=True)

        @pl.when(c == layer)
        def _():
            for cp in ici_out:
                cp.start()

    def finish(ins, out, send_sems, recv_sems):
        c, ici_out, ici_in, d2d_out, d2d_in = copies(out, send_sems, recv_sems)

        @pl.when(c == layer)
        def _():
            for arrived, onward in zip(ici_in, d2d_out):
                arrived.wait_recv()
                onward.start()
            for cp in ici_out + d2d_out:
                cp.wait_send()

        @pl.when(c != layer)
        def _():
            for cp in d2d_in:
                cp.wait_recv()

    return Carry(slots, [jax.ShapeDtypeStruct(s.shape, s.dtype) for s in slots], (n, 6), start, finish, in_place=True)


def carry_pair_send(layer, gs):
    def copies(g, recv, send_sems, recv_sems):
        x, y, c, _ = _place()
        return c, [_remote(g[t], recv[t], send_sems.at[t], recv_sems.at[t], (x, y, 1 - c)) for t in range(len(gs))]

    def start(g, recv, send_sems, recv_sems):
        c, cps = copies(g, recv, send_sems, recv_sems)

        @pl.when(c != layer)
        def _():
            for cp in cps:
                cp.start()

    def finish(g, recv, send_sems, recv_sems):
        c, cps = copies(g, recv, send_sems, recv_sems)

        @pl.when(c != layer)
        def _():
            for cp in cps:
                cp.wait_send()

        @pl.when(c == layer)
        def _():
            for cp in cps:
                cp.wait_recv()

    return Carry(gs, [jax.ShapeDtypeStruct(g.shape, g.dtype) for g in gs], (len(gs),), start, finish)


def grads_pair_sums(layer, gs, recv):
    n = len(gs)

    def body(*refs):
        for t in range(n):
            refs[2 * n + t][...] = (refs[t][...] + refs[n + t][...]).astype(BF16)

    specs = [pl.BlockSpec((None, g.shape[1] // SUM_PARTS, g.shape[2]), lambda s, i: (s, i, 0)) for g in gs]
    return pl.pallas_call(
        body, name=f"grads_pair_sums{layer}", grid=(N_CHIPS, SUM_PARTS), in_specs=specs + specs, out_specs=specs,
        out_shape=[jax.ShapeDtypeStruct(g.shape, BF16) for g in gs], compiler_params=_cparams(2),
    )(*gs, *recv)


def carry_chip_exchange(layer, ps):
    def copies(p, recv, send_sems, recv_sems):
        _, _, c, chips = _place()
        return c, [_remote(p[t].at[2 * chip[0] + chip[1]], recv[t].at[j], send_sems.at[t, j], recv_sems.at[t, j],
                           (*chip, layer)) for j, chip in enumerate(chips) for t in range(len(ps))]

    def start(p, recv, send_sems, recv_sems):
        c, cps = copies(p, recv, send_sems, recv_sems)

        @pl.when(c == layer)
        def _():
            for cp in cps:
                cp.start()

    def finish(p, recv, send_sems, recv_sems):
        c, cps = copies(p, recv, send_sems, recv_sems)

        @pl.when(c == layer)
        def _():
            for cp in cps:
                cp.wait()

    return Carry(ps, [jax.ShapeDtypeStruct((3,) + p.shape[1:], p.dtype) for p in ps], (len(ps), 3), start, finish)


def grads_chip_sums(layer, gs, recv, recv2, into=None):
    n = len(gs)
    my_slot = lambda: 2 * lax.axis_index("x") + lax.axis_index("y")

    def body(*refs):
        outs = refs[-n:]
        for t in range(n):
            r2 = refs[2 * n + t]
            own = refs[t][...] + refs[n + t][...]
            outs[t][...] = ((own + r2[0].astype(F32)) + r2[1].astype(F32)) + r2[2].astype(F32)

    part = lambda g: g.shape[1] // SUM_PARTS
    own_specs = [pl.BlockSpec((None, part(g), g.shape[2]), lambda i: (my_slot(), i, 0)) for g in gs]
    return pl.pallas_call(
        body, name=f"grads_chip_sums{layer}", grid=(SUM_PARTS,),
        in_specs=own_specs + own_specs + [pl.BlockSpec((3, part(g), g.shape[2]), lambda i: (0, i, 0)) for g in gs]
        + ([] if into is None else [ANY] * n),
        out_specs=[pl.BlockSpec((None, part(g), g.shape[2]), lambda i: (layer, i, 0)) for g in gs],
        out_shape=[jax.ShapeDtypeStruct((DEPTH,) + g.shape[1:], F32) for g in gs],
        input_output_aliases={} if into is None else {3 * n + t: t for t in range(n)},
        compiler_params=_cparams(1),
    )(*gs, *recv, *recv2, *([] if into is None else into))


def grads_pair_gather(reds):
    n = len(reds)

    def body(*refs):
        buf = refs[n:2 * n]
        send_sems, recv_sems = refs[2 * n:]
        x, y, c, _ = _place()
        sibling = (x, y, 1 - c)
        cps = [_remote(buf[t].at[c], buf[t].at[c], send_sems.at[t], recv_sems.at[t], sibling) for t in range(n)]
        for cp in cps:
            cp.start()
        for t in range(n):
            _remote(buf[t].at[c], buf[t].at[1 - c], send_sems.at[t], recv_sems.at[t], sibling).wait_recv()
        for cp in cps:
            cp.wait_send()

    return pl.pallas_call(
        body, name="grads_pair_gather", out_shape=[jax.ShapeDtypeStruct(r.shape, r.dtype) for r in reds],
        in_specs=[ANY] * n, out_specs=[ANY] * n, input_output_aliases={t: t for t in range(n)},
        scratch_shapes=[pltpu.SemaphoreType.DMA((n,)), pltpu.SemaphoreType.DMA((n,))],
    )(*reds)


def small_allreduce(v):
    m, n = v.shape

    def body(x_ref, sum_ref, all_ref, send_sems, recv_sems, local_sem):
        x, y, c, chips = _place()
        me, sibling = (x, y, c), (x, y, 1 - c)

        def rows(px, py, pc):
            return all_ref.at[pl.ds((4 * px + 2 * py + pc) * m, m), :]

        def copy(k, block, to, src=None):
            return pltpu.make_async_remote_copy(src_ref=rows(*block) if src is None else src, dst_ref=rows(*block),
                                                send_sem=send_sems.at[k], recv_sem=recv_sems.at[k],
                                                device_id=to, device_id_type=MESH)

        mine = pltpu.make_async_copy(x_ref, rows(*me), local_sem)
        mine.start()
        first = [copy(0, me, sibling, src=x_ref)]
        first += [copy(1 + j, me, (*chip, c), src=x_ref) for j, chip in enumerate(chips)]
        for cp in first:
            cp.start()
        passed = [copy(4 + j, (*chip, c), sibling) for j, chip in enumerate(chips)]
        for j, chip in enumerate(chips):
            copy(1 + j, (*chip, c), me).wait_recv()
            passed[j].start()
        copy(0, sibling, me).wait_recv()
        for j, chip in enumerate(chips):
            copy(4 + j, (*chip, 1 - c), me).wait_recv()
        for cp in first + passed:
            cp.wait_send()
        mine.wait()
        acc = all_ref[0:m, :]
        for dev in range(1, 8):
            acc = acc + all_ref[dev * m:(dev + 1) * m, :]
        sum_ref[...] = acc

    vm = pl.BlockSpec(memory_space=pltpu.VMEM)
    return pl.pallas_call(
        body, name="small_allreduce",
        out_shape=[jax.ShapeDtypeStruct((m, n), F32), jax.ShapeDtypeStruct((8 * m, n), F32)],
        in_specs=[vm], out_specs=[vm, vm],
        scratch_shapes=[pltpu.SemaphoreType.DMA((7,)), pltpu.SemaphoreType.DMA((7,)), pltpu.SemaphoreType.DMA],
    )(v)


def grads_by_chip(layer_grads):
    return [layer_grads[n] if n == "w_in" else _split_layer(layer_grads[n], SHARDED[n]) for n in REDUCE_BIG]


def _adamw_block(w_ref, g_ref, m_ref, v_ref, d_ref, nm_ref, nv_ref):
    gv = g_ref[...]
    nm = ADAM_B1 * m_ref[...] + (1.0 - ADAM_B1) * gv
    nv = ADAM_B2 * v_ref[...] + (1.0 - ADAM_B2) * (gv * gv)
    m_hat = nm / (1.0 - ADAM_B1 ** ADAM_STEP)
    v_hat = nv / (1.0 - ADAM_B2 ** ADAM_STEP)
    d_ref[...] = -ADAM_LR * (m_hat / (jnp.sqrt(v_hat) + ADAM_EPS) + ADAM_WD * w_ref[...])
    nm_ref[...] = nm
    nv_ref[...] = nv


def adamw(name, w, g, m, v, block):
    grid = tuple(s // b for s, b in zip(w.shape, block))

    def body(*refs):
        _adamw_block(*refs)

    spec = pl.BlockSpec(tuple(block), lambda *idx: idx)
    return pl.pallas_call(
        body, name=name, grid=grid, in_specs=[spec] * 4, out_specs=[spec] * 3,
        out_shape=[jax.ShapeDtypeStruct(w.shape, F32)] * 3, compiler_params=_cparams(len(grid)),
    )(w, g, m, v)


def adamw_many(name, groups):
    n = len(groups)

    def spec(a):
        rows, cols = a.shape
        if rows % (8 * ADAM_PARTS) == 0:
            return pl.BlockSpec((rows // ADAM_PARTS, cols), lambda i: (i, 0))
        return pl.BlockSpec((rows, cols), lambda i: (0, 0))

    def body(*refs):
        for t in range(n):
            _adamw_block(*refs[4 * t:4 * t + 4], *refs[4 * n + 3 * t:4 * n + 3 * t + 3])

    res = pl.pallas_call(
        body, name=name, grid=(ADAM_PARTS,),
        in_specs=[spec(grp[0]) for grp in groups for _ in range(4)],
        out_specs=[spec(grp[0]) for grp in groups for _ in range(3)],
        out_shape=[jax.ShapeDtypeStruct(grp[0].shape, F32) for grp in groups for _ in range(3)],
        compiler_params=_cparams(1),
    )(*[a for grp in groups for a in grp])
    return [tuple(res[3 * t:3 * t + 3]) for t in range(n)]


ADAM_PARTS = 4
ADAM_W_IN_COLS = 331

WEIGHT_NAMES = ("w_ada", "b_ada", "norm_g", "w_in", "q_norm_g", "k_norm_g", "sinks", "dw_w", "dw_b", "ln_g", "ln_b",
                "pw2_w", "pw2_b", "sconv_w", "a_log", "dt_bias", "dn_norm_g", "w_proj_a", "w_proj_b", "w_proj_c", "w_out")


def kernel(x, c, w_ada, b_ada, norm_g, w_in, q_norm_g, k_norm_g, sinks, dw_w, dw_b, ln_g, ln_b, pw2_w, pw2_b, sconv_w, a_log, dt_bias, dn_norm_g, w_proj_a, w_proj_b, w_proj_c, w_out, loss_target, m_w_ada, m_b_ada, m_norm_g, m_w_in, m_q_norm_g, m_k_norm_g, m_sinks, m_dw_w, m_dw_b, m_ln_g, m_ln_b, m_pw2_w, m_pw2_b, m_sconv_w, m_a_log, m_dt_bias, m_dn_norm_g, m_w_proj_a, m_w_proj_b, m_w_proj_c, m_w_out, v_w_ada, v_b_ada, v_norm_g, v_w_in, v_q_norm_g, v_k_norm_g, v_sinks, v_dw_w, v_dw_b, v_ln_g, v_ln_b, v_pw2_w, v_pw2_b, v_sconv_w, v_a_log, v_dt_bias, v_dn_norm_g, v_w_proj_a, v_w_proj_b, v_w_proj_c, v_w_out):
    args = dict(locals())
    w = {n: args[n] for n in WEIGHT_NAMES}
    mom = {n: args["m_" + n] for n in WEIGHT_NAMES}
    var = {n: args["v_" + n] for n in WEIGHT_NAMES}

    chip = 2 * lax.axis_index("x") + lax.axis_index("y")
    own = {n: w[n] if n in GATHER_F32 else w[n].astype(BF16) for n in GATHERED}
    own["w_in"] = lax.switch(chip, [functools.partial(_w_in_send, k) for k in range(N_CHIPS)], own["w_in"])
    slots = [[lax.dynamic_update_slice(lax.empty((N_CHIPS,) + own[n].shape[1:], own[n].dtype), own[n][l][None], (chip, 0, 0))
              for n in GATHERED] for l in range(DEPTH)]

    def layer_operands(l, gathered):
        lw = {n: w[n][l] for n in SMALL}
        lw.update({n: g if n == "w_in" else _join_layer(g, SHARDED[n]) for n, g in zip(GATHERED, gathered)})
        return prep_layer(lw)

    layers = [layer_operands(0, run_carry("weights_allgather0", carry_allgather(0, slots[0]))), None]

    mod, conds = ada_fwd(jnp.tile(c, (8, 1)), w["w_ada"], w["b_ada"])
    saved = [None] * DEPTH
    big = GATHERED.index("w_in")
    rest = [i for i in range(len(GATHERED)) if i != big]
    act, saved[0], got_big, got_rest = layer_fwd(
        "0", x[0], mod[0:1], layers[0], carry_inproj=carry_allgather(1, [slots[1][big]]),
        carry_merge=carry_allgather(1, [slots[1][i] for i in rest]))
    gathered1 = dict(zip(rest, got_rest))
    gathered1[big] = got_big[0]
    layers[1] = layer_operands(1, [gathered1[i] for i in range(len(GATHERED))])
    act, saved[1], _, _ = layer_fwd("1", act, mod[1:2], layers[1])
    dact, loss_part = loss_head("loss_head", act, loss_target[0], TM)
    loss = lax.psum(loss_part[0, 0], ("x", "y", "c"))
    layer_grads = [None] * DEPTH
    dact, layer_grads[1], _, _, _ = layer_bwd("1", dact, layers[1], saved[1])
    gs1 = grads_by_chip(layer_grads[1])
    gs0 = []

    def hand_over_layer0(reduced):
        gs0.extend(grads_by_chip(reduced))
        return carry_pair_send(0, gs0)

    dact, layer_grads[0], recv1, got1, recv0 = layer_bwd(
        "0", dact, layers[0], saved[0], carry_merge=carry_pair_send(1, gs1),
        carry_delta=lambda recv: carry_chip_exchange(1, grads_pair_sums(1, gs1, recv)), carry_dh=hand_over_layer0)

    got0 = run_carry("grads_chip_exchange0", carry_chip_exchange(0, grads_pair_sums(0, gs0, recv0)))
    reds = grads_chip_sums(0, gs0, recv0, got0, into=grads_chip_sums(1, gs1, recv1, got1))
    final_grads = dict(zip(REDUCE_BIG, grads_pair_gather(reds)))
    final_grads["w_in"] = lax.switch(chip, [functools.partial(_w_in_receive_grad, k) for k in range(N_CHIPS)],
                                     final_grads["w_in"])
    small_names = SMALL + GATHER_F32
    small_shapes = {n: (DEPTH,) + layer_grads[0][n].shape for n in small_names}
    small_full = {n: jnp.stack([layer_grads[l][n] for l in range(DEPTH)]) for n in small_names}
    small_sum, small_all = small_allreduce(pack_small(small_full, small_names, SMALL_GRAD_ROWS))
    small_sum = unpack_small(small_sum, small_names, small_shapes)
    for n in GATHER_F32:
        width = w[n].shape[2]
        final_grads[n] = lax.dynamic_slice_in_dim(small_sum[n], chip * width, width, axis=2)
    n_mod = DEPTH * 3 * D_MODEL
    dmod = small_all.reshape(8, -1)[:, :n_mod].reshape(8, DEPTH, 3 * D_MODEL)
    width = w["w_ada"].shape[2]
    dmod = jnp.transpose(lax.dynamic_slice_in_dim(dmod, chip * width, width, axis=2), (1, 0, 2))
    final_grads["w_ada"] = ada_bwd(conds, dmod)
    final_grads.update({n: small_sum[n] for n in SMALL})
    small_grads = pack_small(final_grads, SMALL, SMALL_ROWS)

    delta, new_m, new_v = {}, {}, {}
    shp = w["w_in"].shape
    view = lambda a: jnp.transpose(a, (2, 0, 1))
    back = lambda a: jnp.transpose(a, (1, 2, 0))
    g3 = view(final_grads["w_in"])
    final_grads["w_in"] = back(g3)
    d, nm, nv = adamw("adamw_w_in", view(w["w_in"]), g3, view(mom["w_in"]), view(var["w_in"]),
                      (ADAM_W_IN_COLS, shp[0], shp[1]))
    delta["w_in"], new_m["w_in"], new_v["w_in"] = back(d), back(nm), back(nv)
    others = [n for n in SHARDED if n != "w_in"]
    two_d = lambda a: a.reshape(a.shape[0] * a.shape[1], a.shape[2])
    groups = [tuple(two_d(t[n]) for t in (w, final_grads, mom, var)) for n in others]
    groups.append((pack_small(w, SMALL, SMALL_ROWS), small_grads, pack_small(mom, SMALL, SMALL_ROWS),
                   pack_small(var, SMALL, SMALL_ROWS)))
    results = adamw_many("adamw_rest", groups)
    for n, (d, nm, nv) in zip(others, results):
        delta[n], new_m[n], new_v[n] = (a.reshape(w[n].shape) for a in (d, nm, nv))
    for out, packed in zip((delta, new_m, new_v), results[-1]):
        out.update(unpack_small(packed, SMALL, small_shapes))

    return (loss, dact[None], *[final_grads[n] for n in WEIGHT_NAMES], *[delta[n] for n in WEIGHT_NAMES],
            *[new_m[n] for n in WEIGHT_NAMES], *[new_v[n] for n in WEIGHT_NAMES])
```

```python
import functools

import numpy as np
import jax
import jax.numpy as jnp
from jax import lax
from jax.experimental import pallas as pl
from jax.experimental.pallas import tpu as pltpu

F32 = jnp.float32
BF16 = jnp.bfloat16
MESH = pl.DeviceIdType.MESH

D_MODEL = 1024
DEPTH = 2
ATT_HEADS = 8
ATT_HEAD_DIM = 64
WINDOW = 128
CONV_K = 31
DN_HEADS = 4
DN_CONV_K = 4
DN_CHUNK = 64
EPS = 1e-6
NEG_INF = -1e30
N_CHIPS = 4
D_IN = 7944

ADAM_LR = 0.001
ADAM_B1 = 0.9
ADAM_B2 = 0.999
ADAM_EPS = 1e-08
ADAM_WD = 0.01
ADAM_STEP = 10

VMEM_LIMIT = 56 * 1024 * 1024

P_QA, P_ZA, P_GLU, P_ZB, P_ZC, P_MG, P_QKV, P_KA, P_VA, P_AB, P_TOTAL = (
    0, 512, 1024, 2048, 2560, 3072, 6144, 7680, 7808, 7936, 8064)
HEAD_ORDER = (0, 4, 1, 5, 2, 6, 3, 7)


def _in_pieces():
    p = [(0 + 64 * h, 64) for h in HEAD_ORDER]
    p += [(768 + 64 * h, 64) for h in HEAD_ORDER]
    for g in range(4):
        p += [(1280 + 128 * g, 128), (1792 + 128 * g, 128)]
    p += [(2304, 512), (4360, 512), (4872, 3072), (2816, 1536), (512, 128), (640, 128), (4352, 8)]
    return p


def _perm_heads_rows(w):
    return jnp.concatenate([w[64 * h:64 * h + 64] for h in HEAD_ORDER], axis=0)


def _unperm_heads_rows(w):
    inv = [HEAD_ORDER.index(h) for h in range(8)]
    return jnp.concatenate([w[64 * s:64 * s + 64] for s in inv], axis=0)


def _split_bf16(a, terms):
    out, rest = [], a.astype(F32)
    for _ in range(terms - 1):
        out.append(rest.astype(BF16))
        rest = rest - out[-1].astype(F32)
    return out + [rest.astype(BF16)]


def _dot(a, b, dims, exact):
    d = lambda p, q: lax.dot_general(p, q, (dims, ((), ())), preferred_element_type=F32)
    if exact:
        (ah, al), (bh, bl) = _split_bf16(a, 2), _split_bf16(b, 2)
        return d(ah, bh) + (d(ah, bl) + d(al, bh))
    return d(a.astype(BF16), b.astype(BF16))


def _make_mm(exact):
    @jax.custom_vjp
    def nn(a, b):
        return _dot(a, b, ((1,), (0,)), exact)

    @jax.custom_vjp
    def nt(a, b):
        return _dot(a, b, ((1,), (1,)), exact)

    @jax.custom_vjp
    def tn(a, b):
        return _dot(a, b, ((0,), (0,)), exact)

    nn.defvjp(lambda a, b: (nn(a, b), (a, b)),
              lambda r, g: (nt(g, r[1]).astype(r[0].dtype), tn(r[0], g).astype(r[1].dtype)))
    nt.defvjp(lambda a, b: (nt(a, b), (a, b)),
              lambda r, g: (nn(g, r[1]).astype(r[0].dtype), tn(g, r[0]).astype(r[1].dtype)))
    tn.defvjp(lambda a, b: (tn(a, b), (a, b)),
              lambda r, g: (nt(r[1], g).astype(r[0].dtype), nn(r[0], g).astype(r[1].dtype)))
    return nn, nt, tn


mm, mm_nt, mm_tn = _make_mm(False)
xmm, xmm_nt, xmm_tn = _make_mm(True)


@jax.custom_vjp
def sel_mm(m, g):
    mb = m.astype(BF16)
    parts = [jnp.dot(mb, p, preferred_element_type=F32) for p in _split_bf16(g, 3)]
    return parts[0] + (parts[1] + parts[2])


def _sel_mm_bwd(m, dy):
    mb = m.astype(BF16)
    parts = [lax.dot_general(mb, p, (((0,), (0,)), ((), ())), preferred_element_type=F32) for p in _split_bf16(dy, 3)]
    return jnp.zeros_like(m), parts[0] + (parts[1] + parts[2])


sel_mm.defvjp(lambda m, g: (sel_mm(m, g), m), _sel_mm_bwd)


@jax.custom_vjp
def tri_inv(*mats):
    n = mats[0].shape[0]
    eye = jnp.where(lax.broadcasted_iota(jnp.int32, (n, n), 0) == lax.broadcasted_iota(jnp.int32, (n, n), 1), 1.0, 0.0)
    ts = [eye - a for a in mats]
    pws = list(mats)
    for _ in range(5):
        pws = [xmm(pw, pw) for pw in pws]
        ts = [t + xmm(t, pw) for t, pw in zip(ts, pws)]
    return tuple(ts)


def _tri_inv_bwd(ts, dts):
    inner = [xmm_nt(dt, t) for t, dt in zip(ts, dts)]
    return tuple(-xmm_tn(t, m) for t, m in zip(ts, inner))


tri_inv.defvjp(lambda *mats: (tri_inv(*mats),) * 2, _tri_inv_bwd)


@jax.custom_vjp
def tri_inv_known(a, t):
    return t


tri_inv_known.defvjp(lambda a, t: (t, t), lambda t, dt: (_tri_inv_bwd((t,), (dt,))[0], jnp.zeros_like(t)))


def _sigmoid(x):
    return 1.0 / (1.0 + jnp.exp(-x))


def _silu(x):
    return x * _sigmoid(x)


def _softplus(x):
    return jnp.maximum(x, 0.0) + jnp.log(1.0 + jnp.exp(-jnp.abs(x)))


def _cparams(n_grid):
    return pltpu.CompilerParams(dimension_semantics=("arbitrary",) * n_grid, vmem_limit_bytes=VMEM_LIMIT)


def _row_spec(tm, width, colblk):
    return pl.BlockSpec((tm, width), lambda i, cb=colblk: (i, cb))


def _const_spec(shape):
    nd = len(shape)
    return pl.BlockSpec(tuple(shape), lambda i, nd=nd: (0,) * nd)


def rowwise_fwd(name, f, rows, consts, outs, tm, carry=None):
    n_r, n_c = len(rows), len(consts)
    t = rows[0][0].shape[0]
    c_ins, c_in_specs, c_outs, c_out_specs, c_scratch = _host(carry)
    n_in, n_ci, n_co = n_r + n_c, len(c_ins), len(c_outs)

    def body(*refs):
        carried = (refs[n_in:n_in + n_ci], refs[n_in + n_ci + len(outs):n_in + n_ci + len(outs) + n_co],
                   *refs[n_in + n_ci + len(outs) + n_co:])
        if carry is not None:
            carry.emit_start(pl.program_id(0) == 0, *carried)
        vals = [r[...] for r in refs[:n_in]]
        res = f(*vals)
        if not isinstance(res, (tuple, list)):
            res = (res,)
        for o_ref, v, out in zip(refs[n_in + n_ci:n_in + n_ci + len(outs)], res, outs):
            o_ref[...] = (v.T if len(out) == 3 else v).astype(o_ref.dtype)
        if carry is not None:
            carry.emit_finish(pl.program_id(0) == t // tm - 1, *carried)

    return pl.pallas_call(
        body, name=name, grid=(t // tm,),
        in_specs=[_row_spec(tm, w, cb) for _, w, cb in rows] + [_const_spec(c.shape) for c in consts] + c_in_specs,
        out_specs=[_row_spec(tm, o[0], 0) if len(o) == 2 else pl.BlockSpec((o[0], tm), lambda i: (0, i)) for o in outs]
        + c_out_specs,
        out_shape=[jax.ShapeDtypeStruct((t, o[0]) if len(o) == 2 else (o[0], t), o[1]) for o in outs] + c_outs,
        input_output_aliases={} if carry is None else carry.aliases(n_in, len(outs)),
        scratch_shapes=c_scratch,
        compiler_params=_cparams(1),
    )(*[a for a, _, _ in rows], *consts, *c_ins)


def rowwise_bwd(name, f, rows, consts, cts, row_grad_dtypes, tm, carry=None):
    n_r, n_c, n_ct = len(rows), len(consts), len(cts)
    t = rows[0][0].shape[0]
    keep = [k for k, dt in enumerate(row_grad_dtypes) if dt is not None]
    c_ins, c_in_specs, c_outs, c_out_specs, c_scratch = _host(carry)
    n_in, n_out = n_r + n_c + n_ct, len(keep) + n_c

    def body(*refs):
        ins = [r[...].astype(F32) for r in refs[:n_r + n_c]]
        g_out = [r[...].astype(F32) for r in refs[n_r + n_c:n_in]]
        out_refs = refs[n_in + len(c_ins):n_in + len(c_ins) + n_out]
        carried = (refs[n_in:n_in + len(c_ins)], refs[n_in + len(c_ins) + n_out:n_in + len(c_ins) + n_out + len(c_outs)],
                   *refs[n_in + len(c_ins) + n_out + len(c_outs):])
        if carry is not None:
            carry.emit_start(pl.program_id(0) == 0, *carried)

        def fw(*a):
            res = f(*a)
            return tuple(res) if isinstance(res, (tuple, list)) else (res,)

        _, vjp = jax.vjp(fw, *ins)
        grads = vjp(tuple(g_out))
        for o_ref, k in zip(out_refs[:len(keep)], keep):
            o_ref[...] = grads[k].astype(o_ref.dtype)
        first = pl.program_id(0) == 0
        for o_ref, g in zip(out_refs[len(keep):], grads[n_r:]):
            @pl.when(first)
            def _(o_ref=o_ref, g=g):
                o_ref[...] = g

            @pl.when(jnp.logical_not(first))
            def _(o_ref=o_ref, g=g):
                o_ref[...] += g
        if carry is not None:
            carry.emit_finish(pl.program_id(0) == t // tm - 1, *carried)

    return pl.pallas_call(
        body, name=name, grid=(t // tm,),
        in_specs=[_row_spec(tm, w, cb) for _, w, cb in rows] + [_const_spec(c.shape) for c in consts]
        + [_row_spec(tm, w, cb) for _, w, cb in cts] + c_in_specs,
        out_specs=[_row_spec(tm, rows[k][1], 0) for k in keep] + [_const_spec(c.shape) for c in consts] + c_out_specs,
        out_shape=[jax.ShapeDtypeStruct((t, rows[k][1]), row_grad_dtypes[k]) for k in keep]
        + [jax.ShapeDtypeStruct(c.shape, F32) for c in consts] + c_outs,
        scratch_shapes=c_scratch,
        compiler_params=_cparams(1),
    )(*[a for a, _, _ in rows], *consts, *[a for a, _, _ in cts], *c_ins)


def f_norm_mod(x, g, scale, shift):
    y = x * lax.rsqrt(jnp.mean(x * x, axis=-1, keepdims=True) + EPS) * g
    return y * (1.0 + scale) + shift


def f_conf_tail(u, zb, ln_g, ln_b, pw2_w, pw2_b):
    mu = jnp.mean(u, axis=-1, keepdims=True)
    xc = u - mu
    var = jnp.mean(xc * xc, axis=-1, keepdims=True)
    y = _silu(xc * lax.rsqrt(var + EPS) * ln_g + ln_b)
    return (mm(y, pw2_w) + pw2_b) * _silu(zb)


def f_merge(ya, yb, yc, mg, x, gate, wpa, wpb, wpc, wout):
    d = D_MODEL
    merged = (_sigmoid(mg[:, :d]) * mm(ya, wpa) + _sigmoid(mg[:, d:2 * d]) * mm(yb, wpb)
              + _sigmoid(mg[:, 2 * d:]) * mm(yc, wpc))
    return x + gate * mm(merged, wout)


def matmul_nn(name, a, b, out_dtype, tm, tn, tk, b_transposed=False, carry=None):
    m, k = a.shape
    n = b.shape[0] if b_transposed else b.shape[1]
    nk = k // tk
    grid = (m // tm, n // tn, nk)
    b_spec = (pl.BlockSpec((tn, tk), lambda i, j, kk: (j, kk)) if b_transposed
              else pl.BlockSpec((tk, tn), lambda i, j, kk: (kk, j)))
    c_ins, c_in_specs, c_outs, c_out_specs, c_scratch = _host(carry)
    n_ci, n_co = len(c_ins), len(c_outs)

    def body(*refs):
        a_ref, b_ref, o_ref = refs[0], refs[1], refs[2 + n_ci]
        carried = (refs[2:2 + n_ci], refs[3 + n_ci:3 + n_ci + n_co], *refs[3 + n_ci + n_co:3 + n_ci + n_co + len(c_scratch)])
        at = lambda step: functools.reduce(jnp.logical_and, [pl.program_id(d) == s for d, s in enumerate(step)])
        if carry is not None:
            carry.emit_start(at((0, 0, 0)), *carried)
        part = lax.dot_general(a_ref[...].astype(BF16), b_ref[...].astype(BF16),
                               (((1,), (1 if b_transposed else 0,)), ((), ())), preferred_element_type=F32)
        if nk == 1:
            o_ref[...] = part.astype(o_ref.dtype)
        else:
            kk = pl.program_id(2)
            acc_ref = refs[-1]

            @pl.when(kk == 0)
            def _():
                acc_ref[...] = part

            @pl.when(kk > 0)
            def _():
                acc_ref[...] += part

            @pl.when(kk == nk - 1)
            def _():
                o_ref[...] = acc_ref[...].astype(o_ref.dtype)
        if carry is not None:
            carry.emit_finish(at(tuple(g - 1 for g in grid)), *carried)

    res = pl.pallas_call(
        body, name=name, grid=grid,
        in_specs=[pl.BlockSpec((tm, tk), lambda i, j, kk: (i, kk)), b_spec] + c_in_specs,
        out_specs=[pl.BlockSpec((tm, tn), lambda i, j, kk: (i, j))] + c_out_specs,
        out_shape=[jax.ShapeDtypeStruct((m, n), out_dtype)] + c_outs,
        input_output_aliases={} if carry is None else carry.aliases(2, 1),
        scratch_shapes=c_scratch + ([] if nk == 1 else [pltpu.VMEM((tm, tn), F32)]),
        compiler_params=_cparams(3),
    )(a, b, *c_ins)
    return res[0] if carry is None else res


def ada_fwd(c8, w_shard, b_ada):
    n_cols = w_shard.shape[2]
    masks = [(m >> 2 & 1, m >> 1 & 1, m & 1) for m in range(1, 8)]

    def body(c_ref, w_ref, b_ref, mod_ref, conds_ref, cbuf, sendbuf, recvbuf, send_sems, recv_sems):
        x, y, c, chips = _place()
        flip = lambda v, bit: 1 - v if bit else v
        peers = [(flip(x, mx), flip(y, my), flip(c, mc)) for mx, my, mc in masks]
        dev = lambda p: 4 * p[0] + 2 * p[1] + p[2]
        cbuf[dev((x, y, c))] = c_ref[...]
        first = [_remote(c_ref, cbuf.at[dev((x, y, c))], send_sems.at[i], recv_sems.at[i], p) for i, p in enumerate(peers)]
        for cp in first:
            cp.start()
        for i, p in enumerate(peers):
            _remote(c_ref, cbuf.at[dev(p)], send_sems.at[i], recv_sems.at[i], p).wait_recv()
        conds = jnp.concatenate([cbuf[d, 0:1, :] for d in range(8)], axis=0)
        conds_ref[...] = conds
        act = _silu(conds)
        parts = [mm(act, w_ref[l]) for l in range(DEPTH)]
        row8 = lax.broadcasted_iota(jnp.int32, (8, 1), 0)

        def tile_for(chip):
            r = 2 * (2 * chip[0] + chip[1]) + c
            rows = [jnp.sum(jnp.where(row8 == r, parts[l], 0.0), axis=0, keepdims=True) for l in range(DEPTH)]
            return jnp.where(row8 == 0, rows[0], jnp.where(row8 == 1, rows[1], 0.0))

        my_slot = 2 * x + y
        recvbuf[my_slot] = tile_for((x, y))
        second = []
        for j, chip in enumerate(chips):
            sendbuf[j] = tile_for(chip)
            second.append(_remote(sendbuf.at[j], recvbuf.at[my_slot], send_sems.at[7 + j], recv_sems.at[7 + j], (*chip, c)))
            second[-1].start()
        for j, chip in enumerate(chips):
            _remote(sendbuf.at[j], recvbuf.at[2 * chip[0] + chip[1]], send_sems.at[7 + j], recv_sems.at[7 + j],
                    (*chip, c)).wait_recv()
        rows = [jnp.concatenate([recvbuf[k, l:l + 1, :] for k in range(N_CHIPS)], axis=1) + b_ref[l:l + 1, :]
                for l in range(DEPTH)]
        mod_ref[...] = jnp.concatenate(rows + [jnp.zeros((8 - DEPTH, N_CHIPS * n_cols), F32)], axis=0)
        for cp in first + second:
            cp.wait_send()

    vm = pl.BlockSpec(memory_space=pltpu.VMEM)
    return pl.pallas_call(
        body, name="ada_fwd",
        out_shape=[jax.ShapeDtypeStruct((8, N_CHIPS * n_cols), F32), jax.ShapeDtypeStruct((8, D_MODEL), F32)],
        in_specs=[vm, vm, vm], out_specs=[vm, vm],
        scratch_shapes=[pltpu.VMEM((8, 8, D_MODEL), F32), pltpu.VMEM((3, 8, n_cols), F32),
                        pltpu.VMEM((N_CHIPS, 8, n_cols), F32), pltpu.SemaphoreType.DMA((10,)), pltpu.SemaphoreType.DMA((10,))],
        compiler_params=pltpu.CompilerParams(vmem_limit_bytes=VMEM_LIMIT),
    )(c8, w_shard, b_ada)


def ada_bwd(conds, dmod):
    def body(c_ref, d_ref, o_ref):
        act = _silu(c_ref[...])
        for l in range(DEPTH):
            o_ref[l] = mm_tn(act, d_ref[l])

    return pl.pallas_call(
        body, name="ada_bwd", out_shape=jax.ShapeDtypeStruct((DEPTH, D_MODEL, dmod.shape[2]), F32),
        compiler_params=pltpu.CompilerParams(vmem_limit_bytes=VMEM_LIMIT),
    )(conds, dmod)


def _f_attn(first_block, q, za, kc, vc, kp, vp, qg, kg, sinks):
    w = WINDOW
    lane = lax.broadcasted_iota(jnp.int32, (1, 128), 1)
    halves = [lane < 64, lane >= 64]

    def rms_halves(x, g):
        x2 = x * x
        s0 = jnp.sum(jnp.where(halves[0], x2, 0.0), axis=-1, keepdims=True)
        s1 = jnp.sum(jnp.where(halves[1], x2, 0.0), axis=-1, keepdims=True)
        r = jnp.where(halves[0], lax.rsqrt(s0 / 64.0 + EPS), lax.rsqrt(s1 / 64.0 + EPS))
        return x * r * g

    kcat = rms_halves(jnp.concatenate([kp, kc], axis=0), kg)
    vcat = jnp.concatenate([vp, vc], axis=0)
    qi = lax.broadcasted_iota(jnp.int32, (w, 2 * w), 0)
    kj = lax.broadcasted_iota(jnp.int32, (w, 2 * w), 1)
    dist = qi + w - kj
    valid = (dist >= 0) & (dist < w) & (jnp.logical_not(first_block) | (kj >= w))
    distf = dist.astype(F32)
    units = [(grp, half) for grp in range(4) for half in range(2)]
    qns = [rms_halves(q[:, 128 * grp:128 * grp + 128], qg) * (ATT_HEAD_DIM ** -0.5) for grp in range(4)]
    vhalf = [jnp.where(halves[half], vcat, 0.0) for half in range(2)]
    scores, sinks_h = [], []
    for grp, half in units:
        head = HEAD_ORDER[2 * grp + half]
        slope = 2.0 ** (-8.0 * (head + 1) / ATT_HEADS)
        sinks_h.append(jnp.sum(jnp.where(lane == head, sinks, 0.0), axis=-1, keepdims=True))
        s = mm_nt(jnp.where(halves[half], qns[grp], 0.0), kcat) - slope * distf
        scores.append(jnp.where(valid, s, NEG_INF))
    probs = []
    for s, sink in zip(scores, sinks_h):
        m = lax.stop_gradient(jnp.maximum(jnp.max(s, axis=-1, keepdims=True), sink))
        p = jnp.exp(s - m)
        probs.append(p / (jnp.sum(p, axis=-1, keepdims=True) + jnp.exp(sink - m)))
    outs = [mm(p, vhalf[half]) for p, (grp, half) in zip(probs, units)]
    return jnp.concatenate([outs[2 * grp] + outs[2 * grp + 1] for grp in range(4)], axis=1) * _silu(za)


def attn_fwd(name, proj, qg, kg, sinks):
    t = proj.shape[0]
    nb = t // WINDOW

    def body(q_ref, za_ref, kc_ref, vc_ref, kp_ref, vp_ref, qg_ref, kg_ref, s_ref, o_ref):
        first = pl.program_id(0) == 0
        o_ref[...] = _f_attn(first, q_ref[...], za_ref[...], kc_ref[...], vc_ref[...], kp_ref[...], vp_ref[...],
                             qg_ref[...], kg_ref[...], s_ref[...])

    cur = lambda cb: (lambda i: (i, cb))
    prev = lambda cb: (lambda i: (jnp.maximum(i - 1, 0), cb))
    return pl.pallas_call(
        body, name=name, grid=(nb,),
        in_specs=[pl.BlockSpec((WINDOW, 512), cur(P_QA // 512)), pl.BlockSpec((WINDOW, 512), cur(P_ZA // 512)),
                  pl.BlockSpec((WINDOW, 128), cur(P_KA // 128)), pl.BlockSpec((WINDOW, 128), cur(P_VA // 128)),
                  pl.BlockSpec((WINDOW, 128), prev(P_KA // 128)), pl.BlockSpec((WINDOW, 128), prev(P_VA // 128)),
                  _const_spec((1, 128)), _const_spec((1, 128)), _const_spec((1, 128))],
        out_specs=pl.BlockSpec((WINDOW, 512), lambda i: (i, 0)),
        out_shape=jax.ShapeDtypeStruct((t, 512), F32),
        compiler_params=_cparams(1),
    )(proj, proj, proj, proj, proj, proj, qg, kg, sinks)


def attn_bwd(name, proj, qg, kg, sinks, dya):
    t = proj.shape[0]
    nb = t // WINDOW

    def body(q_ref, za_ref, kc_ref, vc_ref, kp_ref, vp_ref, qg_ref, kg_ref, s_ref, dy_ref,
             dqz_ref, dkv_ref, dqg_ref, dkg_ref, ds_ref, carry_ref):
        j = pl.program_id(0)
        first = j == nb - 1

        @pl.when(j == 0)
        def _():
            carry_ref[...] = jnp.zeros_like(carry_ref)
            dqg_ref[...] = jnp.zeros_like(dqg_ref)
            dkg_ref[...] = jnp.zeros_like(dkg_ref)
            ds_ref[...] = jnp.zeros_like(ds_ref)

        ins = [r[...] for r in (q_ref, za_ref, kc_ref, vc_ref, kp_ref, vp_ref, qg_ref, kg_ref, s_ref)]
        _, vjp = jax.vjp(functools.partial(_f_attn, first), *ins)
        dq, dza, dkc, dvc, dkp, dvp, dqg, dkg, dsk = vjp(dy_ref[...])
        dqz_ref[:, 0:512] = dq.astype(dqz_ref.dtype)
        dqz_ref[:, 512:1024] = dza.astype(dqz_ref.dtype)
        dkv_ref[:, 0:128] = (dkc + carry_ref[0]).astype(dkv_ref.dtype)
        dkv_ref[:, 128:256] = (dvc + carry_ref[1]).astype(dkv_ref.dtype)
        carry_ref[0] = dkp
        carry_ref[1] = dvp
        dqg_ref[...] += dqg
        dkg_ref[...] += dkg
        ds_ref[...] += dsk

    cur = lambda cb: (lambda j: (nb - 1 - j, cb))
    prev = lambda cb: (lambda j: (jnp.maximum(nb - 2 - j, 0), cb))
    return pl.pallas_call(
        body, name=name, grid=(nb,),
        in_specs=[pl.BlockSpec((WINDOW, 512), cur(P_QA // 512)), pl.BlockSpec((WINDOW, 512), cur(P_ZA // 512)),
                  pl.BlockSpec((WINDOW, 128), cur(P_KA // 128)), pl.BlockSpec((WINDOW, 128), cur(P_VA // 128)),
                  pl.BlockSpec((WINDOW, 128), prev(P_KA // 128)), pl.BlockSpec((WINDOW, 128), prev(P_VA // 128)),
                  _const_spec((1, 128)), _const_spec((1, 128)), _const_spec((1, 128)),
                  pl.BlockSpec((WINDOW, 512), cur(0))],
        out_specs=[pl.BlockSpec((WINDOW, 1024), cur(0)), pl.BlockSpec((WINDOW, 256), cur(0)),
                   _const_spec((1, 128)), _const_spec((1, 128)), _const_spec((1, 128))],
        out_shape=[jax.ShapeDtypeStruct((t, 1024), BF16), jax.ShapeDtypeStruct((t, 256), BF16),
                   jax.ShapeDtypeStruct((1, 128), F32), jax.ShapeDtypeStruct((1, 128), F32),
                   jax.ShapeDtypeStruct((1, 128), F32)],
        scratch_shapes=[pltpu.VMEM((2, WINDOW, 128), F32)],
        compiler_params=_cparams(1),
    )(proj, proj, proj, proj, proj, proj, qg, kg, sinks, dya)


CONV_ROWS = 256


def _conv_taps(src_ref, w_ref, n_taps, base, t):
    for r0 in range(0, t, CONV_ROWS):
        acc = w_ref[0:1, :] * src_ref[pl.ds(r0 + base, CONV_ROWS), :]
        for k in range(1, n_taps):
            acc = acc + w_ref[k:k + 1, :] * src_ref[pl.ds(r0 + base + k, CONV_ROWS), :]
        yield r0, acc


def _conv_wgrad(dy_ref, src_ref, n_taps, base, t, dy_base=0):
    out = []
    for k in range(n_taps):
        acc = jnp.zeros((8, 128), F32)
        for r0 in range(0, t, CONV_ROWS):
            prod = dy_ref[pl.ds(r0 + dy_base, CONV_ROWS), :] * src_ref[pl.ds(r0 + base + k, CONV_ROWS), :]
            acc = acc + jnp.sum(prod.reshape(CONV_ROWS // 8, 8, 128), axis=0)
        out.append(jnp.sum(acc, axis=0, keepdims=True))
    return out


def glu_conv_fwd(name, proj, w32, bias):
    t = proj.shape[0]
    pad = 32

    def body(x_ref, w_ref, b_ref, o_ref, u_ref):
        u_ref[0:pad, :] = jnp.zeros((pad, 128), F32)
        u_ref[pad:pad + t, :] = x_ref[:, 0:128] * _sigmoid(x_ref[:, 128:256])
        for r0, acc in _conv_taps(u_ref, w_ref, CONV_K, pad - (CONV_K - 1), t):
            o_ref[pl.ds(r0, CONV_ROWS), :] = acc + b_ref[...]

    return pl.pallas_call(
        body, name=name, grid=(4,),
        in_specs=[pl.BlockSpec((t, 256), lambda cb: (0, P_GLU // 256 + cb)), pl.BlockSpec((32, 128), lambda cb: (0, cb)),
                  pl.BlockSpec((1, 128), lambda cb: (0, cb))],
        out_specs=pl.BlockSpec((t, 128), lambda cb: (0, cb)),
        out_shape=jax.ShapeDtypeStruct((t, 512), F32),
        scratch_shapes=[pltpu.VMEM((t + pad, 128), F32)],
        compiler_params=_cparams(1),
    )(proj, w32, bias)


def glu_conv_bwd(name, proj, w32, dub):
    t = proj.shape[0]
    pad = 32
    k1 = CONV_K - 1

    def body(x_ref, w_ref, dy_ref, dx_ref, dw_ref, db_ref, u_ref, dyp_ref, wrev_ref):
        val = x_ref[:, 0:128]
        sg = _sigmoid(x_ref[:, 128:256])
        u_ref[0:pad, :] = jnp.zeros((pad, 128), F32)
        u_ref[pad:pad + t, :] = val * sg
        dyp_ref[0:t, :] = dy_ref[...]
        dyp_ref[t:t + pad, :] = jnp.zeros((pad, 128), F32)
        for k in range(CONV_K):
            wrev_ref[k:k + 1, :] = w_ref[k1 - k:k1 - k + 1, :]
        wrev_ref[CONV_K:32, :] = jnp.zeros((32 - CONV_K, 128), F32)
        for r0, du in _conv_taps(dyp_ref, wrev_ref, CONV_K, 0, t):
            v = x_ref[pl.ds(r0, CONV_ROWS), 0:128]
            s = _sigmoid(x_ref[pl.ds(r0, CONV_ROWS), 128:256])
            dx_ref[pl.ds(r0, CONV_ROWS), 0:128] = (du * s).astype(dx_ref.dtype)
            dx_ref[pl.ds(r0, CONV_ROWS), 128:256] = (du * v * s * (1.0 - s)).astype(dx_ref.dtype)
        dws = _conv_wgrad(dyp_ref, u_ref, CONV_K, pad - k1, t)
        for k in range(CONV_K):
            dw_ref[k:k + 1, :] = dws[k]
        dw_ref[CONV_K:32, :] = jnp.zeros((32 - CONV_K, 128), F32)
        db_ref[...] = jnp.sum(dy_ref[...], axis=0, keepdims=True)

    return pl.pallas_call(
        body, name=name, grid=(4,),
        in_specs=[pl.BlockSpec((t, 256), lambda cb: (0, P_GLU // 256 + cb)), pl.BlockSpec((32, 128), lambda cb: (0, cb)),
                  pl.BlockSpec((t, 128), lambda cb: (0, cb))],
        out_specs=[pl.BlockSpec((t, 256), lambda cb: (0, cb)), pl.BlockSpec((32, 128), lambda cb: (0, cb)),
                   pl.BlockSpec((1, 128), lambda cb: (0, cb))],
        out_shape=[jax.ShapeDtypeStruct((t, 1024), BF16), jax.ShapeDtypeStruct((32, 512), F32),
                   jax.ShapeDtypeStruct((1, 512), F32)],
        scratch_shapes=[pltpu.VMEM((t + pad, 128), F32), pltpu.VMEM((t + pad, 128), F32), pltpu.VMEM((32, 128), F32)],
        compiler_params=_cparams(1),
    )(proj, w32, dub)


def sconv_fwd(name, proj, w8):
    t = proj.shape[0]
    pad = 8
    k1 = DN_CONV_K - 1

    def body(x_ref, w_ref, o_ref, xp_ref):
        xp_ref[0:pad, :] = jnp.zeros((pad, 128), F32)
        xp_ref[pad:pad + t, :] = x_ref[...]
        for r0, acc in _conv_taps(xp_ref, w_ref, DN_CONV_K, pad - k1, t):
            o_ref[pl.ds(r0, CONV_ROWS), :] = _silu(acc)

    return pl.pallas_call(
        body, name=name, grid=(12,),
        in_specs=[pl.BlockSpec((t, 128), lambda cb: (0, P_QKV // 128 + cb)), pl.BlockSpec((8, 128), lambda cb: (0, cb))],
        out_specs=pl.BlockSpec((t, 128), lambda cb: (0, cb)),
        out_shape=jax.ShapeDtypeStruct((t, 1536), F32),
        scratch_shapes=[pltpu.VMEM((t + pad, 128), F32)],
        compiler_params=_cparams(1),
    )(proj, w8)


def sconv_bwd(name, proj, w8, dqkv):
    t = proj.shape[0]
    pad = 8
    k1 = DN_CONV_K - 1

    def body(x_ref, w_ref, dy_ref, dx_ref, dw_ref, xp_ref, dpp_ref, wrev_ref):
        xp_ref[0:pad, :] = jnp.zeros((pad, 128), F32)
        xp_ref[pad:pad + t, :] = x_ref[...]
        for r0, pre in _conv_taps(xp_ref, w_ref, DN_CONV_K, pad - k1, t):
            s = _sigmoid(pre)
            dpp_ref[pl.ds(r0, CONV_ROWS), :] = dy_ref[pl.ds(r0, CONV_ROWS), :] * (s * (1.0 + pre * (1.0 - s)))
        dpp_ref[t:t + pad, :] = jnp.zeros((pad, 128), F32)
        for k in range(DN_CONV_K):
            wrev_ref[k:k + 1, :] = w_ref[k1 - k:k1 - k + 1, :]
        wrev_ref[DN_CONV_K:8, :] = jnp.zeros((8 - DN_CONV_K, 128), F32)
        for r0, dx in _conv_taps(dpp_ref, wrev_ref, DN_CONV_K, 0, t):
            dx_ref[pl.ds(r0, CONV_ROWS), :] = dx.astype(dx_ref.dtype)
        dws = _conv_wgrad(dpp_ref, xp_ref, DN_CONV_K, pad - k1, t)
        for k in range(DN_CONV_K):
            dw_ref[k:k + 1, :] = dws[k]
        dw_ref[DN_CONV_K:8, :] = jnp.zeros((8 - DN_CONV_K, 128), F32)

    return pl.pallas_call(
        body, name=name, grid=(12,),
        in_specs=[pl.BlockSpec((t, 128), lambda cb: (0, P_QKV // 128 + cb)), pl.BlockSpec((8, 128), lambda cb: (0, cb)),
                  pl.BlockSpec((t, 128), lambda cb: (0, cb))],
        out_specs=[pl.BlockSpec((t, 128), lambda cb: (0, cb)), pl.BlockSpec((8, 128), lambda cb: (0, cb))],
        out_shape=[jax.ShapeDtypeStruct((t, 1536), BF16), jax.ShapeDtypeStruct((8, 1536), F32)],
        scratch_shapes=[pltpu.VMEM((t + pad, 128), F32), pltpu.VMEM((t + pad, 128), F32), pltpu.VMEM((8, 128), F32)],
        compiler_params=_cparams(1),
    )(proj, w8, dqkv)


def _f_delta_step(qkv, ab, zc, s0, s1, s2, s3, a_log, dt_bias, dn_g, inverses=None, with_inverses=False):
    cs = DN_CHUNK
    n = 2 * cs
    states = (s0, s1, s2, s3)
    lane = lax.broadcasted_iota(jnp.int32, (1, 128), 1)
    ri = lax.broadcasted_iota(jnp.int32, (n, n), 0)
    ci = lax.broadcasted_iota(jnp.int32, (n, n), 1)
    same = (ri // cs) == (ci // cs)
    lower = same & (ri >= ci)
    strict = same & (ri > ci)
    sums = jnp.concatenate([jnp.where(lower, 1.0, 0.0), jnp.where(same, 1.0, 0.0), jnp.where(ci < cs, 1.0, 0.0),
                            jnp.where(ci >= cs, 1.0, 0.0)], axis=0)
    top = lax.broadcasted_iota(jnp.int32, (n, 1), 0) < cs

    def pick(row, idx):
        return jnp.sum(jnp.where(lane == idx, row, 0.0), axis=-1, keepdims=True)

    def l2n(x):
        return x * lax.rsqrt(jnp.sum(x * x, axis=-1, keepdims=True) + EPS)

    n_chunks = qkv.shape[0] // cs
    units = [(k, pair) for k in range(n_chunks) for pair in range(2)]

    pre = []
    for k, pair in units:
        hs = (2 * pair, 2 * pair + 1)
        rows = slice(k * cs, (k + 1) * cs)
        stack = lambda f: jnp.concatenate([f(hs[0]), f(hs[1])], axis=0)
        qd = l2n(stack(lambda h: qkv[rows, 128 * h:128 * h + 128])) * (128 ** -0.5)
        kd = l2n(stack(lambda h: qkv[rows, 512 + 128 * h:512 + 128 * h + 128]))
        vd = stack(lambda h: qkv[rows, 1024 + 128 * h:1024 + 128 * h + 128])
        beta = _sigmoid(stack(lambda h: pick(ab[rows], 4 + h)))
        g = stack(lambda h: -jnp.exp(pick(a_log, h)) * _softplus(pick(ab[rows], h) + pick(dt_bias, h)))
        g_sums = sel_mm(sums, g * jnp.ones((1, n), F32))
        gc_col = g_sums[0:n]
        gl_b = g_sums[n:2 * n]
        g_end = (g_sums[2 * n:3 * n], g_sums[3 * n:])
        decay = jnp.where(lower, jnp.exp(jnp.where(lower, gc_col - gc_col.T, 0.0)), 0.0)
        kb = kd * beta
        pre.append(dict(qd=qd, kd=kd, vb=vd * beta, kb=kb, gc_col=gc_col, gl_b=gl_b, g_end=g_end, decay=decay,
                        a=jnp.where(strict, mm_nt(kb, kd) * decay, 0.0)))
    if inverses is None:
        tmats = tri_inv(*[p["a"] for p in pre])
    else:
        tmats = [tri_inv_known(p["a"], t) for p, t in zip(pre, inverses)]

    mid = []
    for p, tmat in zip(pre, tmats):
        egc = jnp.exp(p["gc_col"])
        mid.append(dict(u=mm(tmat, p["vb"]), wm=mm(tmat, p["kb"] * egc), qe=p["qd"] * egc,
                        intra=jnp.where(lower, mm_nt(p["qd"], p["kd"]) * p["decay"], 0.0),
                        ke=p["kd"] * jnp.exp(p["gl_b"] - p["gc_col"]), g_end=p["g_end"]))

    ys = []
    for k in range(n_chunks):
        rows = slice(k * cs, (k + 1) * cs)
        new_states, y_heads = [], []
        for pair in range(2):
            m = mid[2 * k + pair]
            hs = (2 * pair, 2 * pair + 1)
            st = (states[hs[0]], states[hs[1]])
            v_new = m["u"] - jnp.concatenate([mm(m["wm"][:cs], st[0]), mm(m["wm"][cs:], st[1])], axis=0)
            o = jnp.concatenate([mm(m["qe"][:cs], st[0]), mm(m["qe"][cs:], st[1])], axis=0) + mm(m["intra"], v_new)
            new_states.append(st[0] * jnp.exp(m["g_end"][0]) + mm_tn(jnp.where(top, m["ke"], 0.0), v_new))
            new_states.append(st[1] * jnp.exp(m["g_end"][1]) + mm_tn(jnp.where(top, 0.0, m["ke"]), v_new))
            od = o * lax.rsqrt(jnp.mean(o * o, axis=-1, keepdims=True) + EPS) * dn_g
            y_heads += [od[:cs] * _silu(zc[rows, 128 * hs[0]:128 * hs[0] + 128]),
                        od[cs:] * _silu(zc[rows, 128 * hs[1]:128 * hs[1] + 128])]
        states = tuple(new_states)
        ys.append(jnp.concatenate(y_heads, axis=1))
    if with_inverses:
        return (jnp.concatenate(ys, axis=0), *states), tmats
    return (jnp.concatenate(ys, axis=0), *states)


DELTA_ROWS = 4 * DN_CHUNK
DELTA_UNITS = 2 * DELTA_ROWS // DN_CHUNK


def delta_fwd(name, qkv, proj, a_log, dt_bias, dn_g):
    t = qkv.shape[0]
    nc = t // DELTA_ROWS

    def body(qkv_ref, ab_ref, zc_ref, al_ref, dt_ref, g_ref, y_ref, ssave_ref, tsave_ref, s_ref):
        @pl.when(pl.program_id(0) == 0)
        def _():
            s_ref[...] = jnp.zeros_like(s_ref)

        ssave_ref[0] = s_ref[...]
        st = [s_ref[128 * h:128 * h + 128, :] for h in range(4)]
        (y, *ns), tmats = _f_delta_step(qkv_ref[...], ab_ref[...], zc_ref[...], *st, al_ref[...], dt_ref[...], g_ref[...],
                                        with_inverses=True)
        y_ref[...] = y
        for h in range(4):
            s_ref[128 * h:128 * h + 128, :] = ns[h]
        for u, tm in enumerate(tmats):
            tsave_ref[0, 128 * u:128 * u + 128, :] = tm

    return pl.pallas_call(
        body, name=name, grid=(nc,),
        in_specs=[pl.BlockSpec((DELTA_ROWS, 1536), lambda i: (i, 0)), pl.BlockSpec((DELTA_ROWS, 128), lambda i: (i, P_AB // 128)),
                  pl.BlockSpec((DELTA_ROWS, 512), lambda i: (i, P_ZC // 512)),
                  _const_spec((1, 128)), _const_spec((1, 128)), _const_spec((1, 128))],
        out_specs=[pl.BlockSpec((DELTA_ROWS, 512), lambda i: (i, 0)), pl.BlockSpec((1, 512, 128), lambda i: (i, 0, 0)),
                   pl.BlockSpec((1, DELTA_UNITS * 128, 128), lambda i: (i, 0, 0))],
        out_shape=[jax.ShapeDtypeStruct((t, 512), F32), jax.ShapeDtypeStruct((nc, 512, 128), F32),
                   jax.ShapeDtypeStruct((nc, DELTA_UNITS * 128, 128), F32)],
        scratch_shapes=[pltpu.VMEM((512, 128), F32)],
        compiler_params=_cparams(1),
    )(qkv, proj, proj, a_log, dt_bias, dn_g)


def delta_bwd(name, qkv, proj, ssave, tsave, a_log, dt_bias, dn_g, dyc, carry=None):
    t = qkv.shape[0]
    nc = t // DELTA_ROWS
    c_ins, c_in_specs, c_outs, c_out_specs, c_scratch = _host(carry)
    n_ci, n_co = len(c_ins), len(c_outs)

    def body(*refs):
        qkv_ref, ab_ref, zc_ref, ss_ref, ts_ref, al_ref, dt_ref, g_ref, dy_ref = refs[:9]
        dqkv_ref, dab_ref, dzc_ref, dal_ref, ddt_ref, dg_ref = refs[9 + n_ci:15 + n_ci]
        ds_ref = refs[15 + n_ci + n_co]
        carried = (refs[9:9 + n_ci], refs[15 + n_ci:15 + n_ci + n_co], *refs[16 + n_ci + n_co:])

        @pl.when(pl.program_id(0) == 0)
        def _():
            ds_ref[...] = jnp.zeros_like(ds_ref)
            dal_ref[...] = jnp.zeros_like(dal_ref)
            ddt_ref[...] = jnp.zeros_like(ddt_ref)
            dg_ref[...] = jnp.zeros_like(dg_ref)

        if carry is not None:
            carry.emit_start(pl.program_id(0) == 0, *carried)

        st = [ss_ref[0, 128 * h:128 * h + 128, :] for h in range(4)]
        known = [ts_ref[0, 128 * u:128 * u + 128, :] for u in range(DELTA_UNITS)]
        _, vjp = jax.vjp(functools.partial(_f_delta_step, inverses=known), qkv_ref[...], ab_ref[...], zc_ref[...], *st,
                         al_ref[...], dt_ref[...], g_ref[...])
        dst = tuple(ds_ref[128 * h:128 * h + 128, :] for h in range(4))
        dqkv, dab, dzc, d0, d1, d2, d3, dal, ddt, dg = vjp((dy_ref[...], *dst))
        dqkv_ref[...] = dqkv
        dab_ref[...] = dab.astype(dab_ref.dtype)
        dzc_ref[...] = dzc.astype(dzc_ref.dtype)
        for h, d in enumerate((d0, d1, d2, d3)):
            ds_ref[128 * h:128 * h + 128, :] = d
        dal_ref[...] += dal
        ddt_ref[...] += ddt
        dg_ref[...] += dg

        if carry is not None:
            carry.emit_finish(pl.program_id(0) == nc - 1, *carried)

    rev = lambda cb: (lambda j: (nc - 1 - j, cb))
    return pl.pallas_call(
        body, name=name, grid=(nc,),
        in_specs=[pl.BlockSpec((DELTA_ROWS, 1536), rev(0)), pl.BlockSpec((DELTA_ROWS, 128), rev(P_AB // 128)),
                  pl.BlockSpec((DELTA_ROWS, 512), rev(P_ZC // 512)), pl.BlockSpec((1, 512, 128), lambda j: (nc - 1 - j, 0, 0)),
                  pl.BlockSpec((1, DELTA_UNITS * 128, 128), lambda j: (nc - 1 - j, 0, 0)),
                  _const_spec((1, 128)), _const_spec((1, 128)), _const_spec((1, 128)),
                  pl.BlockSpec((DELTA_ROWS, 512), rev(0))] + c_in_specs,
        out_specs=[pl.BlockSpec((DELTA_ROWS, 1536), rev(0)), pl.BlockSpec((DELTA_ROWS, 128), rev(0)),
                   pl.BlockSpec((DELTA_ROWS, 512), rev(0)),
                   _const_spec((1, 128)), _const_spec((1, 128)), _const_spec((1, 128))] + c_out_specs,
        out_shape=[jax.ShapeDtypeStruct((t, 1536), F32), jax.ShapeDtypeStruct((t, 128), BF16),
                   jax.ShapeDtypeStruct((t, 512), BF16),
                   jax.ShapeDtypeStruct((1, 128), F32), jax.ShapeDtypeStruct((1, 128), F32), jax.ShapeDtypeStruct((1, 128), F32)]
        + c_outs,
        scratch_shapes=[pltpu.VMEM((512, 128), F32)] + c_scratch,
        compiler_params=_cparams(1),
    )(qkv, proj, proj, ssave, tsave, a_log, dt_bias, dn_g, dyc, *c_ins)


def loss_head(name, y, target, tm):
    t, d = y.shape

    def body(y_ref, t_ref, dy_ref, l_ref):
        err = y_ref[...] - t_ref[...]
        dy_ref[...] = err * (1.0 / d)
        part = 0.5 * jnp.sum(jnp.sum(err * err, axis=-1, keepdims=True) * (1.0 / d), axis=0, keepdims=True)

        @pl.when(pl.program_id(0) == 0)
        def _():
            l_ref[...] = part

        @pl.when(pl.program_id(0) > 0)
        def _():
            l_ref[...] += part

    return pl.pallas_call(
        body, name=name, grid=(t // tm,),
        in_specs=[_row_spec(tm, d, 0), _row_spec(tm, d, 0)],
        out_specs=[_row_spec(tm, d, 0), _const_spec((1, 1))],
        out_shape=[jax.ShapeDtypeStruct((t, d), F32), jax.ShapeDtypeStruct((1, 1), F32)],
        compiler_params=_cparams(1),
    )(y, target)


TM = 1024
TM_MERGE = 256
TM_IN = 1024
TN_IN = 1152


def _lane_pad(v, n=128):
    return jnp.pad(v.astype(F32), (0, n - v.shape[0]))[None, :]


def f_norm_mod_res(x, g, scale, shift):
    return f_norm_mod(x, g, scale, shift), x


def prep_layer(w):
    p = dict(w)
    p["wp"] = _w_in_assemble(w["w_in"])
    p["wpa"] = _perm_heads_rows(w["w_proj_a"])
    p["dw32"] = jnp.pad(w["dw_w"], ((0, 32 - CONV_K), (0, 0)))
    p["sconv8"] = jnp.pad(w["sconv_w"], ((0, 8 - DN_CONV_K), (0, 0)))
    p["qg"] = jnp.tile(w["q_norm_g"], 2)[None, :]
    p["kg"] = jnp.tile(w["k_norm_g"], 2)[None, :]
    p["sinks128"] = _lane_pad(w["sinks"])
    p["al"] = _lane_pad(w["a_log"])
    p["dtb"] = _lane_pad(w["dt_bias"])
    p["dng"] = w["dn_norm_g"][None, :]
    return p


def layer_fwd(tag, x, mod, p, carry_inproj=None, carry_merge=None):
    d = D_MODEL
    shift, scale, gate = mod[:, :d], mod[:, d:2 * d], mod[:, 2 * d:]
    g = p["norm_g"][None, :]
    h, h_t = rowwise_fwd(f"norm_fwd{tag}", lambda *a: (f_norm_mod(*a),) * 2, [(x, d, 0)], [g, scale, shift],
                         [(d, BF16), (d, BF16, "transposed")], TM)
    proj = matmul_nn(f"inproj_fwd{tag}", h, p["wp"], F32, TM_IN, TN_IN, d, carry=carry_inproj)
    proj, got_inproj = (proj, []) if carry_inproj is None else (proj[0], proj[1:])
    ya = attn_fwd(f"attn_fwd{tag}", proj, p["qg"], p["kg"], p["sinks128"])
    ub = glu_conv_fwd(f"glu_conv_fwd{tag}", proj, p["dw32"], p["dw_b"][None, :])
    conf_consts = [p["ln_g"][None, :], p["ln_b"][None, :], p["pw2_w"], p["pw2_b"][None, :]]
    (yb,) = rowwise_fwd(f"conf_fwd{tag}", f_conf_tail, [(ub, 512, 0), (proj, 512, P_ZB // 512)], conf_consts, [(512, F32)], TM)
    qkv = sconv_fwd(f"sconv_fwd{tag}", proj, p["sconv8"])
    yc, ssave, tsave = delta_fwd(f"delta_fwd{tag}", qkv, proj, p["al"], p["dtb"], p["dng"])
    merge_consts = [gate, p["wpa"], p["w_proj_b"], p["w_proj_c"], p["w_out"]]
    merge_rows = [(ya, 512, 0), (yb, 512, 0), (yc, 512, 0), (proj, 3 * d, P_MG // (3 * d)), (x, d, 0)]
    xn, *got_merge = rowwise_fwd(f"merge_fwd{tag}", f_merge, merge_rows, merge_consts, [(d, F32)], TM_MERGE, carry=carry_merge)
    saved = dict(x=x, h_t=h_t, proj=proj, ub=ub, qkv=qkv, ssave=ssave, tsave=tsave, norm_consts=[g, scale, shift],
                 conf_consts=conf_consts, merge_consts=merge_consts, merge_rows=merge_rows)
    return xn, saved, got_inproj, got_merge


def layer_bwd(tag, dxn, p, s, carry_merge=None, carry_delta=None, carry_dh=None):
    d = D_MODEL
    proj = s["proj"]
    merge_no_residual = lambda ya, yb, yc, mg, *consts: f_merge(ya, yb, yc, mg, 0.0, *consts)
    dya, dyb, dyc, dmg, dgate, dwpa, dwpb, dwpc, dwout, *got_merge = rowwise_bwd(
        f"merge_bwd{tag}", merge_no_residual, s["merge_rows"][:4], s["merge_consts"], [(dxn, d, 0)], [F32, F32, F32, BF16],
        TM_MERGE, carry=carry_merge)
    carry_delta = None if carry_delta is None else carry_delta(got_merge)
    dqz, dkv, dqg, dkg, dsinks = attn_bwd(f"attn_bwd{tag}", proj, p["qg"], p["kg"], p["sinks128"], dya)
    dub, dzb, dln_g, dln_b, dpw2_w, dpw2_b = rowwise_bwd(
        f"conf_bwd{tag}", f_conf_tail, [(s["ub"], 512, 0), (proj, 512, P_ZB // 512)], s["conf_consts"], [(dyb, 512, 0)],
        [F32, BF16], TM)
    dglu, ddw32, ddw_b = glu_conv_bwd(f"glu_conv_bwd{tag}", proj, p["dw32"], dub)
    dqkv, dab, dzc, dal, ddtb, ddng, *got_delta = delta_bwd(f"delta_bwd{tag}", s["qkv"], proj, s["ssave"], s["tsave"], p["al"],
                                                            p["dtb"], p["dng"], dyc, carry_delta)
    dqkv_pre, dsconv8 = sconv_bwd(f"sconv_bwd{tag}", proj, p["sconv8"], dqkv)
    dproj = jnp.concatenate([dqz, dglu, dzb, dzc, dmg, dqkv_pre, dkv, dab], axis=1)
    dwp = matmul_nn(f"inproj_bwd_dw{tag}", s["h_t"], dproj, F32, d, TN_IN, 2048)
    reduced = dict(w_in=_w_in_grad_blocks(dwp), pw2_w=dpw2_w, w_proj_a=_unperm_heads_rows(dwpa), w_proj_b=dwpb, w_proj_c=dwpc,
                   w_out=dwout)
    carry_dh = None if carry_dh is None else carry_dh(reduced)
    dh = matmul_nn(f"inproj_bwd_dh{tag}", dproj, p["wp"], F32, TM_IN, d, P_TOTAL // 3, b_transposed=True, carry=carry_dh)
    dh, got_dh = (dh, []) if carry_dh is None else (dh[0], dh[1:])
    dx, dnorm_g, dscale, dshift = rowwise_bwd(
        f"norm_bwd{tag}", f_norm_mod_res, [(s["x"], d, 0)], s["norm_consts"], [(dh, d, 0), (dxn, d, 0)], [F32], TM)
    dmod = jnp.concatenate([dshift, dscale, dgate], axis=1)
    grads = dict(
        reduced, b_ada=dmod[0], norm_g=dnorm_g[0],
        q_norm_g=dqg[0, :64] + dqg[0, 64:], k_norm_g=dkg[0, :64] + dkg[0, 64:], sinks=dsinks[0, :ATT_HEADS],
        dw_w=ddw32[:CONV_K], dw_b=ddw_b[0], ln_g=dln_g[0], ln_b=dln_b[0], pw2_b=dpw2_b[0],
        sconv_w=dsconv8[:DN_CONV_K], a_log=dal[0, :DN_HEADS], dt_bias=ddtb[0, :DN_HEADS], dn_norm_g=ddng[0])
    return dx, grads, got_merge, got_delta, got_dh


SHARDED = {"w_ada": 2, "w_in": 2, "dw_w": 2, "pw2_w": 1, "sconv_w": 2, "w_proj_a": 2, "w_proj_b": 2, "w_proj_c": 2,
           "w_out": 1}
GATHERED = tuple(n for n in SHARDED if n != "w_ada")
GATHER_F32 = ("dw_w", "sconv_w")
REDUCE_BIG = tuple(n for n in GATHERED if n not in GATHER_F32)
SMALL = ("b_ada", "norm_g", "q_norm_g", "k_norm_g", "sinks", "dw_b", "ln_g", "ln_b", "pw2_b", "a_log", "dt_bias",
         "dn_norm_g")
SMALL_ROWS = 104
SMALL_GRAD_ROWS = 448
W_IN_SHARD = D_IN // N_CHIPS
SUM_PARTS = 4


def _w_in_orig():
    orig = np.full(P_TOTAL, -1, np.int64)
    p = 0
    for s, n in _in_pieces():
        orig[p:p + n] = np.arange(s, s + n)
        p += n
    return orig


def _w_in_blocks(k):
    orig = _w_in_orig().reshape(-1, 128)
    lo, hi = k * W_IN_SHARD, (k + 1) * W_IN_SHARD
    return [b for b in range(orig.shape[0]) if np.any((orig[b] >= lo) & (orig[b] < hi))]


W_IN_BLOCKS = max(len(_w_in_blocks(k)) for k in range(N_CHIPS))


def _runs(idx):
    out, i = [], 0
    while i < len(idx):
        j = i + 1
        while j < len(idx) and ((idx[i] < 0 and idx[j] < 0) or (idx[i] >= 0 and idx[j] == idx[j - 1] + 1)):
            j += 1
        out.append((int(idx[i]) if idx[i] >= 0 else -1, j - i))
        i = j
    return out


def _take(a, idx):
    parts = [jnp.zeros(a.shape[:-1] + (n,), a.dtype) if s < 0 else a[..., s:s + n] for s, n in _runs(idx)]
    return parts[0] if len(parts) == 1 else jnp.concatenate(parts, axis=-1)


def _w_in_send(k, shard):
    orig = _w_in_orig().reshape(-1, 128)
    lo, hi = k * W_IN_SHARD, (k + 1) * W_IN_SHARD
    idx = np.concatenate([np.where((orig[b] >= lo) & (orig[b] < hi), orig[b] - lo, -1) for b in _w_in_blocks(k)])
    idx = np.concatenate([idx, np.full((W_IN_BLOCKS - len(_w_in_blocks(k))) * 128, -1)])
    return _take(shard, idx)


def _w_in_assemble(blocks):
    where = [{b: i for i, b in enumerate(_w_in_blocks(k))} for k in range(N_CHIPS)]
    n_blocks = P_TOTAL // 128
    owners = [[(k, where[k][b]) for k in range(N_CHIPS) if b in where[k]] for b in range(n_blocks)]
    parts, b = [], 0
    while b < n_blocks:
        if len(owners[b]) == 1:
            k, pos = owners[b][0]
            e = b + 1
            while e < n_blocks and owners[e] == [(k, pos + e - b)]:
                e += 1
            parts.append(blocks[k][:, pos * 128:(pos + e - b) * 128])
            b = e
        else:
            parts.append(functools.reduce(jnp.add, [blocks[k][:, pos * 128:(pos + 1) * 128] for k, pos in owners[b]]))
            b += 1
    return jnp.concatenate(parts, axis=1)


def _w_in_grad_blocks(wp):
    out = []
    for k in range(N_CHIPS):
        idx = np.concatenate([np.arange(128 * b, 128 * b + 128) for b in _w_in_blocks(k)])
        idx = np.concatenate([idx, np.full((W_IN_BLOCKS - len(_w_in_blocks(k))) * 128, -1)])
        out.append(_take(wp, idx))
    return jnp.stack(out)


def _w_in_receive_grad(k, blocks):
    orig = _w_in_orig()
    inv = np.zeros(D_IN, np.int64)
    inv[orig[orig >= 0]] = np.nonzero(orig >= 0)[0]
    where = {b: i for i, b in enumerate(_w_in_blocks(k))}
    cols = inv[k * W_IN_SHARD:(k + 1) * W_IN_SHARD]
    return _take(blocks, np.array([where[c // 128] * 128 + c % 128 for c in cols]))


def _join_layer(v, axis):
    if axis == 2:
        return jnp.transpose(v, (1, 0, 2)).reshape(v.shape[1], N_CHIPS * v.shape[2])
    return v.reshape(N_CHIPS * v.shape[1], v.shape[2])


def _split_layer(v, axis):
    a, b = v.shape
    if axis == 2:
        return jnp.transpose(v.reshape(a, N_CHIPS, b // N_CHIPS), (1, 0, 2))
    return v.reshape(N_CHIPS, a // N_CHIPS, b)


def pack_small(vals, names, rows):
    flat = jnp.concatenate([vals[n].astype(F32).reshape(-1) for n in names])
    return jnp.pad(flat, (0, rows * 128 - flat.shape[0])).reshape(rows, 128)


def unpack_small(packed, names, shapes):
    flat = packed.reshape(-1)
    out, off = {}, 0
    for n in names:
        k = int(np.prod(shapes[n]))
        out[n] = flat[off:off + k].reshape(shapes[n])
        off += k
    return out


ANY = pl.BlockSpec(memory_space=pl.ANY)


def _place():
    x, y, c = lax.axis_index("x"), lax.axis_index("y"), lax.axis_index("c")
    chips = [(1 - x, y), (x, 1 - y), (1 - x, 1 - y)]
    return x, y, c, chips


def _remote(src, dst, send_sem, recv_sem, to):
    return pltpu.make_async_remote_copy(src_ref=src, dst_ref=dst, send_sem=send_sem, recv_sem=recv_sem, device_id=to,
                                        device_id_type=MESH)


class Carry:
    def __init__(self, ins, out_shapes, sems, start, finish, in_place=False):
        self.ins, self.out_shapes, self.sems, self.start, self.finish, self.in_place = (
            list(ins), list(out_shapes), sems, start, finish, in_place)

    def scratch(self):
        return [pltpu.SemaphoreType.DMA(self.sems), pltpu.SemaphoreType.DMA(self.sems)]

    def aliases(self, first_in, first_out):
        return {first_in + i: first_out + i for i in range(len(self.ins))} if self.in_place else {}

    def emit_start(self, first, in_refs, out_refs, send_sems, recv_sems):
        @pl.when(first)
        def _():
            self.start(in_refs, out_refs, send_sems, recv_sems)

    def emit_finish(self, last, in_refs, out_refs, send_sems, recv_sems):
        @pl.when(last)
        def _():
            self.finish(in_refs, out_refs, send_sems, recv_sems)


def _host(carry):
    if carry is None:
        return [], [], [], [], []
    return carry.ins, [ANY] * len(carry.ins), carry.out_shapes, [ANY] * len(carry.out_shapes), carry.scratch()


def run_carry(name, carry):
    n_in, n_out = len(carry.ins), len(carry.out_shapes)

    def body(*refs):
        ins, outs, sems = refs[:n_in], refs[n_in:n_in + n_out], refs[n_in + n_out:]
        carry.start(ins, outs, *sems)
        carry.finish(ins, outs, *sems)

    return pl.pallas_call(
        body, name=name, out_shape=carry.out_shapes, in_specs=[ANY] * n_in, out_specs=[ANY] * n_out,
        input_output_aliases=carry.aliases(0, 0), scratch_shapes=carry.scratch(),
    )(*carry.ins)


def carry_allgather(layer, slots):
    n = len(slots)

    def copies(out, send_sems, recv_sems, only_ici_out=False):
        x, y, c, chips = _place()
        ici_out, ici_in, d2d_out, d2d_in = [], [], [], []
        for j, chip in enumerate(chips):
            for t in range(n):
                mine, land = out[t].at[2 * x + y], out[t].at[2 * chip[0] + chip[1]]
                ici_out.append(_remote(mine, mine, send_sems.at[t, j], recv_sems.at[t, j], (*chip, layer)))
                if only_ici_out:
                    continue
                ici_in.append(_remote(land, land, send_sems.at[t, j], recv_sems.at[t, j], (*chip, layer)))
                d2d_out.append(_remote(land, land, send_sems.at[t, 3 + j], recv_sems.at[t, 3 + j], (x, y, 1 - layer)))
                d2d_in.append(_remote(land, land, send_sems.at[t, 3 + j], recv_sems.at[t, 3 + j], (x, y, layer)))
        return c, ici_out, ici_in, d2d_out, d2d_in

    def start(ins, out, send_sems, recv_sems):
        c, ici_out, _, _, _ = copies(out, send_sems, recv_sems, only_ici_out=True)

        @pl.when(c == layer)
        def _():
            for cp in ici_out:
                cp.start()

    def finish(ins, out, send_sems, recv_sems):
        c, ici_out, ici_in, d2d_out, d2d_in = copies(out, send_sems, recv_sems)

        @pl.when(c == layer)
        def _():
            for arrived, onward in zip(ici_in, d2d_out):
                arrived.wait_recv()
                onward.start()
            for cp in ici_out + d2d_out:
                cp.wait_send()

        @pl.when(c != layer)
        def _():
            for cp in d2d_in:
                cp.wait_recv()

    return Carry(slots, [jax.ShapeDtypeStruct(s.shape, s.dtype) for s in slots], (n, 6), start, finish, in_place=True)


def carry_pair_send(layer, gs):
    def copies(g, recv, send_sems, recv_sems):
        x, y, c, _ = _place()
        return c, [_remote(g[t], recv[t], send_sems.at[t], recv_sems.at[t], (x, y, 1 - c)) for t in range(len(gs))]

    def start(g, recv, send_sems, recv_sems):
        c, cps = copies(g, recv, send_sems, recv_sems)

        @pl.when(c != layer)
        def _():
            for cp in cps:
                cp.start()

    def finish(g, recv, send_sems, recv_sems):
        c, cps = copies(g, recv, send_sems, recv_sems)

        @pl.when(c != layer)
        def _():
            for cp in cps:
                cp.wait_send()

        @pl.when(c == layer)
        def _():
            for cp in cps:
                cp.wait_recv()

    return Carry(gs, [jax.ShapeDtypeStruct(g.shape, g.dtype) for g in gs], (len(gs),), start, finish)


def grads_pair_sums(layer, gs, recv):
    n = len(gs)

    def body(*refs):
        for t in range(n):
            refs[2 * n + t][...] = (refs[t][...] + refs[n + t][...]).astype(BF16)

    specs = [pl.BlockSpec((None, g.shape[1] // SUM_PARTS, g.shape[2]), lambda s, i: (s, i, 0)) for g in gs]
    return pl.pallas_call(
        body, name=f"grads_pair_sums{layer}", grid=(N_CHIPS, SUM_PARTS), in_specs=specs + specs, out_specs=specs,
        out_shape=[jax.ShapeDtypeStruct(g.shape, BF16) for g in gs], compiler_params=_cparams(2),
    )(*gs, *recv)


def carry_chip_exchange(layer, ps):
    def copies(p, recv, send_sems, recv_sems):
        _, _, c, chips = _place()
        return c, [_remote(p[t].at[2 * chip[0] + chip[1]], recv[t].at[j], send_sems.at[t, j], recv_sems.at[t, j],
                           (*chip, layer)) for j, chip in enumerate(chips) for t in range(len(ps))]

    def start(p, recv, send_sems, recv_sems):
        c, cps = copies(p, recv, send_sems, recv_sems)

        @pl.when(c == layer)
        def _():
            for cp in cps:
                cp.start()

    def finish(p, recv, send_sems, recv_sems):
        c, cps = copies(p, recv, send_sems, recv_sems)

        @pl.when(c == layer)
        def _():
            for cp in cps:
                cp.wait()

    return Carry(ps, [jax.ShapeDtypeStruct((3,) + p.shape[1:], p.dtype) for p in ps], (len(ps), 3), start, finish)


def grads_chip_sums(layer, gs, recv, recv2, into=None):
    n = len(gs)
    my_slot = lambda: 2 * lax.axis_index("x") + lax.axis_index("y")

    def body(*refs):
        outs = refs[-n:]
        for t in range(n):
            r2 = refs[2 * n + t]
            own = refs[t][...] + refs[n + t][...]
            outs[t][...] = ((own + r2[0].astype(F32)) + r2[1].astype(F32)) + r2[2].astype(F32)

    part = lambda g: g.shape[1] // SUM_PARTS
    own_specs = [pl.BlockSpec((None, part(g), g.shape[2]), lambda i: (my_slot(), i, 0)) for g in gs]
    return pl.pallas_call(
        body, name=f"grads_chip_sums{layer}", grid=(SUM_PARTS,),
        in_specs=own_specs + own_specs + [pl.BlockSpec((3, part(g), g.shape[2]), lambda i: (0, i, 0)) for g in gs]
        + ([] if into is None else [ANY] * n),
        out_specs=[pl.BlockSpec((None, part(g), g.shape[2]), lambda i: (layer, i, 0)) for g in gs],
        out_shape=[jax.ShapeDtypeStruct((DEPTH,) + g.shape[1:], F32) for g in gs],
        input_output_aliases={} if into is None else {3 * n + t: t for t in range(n)},
        compiler_params=_cparams(1),
    )(*gs, *recv, *recv2, *([] if into is None else into))


def grads_pair_gather(reds):
    n = len(reds)

    def body(*refs):
        buf = refs[n:2 * n]
        send_sems, recv_sems = refs[2 * n:]
        x, y, c, _ = _place()
        sibling = (x, y, 1 - c)
        cps = [_remote(buf[t].at[c], buf[t].at[c], send_sems.at[t], recv_sems.at[t], sibling) for t in range(n)]
        for cp in cps:
            cp.start()
        for t in range(n):
            _remote(buf[t].at[c], buf[t].at[1 - c], send_sems.at[t], recv_sems.at[t], sibling).wait_recv()
        for cp in cps:
            cp.wait_send()

    return pl.pallas_call(
        body, name="grads_pair_gather", out_shape=[jax.ShapeDtypeStruct(r.shape, r.dtype) for r in reds],
        in_specs=[ANY] * n, out_specs=[ANY] * n, input_output_aliases={t: t for t in range(n)},
        scratch_shapes=[pltpu.SemaphoreType.DMA((n,)), pltpu.SemaphoreType.DMA((n,))],
    )(*reds)


def small_allreduce(v):
    m, n = v.shape

    def body(x_ref, sum_ref, all_ref, send_sems, recv_sems, local_sem):
        x, y, c, chips = _place()
        me, sibling = (x, y, c), (x, y, 1 - c)

        def rows(px, py, pc):
            return all_ref.at[pl.ds((4 * px + 2 * py + pc) * m, m), :]

        def copy(k, block, to, src=None):
            return pltpu.make_async_remote_copy(src_ref=rows(*block) if src is None else src, dst_ref=rows(*block),
                                                send_sem=send_sems.at[k], recv_sem=recv_sems.at[k],
                                                device_id=to, device_id_type=MESH)

        mine = pltpu.make_async_copy(x_ref, rows(*me), local_sem)
        mine.start()
        first = [copy(0, me, sibling, src=x_ref)]
        first += [copy(1 + j, me, (*chip, c), src=x_ref) for j, chip in enumerate(chips)]
        for cp in first:
            cp.start()
        passed = [copy(4 + j, (*chip, c), sibling) for j, chip in enumerate(chips)]
        for j, chip in enumerate(chips):
            copy(1 + j, (*chip, c), me).wait_recv()
            passed[j].start()
        copy(0, sibling, me).wait_recv()
        for j, chip in enumerate(chips):
            copy(4 + j, (*chip, 1 - c), me).wait_recv()
        for cp in first + passed:
            cp.wait_send()
        mine.wait()
        acc = all_ref[0:m, :]
        for dev in range(1, 8):
            acc = acc + all_ref[dev * m:(dev + 1) * m, :]
        sum_ref[...] = acc

    vm = pl.BlockSpec(memory_space=pltpu.VMEM)
    return pl.pallas_call(
        body, name="small_allreduce",
        out_shape=[jax.ShapeDtypeStruct((m, n), F32), jax.ShapeDtypeStruct((8 * m, n), F32)],
        in_specs=[vm], out_specs=[vm, vm],
        scratch_shapes=[pltpu.SemaphoreType.DMA((7,)), pltpu.SemaphoreType.DMA((7,)), pltpu.SemaphoreType.DMA],
    )(v)


def grads_by_chip(layer_grads):
    return [layer_grads[n] if n == "w_in" else _split_layer(layer_grads[n], SHARDED[n]) for n in REDUCE_BIG]


def _adamw_block(w_ref, g_ref, m_ref, v_ref, d_ref, nm_ref, nv_ref):
    gv = g_ref[...]
    nm = ADAM_B1 * m_ref[...] + (1.0 - ADAM_B1) * gv
    nv = ADAM_B2 * v_ref[...] + (1.0 - ADAM_B2) * (gv * gv)
    m_hat = nm / (1.0 - ADAM_B1 ** ADAM_STEP)
    v_hat = nv / (1.0 - ADAM_B2 ** ADAM_STEP)
    d_ref[...] = -ADAM_LR * (m_hat / (jnp.sqrt(v_hat) + ADAM_EPS) + ADAM_WD * w_ref[...])
    nm_ref[...] = nm
    nv_ref[...] = nv


def adamw(name, w, g, m, v, block):
    grid = tuple(s // b for s, b in zip(w.shape, block))

    def body(*refs):
        _adamw_block(*refs)

    spec = pl.BlockSpec(tuple(block), lambda *idx: idx)
    return pl.pallas_call(
        body, name=name, grid=grid, in_specs=[spec] * 4, out_specs=[spec] * 3,
        out_shape=[jax.ShapeDtypeStruct(w.shape, F32)] * 3, compiler_params=_cparams(len(grid)),
    )(w, g, m, v)


def adamw_many(name, groups):
    n = len(groups)

    def spec(a):
        rows, cols = a.shape
        if rows % (8 * ADAM_PARTS) == 0:
            return pl.BlockSpec((rows // ADAM_PARTS, cols), lambda i: (i, 0))
        return pl.BlockSpec((rows, cols), lambda i: (0, 0))

    def body(*refs):
        for t in range(n):
            _adamw_block(*refs[4 * t:4 * t + 4], *refs[4 * n + 3 * t:4 * n + 3 * t + 3])

    res = pl.pallas_call(
        body, name=name, grid=(ADAM_PARTS,),
        in_specs=[spec(grp[0]) for grp in groups for _ in range(4)],
        out_specs=[spec(grp[0]) for grp in groups for _ in range(3)],
        out_shape=[jax.ShapeDtypeStruct(grp[0].shape, F32) for grp in groups for _ in range(3)],
        compiler_params=_cparams(1),
    )(*[a for grp in groups for a in grp])
    return [tuple(res[3 * t:3 * t + 3]) for t in range(n)]


ADAM_PARTS = 4
ADAM_W_IN_COLS = 331

WEIGHT_NAMES = ("w_ada", "b_ada", "norm_g", "w_in", "q_norm_g", "k_norm_g", "sinks", "dw_w", "dw_b", "ln_g", "ln_b",
                "pw2_w", "pw2_b", "sconv_w", "a_log", "dt_bias", "dn_norm_g", "w_proj_a", "w_proj_b", "w_proj_c", "w_out")


def kernel(x, c, w_ada, b_ada, norm_g, w_in, q_norm_g, k_norm_g, sinks, dw_w, dw_b, ln_g, ln_b, pw2_w, pw2_b, sconv_w, a_log, dt_bias, dn_norm_g, w_proj_a, w_proj_b, w_proj_c, w_out, loss_target, m_w_ada, m_b_ada, m_norm_g, m_w_in, m_q_norm_g, m_k_norm_g, m_sinks, m_dw_w, m_dw_b, m_ln_g, m_ln_b, m_pw2_w, m_pw2_b, m_sconv_w, m_a_log, m_dt_bias, m_dn_norm_g, m_w_proj_a, m_w_proj_b, m_w_proj_c, m_w_out, v_w_ada, v_b_ada, v_norm_g, v_w_in, v_q_norm_g, v_k_norm_g, v_sinks, v_dw_w, v_dw_b, v_ln_g, v_ln_b, v_pw2_w, v_pw2_b, v_sconv_w, v_a_log, v_dt_bias, v_dn_norm_g, v_w_proj_a, v_w_proj_b, v_w_proj_c, v_w_out):
    args = dict(locals())
    w = {n: args[n] for n in WEIGHT_NAMES}
    mom = {n: args["m_" + n] for n in WEIGHT_NAMES}
    var = {n: args["v_" + n] for n in WEIGHT_NAMES}

    chip = 2 * lax.axis_index("x") + lax.axis_index("y")
    own = {n: w[n] if n in GATHER_F32 else w[n].astype(BF16) for n in GATHERED}
    own["w_in"] = lax.switch(chip, [functools.partial(_w_in_send, k) for k in range(N_CHIPS)], own["w_in"])
    slots = [[lax.dynamic_update_slice(lax.empty((N_CHIPS,) + own[n].shape[1:], own[n].dtype), own[n][l][None], (chip, 0, 0))
              for n in GATHERED] for l in range(DEPTH)]

    def layer_operands(l, gathered):
        lw = {n: w[n][l] for n in SMALL}
        lw.update({n: g if n == "w_in" else _join_layer(g, SHARDED[n]) for n, g in zip(GATHERED, gathered)})
        return prep_layer(lw)

    layers = [layer_operands(0, run_carry("weights_allgather0", carry_allgather(0, slots[0]))), None]

    mod, conds = ada_fwd(jnp.tile(c, (8, 1)), w["w_ada"], w["b_ada"])
    saved = [None] * DEPTH
    big = GATHERED.index("w_in")
    rest = [i for i in range(len(GATHERED)) if i != big]
    act, saved[0], got_big, got_rest = layer_fwd(
        "0", x[0], mod[0:1], layers[0], carry_inproj=carry_allgather(1, [slots[1][big]]),
        carry_merge=carry_allgather(1, [slots[1][i] for i in rest]))
    gathered1 = dict(zip(rest, got_rest))
    gathered1[big] = got_big[0]
    layers[1] = layer_operands(1, [gathered1[i] for i in range(len(GATHERED))])
    act, saved[1], _, _ = layer_fwd("1", act, mod[1:2], layers[1])
    dact, loss_part = loss_head("loss_head", act, loss_target[0], TM)
    loss = lax.psum(loss_part[0, 0], ("x", "y", "c"))
    layer_grads = [None] * DEPTH
    dact, layer_grads[1], _, _, _ = layer_bwd("1", dact, layers[1], saved[1])
    gs1 = grads_by_chip(layer_grads[1])
    gs0 = []

    def hand_over_layer0(reduced):
        gs0.extend(grads_by_chip(reduced))
        return carry_pair_send(0, gs0)

    dact, layer_grads[0], recv1, got1, recv0 = layer_bwd(
        "0", dact, layers[0], saved[0], carry_merge=carry_pair_send(1, gs1),
        carry_delta=lambda recv: carry_chip_exchange(1, grads_pair_sums(1, gs1, recv)), carry_dh=hand_over_layer0)

    got0 = run_carry("grads_chip_exchange0", carry_chip_exchange(0, grads_pair_sums(0, gs0, recv0)))
    reds = grads_chip_sums(0, gs0, recv0, got0, into=grads_chip_sums(1, gs1, recv1, got1))
    final_grads = dict(zip(REDUCE_BIG, grads_pair_gather(reds)))
    final_grads["w_in"] = lax.switch(chip, [functools.partial(_w_in_receive_grad, k) for k in range(N_CHIPS)],
                                     final_grads["w_in"])
    small_names = SMALL + GATHER_F32
    small_shapes = {n: (DEPTH,) + layer_grads[0][n].shape for n in small_names}
    small_full = {n: jnp.stack([layer_grads[l][n] for l in range(DEPTH)]) for n in small_names}
    small_sum, small_all = small_allreduce(pack_small(small_full, small_names, SMALL_GRAD_ROWS))
    small_sum = unpack_small(small_sum, small_names, small_shapes)
    for n in GATHER_F32:
        width = w[n].shape[2]
        final_grads[n] = lax.dynamic_slice_in_dim(small_sum[n], chip * width, width, axis=2)
    n_mod = DEPTH * 3 * D_MODEL
    dmod = small_all.reshape(8, -1)[:, :n_mod].reshape(8, DEPTH, 3 * D_MODEL)
    width = w["w_ada"].shape[2]
    dmod = jnp.transpose(lax.dynamic_slice_in_dim(dmod, chip * width, width, axis=2), (1, 0, 2))
    final_grads["w_ada"] = ada_bwd(conds, dmod)
    final_grads.update({n: small_sum[n] for n in SMALL})
    small_grads = pack_small(final_grads, SMALL, SMALL_ROWS)

    delta, new_m, new_v = {}, {}, {}
    shp = w["w_in"].shape
    view = lambda a: jnp.transpose(a, (2, 0, 1))
    back = lambda a: jnp.transpose(a, (1, 2, 0))
    g3 = view(final_grads["w_in"])
    final_grads["w_in"] = back(g3)
    d, nm, nv = adamw("adamw_w_in", view(w["w_in"]), g3, view(mom["w_in"]), view(var["w_in"]),
                      (ADAM_W_IN_COLS, shp[0], shp[1]))
    delta["w_in"], new_m["w_in"], new_v["w_in"] = back(d), back(nm), back(nv)
    others = [n for n in SHARDED if n != "w_in"]
    two_d = lambda a: a.reshape(a.shape[0] * a.shape[1], a.shape[2])
    groups = [tuple(two_d(t[n]) for t in (w, final_grads, mom, var)) for n in others]
    groups.append((pack_small(w, SMALL, SMALL_ROWS), small_grads, pack_small(mom, SMALL, SMALL_ROWS),
                   pack_small(var, SMALL, SMALL_ROWS)))
    results = adamw_many("adamw_rest", groups)
    for n, (d, nm, nv) in zip(others, results):
        delta[n], new_m[n], new_v[n] = (a.reshape(w[n].shape) for a in (d, nm, nv))
    for out, packed in zip((delta, new_m, new_v), results[-1]):
        out.update(unpack_small(packed, SMALL, small_shapes))

    return (loss, dact[None], *[final_grads[n] for n in WEIGHT_NAMES], *[delta[n] for n in WEIGHT_NAMES],
            *[new_m[n] for n in WEIGHT_NAMES], *[new_v[n] for n in WEIGHT_NAMES])
```

```python
import functools

import numpy as np
import jax
import jax.numpy as jnp
from jax import lax
from jax.experimental import pallas as pl
from jax.experimental.pallas import tpu as pltpu

F32 = jnp.float32
BF16 = jnp.bfloat16
MESH = pl.DeviceIdType.MESH

D_MODEL = 1024
DEPTH = 2
ATT_HEADS = 8
ATT_HEAD_DIM = 64
WINDOW = 128
CONV_K = 31
DN_HEADS = 4
DN_CONV_K = 4
DN_CHUNK = 64
EPS = 1e-6
NEG_INF = -1e30
N_CHIPS = 4
D_IN = 7944

ADAM_LR = 0.001
ADAM_B1 = 0.9
ADAM_B2 = 0.999
ADAM_EPS = 1e-08
ADAM_WD = 0.01
ADAM_STEP = 10

VMEM_LIMIT = 56 * 1024 * 1024

P_QA, P_ZA, P_GLU, P_ZB, P_ZC, P_MG, P_QKV, P_KA, P_VA, P_AB, P_TOTAL = (
    0, 512, 1024, 2048, 2560, 3072, 6144, 7680, 7808, 7936, 8064)
HEAD_ORDER = (0, 4, 1, 5, 2, 6, 3, 7)


def _in_pieces():
    p = [(0 + 64 * h, 64) for h in HEAD_ORDER]
    p += [(768 + 64 * h, 64) for h in HEAD_ORDER]
    for g in range(4):
        p += [(1280 + 128 * g, 128), (1792 + 128 * g, 128)]
    p += [(2304, 512), (4360, 512), (4872, 3072), (2816, 1536), (512, 128), (640, 128), (4352, 8)]
    return p


def _perm_heads_rows(w):
    return jnp.concatenate([w[64 * h:64 * h + 64] for h in HEAD_ORDER], axis=0)


def _unperm_heads_rows(w):
    inv = [HEAD_ORDER.index(h) for h in range(8)]
    return jnp.concatenate([w[64 * s:64 * s + 64] for s in inv], axis=0)


def _split_bf16(a, terms):
    out, rest = [], a.astype(F32)
    for _ in range(terms - 1):
        out.append(rest.astype(BF16))
        rest = rest - out[-1].astype(F32)
    return out + [rest.astype(BF16)]


def _dot(a, b, dims, exact):
    d = lambda p, q: lax.dot_general(p, q, (dims, ((), ())), preferred_element_type=F32)
    if exact:
        (ah, al), (bh, bl) = _split_bf16(a, 2), _split_bf16(b, 2)
        return d(ah, bh) + (d(ah, bl) + d(al, bh))
    return d(a.astype(BF16), b.astype(BF16))


def _make_mm(exact):
    @jax.custom_vjp
    def nn(a, b):
        return _dot(a, b, ((1,), (0,)), exact)

    @jax.custom_vjp
    def nt(a, b):
        return _dot(a, b, ((1,), (1,)), exact)

    @jax.custom_vjp
    def tn(a, b):
        return _dot(a, b, ((0,), (0,)), exact)

    nn.defvjp(lambda a, b: (nn(a, b), (a, b)),
              lambda r, g: (nt(g, r[1]).astype(r[0].dtype), tn(r[0], g).astype(r[1].dtype)))
    nt.defvjp(lambda a, b: (nt(a, b), (a, b)),
              lambda r, g: (nn(g, r[1]).astype(r[0].dtype), tn(g, r[0]).astype(r[1].dtype)))
    tn.defvjp(lambda a, b: (tn(a, b), (a, b)),
              lambda r, g: (nt(r[1], g).astype(r[0].dtype), nn(r[0], g).astype(r[1].dtype)))
    return nn, nt, tn


mm, mm_nt, mm_tn = _make_mm(False)
xmm, xmm_nt, xmm_tn = _make_mm(True)


@jax.custom_vjp
def sel_mm(m, g):
    mb = m.astype(BF16)
    parts = [jnp.dot(mb, p, preferred_element_type=F32) for p in _split_bf16(g, 3)]
    return parts[0] + (parts[1] + parts[2])


def _sel_mm_bwd(m, dy):
    mb = m.astype(BF16)
    parts = [lax.dot_general(mb, p, (((0,), (0,)), ((), ())), preferred_element_type=F32) for p in _split_bf16(dy, 3)]
    return jnp.zeros_like(m), parts[0] + (parts[1] + parts[2])


sel_mm.defvjp(lambda m, g: (sel_mm(m, g), m), _sel_mm_bwd)


@jax.custom_vjp
def tri_inv(*mats):
    n = mats[0].shape[0]
    eye = jnp.where(lax.broadcasted_iota(jnp.int32, (n, n), 0) == lax.broadcasted_iota(jnp.int32, (n, n), 1), 1.0, 0.0)
    ts = [eye - a for a in mats]
    pws = list(mats)
    for _ in range(5):
        pws = [xmm(pw, pw) for pw in pws]
        ts = [t + xmm(t, pw) for t, pw in zip(ts, pws)]
    return tuple(ts)


def _tri_inv_bwd(ts, dts):
    inner = [xmm_nt(dt, t) for t, dt in zip(ts, dts)]
    return tuple(-xmm_tn(t, m) for t, m in zip(ts, inner))


tri_inv.defvjp(lambda *mats: (tri_inv(*mats),) * 2, _tri_inv_bwd)


@jax.custom_vjp
def tri_inv_known(a, t):
    return t


tri_inv_known.defvjp(lambda a, t: (t, t), lambda t, dt: (_tri_inv_bwd((t,), (dt,))[0], jnp.zeros_like(t)))


def _sigmoid(x):
    return 1.0 / (1.0 + jnp.exp(-x))


def _silu(x):
    return x * _sigmoid(x)


def _softplus(x):
    return jnp.maximum(x, 0.0) + jnp.log(1.0 + jnp.exp(-jnp.abs(x)))


def _cparams(n_grid):
    return pltpu.CompilerParams(dimension_semantics=("arbitrary",) * n_grid, vmem_limit_bytes=VMEM_LIMIT)


def _row_spec(tm, width, colblk):
    return pl.BlockSpec((tm, width), lambda i, cb=colblk: (i, cb))


def _const_spec(shape):
    nd = len(shape)
    return pl.BlockSpec(tuple(shape), lambda i, nd=nd: (0,) * nd)


def rowwise_fwd(name, f, rows, consts, outs, tm, carry=None):
    n_r, n_c = len(rows), len(consts)
    t = rows[0][0].shape[0]
    c_ins, c_in_specs, c_outs, c_out_specs, c_scratch = _host(carry)
    n_in, n_ci, n_co = n_r + n_c, len(c_ins), len(c_outs)

    def body(*refs):
        carried = (refs[n_in:n_in + n_ci], refs[n_in + n_ci + len(outs):n_in + n_ci + len(outs) + n_co],
                   *refs[n_in + n_ci + len(outs) + n_co:])
        if carry is not None:
            carry.emit_start(pl.program_id(0) == 0, *carried)
        vals = [r[...] for r in refs[:n_in]]
        res = f(*vals)
        if not isinstance(res, (tuple, list)):
            res = (res,)
        for o_ref, v, out in zip(refs[n_in + n_ci:n_in + n_ci + len(outs)], res, outs):
            o_ref[...] = (v.T if len(out) == 3 else v).astype(o_ref.dtype)
        if carry is not None:
            carry.emit_finish(pl.program_id(0) == t // tm - 1, *carried)

    return pl.pallas_call(
        body, name=name, grid=(t // tm,),
        in_specs=[_row_spec(tm, w, cb) for _, w, cb in rows] + [_const_spec(c.shape) for c in consts] + c_in_specs,
        out_specs=[_row_spec(tm, o[0], 0) if len(o) == 2 else pl.BlockSpec((o[0], tm), lambda i: (0, i)) for o in outs]
        + c_out_specs,
        out_shape=[jax.ShapeDtypeStruct((t, o[0]) if len(o) == 2 else (o[0], t), o[1]) for o in outs] + c_outs,
        input_output_aliases={} if carry is None else carry.aliases(n_in, len(outs)),
        scratch_shapes=c_scratch,
        compiler_params=_cparams(1),
    )(*[a for a, _, _ in rows], *consts, *c_ins)


def rowwise_bwd(name, f, rows, consts, cts, row_grad_dtypes, tm, carry=None):
    n_r, n_c, n_ct = len(rows), len(consts), len(cts)
    t = rows[0][0].shape[0]
    keep = [k for k, dt in enumerate(row_grad_dtypes) if dt is not None]
    c_ins, c_in_specs, c_outs, c_out_specs, c_scratch = _host(carry)
    n_in, n_out = n_r + n_c + n_ct, len(keep) + n_c

    def body(*refs):
        ins = [r[...].astype(F32) for r in refs[:n_r + n_c]]
        g_out = [r[...].astype(F32) for r in refs[n_r + n_c:n_in]]
        out_refs = refs[n_in + len(c_ins):n_in + len(c_ins) + n_out]
        carried = (refs[n_in:n_in + len(c_ins)], refs[n_in + len(c_ins) + n_out:n_in + len(c_ins) + n_out + len(c_outs)],
                   *refs[n_in + len(c_ins) + n_out + len(c_outs):])
        if carry is not None:
            carry.emit_start(pl.program_id(0) == 0, *carried)

        def fw(*a):
            res = f(*a)
            return tuple(res) if isinstance(res, (tuple, list)) else (res,)

        _, vjp = jax.vjp(fw, *ins)
        grads = vjp(tuple(g_out))
        for o_ref, k in zip(out_refs[:len(keep)], keep):
            o_ref[...] = grads[k].astype(o_ref.dtype)
        first = pl.program_id(0) == 0
        for o_ref, g in zip(out_refs[len(keep):], grads[n_r:]):
            @pl.when(first)
            def _(o_ref=o_ref, g=g):
                o_ref[...] = g

            @pl.when(jnp.logical_not(first))
            def _(o_ref=o_ref, g=g):
                o_ref[...] += g
        if carry is not None:
            carry.emit_finish(pl.program_id(0) == t // tm - 1, *carried)

    return pl.pallas_call(
        body, name=name, grid=(t // tm,),
        in_specs=[_row_spec(tm, w, cb) for _, w, cb in rows] + [_const_spec(c.shape) for c in consts]
        + [_row_spec(tm, w, cb) for _, w, cb in cts] + c_in_specs,
        out_specs=[_row_spec(tm, rows[k][1], 0) for k in keep] + [_const_spec(c.shape) for c in consts] + c_out_specs,
        out_shape=[jax.ShapeDtypeStruct((t, rows[k][1]), row_grad_dtypes[k]) for k in keep]
        + [jax.ShapeDtypeStruct(c.shape, F32) for c in consts] + c_outs,
        scratch_shapes=c_scratch,
        compiler_params=_cparams(1),
    )(*[a for a, _, _ in rows], *consts, *[a for a, _, _ in cts], *c_ins)


def f_norm_mod(x, g, scale, shift):
    y = x * lax.rsqrt(jnp.mean(x * x, axis=-1, keepdims=True) + EPS) * g
    return y * (1.0 + scale) + shift


def f_conf_tail(u, zb, ln_g, ln_b, pw2_w, pw2_b):
    mu = jnp.mean(u, axis=-1, keepdims=True)
    xc = u - mu
    var = jnp.mean(xc * xc, axis=-1, keepdims=True)
    y = _silu(xc * lax.rsqrt(var + EPS) * ln_g + ln_b)
    return (mm(y, pw2_w) + pw2_b) * _silu(zb)


def f_merge(ya, yb, yc, mg, x, gate, wpa, wpb, wpc, wout):
    d = D_MODEL
    merged = (_sigmoid(mg[:, :d]) * mm(ya, wpa) + _sigmoid(mg[:, d:2 * d]) * mm(yb, wpb)
              + _sigmoid(mg[:, 2 * d:]) * mm(yc, wpc))
    return x + gate * mm(merged, wout)


def matmul_nn(name, a, b, out_dtype, tm, tn, tk, b_transposed=False, carry=None):
    m, k = a.shape
    n = b.shape[0] if b_transposed else b.shape[1]
    nk = k // tk
    grid = (m // tm, n // tn, nk)
    b_spec = (pl.BlockSpec((tn, tk), lambda i, j, kk: (j, kk)) if b_transposed
              else pl.BlockSpec((tk, tn), lambda i, j, kk: (kk, j)))
    c_ins, c_in_specs, c_outs, c_out_specs, c_scratch = _host(carry)
    n_ci, n_co = len(c_ins), len(c_outs)

    def body(*refs):
        a_ref, b_ref, o_ref = refs[0], refs[1], refs[2 + n_ci]
        carried = (refs[2:2 + n_ci], refs[3 + n_ci:3 + n_ci + n_co], *refs[3 + n_ci + n_co:3 + n_ci + n_co + len(c_scratch)])
        at = lambda step: functools.reduce(jnp.logical_and, [pl.program_id(d) == s for d, s in enumerate(step)])
        if carry is not None:
            carry.emit_start(at((0, 0, 0)), *carried)
        part = lax.dot_general(a_ref[...].astype(BF16), b_ref[...].astype(BF16),
                               (((1,), (1 if b_transposed else 0,)), ((), ())), preferred_element_type=F32)
        if nk == 1:
            o_ref[...] = part.astype(o_ref.dtype)
        else:
            kk = pl.program_id(2)
            acc_ref = refs[-1]

            @pl.when(kk == 0)
            def _():
                acc_ref[...] = part

            @pl.when(kk > 0)
            def _():
                acc_ref[...] += part

            @pl.when(kk == nk - 1)
            def _():
                o_ref[...] = acc_ref[...].astype(o_ref.dtype)
        if carry is not None:
            carry.emit_finish(at(tuple(g - 1 for g in grid)), *carried)

    res = pl.pallas_call(
        body, name=name, grid=grid,
        in_specs=[pl.BlockSpec((tm, tk), lambda i, j, kk: (i, kk)), b_spec] + c_in_specs,
        out_specs=[pl.BlockSpec((tm, tn), lambda i, j, kk: (i, j))] + c_out_specs,
        out_shape=[jax.ShapeDtypeStruct((m, n), out_dtype)] + c_outs,
        input_output_aliases={} if carry is None else carry.aliases(2, 1),
        scratch_shapes=c_scratch + ([] if nk == 1 else [pltpu.VMEM((tm, tn), F32)]),
        compiler_params=_cparams(3),
    )(a, b, *c_ins)
    return res[0] if carry is None else res


def ada_fwd(c8, w_shard, b_ada):
    n_cols = w_shard.shape[2]
    masks = [(m >> 2 & 1, m >> 1 & 1, m & 1) for m in range(1, 8)]

    def body(c_ref, w_ref, b_ref, mod_ref, conds_ref, cbuf, sendbuf, recvbuf, send_sems, recv_sems):
        x, y, c, chips = _place()
        flip = lambda v, bit: 1 - v if bit else v
        peers = [(flip(x, mx), flip(y, my), flip(c, mc)) for mx, my, mc in masks]
        dev = lambda p: 4 * p[0] + 2 * p[1] + p[2]
        cbuf[dev((x, y, c))] = c_ref[...]
        first = [_remote(c_ref, cbuf.at[dev((x, y, c))], send_sems.at[i], recv_sems.at[i], p) for i, p in enumerate(peers)]
        for cp in first:
            cp.start()
        for i, p in enumerate(peers):
            _remote(c_ref, cbuf.at[dev(p)], send_sems.at[i], recv_sems.at[i], p).wait_recv()
        conds = jnp.concatenate([cbuf[d, 0:1, :] for d in range(8)], axis=0)
        conds_ref[...] = conds
        act = _silu(conds)
        parts = [mm(act, w_ref[l]) for l in range(DEPTH)]
        row8 = lax.broadcasted_iota(jnp.int32, (8, 1), 0)

        def tile_for(chip):
            r = 2 * (2 * chip[0] + chip[1]) + c
            rows = [jnp.sum(jnp.where(row8 == r, parts[l], 0.0), axis=0, keepdims=True) for l in range(DEPTH)]
            return jnp.where(row8 == 0, rows[0], jnp.where(row8 == 1, rows[1], 0.0))

        my_slot = 2 * x + y
        recvbuf[my_slot] = tile_for((x, y))
        second = []
        for j, chip in enumerate(chips):
            sendbuf[j] = tile_for(chip)
            second.append(_remote(sendbuf.at[j], recvbuf.at[my_slot], send_sems.at[7 + j], recv_sems.at[7 + j], (*chip, c)))
            second[-1].start()
        for j, chip in enumerate(chips):
            _remote(sendbuf.at[j], recvbuf.at[2 * chip[0] + chip[1]], send_sems.at[7 + j], recv_sems.at[7 + j],
                    (*chip, c)).wait_recv()
        rows = [jnp.concatenate([recvbuf[k, l:l + 1, :] for k in range(N_CHIPS)], axis=1) + b_ref[l:l + 1, :]
                for l in range(DEPTH)]
        mod_ref[...] = jnp.concatenate(rows + [jnp.zeros((8 - DEPTH, N_CHIPS * n_cols), F32)], axis=0)
        for cp in first + second:
            cp.wait_send()

    vm = pl.BlockSpec(memory_space=pltpu.VMEM)
    return pl.pallas_call(
        body, name="ada_fwd",
        out_shape=[jax.ShapeDtypeStruct((8, N_CHIPS * n_cols), F32), jax.ShapeDtypeStruct((8, D_MODEL), F32)],
        in_specs=[vm, vm, vm], out_specs=[vm, vm],
        scratch_shapes=[pltpu.VMEM((8, 8, D_MODEL), F32), pltpu.VMEM((3, 8, n_cols), F32),
                        pltpu.VMEM((N_CHIPS, 8, n_cols), F32), pltpu.SemaphoreType.DMA((10,)), pltpu.SemaphoreType.DMA((10,))],
        compiler_params=pltpu.CompilerParams(vmem_limit_bytes=VMEM_LIMIT),
    )(c8, w_shard, b_ada)


def ada_bwd(conds, dmod):
    def body(c_ref, d_ref, o_ref):
        act = _silu(c_ref[...])
        for l in range(DEPTH):
            o_ref[l] = mm_tn(act, d_ref[l])

    return pl.pallas_call(
        body, name="ada_bwd", out_shape=jax.ShapeDtypeStruct((DEPTH, D_MODEL, dmod.shape[2]), F32),
        compiler_params=pltpu.CompilerParams(vmem_limit_bytes=VMEM_LIMIT),
    )(conds, dmod)


def _f_attn(first_block, q, za, kc, vc, kp, vp, qg, kg, sinks):
    w = WINDOW
    lane = lax.broadcasted_iota(jnp.int32, (1, 128), 1)
    halves = [lane < 64, lane >= 64]

    def rms_halves(x, g):
        x2 = x * x
        s0 = jnp.sum(jnp.where(halves[0], x2, 0.0), axis=-1, keepdims=True)
        s1 = jnp.sum(jnp.where(halves[1], x2, 0.0), axis=-1, keepdims=True)
        r = jnp.where(halves[0], lax.rsqrt(s0 / 64.0 + EPS), lax.rsqrt(s1 / 64.0 + EPS))
        return x * r * g

    kcat = rms_halves(jnp.concatenate([kp, kc], axis=0), kg)
    vcat = jnp.concatenate([vp, vc], axis=0)
    qi = lax.broadcasted_iota(jnp.int32, (w, 2 * w), 0)
    kj = lax.broadcasted_iota(jnp.int32, (w, 2 * w), 1)
    dist = qi + w - kj
    valid = (dist >= 0) & (dist < w) & (jnp.logical_not(first_block) | (kj >= w))
    distf = dist.astype(F32)
    units = [(grp, half) for grp in range(4) for half in range(2)]
    qns = [rms_halves(q[:, 128 * grp:128 * grp + 128], qg) * (ATT_HEAD_DIM ** -0.5) for grp in range(4)]
    vhalf = [jnp.where(halves[half], vcat, 0.0) for half in range(2)]
    scores, sinks_h = [], []
    for grp, half in units:
        head = HEAD_ORDER[2 * grp + half]
        slope = 2.0 ** (-8.0 * (head + 1) / ATT_HEADS)
        sinks_h.append(jnp.sum(jnp.where(lane == head, sinks, 0.0), axis=-1, keepdims=True))
        s = mm_nt(jnp.where(halves[half], qns[grp], 0.0), kcat) - slope * distf
        scores.append(jnp.where(valid, s, NEG_INF))
    probs = []
    for s, sink in zip(scores, sinks_h):
        m = lax.stop_gradient(jnp.maximum(jnp.max(s, axis=-1, keepdims=True), sink))
        p = jnp.exp(s - m)
        probs.append(p / (jnp.sum(p, axis=-1, keepdims=True) + jnp.exp(sink - m)))
    outs = [mm(p, vhalf[half]) for p, (grp, half) in zip(probs, units)]
    return jnp.concatenate([outs[2 * grp] + outs[2 * grp + 1] for grp in range(4)], axis=1) * _silu(za)


def attn_fwd(name, proj, qg, kg, sinks):
    t = proj.shape[0]
    nb = t // WINDOW

    def body(q_ref, za_ref, kc_ref, vc_ref, kp_ref, vp_ref, qg_ref, kg_ref, s_ref, o_ref):
        first = pl.program_id(0) == 0
        o_ref[...] = _f_attn(first, q_ref[...], za_ref[...], kc_ref[...], vc_ref[...], kp_ref[...], vp_ref[...],
                             qg_ref[...], kg_ref[...], s_ref[...])

    cur = lambda cb: (lambda i: (i, cb))
    prev = lambda cb: (lambda i: (jnp.maximum(i - 1, 0), cb))
    return pl.pallas_call(
        body, name=name, grid=(nb,),
        in_specs=[pl.BlockSpec((WINDOW, 512), cur(P_QA // 512)), pl.BlockSpec((WINDOW, 512), cur(P_ZA // 512)),
                  pl.BlockSpec((WINDOW, 128), cur(P_KA // 128)), pl.BlockSpec((WINDOW, 128), cur(P_VA // 128)),
                  pl.BlockSpec((WINDOW, 128), prev(P_KA // 128)), pl.BlockSpec((WINDOW, 128), prev(P_VA // 128)),
                  _const_spec((1, 128)), _const_spec((1, 128)), _const_spec((1, 128))],
        out_specs=pl.BlockSpec((WINDOW, 512), lambda i: (i, 0)),
        out_shape=jax.ShapeDtypeStruct((t, 512), F32),
        compiler_params=_cparams(1),
    )(proj, proj, proj, proj, proj, proj, qg, kg, sinks)


def attn_bwd(name, proj, qg, kg, sinks, dya):
    t = proj.shape[0]
    nb = t // WINDOW

    def body(q_ref, za_ref, kc_ref, vc_ref, kp_ref, vp_ref, qg_ref, kg_ref, s_ref, dy_ref,
             dqz_ref, dkv_ref, dqg_ref, dkg_ref, ds_ref, carry_ref):
        j = pl.program_id(0)
        first = j == nb - 1

        @pl.when(j == 0)
        def _():
            carry_ref[...] = jnp.zeros_like(carry_ref)
            dqg_ref[...] = jnp.zeros_like(dqg_ref)
            dkg_ref[...] = jnp.zeros_like(dkg_ref)
            ds_ref[...] = jnp.zeros_like(ds_ref)

        ins = [r[...] for r in (q_ref, za_ref, kc_ref, vc_ref, kp_ref, vp_ref, qg_ref, kg_ref, s_ref)]
        _, vjp = jax.vjp(functools.partial(_f_attn, first), *ins)
        dq, dza, dkc, dvc, dkp, dvp, dqg, dkg, dsk = vjp(dy_ref[...])
        dqz_ref[:, 0:512] = dq.astype(dqz_ref.dtype)
        dqz_ref[:, 512:1024] = dza.astype(dqz_ref.dtype)
        dkv_ref[:, 0:128] = (dkc + carry_ref[0]).astype(dkv_ref.dtype)
        dkv_ref[:, 128:256] = (dvc + carry_ref[1]).astype(dkv_ref.dtype)
        carry_ref[0] = dkp
        carry_ref[1] = dvp
        dqg_ref[...] += dqg
        dkg_ref[...] += dkg
        ds_ref[...] += dsk

    cur = lambda cb: (lambda j: (nb - 1 - j, cb))
    prev = lambda cb: (lambda j: (jnp.maximum(nb - 2 - j, 0), cb))
    return pl.pallas_call(
        body, name=name, grid=(nb,),
        in_specs=[pl.BlockSpec((WINDOW, 512), cur(P_QA // 512)), pl.BlockSpec((WINDOW, 512), cur(P_ZA // 512)),
                  pl.BlockSpec((WINDOW, 128), cur(P_KA // 128)), pl.BlockSpec((WINDOW, 128), cur(P_VA // 128)),
                  pl.BlockSpec((WINDOW, 128), prev(P_KA // 128)), pl.BlockSpec((WINDOW, 128), prev(P_VA // 128)),
                  _const_spec((1, 128)), _const_spec((1, 128)), _const_spec((1, 128)),
                  pl.BlockSpec((WINDOW, 512), cur(0))],
        out_specs=[pl.BlockSpec((WINDOW, 1024), cur(0)), pl.BlockSpec((WINDOW, 256), cur(0)),
                   _const_spec((1, 128)), _const_spec((1, 128)), _const_spec((1, 128))],
        out_shape=[jax.ShapeDtypeStruct((t, 1024), BF16), jax.ShapeDtypeStruct((t, 256), BF16),
                   jax.ShapeDtypeStruct((1, 128), F32), jax.ShapeDtypeStruct((1, 128), F32),
                   jax.ShapeDtypeStruct((1, 128), F32)],
        scratch_shapes=[pltpu.VMEM((2, WINDOW, 128), F32)],
        compiler_params=_cparams(1),
    )(proj, proj, proj, proj, proj, proj, qg, kg, sinks, dya)


CONV_ROWS = 256


def _conv_taps(src_ref, w_ref, n_taps, base, t):
    for r0 in range(0, t, CONV_ROWS):
        acc = w_ref[0:1, :] * src_ref[pl.ds(r0 + base, CONV_ROWS), :]
        for k in range(1, n_taps):
            acc = acc + w_ref[k:k + 1, :] * src_ref[pl.ds(r0 + base + k, CONV_ROWS), :]
        yield r0, acc


def _conv_wgrad(dy_ref, src_ref, n_taps, base, t, dy_base=0):
    out = []
    for k in range(n_taps):
        acc = jnp.zeros((8, 128), F32)
        for r0 in range(0, t, CONV_ROWS):
            prod = dy_ref[pl.ds(r0 + dy_base, CONV_ROWS), :] * src_ref[pl.ds(r0 + base + k, CONV_ROWS), :]
            acc = acc + jnp.sum(prod.reshape(CONV_ROWS // 8, 8, 128), axis=0)
        out.append(jnp.sum(acc, axis=0, keepdims=True))
    return out


def glu_conv_fwd(name, proj, w32, bias):
    t = proj.shape[0]
    pad = 32

    def body(x_ref, w_ref, b_ref, o_ref, u_ref):
        u_ref[0:pad, :] = jnp.zeros((pad, 128), F32)
        u_ref[pad:pad + t, :] = x_ref[:, 0:128] * _sigmoid(x_ref[:, 128:256])
        for r0, acc in _conv_taps(u_ref, w_ref, CONV_K, pad - (CONV_K - 1), t):
            o_ref[pl.ds(r0, CONV_ROWS), :] = acc + b_ref[...]

    return pl.pallas_call(
        body, name=name, grid=(4,),
        in_specs=[pl.BlockSpec((t, 256), lambda cb: (0, P_GLU // 256 + cb)), pl.BlockSpec((32, 128), lambda cb: (0, cb)),
                  pl.BlockSpec((1, 128), lambda cb: (0, cb))],
        out_specs=pl.BlockSpec((t, 128), lambda cb: (0, cb)),
        out_shape=jax.ShapeDtypeStruct((t, 512), F32),
        scratch_shapes=[pltpu.VMEM((t + pad, 128), F32)],
        compiler_params=_cparams(1),
    )(proj, w32, bias)


def glu_conv_bwd(name, proj, w32, dub):
    t = proj.shape[0]
    pad = 32
    k1 = CONV_K - 1

    def body(x_ref, w_ref, dy_ref, dx_ref, dw_ref, db_ref, u_ref, dyp_ref, wrev_ref):
        val = x_ref[:, 0:128]
        sg = _sigmoid(x_ref[:, 128:256])
        u_ref[0:pad, :] = jnp.zeros((pad, 128), F32)
        u_ref[pad:pad + t, :] = val * sg
        dyp_ref[0:t, :] = dy_ref[...]
        dyp_ref[t:t + pad, :] = jnp.zeros((pad, 128), F32)
        for k in range(CONV_K):
            wrev_ref[k:k + 1, :] = w_ref[k1 - k:k1 - k + 1, :]
        wrev_ref[CONV_K:32, :] = jnp.zeros((32 - CONV_K, 128), F32)
        for r0, du in _conv_taps(dyp_ref, wrev_ref, CONV_K, 0, t):
            v = x_ref[pl.ds(r0, CONV_ROWS), 0:128]
            s = _sigmoid(x_ref[pl.ds(r0, CONV_ROWS), 128:256])
            dx_ref[pl.ds(r0, CONV_ROWS), 0:128] = (du * s).astype(dx_ref.dtype)
            dx_ref[pl.ds(r0, CONV_ROWS), 128:256] = (du * v * s * (1.0 - s)).astype(dx_ref.dtype)
        dws = _conv_wgrad(dyp_ref, u_ref, CONV_K, pad - k1, t)
        for k in range(CONV_K):
            dw_ref[k:k + 1, :] = dws[k]
        dw_ref[CONV_K:32, :] = jnp.zeros((32 - CONV_K, 128), F32)
        db_ref[...] = jnp.sum(dy_ref[...], axis=0, keepdims=True)

    return pl.pallas_call(
        body, name=name, grid=(4,),
        in_specs=[pl.BlockSpec((t, 256), lambda cb: (0, P_GLU // 256 + cb)), pl.BlockSpec((32, 128), lambda cb: (0, cb)),
                  pl.BlockSpec((t, 128), lambda cb: (0, cb))],
        out_specs=[pl.BlockSpec((t, 256), lambda cb: (0, cb)), pl.BlockSpec((32, 128), lambda cb: (0, cb)),
                   pl.BlockSpec((1, 128), lambda cb: (0, cb))],
        out_shape=[jax.ShapeDtypeStruct((t, 1024), BF16), jax.ShapeDtypeStruct((32, 512), F32),
                   jax.ShapeDtypeStruct((1, 512), F32)],
        scratch_shapes=[pltpu.VMEM((t + pad, 128), F32), pltpu.VMEM((t + pad, 128), F32), pltpu.VMEM((32, 128), F32)],
        compiler_params=_cparams(1),
    )(proj, w32, dub)


def sconv_fwd(name, proj, w8):
    t = proj.shape[0]
    pad = 8
    k1 = DN_CONV_K - 1

    def body(x_ref, w_ref, o_ref, xp_ref):
        xp_ref[0:pad, :] = jnp.zeros((pad, 128), F32)
        xp_ref[pad:pad + t, :] = x_ref[...]
        for r0, acc in _conv_taps(xp_ref, w_ref, DN_CONV_K, pad - k1, t):
            o_ref[pl.ds(r0, CONV_ROWS), :] = _silu(acc)

    return pl.pallas_call(
        body, name=name, grid=(12,),
        in_specs=[pl.BlockSpec((t, 128), lambda cb: (0, P_QKV // 128 + cb)), pl.BlockSpec((8, 128), lambda cb: (0, cb))],
        out_specs=pl.BlockSpec((t, 128), lambda cb: (0, cb)),
        out_shape=jax.ShapeDtypeStruct((t, 1536), F32),
        scratch_shapes=[pltpu.VMEM((t + pad, 128), F32)],
        compiler_params=_cparams(1),
    )(proj, w8)


def sconv_bwd(name, proj, w8, dqkv):
    t = proj.shape[0]
    pad = 8
    k1 = DN_CONV_K - 1

    def body(x_ref, w_ref, dy_ref, dx_ref, dw_ref, xp_ref, dpp_ref, wrev_ref):
        xp_ref[0:pad, :] = jnp.zeros((pad, 128), F32)
        xp_ref[pad:pad + t, :] = x_ref[...]
        for r0, pre in _conv_taps(xp_ref, w_ref, DN_CONV_K, pad - k1, t):
            s = _sigmoid(pre)
            dpp_ref[pl.ds(r0, CONV_ROWS), :] = dy_ref[pl.ds(r0, CONV_ROWS), :] * (s * (1.0 + pre * (1.0 - s)))
        dpp_ref[t:t + pad, :] = jnp.zeros((pad, 128), F32)
        for k in range(DN_CONV_K):
            wrev_ref[k:k + 1, :] = w_ref[k1 - k:k1 - k + 1, :]
        wrev_ref[DN_CONV_K:8, :] = jnp.zeros((8 - DN_CONV_K, 128), F32)
        for r0, dx in _conv_taps(dpp_ref, wrev_ref, DN_CONV_K, 0, t):
            dx_ref[pl.ds(r0, CONV_ROWS), :] = dx.astype(dx_ref.dtype)
        dws = _conv_wgrad(dpp_ref, xp_ref, DN_CONV_K, pad - k1, t)
        for k in range(DN_CONV_K):
            dw_ref[k:k + 1, :] = dws[k]
        dw_ref[DN_CONV_K:8, :] = jnp.zeros((8 - DN_CONV_K, 128), F32)

    return pl.pallas_call(
        body, name=name, grid=(12,),
        in_specs=[pl.BlockSpec((t, 128), lambda cb: (0, P_QKV // 128 + cb)), pl.BlockSpec((8, 128), lambda cb: (0, cb)),
                  pl.BlockSpec((t, 128), lambda cb: (0, cb))],
        out_specs=[pl.BlockSpec((t, 128), lambda cb: (0, cb)), pl.BlockSpec((8, 128), lambda cb: (0, cb))],
        out_shape=[jax.ShapeDtypeStruct((t, 1536), BF16), jax.ShapeDtypeStruct((8, 1536), F32)],
        scratch_shapes=[pltpu.VMEM((t + pad, 128), F32), pltpu.VMEM((t + pad, 128), F32), pltpu.VMEM((8, 128), F32)],
        compiler_params=_cparams(1),
    )(proj, w8, dqkv)


def _f_delta_step(qkv, ab, zc, s0, s1, s2, s3, a_log, dt_bias, dn_g, inverses=None, with_inverses=False):
    cs = DN_CHUNK
    n = 2 * cs
    states = (s0, s1, s2, s3)
    lane = lax.broadcasted_iota(jnp.int32, (1, 128), 1)
    ri = lax.broadcasted_iota(jnp.int32, (n, n), 0)
    ci = lax.broadcasted_iota(jnp.int32, (n, n), 1)
    same = (ri // cs) == (ci // cs)
    lower = same & (ri >= ci)
    strict = same & (ri > ci)
    sums = jnp.concatenate([jnp.where(lower, 1.0, 0.0), jnp.where(same, 1.0, 0.0), jnp.where(ci < cs, 1.0, 0.0),
                            jnp.where(ci >= cs, 1.0, 0.0)], axis=0)
    top = lax.broadcasted_iota(jnp.int32, (n, 1), 0) < cs

    def pick(row, idx):
        return jnp.sum(jnp.where(lane == idx, row, 0.0), axis=-1, keepdims=True)

    def l2n(x):
        return x * lax.rsqrt(jnp.sum(x * x, axis=-1, keepdims=True) + EPS)

    n_chunks = qkv.shape[0] // cs
    units = [(k, pair) for k in range(n_chunks) for pair in range(2)]

    pre = []
    for k, pair in units:
        hs = (2 * pair, 2 * pair + 1)
        rows = slice(k * cs, (k + 1) * cs)
        stack = lambda f: jnp.concatenate([f(hs[0]), f(hs[1])], axis=0)
        qd = l2n(stack(lambda h: qkv[rows, 128 * h:128 * h + 128])) * (128 ** -0.5)
        kd = l2n(stack(lambda h: qkv[rows, 512 + 128 * h:512 + 128 * h + 128]))
        vd = stack(lambda h: qkv[rows, 1024 + 128 * h:1024 + 128 * h + 128])
        beta = _sigmoid(stack(lambda h: pick(ab[rows], 4 + h)))
        g = stack(lambda h: -jnp.exp(pick(a_log, h)) * _softplus(pick(ab[rows], h) + pick(dt_bias, h)))
        g_sums = sel_mm(sums, g * jnp.ones((1, n), F32))
        gc_col = g_sums[0:n]
        gl_b = g_sums[n:2 * n]
        g_end = (g_sums[2 * n:3 * n], g_sums[3 * n:])
        decay = jnp.where(lower, jnp.exp(jnp.where(lower, gc_col - gc_col.T, 0.0)), 0.0)
        kb = kd * beta
        pre.append(dict(qd=qd, kd=kd, vb=vd * beta, kb=kb, gc_col=gc_col, gl_b=gl_b, g_end=g_end, decay=decay,
                        a=jnp.where(strict, mm_nt(kb, kd) * decay, 0.0)))
    if inverses is None:
        tmats = tri_inv(*[p["a"] for p in pre])
    else:
        tmats = [tri_inv_known(p["a"], t) for p, t in zip(pre, inverses)]

    mid = []
    for p, tmat in zip(pre, tmats):
        egc = jnp.exp(p["gc_col"])
        mid.append(dict(u=mm(tmat, p["vb"]), wm=mm(tmat, p["kb"] * egc), qe=p["qd"] * egc,
                        intra=jnp.where(lower, mm_nt(p["qd"], p["kd"]) * p["decay"], 0.0),
                        ke=p["kd"] * jnp.exp(p["gl_b"] - p["gc_col"]), g_end=p["g_end"]))

    ys = []
    for k in range(n_chunks):
        rows = slice(k * cs, (k + 1) * cs)
        new_states, y_heads = [], []
        for pair in range(2):
            m = mid[2 * k + pair]
            hs = (2 * pair, 2 * pair + 1)
            st = (states[hs[0]], states[hs[1]])
            v_new = m["u"] - jnp.concatenate([mm(m["wm"][:cs], st[0]), mm(m["wm"][cs:], st[1])], axis=0)
            o = jnp.concatenate([mm(m["qe"][:cs], st[0]), mm(m["qe"][cs:], st[1])], axis=0) + mm(m["intra"], v_new)
            new_states.append(st[0] * jnp.exp(m["g_end"][0]) + mm_tn(jnp.where(top, m["ke"], 0.0), v_new))
            new_states.append(st[1] * jnp.exp(m["g_end"][1]) + mm_tn(jnp.where(top, 0.0, m["ke"]), v_new))
            od = o * lax.rsqrt(jnp.mean(o * o, axis=-1, keepdims=True) + EPS) * dn_g
            y_heads += [od[:cs] * _silu(zc[rows, 128 * hs[0]:128 * hs[0] + 128]),
                        od[cs:] * _silu(zc[rows, 128 * hs[1]:128 * hs[1] + 128])]
        states = tuple(new_states)
        ys.append(jnp.concatenate(y_heads, axis=1))
    if with_inverses:
        return (jnp.concatenate(ys, axis=0), *states), tmats
    return (jnp.concatenate(ys, axis=0), *states)


DELTA_ROWS = 4 * DN_CHUNK
DELTA_UNITS = 2 * DELTA_ROWS // DN_CHUNK


def delta_fwd(name, qkv, proj, a_log, dt_bias, dn_g):
    t = qkv.shape[0]
    nc = t // DELTA_ROWS

    def body(qkv_ref, ab_ref, zc_ref, al_ref, dt_ref, g_ref, y_ref, ssave_ref, tsave_ref, s_ref):
        @pl.when(pl.program_id(0) == 0)
        def _():
            s_ref[...] = jnp.zeros_like(s_ref)

        ssave_ref[0] = s_ref[...]
        st = [s_ref[128 * h:128 * h + 128, :] for h in range(4)]
        (y, *ns), tmats = _f_delta_step(qkv_ref[...], ab_ref[...], zc_ref[...], *st, al_ref[...], dt_ref[...], g_ref[...],
                                        with_inverses=True)
        y_ref[...] = y
        for h in range(4):
            s_ref[128 * h:128 * h + 128, :] = ns[h]
        for u, tm in enumerate(tmats):
            tsave_ref[0, 128 * u:128 * u + 128, :] = tm

    return pl.pallas_call(
        body, name=name, grid=(nc,),
        in_specs=[pl.BlockSpec((DELTA_ROWS, 1536), lambda i: (i, 0)), pl.BlockSpec((DELTA_ROWS, 128), lambda i: (i, P_AB // 128)),
                  pl.BlockSpec((DELTA_ROWS, 512), lambda i: (i, P_ZC // 512)),
                  _const_spec((1, 128)), _const_spec((1, 128)), _const_spec((1, 128))],
        out_specs=[pl.BlockSpec((DELTA_ROWS, 512), lambda i: (i, 0)), pl.BlockSpec((1, 512, 128), lambda i: (i, 0, 0)),
                   pl.BlockSpec((1, DELTA_UNITS * 128, 128), lambda i: (i, 0, 0))],
        out_shape=[jax.ShapeDtypeStruct((t, 512), F32), jax.ShapeDtypeStruct((nc, 512, 128), F32),
                   jax.ShapeDtypeStruct((nc, DELTA_UNITS * 128, 128), F32)],
        scratch_shapes=[pltpu.VMEM((512, 128), F32)],
        compiler_params=_cparams(1),
    )(qkv, proj, proj, a_log, dt_bias, dn_g)


def delta_bwd(name, qkv, proj, ssave, tsave, a_log, dt_bias, dn_g, dyc, carry=None):
    t = qkv.shape[0]
    nc = t // DELTA_ROWS
    c_ins, c_in_specs, c_outs, c_out_specs, c_scratch = _host(carry)
    n_ci, n_co = len(c_ins), len(c_outs)

    def body(*refs):
        qkv_ref, ab_ref, zc_ref, ss_ref, ts_ref, al_ref, dt_ref, g_ref, dy_ref = refs[:9]
        dqkv_ref, dab_ref, dzc_ref, dal_ref, ddt_ref, dg_ref = refs[9 + n_ci:15 + n_ci]
        ds_ref = refs[15 + n_ci + n_co]
        carried = (refs[9:9 + n_ci], refs[15 + n_ci:15 + n_ci + n_co], *refs[16 + n_ci + n_co:])

        @pl.when(pl.program_id(0) == 0)
        def _():
            ds_ref[...] = jnp.zeros_like(ds_ref)
            dal_ref[...] = jnp.zeros_like(dal_ref)
            ddt_ref[...] = jnp.zeros_like(ddt_ref)
            dg_ref[...] = jnp.zeros_like(dg_ref)

        if carry is not None:
            carry.emit_start(pl.program_id(0) == 0, *carried)

        st = [ss_ref[0, 128 * h:128 * h + 128, :] for h in range(4)]
        known = [ts_ref[0, 128 * u:128 * u + 128, :] for u in range(DELTA_UNITS)]
        _, vjp = jax.vjp(functools.partial(_f_delta_step, inverses=known), qkv_ref[...], ab_ref[...], zc_ref[...], *st,
                         al_ref[...], dt_ref[...], g_ref[...])
        dst = tuple(ds_ref[128 * h:128 * h + 128, :] for h in range(4))
        dqkv, dab, dzc, d0, d1, d2, d3, dal, ddt, dg = vjp((dy_ref[...], *dst))
        dqkv_ref[...] = dqkv
        dab_ref[...] = dab.astype(dab_ref.dtype)
        dzc_ref[...] = dzc.astype(dzc_ref.dtype)
        for h, d in enumerate((d0, d1, d2, d3)):
            ds_ref[128 * h:128 * h + 128, :] = d
        dal_ref[...] += dal
        ddt_ref[...] += ddt
        dg_ref[...] += dg

        if carry is not None:
            carry.emit_finish(pl.program_id(0) == nc - 1, *carried)

    rev = lambda cb: (lambda j: (nc - 1 - j, cb))
    return pl.pallas_call(
        body, name=name, grid=(nc,),
        in_specs=[pl.BlockSpec((DELTA_ROWS, 1536), rev(0)), pl.BlockSpec((DELTA_ROWS, 128), rev(P_AB // 128)),
                  pl.BlockSpec((DELTA_ROWS, 512), rev(P_ZC // 512)), pl.BlockSpec((1, 512, 128), lambda j: (nc - 1 - j, 0, 0)),
                  pl.BlockSpec((1, DELTA_UNITS * 128, 128), lambda j: (nc - 1 - j, 0, 0)),
                  _const_spec((1, 128)), _const_spec((1, 128)), _const_spec((1, 128)),
                  pl.BlockSpec((DELTA_ROWS, 512), rev(0))] + c_in_specs,
        out_specs=[pl.BlockSpec((DELTA_ROWS, 1536), rev(0)), pl.BlockSpec((DELTA_ROWS, 128), rev(0)),
                   pl.BlockSpec((DELTA_ROWS, 512), rev(0)),
                   _const_spec((1, 128)), _const_spec((1, 128)), _const_spec((1, 128))] + c_out_specs,
        out_shape=[jax.ShapeDtypeStruct((t, 1536), F32), jax.ShapeDtypeStruct((t, 128), BF16),
                   jax.ShapeDtypeStruct((t, 512), BF16),
                   jax.ShapeDtypeStruct((1, 128), F32), jax.ShapeDtypeStruct((1, 128), F32), jax.ShapeDtypeStruct((1, 128), F32)]
        + c_outs,
        scratch_shapes=[pltpu.VMEM((512, 128), F32)] + c_scratch,
        compiler_params=_cparams(1),
    )(qkv, proj, proj, ssave, tsave, a_log, dt_bias, dn_g, dyc, *c_ins)


def loss_head(name, y, target, tm):
    t, d = y.shape

    def body(y_ref, t_ref, dy_ref, l_ref):
        err = y_ref[...] - t_ref[...]
        dy_ref[...] = err * (1.0 / d)
        part = 0.5 * jnp.sum(jnp.sum(err * err, axis=-1, keepdims=True) * (1.0 / d), axis=0, keepdims=True)

        @pl.when(pl.program_id(0) == 0)
        def _():
            l_ref[...] = part

        @pl.when(pl.program_id(0) > 0)
        def _():
            l_ref[...] += part

    return pl.pallas_call(
        body, name=name, grid=(t // tm,),
        in_specs=[_row_spec(tm, d, 0), _row_spec(tm, d, 0)],
        out_specs=[_row_spec(tm, d, 0), _const_spec((1, 1))],
        out_shape=[jax.ShapeDtypeStruct((t, d), F32), jax.ShapeDtypeStruct((1, 1), F32)],
        compiler_params=_cparams(1),
    )(y, target)


TM = 1024
TM_MERGE = 256
TM_IN = 1024
TN_IN = 1152


def _lane_pad(v, n=128):
    return jnp.pad(v.astype(F32), (0, n - v.shape[0]))[None, :]


def f_norm_mod_res(x, g, scale, shift):
    return f_norm_mod(x, g, scale, shift), x


def prep_layer(w):
    p = dict(w)
    p["wp"] = _w_in_assemble(w["w_in"])
    p["wpa"] = _perm_heads_rows(w["w_proj_a"])
    p["dw32"] = jnp.pad(w["dw_w"], ((0, 32 - CONV_K), (0, 0)))
    p["sconv8"] = jnp.pad(w["sconv_w"], ((0, 8 - DN_CONV_K), (0, 0)))
    p["qg"] = jnp.tile(w["q_norm_g"], 2)[None, :]
    p["kg"] = jnp.tile(w["k_norm_g"], 2)[None, :]
    p["sinks128"] = _lane_pad(w["sinks"])
    p["al"] = _lane_pad(w["a_log"])
    p["dtb"] = _lane_pad(w["dt_bias"])
    p["dng"] = w["dn_norm_g"][None, :]
    return p


def layer_fwd(tag, x, mod, p, carry_inproj=None, carry_merge=None):
    d = D_MODEL
    shift, scale, gate = mod[:, :d], mod[:, d:2 * d], mod[:, 2 * d:]
    g = p["norm_g"][None, :]
    h, h_t = rowwise_fwd(f"norm_fwd{tag}", lambda *a: (f_norm_mod(*a),) * 2, [(x, d, 0)], [g, scale, shift],
                         [(d, BF16), (d, BF16, "transposed")], TM)
    proj = matmul_nn(f"inproj_fwd{tag}", h, p["wp"], F32, TM_IN, P_TOTAL // 3, d, carry=carry_inproj)
    proj, got_inproj = (proj, []) if carry_inproj is None else (proj[0], proj[1:])
    ya = attn_fwd(f"attn_fwd{tag}", proj, p["qg"], p["kg"], p["sinks128"])
    ub = glu_conv_fwd(f"glu_conv_fwd{tag}", proj, p["dw32"], p["dw_b"][None, :])
    conf_consts = [p["ln_g"][None, :], p["ln_b"][None, :], p["pw2_w"], p["pw2_b"][None, :]]
    (yb,) = rowwise_fwd(f"conf_fwd{tag}", f_conf_tail, [(ub, 512, 0), (proj, 512, P_ZB // 512)], conf_consts, [(512, F32)], TM)
    qkv = sconv_fwd(f"sconv_fwd{tag}", proj, p["sconv8"])
    yc, ssave, tsave = delta_fwd(f"delta_fwd{tag}", qkv, proj, p["al"], p["dtb"], p["dng"])
    merge_consts = [gate, p["wpa"], p["w_proj_b"], p["w_proj_c"], p["w_out"]]
    merge_rows = [(ya, 512, 0), (yb, 512, 0), (yc, 512, 0), (proj, 3 * d, P_MG // (3 * d)), (x, d, 0)]
    xn, *got_merge = rowwise_fwd(f"merge_fwd{tag}", f_merge, merge_rows, merge_consts, [(d, F32)], 2 * TM_MERGE,
                                 carry=carry_merge)
    saved = dict(x=x, h_t=h_t, proj=proj, ub=ub, qkv=qkv, ssave=ssave, tsave=tsave, norm_consts=[g, scale, shift],
                 conf_consts=conf_consts, merge_consts=merge_consts, merge_rows=merge_rows)
    return xn, saved, got_inproj, got_merge


def layer_bwd(tag, dxn, p, s, carry_merge=None, carry_delta=None, carry_dh=None):
    d = D_MODEL
    proj = s["proj"]
    merge_no_residual = lambda ya, yb, yc, mg, *consts: f_merge(ya, yb, yc, mg, 0.0, *consts)
    dya, dyb, dyc, dmg, dgate, dwpa, dwpb, dwpc, dwout, *got_merge = rowwise_bwd(
        f"merge_bwd{tag}", merge_no_residual, s["merge_rows"][:4], s["merge_consts"], [(dxn, d, 0)], [F32, F32, F32, BF16],
        TM_MERGE, carry=carry_merge)
    carry_delta = None if carry_delta is None else carry_delta(got_merge)
    dqz, dkv, dqg, dkg, dsinks = attn_bwd(f"attn_bwd{tag}", proj, p["qg"], p["kg"], p["sinks128"], dya)
    dub, dzb, dln_g, dln_b, dpw2_w, dpw2_b = rowwise_bwd(
        f"conf_bwd{tag}", f_conf_tail, [(s["ub"], 512, 0), (proj, 512, P_ZB // 512)], s["conf_consts"], [(dyb, 512, 0)],
        [F32, BF16], TM)
    dglu, ddw32, ddw_b = glu_conv_bwd(f"glu_conv_bwd{tag}", proj, p["dw32"], dub)
    dqkv, dab, dzc, dal, ddtb, ddng, *got_delta = delta_bwd(f"delta_bwd{tag}", s["qkv"], proj, s["ssave"], s["tsave"], p["al"],
                                                            p["dtb"], p["dng"], dyc, carry_delta)
    dqkv_pre, dsconv8 = sconv_bwd(f"sconv_bwd{tag}", proj, p["sconv8"], dqkv)
    dproj = jnp.concatenate([dqz, dglu, dzb, dzc, dmg, dqkv_pre, dkv, dab], axis=1)
    dwp = matmul_nn(f"inproj_bwd_dw{tag}", s["h_t"], dproj, F32, d, TN_IN, 2048)
    reduced = dict(w_in=_w_in_grad_blocks(dwp), pw2_w=dpw2_w, w_proj_a=_unperm_heads_rows(dwpa), w_proj_b=dwpb, w_proj_c=dwpc,
                   w_out=dwout)
    carry_dh = None if carry_dh is None else carry_dh(reduced)
    dh = matmul_nn(f"inproj_bwd_dh{tag}", dproj, p["wp"], F32, TM_IN, d, P_TOTAL // 3, b_transposed=True, carry=carry_dh)
    dh, got_dh = (dh, []) if carry_dh is None else (dh[0], dh[1:])
    dx, dnorm_g, dscale, dshift = rowwise_bwd(
        f"norm_bwd{tag}", f_norm_mod_res, [(s["x"], d, 0)], s["norm_consts"], [(dh, d, 0), (dxn, d, 0)], [F32], TM)
    dmod = jnp.concatenate([dshift, dscale, dgate], axis=1)
    grads = dict(
        reduced, b_ada=dmod[0], norm_g=dnorm_g[0],
        q_norm_g=dqg[0, :64] + dqg[0, 64:], k_norm_g=dkg[0, :64] + dkg[0, 64:], sinks=dsinks[0, :ATT_HEADS],
        dw_w=ddw32[:CONV_K], dw_b=ddw_b[0], ln_g=dln_g[0], ln_b=dln_b[0], pw2_b=dpw2_b[0],
        sconv_w=dsconv8[:DN_CONV_K], a_log=dal[0, :DN_HEADS], dt_bias=ddtb[0, :DN_HEADS], dn_norm_g=ddng[0])
    return dx, grads, got_merge, got_delta, got_dh


SHARDED = {"w_ada": 2, "w_in": 2, "dw_w": 2, "pw2_w": 1, "sconv_w": 2, "w_proj_a": 2, "w_proj_b": 2, "w_proj_c": 2,
           "w_out": 1}
GATHERED = tuple(n for n in SHARDED if n != "w_ada")
GATHER_F32 = ("dw_w", "sconv_w")
REDUCE_BIG = tuple(n for n in GATHERED if n not in GATHER_F32)
SMALL = ("b_ada", "norm_g", "q_norm_g", "k_norm_g", "sinks", "dw_b", "ln_g", "ln_b", "pw2_b", "a_log", "dt_bias",
         "dn_norm_g")
SMALL_ROWS = 104
SMALL_GRAD_ROWS = 448
W_IN_SHARD = D_IN // N_CHIPS
SUM_PARTS = 4


def _w_in_orig():
    orig = np.full(P_TOTAL, -1, np.int64)
    p = 0
    for s, n in _in_pieces():
        orig[p:p + n] = np.arange(s, s + n)
        p += n
    return orig


def _w_in_blocks(k):
    orig = _w_in_orig().reshape(-1, 128)
    lo, hi = k * W_IN_SHARD, (k + 1) * W_IN_SHARD
    return [b for b in range(orig.shape[0]) if np.any((orig[b] >= lo) & (orig[b] < hi))]


W_IN_BLOCKS = max(len(_w_in_blocks(k)) for k in range(N_CHIPS))


def _runs(idx):
    out, i = [], 0
    while i < len(idx):
        j = i + 1
        while j < len(idx) and ((idx[i] < 0 and idx[j] < 0) or (idx[i] >= 0 and idx[j] == idx[j - 1] + 1)):
            j += 1
        out.append((int(idx[i]) if idx[i] >= 0 else -1, j - i))
        i = j
    return out


def _take(a, idx):
    parts = [jnp.zeros(a.shape[:-1] + (n,), a.dtype) if s < 0 else a[..., s:s + n] for s, n in _runs(idx)]
    return parts[0] if len(parts) == 1 else jnp.concatenate(parts, axis=-1)


def _w_in_send(k, shard):
    orig = _w_in_orig().reshape(-1, 128)
    lo, hi = k * W_IN_SHARD, (k + 1) * W_IN_SHARD
    idx = np.concatenate([np.where((orig[b] >= lo) & (orig[b] < hi), orig[b] - lo, -1) for b in _w_in_blocks(k)])
    idx = np.concatenate([idx, np.full((W_IN_BLOCKS - len(_w_in_blocks(k))) * 128, -1)])
    return _take(shard, idx)


def _w_in_assemble(blocks):
    where = [{b: i for i, b in enumerate(_w_in_blocks(k))} for k in range(N_CHIPS)]
    n_blocks = P_TOTAL // 128
    owners = [[(k, where[k][b]) for k in range(N_CHIPS) if b in where[k]] for b in range(n_blocks)]
    parts, b = [], 0
    while b < n_blocks:
        if len(owners[b]) == 1:
            k, pos = owners[b][0]
            e = b + 1
            while e < n_blocks and owners[e] == [(k, pos + e - b)]:
                e += 1
            parts.append(blocks[k][:, pos * 128:(pos + e - b) * 128])
            b = e
        else:
            parts.append(functools.reduce(jnp.add, [blocks[k][:, pos * 128:(pos + 1) * 128] for k, pos in owners[b]]))
            b += 1
    return jnp.concatenate(parts, axis=1)


def _w_in_grad_blocks(wp):
    out = []
    for k in range(N_CHIPS):
        idx = np.concatenate([np.arange(128 * b, 128 * b + 128) for b in _w_in_blocks(k)])
        idx = np.concatenate([idx, np.full((W_IN_BLOCKS - len(_w_in_blocks(k))) * 128, -1)])
        out.append(_take(wp, idx))
    return jnp.stack(out)


def _w_in_receive_grad(k, blocks):
    orig = _w_in_orig()
    inv = np.zeros(D_IN, np.int64)
    inv[orig[orig >= 0]] = np.nonzero(orig >= 0)[0]
    where = {b: i for i, b in enumerate(_w_in_blocks(k))}
    cols = inv[k * W_IN_SHARD:(k + 1) * W_IN_SHARD]
    return _take(blocks, np.array([where[c // 128] * 128 + c % 128 for c in cols]))


def _join_layer(v, axis):
    if axis == 2:
        return jnp.transpose(v, (1, 0, 2)).reshape(v.shape[1], N_CHIPS * v.shape[2])
    return v.reshape(N_CHIPS * v.shape[1], v.shape[2])


def _split_layer(v, axis):
    a, b = v.shape
    if axis == 2:
        return jnp.transpose(v.reshape(a, N_CHIPS, b // N_CHIPS), (1, 0, 2))
    return v.reshape(N_CHIPS, a // N_CHIPS, b)


def pack_small(vals, names, rows):
    flat = jnp.concatenate([vals[n].astype(F32).reshape(-1) for n in names])
    return jnp.pad(flat, (0, rows * 128 - flat.shape[0])).reshape(rows, 128)


def unpack_small(packed, names, shapes):
    flat = packed.reshape(-1)
    out, off = {}, 0
    for n in names:
        k = int(np.prod(shapes[n]))
        out[n] = flat[off:off + k].reshape(shapes[n])
        off += k
    return out


ANY = pl.BlockSpec(memory_space=pl.ANY)


def _place():
    x, y, c = lax.axis_index("x"), lax.axis_index("y"), lax.axis_index("c")
    chips = [(1 - x, y), (x, 1 - y), (1 - x, 1 - y)]
    return x, y, c, chips


def _remote(src, dst, send_sem, recv_sem, to):
    return pltpu.make_async_remote_copy(src_ref=src, dst_ref=dst, send_sem=send_sem, recv_sem=recv_sem, device_id=to,
                                        device_id_type=MESH)


class Carry:
    def __init__(self, ins, out_shapes, sems, start, finish, in_place=False):
        self.ins, self.out_shapes, self.sems, self.start, self.finish, self.in_place = (
            list(ins), list(out_shapes), sems, start, finish, in_place)

    def scratch(self):
        return [pltpu.SemaphoreType.DMA(self.sems), pltpu.SemaphoreType.DMA(self.sems)]

    def aliases(self, first_in, first_out):
        return {first_in + i: first_out + i for i in range(len(self.ins))} if self.in_place else {}

    def emit_start(self, first, in_refs, out_refs, send_sems, recv_sems):
        @pl.when(first)
        def _():
            self.start(in_refs, out_refs, send_sems, recv_sems)

    def emit_finish(self, last, in_refs, out_refs, send_sems, recv_sems):
        @pl.when(last)
        def _():
            self.finish(in_refs, out_refs, send_sems, recv_sems)


def _host(carry):
    if carry is None:
        return [], [], [], [], []
    return carry.ins, [ANY] * len(carry.ins), carry.out_shapes, [ANY] * len(carry.out_shapes), carry.scratch()


def run_carry(name, carry):
    n_in, n_out = len(carry.ins), len(carry.out_shapes)

    def body(*refs):
        ins, outs, sems = refs[:n_in], refs[n_in:n_in + n_out], refs[n_in + n_out:]
        carry.start(ins, outs, *sems)
        carry.finish(ins, outs, *sems)

    return pl.pallas_call(
        body, name=name, out_shape=carry.out_shapes, in_specs=[ANY] * n_in, out_specs=[ANY] * n_out,
        input_output_aliases=carry.aliases(0, 0), scratch_shapes=carry.scratch(),
    )(*carry.ins)


def carry_allgather(layer, slots):
    n = len(slots)

    def copies(out, send_sems, recv_sems, only_ici_out=False):
        x, y, c, chips = _place()
        ici_out, ici_in, d2d_out, d2d_in = [], [], [], []
        for j, chip in enumerate(chips):
            for t in range(n):
                mine, land = out[t].at[2 * x + y], out[t].at[2 * chip[0] + chip[1]]
                ici_out.append(_remote(mine, mine, send_sems.at[t, j], recv_sems.at[t, j], (*chip, layer)))
                if only_ici_out:
                    continue
                ici_in.append(_remote(land, land, send_sems.at[t, j], recv_sems.at[t, j], (*chip, layer)))
                d2d_out.append(_remote(land, land, send_sems.at[t, 3 + j], recv_sems.at[t, 3 + j], (x, y, 1 - layer)))
                d2d_in.append(_remote(land, land, send_sems.at[t, 3 + j], recv_sems.at[t, 3 + j], (x, y, layer)))
        return c, ici_out, ici_in, d2d_out, d2d_in

    def start(ins, out, send_sems, recv_sems):
        c, ici_out, _, _, _ = copies(out, send_sems, recv_sems, only_ici_out=True)

        @pl.when(c == layer)
        def _():
            for cp in ici_out:
                cp.start()

    def finish(ins, out, send_sems, recv_sems):
        c, ici_out, ici_in, d2d_out, d2d_in = copies(out, send_sems, recv_sems)

        @pl.when(c == layer)
        def _():
            for arrived, onward in zip(ici_in, d2d_out):
                arrived.wait_recv()
                onward.start()
            for cp in ici_out + d2d_out:
                cp.wait_send()

        @pl.when(c != layer)
        def _():
            for cp in d2d_in:
                cp.wait_recv()

    return Carry(slots, [jax.ShapeDtypeStruct(s.shape, s.dtype) for s in slots], (n, 6), start, finish, in_place=True)


def carry_pair_send(layer, gs):
    def copies(g, recv, send_sems, recv_sems):
        x, y, c, _ = _place()
        return c, [_remote(g[t], recv[t], send_sems.at[t], recv_sems.at[t], (x, y, 1 - c)) for t in range(len(gs))]

    def start(g, recv, send_sems, recv_sems):
        c, cps = copies(g, recv, send_sems, recv_sems)

        @pl.when(c != layer)
        def _():
            for cp in cps:
                cp.start()

    def finish(g, recv, send_sems, recv_sems):
        c, cps = copies(g, recv, send_sems, recv_sems)

        @pl.when(c != layer)
        def _():
            for cp in cps:
                cp.wait_send()

        @pl.when(c == layer)
        def _():
            for cp in cps:
                cp.wait_recv()

    return Carry(gs, [jax.ShapeDtypeStruct(g.shape, g.dtype) for g in gs], (len(gs),), start, finish)


def grads_pair_sums(layer, gs, recv):
    n = len(gs)

    def body(*refs):
        for t in range(n):
            refs[2 * n + t][...] = (refs[t][...] + refs[n + t][...]).astype(BF16)

    specs = [pl.BlockSpec((None, g.shape[1] // SUM_PARTS, g.shape[2]), lambda s, i: (s, i, 0)) for g in gs]
    return pl.pallas_call(
        body, name=f"grads_pair_sums{layer}", grid=(N_CHIPS, SUM_PARTS), in_specs=specs + specs, out_specs=specs,
        out_shape=[jax.ShapeDtypeStruct(g.shape, BF16) for g in gs], compiler_params=_cparams(2),
    )(*gs, *recv)


def carry_chip_exchange(layer, ps):
    def copies(p, recv, send_sems, recv_sems):
        _, _, c, chips = _place()
        return c, [_remote(p[t].at[2 * chip[0] + chip[1]], recv[t].at[j], send_sems.at[t, j], recv_sems.at[t, j],
                           (*chip, layer)) for j, chip in enumerate(chips) for t in range(len(ps))]

    def start(p, recv, send_sems, recv_sems):
        c, cps = copies(p, recv, send_sems, recv_sems)

        @pl.when(c == layer)
        def _():
            for cp in cps:
                cp.start()

    def finish(p, recv, send_sems, recv_sems):
        c, cps = copies(p, recv, send_sems, recv_sems)

        @pl.when(c == layer)
        def _():
            for cp in cps:
                cp.wait()

    return Carry(ps, [jax.ShapeDtypeStruct((3,) + p.shape[1:], p.dtype) for p in ps], (len(ps), 3), start, finish)


def grads_chip_sums(layer, gs, recv, recv2, into=None):
    n = len(gs)
    my_slot = lambda: 2 * lax.axis_index("x") + lax.axis_index("y")

    def body(*refs):
        outs = refs[-n:]
        for t in range(n):
            r2 = refs[2 * n + t]
            own = refs[t][...] + refs[n + t][...]
            outs[t][...] = ((own + r2[0].astype(F32)) + r2[1].astype(F32)) + r2[2].astype(F32)

    part = lambda g: g.shape[1] // SUM_PARTS
    own_specs = [pl.BlockSpec((None, part(g), g.shape[2]), lambda i: (my_slot(), i, 0)) for g in gs]
    return pl.pallas_call(
        body, name=f"grads_chip_sums{layer}", grid=(SUM_PARTS,),
        in_specs=own_specs + own_specs + [pl.BlockSpec((3, part(g), g.shape[2]), lambda i: (0, i, 0)) for g in gs]
        + ([] if into is None else [ANY] * n),
        out_specs=[pl.BlockSpec((None, part(g), g.shape[2]), lambda i: (layer, i, 0)) for g in gs],
        out_shape=[jax.ShapeDtypeStruct((DEPTH,) + g.shape[1:], F32) for g in gs],
        input_output_aliases={} if into is None else {3 * n + t: t for t in range(n)},
        compiler_params=_cparams(1),
    )(*gs, *recv, *recv2, *([] if into is None else into))


def grads_pair_gather(reds):
    n = len(reds)

    def body(*refs):
        buf = refs[n:2 * n]
        send_sems, recv_sems = refs[2 * n:]
        x, y, c, _ = _place()
        sibling = (x, y, 1 - c)
        cps = [_remote(buf[t].at[c], buf[t].at[c], send_sems.at[t], recv_sems.at[t], sibling) for t in range(n)]
        for cp in cps:
            cp.start()
        for t in range(n):
            _remote(buf[t].at[c], buf[t].at[1 - c], send_sems.at[t], recv_sems.at[t], sibling).wait_recv()
        for cp in cps:
            cp.wait_send()

    return pl.pallas_call(
        body, name="grads_pair_gather", out_shape=[jax.ShapeDtypeStruct(r.shape, r.dtype) for r in reds],
        in_specs=[ANY] * n, out_specs=[ANY] * n, input_output_aliases={t: t for t in range(n)},
        scratch_shapes=[pltpu.SemaphoreType.DMA((n,)), pltpu.SemaphoreType.DMA((n,))],
    )(*reds)


def small_allreduce(v):
    m, n = v.shape

    def body(x_ref, sum_ref, all_ref, send_sems, recv_sems, local_sem):
        x, y, c, chips = _place()
        me, sibling = (x, y, c), (x, y, 1 - c)

        def rows(px, py, pc):
            return all_ref.at[pl.ds((4 * px + 2 * py + pc) * m, m), :]

        def copy(k, block, to, src=None):
            return pltpu.make_async_remote_copy(src_ref=rows(*block) if src is None else src, dst_ref=rows(*block),
                                                send_sem=send_sems.at[k], recv_sem=recv_sems.at[k],
                                                device_id=to, device_id_type=MESH)

        mine = pltpu.make_async_copy(x_ref, rows(*me), local_sem)
        mine.start()
        first = [copy(0, me, sibling, src=x_ref)]
        first += [copy(1 + j, me, (*chip, c), src=x_ref) for j, chip in enumerate(chips)]
        for cp in first:
            cp.start()
        passed = [copy(4 + j, (*chip, c), sibling) for j, chip in enumerate(chips)]
        for j, chip in enumerate(chips):
            copy(1 + j, (*chip, c), me).wait_recv()
            passed[j].start()
        copy(0, sibling, me).wait_recv()
        for j, chip in enumerate(chips):
            copy(4 + j, (*chip, 1 - c), me).wait_recv()
        for cp in first + passed:
            cp.wait_send()
        mine.wait()
        acc = all_ref[0:m, :]
        for dev in range(1, 8):
            acc = acc + all_ref[dev * m:(dev + 1) * m, :]
        sum_ref[...] = acc

    vm = pl.BlockSpec(memory_space=pltpu.VMEM)
    return pl.pallas_call(
        body, name="small_allreduce",
        out_shape=[jax.ShapeDtypeStruct((m, n), F32), jax.ShapeDtypeStruct((8 * m, n), F32)],
        in_specs=[vm], out_specs=[vm, vm],
        scratch_shapes=[pltpu.SemaphoreType.DMA((7,)), pltpu.SemaphoreType.DMA((7,)), pltpu.SemaphoreType.DMA],
    )(v)


def grads_by_chip(layer_grads):
    return [layer_grads[n] if n == "w_in" else _split_layer(layer_grads[n], SHARDED[n]) for n in REDUCE_BIG]


def _adamw_block(w_ref, g_ref, m_ref, v_ref, d_ref, nm_ref, nv_ref):
    gv = g_ref[...]
    nm = ADAM_B1 * m_ref[...] + (1.0 - ADAM_B1) * gv
    nv = ADAM_B2 * v_ref[...] + (1.0 - ADAM_B2) * (gv * gv)
    m_hat = nm / (1.0 - ADAM_B1 ** ADAM_STEP)
    v_hat = nv / (1.0 - ADAM_B2 ** ADAM_STEP)
    d_ref[...] = -ADAM_LR * (m_hat / (jnp.sqrt(v_hat) + ADAM_EPS) + ADAM_WD * w_ref[...])
    nm_ref[...] = nm
    nv_ref[...] = nv


def adamw(name, w, g, m, v, block):
    grid = tuple(s // b for s, b in zip(w.shape, block))

    def body(*refs):
        _adamw_block(*refs)

    spec = pl.BlockSpec(tuple(block), lambda *idx: idx)
    return pl.pallas_call(
        body, name=name, grid=grid, in_specs=[spec] * 4, out_specs=[spec] * 3,
        out_shape=[jax.ShapeDtypeStruct(w.shape, F32)] * 3, compiler_params=_cparams(len(grid)),
    )(w, g, m, v)


def adamw_many(name, groups):
    n = len(groups)

    def spec(a):
        rows, cols = a.shape
        if rows % (8 * ADAM_PARTS) == 0:
            return pl.BlockSpec((rows // ADAM_PARTS, cols), lambda i: (i, 0))
        return pl.BlockSpec((rows, cols), lambda i: (0, 0))

    def body(*refs):
        for t in range(n):
            _adamw_block(*refs[4 * t:4 * t + 4], *refs[4 * n + 3 * t:4 * n + 3 * t + 3])

    res = pl.pallas_call(
        body, name=name, grid=(ADAM_PARTS,),
        in_specs=[spec(grp[0]) for grp in groups for _ in range(4)],
        out_specs=[spec(grp[0]) for grp in groups for _ in range(3)],
        out_shape=[jax.ShapeDtypeStruct(grp[0].shape, F32) for grp in groups for _ in range(3)],
        compiler_params=_cparams(1),
    )(*[a for grp in groups for a in grp])
    return [tuple(res[3 * t:3 * t + 3]) for t in range(n)]


ADAM_PARTS = 4
ADAM_W_IN_COLS = 331

WEIGHT_NAMES = ("w_ada", "b_ada", "norm_g", "w_in", "q_norm_g", "k_norm_g", "sinks", "dw_w", "dw_b", "ln_g", "ln_b",
                "pw2_w", "pw2_b", "sconv_w", "a_log", "dt_bias", "dn_norm_g", "w_proj_a", "w_proj_b", "w_proj_c", "w_out")


def kernel(x, c, w_ada, b_ada, norm_g, w_in, q_norm_g, k_norm_g, sinks, dw_w, dw_b, ln_g, ln_b, pw2_w, pw2_b, sconv_w, a_log, dt_bias, dn_norm_g, w_proj_a, w_proj_b, w_proj_c, w_out, loss_target, m_w_ada, m_b_ada, m_norm_g, m_w_in, m_q_norm_g, m_k_norm_g, m_sinks, m_dw_w, m_dw_b, m_ln_g, m_ln_b, m_pw2_w, m_pw2_b, m_sconv_w, m_a_log, m_dt_bias, m_dn_norm_g, m_w_proj_a, m_w_proj_b, m_w_proj_c, m_w_out, v_w_ada, v_b_ada, v_norm_g, v_w_in, v_q_norm_g, v_k_norm_g, v_sinks, v_dw_w, v_dw_b, v_ln_g, v_ln_b, v_pw2_w, v_pw2_b, v_sconv_w, v_a_log, v_dt_bias, v_dn_norm_g, v_w_proj_a, v_w_proj_b, v_w_proj_c, v_w_out):
    args = dict(locals())
    w = {n: args[n] for n in WEIGHT_NAMES}
    mom = {n: args["m_" + n] for n in WEIGHT_NAMES}
    var = {n: args["v_" + n] for n in WEIGHT_NAMES}

    chip = 2 * lax.axis_index("x") + lax.axis_index("y")
    own = {n: w[n] if n in GATHER_F32 else w[n].astype(BF16) for n in GATHERED}
    own["w_in"] = lax.switch(chip, [functools.partial(_w_in_send, k) for k in range(N_CHIPS)], own["w_in"])
    slots = [[lax.dynamic_update_slice(lax.empty((N_CHIPS,) + own[n].shape[1:], own[n].dtype), own[n][l][None], (chip, 0, 0))
              for n in GATHERED] for l in range(DEPTH)]

    def layer_operands(l, gathered):
        lw = {n: w[n][l] for n in SMALL}
        lw.update({n: g if n == "w_in" else _join_layer(g, SHARDED[n]) for n, g in zip(GATHERED, gathered)})
        return prep_layer(lw)

    layers = [layer_operands(0, run_carry("weights_allgather0", carry_allgather(0, slots[0]))), None]

    mod, conds = ada_fwd(jnp.tile(c, (8, 1)), w["w_ada"], w["b_ada"])
    saved = [None] * DEPTH
    big = GATHERED.index("w_in")
    rest = [i for i in range(len(GATHERED)) if i != big]
    act, saved[0], got_big, got_rest = layer_fwd(
        "0", x[0], mod[0:1], layers[0], carry_inproj=carry_allgather(1, [slots[1][big]]),
        carry_merge=carry_allgather(1, [slots[1][i] for i in rest]))
    gathered1 = dict(zip(rest, got_rest))
    gathered1[big] = got_big[0]
    layers[1] = layer_operands(1, [gathered1[i] for i in range(len(GATHERED))])
    act, saved[1], _, _ = layer_fwd("1", act, mod[1:2], layers[1])
    dact, loss_part = loss_head("loss_head", act, loss_target[0], TM)
    loss = lax.psum(loss_part[0, 0], ("x", "y", "c"))
    layer_grads = [None] * DEPTH
    dact, layer_grads[1], _, _, _ = layer_bwd("1", dact, layers[1], saved[1])
    gs1 = grads_by_chip(layer_grads[1])
    gs0 = []

    def hand_over_layer0(reduced):
        gs0.extend(grads_by_chip(reduced))
        return carry_pair_send(0, gs0)

    dact, layer_grads[0], recv1, got1, recv0 = layer_bwd(
        "0", dact, layers[0], saved[0], carry_merge=carry_pair_send(1, gs1),
        carry_delta=lambda recv: carry_chip_exchange(1, grads_pair_sums(1, gs1, recv)), carry_dh=hand_over_layer0)

    got0 = run_carry("grads_chip_exchange0", carry_chip_exchange(0, grads_pair_sums(0, gs0, recv0)))
    reds = grads_chip_sums(0, gs0, recv0, got0, into=grads_chip_sums(1, gs1, recv1, got1))
    final_grads = dict(zip(REDUCE_BIG, grads_pair_gather(reds)))
    final_grads["w_in"] = lax.switch(chip, [functools.partial(_w_in_receive_grad, k) for k in range(N_CHIPS)],
                                     final_grads["w_in"])
    small_names = SMALL + GATHER_F32
    small_shapes = {n: (DEPTH,) + layer_grads[0][n].shape for n in small_names}
    small_full = {n: jnp.stack([layer_grads[l][n] for l in range(DEPTH)]) for n in small_names}
    small_sum, small_all = small_allreduce(pack_small(small_full, small_names, SMALL_GRAD_ROWS))
    small_sum = unpack_small(small_sum, small_names, small_shapes)
    for n in GATHER_F32:
        width = w[n].shape[2]
        final_grads[n] = lax.dynamic_slice_in_dim(small_sum[n], chip * width, width, axis=2)
    n_mod = DEPTH * 3 * D_MODEL
    dmod = small_all.reshape(8, -1)[:, :n_mod].reshape(8, DEPTH, 3 * D_MODEL)
    width = w["w_ada"].shape[2]
    dmod = jnp.transpose(lax.dynamic_slice_in_dim(dmod, chip * width, width, axis=2), (1, 0, 2))
    final_grads["w_ada"] = ada_bwd(conds, dmod)
    final_grads.update({n: small_sum[n] for n in SMALL})
    small_grads = pack_small(final_grads, SMALL, SMALL_ROWS)

    delta, new_m, new_v = {}, {}, {}
    shp = w["w_in"].shape
    view = lambda a: jnp.transpose(a, (2, 0, 1))
    back = lambda a: jnp.transpose(a, (1, 2, 0))
    g3 = view(final_grads["w_in"])
    final_grads["w_in"] = back(g3)
    d, nm, nv = adamw("adamw_w_in", view(w["w_in"]), g3, view(mom["w_in"]), view(var["w_in"]),
                      (ADAM_W_IN_COLS, shp[0], shp[1]))
    delta["w_in"], new_m["w_in"], new_v["w_in"] = back(d), back(nm), back(nv)
    others = [n for n in SHARDED if n != "w_in"]
    two_d = lambda a: a.reshape(a.shape[0] * a.shape[1], a.shape[2])
    groups = [tuple(two_d(t[n]) for t in (w, final_grads, mom, var)) for n in others]
    groups.append((pack_small(w, SMALL, SMALL_ROWS), small_grads, pack_small(mom, SMALL, SMALL_ROWS),
                   pack_small(var, SMALL, SMALL_ROWS)))
    results = adamw_many("adamw_rest", groups)
    for n, (d, nm, nv) in zip(others, results):
        delta[n], new_m[n], new_v[n] = (a.reshape(w[n].shape) for a in (d, nm, nv))
    for out, packed in zip((delta, new_m, new_v), results[-1]):
        out.update(unpack_small(packed, SMALL, small_shapes))

    return (loss, dact[None], *[final_grads[n] for n in WEIGHT_NAMES], *[delta[n] for n in WEIGHT_NAMES],
            *[new_m[n] for n in WEIGHT_NAMES], *[new_v[n] for n in WEIGHT_NAMES])
```

```python
import functools

import numpy as np
import jax
import jax.numpy as jnp
from jax import lax
from jax.experimental import pallas as pl
from jax.experimental.pallas import tpu as pltpu

F32 = jnp.float32
BF16 = jnp.bfloat16
MESH = pl.DeviceIdType.MESH

D_MODEL = 1024
DEPTH = 2
ATT_HEADS = 8
ATT_HEAD_DIM = 64
WINDOW = 128
CONV_K = 31
DN_HEADS = 4
DN_CONV_K = 4
DN_CHUNK = 64
EPS = 1e-6
NEG_INF = -1e30
N_CHIPS = 4
D_IN = 7944

ADAM_LR = 0.001
ADAM_B1 = 0.9
ADAM_B2 = 0.999
ADAM_EPS = 1e-08
ADAM_WD = 0.01
ADAM_STEP = 10

VMEM_LIMIT = 56 * 1024 * 1024

P_QA, P_ZA, P_GLU, P_ZB, P_ZC, P_MG, P_QKV, P_KA, P_VA, P_AB, P_TOTAL = (
    0, 512, 1024, 2048, 2560, 3072, 6144, 7680, 7808, 7936, 8064)
HEAD_ORDER = (0, 4, 1, 5, 2, 6, 3, 7)


def _in_pieces():
    p = [(0 + 64 * h, 64) for h in HEAD_ORDER]
    p += [(768 + 64 * h, 64) for h in HEAD_ORDER]
    for g in range(4):
        p += [(1280 + 128 * g, 128), (1792 + 128 * g, 128)]
    p += [(2304, 512), (4360, 512), (4872, 3072), (2816, 1536), (512, 128), (640, 128), (4352, 8)]
    return p


def _perm_heads_rows(w):
    return jnp.concatenate([w[64 * h:64 * h + 64] for h in HEAD_ORDER], axis=0)


def _unperm_heads_rows(w):
    inv = [HEAD_ORDER.index(h) for h in range(8)]
    return jnp.concatenate([w[64 * s:64 * s + 64] for s in inv], axis=0)


def _split_bf16(a, terms):
    out, rest = [], a.astype(F32)
    for _ in range(terms - 1):
        out.append(rest.astype(BF16))
        rest = rest - out[-1].astype(F32)
    return out + [rest.astype(BF16)]


def _dot(a, b, dims, exact):
    d = lambda p, q: lax.dot_general(p, q, (dims, ((), ())), preferred_element_type=F32)
    if exact:
        (ah, al), (bh, bl) = _split_bf16(a, 2), _split_bf16(b, 2)
        return d(ah, bh) + (d(ah, bl) + d(al, bh))
    return d(a.astype(BF16), b.astype(BF16))


def _make_mm(exact):
    @jax.custom_vjp
    def nn(a, b):
        return _dot(a, b, ((1,), (0,)), exact)

    @jax.custom_vjp
    def nt(a, b):
        return _dot(a, b, ((1,), (1,)), exact)

    @jax.custom_vjp
    def tn(a, b):
        return _dot(a, b, ((0,), (0,)), exact)

    nn.defvjp(lambda a, b: (nn(a, b), (a, b)),
              lambda r, g: (nt(g, r[1]).astype(r[0].dtype), tn(r[0], g).astype(r[1].dtype)))
    nt.defvjp(lambda a, b: (nt(a, b), (a, b)),
              lambda r, g: (nn(g, r[1]).astype(r[0].dtype), tn(g, r[0]).astype(r[1].dtype)))
    tn.defvjp(lambda a, b: (tn(a, b), (a, b)),
              lambda r, g: (nt(r[1], g).astype(r[0].dtype), nn(r[0], g).astype(r[1].dtype)))
    return nn, nt, tn


mm, mm_nt, mm_tn = _make_mm(False)
xmm, xmm_nt, xmm_tn = _make_mm(True)


@jax.custom_vjp
def sel_mm(m, g):
    mb = m.astype(BF16)
    parts = [jnp.dot(mb, p, preferred_element_type=F32) for p in _split_bf16(g, 3)]
    return parts[0] + (parts[1] + parts[2])


def _sel_mm_bwd(m, dy):
    mb = m.astype(BF16)
    parts = [lax.dot_general(mb, p, (((0,), (0,)), ((), ())), preferred_element_type=F32) for p in _split_bf16(dy, 3)]
    return jnp.zeros_like(m), parts[0] + (parts[1] + parts[2])


sel_mm.defvjp(lambda m, g: (sel_mm(m, g), m), _sel_mm_bwd)


@jax.custom_vjp
def tri_inv(*mats):
    n = mats[0].shape[0]
    eye = jnp.where(lax.broadcasted_iota(jnp.int32, (n, n), 0) == lax.broadcasted_iota(jnp.int32, (n, n), 1), 1.0, 0.0)
    ts = [eye - a for a in mats]
    pws = list(mats)
    for _ in range(5):
        pws = [xmm(pw, pw) for pw in pws]
        ts = [t + xmm(t, pw) for t, pw in zip(ts, pws)]
    return tuple(ts)


def _tri_inv_bwd(ts, dts):
    inner = [xmm_nt(dt, t) for t, dt in zip(ts, dts)]
    return tuple(-xmm_tn(t, m) for t, m in zip(ts, inner))


tri_inv.defvjp(lambda *mats: (tri_inv(*mats),) * 2, _tri_inv_bwd)


@jax.custom_vjp
def tri_inv_known(a, t):
    return t


tri_inv_known.defvjp(lambda a, t: (t, t), lambda t, dt: (_tri_inv_bwd((t,), (dt,))[0], jnp.zeros_like(t)))


def _sigmoid(x):
    return 1.0 / (1.0 + jnp.exp(-x))


def _silu(x):
    return x * _sigmoid(x)


def _softplus(x):
    return jnp.maximum(x, 0.0) + jnp.log(1.0 + jnp.exp(-jnp.abs(x)))


def _cparams(n_grid):
    return pltpu.CompilerParams(dimension_semantics=("arbitrary",) * n_grid, vmem_limit_bytes=VMEM_LIMIT)


def _row_spec(tm, width, colblk):
    return pl.BlockSpec((tm, width), lambda i, cb=colblk: (i, cb))


def _const_spec(shape):
    nd = len(shape)
    return pl.BlockSpec(tuple(shape), lambda i, nd=nd: (0,) * nd)


def rowwise_fwd(name, f, rows, consts, outs, tm, carry=None):
    n_r, n_c = len(rows), len(consts)
    t = rows[0][0].shape[0]
    c_ins, c_in_specs, c_outs, c_out_specs, c_scratch = _host(carry)
    n_in, n_ci, n_co = n_r + n_c, len(c_ins), len(c_outs)

    def body(*refs):
        carried = (refs[n_in:n_in + n_ci], refs[n_in + n_ci + len(outs):n_in + n_ci + len(outs) + n_co],
                   *refs[n_in + n_ci + len(outs) + n_co:])
        if carry is not None:
            carry.emit_start(pl.program_id(0) == 0, *carried)
        vals = [r[...] for r in refs[:n_in]]
        res = f(*vals)
        if not isinstance(res, (tuple, list)):
            res = (res,)
        for o_ref, v, out in zip(refs[n_in + n_ci:n_in + n_ci + len(outs)], res, outs):
            o_ref[...] = (v.T if len(out) == 3 else v).astype(o_ref.dtype)
        if carry is not None:
            carry.emit_finish(pl.program_id(0) == t // tm - 1, *carried)

    return pl.pallas_call(
        body, name=name, grid=(t // tm,),
        in_specs=[_row_spec(tm, w, cb) for _, w, cb in rows] + [_const_spec(c.shape) for c in consts] + c_in_specs,
        out_specs=[_row_spec(tm, o[0], 0) if len(o) == 2 else pl.BlockSpec((o[0], tm), lambda i: (0, i)) for o in outs]
        + c_out_specs,
        out_shape=[jax.ShapeDtypeStruct((t, o[0]) if len(o) == 2 else (o[0], t), o[1]) for o in outs] + c_outs,
        input_output_aliases={} if carry is None else carry.aliases(n_in, len(outs)),
        scratch_shapes=c_scratch,
        compiler_params=_cparams(1),
    )(*[a for a, _, _ in rows], *consts, *c_ins)


def rowwise_bwd(name, f, rows, consts, cts, row_grad_dtypes, tm, carry=None):
    n_r, n_c, n_ct = len(rows), len(consts), len(cts)
    t = rows[0][0].shape[0]
    keep = [k for k, dt in enumerate(row_grad_dtypes) if dt is not None]
    c_ins, c_in_specs, c_outs, c_out_specs, c_scratch = _host(carry)
    n_in, n_out = n_r + n_c + n_ct, len(keep) + n_c

    def body(*refs):
        ins = [r[...].astype(F32) for r in refs[:n_r + n_c]]
        g_out = [r[...].astype(F32) for r in refs[n_r + n_c:n_in]]
        out_refs = refs[n_in + len(c_ins):n_in + len(c_ins) + n_out]
        carried = (refs[n_in:n_in + len(c_ins)], refs[n_in + len(c_ins) + n_out:n_in + len(c_ins) + n_out + len(c_outs)],
                   *refs[n_in + len(c_ins) + n_out + len(c_outs):])
        if carry is not None:
            carry.emit_start(pl.program_id(0) == 0, *carried)

        def fw(*a):
            res = f(*a)
            return tuple(res) if isinstance(res, (tuple, list)) else (res,)

        _, vjp = jax.vjp(fw, *ins)
        grads = vjp(tuple(g_out))
        for o_ref, k in zip(out_refs[:len(keep)], keep):
            o_ref[...] = grads[k].astype(o_ref.dtype)
        first = pl.program_id(0) == 0
        for o_ref, g in zip(out_refs[len(keep):], grads[n_r:]):
            @pl.when(first)
            def _(o_ref=o_ref, g=g):
                o_ref[...] = g

            @pl.when(jnp.logical_not(first))
            def _(o_ref=o_ref, g=g):
                o_ref[...] += g
        if carry is not None:
            carry.emit_finish(pl.program_id(0) == t // tm - 1, *carried)

    return pl.pallas_call(
        body, name=name, grid=(t // tm,),
        in_specs=[_row_spec(tm, w, cb) for _, w, cb in rows] + [_const_spec(c.shape) for c in consts]
        + [_row_spec(tm, w, cb) for _, w, cb in cts] + c_in_specs,
        out_specs=[_row_spec(tm, rows[k][1], 0) for k in keep] + [_const_spec(c.shape) for c in consts] + c_out_specs,
        out_shape=[jax.ShapeDtypeStruct((t, rows[k][1]), row_grad_dtypes[k]) for k in keep]
        + [jax.ShapeDtypeStruct(c.shape, F32) for c in consts] + c_outs,
        scratch_shapes=c_scratch,
        compiler_params=_cparams(1),
    )(*[a for a, _, _ in rows], *consts, *[a for a, _, _ in cts], *c_ins)


def f_norm_mod(x, g, scale, shift):
    y = x * lax.rsqrt(jnp.mean(x * x, axis=-1, keepdims=True) + EPS) * g
    return y * (1.0 + scale) + shift


def f_conf_tail(u, zb, ln_g, ln_b, pw2_w, pw2_b):
    mu = jnp.mean(u, axis=-1, keepdims=True)
    xc = u - mu
    var = jnp.mean(xc * xc, axis=-1, keepdims=True)
    y = _silu(xc * lax.rsqrt(var + EPS) * ln_g + ln_b)
    return (mm(y, pw2_w) + pw2_b) * _silu(zb)


def f_merge(ya, yb, yc, mg, x, gate, wpa, wpb, wpc, wout):
    d = D_MODEL
    merged = (_sigmoid(mg[:, :d]) * mm(ya, wpa) + _sigmoid(mg[:, d:2 * d]) * mm(yb, wpb)
              + _sigmoid(mg[:, 2 * d:]) * mm(yc, wpc))
    return x + gate * mm(merged, wout)


def matmul_nn(name, a, b, out_dtype, tm, tn, tk, b_transposed=False, carry=None):
    m, k = a.shape
    n = b.shape[0] if b_transposed else b.shape[1]
    nk = k // tk
    grid = (m // tm, n // tn, nk)
    b_spec = (pl.BlockSpec((tn, tk), lambda i, j, kk: (j, kk)) if b_transposed
              else pl.BlockSpec((tk, tn), lambda i, j, kk: (kk, j)))
    c_ins, c_in_specs, c_outs, c_out_specs, c_scratch = _host(carry)
    n_ci, n_co = len(c_ins), len(c_outs)

    def body(*refs):
        a_ref, b_ref, o_ref = refs[0], refs[1], refs[2 + n_ci]
        carried = (refs[2:2 + n_ci], refs[3 + n_ci:3 + n_ci + n_co], *refs[3 + n_ci + n_co:3 + n_ci + n_co + len(c_scratch)])
        at = lambda step: functools.reduce(jnp.logical_and, [pl.program_id(d) == s for d, s in enumerate(step)])
        if carry is not None:
            carry.emit_start(at((0, 0, 0)), *carried)
        part = lax.dot_general(a_ref[...].astype(BF16), b_ref[...].astype(BF16),
                               (((1,), (1 if b_transposed else 0,)), ((), ())), preferred_element_type=F32)
        if nk == 1:
            o_ref[...] = part.astype(o_ref.dtype)
        else:
            kk = pl.program_id(2)
            acc_ref = refs[-1]

            @pl.when(kk == 0)
            def _():
                acc_ref[...] = part

            @pl.when(kk > 0)
            def _():
                acc_ref[...] += part

            @pl.when(kk == nk - 1)
            def _():
                o_ref[...] = acc_ref[...].astype(o_ref.dtype)
        if carry is not None:
            carry.emit_finish(at(tuple(g - 1 for g in grid)), *carried)

    res = pl.pallas_call(
        body, name=name, grid=grid,
        in_specs=[pl.BlockSpec((tm, tk), lambda i, j, kk: (i, kk)), b_spec] + c_in_specs,
        out_specs=[pl.BlockSpec((tm, tn), lambda i, j, kk: (i, j))] + c_out_specs,
        out_shape=[jax.ShapeDtypeStruct((m, n), out_dtype)] + c_outs,
        input_output_aliases={} if carry is None else carry.aliases(2, 1),
        scratch_shapes=c_scratch + ([] if nk == 1 else [pltpu.VMEM((tm, tn), F32)]),
        compiler_params=_cparams(3),
    )(a, b, *c_ins)
    return res[0] if carry is None else res


def ada_fwd(c8, w_shard, b_ada):
    n_cols = w_shard.shape[2]
    masks = [(m >> 2 & 1, m >> 1 & 1, m & 1) for m in range(1, 8)]

    def body(c_ref, w_ref, b_ref, mod_ref, conds_ref, cbuf, sendbuf, recvbuf, send_sems, recv_sems):
        x, y, c, chips = _place()
        flip = lambda v, bit: 1 - v if bit else v
        peers = [(flip(x, mx), flip(y, my), flip(c, mc)) for mx, my, mc in masks]
        dev = lambda p: 4 * p[0] + 2 * p[1] + p[2]
        cbuf[dev((x, y, c))] = c_ref[...]
        first = [_remote(c_ref, cbuf.at[dev((x, y, c))], send_sems.at[i], recv_sems.at[i], p) for i, p in enumerate(peers)]
        for cp in first:
            cp.start()
        for i, p in enumerate(peers):
            _remote(c_ref, cbuf.at[dev(p)], send_sems.at[i], recv_sems.at[i], p).wait_recv()
        conds = jnp.concatenate([cbuf[d, 0:1, :] for d in range(8)], axis=0)
        conds_ref[...] = conds
        act = _silu(conds)
        parts = [mm(act, w_ref[l]) for l in range(DEPTH)]
        row8 = lax.broadcasted_iota(jnp.int32, (8, 1), 0)

        def tile_for(chip):
            r = 2 * (2 * chip[0] + chip[1]) + c
            rows = [jnp.sum(jnp.where(row8 == r, parts[l], 0.0), axis=0, keepdims=True) for l in range(DEPTH)]
            return jnp.where(row8 == 0, rows[0], jnp.where(row8 == 1, rows[1], 0.0))

        my_slot = 2 * x + y
        recvbuf[my_slot] = tile_for((x, y))
        second = []
        for j, chip in enumerate(chips):
            sendbuf[j] = tile_for(chip)
            second.append(_remote(sendbuf.at[j], recvbuf.at[my_slot], send_sems.at[7 + j], recv_sems.at[7 + j], (*chip, c)))
            second[-1].start()
        for j, chip in enumerate(chips):
            _remote(sendbuf.at[j], recvbuf.at[2 * chip[0] + chip[1]], send_sems.at[7 + j], recv_sems.at[7 + j],
                    (*chip, c)).wait_recv()
        rows = [jnp.concatenate([recvbuf[k, l:l + 1, :] for k in range(N_CHIPS)], axis=1) + b_ref[l:l + 1, :]
                for l in range(DEPTH)]
        mod_ref[...] = jnp.concatenate(rows + [jnp.zeros((8 - DEPTH, N_CHIPS * n_cols), F32)], axis=0)
        for cp in first + second:
            cp.wait_send()

    vm = pl.BlockSpec(memory_space=pltpu.VMEM)
    return pl.pallas_call(
        body, name="ada_fwd",
        out_shape=[jax.ShapeDtypeStruct((8, N_CHIPS * n_cols), F32), jax.ShapeDtypeStruct((8, D_MODEL), F32)],
        in_specs=[vm, vm, vm], out_specs=[vm, vm],
        scratch_shapes=[pltpu.VMEM((8, 8, D_MODEL), F32), pltpu.VMEM((3, 8, n_cols), F32),
                        pltpu.VMEM((N_CHIPS, 8, n_cols), F32), pltpu.SemaphoreType.DMA((10,)), pltpu.SemaphoreType.DMA((10,))],
        compiler_params=pltpu.CompilerParams(vmem_limit_bytes=VMEM_LIMIT),
    )(c8, w_shard, b_ada)


def ada_bwd(conds, dmod):
    def body(c_ref, d_ref, o_ref):
        act = _silu(c_ref[...])
        for l in range(DEPTH):
            o_ref[l] = mm_tn(act, d_ref[l])

    return pl.pallas_call(
        body, name="ada_bwd", out_shape=jax.ShapeDtypeStruct((DEPTH, D_MODEL, dmod.shape[2]), F32),
        compiler_params=pltpu.CompilerParams(vmem_limit_bytes=VMEM_LIMIT),
    )(conds, dmod)


def _f_attn(first_block, q, za, kc, vc, kp, vp, qg, kg, sinks):
    w = WINDOW
    lane = lax.broadcasted_iota(jnp.int32, (1, 128), 1)
    halves = [lane < 64, lane >= 64]

    def rms_halves(x, g):
        x2 = x * x
        s0 = jnp.sum(jnp.where(halves[0], x2, 0.0), axis=-1, keepdims=True)
        s1 = jnp.sum(jnp.where(halves[1], x2, 0.0), axis=-1, keepdims=True)
        r = jnp.where(halves[0], lax.rsqrt(s0 / 64.0 + EPS), lax.rsqrt(s1 / 64.0 + EPS))
        return x * r * g

    kcat = rms_halves(jnp.concatenate([kp, kc], axis=0), kg)
    vcat = jnp.concatenate([vp, vc], axis=0)
    qi = lax.broadcasted_iota(jnp.int32, (w, 2 * w), 0)
    kj = lax.broadcasted_iota(jnp.int32, (w, 2 * w), 1)
    dist = qi + w - kj
    valid = (dist >= 0) & (dist < w) & (jnp.logical_not(first_block) | (kj >= w))
    distf = dist.astype(F32)
    units = [(grp, half) for grp in range(4) for half in range(2)]
    qns = [rms_halves(q[:, 128 * grp:128 * grp + 128], qg) * (ATT_HEAD_DIM ** -0.5) for grp in range(4)]
    vhalf = [jnp.where(halves[half], vcat, 0.0) for half in range(2)]
    scores, sinks_h = [], []
    for grp, half in units:
        head = HEAD_ORDER[2 * grp + half]
        slope = 2.0 ** (-8.0 * (head + 1) / ATT_HEADS)
        sinks_h.append(jnp.sum(jnp.where(lane == head, sinks, 0.0), axis=-1, keepdims=True))
        s = mm_nt(jnp.where(halves[half], qns[grp], 0.0), kcat) - slope * distf
        scores.append(jnp.where(valid, s, NEG_INF))
    probs = []
    for s, sink in zip(scores, sinks_h):
        m = lax.stop_gradient(jnp.maximum(jnp.max(s, axis=-1, keepdims=True), sink))
        p = jnp.exp(s - m)
        probs.append(p / (jnp.sum(p, axis=-1, keepdims=True) + jnp.exp(sink - m)))
    outs = [mm(p, vhalf[half]) for p, (grp, half) in zip(probs, units)]
    return jnp.concatenate([outs[2 * grp] + outs[2 * grp + 1] for grp in range(4)], axis=1) * _silu(za)


def attn_fwd(name, proj, qg, kg, sinks):
    t = proj.shape[0]
    nb = t // WINDOW

    def body(q_ref, za_ref, kc_ref, vc_ref, kp_ref, vp_ref, qg_ref, kg_ref, s_ref, o_ref):
        first = pl.program_id(0) == 0
        o_ref[...] = _f_attn(first, q_ref[...], za_ref[...], kc_ref[...], vc_ref[...], kp_ref[...], vp_ref[...],
                             qg_ref[...], kg_ref[...], s_ref[...])

    cur = lambda cb: (lambda i: (i, cb))
    prev = lambda cb: (lambda i: (jnp.maximum(i - 1, 0), cb))
    return pl.pallas_call(
        body, name=name, grid=(nb,),
        in_specs=[pl.BlockSpec((WINDOW, 512), cur(P_QA // 512)), pl.BlockSpec((WINDOW, 512), cur(P_ZA // 512)),
                  pl.BlockSpec((WINDOW, 128), cur(P_KA // 128)), pl.BlockSpec((WINDOW, 128), cur(P_VA // 128)),
                  pl.BlockSpec((WINDOW, 128), prev(P_KA // 128)), pl.BlockSpec((WINDOW, 128), prev(P_VA // 128)),
                  _const_spec((1, 128)), _const_spec((1, 128)), _const_spec((1, 128))],
        out_specs=pl.BlockSpec((WINDOW, 512), lambda i: (i, 0)),
        out_shape=jax.ShapeDtypeStruct((t, 512), F32),
        compiler_params=_cparams(1),
    )(proj, proj, proj, proj, proj, proj, qg, kg, sinks)


def attn_bwd(name, proj, qg, kg, sinks, dya):
    t = proj.shape[0]
    nb = t // WINDOW

    def body(q_ref, za_ref, kc_ref, vc_ref, kp_ref, vp_ref, qg_ref, kg_ref, s_ref, dy_ref,
             dqz_ref, dkv_ref, dqg_ref, dkg_ref, ds_ref, carry_ref):
        j = pl.program_id(0)
        first = j == nb - 1

        @pl.when(j == 0)
        def _():
            carry_ref[...] = jnp.zeros_like(carry_ref)
            dqg_ref[...] = jnp.zeros_like(dqg_ref)
            dkg_ref[...] = jnp.zeros_like(dkg_ref)
            ds_ref[...] = jnp.zeros_like(ds_ref)

        ins = [r[...] for r in (q_ref, za_ref, kc_ref, vc_ref, kp_ref, vp_ref, qg_ref, kg_ref, s_ref)]
        _, vjp = jax.vjp(functools.partial(_f_attn, first), *ins)
        dq, dza, dkc, dvc, dkp, dvp, dqg, dkg, dsk = vjp(dy_ref[...])
        dqz_ref[:, 0:512] = dq.astype(dqz_ref.dtype)
        dqz_ref[:, 512:1024] = dza.astype(dqz_ref.dtype)
        dkv_ref[:, 0:128] = (dkc + carry_ref[0]).astype(dkv_ref.dtype)
        dkv_ref[:, 128:256] = (dvc + carry_ref[1]).astype(dkv_ref.dtype)
        carry_ref[0] = dkp
        carry_ref[1] = dvp
        dqg_ref[...] += dqg
        dkg_ref[...] += dkg
        ds_ref[...] += dsk

    cur = lambda cb: (lambda j: (nb - 1 - j, cb))
    prev = lambda cb: (lambda j: (jnp.maximum(nb - 2 - j, 0), cb))
    return pl.pallas_call(
        body, name=name, grid=(nb,),
        in_specs=[pl.BlockSpec((WINDOW, 512), cur(P_QA // 512)), pl.BlockSpec((WINDOW, 512), cur(P_ZA // 512)),
                  pl.BlockSpec((WINDOW, 128), cur(P_KA // 128)), pl.BlockSpec((WINDOW, 128), cur(P_VA // 128)),
                  pl.BlockSpec((WINDOW, 128), prev(P_KA // 128)), pl.BlockSpec((WINDOW, 128), prev(P_VA // 128)),
                  _const_spec((1, 128)), _const_spec((1, 128)), _const_spec((1, 128)),
                  pl.BlockSpec((WINDOW, 512), cur(0))],
        out_specs=[pl.BlockSpec((WINDOW, 1024), cur(0)), pl.BlockSpec((WINDOW, 256), cur(0)),
                   _const_spec((1, 128)), _const_spec((1, 128)), _const_spec((1, 128))],
        out_shape=[jax.ShapeDtypeStruct((t, 1024), BF16), jax.ShapeDtypeStruct((t, 256), BF16),
                   jax.ShapeDtypeStruct((1, 128), F32), jax.ShapeDtypeStruct((1, 128), F32),
                   jax.ShapeDtypeStruct((1, 128), F32)],
        scratch_shapes=[pltpu.VMEM((2, WINDOW, 128), F32)],
        compiler_params=_cparams(1),
    )(proj, proj, proj, proj, proj, proj, qg, kg, sinks, dya)


CONV_ROWS = 256


def _conv_taps(src_ref, w_ref, n_taps, base, t):
    for r0 in range(0, t, CONV_ROWS):
        acc = w_ref[0:1, :] * src_ref[pl.ds(r0 + base, CONV_ROWS), :]
        for k in range(1, n_taps):
            acc = acc + w_ref[k:k + 1, :] * src_ref[pl.ds(r0 + base + k, CONV_ROWS), :]
        yield r0, acc


def _conv_wgrad(dy_ref, src_ref, n_taps, base, t, dy_base=0):
    out = []
    for k in range(n_taps):
        acc = jnp.zeros((8, 128), F32)
        for r0 in range(0, t, CONV_ROWS):
            prod = dy_ref[pl.ds(r0 + dy_base, CONV_ROWS), :] * src_ref[pl.ds(r0 + base + k, CONV_ROWS), :]
            acc = acc + jnp.sum(prod.reshape(CONV_ROWS // 8, 8, 128), axis=0)
        out.append(jnp.sum(acc, axis=0, keepdims=True))
    return out


def glu_conv_fwd(name, proj, w32, bias):
    t = proj.shape[0]
    pad = 32

    def body(x_ref, w_ref, b_ref, o_ref, u_ref):
        u_ref[0:pad, :] = jnp.zeros((pad, 128), F32)
        u_ref[pad:pad + t, :] = x_ref[:, 0:128] * _sigmoid(x_ref[:, 128:256])
        for r0, acc in _conv_taps(u_ref, w_ref, CONV_K, pad - (CONV_K - 1), t):
            o_ref[pl.ds(r0, CONV_ROWS), :] = acc + b_ref[...]

    return pl.pallas_call(
        body, name=name, grid=(4,),
        in_specs=[pl.BlockSpec((t, 256), lambda cb: (0, P_GLU // 256 + cb)), pl.BlockSpec((32, 128), lambda cb: (0, cb)),
                  pl.BlockSpec((1, 128), lambda cb: (0, cb))],
        out_specs=pl.BlockSpec((t, 128), lambda cb: (0, cb)),
        out_shape=jax.ShapeDtypeStruct((t, 512), F32),
        scratch_shapes=[pltpu.VMEM((t + pad, 128), F32)],
        compiler_params=_cparams(1),
    )(proj, w32, bias)


def glu_conv_bwd(name, proj, w32, dub):
    t = proj.shape[0]
    pad = 32
    k1 = CONV_K - 1

    def body(x_ref, w_ref, dy_ref, dx_ref, dw_ref, db_ref, u_ref, dyp_ref, wrev_ref):
        val = x_ref[:, 0:128]
        sg = _sigmoid(x_ref[:, 128:256])
        u_ref[0:pad, :] = jnp.zeros((pad, 128), F32)
        u_ref[pad:pad + t, :] = val * sg
        dyp_ref[0:t, :] = dy_ref[...]
        dyp_ref[t:t + pad, :] = jnp.zeros((pad, 128), F32)
        for k in range(CONV_K):
            wrev_ref[k:k + 1, :] = w_ref[k1 - k:k1 - k + 1, :]
        wrev_ref[CONV_K:32, :] = jnp.zeros((32 - CONV_K, 128), F32)
        for r0, du in _conv_taps(dyp_ref, wrev_ref, CONV_K, 0, t):
            v = x_ref[pl.ds(r0, CONV_ROWS), 0:128]
            s = _sigmoid(x_ref[pl.ds(r0, CONV_ROWS), 128:256])
            dx_ref[pl.ds(r0, CONV_ROWS), 0:128] = (du * s).astype(dx_ref.dtype)
            dx_ref[pl.ds(r0, CONV_ROWS), 128:256] = (du * v * s * (1.0 - s)).astype(dx_ref.dtype)
        dws = _conv_wgrad(dyp_ref, u_ref, CONV_K, pad - k1, t)
        for k in range(CONV_K):
            dw_ref[k:k + 1, :] = dws[k]
        dw_ref[CONV_K:32, :] = jnp.zeros((32 - CONV_K, 128), F32)
        db_ref[...] = jnp.sum(dy_ref[...], axis=0, keepdims=True)

    return pl.pallas_call(
        body, name=name, grid=(4,),
        in_specs=[pl.BlockSpec((t, 256), lambda cb: (0, P_GLU // 256 + cb)), pl.BlockSpec((32, 128), lambda cb: (0, cb)),
                  pl.BlockSpec((t, 128), lambda cb: (0, cb))],
        out_specs=[pl.BlockSpec((t, 256), lambda cb: (0, cb)), pl.BlockSpec((32, 128), lambda cb: (0, cb)),
                   pl.BlockSpec((1, 128), lambda cb: (0, cb))],
        out_shape=[jax.ShapeDtypeStruct((t, 1024), BF16), jax.ShapeDtypeStruct((32, 512), F32),
                   jax.ShapeDtypeStruct((1, 512), F32)],
        scratch_shapes=[pltpu.VMEM((t + pad, 128), F32), pltpu.VMEM((t + pad, 128), F32), pltpu.VMEM((32, 128), F32)],
        compiler_params=_cparams(1),
    )(proj, w32, dub)


def sconv_fwd(name, proj, w8):
    t = proj.shape[0]
    pad = 8
    k1 = DN_CONV_K - 1

    def body(x_ref, w_ref, o_ref, xp_ref):
        xp_ref[0:pad, :] = jnp.zeros((pad, 128), F32)
        xp_ref[pad:pad + t, :] = x_ref[...]
        for r0, acc in _conv_taps(xp_ref, w_ref, DN_CONV_K, pad - k1, t):
            o_ref[pl.ds(r0, CONV_ROWS), :] = _silu(acc)

    return pl.pallas_call(
        body, name=name, grid=(12,),
        in_specs=[pl.BlockSpec((t, 128), lambda cb: (0, P_QKV // 128 + cb)), pl.BlockSpec((8, 128), lambda cb: (0, cb))],
        out_specs=pl.BlockSpec((t, 128), lambda cb: (0, cb)),
        out_shape=jax.ShapeDtypeStruct((t, 1536), F32),
        scratch_shapes=[pltpu.VMEM((t + pad, 128), F32)],
        compiler_params=_cparams(1),
    )(proj, w8)


def sconv_bwd(name, proj, w8, dqkv):
    t = proj.shape[0]
    pad = 8
    k1 = DN_CONV_K - 1

    def body(x_ref, w_ref, dy_ref, dx_ref, dw_ref, xp_ref, dpp_ref, wrev_ref):
        xp_ref[0:pad, :] = jnp.zeros((pad, 128), F32)
        xp_ref[pad:pad + t, :] = x_ref[...]
        for r0, pre in _conv_taps(xp_ref, w_ref, DN_CONV_K, pad - k1, t):
            s = _sigmoid(pre)
            dpp_ref[pl.ds(r0, CONV_ROWS), :] = dy_ref[pl.ds(r0, CONV_ROWS), :] * (s * (1.0 + pre * (1.0 - s)))
        dpp_ref[t:t + pad, :] = jnp.zeros((pad, 128), F32)
        for k in range(DN_CONV_K):
            wrev_ref[k:k + 1, :] = w_ref[k1 - k:k1 - k + 1, :]
        wrev_ref[DN_CONV_K:8, :] = jnp.zeros((8 - DN_CONV_K, 128), F32)
        for r0, dx in _conv_taps(dpp_ref, wrev_ref, DN_CONV_K, 0, t):
            dx_ref[pl.ds(r0, CONV_ROWS), :] = dx.astype(dx_ref.dtype)
        dws = _conv_wgrad(dpp_ref, xp_ref, DN_CONV_K, pad - k1, t)
        for k in range(DN_CONV_K):
            dw_ref[k:k + 1, :] = dws[k]
        dw_ref[DN_CONV_K:8, :] = jnp.zeros((8 - DN_CONV_K, 128), F32)

    return pl.pallas_call(
        body, name=name, grid=(12,),
        in_specs=[pl.BlockSpec((t, 128), lambda cb: (0, P_QKV // 128 + cb)), pl.BlockSpec((8, 128), lambda cb: (0, cb)),
                  pl.BlockSpec((t, 128), lambda cb: (0, cb))],
        out_specs=[pl.BlockSpec((t, 128), lambda cb: (0, cb)), pl.BlockSpec((8, 128), lambda cb: (0, cb))],
        out_shape=[jax.ShapeDtypeStruct((t, 1536), BF16), jax.ShapeDtypeStruct((8, 1536), F32)],
        scratch_shapes=[pltpu.VMEM((t + pad, 128), F32), pltpu.VMEM((t + pad, 128), F32), pltpu.VMEM((8, 128), F32)],
        compiler_params=_cparams(1),
    )(proj, w8, dqkv)


def _f_delta_step(qkv, ab, zc, s0, s1, s2, s3, a_log, dt_bias, dn_g, inverses=None, with_inverses=False):
    cs = DN_CHUNK
    n = 2 * cs
    states = (s0, s1, s2, s3)
    lane = lax.broadcasted_iota(jnp.int32, (1, 128), 1)
    ri = lax.broadcasted_iota(jnp.int32, (n, n), 0)
    ci = lax.broadcasted_iota(jnp.int32, (n, n), 1)
    same = (ri // cs) == (ci // cs)
    lower = same & (ri >= ci)
    strict = same & (ri > ci)
    sums = jnp.concatenate([jnp.where(lower, 1.0, 0.0), jnp.where(same, 1.0, 0.0), jnp.where(ci < cs, 1.0, 0.0),
                            jnp.where(ci >= cs, 1.0, 0.0)], axis=0)
    top = lax.broadcasted_iota(jnp.int32, (n, 1), 0) < cs

    def pick(row, idx):
        return jnp.sum(jnp.where(lane == idx, row, 0.0), axis=-1, keepdims=True)

    def l2n(x):
        return x * lax.rsqrt(jnp.sum(x * x, axis=-1, keepdims=True) + EPS)

    n_chunks = qkv.shape[0] // cs
    units = [(k, pair) for k in range(n_chunks) for pair in range(2)]

    pre = []
    for k, pair in units:
        hs = (2 * pair, 2 * pair + 1)
        rows = slice(k * cs, (k + 1) * cs)
        stack = lambda f: jnp.concatenate([f(hs[0]), f(hs[1])], axis=0)
        qd = l2n(stack(lambda h: qkv[rows, 128 * h:128 * h + 128])) * (128 ** -0.5)
        kd = l2n(stack(lambda h: qkv[rows, 512 + 128 * h:512 + 128 * h + 128]))
        vd = stack(lambda h: qkv[rows, 1024 + 128 * h:1024 + 128 * h + 128])
        beta = _sigmoid(stack(lambda h: pick(ab[rows], 4 + h)))
        g = stack(lambda h: -jnp.exp(pick(a_log, h)) * _softplus(pick(ab[rows], h) + pick(dt_bias, h)))
        g_sums = sel_mm(sums, g * jnp.ones((1, n), F32))
        gc_col = g_sums[0:n]
        gl_b = g_sums[n:2 * n]
        g_end = (g_sums[2 * n:3 * n], g_sums[3 * n:])
        decay = jnp.where(lower, jnp.exp(jnp.where(lower, gc_col - gc_col.T, 0.0)), 0.0)
        kb = kd * beta
        pre.append(dict(qd=qd, kd=kd, vb=vd * beta, kb=kb, gc_col=gc_col, gl_b=gl_b, g_end=g_end, decay=decay,
                        a=jnp.where(strict, mm_nt(kb, kd) * decay, 0.0)))
    if inverses is None:
        tmats = tri_inv(*[p["a"] for p in pre])
    else:
        tmats = [tri_inv_known(p["a"], t) for p, t in zip(pre, inverses)]

    mid = []
    for p, tmat in zip(pre, tmats):
        egc = jnp.exp(p["gc_col"])
        mid.append(dict(u=mm(tmat, p["vb"]), wm=mm(tmat, p["kb"] * egc), qe=p["qd"] * egc,
                        intra=jnp.where(lower, mm_nt(p["qd"], p["kd"]) * p["decay"], 0.0),
                        ke=p["kd"] * jnp.exp(p["gl_b"] - p["gc_col"]), g_end=p["g_end"]))

    ys = []
    for k in range(n_chunks):
        rows = slice(k * cs, (k + 1) * cs)
        new_states, y_heads = [], []
        for pair in range(2):
            m = mid[2 * k + pair]
            hs = (2 * pair, 2 * pair + 1)
            st = (states[hs[0]], states[hs[1]])
            v_new = m["u"] - jnp.concatenate([mm(m["wm"][:cs], st[0]), mm(m["wm"][cs:], st[1])], axis=0)
            o = jnp.concatenate([mm(m["qe"][:cs], st[0]), mm(m["qe"][cs:], st[1])], axis=0) + mm(m["intra"], v_new)
            new_states.append(st[0] * jnp.exp(m["g_end"][0]) + mm_tn(jnp.where(top, m["ke"], 0.0), v_new))
            new_states.append(st[1] * jnp.exp(m["g_end"][1]) + mm_tn(jnp.where(top, 0.0, m["ke"]), v_new))
            od = o * lax.rsqrt(jnp.mean(o * o, axis=-1, keepdims=True) + EPS) * dn_g
            y_heads += [od[:cs] * _silu(zc[rows, 128 * hs[0]:128 * hs[0] + 128]),
                        od[cs:] * _silu(zc[rows, 128 * hs[1]:128 * hs[1] + 128])]
        states = tuple(new_states)
        ys.append(jnp.concatenate(y_heads, axis=1))
    if with_inverses:
        return (jnp.concatenate(ys, axis=0), *states), tmats
    return (jnp.concatenate(ys, axis=0), *states)


DELTA_ROWS = 4 * DN_CHUNK
DELTA_UNITS = 2 * DELTA_ROWS // DN_CHUNK


def delta_fwd(name, qkv, proj, a_log, dt_bias, dn_g):
    t = qkv.shape[0]
    nc = t // DELTA_ROWS

    def body(qkv_ref, ab_ref, zc_ref, al_ref, dt_ref, g_ref, y_ref, ssave_ref, tsave_ref, s_ref):
        @pl.when(pl.program_id(0) == 0)
        def _():
            s_ref[...] = jnp.zeros_like(s_ref)

        ssave_ref[0] = s_ref[...]
        st = [s_ref[128 * h:128 * h + 128, :] for h in range(4)]
        (y, *ns), tmats = _f_delta_step(qkv_ref[...], ab_ref[...], zc_ref[...], *st, al_ref[...], dt_ref[...], g_ref[...],
                                        with_inverses=True)
        y_ref[...] = y
        for h in range(4):
            s_ref[128 * h:128 * h + 128, :] = ns[h]
        for u, tm in enumerate(tmats):
            tsave_ref[0, 128 * u:128 * u + 128, :] = tm

    return pl.pallas_call(
        body, name=name, grid=(nc,),
        in_specs=[pl.BlockSpec((DELTA_ROWS, 1536), lambda i: (i, 0)), pl.BlockSpec((DELTA_ROWS, 128), lambda i: (i, P_AB // 128)),
                  pl.BlockSpec((DELTA_ROWS, 512), lambda i: (i, P_ZC // 512)),
                  _const_spec((1, 128)), _const_spec((1, 128)), _const_spec((1, 128))],
        out_specs=[pl.BlockSpec((DELTA_ROWS, 512), lambda i: (i, 0)), pl.BlockSpec((1, 512, 128), lambda i: (i, 0, 0)),
                   pl.BlockSpec((1, DELTA_UNITS * 128, 128), lambda i: (i, 0, 0))],
        out_shape=[jax.ShapeDtypeStruct((t, 512), F32), jax.ShapeDtypeStruct((nc, 512, 128), F32),
                   jax.ShapeDtypeStruct((nc, DELTA_UNITS * 128, 128), F32)],
        scratch_shapes=[pltpu.VMEM((512, 128), F32)],
        compiler_params=_cparams(1),
    )(qkv, proj, proj, a_log, dt_bias, dn_g)


def delta_bwd(name, qkv, proj, ssave, tsave, a_log, dt_bias, dn_g, dyc, carry=None):
    t = qkv.shape[0]
    nc = t // DELTA_ROWS
    c_ins, c_in_specs, c_outs, c_out_specs, c_scratch = _host(carry)
    n_ci, n_co = len(c_ins), len(c_outs)

    def body(*refs):
        qkv_ref, ab_ref, zc_ref, ss_ref, ts_ref, al_ref, dt_ref, g_ref, dy_ref = refs[:9]
        dqkv_ref, dab_ref, dzc_ref, dal_ref, ddt_ref, dg_ref = refs[9 + n_ci:15 + n_ci]
        ds_ref = refs[15 + n_ci + n_co]
        carried = (refs[9:9 + n_ci], refs[15 + n_ci:15 + n_ci + n_co], *refs[16 + n_ci + n_co:])

        @pl.when(pl.program_id(0) == 0)
        def _():
            ds_ref[...] = jnp.zeros_like(ds_ref)
            dal_ref[...] = jnp.zeros_like(dal_ref)
            ddt_ref[...] = jnp.zeros_like(ddt_ref)
            dg_ref[...] = jnp.zeros_like(dg_ref)

        if carry is not None:
            carry.emit_start(pl.program_id(0) == 0, *carried)

        st = [ss_ref[0, 128 * h:128 * h + 128, :] for h in range(4)]
        known = [ts_ref[0, 128 * u:128 * u + 128, :] for u in range(DELTA_UNITS)]
        _, vjp = jax.vjp(functools.partial(_f_delta_step, inverses=known), qkv_ref[...], ab_ref[...], zc_ref[...], *st,
                         al_ref[...], dt_ref[...], g_ref[...])
        dst = tuple(ds_ref[128 * h:128 * h + 128, :] for h in range(4))
        dqkv, dab, dzc, d0, d1, d2, d3, dal, ddt, dg = vjp((dy_ref[...], *dst))
        dqkv_ref[...] = dqkv
        dab_ref[...] = dab.astype(dab_ref.dtype)
        dzc_ref[...] = dzc.astype(dzc_ref.dtype)
        for h, d in enumerate((d0, d1, d2, d3)):
            ds_ref[128 * h:128 * h + 128, :] = d
        dal_ref[...] += dal
        ddt_ref[...] += ddt
        dg_ref[...] += dg

        if carry is not None:
            carry.emit_finish(pl.program_id(0) == nc - 1, *carried)

    rev = lambda cb: (lambda j: (nc - 1 - j, cb))
    return pl.pallas_call(
        body, name=name, grid=(nc,),
        in_specs=[pl.BlockSpec((DELTA_ROWS, 1536), rev(0)), pl.BlockSpec((DELTA_ROWS, 128), rev(P_AB // 128)),
                  pl.BlockSpec((DELTA_ROWS, 512), rev(P_ZC // 512)), pl.BlockSpec((1, 512, 128), lambda j: (nc - 1 - j, 0, 0)),
                  pl.BlockSpec((1, DELTA_UNITS * 128, 128), lambda j: (nc - 1 - j, 0, 0)),
                  _const_spec((1, 128)), _const_spec((1, 128)), _const_spec((1, 128)),
                  pl.BlockSpec((DELTA_ROWS, 512), rev(0))] + c_in_specs,
        out_specs=[pl.BlockSpec((DELTA_ROWS, 1536), rev(0)), pl.BlockSpec((DELTA_ROWS, 128), rev(0)),
                   pl.BlockSpec((DELTA_ROWS, 512), rev(0)),
                   _const_spec((1, 128)), _const_spec((1, 128)), _const_spec((1, 128))] + c_out_specs,
        out_shape=[jax.ShapeDtypeStruct((t, 1536), F32), jax.ShapeDtypeStruct((t, 128), BF16),
                   jax.ShapeDtypeStruct((t, 512), BF16),
                   jax.ShapeDtypeStruct((1, 128), F32), jax.ShapeDtypeStruct((1, 128), F32), jax.ShapeDtypeStruct((1, 128), F32)]
        + c_outs,
        scratch_shapes=[pltpu.VMEM((512, 128), F32)] + c_scratch,
        compiler_params=_cparams(1),
    )(qkv, proj, proj, ssave, tsave, a_log, dt_bias, dn_g, dyc, *c_ins)


def loss_head(name, y, target, tm):
    t, d = y.shape

    def body(y_ref, t_ref, dy_ref, l_ref):
        err = y_ref[...] - t_ref[...]
        dy_ref[...] = err * (1.0 / d)
        part = 0.5 * jnp.sum(jnp.sum(err * err, axis=-1, keepdims=True) * (1.0 / d), axis=0, keepdims=True)

        @pl.when(pl.program_id(0) == 0)
        def _():
            l_ref[...] = part

        @pl.when(pl.program_id(0) > 0)
        def _():
            l_ref[...] += part

    return pl.pallas_call(
        body, name=name, grid=(t // tm,),
        in_specs=[_row_spec(tm, d, 0), _row_spec(tm, d, 0)],
        out_specs=[_row_spec(tm, d, 0), _const_spec((1, 1))],
        out_shape=[jax.ShapeDtypeStruct((t, d), F32), jax.ShapeDtypeStruct((1, 1), F32)],
        compiler_params=_cparams(1),
    )(y, target)


TM = 1024
TM_MERGE = 256
TM_IN = 1024
TN_IN = 1152


def _lane_pad(v, n=128):
    return jnp.pad(v.astype(F32), (0, n - v.shape[0]))[None, :]


def f_norm_mod_res(x, g, scale, shift):
    return f_norm_mod(x, g, scale, shift), x


def prep_layer(w):
    p = dict(w)
    p["wp"] = _w_in_assemble(w["w_in"])
    p["wpa"] = _perm_heads_rows(w["w_proj_a"])
    p["dw32"] = jnp.pad(w["dw_w"], ((0, 32 - CONV_K), (0, 0)))
    p["sconv8"] = jnp.pad(w["sconv_w"], ((0, 8 - DN_CONV_K), (0, 0)))
    p["qg"] = jnp.tile(w["q_norm_g"], 2)[None, :]
    p["kg"] = jnp.tile(w["k_norm_g"], 2)[None, :]
    p["sinks128"] = _lane_pad(w["sinks"])
    p["al"] = _lane_pad(w["a_log"])
    p["dtb"] = _lane_pad(w["dt_bias"])
    p["dng"] = w["dn_norm_g"][None, :]
    return p


def layer_fwd(tag, x, mod, p, carry_inproj=None, carry_merge=None):
    d = D_MODEL
    shift, scale, gate = mod[:, :d], mod[:, d:2 * d], mod[:, 2 * d:]
    g = p["norm_g"][None, :]
    h, h_t = rowwise_fwd(f"norm_fwd{tag}", lambda *a: (f_norm_mod(*a),) * 2, [(x, d, 0)], [g, scale, shift],
                         [(d, BF16), (d, BF16, "transposed")], TM)
    proj = matmul_nn(f"inproj_fwd{tag}", h, p["wp"], F32, TM_IN, P_TOTAL // 3, d, carry=carry_inproj)
    proj, got_inproj = (proj, []) if carry_inproj is None else (proj[0], proj[1:])
    ya = attn_fwd(f"attn_fwd{tag}", proj, p["qg"], p["kg"], p["sinks128"])
    ub = glu_conv_fwd(f"glu_conv_fwd{tag}", proj, p["dw32"], p["dw_b"][None, :])
    conf_consts = [p["ln_g"][None, :], p["ln_b"][None, :], p["pw2_w"], p["pw2_b"][None, :]]
    (yb,) = rowwise_fwd(f"conf_fwd{tag}", f_conf_tail, [(ub, 512, 0), (proj, 512, P_ZB // 512)], conf_consts, [(512, F32)], TM)
    qkv = sconv_fwd(f"sconv_fwd{tag}", proj, p["sconv8"])
    yc, ssave, tsave = delta_fwd(f"delta_fwd{tag}", qkv, proj, p["al"], p["dtb"], p["dng"])
    merge_consts = [gate, p["wpa"], p["w_proj_b"], p["w_proj_c"], p["w_out"]]
    merge_rows = [(ya, 512, 0), (yb, 512, 0), (yc, 512, 0), (proj, 3 * d, P_MG // (3 * d)), (x, d, 0)]
    xn, *got_merge = rowwise_fwd(f"merge_fwd{tag}", f_merge, merge_rows, merge_consts, [(d, F32)], 2 * TM_MERGE,
                                 carry=carry_merge)
    saved = dict(x=x, h_t=h_t, proj=proj, ub=ub, qkv=qkv, ssave=ssave, tsave=tsave, norm_consts=[g, scale, shift],
                 conf_consts=conf_consts, merge_consts=merge_consts, merge_rows=merge_rows)
    return xn, saved, got_inproj, got_merge


def layer_bwd(tag, dxn, p, s, carry_merge=None, carry_delta=None, carry_dh=None):
    d = D_MODEL
    proj = s["proj"]
    merge_no_residual = lambda ya, yb, yc, mg, *consts: f_merge(ya, yb, yc, mg, 0.0, *consts)
    dya, dyb, dyc, dmg, dgate, dwpa, dwpb, dwpc, dwout, *got_merge = rowwise_bwd(
        f"merge_bwd{tag}", merge_no_residual, s["merge_rows"][:4], s["merge_consts"], [(dxn, d, 0)], [F32, F32, F32, BF16],
        TM_MERGE, carry=carry_merge)
    carry_delta = None if carry_delta is None else carry_delta(got_merge)
    dqz, dkv, dqg, dkg, dsinks = attn_bwd(f"attn_bwd{tag}", proj, p["qg"], p["kg"], p["sinks128"], dya)
    dub, dzb, dln_g, dln_b, dpw2_w, dpw2_b = rowwise_bwd(
        f"conf_bwd{tag}", f_conf_tail, [(s["ub"], 512, 0), (proj, 512, P_ZB // 512)], s["conf_consts"], [(dyb, 512, 0)],
        [F32, BF16], TM)
    dglu, ddw32, ddw_b = glu_conv_bwd(f"glu_conv_bwd{tag}", proj, p["dw32"], dub)
    dqkv, dab, dzc, dal, ddtb, ddng, *got_delta = delta_bwd(f"delta_bwd{tag}", s["qkv"], proj, s["ssave"], s["tsave"], p["al"],
                                                            p["dtb"], p["dng"], dyc, carry_delta)
    dqkv_pre, dsconv8 = sconv_bwd(f"sconv_bwd{tag}", proj, p["sconv8"], dqkv)
    dproj = jnp.concatenate([dqz, dglu, dzb, dzc, dmg, dqkv_pre, dkv, dab], axis=1)
    dwp = matmul_nn(f"inproj_bwd_dw{tag}", s["h_t"], dproj, F32, d, TN_IN, dproj.shape[0])
    reduced = dict(w_in=_w_in_grad_blocks(dwp), pw2_w=dpw2_w, w_proj_a=_unperm_heads_rows(dwpa), w_proj_b=dwpb, w_proj_c=dwpc,
                   w_out=dwout)
    carry_dh = None if carry_dh is None else carry_dh(reduced)
    dh = matmul_nn(f"inproj_bwd_dh{tag}", dproj, p["wp"], F32, TM_IN, d, P_TOTAL // 3, b_transposed=True, carry=carry_dh)
    dh, got_dh = (dh, []) if carry_dh is None else (dh[0], dh[1:])
    dx, dnorm_g, dscale, dshift = rowwise_bwd(
        f"norm_bwd{tag}", f_norm_mod_res, [(s["x"], d, 0)], s["norm_consts"], [(dh, d, 0), (dxn, d, 0)], [F32], TM)
    dmod = jnp.concatenate([dshift, dscale, dgate], axis=1)
    grads = dict(
        reduced, b_ada=dmod[0], norm_g=dnorm_g[0],
        q_norm_g=dqg[0, :64] + dqg[0, 64:], k_norm_g=dkg[0, :64] + dkg[0, 64:], sinks=dsinks[0, :ATT_HEADS],
        dw_w=ddw32[:CONV_K], dw_b=ddw_b[0], ln_g=dln_g[0], ln_b=dln_b[0], pw2_b=dpw2_b[0],
        sconv_w=dsconv8[:DN_CONV_K], a_log=dal[0, :DN_HEADS], dt_bias=ddtb[0, :DN_HEADS], dn_norm_g=ddng[0])
    return dx, grads, got_merge, got_delta, got_dh


SHARDED = {"w_ada": 2, "w_in": 2, "dw_w": 2, "pw2_w": 1, "sconv_w": 2, "w_proj_a": 2, "w_proj_b": 2, "w_proj_c": 2,
           "w_out": 1}
GATHERED = tuple(n for n in SHARDED if n != "w_ada")
GATHER_F32 = ("dw_w", "sconv_w")
REDUCE_BIG = tuple(n for n in GATHERED if n not in GATHER_F32)
SMALL = ("b_ada", "norm_g", "q_norm_g", "k_norm_g", "sinks", "dw_b", "ln_g", "ln_b", "pw2_b", "a_log", "dt_bias",
         "dn_norm_g")
SMALL_ROWS = 104
SMALL_GRAD_ROWS = 448
W_IN_SHARD = D_IN // N_CHIPS
SUM_PARTS = 4


def _w_in_orig():
    orig = np.full(P_TOTAL, -1, np.int64)
    p = 0
    for s, n in _in_pieces():
        orig[p:p + n] = np.arange(s, s + n)
        p += n
    return orig


def _w_in_blocks(k):
    orig = _w_in_orig().reshape(-1, 128)
    lo, hi = k * W_IN_SHARD, (k + 1) * W_IN_SHARD
    return [b for b in range(orig.shape[0]) if np.any((orig[b] >= lo) & (orig[b] < hi))]


W_IN_BLOCKS = max(len(_w_in_blocks(k)) for k in range(N_CHIPS))


def _runs(idx):
    out, i = [], 0
    while i < len(idx):
        j = i + 1
        while j < len(idx) and ((idx[i] < 0 and idx[j] < 0) or (idx[i] >= 0 and idx[j] == idx[j - 1] + 1)):
            j += 1
        out.append((int(idx[i]) if idx[i] >= 0 else -1, j - i))
        i = j
    return out


def _take(a, idx):
    parts = [jnp.zeros(a.shape[:-1] + (n,), a.dtype) if s < 0 else a[..., s:s + n] for s, n in _runs(idx)]
    return parts[0] if len(parts) == 1 else jnp.concatenate(parts, axis=-1)


def _w_in_send(k, shard):
    orig = _w_in_orig().reshape(-1, 128)
    lo, hi = k * W_IN_SHARD, (k + 1) * W_IN_SHARD
    idx = np.concatenate([np.where((orig[b] >= lo) & (orig[b] < hi), orig[b] - lo, -1) for b in _w_in_blocks(k)])
    idx = np.concatenate([idx, np.full((W_IN_BLOCKS - len(_w_in_blocks(k))) * 128, -1)])
    return _take(shard, idx)


def _w_in_assemble(blocks):
    where = [{b: i for i, b in enumerate(_w_in_blocks(k))} for k in range(N_CHIPS)]
    n_blocks = P_TOTAL // 128
    owners = [[(k, where[k][b]) for k in range(N_CHIPS) if b in where[k]] for b in range(n_blocks)]
    parts, b = [], 0
    while b < n_blocks:
        if len(owners[b]) == 1:
            k, pos = owners[b][0]
            e = b + 1
            while e < n_blocks and owners[e] == [(k, pos + e - b)]:
                e += 1
            parts.append(blocks[k][:, pos * 128:(pos + e - b) * 128])
            b = e
        else:
            parts.append(functools.reduce(jnp.add, [blocks[k][:, pos * 128:(pos + 1) * 128] for k, pos in owners[b]]))
            b += 1
    return jnp.concatenate(parts, axis=1)


def _w_in_grad_blocks(wp):
    out = []
    for k in range(N_CHIPS):
        idx = np.concatenate([np.arange(128 * b, 128 * b + 128) for b in _w_in_blocks(k)])
        idx = np.concatenate([idx, np.full((W_IN_BLOCKS - len(_w_in_blocks(k))) * 128, -1)])
        out.append(_take(wp, idx))
    return jnp.stack(out)


def _w_in_receive_grad(k, blocks):
    orig = _w_in_orig()
    inv = np.zeros(D_IN, np.int64)
    inv[orig[orig >= 0]] = np.nonzero(orig >= 0)[0]
    where = {b: i for i, b in enumerate(_w_in_blocks(k))}
    cols = inv[k * W_IN_SHARD:(k + 1) * W_IN_SHARD]
    return _take(blocks, np.array([where[c // 128] * 128 + c % 128 for c in cols]))


def _join_layer(v, axis):
    if axis == 2:
        return jnp.transpose(v, (1, 0, 2)).reshape(v.shape[1], N_CHIPS * v.shape[2])
    return v.reshape(N_CHIPS * v.shape[1], v.shape[2])


def _split_layer(v, axis):
    a, b = v.shape
    if axis == 2:
        return jnp.transpose(v.reshape(a, N_CHIPS, b // N_CHIPS), (1, 0, 2))
    return v.reshape(N_CHIPS, a // N_CHIPS, b)


def pack_small(vals, names, rows):
    flat = jnp.concatenate([vals[n].astype(F32).reshape(-1) for n in names])
    return jnp.pad(flat, (0, rows * 128 - flat.shape[0])).reshape(rows, 128)


def unpack_small(packed, names, shapes):
    flat = packed.reshape(-1)
    out, off = {}, 0
    for n in names:
        k = int(np.prod(shapes[n]))
        out[n] = flat[off:off + k].reshape(shapes[n])
        off += k
    return out


ANY = pl.BlockSpec(memory_space=pl.ANY)


def _place():
    x, y, c = lax.axis_index("x"), lax.axis_index("y"), lax.axis_index("c")
    chips = [(1 - x, y), (x, 1 - y), (1 - x, 1 - y)]
    return x, y, c, chips


def _remote(src, dst, send_sem, recv_sem, to):
    return pltpu.make_async_remote_copy(src_ref=src, dst_ref=dst, send_sem=send_sem, recv_sem=recv_sem, device_id=to,
                                        device_id_type=MESH)


class Carry:
    def __init__(self, ins, out_shapes, sems, start, finish, in_place=False):
        self.ins, self.out_shapes, self.sems, self.start, self.finish, self.in_place = (
            list(ins), list(out_shapes), sems, start, finish, in_place)

    def scratch(self):
        return [pltpu.SemaphoreType.DMA(self.sems), pltpu.SemaphoreType.DMA(self.sems)]

    def aliases(self, first_in, first_out):
        return {first_in + i: first_out + i for i in range(len(self.ins))} if self.in_place else {}

    def emit_start(self, first, in_refs, out_refs, send_sems, recv_sems):
        @pl.when(first)
        def _():
            self.start(in_refs, out_refs, send_sems, recv_sems)

    def emit_finish(self, last, in_refs, out_refs, send_sems, recv_sems):
        @pl.when(last)
        def _():
            self.finish(in_refs, out_refs, send_sems, recv_sems)


def _host(carry):
    if carry is None:
        return [], [], [], [], []
    return carry.ins, [ANY] * len(carry.ins), carry.out_shapes, [ANY] * len(carry.out_shapes), carry.scratch()


def run_carry(name, carry):
    n_in, n_out = len(carry.ins), len(carry.out_shapes)

    def body(*refs):
        ins, outs, sems = refs[:n_in], refs[n_in:n_in + n_out], refs[n_in + n_out:]
        carry.start(ins, outs, *sems)
        carry.finish(ins, outs, *sems)

    return pl.pallas_call(
        body, name=name, out_shape=carry.out_shapes, in_specs=[ANY] * n_in, out_specs=[ANY] * n_out,
        input_output_aliases=carry.aliases(0, 0), scratch_shapes=carry.scratch(),
    )(*carry.ins)


def carry_allgather(layer, slots):
    n = len(slots)

    def copies(out, send_sems, recv_sems, only_ici_out=False):
        x, y, c, chips = _place()
        ici_out, ici_in, d2d_out, d2d_in = [], [], [], []
        for j, chip in enumerate(chips):
            for t in range(n):
                mine, land = out[t].at[2 * x + y], out[t].at[2 * chip[0] + chip[1]]
                ici_out.append(_remote(mine, mine, send_sems.at[t, j], recv_sems.at[t, j], (*chip, layer)))
                if only_ici_out:
                    continue
                ici_in.append(_remote(land, land, send_sems.at[t, j], recv_sems.at[t, j], (*chip, layer)))
                d2d_out.append(_remote(land, land, send_sems.at[t, 3 + j], recv_sems.at[t, 3 + j], (x, y, 1 - layer)))
                d2d_in.append(_remote(land, land, send_sems.at[t, 3 + j], recv_sems.at[t, 3 + j], (x, y, layer)))
        return c, ici_out, ici_in, d2d_out, d2d_in

    def start(ins, out, send_sems, recv_sems):
        c, ici_out, _, _, _ = copies(out, send_sems, recv_sems, only_ici_out=True)

        @pl.when(c == layer)
        def _():
            for cp in ici_out:
                cp.start()

    def finish(ins, out, send_sems, recv_sems):
        c, ici_out, ici_in, d2d_out, d2d_in = copies(out, send_sems, recv_sems)

        @pl.when(c == layer)
        def _():
            for arrived, onward in zip(ici_in, d2d_out):
                arrived.wait_recv()
                onward.start()
            for cp in ici_out + d2d_out:
                cp.wait_send()

        @pl.when(c != layer)
        def _():
            for cp in d2d_in:
                cp.wait_recv()

    return Carry(slots, [jax.ShapeDtypeStruct(s.shape, s.dtype) for s in slots], (n, 6), start, finish, in_place=True)


def carry_pair_send(layer, gs):
    def copies(g, recv, send_sems, recv_sems):
        x, y, c, _ = _place()
        return c, [_remote(g[t], recv[t], send_sems.at[t], recv_sems.at[t], (x, y, 1 - c)) for t in range(len(gs))]

    def start(g, recv, send_sems, recv_sems):
        c, cps = copies(g, recv, send_sems, recv_sems)

        @pl.when(c != layer)
        def _():
            for cp in cps:
                cp.start()

    def finish(g, recv, send_sems, recv_sems):
        c, cps = copies(g, recv, send_sems, recv_sems)

        @pl.when(c != layer)
        def _():
            for cp in cps:
                cp.wait_send()

        @pl.when(c == layer)
        def _():
            for cp in cps:
                cp.wait_recv()

    return Carry(gs, [jax.ShapeDtypeStruct(g.shape, g.dtype) for g in gs], (len(gs),), start, finish)


def grads_pair_sums(layer, gs, recv):
    n = len(gs)

    def body(*refs):
        for t in range(n):
            refs[2 * n + t][...] = (refs[t][...] + refs[n + t][...]).astype(BF16)

    specs = [pl.BlockSpec((None, g.shape[1] // SUM_PARTS, g.shape[2]), lambda s, i: (s, i, 0)) for g in gs]
    return pl.pallas_call(
        body, name=f"grads_pair_sums{layer}", grid=(N_CHIPS, SUM_PARTS), in_specs=specs + specs, out_specs=specs,
        out_shape=[jax.ShapeDtypeStruct(g.shape, BF16) for g in gs], compiler_params=_cparams(2),
    )(*gs, *recv)


def carry_chip_exchange(layer, ps):
    def copies(p, recv, send_sems, recv_sems):
        _, _, c, chips = _place()
        return c, [_remote(p[t].at[2 * chip[0] + chip[1]], recv[t].at[j], send_sems.at[t, j], recv_sems.at[t, j],
                           (*chip, layer)) for j, chip in enumerate(chips) for t in range(len(ps))]

    def start(p, recv, send_sems, recv_sems):
        c, cps = copies(p, recv, send_sems, recv_sems)

        @pl.when(c == layer)
        def _():
            for cp in cps:
                cp.start()

    def finish(p, recv, send_sems, recv_sems):
        c, cps = copies(p, recv, send_sems, recv_sems)

        @pl.when(c == layer)
        def _():
            for cp in cps:
                cp.wait()

    return Carry(ps, [jax.ShapeDtypeStruct((3,) + p.shape[1:], p.dtype) for p in ps], (len(ps), 3), start, finish)


def grads_chip_sums(layer, gs, recv, recv2, into=None):
    n = len(gs)
    my_slot = lambda: 2 * lax.axis_index("x") + lax.axis_index("y")

    def body(*refs):
        outs = refs[-n:]
        for t in range(n):
            r2 = refs[2 * n + t]
            own = refs[t][...] + refs[n + t][...]
            outs[t][...] = ((own + r2[0].astype(F32)) + r2[1].astype(F32)) + r2[2].astype(F32)

    part = lambda g: g.shape[1] // SUM_PARTS
    own_specs = [pl.BlockSpec((None, part(g), g.shape[2]), lambda i: (my_slot(), i, 0)) for g in gs]
    return pl.pallas_call(
        body, name=f"grads_chip_sums{layer}", grid=(SUM_PARTS,),
        in_specs=own_specs + own_specs + [pl.BlockSpec((3, part(g), g.shape[2]), lambda i: (0, i, 0)) for g in gs]
        + ([] if into is None else [ANY] * n),
        out_specs=[pl.BlockSpec((None, part(g), g.shape[2]), lambda i: (layer, i, 0)) for g in gs],
        out_shape=[jax.ShapeDtypeStruct((DEPTH,) + g.shape[1:], F32) for g in gs],
        input_output_aliases={} if into is None else {3 * n + t: t for t in range(n)},
        compiler_params=_cparams(1),
    )(*gs, *recv, *recv2, *([] if into is None else into))


def grads_pair_gather(reds):
    n = len(reds)

    def body(*refs):
        buf = refs[n:2 * n]
        send_sems, recv_sems = refs[2 * n:]
        x, y, c, _ = _place()
        sibling = (x, y, 1 - c)
        cps = [_remote(buf[t].at[c], buf[t].at[c], send_sems.at[t], recv_sems.at[t], sibling) for t in range(n)]
        for cp in cps:
            cp.start()
        for t in range(n):
            _remote(buf[t].at[c], buf[t].at[1 - c], send_sems.at[t], recv_sems.at[t], sibling).wait_recv()
        for cp in cps:
            cp.wait_send()

    return pl.pallas_call(
        body, name="grads_pair_gather", out_shape=[jax.ShapeDtypeStruct(r.shape, r.dtype) for r in reds],
        in_specs=[ANY] * n, out_specs=[ANY] * n, input_output_aliases={t: t for t in range(n)},
        scratch_shapes=[pltpu.SemaphoreType.DMA((n,)), pltpu.SemaphoreType.DMA((n,))],
    )(*reds)


def small_allreduce(v):
    m, n = v.shape

    def body(x_ref, sum_ref, all_ref, send_sems, recv_sems, local_sem):
        x, y, c, chips = _place()
        me, sibling = (x, y, c), (x, y, 1 - c)

        def rows(px, py, pc):
            return all_ref.at[pl.ds((4 * px + 2 * py + pc) * m, m), :]

        def copy(k, block, to, src=None):
            return pltpu.make_async_remote_copy(src_ref=rows(*block) if src is None else src, dst_ref=rows(*block),
                                                send_sem=send_sems.at[k], recv_sem=recv_sems.at[k],
                                                device_id=to, device_id_type=MESH)

        mine = pltpu.make_async_copy(x_ref, rows(*me), local_sem)
        mine.start()
        first = [copy(0, me, sibling, src=x_ref)]
        first += [copy(1 + j, me, (*chip, c), src=x_ref) for j, chip in enumerate(chips)]
        for cp in first:
            cp.start()
        passed = [copy(4 + j, (*chip, c), sibling) for j, chip in enumerate(chips)]
        for j, chip in enumerate(chips):
            copy(1 + j, (*chip, c), me).wait_recv()
            passed[j].start()
        copy(0, sibling, me).wait_recv()
        for j, chip in enumerate(chips):
            copy(4 + j, (*chip, 1 - c), me).wait_recv()
        for cp in first + passed:
            cp.wait_send()
        mine.wait()
        acc = all_ref[0:m, :]
        for dev in range(1, 8):
            acc = acc + all_ref[dev * m:(dev + 1) * m, :]
        sum_ref[...] = acc

    vm = pl.BlockSpec(memory_space=pltpu.VMEM)
    return pl.pallas_call(
        body, name="small_allreduce",
        out_shape=[jax.ShapeDtypeStruct((m, n), F32), jax.ShapeDtypeStruct((8 * m, n), F32)],
        in_specs=[vm], out_specs=[vm, vm],
        scratch_shapes=[pltpu.SemaphoreType.DMA((7,)), pltpu.SemaphoreType.DMA((7,)), pltpu.SemaphoreType.DMA],
    )(v)


def grads_by_chip(layer_grads):
    return [layer_grads[n] if n == "w_in" else _split_layer(layer_grads[n], SHARDED[n]) for n in REDUCE_BIG]


def _adamw_block(w_ref, g_ref, m_ref, v_ref, d_ref, nm_ref, nv_ref):
    gv = g_ref[...]
    nm = ADAM_B1 * m_ref[...] + (1.0 - ADAM_B1) * gv
    nv = ADAM_B2 * v_ref[...] + (1.0 - ADAM_B2) * (gv * gv)
    m_hat = nm / (1.0 - ADAM_B1 ** ADAM_STEP)
    v_hat = nv / (1.0 - ADAM_B2 ** ADAM_STEP)
    d_ref[...] = -ADAM_LR * (m_hat / (jnp.sqrt(v_hat) + ADAM_EPS) + ADAM_WD * w_ref[...])
    nm_ref[...] = nm
    nv_ref[...] = nv


def adamw(name, w, g, m, v, block):
    grid = tuple(s // b for s, b in zip(w.shape, block))

    def body(*refs):
        _adamw_block(*refs)

    spec = pl.BlockSpec(tuple(block), lambda *idx: idx)
    return pl.pallas_call(
        body, name=name, grid=grid, in_specs=[spec] * 4, out_specs=[spec] * 3,
        out_shape=[jax.ShapeDtypeStruct(w.shape, F32)] * 3, compiler_params=_cparams(len(grid)),
    )(w, g, m, v)


def adamw_many(name, groups):
    n = len(groups)

    def spec(a):
        rows, cols = a.shape
        if rows % (8 * ADAM_PARTS) == 0:
            return pl.BlockSpec((rows // ADAM_PARTS, cols), lambda i: (i, 0))
        return pl.BlockSpec((rows, cols), lambda i: (0, 0))

    def body(*refs):
        for t in range(n):
            _adamw_block(*refs[4 * t:4 * t + 4], *refs[4 * n + 3 * t:4 * n + 3 * t + 3])

    res = pl.pallas_call(
        body, name=name, grid=(ADAM_PARTS,),
        in_specs=[spec(grp[0]) for grp in groups for _ in range(4)],
        out_specs=[spec(grp[0]) for grp in groups for _ in range(3)],
        out_shape=[jax.ShapeDtypeStruct(grp[0].shape, F32) for grp in groups for _ in range(3)],
        compiler_params=_cparams(1),
    )(*[a for grp in groups for a in grp])
    return [tuple(res[3 * t:3 * t + 3]) for t in range(n)]


ADAM_PARTS = 4
ADAM_W_IN_COLS = 331

WEIGHT_NAMES = ("w_ada", "b_ada", "norm_g", "w_in", "q_norm_g", "k_norm_g", "sinks", "dw_w", "dw_b", "ln_g", "ln_b",
                "pw2_w", "pw2_b", "sconv_w", "a_log", "dt_bias", "dn_norm_g", "w_proj_a", "w_proj_b", "w_proj_c", "w_out")


def kernel(x, c, w_ada, b_ada, norm_g, w_in, q_norm_g, k_norm_g, sinks, dw_w, dw_b, ln_g, ln_b, pw2_w, pw2_b, sconv_w, a_log, dt_bias, dn_norm_g, w_proj_a, w_proj_b, w_proj_c, w_out, loss_target, m_w_ada, m_b_ada, m_norm_g, m_w_in, m_q_norm_g, m_k_norm_g, m_sinks, m_dw_w, m_dw_b, m_ln_g, m_ln_b, m_pw2_w, m_pw2_b, m_sconv_w, m_a_log, m_dt_bias, m_dn_norm_g, m_w_proj_a, m_w_proj_b, m_w_proj_c, m_w_out, v_w_ada, v_b_ada, v_norm_g, v_w_in, v_q_norm_g, v_k_norm_g, v_sinks, v_dw_w, v_dw_b, v_ln_g, v_ln_b, v_pw2_w, v_pw2_b, v_sconv_w, v_a_log, v_dt_bias, v_dn_norm_g, v_w_proj_a, v_w_proj_b, v_w_proj_c, v_w_out):
    args = dict(locals())
    w = {n: args[n] for n in WEIGHT_NAMES}
    mom = {n: args["m_" + n] for n in WEIGHT_NAMES}
    var = {n: args["v_" + n] for n in WEIGHT_NAMES}

    chip = 2 * lax.axis_index("x") + lax.axis_index("y")
    own = {n: w[n] if n in GATHER_F32 else w[n].astype(BF16) for n in GATHERED}
    own["w_in"] = lax.switch(chip, [functools.partial(_w_in_send, k) for k in range(N_CHIPS)], own["w_in"])
    slots = [[lax.dynamic_update_slice(lax.empty((N_CHIPS,) + own[n].shape[1:], own[n].dtype), own[n][l][None], (chip, 0, 0))
              for n in GATHERED] for l in range(DEPTH)]

    def layer_operands(l, gathered):
        lw = {n: w[n][l] for n in SMALL}
        lw.update({n: g if n == "w_in" else _join_layer(g, SHARDED[n]) for n, g in zip(GATHERED, gathered)})
        return prep_layer(lw)

    layers = [layer_operands(0, run_carry("weights_allgather0", carry_allgather(0, slots[0]))), None]

    mod, conds = ada_fwd(jnp.tile(c, (8, 1)), w["w_ada"], w["b_ada"])
    saved = [None] * DEPTH
    big = GATHERED.index("w_in")
    rest = [i for i in range(len(GATHERED)) if i != big]
    act, saved[0], got_big, got_rest = layer_fwd(
        "0", x[0], mod[0:1], layers[0], carry_inproj=carry_allgather(1, [slots[1][big]]),
        carry_merge=carry_allgather(1, [slots[1][i] for i in rest]))
    gathered1 = dict(zip(rest, got_rest))
    gathered1[big] = got_big[0]
    layers[1] = layer_operands(1, [gathered1[i] for i in range(len(GATHERED))])
    act, saved[1], _, _ = layer_fwd("1", act, mod[1:2], layers[1])
    dact, loss_part = loss_head("loss_head", act, loss_target[0], TM)
    loss = lax.psum(loss_part[0, 0], ("x", "y", "c"))
    layer_grads = [None] * DEPTH
    dact, layer_grads[1], _, _, _ = layer_bwd("1", dact, layers[1], saved[1])
    gs1 = grads_by_chip(layer_grads[1])
    gs0 = []

    def hand_over_layer0(reduced):
        gs0.extend(grads_by_chip(reduced))
        return carry_pair_send(0, gs0)

    dact, layer_grads[0], recv1, got1, recv0 = layer_bwd(
        "0", dact, layers[0], saved[0], carry_merge=carry_pair_send(1, gs1),
        carry_delta=lambda recv: carry_chip_exchange(1, grads_pair_sums(1, gs1, recv)), carry_dh=hand_over_layer0)

    got0 = run_carry("grads_chip_exchange0", carry_chip_exchange(0, grads_pair_sums(0, gs0, recv0)))
    reds = grads_chip_sums(0, gs0, recv0, got0, into=grads_chip_sums(1, gs1, recv1, got1))
    final_grads = dict(zip(REDUCE_BIG, grads_pair_gather(reds)))
    final_grads["w_in"] = lax.switch(chip, [functools.partial(_w_in_receive_grad, k) for k in range(N_CHIPS)],
                                     final_grads["w_in"])
    small_names = SMALL + GATHER_F32
    small_shapes = {n: (DEPTH,) + layer_grads[0][n].shape for n in small_names}
    small_full = {n: jnp.stack([layer_grads[l][n] for l in range(DEPTH)]) for n in small_names}
    small_sum, small_all = small_allreduce(pack_small(small_full, small_names, SMALL_GRAD_ROWS))
    small_sum = unpack_small(small_sum, small_names, small_shapes)
    for n in GATHER_F32:
        width = w[n].shape[2]
        final_grads[n] = lax.dynamic_slice_in_dim(small_sum[n], chip * width, width, axis=2)
    n_mod = DEPTH * 3 * D_MODEL
    dmod = small_all.reshape(8, -1)[:, :n_mod].reshape(8, DEPTH, 3 * D_MODEL)
    width = w["w_ada"].shape[2]
    dmod = jnp.transpose(lax.dynamic_slice_in_dim(dmod, chip * width, width, axis=2), (1, 0, 2))
    final_grads["w_ada"] = ada_bwd(conds, dmod)
    final_grads.update({n: small_sum[n] for n in SMALL})
    small_grads = pack_small(final_grads, SMALL, SMALL_ROWS)

    delta, new_m, new_v = {}, {}, {}
    shp = w["w_in"].shape
    view = lambda a: jnp.transpose(a, (2, 0, 1))
    back = lambda a: jnp.transpose(a, (1, 2, 0))
    g3 = view(final_grads["w_in"])
    final_grads["w_in"] = back(g3)
    d, nm, nv = adamw("adamw_w_in", view(w["w_in"]), g3, view(mom["w_in"]), view(var["w_in"]),
                      (ADAM_W_IN_COLS, shp[0], shp[1]))
    delta["w_in"], new_m["w_in"], new_v["w_in"] = back(d), back(nm), back(nv)
    others = [n for n in SHARDED if n != "w_in"]
    two_d = lambda a: a.reshape(a.shape[0] * a.shape[1], a.shape[2])
    groups = [tuple(two_d(t[n]) for t in (w, final_grads, mom, var)) for n in others]
    groups.append((pack_small(w, SMALL, SMALL_ROWS), small_grads, pack_small(mom, SMALL, SMALL_ROWS),
                   pack_small(var, SMALL, SMALL_ROWS)))
    results = adamw_many("adamw_rest", groups)
    for n, (d, nm, nv) in zip(others, results):
        delta[n], new_m[n], new_v[n] = (a.reshape(w[n].shape) for a in (d, nm, nv))
    for out, packed in zip((delta, new_m, new_v), results[-1]):
        out.update(unpack_small(packed, SMALL, small_shapes))

    return (loss, dact[None], *[final_grads[n] for n in WEIGHT_NAMES], *[delta[n] for n in WEIGHT_NAMES],
            *[new_m[n] for n in WEIGHT_NAMES], *[new_v[n] for n in WEIGHT_NAMES])
```
